```python
import math
import jax, jax.numpy as jnp
from jax import lax
import numpy as np

D_MODEL = 1024
BATCH = 8
SEQ = 8192
DEPTH = 2

CHUNK = 64
QBLOCK = 128
EPS = 1e-6
NEG_INF = -1e30

LRU_WIDTH = 512
LRU_HEADS = 8
LRU_HEAD_DIM = LRU_WIDTH // LRU_HEADS
CONV_WIDTH = 4
LRU_C = 8.0

MLA_HEADS = 8
MLA_Q_LORA = 384
MLA_KV_LORA = 256
MLA_NOPE = 64
MLA_ROPE = 32
MLA_V = 64
ROPE_BASE = 10000.0

FOX_HEADS = 8
FOX_HEAD_DIM = 64
FOX_WIDTH = FOX_HEADS * FOX_HEAD_DIM

N_BRANCH = 3
D_FF = ((8 * D_MODEL // 3 + 255) // 256) * 256
PLE_DIM = 256

SPLIT_SIZES = (
    LRU_WIDTH,
    LRU_WIDTH,
    MLA_Q_LORA,
    MLA_KV_LORA + MLA_ROPE,
    FOX_WIDTH,
    FOX_WIDTH,
    FOX_WIDTH,
    FOX_HEADS,
    N_BRANCH * D_MODEL,
)
D_IN = 2 * LRU_WIDTH + MLA_Q_LORA + MLA_KV_LORA + MLA_ROPE + 3 * FOX_WIDTH + FOX_HEADS + N_BRANCH * D_MODEL

kernel_name = "hybrid_gated_rglru_mla_fox_encoder"


def rmsnorm(x, g):
    xf = x.astype(jnp.float32)
    y = xf * lax.rsqrt(jnp.mean(xf * xf, axis=-1, keepdims=True) + EPS)
    return (y * g.astype(jnp.float32)).astype(x.dtype)


def split_columns(z):
    idx = []
    acc = 0
    for s in SPLIT_SIZES[:-1]:
        acc += s
        idx.append(acc)
    return jnp.split(z, idx, axis=-1)


def rope(x, cos, sin):
    half = x.shape[-1] // 2
    x1, x2 = x[..., :half], x[..., half:]
    c = cos[None, :, None, :].astype(x.dtype)
    s = sin[None, :, None, :].astype(x.dtype)
    return jnp.concatenate([x1 * c - x2 * s, x2 * c + x1 * s], axis=-1)


def block_attention(q, k, v, scale, unit, decay=None):
    B, S, H, Dk = q.shape
    nb = S // QBLOCK
    q_blocks = q.reshape(B, nb, QBLOCK, H, Dk).transpose(1, 0, 2, 3, 4)
    key_unit = jnp.arange(S) // unit
    decay_t = None if decay is None else decay.transpose(0, 2, 1)

    def one_block(args):
        ib, q_blk = args
        s = jnp.einsum('bqhd,bkhd->bhqk', q_blk, k, preferred_element_type=jnp.float32) * scale
        if decay_t is not None:
            dq = lax.dynamic_slice_in_dim(decay_t, ib * QBLOCK, QBLOCK, axis=2)
            s = s + dq[:, :, :, None] - decay_t[:, :, None, :]
        q_unit = (ib * QBLOCK + jnp.arange(QBLOCK)) // unit
        mask = q_unit[:, None] >= key_unit[None, :]
        s = jnp.where(mask[None, None], s, NEG_INF)
        pr = jax.nn.softmax(s, axis=-1)
        return jnp.einsum('bhqk,bkhd->bqhd', pr.astype(v.dtype), v)

    out = lax.map(one_block, (jnp.arange(nb), q_blocks))
    return out.transpose(1, 0, 2, 3, 4).reshape(B, S, H, v.shape[-1])


def _lru_combine(left, right):
    a1, b1 = left
    a2, b2 = right
    return a1 * a2, a2 * b1 + b2


def rglru_branch(u, u_gate, conv_w, conv_b, wa, ba, wx, bx, lam):
    B, S, W = u.shape
    up = jnp.pad(u, ((0, 0), (CONV_WIDTH - 1, 0), (0, 0)))
    xc = conv_b + up[:, 0:S] * conv_w[0]
    for kk in range(1, CONV_WIDTH):
        xc = xc + up[:, kk:kk + S] * conv_w[kk]
    xh = xc.reshape(B, S, LRU_HEADS, LRU_HEAD_DIM)
    r = jax.nn.sigmoid(jnp.einsum('bshi,hij->bshj', xh, wa).reshape(B, S, W) + ba)
    ig = jax.nn.sigmoid(jnp.einsum('bshi,hij->bshj', xh, wx).reshape(B, S, W) + bx)
    log_a = -LRU_C * r.astype(jnp.float32) * jax.nn.softplus(-lam.astype(jnp.float32))
    a = jnp.exp(log_a)
    b = jnp.sqrt(-jnp.expm1(2.0 * log_a)) * (ig * xc).astype(jnp.float32)
    _, h = lax.associative_scan(_lru_combine, (a, b), axis=1)
    return h.astype(u.dtype) * jax.nn.gelu(u_gate)


def mla_branch(c_q, ckv_rope, q_norm, wuq, kv_norm, wukv, cos, sin):
    B, S, _ = c_q.shape
    q = (rmsnorm(c_q, q_norm) @ wuq).reshape(B, S, MLA_HEADS, MLA_NOPE + MLA_ROPE)
    q_nope, q_rope = q[..., :MLA_NOPE], q[..., MLA_NOPE:]
    c_kv, k_rope = ckv_rope[..., :MLA_KV_LORA], ckv_rope[..., MLA_KV_LORA:]
    kv = (rmsnorm(c_kv, kv_norm) @ wukv).reshape(B, S, MLA_HEADS, MLA_NOPE + MLA_V)
    k_nope, v = kv[..., :MLA_NOPE], kv[..., MLA_NOPE:]
    q_rope = rope(q_rope, cos, sin)
    k_rope = rope(k_rope[:, :, None, :], cos, sin)
    q_full = jnp.concatenate([q_nope, q_rope], axis=-1)
    k_full = jnp.concatenate([k_nope, jnp.broadcast_to(k_rope, (B, S, MLA_HEADS, MLA_ROPE))], axis=-1)
    o = block_attention(q_full, k_full, v, (MLA_NOPE + MLA_ROPE) ** -0.5, CHUNK)
    return o.reshape(B, S, MLA_HEADS * MLA_V)


def fox_branch(fq, fk, fv, f_logit, bf):
    B, S, _ = fq.shape
    q = fq.reshape(B, S, FOX_HEADS, FOX_HEAD_DIM)
    k = fk.reshape(B, S, FOX_HEADS, FOX_HEAD_DIM)
    v = fv.reshape(B, S, FOX_HEADS, FOX_HEAD_DIM)
    log_f = jax.nn.log_sigmoid((f_logit + bf).astype(jnp.float32))
    cum = jnp.cumsum(log_f, axis=1)
    o = block_attention(q, k, v, FOX_HEAD_DIM ** -0.5, 1, decay=cum)
    return o.reshape(B, S, FOX_WIDTH)


def _fwd_setup_inputs(seed: int = 0) -> dict:
    key = jax.random.key(seed)
    ks = jax.random.split(key, 32)

    def nrm(k, shape, scale):
        return jax.random.normal(k, shape, jnp.float32) * scale

    def gain(k, shape):
        return 1.0 + 0.05 * jax.random.normal(k, shape, jnp.float32)

    u = jax.random.uniform(ks[10], (DEPTH, LRU_WIDTH), jnp.float32, 0.9, 0.999)
    a = u ** (1.0 / LRU_C)
    lru_lambda = jnp.log(a) - jnp.log1p(-a)

    return {
        "x": nrm(ks[0], (BATCH, SEQ, D_MODEL), 1.0),
        "p": nrm(ks[1], (DEPTH, BATCH, SEQ, PLE_DIM), 1.0),
        "mix_norm": gain(ks[2], (DEPTH, D_MODEL)),
        "w_in": nrm(ks[3], (DEPTH, D_MODEL, D_IN), D_MODEL ** -0.5),
        "gate_b": nrm(ks[4], (DEPTH, N_BRANCH * D_MODEL), 0.1),
        "conv_w": nrm(ks[5], (DEPTH, CONV_WIDTH, LRU_WIDTH), CONV_WIDTH ** -0.5),
        "conv_b": nrm(ks[6], (DEPTH, LRU_WIDTH), 0.1),
        "lru_wa": nrm(ks[7], (DEPTH, LRU_HEADS, LRU_HEAD_DIM, LRU_HEAD_DIM), LRU_HEAD_DIM ** -0.5),
        "lru_ba": nrm(ks[8], (DEPTH, LRU_WIDTH), 0.1),
        "lru_wx": nrm(ks[9], (DEPTH, LRU_HEADS, LRU_HEAD_DIM, LRU_HEAD_DIM), LRU_HEAD_DIM ** -0.5),
        "lru_bx": nrm(ks[11], (DEPTH, LRU_WIDTH), 0.1),
        "lru_lambda": lru_lambda,
        "mla_q_norm": gain(ks[12], (DEPTH, MLA_Q_LORA)),
        "mla_wuq": nrm(ks[13], (DEPTH, MLA_Q_LORA, MLA_HEADS * (MLA_NOPE + MLA_ROPE)), MLA_Q_LORA ** -0.5),
        "mla_kv_norm": gain(ks[14], (DEPTH, MLA_KV_LORA)),
        "mla_wukv": nrm(ks[15], (DEPTH, MLA_KV_LORA, MLA_HEADS * (MLA_NOPE + MLA_V)), MLA_KV_LORA ** -0.5),
        "fox_bf": jax.random.uniform(ks[16], (DEPTH, FOX_HEADS), jnp.float32, 1.0, 5.0),
        "w_br_a": nrm(ks[17], (DEPTH, LRU_WIDTH, D_MODEL), LRU_WIDTH ** -0.5),
        "w_br_b": nrm(ks[18], (DEPTH, MLA_HEADS * MLA_V, D_MODEL), (MLA_HEADS * MLA_V) ** -0.5),
        "w_br_c": nrm(ks[19], (DEPTH, FOX_WIDTH, D_MODEL), FOX_WIDTH ** -0.5),
        "w_o": nrm(ks[20], (DEPTH, D_MODEL, D_MODEL), D_MODEL ** -0.5),
        "ffn_norm": gain(ks[21], (DEPTH, D_MODEL)),
        "w_gate_up": nrm(ks[22], (DEPTH, D_MODEL, 2 * D_FF), D_MODEL ** -0.5),
        "w_down": nrm(ks[23], (DEPTH, D_FF, D_MODEL), D_FF ** -0.5),
        "ple_norm": gain(ks[24], (DEPTH, D_MODEL)),
        "w_ple_gate": nrm(ks[25], (DEPTH, D_MODEL, D_MODEL), D_MODEL ** -0.5),
        "w_ple": nrm(ks[26], (DEPTH, PLE_DIM, D_MODEL), PLE_DIM ** -0.5),
        "final_norm": gain(ks[27], (D_MODEL,)),
    }


def _fwd_reference(x, p, mix_norm, w_in, gate_b, conv_w, conv_b, lru_wa, lru_ba, lru_wx, lru_bx, lru_lambda,
              mla_q_norm, mla_wuq, mla_kv_norm, mla_wukv, fox_bf, w_br_a, w_br_b, w_br_c, w_o,
              ffn_norm, w_gate_up, w_down, ple_norm, w_ple_gate, w_ple, final_norm):
    B, S, D = x.shape
    pos = jnp.arange(S, dtype=jnp.float32)
    inv_freq = ROPE_BASE ** (-jnp.arange(0, MLA_ROPE, 2, dtype=jnp.float32) / MLA_ROPE)
    ang = pos[:, None] * inv_freq[None, :]
    cos, sin = jnp.cos(ang), jnp.sin(ang)

    for i in range(DEPTH):
        h = rmsnorm(x, mix_norm[i])
        z = h @ w_in[i]
        u_rnn, u_gelu, c_q, ckv_rope, fq, fk, fv, f_logit, gate_logit = split_columns(z)
        y_a = rglru_branch(u_rnn, u_gelu, conv_w[i], conv_b[i], lru_wa[i], lru_ba[i],
                           lru_wx[i], lru_bx[i], lru_lambda[i]) @ w_br_a[i]
        y_b = mla_branch(c_q, ckv_rope, mla_q_norm[i], mla_wuq[i], mla_kv_norm[i],
                         mla_wukv[i], cos, sin) @ w_br_b[i]
        y_c = fox_branch(fq, fk, fv, f_logit, fox_bf[i]) @ w_br_c[i]
        g = jax.nn.sigmoid(gate_logit + gate_b[i]).reshape(B, S, N_BRANCH, D)
        merged = g[:, :, 0] * y_a + g[:, :, 1] * y_b + g[:, :, 2] * y_c
        x = x + merged @ w_o[i]
        hf = rmsnorm(x, ffn_norm[i]) @ w_gate_up[i]
        x = x + (jax.nn.silu(hf[..., :D_FF]) * hf[..., D_FF:]) @ w_down[i]
        pg = jax.nn.sigmoid(rmsnorm(x, ple_norm[i]) @ w_ple_gate[i])
        x = x + pg * (p[i] @ w_ple[i])
    return rmsnorm(x, final_norm)


import jax as _jax
import jax.numpy as _jnp

TWIN_FORMAT = 'train_step'
FWD_PARAMS = ['x', 'p', 'mix_norm', 'w_in', 'gate_b', 'conv_w', 'conv_b', 'lru_wa', 'lru_ba', 'lru_wx', 'lru_bx', 'lru_lambda', 'mla_q_norm', 'mla_wuq', 'mla_kv_norm', 'mla_wukv', 'fox_bf', 'w_br_a', 'w_br_b', 'w_br_c', 'w_o', 'ffn_norm', 'w_gate_up', 'w_down', 'ple_norm', 'w_ple_gate', 'w_ple', 'final_norm']
TWIN_WEIGHTS = ['mix_norm', 'w_in', 'gate_b', 'conv_w', 'conv_b', 'lru_wa', 'lru_ba', 'lru_wx', 'lru_bx', 'lru_lambda', 'mla_q_norm', 'mla_wuq', 'mla_kv_norm', 'mla_wukv', 'fox_bf', 'w_br_a', 'w_br_b', 'w_br_c', 'w_o', 'ffn_norm', 'w_gate_up', 'w_down', 'ple_norm', 'w_ple_gate', 'w_ple', 'final_norm']
TWIN_DIFF_INPUT = 'x'
TWIN_INPUTS = ['x', 'p', 'mix_norm', 'w_in', 'gate_b', 'conv_w', 'conv_b', 'lru_wa', 'lru_ba', 'lru_wx', 'lru_bx', 'lru_lambda', 'mla_q_norm', 'mla_wuq', 'mla_kv_norm', 'mla_wukv', 'fox_bf', 'w_br_a', 'w_br_b', 'w_br_c', 'w_o', 'ffn_norm', 'w_gate_up', 'w_down', 'ple_norm', 'w_ple_gate', 'w_ple', 'final_norm', 'loss_target', 'm_mix_norm', 'm_w_in', 'm_gate_b', 'm_conv_w', 'm_conv_b', 'm_lru_wa', 'm_lru_ba', 'm_lru_wx', 'm_lru_bx', 'm_lru_lambda', 'm_mla_q_norm', 'm_mla_wuq', 'm_mla_kv_norm', 'm_mla_wukv', 'm_fox_bf', 'm_w_br_a', 'm_w_br_b', 'm_w_br_c', 'm_w_o', 'm_ffn_norm', 'm_w_gate_up', 'm_w_down', 'm_ple_norm', 'm_w_ple_gate', 'm_w_ple', 'm_final_norm', 'v_mix_norm', 'v_w_in', 'v_gate_b', 'v_conv_w', 'v_conv_b', 'v_lru_wa', 'v_lru_ba', 'v_lru_wx', 'v_lru_bx', 'v_lru_lambda', 'v_mla_q_norm', 'v_mla_wuq', 'v_mla_kv_norm', 'v_mla_wukv', 'v_fox_bf', 'v_w_br_a', 'v_w_br_b', 'v_w_br_c', 'v_w_o', 'v_ffn_norm', 'v_w_gate_up', 'v_w_down', 'v_ple_norm', 'v_w_ple_gate', 'v_w_ple', 'v_final_norm']
TWIN_OUTPUTS = ['loss', 'grad_x', 'grad_mix_norm', 'grad_w_in', 'grad_gate_b', 'grad_conv_w', 'grad_conv_b', 'grad_lru_wa', 'grad_lru_ba', 'grad_lru_wx', 'grad_lru_bx', 'grad_lru_lambda', 'grad_mla_q_norm', 'grad_mla_wuq', 'grad_mla_kv_norm', 'grad_mla_wukv', 'grad_fox_bf', 'grad_w_br_a', 'grad_w_br_b', 'grad_w_br_c', 'grad_w_o', 'grad_ffn_norm', 'grad_w_gate_up', 'grad_w_down', 'grad_ple_norm', 'grad_w_ple_gate', 'grad_w_ple', 'grad_final_norm', 'delta_mix_norm', 'delta_w_in', 'delta_gate_b', 'delta_conv_w', 'delta_conv_b', 'delta_lru_wa', 'delta_lru_ba', 'delta_lru_wx', 'delta_lru_bx', 'delta_lru_lambda', 'delta_mla_q_norm', 'delta_mla_wuq', 'delta_mla_kv_norm', 'delta_mla_wukv', 'delta_fox_bf', 'delta_w_br_a', 'delta_w_br_b', 'delta_w_br_c', 'delta_w_o', 'delta_ffn_norm', 'delta_w_gate_up', 'delta_w_down', 'delta_ple_norm', 'delta_w_ple_gate', 'delta_w_ple', 'delta_final_norm', 'new_m_mix_norm', 'new_m_w_in', 'new_m_gate_b', 'new_m_conv_w', 'new_m_conv_b', 'new_m_lru_wa', 'new_m_lru_ba', 'new_m_lru_wx', 'new_m_lru_bx', 'new_m_lru_lambda', 'new_m_mla_q_norm', 'new_m_mla_wuq', 'new_m_mla_kv_norm', 'new_m_mla_wukv', 'new_m_fox_bf', 'new_m_w_br_a', 'new_m_w_br_b', 'new_m_w_br_c', 'new_m_w_o', 'new_m_ffn_norm', 'new_m_w_gate_up', 'new_m_w_down', 'new_m_ple_norm', 'new_m_w_ple_gate', 'new_m_w_ple', 'new_m_final_norm', 'new_v_mix_norm', 'new_v_w_in', 'new_v_gate_b', 'new_v_conv_w', 'new_v_conv_b', 'new_v_lru_wa', 'new_v_lru_ba', 'new_v_lru_wx', 'new_v_lru_bx', 'new_v_lru_lambda', 'new_v_mla_q_norm', 'new_v_mla_wuq', 'new_v_mla_kv_norm', 'new_v_mla_wukv', 'new_v_fox_bf', 'new_v_w_br_a', 'new_v_w_br_b', 'new_v_w_br_c', 'new_v_w_o', 'new_v_ffn_norm', 'new_v_w_gate_up', 'new_v_w_down', 'new_v_ple_norm', 'new_v_w_ple_gate', 'new_v_w_ple', 'new_v_final_norm']
TWIN_LEAF_KINDS = {'loss': 'loss', 'grad_x': 'grad_x', 'grad_mix_norm': 'grad_w', 'grad_w_in': 'grad_w', 'grad_gate_b': 'grad_w', 'grad_conv_w': 'grad_w', 'grad_conv_b': 'grad_w', 'grad_lru_wa': 'grad_w', 'grad_lru_ba': 'grad_w', 'grad_lru_wx': 'grad_w', 'grad_lru_bx': 'grad_w', 'grad_lru_lambda': 'grad_w', 'grad_mla_q_norm': 'grad_w', 'grad_mla_wuq': 'grad_w', 'grad_mla_kv_norm': 'grad_w', 'grad_mla_wukv': 'grad_w', 'grad_fox_bf': 'grad_w', 'grad_w_br_a': 'grad_w', 'grad_w_br_b': 'grad_w', 'grad_w_br_c': 'grad_w', 'grad_w_o': 'grad_w', 'grad_ffn_norm': 'grad_w', 'grad_w_gate_up': 'grad_w', 'grad_w_down': 'grad_w', 'grad_ple_norm': 'grad_w', 'grad_w_ple_gate': 'grad_w', 'grad_w_ple': 'grad_w', 'grad_final_norm': 'grad_w', 'delta_mix_norm': 'delta_w', 'delta_w_in': 'delta_w', 'delta_gate_b': 'delta_w', 'delta_conv_w': 'delta_w', 'delta_conv_b': 'delta_w', 'delta_lru_wa': 'delta_w', 'delta_lru_ba': 'delta_w', 'delta_lru_wx': 'delta_w', 'delta_lru_bx': 'delta_w', 'delta_lru_lambda': 'delta_w', 'delta_mla_q_norm': 'delta_w', 'delta_mla_wuq': 'delta_w', 'delta_mla_kv_norm': 'delta_w', 'delta_mla_wukv': 'delta_w', 'delta_fox_bf': 'delta_w', 'delta_w_br_a': 'delta_w', 'delta_w_br_b': 'delta_w', 'delta_w_br_c': 'delta_w', 'delta_w_o': 'delta_w', 'delta_ffn_norm': 'delta_w', 'delta_w_gate_up': 'delta_w', 'delta_w_down': 'delta_w', 'delta_ple_norm': 'delta_w', 'delta_w_ple_gate': 'delta_w', 'delta_w_ple': 'delta_w', 'delta_final_norm': 'delta_w', 'new_m_mix_norm': 'new_m', 'new_m_w_in': 'new_m', 'new_m_gate_b': 'new_m', 'new_m_conv_w': 'new_m', 'new_m_conv_b': 'new_m', 'new_m_lru_wa': 'new_m', 'new_m_lru_ba': 'new_m', 'new_m_lru_wx': 'new_m', 'new_m_lru_bx': 'new_m', 'new_m_lru_lambda': 'new_m', 'new_m_mla_q_norm': 'new_m', 'new_m_mla_wuq': 'new_m', 'new_m_mla_kv_norm': 'new_m', 'new_m_mla_wukv': 'new_m', 'new_m_fox_bf': 'new_m', 'new_m_w_br_a': 'new_m', 'new_m_w_br_b': 'new_m', 'new_m_w_br_c': 'new_m', 'new_m_w_o': 'new_m', 'new_m_ffn_norm': 'new_m', 'new_m_w_gate_up': 'new_m', 'new_m_w_down': 'new_m', 'new_m_ple_norm': 'new_m', 'new_m_w_ple_gate': 'new_m', 'new_m_w_ple': 'new_m', 'new_m_final_norm': 'new_m', 'new_v_mix_norm': 'new_v', 'new_v_w_in': 'new_v', 'new_v_gate_b': 'new_v', 'new_v_conv_w': 'new_v', 'new_v_conv_b': 'new_v', 'new_v_lru_wa': 'new_v', 'new_v_lru_ba': 'new_v', 'new_v_lru_wx': 'new_v', 'new_v_lru_bx': 'new_v', 'new_v_lru_lambda': 'new_v', 'new_v_mla_q_norm': 'new_v', 'new_v_mla_wuq': 'new_v', 'new_v_mla_kv_norm': 'new_v', 'new_v_mla_wukv': 'new_v', 'new_v_fox_bf': 'new_v', 'new_v_w_br_a': 'new_v', 'new_v_w_br_b': 'new_v', 'new_v_w_br_c': 'new_v', 'new_v_w_o': 'new_v', 'new_v_ffn_norm': 'new_v', 'new_v_w_gate_up': 'new_v', 'new_v_w_down': 'new_v', 'new_v_ple_norm': 'new_v', 'new_v_w_ple_gate': 'new_v', 'new_v_w_ple': 'new_v', 'new_v_final_norm': 'new_v'}


def _forward(args):
    return _fwd_reference(*[args[k] for k in FWD_PARAMS])


def _output_shape():
    def fwd():
        inp = _fwd_setup_inputs(0)
        return _fwd_reference(*[inp[k] for k in FWD_PARAMS])
    out = _jax.eval_shape(fwd)
    return out.shape, out.dtype

N_MICROBATCH = 1
ADAM_LR = 0.001
ADAM_B1 = 0.9
ADAM_B2 = 0.999
ADAM_EPS = 1e-08
ADAM_WD = 0.01
ADAM_STEP = 10
PER_EXAMPLE_BATCH_AXIS = {'x': 0, 'p': 1, 'loss_target': 0}
SHARED_INPUTS = []
_WEIGHT_DTYPES = {'mix_norm': _jnp.float32, 'w_in': _jnp.float32, 'gate_b': _jnp.float32, 'conv_w': _jnp.float32, 'conv_b': _jnp.float32, 'lru_wa': _jnp.float32, 'lru_ba': _jnp.float32, 'lru_wx': _jnp.float32, 'lru_bx': _jnp.float32, 'lru_lambda': _jnp.float32, 'mla_q_norm': _jnp.float32, 'mla_wuq': _jnp.float32, 'mla_kv_norm': _jnp.float32, 'mla_wukv': _jnp.float32, 'fox_bf': _jnp.float32, 'w_br_a': _jnp.float32, 'w_br_b': _jnp.float32, 'w_br_c': _jnp.float32, 'w_o': _jnp.float32, 'ffn_norm': _jnp.float32, 'w_gate_up': _jnp.float32, 'w_down': _jnp.float32, 'ple_norm': _jnp.float32, 'w_ple_gate': _jnp.float32, 'w_ple': _jnp.float32, 'final_norm': _jnp.float32}
MOMENT_SCALE = {'mix_norm': 1.291078e-01, 'w_in': 5.093073e-02, 'gate_b': 2.571739e-02, 'conv_w': 9.668646e-02, 'conv_b': 1.099176e+00, 'lru_wa': 3.452392e-02, 'lru_ba': 2.791904e-02, 'lru_wx': 6.240022e-02, 'lru_bx': 6.175005e-02, 'lru_lambda': 6.127938e-02, 'mla_q_norm': 3.058780e-02, 'mla_wuq': 2.169012e-02, 'mla_kv_norm': 5.405519e-02, 'mla_wukv': 2.656635e-02, 'fox_bf': 1.822922e-01, 'w_br_a': 1.059234e-01, 'w_br_b': 2.135840e-02, 'w_br_c': 4.622765e-02, 'w_o': 1.040973e-01, 'ffn_norm': 1.569784e-01, 'w_gate_up': 6.705235e-02, 'w_down': 1.094367e-01, 'ple_norm': 3.908345e-02, 'w_ple_gate': 3.893597e-02, 'w_ple': 9.786942e-02, 'final_norm': 6.405652e+01}


def _to_microbatches(a, axis):
    t = _jnp.moveaxis(a, axis, 0)
    t = t.reshape((N_MICROBATCH, t.shape[0] // N_MICROBATCH) + t.shape[1:])
    return _jnp.moveaxis(t, 1, axis + 1)


def setup_inputs(seed: int = 0) -> dict:
    inp = _fwd_setup_inputs(seed)
    key = _jax.random.fold_in(_jax.random.key(seed), 7919)
    shape, _ = _output_shape()
    out = dict(inp)
    out["loss_target"] = _jax.random.normal(_jax.random.fold_in(key, 0), shape, _jnp.float32)
    for i, name in enumerate(TWIN_WEIGHTS):
        w = inp[name].astype(_jnp.float32)
        if MOMENT_SCALE is None:
            s = _jnp.sqrt(_jnp.mean(_jnp.square(w)) + 1e-30)
        else:
            s = MOMENT_SCALE[name]
        km, kv = _jax.random.split(_jax.random.fold_in(key, i + 1))
        out[name] = w
        out["m_" + name] = s * _jax.random.normal(km, w.shape, _jnp.float32)
        out["v_" + name] = (s * s) * _jax.random.uniform(kv, w.shape, _jnp.float32, 0.5, 1.5)
    if N_MICROBATCH > 1:
        for name, axis in PER_EXAMPLE_BATCH_AXIS.items():
            out[name] = _to_microbatches(out[name], axis)
    return {'x': out['x'], 'p': out['p'], 'mix_norm': out['mix_norm'], 'w_in': out['w_in'], 'gate_b': out['gate_b'], 'conv_w': out['conv_w'], 'conv_b': out['conv_b'], 'lru_wa': out['lru_wa'], 'lru_ba': out['lru_ba'], 'lru_wx': out['lru_wx'], 'lru_bx': out['lru_bx'], 'lru_lambda': out['lru_lambda'], 'mla_q_norm': out['mla_q_norm'], 'mla_wuq': out['mla_wuq'], 'mla_kv_norm': out['mla_kv_norm'], 'mla_wukv': out['mla_wukv'], 'fox_bf': out['fox_bf'], 'w_br_a': out['w_br_a'], 'w_br_b': out['w_br_b'], 'w_br_c': out['w_br_c'], 'w_o': out['w_o'], 'ffn_norm': out['ffn_norm'], 'w_gate_up': out['w_gate_up'], 'w_down': out['w_down'], 'ple_norm': out['ple_norm'], 'w_ple_gate': out['w_ple_gate'], 'w_ple': out['w_ple'], 'final_norm': out['final_norm'], 'loss_target': out['loss_target'], 'm_mix_norm': out['m_mix_norm'], 'm_w_in': out['m_w_in'], 'm_gate_b': out['m_gate_b'], 'm_conv_w': out['m_conv_w'], 'm_conv_b': out['m_conv_b'], 'm_lru_wa': out['m_lru_wa'], 'm_lru_ba': out['m_lru_ba'], 'm_lru_wx': out['m_lru_wx'], 'm_lru_bx': out['m_lru_bx'], 'm_lru_lambda': out['m_lru_lambda'], 'm_mla_q_norm': out['m_mla_q_norm'], 'm_mla_wuq': out['m_mla_wuq'], 'm_mla_kv_norm': out['m_mla_kv_norm'], 'm_mla_wukv': out['m_mla_wukv'], 'm_fox_bf': out['m_fox_bf'], 'm_w_br_a': out['m_w_br_a'], 'm_w_br_b': out['m_w_br_b'], 'm_w_br_c': out['m_w_br_c'], 'm_w_o': out['m_w_o'], 'm_ffn_norm': out['m_ffn_norm'], 'm_w_gate_up': out['m_w_gate_up'], 'm_w_down': out['m_w_down'], 'm_ple_norm': out['m_ple_norm'], 'm_w_ple_gate': out['m_w_ple_gate'], 'm_w_ple': out['m_w_ple'], 'm_final_norm': out['m_final_norm'], 'v_mix_norm': out['v_mix_norm'], 'v_w_in': out['v_w_in'], 'v_gate_b': out['v_gate_b'], 'v_conv_w': out['v_conv_w'], 'v_conv_b': out['v_conv_b'], 'v_lru_wa': out['v_lru_wa'], 'v_lru_ba': out['v_lru_ba'], 'v_lru_wx': out['v_lru_wx'], 'v_lru_bx': out['v_lru_bx'], 'v_lru_lambda': out['v_lru_lambda'], 'v_mla_q_norm': out['v_mla_q_norm'], 'v_mla_wuq': out['v_mla_wuq'], 'v_mla_kv_norm': out['v_mla_kv_norm'], 'v_mla_wukv': out['v_mla_wukv'], 'v_fox_bf': out['v_fox_bf'], 'v_w_br_a': out['v_w_br_a'], 'v_w_br_b': out['v_w_br_b'], 'v_w_br_c': out['v_w_br_c'], 'v_w_o': out['v_w_o'], 'v_ffn_norm': out['v_ffn_norm'], 'v_w_gate_up': out['v_w_gate_up'], 'v_w_down': out['v_w_down'], 'v_ple_norm': out['v_ple_norm'], 'v_w_ple_gate': out['v_w_ple_gate'], 'v_w_ple': out['v_w_ple'], 'v_final_norm': out['v_final_norm']}


def _loss(weights, diff, rest, loss_target):
    with _jax.named_scope("forward"):
        args = {**rest, TWIN_DIFF_INPUT: diff, **{k: w.astype(_WEIGHT_DTYPES[k]) for k, w in weights.items()}}
        y = _forward(args)
    with _jax.named_scope("loss_head"):
        err = _jnp.square(y.astype(_jnp.float32) - loss_target)
        return 0.5 * _jnp.sum(_jnp.mean(err, axis=-1)) if err.ndim else 0.5 * err


def _adamw(w, g, m, v):
    m = ADAM_B1 * m + (1.0 - ADAM_B1) * g
    v = ADAM_B2 * v + (1.0 - ADAM_B2) * _jnp.square(g)
    m_hat = m / (1.0 - ADAM_B1 ** ADAM_STEP)
    v_hat = v / (1.0 - ADAM_B2 ** ADAM_STEP)
    delta = -ADAM_LR * (m_hat / (_jnp.sqrt(v_hat) + ADAM_EPS) + ADAM_WD * w)
    return delta, m, v


def reference(x, p, mix_norm, w_in, gate_b, conv_w, conv_b, lru_wa, lru_ba, lru_wx, lru_bx, lru_lambda, mla_q_norm, mla_wuq, mla_kv_norm, mla_wukv, fox_bf, w_br_a, w_br_b, w_br_c, w_o, ffn_norm, w_gate_up, w_down, ple_norm, w_ple_gate, w_ple, final_norm, loss_target, m_mix_norm, m_w_in, m_gate_b, m_conv_w, m_conv_b, m_lru_wa, m_lru_ba, m_lru_wx, m_lru_bx, m_lru_lambda, m_mla_q_norm, m_mla_wuq, m_mla_kv_norm, m_mla_wukv, m_fox_bf, m_w_br_a, m_w_br_b, m_w_br_c, m_w_o, m_ffn_norm, m_w_gate_up, m_w_down, m_ple_norm, m_w_ple_gate, m_w_ple, m_final_norm, v_mix_norm, v_w_in, v_gate_b, v_conv_w, v_conv_b, v_lru_wa, v_lru_ba, v_lru_wx, v_lru_bx, v_lru_lambda, v_mla_q_norm, v_mla_wuq, v_mla_kv_norm, v_mla_wukv, v_fox_bf, v_w_br_a, v_w_br_b, v_w_br_c, v_w_o, v_ffn_norm, v_w_gate_up, v_w_down, v_ple_norm, v_w_ple_gate, v_w_ple, v_final_norm):
    given = dict(x=x, p=p, mix_norm=mix_norm, w_in=w_in, gate_b=gate_b, conv_w=conv_w, conv_b=conv_b, lru_wa=lru_wa, lru_ba=lru_ba, lru_wx=lru_wx, lru_bx=lru_bx, lru_lambda=lru_lambda, mla_q_norm=mla_q_norm, mla_wuq=mla_wuq, mla_kv_norm=mla_kv_norm, mla_wukv=mla_wukv, fox_bf=fox_bf, w_br_a=w_br_a, w_br_b=w_br_b, w_br_c=w_br_c, w_o=w_o, ffn_norm=ffn_norm, w_gate_up=w_gate_up, w_down=w_down, ple_norm=ple_norm, w_ple_gate=w_ple_gate, w_ple=w_ple, final_norm=final_norm, loss_target=loss_target, m_mix_norm=m_mix_norm, m_w_in=m_w_in, m_gate_b=m_gate_b, m_conv_w=m_conv_w, m_conv_b=m_conv_b, m_lru_wa=m_lru_wa, m_lru_ba=m_lru_ba, m_lru_wx=m_lru_wx, m_lru_bx=m_lru_bx, m_lru_lambda=m_lru_lambda, m_mla_q_norm=m_mla_q_norm, m_mla_wuq=m_mla_wuq, m_mla_kv_norm=m_mla_kv_norm, m_mla_wukv=m_mla_wukv, m_fox_bf=m_fox_bf, m_w_br_a=m_w_br_a, m_w_br_b=m_w_br_b, m_w_br_c=m_w_br_c, m_w_o=m_w_o, m_ffn_norm=m_ffn_norm, m_w_gate_up=m_w_gate_up, m_w_down=m_w_down, m_ple_norm=m_ple_norm, m_w_ple_gate=m_w_ple_gate, m_w_ple=m_w_ple, m_final_norm=m_final_norm, v_mix_norm=v_mix_norm, v_w_in=v_w_in, v_gate_b=v_gate_b, v_conv_w=v_conv_w, v_conv_b=v_conv_b, v_lru_wa=v_lru_wa, v_lru_ba=v_lru_ba, v_lru_wx=v_lru_wx, v_lru_bx=v_lru_bx, v_lru_lambda=v_lru_lambda, v_mla_q_norm=v_mla_q_norm, v_mla_wuq=v_mla_wuq, v_mla_kv_norm=v_mla_kv_norm, v_mla_wukv=v_mla_wukv, v_fox_bf=v_fox_bf, v_w_br_a=v_w_br_a, v_w_br_b=v_w_br_b, v_w_br_c=v_w_br_c, v_w_o=v_w_o, v_ffn_norm=v_ffn_norm, v_w_gate_up=v_w_gate_up, v_w_down=v_w_down, v_ple_norm=v_ple_norm, v_w_ple_gate=v_w_ple_gate, v_w_ple=v_w_ple, v_final_norm=v_final_norm)
    weights = {n: given[n] for n in TWIN_WEIGHTS}
    shared = {n: given[n] for n in SHARED_INPUTS}
    per_example = {n: given[n] for n in ['x', 'p']}
    grad_fn = _jax.value_and_grad(_loss, argnums=(0, 1))

    def one_microbatch(ex, loss_target):
        ex = dict(ex)
        diff = ex.pop(TWIN_DIFF_INPUT)
        return grad_fn(weights, diff, {**shared, **ex}, loss_target)

    if N_MICROBATCH == 1:
        loss, (grad_w, grad_x) = one_microbatch(per_example, given["loss_target"])
    else:
        def body(carry, xs):
            loss_sum, grad_sum = carry
            l_k, (gw_k, gx_k) = one_microbatch(xs[0], xs[1])
            with _jax.named_scope("update"):
                return (loss_sum + l_k, _jax.tree.map(_jnp.add, grad_sum, gw_k)), gx_k

        init = (_jnp.zeros((), _jnp.float32), _jax.tree.map(_jnp.zeros_like, weights))
        (loss, grad_w), grad_x = _jax.lax.scan(body, init, (per_example, given["loss_target"]))
    with _jax.named_scope("update"):
        delta_w, new_m, new_v = {}, {}, {}
        for n in TWIN_WEIGHTS:
            delta_w[n], new_m[n], new_v[n] = _adamw(weights[n], grad_w[n], given["m_" + n], given["v_" + n])
    return (loss, grad_x, *[grad_w[n] for n in TWIN_WEIGHTS], *[delta_w[n] for n in TWIN_WEIGHTS],
            *[new_m[n] for n in TWIN_WEIGHTS], *[new_v[n] for n in TWIN_WEIGHTS])
```

```python
import functools
import math

import jax
import jax.numpy as jnp
from jax import lax
from jax.experimental import pallas as pl
from jax.experimental.pallas import tpu as pltpu

F32 = jnp.float32
BF16 = jnp.bfloat16
MXU_DTYPE = jnp.bfloat16

D_MODEL = 1024
DEPTH = 2
CHUNK = 64
EPS = 1e-6
NEG_INF = -1e30
LRU_WIDTH = 512
LRU_HEADS = 8
LRU_HEAD_DIM = 64
CONV_WIDTH = 4
LRU_C = 8.0
HEADS = 8
MLA_Q_LORA = 384
MLA_KV_LORA = 256
MLA_NOPE = 64
MLA_ROPE = 32
MLA_V = 64
ROPE_BASE = 10000.0
FOX_HEAD_DIM = 64
FOX_WIDTH = 512
D_FF = 2816
PLE_DIM = 256
D_IN = 6312
ADAM_LR = 0.001
ADAM_B1 = 0.9
ADAM_B2 = 0.999
ADAM_EPS = 1e-08
ADAM_WD = 0.01
ADAM_STEP = 10

V7X_VMEM_BYTES = 64 * 1024 * 1024
V7X_LANES = 128
V7X_SUBLANES = 8
VMEM_LIMIT_CAP = 56 * 1024 * 1024
N_DEV = 8

SEGS = (
    ("u", 0, 512, 0, 512),
    ("ug", 512, 512, 512, 512),
    ("cq", 1024, 384, 1152, 384),
    ("ckv", 1408, 256, 1536, 256),
    ("kr", 1664, 32, 1792, 128),
    ("fq", 1696, 512, 2048, 512),
    ("fk", 2208, 512, 2560, 512),
    ("fv", 2720, 512, 3072, 512),
    ("fl", 3232, 8, 3584, 128),
    ("gate", 3240, 3072, 4096, 1024),
)
D_IN_PAD = 7168
SEG = {s[0]: s for s in SEGS}

SHARDED = (("w_in", 2), ("mla_wuq", 2), ("mla_wukv", 2), ("w_br_a", 2), ("w_br_b", 2), ("w_br_c", 2),
           ("w_o", 1), ("w_gate_up", 2), ("w_down", 1), ("w_ple_gate", 1), ("w_ple", 2), ("conv_w", 2))
REPLICATED = ("mix_norm", "gate_b", "conv_b", "lru_wa", "lru_ba", "lru_wx", "lru_bx", "lru_lambda",
              "mla_q_norm", "mla_kv_norm", "fox_bf", "ffn_norm", "ple_norm", "final_norm")
WEIGHTS = ("mix_norm", "w_in", "gate_b", "conv_w", "conv_b", "lru_wa", "lru_ba", "lru_wx", "lru_bx",
           "lru_lambda", "mla_q_norm", "mla_wuq", "mla_kv_norm", "mla_wukv", "fox_bf", "w_br_a", "w_br_b",
           "w_br_c", "w_o", "ffn_norm", "w_gate_up", "w_down", "ple_norm", "w_ple_gate", "w_ple", "final_norm")
PAYLOAD_LANES = 1024


def _tile(n, cap=1024):
    best = None
    for t in range(V7X_LANES, min(n, cap) + 1, V7X_LANES):
        if n % t == 0:
            best = t
    return best if best is not None else n


def _row_tile(s, pref):
    t = min(pref, s // 2)
    assert s % t == 0 and t % V7X_SUBLANES == 0
    return t


def _nbytes(shape, dtype):
    return math.prod(shape) * jnp.dtype(dtype).itemsize


def _params(sem, vmem_bytes):
    limit = int(min(VMEM_LIMIT_CAP, max(16 * 1024 * 1024, vmem_bytes)))
    return pltpu.CompilerParams(dimension_semantics=sem, vmem_limit_bytes=limit)


def _full(shape):
    return pl.BlockSpec(shape, lambda *_: (0,) * len(shape))


def _rows(t, w, cb=0):
    return pl.BlockSpec((t, w), lambda i: (i, cb))


def _mxu(v):
    return v.astype(MXU_DTYPE)


def _dot(a, b):
    return lax.dot_general(_mxu(a), _mxu(b), (((1,), (0,)), ((), ())), preferred_element_type=F32)


def _dot_nt(a, b):
    return lax.dot_general(_mxu(a), _mxu(b), (((1,), (1,)), ((), ())), preferred_element_type=F32)


def _dot_tn(a, b):
    return lax.dot_general(_mxu(a), _mxu(b), (((0,), (0,)), ((), ())), preferred_element_type=F32)


def _sigmoid(v):
    return 1.0 / (1.0 + jnp.exp(-v))


def _softplus(v):
    return jnp.maximum(v, 0.0) + jnp.log(1.0 + jnp.exp(-jnp.abs(v)))


def _neg_expm1(v):
    series = -v * (1.0 + v * (0.5 + v * (1.0 / 6.0 + v * (1.0 / 24.0))))
    return jnp.where(v > -0.03, series, 1.0 - jnp.exp(v))


_GELU_C = math.sqrt(2.0 / math.pi)
_GELU_A = 0.044715


def _gelu(v):
    t = jnp.tanh(_GELU_C * (v + _GELU_A * v * v * v))
    return 0.5 * v * (1.0 + t)


def _gelu_grad(v):
    t = jnp.tanh(_GELU_C * (v + _GELU_A * v * v * v))
    return 0.5 * (1.0 + t) + 0.5 * v * (1.0 - t * t) * _GELU_C * (1.0 + 3.0 * _GELU_A * v * v)


def _mm(a, b, *, ta=False, tb=False, out_dtype=F32, res=None, name):
    k_dim, m_dim = (a.shape[0], a.shape[1]) if ta else (a.shape[1], a.shape[0])
    n_dim = b.shape[0] if tb else b.shape[1]
    assert (b.shape[1] if tb else b.shape[0]) == k_dim
    tm, tn, tk = _tile(m_dim), _tile(n_dim, 1408), _tile(k_dim, 1408)
    nk = k_dim // tk
    a_spec = pl.BlockSpec((tk, tm), lambda i, j, k: (k, i)) if ta else pl.BlockSpec((tm, tk), lambda i, j, k: (i, k))
    b_spec = pl.BlockSpec((tn, tk), lambda i, j, k: (j, k)) if tb else pl.BlockSpec((tk, tn), lambda i, j, k: (k, j))
    o_spec = pl.BlockSpec((tm, tn), lambda i, j, k: (i, j))
    has_res = res is not None

    def body(*refs):
        a_ref, b_ref = refs[0], refs[1]
        res_ref = refs[2] if has_res else None
        o_ref = refs[3] if has_res else refs[2]
        acc_ref = refs[-1]
        k = pl.program_id(2)
        if ta:
            part = _dot_tn(a_ref[...], b_ref[...])
        elif tb:
            part = _dot_nt(a_ref[...], b_ref[...])
        else:
            part = _dot(a_ref[...], b_ref[...])

        def finish(total):
            if has_res:
                total = total + res_ref[...].astype(F32)
            o_ref[...] = total.astype(o_ref.dtype)

        if nk == 1:
            finish(part)
        else:
            @pl.when(k == 0)
            def _():
                acc_ref[...] = part

            @pl.when(jnp.logical_and(k > 0, k < nk - 1))
            def _():
                acc_ref[...] += part

            @pl.when(k == nk - 1)
            def _():
                finish(acc_ref[...] + part)

    ins = [a, b] + ([res] if has_res else [])
    in_specs = [a_spec, b_spec] + ([o_spec] if has_res else [])
    acc_shape = (tm, tn) if nk > 1 else (V7X_SUBLANES, V7X_LANES)
    vmem = (2 * (_nbytes((tm, tk), a.dtype) + _nbytes((tk, tn), b.dtype) + _nbytes((tm, tn), out_dtype)
                 + (_nbytes((tm, tn), res.dtype) if has_res else 0))
            + _nbytes((tm, tk), MXU_DTYPE) + _nbytes((tk, tn), MXU_DTYPE) + 3 * _nbytes((tm, tn), F32))
    return pl.pallas_call(
        body, name=name, grid=(m_dim // tm, n_dim // tn, nk),
        in_specs=in_specs, out_specs=o_spec,
        out_shape=jax.ShapeDtypeStruct((m_dim, n_dim), out_dtype),
        scratch_shapes=[pltpu.VMEM(acc_shape, F32)],
        compiler_params=_params(("parallel", "parallel", "arbitrary"), vmem),
    )(*ins)


def _rms_fwd(x, g, *, width, cb=0, name):
    s = x.shape[0]
    t = _tile(s, 512)

    def body(x_ref, g_ref, o_ref):
        xv = x_ref[...].astype(F32)
        r = lax.rsqrt(jnp.mean(xv * xv, axis=-1, keepdims=True) + EPS)
        o_ref[...] = (xv * r * g_ref[...]).astype(o_ref.dtype)

    return pl.pallas_call(
        body, name=name, grid=(s // t,),
        in_specs=[_rows(t, width, cb), _full((1, width))], out_specs=_rows(t, width),
        out_shape=jax.ShapeDtypeStruct((s, width), MXU_DTYPE),
        compiler_params=_params(("parallel",), 8 * _nbytes((t, width), F32)),
    )(x, g)


def _rms_bwd(x, g, dn, *, width, cb=0, res=None, out_dtype=F32, name):
    s = x.shape[0]
    t = _tile(s, 256)
    has_res = res is not None

    def body(*refs):
        x_ref, g_ref, dn_ref = refs[:3]
        res_ref = refs[3] if has_res else None
        dx_ref, dg_ref = refs[-2], refs[-1]
        xv = x_ref[...].astype(F32)
        dnv = dn_ref[...].astype(F32)
        r = lax.rsqrt(jnp.mean(xv * xv, axis=-1, keepdims=True) + EPS)
        xr = xv * r
        dng = dnv * g_ref[...]
        dx = r * dng - xr * (r * r) * jnp.mean(dng * xv, axis=-1, keepdims=True)
        if has_res:
            dx = dx + res_ref[...].astype(F32)
        dx_ref[...] = dx.astype(dx_ref.dtype)
        part = jnp.sum(dnv * xr, axis=0, keepdims=True)

        @pl.when(pl.program_id(0) == 0)
        def _():
            dg_ref[...] = part

        @pl.when(pl.program_id(0) > 0)
        def _():
            dg_ref[...] += part

    ins = [x, g, dn] + ([res] if has_res else [])
    in_specs = [_rows(t, width, cb), _full((1, width)), _rows(t, width)] + ([_rows(t, width)] if has_res else [])
    return pl.pallas_call(
        body, name=name, grid=(s // t,),
        in_specs=in_specs, out_specs=[_rows(t, width), _full((1, width))],
        out_shape=[jax.ShapeDtypeStruct((s, width), out_dtype), jax.ShapeDtypeStruct((1, width), F32)],
        compiler_params=_params(("arbitrary",), 16 * _nbytes((t, width), F32)),
    )(*ins)


def _final_loss(x, g, target, *, name):
    s, d = x.shape
    t = _tile(s, 256)

    def body(x_ref, g_ref, t_ref, loss_ref, dx_ref, dg_ref):
        xv = x_ref[...]
        r = lax.rsqrt(jnp.mean(xv * xv, axis=-1, keepdims=True) + EPS)
        xr = xv * r
        err = xr * g_ref[...] - t_ref[...]
        part_loss = 0.5 * jnp.sum(jnp.mean(err * err, axis=-1, keepdims=True), axis=0, keepdims=True)
        dnv = err * (1.0 / d)
        dng = dnv * g_ref[...]
        dx_ref[...] = r * dng - xr * (r * r) * jnp.mean(dng * xv, axis=-1, keepdims=True)
        part_dg = jnp.sum(dnv * xr, axis=0, keepdims=True)

        @pl.when(pl.program_id(0) == 0)
        def _():
            dg_ref[...] = part_dg
            loss_ref[...] = jnp.zeros(loss_ref.shape, F32) + part_loss

        @pl.when(pl.program_id(0) > 0)
        def _():
            dg_ref[...] += part_dg
            loss_ref[...] += part_loss

    return pl.pallas_call(
        body, name=name, grid=(s // t,),
        in_specs=[_rows(t, d), _full((1, d)), _rows(t, d)],
        out_specs=[_full((V7X_SUBLANES, V7X_LANES)), _rows(t, d), _full((1, d))],
        out_shape=[jax.ShapeDtypeStruct((V7X_SUBLANES, V7X_LANES), F32), jax.ShapeDtypeStruct((s, d), F32),
                   jax.ShapeDtypeStruct((1, d), F32)],
        compiler_params=_params(("arbitrary",), 16 * _nbytes((t, d), F32)),
    )(x, g, target)


def _shift_down(v, d, fill, rows):
    return jnp.where(rows >= d, pltpu.roll(v, d, 0), fill)


def _shift_up(v, d, fill, rows, t):
    return jnp.where(rows < t - d, pltpu.roll(v, t - d, 0), fill)


def _lru_gates(xc, wa_ref, ba_ref, wx_ref, bx_ref, lam_ref):
    ra = _sigmoid(_dot(xc, wa_ref[...]) + ba_ref[...])
    ig = _sigmoid(_dot(xc, wx_ref[...]) + bx_ref[...])
    sp = _softplus(-lam_ref[...])
    log_a = -LRU_C * ra * sp
    a = jnp.exp(log_a)
    s2 = _neg_expm1(2.0 * log_a)
    return ra, ig, sp, a, s2


def _conv(ubuf, cw_ref, cb_ref, t):
    big = ubuf[...]
    shifted = [pltpu.roll(big, CONV_WIDTH - 1 - k, 0)[V7X_SUBLANES:t + V7X_SUBLANES] if k < CONV_WIDTH - 1
               else big[V7X_SUBLANES:t + V7X_SUBLANES] for k in range(CONV_WIDTH)]
    xc = cb_ref[...] + shifted[0] * cw_ref[0:1, :]
    for k in range(1, CONV_WIDTH):
        xc = xc + shifted[k] * cw_ref[k:k + 1, :]
    return xc, shifted


def _lru_fwd(z, cw, cb, wa, ba, wx, bx, lam, *, name):
    s = z.shape[0]
    w = LRU_WIDTH
    t = _row_tile(s, 256)
    steps = [1 << k for k in range(int(math.log2(t)))]

    def body(u_ref, ug_ref, cw_ref, cb_ref, wa_ref, ba_ref, wx_ref, bx_ref, lam_ref, y_ref, h_ref, ubuf, hc):
        @pl.when(pl.program_id(0) == 0)
        def _():
            ubuf[0:V7X_SUBLANES, :] = jnp.zeros((V7X_SUBLANES, w), F32)
            hc[...] = jnp.zeros_like(hc)

        ubuf[V7X_SUBLANES:t + V7X_SUBLANES, :] = u_ref[...]
        xc, _ = _conv(ubuf, cw_ref, cb_ref, t)
        _, ig, _, a, s2 = _lru_gates(xc, wa_ref, ba_ref, wx_ref, bx_ref, lam_ref)
        b = jnp.sqrt(s2) * (ig * xc)
        rows = lax.broadcasted_iota(jnp.int32, (t, w), 0)
        for d in steps:
            b = a * _shift_down(b, d, 0.0, rows) + b
            a = a * _shift_down(a, d, 1.0, rows)
        h = a * hc[0:1, :] + b
        h_ref[...] = h
        y_ref[...] = (h * _gelu(ug_ref[...])).astype(y_ref.dtype)
        hc[0:1, :] = h_ref[t - 1:t, :]
        ubuf[0:V7X_SUBLANES, :] = ubuf[t:t + V7X_SUBLANES, :]

    vec = _full((1, w))
    return pl.pallas_call(
        body, name=name, grid=(s // t,),
        in_specs=[_rows(t, w, 0), _rows(t, w, 1), _full((CONV_WIDTH, w)), vec, _full((w, w)), vec, _full((w, w)),
                  vec, vec],
        out_specs=[_rows(t, w), _rows(t, w)],
        out_shape=[jax.ShapeDtypeStruct((s, w), MXU_DTYPE), jax.ShapeDtypeStruct((s, w), F32)],
        scratch_shapes=[pltpu.VMEM((t + V7X_SUBLANES, w), F32), pltpu.VMEM((V7X_SUBLANES, w), F32)],
        compiler_params=_params(("arbitrary",), 40 * _nbytes((t, w), F32)),
    )(z, z, cw, cb, wa, ba, wx, bx, lam)


def _lru_bwd(z, h, dy, cw, cb, wa, ba, wx, bx, lam, *, name):
    s = z.shape[0]
    w = LRU_WIDTH
    t = _row_tile(s, 256)
    nt = s // t
    per8 = t // V7X_SUBLANES
    steps = [1 << k for k in range(int(math.log2(t)))]

    def body(u_ref, ug_ref, h_ref, dy_ref, uprev_ref, hprev_ref, cw_ref, cb_ref, wa_ref, ba_ref, wx_ref, bx_ref,
             lam_ref, dz_ref, dcw_ref, dcb_ref, dwa_ref, dba_ref, dwx_ref, dbx_ref, dlam_ref,
             ubuf, dbuf, acar, dhcar, tmp):
        i = pl.program_id(0)
        first_tile = i == nt - 1

        @pl.when(i == 0)
        def _():
            for r in (dcw_ref, dcb_ref, dwa_ref, dba_ref, dwx_ref, dbx_ref, dlam_ref, acar, dhcar):
                r[...] = jnp.zeros_like(r)
            dbuf[t:t + V7X_SUBLANES, :] = jnp.zeros((V7X_SUBLANES, w), F32)

        keep = jnp.where(first_tile, 0.0, 1.0)
        ubuf[0:V7X_SUBLANES, :] = uprev_ref[...] * keep
        ubuf[V7X_SUBLANES:t + V7X_SUBLANES, :] = u_ref[...]
        xc, shifted = _conv(ubuf, cw_ref, cb_ref, t)
        ra, ig, sp, a, s2 = _lru_gates(xc, wa_ref, ba_ref, wx_ref, bx_ref, lam_ref)
        sq = jnp.sqrt(s2)
        gx = ig * xc
        rows = lax.broadcasted_iota(jnp.int32, (t, w), 0)
        ugv = ug_ref[...]
        dyv = dy_ref[...].astype(F32)
        hv = h_ref[...]

        acc_g = dyv * _gelu(ugv)
        acc_a = _shift_up(a, 1, acar[0:1, :], rows, t)
        for d in steps:
            acc_g = acc_a * _shift_up(acc_g, d, 0.0, rows, t) + acc_g
            acc_a = acc_a * _shift_up(acc_a, d, 1.0, rows, t)
        dh = acc_a * dhcar[0:1, :] + acc_g

        hprev = _shift_down(hv, 1, hprev_ref[V7X_SUBLANES - 1:V7X_SUBLANES, :] * keep, rows)
        d_a = dh * hprev
        d_sq = dh * gx
        d_gx = dh * sq
        d_ig = d_gx * xc
        dxc = d_gx * ig
        d_log_a = d_a * a - d_sq * (1.0 - s2) / sq
        d_ra = d_log_a * (-LRU_C * sp)
        lamv = lam_ref[...]
        dlam_ref[...] += jnp.sum(d_log_a * (-LRU_C * ra), axis=0, keepdims=True) * (-_sigmoid(-lamv))
        dpa = d_ra * ra * (1.0 - ra)
        dpx = d_ig * ig * (1.0 - ig)
        dba_ref[...] += jnp.sum(dpa, axis=0, keepdims=True)
        dbx_ref[...] += jnp.sum(dpx, axis=0, keepdims=True)
        dwa_ref[...] += _dot_tn(xc, dpa)
        dwx_ref[...] += _dot_tn(xc, dpx)
        dxc = dxc + _dot_nt(dpa, wa_ref[...]) + _dot_nt(dpx, wx_ref[...])
        dcb_ref[...] += jnp.sum(dxc, axis=0, keepdims=True)
        for k in range(CONV_WIDTH):
            dcw_ref[k:k + 1, :] += jnp.sum(dxc * shifted[k], axis=0, keepdims=True)

        dbuf[0:t, :] = dxc
        bigd = dbuf[...]
        du = dxc * cw_ref[CONV_WIDTH - 1:CONV_WIDTH, :]
        for k in range(CONV_WIDTH - 1):
            e = CONV_WIDTH - 1 - k
            du = du + pltpu.roll(bigd, t + V7X_SUBLANES - e, 0)[0:t] * cw_ref[k:k + 1, :]
        dz_ref[:, 0:w] = du.astype(dz_ref.dtype)
        dz_ref[:, w:2 * w] = (dyv * hv * _gelu_grad(ugv)).astype(dz_ref.dtype)

        dbuf[t:t + V7X_SUBLANES, :] = dbuf[0:V7X_SUBLANES, :]
        tmp[...] = a
        acar[0:1, :] = tmp[0:1, :]
        tmp[...] = dh
        dhcar[0:1, :] = tmp[0:1, :]

    vec = _full((1, w))
    rev = lambda cbk: pl.BlockSpec((t, w), lambda i: (nt - 1 - i, cbk))
    prev8 = lambda cbk: pl.BlockSpec((V7X_SUBLANES, w),
                                     lambda i: (jnp.maximum((nt - 1 - i) * per8 - 1, 0), cbk))
    return pl.pallas_call(
        body, name=name, grid=(nt,),
        in_specs=[rev(0), rev(1), rev(0), rev(0), prev8(0), prev8(0), _full((CONV_WIDTH, w)), vec, _full((w, w)),
                  vec, _full((w, w)), vec, vec],
        out_specs=[pl.BlockSpec((t, 2 * w), lambda i: (nt - 1 - i, 0)), _full((CONV_WIDTH, w)), vec,
                   _full((w, w)), vec, _full((w, w)), vec, vec],
        out_shape=[jax.ShapeDtypeStruct((s, 2 * w), MXU_DTYPE), jax.ShapeDtypeStruct((CONV_WIDTH, w), F32),
                   jax.ShapeDtypeStruct((1, w), F32), jax.ShapeDtypeStruct((w, w), F32),
                   jax.ShapeDtypeStruct((1, w), F32), jax.ShapeDtypeStruct((w, w), F32),
                   jax.ShapeDtypeStruct((1, w), F32), jax.ShapeDtypeStruct((1, w), F32)],
        scratch_shapes=[pltpu.VMEM((t + V7X_SUBLANES, w), F32), pltpu.VMEM((t + V7X_SUBLANES, w), F32),
                        pltpu.VMEM((V7X_SUBLANES, w), F32), pltpu.VMEM((V7X_SUBLANES, w), F32),
                        pltpu.VMEM((t, w), F32)],
        compiler_params=_params(("arbitrary",), 80 * _nbytes((t, w), F32)),
    )(z, z, h, dy, z, h, cw, cb, wa, ba, wx, bx, lam)


def _rope_apply(v, cos, sin, width):
    half = MLA_ROPE // 2
    lanes = lax.broadcasted_iota(jnp.int32, v.shape, 1)
    first = (lanes % MLA_ROPE) < half
    partner = jnp.where(first, pltpu.roll(v, width - half, 1), pltpu.roll(v, half, 1))
    return v * cos + partner * sin


def _rope(x, cos, sin, *, width, cb, out_dtype, sum_heads=False, name):
    s = x.shape[0]
    t = _tile(s, 512)
    out_w = V7X_LANES if sum_heads else width

    def body(x_ref, c_ref, s_ref, o_ref):
        v = x_ref[...].astype(F32)
        if sum_heads:
            v = v[:, 0:V7X_LANES] + v[:, V7X_LANES:2 * V7X_LANES]
            v = v + pltpu.roll(v, 64, 1)
            v = v + pltpu.roll(v, 32, 1)
            out = _rope_apply(v, c_ref[...], s_ref[...], V7X_LANES)
            lanes = lax.broadcasted_iota(jnp.int32, out.shape, 1)
            out = jnp.where(lanes < MLA_ROPE, out, 0.0)
        else:
            out = _rope_apply(v, c_ref[...], s_ref[...], width)
        o_ref[...] = out.astype(o_ref.dtype)

    return pl.pallas_call(
        body, name=name, grid=(s // t,),
        in_specs=[_rows(t, width, cb), _rows(t, out_w), _rows(t, out_w)], out_specs=_rows(t, out_w),
        out_shape=jax.ShapeDtypeStruct((s, out_w), out_dtype),
        compiler_params=_params(("parallel",), 12 * _nbytes((t, width), F32)),
    )(x, cos, sin)


def _visible(t, unit, transposed):
    q_idx = lax.broadcasted_iota(jnp.int32, (t, t), 1 if transposed else 0)
    k_idx = lax.broadcasted_iota(jnp.int32, (t, t), 0 if transposed else 1)
    shift = int(math.log2(unit))
    return (q_idx >> shift) >= (k_idx >> shift)


def _attn_tile(s):
    return min(512, s // 4)


def _attn_fwd(q, k, v, cq, ck, *, scale, unit, name):
    hn, s, dk = q.shape
    dv = v.shape[-1]
    t = _attn_tile(s)
    decay = cq is not None

    def body(*refs):
        q_ref, k_ref, v_ref = refs[:3]
        cq_ref, ck_ref = (refs[3], refs[4]) if decay else (None, None)
        o_ref, lse_ref = refs[-2], refs[-1]
        i = pl.program_id(1)
        qt = q_ref[0]

        def tile(j, carry, masked):
            m, l, acc = carry
            off = pl.multiple_of(j * t, t)
            kt = k_ref[0, pl.ds(off, t), :]
            vt = v_ref[0, pl.ds(off, t), :]
            sc = _dot_nt(qt, kt) * scale
            if decay:
                sc = sc + cq_ref[0] - ck_ref[0, :, pl.ds(off, t)]
            if masked:
                sc = jnp.where(_visible(t, unit, False), sc, NEG_INF)
            m_new = jnp.maximum(m, jnp.max(sc, axis=-1, keepdims=True))
            alpha = jnp.exp(m - m_new)
            pr = jnp.exp(sc - m_new)
            l = alpha * l + jnp.sum(pr, axis=-1, keepdims=True)
            acc = alpha * acc + _dot(pr, vt)
            return m_new, l, acc

        init = (jnp.full((t, 1), NEG_INF, F32), jnp.zeros((t, 1), F32), jnp.zeros((t, dv), F32))
        carry = lax.fori_loop(0, i, lambda j, c: tile(j, c, False), init)
        m, l, acc = tile(i, carry, True)
        o_ref[0] = (acc / l).astype(o_ref.dtype)
        lse_ref[0] = m + jnp.log(l)

    qs = lambda d: pl.BlockSpec((1, t, d), lambda h, i: (h, i, 0))
    whole = lambda d: pl.BlockSpec((1, s, d), lambda h, i: (h, 0, 0))
    in_specs = [qs(dk), whole(dk), whole(dv)]
    ins = [q, k, v]
    if decay:
        in_specs += [qs(1), pl.BlockSpec((1, 1, s), lambda h, i: (h, 0, 0))]
        ins += [cq, ck]
    vmem = 4 * _nbytes((s, dk + dv), q.dtype) + 10 * _nbytes((t, t), F32) + 8 * _nbytes((t, V7X_LANES), F32)
    return pl.pallas_call(
        body, name=name, grid=(hn, s // t),
        in_specs=in_specs, out_specs=[qs(dv), qs(1)],
        out_shape=[jax.ShapeDtypeStruct((hn, s, dv), MXU_DTYPE), jax.ShapeDtypeStruct((hn, s, 1), F32)],
        compiler_params=_params(("parallel", "arbitrary"), vmem),
    )(*ins)


def _attn_bwd_q(q, k, v, do, lse, cq, ck, *, scale, unit, name):
    hn, s, dk = q.shape
    dv = v.shape[-1]
    t = _attn_tile(s)
    nt = s // t
    decay = cq is not None

    def body(*refs):
        q_ref, k_ref, v_ref, do_ref, lse_ref = refs[:5]
        cq_ref, ck_ref = (refs[5], refs[6]) if decay else (None, None)
        dq_ref, dl_ref, p_sc, dp_sc = refs[-4:]
        i = pl.program_id(1)
        qt = q_ref[0]
        dot = do_ref[0]
        lse_t = lse_ref[0]

        def sweep1(j, delta, masked):
            off = pl.multiple_of(j * t, t)
            kt = k_ref[0, pl.ds(off, t), :]
            vt = v_ref[0, pl.ds(off, t), :]
            sc = _dot_nt(qt, kt) * scale
            if decay:
                sc = sc + cq_ref[0] - ck_ref[0, :, pl.ds(off, t)]
            if masked:
                sc = jnp.where(_visible(t, unit, False), sc, NEG_INF)
            pr = jnp.exp(sc - lse_t)
            dp = _dot_nt(dot, vt)
            p_sc[j] = pr
            dp_sc[j] = dp
            return delta + jnp.sum(pr * dp, axis=-1, keepdims=True)

        delta = lax.fori_loop(0, i, lambda j, c: sweep1(j, c, False), jnp.zeros((t, 1), F32))
        delta = sweep1(i, delta, True)

        def sweep2(j, dq):
            off = pl.multiple_of(j * t, t)
            ds = p_sc[j] * (dp_sc[j] - delta)
            return dq + _dot(ds, k_ref[0, pl.ds(off, t), :])

        dq = lax.fori_loop(0, i + 1, sweep2, jnp.zeros((t, dk), F32))
        dq_ref[0] = dq * scale
        dl_ref[0] = delta

    qs = lambda d: pl.BlockSpec((1, t, d), lambda h, i: (h, i, 0))
    whole = lambda d: pl.BlockSpec((1, s, d), lambda h, i: (h, 0, 0))
    in_specs = [qs(dk), whole(dk), whole(dv), qs(dv), qs(1)]
    ins = [q, k, v, do, lse]
    if decay:
        in_specs += [qs(1), pl.BlockSpec((1, 1, s), lambda h, i: (h, 0, 0))]
        ins += [cq, ck]
    vmem = (4 * _nbytes((s, dk + dv), q.dtype) + 2 * _nbytes((nt, t, t), F32) + 8 * _nbytes((t, t), F32)
            + 12 * _nbytes((t, V7X_LANES), F32))
    return pl.pallas_call(
        body, name=name, grid=(hn, nt),
        in_specs=in_specs, out_specs=[qs(dk), qs(1)],
        out_shape=[jax.ShapeDtypeStruct((hn, s, dk), F32), jax.ShapeDtypeStruct((hn, s, 1), F32)],
        scratch_shapes=[pltpu.VMEM((nt, t, t), F32), pltpu.VMEM((nt, t, t), F32)],
        compiler_params=_params(("parallel", "arbitrary"), vmem),
    )(*ins)


def _attn_bwd_kv(q, k, v, do, lse_row, delta_row, cq_row, ck, *, scale, unit, name):
    hn, s, dk = q.shape
    dv = v.shape[-1]
    t = _attn_tile(s)
    nt = s // t
    decay = ck is not None

    def body(*refs):
        k_ref, v_ref, q_ref, do_ref, lse_ref, dl_ref = refs[:6]
        ck_ref, cq_ref = (refs[6], refs[7]) if decay else (None, None)
        outs = refs[8:] if decay else refs[6:]
        dk_ref, dv_ref = outs[0], outs[1]
        j = pl.program_id(1)
        kt = k_ref[0]
        vt = v_ref[0]

        def tile(i, carry, masked):
            dk_acc, dv_acc, dc_acc = carry
            off = pl.multiple_of(i * t, t)
            qt = q_ref[0, pl.ds(off, t), :]
            dot = do_ref[0, pl.ds(off, t), :]
            sc = _dot_nt(kt, qt) * scale
            if decay:
                sc = sc + cq_ref[0, :, pl.ds(off, t)] - ck_ref[0]
            if masked:
                sc = jnp.where(_visible(t, unit, True), sc, NEG_INF)
            pr = jnp.exp(sc - lse_ref[0, :, pl.ds(off, t)])
            dv_acc = dv_acc + _dot(pr, dot)
            ds = pr * (_dot_nt(vt, dot) - dl_ref[0, :, pl.ds(off, t)])
            dk_acc = dk_acc + _dot(ds, qt)
            if decay:
                dc_acc = dc_acc + jnp.sum(ds, axis=-1, keepdims=True)
            return dk_acc, dv_acc, dc_acc

        init = (jnp.zeros((t, dk), F32), jnp.zeros((t, dv), F32), jnp.zeros((t, 1), F32))
        carry = tile(j, init, True)
        dk_acc, dv_acc, dc_acc = lax.fori_loop(j + 1, nt, lambda i, c: tile(i, c, False), carry)
        dk_ref[0] = dk_acc * scale
        dv_ref[0] = dv_acc
        if decay:
            outs[2][0] = -dc_acc

    ks = lambda d: pl.BlockSpec((1, t, d), lambda h, j: (h, j, 0))
    whole = lambda d: pl.BlockSpec((1, s, d), lambda h, j: (h, 0, 0))
    row = pl.BlockSpec((1, 1, s), lambda h, j: (h, 0, 0))
    in_specs = [ks(dk), ks(dv), whole(dk), whole(dv), row, row]
    ins = [k, v, q, do, lse_row, delta_row]
    out_specs = [ks(dk), ks(dv)]
    out_shape = [jax.ShapeDtypeStruct((hn, s, dk), F32), jax.ShapeDtypeStruct((hn, s, dv), F32)]
    if decay:
        in_specs += [ks(1), row]
        ins += [ck, cq_row]
        out_specs.append(ks(1))
        out_shape.append(jax.ShapeDtypeStruct((hn, s, 1), F32))
    vmem = 4 * _nbytes((s, dk + dv), q.dtype) + 10 * _nbytes((t, t), F32) + 12 * _nbytes((t, V7X_LANES), F32)
    return pl.pallas_call(
        body, name=name, grid=(hn, nt),
        in_specs=in_specs, out_specs=out_specs, out_shape=out_shape,
        compiler_params=_params(("parallel", "arbitrary"), vmem),
    )(*ins)


def _fox_cum(z, bf, *, name):
    s = z.shape[0]
    w = V7X_LANES
    t = _row_tile(s, 512)
    steps = [1 << k for k in range(int(math.log2(t)))]
    cb = SEG["fl"][3] // w

    def body(f_ref, bf_ref, c_ref, car):
        @pl.when(pl.program_id(0) == 0)
        def _():
            car[...] = jnp.zeros_like(car)

        acc = -_softplus(-(f_ref[...] + bf_ref[...]))
        rows = lax.broadcasted_iota(jnp.int32, (t, w), 0)
        for d in steps:
            acc = acc + _shift_down(acc, d, 0.0, rows)
        c_ref[...] = acc + car[0:1, :]
        car[0:1, :] = c_ref[t - 1:t, :]

    return pl.pallas_call(
        body, name=name, grid=(s // t,),
        in_specs=[_rows(t, w, cb), _full((1, w))], out_specs=_rows(t, w),
        out_shape=jax.ShapeDtypeStruct((s, w), F32),
        scratch_shapes=[pltpu.VMEM((V7X_SUBLANES, w), F32)],
        compiler_params=_params(("arbitrary",), 16 * _nbytes((t, w), F32)),
    )(z, bf)


def _fox_cum_bwd(z, bf, dcum, *, name):
    s = z.shape[0]
    w = V7X_LANES
    t = _row_tile(s, 512)
    nt = s // t
    steps = [1 << k for k in range(int(math.log2(t)))]
    cb = SEG["fl"][3] // w

    def body(f_ref, bf_ref, dc_ref, df_ref, dbf_ref, car, tmp):
        @pl.when(pl.program_id(0) == 0)
        def _():
            car[...] = jnp.zeros_like(car)
            dbf_ref[...] = jnp.zeros_like(dbf_ref)

        acc = dc_ref[...]
        rows = lax.broadcasted_iota(jnp.int32, (t, w), 0)
        for d in steps:
            acc = acc + _shift_up(acc, d, 0.0, rows, t)
        dlf = acc + car[0:1, :]
        tmp[...] = dlf
        car[0:1, :] = tmp[0:1, :]
        df = dlf * _sigmoid(-(f_ref[...] + bf_ref[...]))
        df_ref[...] = df.astype(df_ref.dtype)
        dbf_ref[...] += jnp.sum(df, axis=0, keepdims=True)

    rev = lambda cbk: pl.BlockSpec((t, w), lambda i: (nt - 1 - i, cbk))
    return pl.pallas_call(
        body, name=name, grid=(nt,),
        in_specs=[rev(cb), _full((1, w)), rev(0)], out_specs=[rev(0), _full((1, w))],
        out_shape=[jax.ShapeDtypeStruct((s, w), MXU_DTYPE), jax.ShapeDtypeStruct((1, w), F32)],
        scratch_shapes=[pltpu.VMEM((V7X_SUBLANES, w), F32), pltpu.VMEM((t, w), F32)],
        compiler_params=_params(("arbitrary",), 16 * _nbytes((t, w), F32)),
    )(z, bf, dcum)


_GATE_CB = SEG["gate"][3] // D_MODEL


def _merge_fwd(ya, yb, yc, z, gate_b, *, name):
    s = ya.shape[0]
    d = D_MODEL
    t = _tile(s, 256)

    def body(ya_ref, yb_ref, yc_ref, g0_ref, g1_ref, g2_ref, gb_ref, o_ref):
        out = _sigmoid(g0_ref[...] + gb_ref[:, 0:d]) * ya_ref[...]
        out = out + _sigmoid(g1_ref[...] + gb_ref[:, d:2 * d]) * yb_ref[...]
        out = out + _sigmoid(g2_ref[...] + gb_ref[:, 2 * d:3 * d]) * yc_ref[...]
        o_ref[...] = out.astype(o_ref.dtype)

    return pl.pallas_call(
        body, name=name, grid=(s // t,),
        in_specs=[_rows(t, d)] * 3 + [_rows(t, d, _GATE_CB + b) for b in range(3)] + [_full((1, 3 * d))],
        out_specs=_rows(t, d), out_shape=jax.ShapeDtypeStruct((s, d), MXU_DTYPE),
        compiler_params=_params(("parallel",), 20 * _nbytes((t, d), F32)),
    )(ya, yb, yc, z, z, z, gate_b)


def _merge_bwd(dm, ya, yb, yc, z, gate_b, *, name):
    s = ya.shape[0]
    d = D_MODEL
    t = _tile(s, 256)

    def body(dm_ref, ya_ref, yb_ref, yc_ref, g0_ref, g1_ref, g2_ref, gb_ref, da_ref, db_ref, dc_ref, dgl_ref,
             dgb_ref):
        dmv = dm_ref[...]
        parts = []
        for b, (y_ref, g_ref, dy_ref) in enumerate(((ya_ref, g0_ref, da_ref), (yb_ref, g1_ref, db_ref),
                                                    (yc_ref, g2_ref, dc_ref))):
            gate = _sigmoid(g_ref[...] + gb_ref[:, b * d:(b + 1) * d])
            dy_ref[...] = (dmv * gate).astype(dy_ref.dtype)
            dgl = dmv * y_ref[...] * gate * (1.0 - gate)
            dgl_ref[:, b * d:(b + 1) * d] = dgl.astype(dgl_ref.dtype)
            parts.append(jnp.sum(dgl, axis=0, keepdims=True))

        @pl.when(pl.program_id(0) == 0)
        def _():
            for b, part in enumerate(parts):
                dgb_ref[:, b * d:(b + 1) * d] = part

        @pl.when(pl.program_id(0) > 0)
        def _():
            for b, part in enumerate(parts):
                dgb_ref[:, b * d:(b + 1) * d] += part

    return pl.pallas_call(
        body, name=name, grid=(s // t,),
        in_specs=[_rows(t, d)] * 4 + [_rows(t, d, _GATE_CB + b) for b in range(3)] + [_full((1, 3 * d))],
        out_specs=[_rows(t, d)] * 3 + [_rows(t, 3 * d), _full((1, 3 * d))],
        out_shape=[jax.ShapeDtypeStruct((s, d), MXU_DTYPE)] * 3
        + [jax.ShapeDtypeStruct((s, 3 * d), MXU_DTYPE), jax.ShapeDtypeStruct((1, 3 * d), F32)],
        compiler_params=_params(("arbitrary",), 36 * _nbytes((t, d), F32)),
    )(dm, ya, yb, yc, z, z, z, gate_b)


def _swiglu_fwd(hf, *, name):
    s = hf.shape[0]
    t = _tile(s, 256)

    def body(g_ref, u_ref, o_ref):
        gv = g_ref[...]
        o_ref[...] = (gv * _sigmoid(gv) * u_ref[...]).astype(o_ref.dtype)

    return pl.pallas_call(
        body, name=name, grid=(s // t,),
        in_specs=[_rows(t, D_FF, 0), _rows(t, D_FF, 1)], out_specs=_rows(t, D_FF),
        out_shape=jax.ShapeDtypeStruct((s, D_FF), MXU_DTYPE),
        compiler_params=_params(("parallel",), 10 * _nbytes((t, D_FF), F32)),
    )(hf, hf)


def _swiglu_bwd(hf, dact, *, name):
    s = hf.shape[0]
    t = _tile(s, 256)

    def body(g_ref, u_ref, da_ref, o_ref):
        gv = g_ref[...]
        dav = da_ref[...]
        sg = _sigmoid(gv)
        o_ref[:, 0:D_FF] = (dav * u_ref[...] * sg * (1.0 + gv * (1.0 - sg))).astype(o_ref.dtype)
        o_ref[:, D_FF:2 * D_FF] = (dav * gv * sg).astype(o_ref.dtype)

    return pl.pallas_call(
        body, name=name, grid=(s // t,),
        in_specs=[_rows(t, D_FF, 0), _rows(t, D_FF, 1), _rows(t, D_FF)], out_specs=_rows(t, 2 * D_FF),
        out_shape=jax.ShapeDtypeStruct((s, 2 * D_FF), MXU_DTYPE),
        compiler_params=_params(("parallel",), 14 * _nbytes((t, D_FF), F32)),
    )(hf, hf, dact)


def _ple_fwd(x, lg, pe, *, name):
    s, d = x.shape
    t = _tile(s, 512)

    def body(x_ref, lg_ref, pe_ref, o_ref):
        o_ref[...] = x_ref[...] + _sigmoid(lg_ref[...]) * pe_ref[...]

    return pl.pallas_call(
        body, name=name, grid=(s // t,),
        in_specs=[_rows(t, d)] * 3, out_specs=_rows(t, d), out_shape=jax.ShapeDtypeStruct((s, d), F32),
        compiler_params=_params(("parallel",), 12 * _nbytes((t, d), F32)),
    )(x, lg, pe)


def _ple_bwd(dx, lg, pe, *, name):
    s, d = dx.shape
    t = _tile(s, 512)

    def body(dx_ref, lg_ref, pe_ref, dpe_ref, dlg_ref):
        dxv = dx_ref[...]
        sg = _sigmoid(lg_ref[...])
        dpe_ref[...] = (dxv * sg).astype(dpe_ref.dtype)
        dlg_ref[...] = (dxv * pe_ref[...] * sg * (1.0 - sg)).astype(dlg_ref.dtype)

    return pl.pallas_call(
        body, name=name, grid=(s // t,),
        in_specs=[_rows(t, d)] * 3, out_specs=[_rows(t, d)] * 2,
        out_shape=[jax.ShapeDtypeStruct((s, d), MXU_DTYPE)] * 2,
        compiler_params=_params(("parallel",), 14 * _nbytes((t, d), F32)),
    )(dx, lg, pe)


def _adamw(parts, w, m, v, *, name):
    rows, lanes = w.shape
    t = math.gcd(rows, 160)
    assert rows % t == 0 and t % V7X_SUBLANES == 0
    c1 = 1.0 / (1.0 - ADAM_B1 ** ADAM_STEP)
    c2 = 1.0 / (1.0 - ADAM_B2 ** ADAM_STEP)

    def body(p_ref, w_ref, m_ref, v_ref, g_ref, d_ref, nm_ref, nv_ref):
        g = p_ref[0].astype(F32)
        for j in range(1, N_DEV):
            g = g + p_ref[j].astype(F32)
        m2 = ADAM_B1 * m_ref[...] + (1.0 - ADAM_B1) * g
        v2 = ADAM_B2 * v_ref[...] + (1.0 - ADAM_B2) * (g * g)
        g_ref[...] = g
        nm_ref[...] = m2
        nv_ref[...] = v2
        d_ref[...] = -ADAM_LR * ((m2 * c1) / (jnp.sqrt(v2 * c2) + ADAM_EPS) + ADAM_WD * w_ref[...])

    blk = _rows(t, lanes)
    return pl.pallas_call(
        body, name=name, grid=(rows // t,),
        in_specs=[pl.BlockSpec((N_DEV, t, lanes), lambda i: (0, i, 0)), blk, blk, blk], out_specs=[blk] * 4,
        out_shape=[jax.ShapeDtypeStruct((rows, lanes), F32)] * 4,
        compiler_params=_params(("parallel",), 40 * _nbytes((t, lanes), F32)),
    )(parts, w, m, v)


def _mesh_pos():
    return lax.axis_index("x"), lax.axis_index("y"), lax.axis_index("c")


def _all_gather(blk, *, name):
    r, c_dim = blk.shape

    def body(x_ref, out_ref, send_sems, recv_sems, local_sem):
        x, y, c = _mesh_pos()
        me, sibling = (x, y, c), (x, y, 1 - c)
        chips = [(1 - x, y), (x, 1 - y), (1 - x, 1 - y)]

        def slot(px, py, pc):
            return out_ref.at[4 * px + 2 * py + pc]

        def copy(k, block, to, src=None):
            return pltpu.make_async_remote_copy(
                src_ref=slot(*block) if src is None else src, dst_ref=slot(*block),
                send_sem=send_sems.at[k], recv_sem=recv_sems.at[k],
                device_id=to, device_id_type=pl.DeviceIdType.MESH)

        mine = pltpu.make_async_copy(x_ref, slot(*me), local_sem)
        mine.start()
        first = [copy(0, me, sibling, src=x_ref)]
        first += [copy(1 + j, me, (*chip, c), src=x_ref) for j, chip in enumerate(chips)]
        for cp in first:
            cp.start()
        passed = [copy(4 + j, (*chip, c), sibling) for j, chip in enumerate(chips)]
        for j, chip in enumerate(chips):
            copy(1 + j, (*chip, c), me).wait_recv()
            passed[j].start()
        copy(0, sibling, me).wait_recv()
        for j, chip in enumerate(chips):
            copy(4 + j, (*chip, 1 - c), me).wait_recv()
        for cp in first + passed:
            cp.wait_send()
        mine.wait()

    return pl.pallas_call(
        body, name=name,
        out_shape=jax.ShapeDtypeStruct((N_DEV, r, c_dim), blk.dtype),
        in_specs=[pl.BlockSpec(memory_space=pl.ANY)], out_specs=pl.BlockSpec(memory_space=pl.ANY),
        scratch_shapes=[pltpu.SemaphoreType.DMA((7,)), pltpu.SemaphoreType.DMA((7,)), pltpu.SemaphoreType.DMA],
    )(blk)


def _all_to_all(pay, *, name):
    _, r, c_dim = pay.shape

    def body(in_ref, out_ref, send_sems, recv_sems, local_sem):
        x, y, c = _mesh_pos()
        me = 4 * x + 2 * y + c
        local = pltpu.make_async_copy(in_ref.at[me], out_ref.at[me], local_sem)
        local.start()
        copies = []
        for k in range(1, N_DEV):
            px = 1 - x if k & 4 else x
            py = 1 - y if k & 2 else y
            pc = 1 - c if k & 1 else c
            copies.append(pltpu.make_async_remote_copy(
                src_ref=in_ref.at[4 * px + 2 * py + pc], dst_ref=out_ref.at[me],
                send_sem=send_sems.at[k - 1], recv_sem=recv_sems.at[k - 1],
                device_id=(px, py, pc), device_id_type=pl.DeviceIdType.MESH))
        for cp in copies:
            cp.start()
        for cp in copies:
            cp.wait()
        local.wait()

    return pl.pallas_call(
        body, name=name,
        out_shape=jax.ShapeDtypeStruct((N_DEV, r, c_dim), pay.dtype),
        in_specs=[pl.BlockSpec(memory_space=pl.ANY)], out_specs=pl.BlockSpec(memory_space=pl.ANY),
        scratch_shapes=[pltpu.SemaphoreType.DMA((7,)), pltpu.SemaphoreType.DMA((7,)), pltpu.SemaphoreType.DMA],
    )(pay)


def _flat_rows(parts, row_multiple):
    flat = jnp.concatenate([p.reshape(-1) for p in parts])
    chunk = PAYLOAD_LANES * row_multiple
    total = -(-flat.shape[0] // chunk) * chunk
    return jnp.pad(flat, (0, total - flat.shape[0])).reshape(total // PAYLOAD_LANES, PAYLOAD_LANES)


def _split_flat(flat, shapes):
    out, off = [], 0
    flat = flat.reshape(-1)
    for shp in shapes:
        n = math.prod(shp)
        out.append(flat[off:off + n].reshape(shp))
        off += n
    return out


def _pad_w_in(w):
    pieces, cursor = [], 0
    for _, off, width, pad_off, _ in SEGS:
        if pad_off > cursor:
            pieces.append(jnp.zeros(w.shape[:-1] + (pad_off - cursor,), w.dtype))
        pieces.append(w[..., off:off + width])
        cursor = pad_off + width
    pieces.append(jnp.zeros(w.shape[:-1] + (D_IN_PAD - cursor,), w.dtype))
    return jnp.concatenate(pieces, axis=-1)


def _unpad_w_in(w):
    return jnp.concatenate([w[..., pad_off:pad_off + width] for _, _, width, pad_off, _ in SEGS], axis=-1)


def _heads(a, hd):
    return a.reshape(a.shape[0], HEADS, hd).transpose(1, 0, 2)


def _unheads(a):
    return a.transpose(1, 0, 2).reshape(a.shape[1], -1)


def _block_diag(w):
    eye = jnp.eye(LRU_HEADS, dtype=w.dtype)
    return (eye[:, None, :, None] * w[:, :, None, :]).reshape(LRU_WIDTH, LRU_WIDTH)


def _diag_blocks(w):
    w4 = w.reshape(LRU_HEADS, LRU_HEAD_DIM, LRU_HEADS, LRU_HEAD_DIM)
    return jnp.stack([w4[h, :, h, :] for h in range(LRU_HEADS)])


def _lane_pad(a, width):
    return jnp.pad(a, ((0, 0), (0, width - a.shape[-1])))


def _layer_fwd(x, p_i, wts, tabs, tag):
    n = functools.partial(lambda base, t=tag: f"{base}_{t}")
    sv = {"x": x}
    n1 = _rms_fwd(x, wts["mix_norm"], width=D_MODEL, name=n("mix_norm_fwd"))
    z = _mm(n1, wts["w_in"], name=n("w_in_fwd"))
    sv.update(n1=n1, z=z)

    ya_pre, hseq = _lru_fwd(z, wts["conv_w"], wts["conv_b"], wts["lru_wa"], wts["lru_ba"], wts["lru_wx"],
                            wts["lru_bx"], wts["lru_lambda"], name=n("lru_fwd"))
    ya = _mm(ya_pre, wts["w_br_a"], name=n("br_a_fwd"))
    sv.update(ya_pre=ya_pre, hseq=hseq, ya=ya)

    cqn = _rms_fwd(z, wts["mla_q_norm"], width=MLA_Q_LORA, cb=SEG["cq"][3] // MLA_Q_LORA, name=n("q_norm_fwd"))
    ckvn = _rms_fwd(z, wts["mla_kv_norm"], width=MLA_KV_LORA, cb=SEG["ckv"][3] // MLA_KV_LORA,
                    name=n("kv_norm_fwd"))
    qp = _mm(cqn, wts["mla_wuq"], name=n("wuq_fwd"))
    kv = _mm(ckvn, wts["mla_wukv"], out_dtype=MXU_DTYPE, name=n("wukv_fwd"))
    q_rope = _rope(qp, tabs["cos256"], tabs["sin256"], width=256, cb=2, out_dtype=MXU_DTYPE, name=n("q_rope_fwd"))
    k_rope = _rope(z, tabs["cos128"], tabs["sin128"], width=V7X_LANES, cb=SEG["kr"][3] // V7X_LANES,
                   out_dtype=MXU_DTYPE, name=n("k_rope_fwd"))
    s = x.shape[0]
    q_full = jnp.concatenate([qp[:, :512].astype(MXU_DTYPE).reshape(s, HEADS, MLA_NOPE),
                              q_rope.reshape(s, HEADS, MLA_ROPE)], axis=-1).transpose(1, 0, 2)
    k_full = jnp.concatenate([kv[:, :512].reshape(s, HEADS, MLA_NOPE),
                              jnp.broadcast_to(k_rope[:, None, :MLA_ROPE], (s, HEADS, MLA_ROPE))],
                             axis=-1).transpose(1, 0, 2)
    v_mla = _heads(kv[:, 512:], MLA_V)
    o_b, lse_b = _attn_fwd(q_full, k_full, v_mla, None, None, scale=(MLA_NOPE + MLA_ROPE) ** -0.5, unit=CHUNK,
                           name=n("mla_attn_fwd"))
    ob_flat = _unheads(o_b)
    yb = _mm(ob_flat, wts["w_br_b"], name=n("br_b_fwd"))
    sv.update(cqn=cqn, ckvn=ckvn, q_full=q_full, k_full=k_full, v_mla=v_mla, o_b=o_b, lse_b=lse_b,
              ob_flat=ob_flat, yb=yb)

    cum = _fox_cum(z, wts["fox_bf"], name=n("fox_cum_fwd"))
    cum_h = cum[:, :HEADS].T
    cq, ck = cum_h[:, :, None], cum_h[:, None, :]
    fq = _heads(z[:, SEG["fq"][3]:SEG["fq"][3] + FOX_WIDTH].astype(MXU_DTYPE), FOX_HEAD_DIM)
    fk = _heads(z[:, SEG["fk"][3]:SEG["fk"][3] + FOX_WIDTH].astype(MXU_DTYPE), FOX_HEAD_DIM)
    fv = _heads(z[:, SEG["fv"][3]:SEG["fv"][3] + FOX_WIDTH].astype(MXU_DTYPE), FOX_HEAD_DIM)
    o_c, lse_c = _attn_fwd(fq, fk, fv, cq, ck, scale=FOX_HEAD_DIM ** -0.5, unit=1, name=n("fox_attn_fwd"))
    oc_flat = _unheads(o_c)
    yc = _mm(oc_flat, wts["w_br_c"], name=n("br_c_fwd"))
    sv.update(cq=cq, ck=ck, fq=fq, fk=fk, fv=fv, o_c=o_c, lse_c=lse_c, oc_flat=oc_flat, yc=yc)

    merged = _merge_fwd(ya, yb, yc, z, wts["gate_b"], name=n("merge_fwd"))
    x1 = _mm(merged, wts["w_o"], res=x, name=n("w_o_fwd"))
    n2 = _rms_fwd(x1, wts["ffn_norm"], width=D_MODEL, name=n("ffn_norm_fwd"))
    hf = _mm(n2, wts["w_gate_up"], name=n("gate_up_fwd"))
    act = _swiglu_fwd(hf, name=n("swiglu_fwd"))
    x2 = _mm(act, wts["w_down"], res=x1, name=n("down_fwd"))
    n3 = _rms_fwd(x2, wts["ple_norm"], width=D_MODEL, name=n("ple_norm_fwd"))
    lg = _mm(n3, wts["w_ple_gate"], name=n("ple_gate_fwd"))
    pe = _mm(p_i, wts["w_ple"], name=n("ple_fwd_mm"))
    x3 = _ple_fwd(x2, lg, pe, name=n("ple_fwd"))
    sv.update(merged=merged, x1=x1, n2=n2, hf=hf, act=act, x2=x2, n3=n3, lg=lg, pe=pe, p_i=p_i)
    return x3, sv


def _layer_bwd(dx3, sv, wts, tabs, tag):
    n = functools.partial(lambda base, t=tag: f"{base}_{t}")
    gr = {}
    z = sv["z"]
    s = z.shape[0]

    dpe, dlg = _ple_bwd(dx3, sv["lg"], sv["pe"], name=n("ple_bwd"))
    gr["w_ple"] = _mm(sv["p_i"], dpe, ta=True, name=n("ple_dw"))
    gr["w_ple_gate"] = _mm(sv["n3"], dlg, ta=True, name=n("ple_gate_dw"))
    dn3 = _mm(dlg, wts["w_ple_gate"], tb=True, name=n("ple_gate_dx"))
    dx2, gr["ple_norm"] = _rms_bwd(sv["x2"], wts["ple_norm"], dn3, width=D_MODEL, res=dx3, name=n("ple_norm_bwd"))

    dact = _mm(dx2, wts["w_down"], tb=True, name=n("down_dx"))
    gr["w_down"] = _mm(sv["act"], dx2, ta=True, name=n("down_dw"))
    dhf = _swiglu_bwd(sv["hf"], dact, name=n("swiglu_bwd"))
    gr["w_gate_up"] = _mm(sv["n2"], dhf, ta=True, name=n("gate_up_dw"))
    dn2 = _mm(dhf, wts["w_gate_up"], tb=True, name=n("gate_up_dx"))
    dx1, gr["ffn_norm"] = _rms_bwd(sv["x1"], wts["ffn_norm"], dn2, width=D_MODEL, res=dx2, name=n("ffn_norm_bwd"))

    dmerged = _mm(dx1, wts["w_o"], tb=True, name=n("w_o_dx"))
    gr["w_o"] = _mm(sv["merged"], dx1, ta=True, name=n("w_o_dw"))
    dya, dyb, dyc, dgl, gr["gate_b"] = _merge_bwd(dmerged, sv["ya"], sv["yb"], sv["yc"], z, wts["gate_b"],
                                                  name=n("merge_bwd"))
    gr["w_br_a"] = _mm(sv["ya_pre"], dya, ta=True, name=n("br_a_dw"))
    gr["w_br_b"] = _mm(sv["ob_flat"], dyb, ta=True, name=n("br_b_dw"))
    gr["w_br_c"] = _mm(sv["oc_flat"], dyc, ta=True, name=n("br_c_dw"))
    dya_pre = _mm(dya, wts["w_br_a"], tb=True, name=n("br_a_dx"))
    dob = _mm(dyb, wts["w_br_b"], tb=True, out_dtype=MXU_DTYPE, name=n("br_b_dx"))
    doc = _mm(dyc, wts["w_br_c"], tb=True, out_dtype=MXU_DTYPE, name=n("br_c_dx"))

    (dz_a, gr["conv_w"], gr["conv_b"], dwa, gr["lru_ba"], dwx, gr["lru_bx"], gr["lru_lambda"]) = _lru_bwd(
        z, sv["hseq"], dya_pre, wts["conv_w"], wts["conv_b"], wts["lru_wa"], wts["lru_ba"], wts["lru_wx"],
        wts["lru_bx"], wts["lru_lambda"], name=n("lru_bwd"))
    gr["lru_wa"], gr["lru_wx"] = _diag_blocks(dwa), _diag_blocks(dwx)

    scale_b = (MLA_NOPE + MLA_ROPE) ** -0.5
    do_b = _heads(dob, MLA_V)
    dq_full, delta_b = _attn_bwd_q(sv["q_full"], sv["k_full"], sv["v_mla"], do_b, sv["lse_b"], None, None,
                                   scale=scale_b, unit=CHUNK, name=n("mla_attn_dq"))
    dk_full, dv_mla = _attn_bwd_kv(sv["q_full"], sv["k_full"], sv["v_mla"], do_b,
                                   sv["lse_b"].reshape(HEADS, 1, s), delta_b.reshape(HEADS, 1, s), None, None,
                                   scale=scale_b, unit=CHUNK, name=n("mla_attn_dkv"))
    dq_t = dq_full.transpose(1, 0, 2)
    dk_t = dk_full.transpose(1, 0, 2)
    dq_rope = _rope(dq_t[:, :, MLA_NOPE:].reshape(s, HEADS * MLA_ROPE), tabs["cos256"], -tabs["sin256"], width=256,
                    cb=0, out_dtype=MXU_DTYPE, name=n("q_rope_bwd"))
    dk_rope = _rope(dk_t[:, :, MLA_NOPE:].reshape(s, HEADS * MLA_ROPE), tabs["cos128"], -tabs["sin128"], width=256,
                    cb=0, out_dtype=MXU_DTYPE, sum_heads=True, name=n("k_rope_bwd"))
    dqp = jnp.concatenate([dq_t[:, :, :MLA_NOPE].reshape(s, 512).astype(MXU_DTYPE), dq_rope], axis=-1)
    dkv = jnp.concatenate([dk_t[:, :, :MLA_NOPE].reshape(s, 512), _unheads(dv_mla)], axis=-1).astype(MXU_DTYPE)
    gr["mla_wuq"] = _mm(sv["cqn"], dqp, ta=True, name=n("wuq_dw"))
    gr["mla_wukv"] = _mm(sv["ckvn"], dkv, ta=True, name=n("wukv_dw"))
    dcqn = _mm(dqp, wts["mla_wuq"], tb=True, name=n("wuq_dx"))
    dckvn = _mm(dkv, wts["mla_wukv"], tb=True, name=n("wukv_dx"))
    dcq, gr["mla_q_norm"] = _rms_bwd(z, wts["mla_q_norm"], dcqn, width=MLA_Q_LORA, cb=SEG["cq"][3] // MLA_Q_LORA,
                                     out_dtype=MXU_DTYPE, name=n("q_norm_bwd"))
    dckv, gr["mla_kv_norm"] = _rms_bwd(z, wts["mla_kv_norm"], dckvn, width=MLA_KV_LORA,
                                       cb=SEG["ckv"][3] // MLA_KV_LORA, out_dtype=MXU_DTYPE, name=n("kv_norm_bwd"))

    scale_c = FOX_HEAD_DIM ** -0.5
    do_c = _heads(doc, FOX_HEAD_DIM)
    dfq, delta_c = _attn_bwd_q(sv["fq"], sv["fk"], sv["fv"], do_c, sv["lse_c"], sv["cq"], sv["ck"],
                               scale=scale_c, unit=1, name=n("fox_attn_dq"))
    dfk, dfv, dcum = _attn_bwd_kv(sv["fq"], sv["fk"], sv["fv"], do_c, sv["lse_c"].reshape(HEADS, 1, s),
                                  delta_c.reshape(HEADS, 1, s), sv["ck"], sv["cq"],
                                  scale=scale_c, unit=1, name=n("fox_attn_dkv"))
    dcum_rows = _lane_pad(dcum.reshape(HEADS, s).T, V7X_LANES)
    dfl, dbf = _fox_cum_bwd(z, wts["fox_bf"], dcum_rows, name=n("fox_cum_bwd"))
    gr["fox_bf"] = dbf[:, :HEADS]

    zero = lambda width: jnp.zeros((s, width), MXU_DTYPE)
    dz = jnp.concatenate([dz_a, zero(128), dcq, dckv, dk_rope, zero(128),
                          _unheads(dfq).astype(MXU_DTYPE), _unheads(dfk).astype(MXU_DTYPE),
                          _unheads(dfv).astype(MXU_DTYPE), dfl, zero(384), dgl], axis=-1)
    gr["w_in"] = _mm(sv["n1"], dz, ta=True, name=n("w_in_dw"))
    dn1 = _mm(dz, wts["w_in"], tb=True, name=n("w_in_dx"))
    dx, gr["mix_norm"] = _rms_bwd(sv["x"], wts["mix_norm"], dn1, width=D_MODEL, res=dx1, name=n("mix_norm_bwd"))
    return dx, gr


def _rope_tables(s):
    pos = jnp.arange(s, dtype=F32)
    inv_freq = ROPE_BASE ** (-jnp.arange(0, MLA_ROPE, 2, dtype=F32) / MLA_ROPE)
    ang = pos[:, None] * inv_freq[None, :]
    cos, sin = jnp.cos(ang), jnp.sin(ang)
    cos32 = jnp.concatenate([cos, cos], axis=-1)
    sin32 = jnp.concatenate([-sin, sin], axis=-1)
    return {"cos256": jnp.tile(cos32, (1, 8)), "sin256": jnp.tile(sin32, (1, 8)),
            "cos128": jnp.tile(cos32, (1, 4)), "sin128": jnp.tile(sin32, (1, 4))}


def _gather_weights(shards):
    names = [nm for nm, _ in SHARDED if nm != "conv_w"]
    pay = _flat_rows([shards[nm].astype(MXU_DTYPE) for nm in names], 16)
    got = _all_gather(pay, name="weights_all_gather").reshape(N_DEV, -1)
    full, off = {}, 0
    axes = dict(SHARDED)
    for nm in names:
        shp = shards[nm].shape
        cnt = math.prod(shp)
        blk = got[:, off:off + cnt].reshape((N_DEV,) + shp)
        off += cnt
        if axes[nm] == 2:
            full[nm] = blk.transpose(1, 2, 0, 3).reshape(shp[0], shp[1], N_DEV * shp[2])
        else:
            full[nm] = blk.transpose(1, 0, 2, 3).reshape(shp[0], N_DEV * shp[1], shp[2])
    cw = _all_gather(_flat_rows([shards["conv_w"]], 8), name="conv_w_all_gather").reshape(N_DEV, -1)
    shp = shards["conv_w"].shape
    full["conv_w"] = cw[:, :math.prod(shp)].reshape((N_DEV,) + shp).transpose(1, 2, 0, 3).reshape(
        shp[0], shp[1], N_DEV * shp[2])
    return full


def _to_dest_major(g, axis):
    d0, r, c = g.shape
    if axis == 2:
        return g.reshape(d0, r, N_DEV, c // N_DEV).transpose(2, 0, 1, 3).reshape(N_DEV, -1)
    return g.reshape(d0, N_DEV, r // N_DEV, c).transpose(1, 0, 2, 3).reshape(N_DEV, -1)


def kernel(x, p, mix_norm, w_in, gate_b, conv_w, conv_b, lru_wa, lru_ba, lru_wx, lru_bx, lru_lambda, mla_q_norm, mla_wuq, mla_kv_norm, mla_wukv, fox_bf, w_br_a, w_br_b, w_br_c, w_o, ffn_norm, w_gate_up, w_down, ple_norm, w_ple_gate, w_ple, final_norm, loss_target, m_mix_norm, m_w_in, m_gate_b, m_conv_w, m_conv_b, m_lru_wa, m_lru_ba, m_lru_wx, m_lru_bx, m_lru_lambda, m_mla_q_norm, m_mla_wuq, m_mla_kv_norm, m_mla_wukv, m_fox_bf, m_w_br_a, m_w_br_b, m_w_br_c, m_w_o, m_ffn_norm, m_w_gate_up, m_w_down, m_ple_norm, m_w_ple_gate, m_w_ple, m_final_norm, v_mix_norm, v_w_in, v_gate_b, v_conv_w, v_conv_b, v_lru_wa, v_lru_ba, v_lru_wx, v_lru_bx, v_lru_lambda, v_mla_q_norm, v_mla_wuq, v_mla_kv_norm, v_mla_wukv, v_fox_bf, v_w_br_a, v_w_br_b, v_w_br_c, v_w_o, v_ffn_norm, v_w_gate_up, v_w_down, v_ple_norm, v_w_ple_gate, v_w_ple, v_final_norm):
    given = dict(locals())
    w_loc = {nm: given[nm] for nm in WEIGHTS}
    m_loc = {nm: given["m_" + nm] for nm in WEIGHTS}
    v_loc = {nm: given["v_" + nm] for nm in WEIGHTS}
    xs = x[0]
    s = xs.shape[0]
    tabs = _rope_tables(s)

    full = _gather_weights({nm: w_loc[nm] for nm, _ in SHARDED})
    full["w_in"] = _pad_w_in(full["w_in"])
    wq = full["mla_wuq"].reshape(DEPTH, MLA_Q_LORA, HEADS, MLA_NOPE + MLA_ROPE)
    full["mla_wuq"] = jnp.concatenate([wq[..., :MLA_NOPE].reshape(DEPTH, MLA_Q_LORA, -1),
                                       wq[..., MLA_NOPE:].reshape(DEPTH, MLA_Q_LORA, -1)], axis=-1)
    wkv = full["mla_wukv"].reshape(DEPTH, MLA_KV_LORA, HEADS, MLA_NOPE + MLA_V)
    full["mla_wukv"] = jnp.concatenate([wkv[..., :MLA_NOPE].reshape(DEPTH, MLA_KV_LORA, -1),
                                        wkv[..., MLA_NOPE:].reshape(DEPTH, MLA_KV_LORA, -1)], axis=-1)

    def layer_weights(i):
        wts = {nm: full[nm][i] for nm, _ in SHARDED}
        for nm in ("mix_norm", "gate_b", "conv_b", "lru_ba", "lru_bx", "lru_lambda", "mla_q_norm", "mla_kv_norm",
                   "ffn_norm", "ple_norm"):
            wts[nm] = w_loc[nm][i][None, :]
        wts["fox_bf"] = _lane_pad(w_loc["fox_bf"][i][None, :], V7X_LANES)
        wts["lru_wa"] = _block_diag(w_loc["lru_wa"][i]).astype(MXU_DTYPE)
        wts["lru_wx"] = _block_diag(w_loc["lru_wx"][i]).astype(MXU_DTYPE)
        return wts

    layers = [layer_weights(i) for i in range(DEPTH)]

    h = xs
    saved = []
    for i in range(DEPTH):
        h, sv = _layer_fwd(h, p[i, 0].astype(MXU_DTYPE), layers[i], tabs, f"l{i}")
        saved.append(sv)
    loss_blk, dh, dg_final = _final_loss(h, w_loc["final_norm"][None, :], loss_target[0], name="final_loss")
    loss = lax.psum(loss_blk[0, 0], ("x", "y", "c"))

    grads = [None] * DEPTH
    for i in reversed(range(DEPTH)):
        dh, grads[i] = _layer_bwd(dh, saved[i], layers[i], tabs, f"l{i}")
    grad_x = dh[None]

    def stacked(nm):
        return jnp.stack([grads[i][nm] for i in range(DEPTH)])

    gfull = {}
    for nm, _ in SHARDED:
        gfull[nm] = stacked(nm)
    gfull["w_in"] = _unpad_w_in(gfull["w_in"])
    gq = gfull["mla_wuq"]
    gfull["mla_wuq"] = jnp.concatenate(
        [gq[..., :512].reshape(DEPTH, MLA_Q_LORA, HEADS, MLA_NOPE),
         gq[..., 512:].reshape(DEPTH, MLA_Q_LORA, HEADS, MLA_ROPE)], axis=-1).reshape(DEPTH, MLA_Q_LORA, -1)
    gkv = gfull["mla_wukv"]
    gfull["mla_wukv"] = jnp.concatenate(
        [gkv[..., :512].reshape(DEPTH, MLA_KV_LORA, HEADS, MLA_NOPE),
         gkv[..., 512:].reshape(DEPTH, MLA_KV_LORA, HEADS, MLA_V)], axis=-1).reshape(DEPTH, MLA_KV_LORA, -1)

    pay = jnp.concatenate([_to_dest_major(gfull[nm], ax) for nm, ax in SHARDED], axis=1)
    chunk = PAYLOAD_LANES * 16
    total = -(-pay.shape[1] // chunk) * chunk
    pay = jnp.pad(pay, ((0, 0), (0, total - pay.shape[1]))).astype(MXU_DTYPE).reshape(N_DEV, -1, PAYLOAD_LANES)
    parts = _all_to_all(pay, name="grads_all_to_all")
    names_s = [nm for nm, _ in SHARDED]
    shapes_s = [w_loc[nm].shape for nm in names_s]
    outs_s = _adamw(parts, _flat_rows([w_loc[nm] for nm in names_s], 16),
                    _flat_rows([m_loc[nm] for nm in names_s], 16),
                    _flat_rows([v_loc[nm] for nm in names_s], 16), name="adamw_sharded")
    res_s = [dict(zip(names_s, _split_flat(o, shapes_s))) for o in outs_s]

    small = {nm: stacked(nm) for nm in REPLICATED if nm != "final_norm"}
    small["final_norm"] = dg_final
    names_r = list(REPLICATED)
    shapes_r = [w_loc[nm].shape for nm in names_r]
    parts_r = _all_gather(_flat_rows([small[nm] for nm in names_r], 8), name="small_grads_all_gather")
    outs_r = _adamw(parts_r, _flat_rows([w_loc[nm] for nm in names_r], 8),
                    _flat_rows([m_loc[nm] for nm in names_r], 8),
                    _flat_rows([v_loc[nm] for nm in names_r], 8), name="adamw_replicated")
    res_r = [dict(zip(names_r, _split_flat(o, shapes_r))) for o in outs_r]

    out = [loss, grad_x]
    for kind in range(4):
        for nm in WEIGHTS:
            out.append(res_s[kind][nm] if nm in res_s[kind] else res_r[kind][nm])
    return tuple(out)
```

```python
import functools
import math

import jax
import jax.numpy as jnp
from jax import lax
from jax.experimental import pallas as pl
from jax.experimental.pallas import tpu as pltpu

F32 = jnp.float32
BF16 = jnp.bfloat16
MXU_DTYPE = jnp.bfloat16

D_MODEL = 1024
DEPTH = 2
CHUNK = 64
EPS = 1e-6
NEG_INF = -1e30
LRU_WIDTH = 512
LRU_HEADS = 8
LRU_HEAD_DIM = 64
CONV_WIDTH = 4
LRU_C = 8.0
HEADS = 8
MLA_Q_LORA = 384
MLA_KV_LORA = 256
MLA_NOPE = 64
MLA_ROPE = 32
MLA_V = 64
ROPE_BASE = 10000.0
FOX_HEAD_DIM = 64
FOX_WIDTH = 512
D_FF = 2816
PLE_DIM = 256
D_IN = 6312
ADAM_LR = 0.001
ADAM_B1 = 0.9
ADAM_B2 = 0.999
ADAM_EPS = 1e-08
ADAM_WD = 0.01
ADAM_STEP = 10

V7X_VMEM_BYTES = 64 * 1024 * 1024
V7X_LANES = 128
V7X_SUBLANES = 8
VMEM_LIMIT_CAP = 56 * 1024 * 1024
N_DEV = 8

SEGS = (
    ("u", 0, 512, 0, 512),
    ("ug", 512, 512, 512, 512),
    ("cq", 1024, 384, 1152, 384),
    ("ckv", 1408, 256, 1536, 256),
    ("kr", 1664, 32, 1792, 128),
    ("fq", 1696, 512, 2048, 512),
    ("fk", 2208, 512, 2560, 512),
    ("fv", 2720, 512, 3072, 512),
    ("fl", 3232, 8, 3584, 128),
    ("gate", 3240, 3072, 4096, 1024),
)
D_IN_PAD = 7168
SEG = {s[0]: s for s in SEGS}

SHARDED = (("w_in", 2), ("mla_wuq", 2), ("mla_wukv", 2), ("w_br_a", 2), ("w_br_b", 2), ("w_br_c", 2),
           ("w_o", 1), ("w_gate_up", 2), ("w_down", 1), ("w_ple_gate", 1), ("w_ple", 2), ("conv_w", 2))
REPLICATED = ("mix_norm", "gate_b", "conv_b", "lru_wa", "lru_ba", "lru_wx", "lru_bx", "lru_lambda",
              "mla_q_norm", "mla_kv_norm", "fox_bf", "ffn_norm", "ple_norm", "final_norm")
WEIGHTS = ("mix_norm", "w_in", "gate_b", "conv_w", "conv_b", "lru_wa", "lru_ba", "lru_wx", "lru_bx",
           "lru_lambda", "mla_q_norm", "mla_wuq", "mla_kv_norm", "mla_wukv", "fox_bf", "w_br_a", "w_br_b",
           "w_br_c", "w_o", "ffn_norm", "w_gate_up", "w_down", "ple_norm", "w_ple_gate", "w_ple", "final_norm")
PAYLOAD_LANES = 1024


def _tile(n, cap=1024):
    best = None
    for t in range(V7X_LANES, min(n, cap) + 1, V7X_LANES):
        if n % t == 0:
            best = t
    return best if best is not None else n


def _row_tile(s, pref):
    t = min(pref, s // 2)
    assert s % t == 0 and t % V7X_SUBLANES == 0
    return t


def _nbytes(shape, dtype):
    return math.prod(shape) * jnp.dtype(dtype).itemsize


def _params(sem, vmem_bytes):
    limit = int(min(VMEM_LIMIT_CAP, max(16 * 1024 * 1024, vmem_bytes)))
    return pltpu.CompilerParams(dimension_semantics=sem, vmem_limit_bytes=limit)


def _full(shape):
    return pl.BlockSpec(shape, lambda *_: (0,) * len(shape))


def _rows(t, w, cb=0):
    return pl.BlockSpec((t, w), lambda i: (i, cb))


def _mxu(v):
    return v.astype(MXU_DTYPE)


def _dot(a, b):
    return lax.dot_general(_mxu(a), _mxu(b), (((1,), (0,)), ((), ())), preferred_element_type=F32)


def _dot_nt(a, b):
    return lax.dot_general(_mxu(a), _mxu(b), (((1,), (1,)), ((), ())), preferred_element_type=F32)


def _dot_tn(a, b):
    return lax.dot_general(_mxu(a), _mxu(b), (((0,), (0,)), ((), ())), preferred_element_type=F32)


def _sigmoid(v):
    return 1.0 / (1.0 + jnp.exp(-v))


def _softplus(v):
    return jnp.maximum(v, 0.0) + jnp.log(1.0 + jnp.exp(-jnp.abs(v)))


def _neg_expm1(v):
    series = -v * (1.0 + v * (0.5 + v * (1.0 / 6.0 + v * (1.0 / 24.0))))
    return jnp.where(v > -0.03, series, 1.0 - jnp.exp(v))


_GELU_C = math.sqrt(2.0 / math.pi)
_GELU_A = 0.044715


def _gelu(v):
    t = jnp.tanh(_GELU_C * (v + _GELU_A * v * v * v))
    return 0.5 * v * (1.0 + t)


def _gelu_grad(v):
    t = jnp.tanh(_GELU_C * (v + _GELU_A * v * v * v))
    return 0.5 * (1.0 + t) + 0.5 * v * (1.0 - t * t) * _GELU_C * (1.0 + 3.0 * _GELU_A * v * v)


def _mm(a, b, *, ta=False, tb=False, out_dtype=F32, res=None, also_mxu=False, name):
    k_dim, m_dim = (a.shape[0], a.shape[1]) if ta else (a.shape[1], a.shape[0])
    n_dim = b.shape[0] if tb else b.shape[1]
    assert (b.shape[1] if tb else b.shape[0]) == k_dim
    tm, tn, tk = _tile(m_dim), _tile(n_dim, 1408), _tile(k_dim, 1408)
    nk = k_dim // tk
    a_spec = pl.BlockSpec((tk, tm), lambda i, j, k: (k, i)) if ta else pl.BlockSpec((tm, tk), lambda i, j, k: (i, k))
    b_spec = pl.BlockSpec((tn, tk), lambda i, j, k: (j, k)) if tb else pl.BlockSpec((tk, tn), lambda i, j, k: (k, j))
    o_spec = pl.BlockSpec((tm, tn), lambda i, j, k: (i, j))
    has_res = res is not None

    def body(*refs):
        a_ref, b_ref = refs[0], refs[1]
        res_ref = refs[2] if has_res else None
        o_ref = refs[3] if has_res else refs[2]
        o2_ref = refs[-2] if also_mxu else None
        acc_ref = refs[-1]
        k = pl.program_id(2)
        if ta:
            part = _dot_tn(a_ref[...], b_ref[...])
        elif tb:
            part = _dot_nt(a_ref[...], b_ref[...])
        else:
            part = _dot(a_ref[...], b_ref[...])

        def finish(total):
            if has_res:
                total = total + res_ref[...].astype(F32)
            o_ref[...] = total.astype(o_ref.dtype)
            if also_mxu:
                o2_ref[...] = total.astype(o2_ref.dtype)

        if nk == 1:
            finish(part)
        else:
            @pl.when(k == 0)
            def _():
                acc_ref[...] = part

            @pl.when(jnp.logical_and(k > 0, k < nk - 1))
            def _():
                acc_ref[...] += part

            @pl.when(k == nk - 1)
            def _():
                finish(acc_ref[...] + part)

    ins = [a, b] + ([res] if has_res else [])
    in_specs = [a_spec, b_spec] + ([o_spec] if has_res else [])
    acc_shape = (tm, tn) if nk > 1 else (V7X_SUBLANES, V7X_LANES)
    vmem = (2 * (_nbytes((tm, tk), a.dtype) + _nbytes((tk, tn), b.dtype) + _nbytes((tm, tn), out_dtype)
                 + (_nbytes((tm, tn), res.dtype) if has_res else 0))
            + _nbytes((tm, tk), MXU_DTYPE) + _nbytes((tk, tn), MXU_DTYPE) + 3 * _nbytes((tm, tn), F32))
    return pl.pallas_call(
        body, name=name, grid=(m_dim // tm, n_dim // tn, nk),
        in_specs=in_specs, out_specs=[o_spec, o_spec] if also_mxu else o_spec,
        out_shape=([jax.ShapeDtypeStruct((m_dim, n_dim), out_dtype), jax.ShapeDtypeStruct((m_dim, n_dim), MXU_DTYPE)]
                   if also_mxu else jax.ShapeDtypeStruct((m_dim, n_dim), out_dtype)),
        scratch_shapes=[pltpu.VMEM(acc_shape, F32)],
        compiler_params=_params(("parallel", "parallel", "arbitrary"), vmem),
    )(*ins)


def _rms_fwd(x, g, *, width, cb=0, name):
    s = x.shape[0]
    t = _tile(s, 512)

    def body(x_ref, g_ref, o_ref):
        xv = x_ref[...].astype(F32)
        r = lax.rsqrt(jnp.mean(xv * xv, axis=-1, keepdims=True) + EPS)
        o_ref[...] = (xv * r * g_ref[...]).astype(o_ref.dtype)

    return pl.pallas_call(
        body, name=name, grid=(s // t,),
        in_specs=[_rows(t, width, cb), _full((1, width))], out_specs=_rows(t, width),
        out_shape=jax.ShapeDtypeStruct((s, width), MXU_DTYPE),
        compiler_params=_params(("parallel",), 8 * _nbytes((t, width), F32)),
    )(x, g)


def _rms_bwd(x, g, dn, *, width, cb=0, res=None, out_dtype=F32, name):
    s = x.shape[0]
    t = _tile(s, 256)
    has_res = res is not None

    def body(*refs):
        x_ref, g_ref, dn_ref = refs[:3]
        res_ref = refs[3] if has_res else None
        dx_ref, dg_ref = refs[-2], refs[-1]
        xv = x_ref[...].astype(F32)
        dnv = dn_ref[...].astype(F32)
        r = lax.rsqrt(jnp.mean(xv * xv, axis=-1, keepdims=True) + EPS)
        xr = xv * r
        dng = dnv * g_ref[...]
        dx = r * dng - xr * (r * r) * jnp.mean(dng * xv, axis=-1, keepdims=True)
        if has_res:
            dx = dx + res_ref[...].astype(F32)
        dx_ref[...] = dx.astype(dx_ref.dtype)
        part = jnp.sum(dnv * xr, axis=0, keepdims=True)

        @pl.when(pl.program_id(0) == 0)
        def _():
            dg_ref[...] = part

        @pl.when(pl.program_id(0) > 0)
        def _():
            dg_ref[...] += part

    ins = [x, g, dn] + ([res] if has_res else [])
    in_specs = [_rows(t, width, cb), _full((1, width)), _rows(t, width)] + ([_rows(t, width)] if has_res else [])
    return pl.pallas_call(
        body, name=name, grid=(s // t,),
        in_specs=in_specs, out_specs=[_rows(t, width), _full((1, width))],
        out_shape=[jax.ShapeDtypeStruct((s, width), out_dtype), jax.ShapeDtypeStruct((1, width), F32)],
        compiler_params=_params(("arbitrary",), 16 * _nbytes((t, width), F32)),
    )(*ins)


def _final_loss(x, g, target, *, name):
    s, d = x.shape
    t = _tile(s, 256)

    def body(x_ref, g_ref, t_ref, loss_ref, dx_ref, dg_ref):
        xv = x_ref[...]
        r = lax.rsqrt(jnp.mean(xv * xv, axis=-1, keepdims=True) + EPS)
        xr = xv * r
        err = xr * g_ref[...] - t_ref[...]
        part_loss = 0.5 * jnp.sum(jnp.mean(err * err, axis=-1, keepdims=True), axis=0, keepdims=True)
        dnv = err * (1.0 / d)
        dng = dnv * g_ref[...]
        dx_ref[...] = r * dng - xr * (r * r) * jnp.mean(dng * xv, axis=-1, keepdims=True)
        part_dg = jnp.sum(dnv * xr, axis=0, keepdims=True)

        @pl.when(pl.program_id(0) == 0)
        def _():
            dg_ref[...] = part_dg
            loss_ref[...] = jnp.zeros(loss_ref.shape, F32) + part_loss

        @pl.when(pl.program_id(0) > 0)
        def _():
            dg_ref[...] += part_dg
            loss_ref[...] += part_loss

    return pl.pallas_call(
        body, name=name, grid=(s // t,),
        in_specs=[_rows(t, d), _full((1, d)), _rows(t, d)],
        out_specs=[_full((V7X_SUBLANES, V7X_LANES)), _rows(t, d), _full((1, d))],
        out_shape=[jax.ShapeDtypeStruct((V7X_SUBLANES, V7X_LANES), F32), jax.ShapeDtypeStruct((s, d), F32),
                   jax.ShapeDtypeStruct((1, d), F32)],
        compiler_params=_params(("arbitrary",), 16 * _nbytes((t, d), F32)),
    )(x, g, target)


def _shift_down(v, d, fill, rows):
    return jnp.where(rows >= d, pltpu.roll(v, d, 0), fill)


def _shift_up(v, d, fill, rows, t):
    return jnp.where(rows < t - d, pltpu.roll(v, t - d, 0), fill)


def _lru_gates(xc, wa_ref, ba_ref, wx_ref, bx_ref, lam_ref):
    ra = _sigmoid(_dot(xc, wa_ref[...]) + ba_ref[...])
    ig = _sigmoid(_dot(xc, wx_ref[...]) + bx_ref[...])
    sp = _softplus(-lam_ref[...])
    log_a = -LRU_C * ra * sp
    a = jnp.exp(log_a)
    s2 = _neg_expm1(2.0 * log_a)
    return ra, ig, sp, a, s2


def _conv(ubuf, cw_ref, cb_ref, t):
    big = ubuf[...]
    shifted = [pltpu.roll(big, CONV_WIDTH - 1 - k, 0)[V7X_SUBLANES:t + V7X_SUBLANES] if k < CONV_WIDTH - 1
               else big[V7X_SUBLANES:t + V7X_SUBLANES] for k in range(CONV_WIDTH)]
    xc = cb_ref[...] + shifted[0] * cw_ref[0:1, :]
    for k in range(1, CONV_WIDTH):
        xc = xc + shifted[k] * cw_ref[k:k + 1, :]
    return xc, shifted


def _lru_fwd(z, cw, cb, wa, ba, wx, bx, lam, *, name):
    s = z.shape[0]
    w = LRU_WIDTH
    t = _row_tile(s, 256)
    steps = [1 << k for k in range(int(math.log2(t)))]

    def body(u_ref, ug_ref, cw_ref, cb_ref, wa_ref, ba_ref, wx_ref, bx_ref, lam_ref, y_ref, h_ref, ubuf, hc):
        @pl.when(pl.program_id(0) == 0)
        def _():
            ubuf[0:V7X_SUBLANES, :] = jnp.zeros((V7X_SUBLANES, w), F32)
            hc[...] = jnp.zeros_like(hc)

        ubuf[V7X_SUBLANES:t + V7X_SUBLANES, :] = u_ref[...]
        xc, _ = _conv(ubuf, cw_ref, cb_ref, t)
        _, ig, _, a, s2 = _lru_gates(xc, wa_ref, ba_ref, wx_ref, bx_ref, lam_ref)
        b = jnp.sqrt(s2) * (ig * xc)
        rows = lax.broadcasted_iota(jnp.int32, (t, w), 0)
        for d in steps:
            b = a * _shift_down(b, d, 0.0, rows) + b
            a = a * _shift_down(a, d, 1.0, rows)
        h = a * hc[0:1, :] + b
        h_ref[...] = h
        y_ref[...] = (h * _gelu(ug_ref[...])).astype(y_ref.dtype)
        hc[0:1, :] = h_ref[t - 1:t, :]
        ubuf[0:V7X_SUBLANES, :] = ubuf[t:t + V7X_SUBLANES, :]

    vec = _full((1, w))
    return pl.pallas_call(
        body, name=name, grid=(s // t,),
        in_specs=[_rows(t, w, 0), _rows(t, w, 1), _full((CONV_WIDTH, w)), vec, _full((w, w)), vec, _full((w, w)),
                  vec, vec],
        out_specs=[_rows(t, w), _rows(t, w)],
        out_shape=[jax.ShapeDtypeStruct((s, w), MXU_DTYPE), jax.ShapeDtypeStruct((s, w), F32)],
        scratch_shapes=[pltpu.VMEM((t + V7X_SUBLANES, w), F32), pltpu.VMEM((V7X_SUBLANES, w), F32)],
        compiler_params=_params(("arbitrary",), 40 * _nbytes((t, w), F32)),
    )(z, z, cw, cb, wa, ba, wx, bx, lam)


def _lru_bwd(z, h, dy, cw, cb, wa, ba, wx, bx, lam, *, name):
    s = z.shape[0]
    w = LRU_WIDTH
    t = _row_tile(s, 256)
    nt = s // t
    per8 = t // V7X_SUBLANES
    steps = [1 << k for k in range(int(math.log2(t)))]

    def body(u_ref, ug_ref, h_ref, dy_ref, uprev_ref, hprev_ref, cw_ref, cb_ref, wa_ref, ba_ref, wx_ref, bx_ref,
             lam_ref, dz_ref, dcw_ref, dcb_ref, dwa_ref, dba_ref, dwx_ref, dbx_ref, dlam_ref,
             ubuf, dbuf, acar, dhcar, tmp):
        i = pl.program_id(0)
        first_tile = i == nt - 1

        @pl.when(i == 0)
        def _():
            for r in (dcw_ref, dcb_ref, dwa_ref, dba_ref, dwx_ref, dbx_ref, dlam_ref, acar, dhcar):
                r[...] = jnp.zeros_like(r)
            dbuf[t:t + V7X_SUBLANES, :] = jnp.zeros((V7X_SUBLANES, w), F32)

        keep = jnp.where(first_tile, 0.0, 1.0)
        ubuf[0:V7X_SUBLANES, :] = uprev_ref[...] * keep
        ubuf[V7X_SUBLANES:t + V7X_SUBLANES, :] = u_ref[...]
        xc, shifted = _conv(ubuf, cw_ref, cb_ref, t)
        ra, ig, sp, a, s2 = _lru_gates(xc, wa_ref, ba_ref, wx_ref, bx_ref, lam_ref)
        sq = jnp.sqrt(s2)
        gx = ig * xc
        rows = lax.broadcasted_iota(jnp.int32, (t, w), 0)
        ugv = ug_ref[...]
        dyv = dy_ref[...].astype(F32)
        hv = h_ref[...]

        acc_g = dyv * _gelu(ugv)
        acc_a = _shift_up(a, 1, acar[0:1, :], rows, t)
        for d in steps:
            acc_g = acc_a * _shift_up(acc_g, d, 0.0, rows, t) + acc_g
            acc_a = acc_a * _shift_up(acc_a, d, 1.0, rows, t)
        dh = acc_a * dhcar[0:1, :] + acc_g

        hprev = _shift_down(hv, 1, hprev_ref[V7X_SUBLANES - 1:V7X_SUBLANES, :] * keep, rows)
        d_a = dh * hprev
        d_sq = dh * gx
        d_gx = dh * sq
        d_ig = d_gx * xc
        dxc = d_gx * ig
        d_log_a = d_a * a - d_sq * (1.0 - s2) / sq
        d_ra = d_log_a * (-LRU_C * sp)
        lamv = lam_ref[...]
        dlam_ref[...] += jnp.sum(d_log_a * (-LRU_C * ra), axis=0, keepdims=True) * (-_sigmoid(-lamv))
        dpa = d_ra * ra * (1.0 - ra)
        dpx = d_ig * ig * (1.0 - ig)
        dba_ref[...] += jnp.sum(dpa, axis=0, keepdims=True)
        dbx_ref[...] += jnp.sum(dpx, axis=0, keepdims=True)
        dwa_ref[...] += _dot_tn(xc, dpa)
        dwx_ref[...] += _dot_tn(xc, dpx)
        dxc = dxc + _dot_nt(dpa, wa_ref[...]) + _dot_nt(dpx, wx_ref[...])
        dcb_ref[...] += jnp.sum(dxc, axis=0, keepdims=True)
        for k in range(CONV_WIDTH):
            dcw_ref[k:k + 1, :] += jnp.sum(dxc * shifted[k], axis=0, keepdims=True)

        dbuf[0:t, :] = dxc
        bigd = dbuf[...]
        du = dxc * cw_ref[CONV_WIDTH - 1:CONV_WIDTH, :]
        for k in range(CONV_WIDTH - 1):
            e = CONV_WIDTH - 1 - k
            du = du + pltpu.roll(bigd, t + V7X_SUBLANES - e, 0)[0:t] * cw_ref[k:k + 1, :]
        dz_ref[:, 0:w] = du.astype(dz_ref.dtype)
        dz_ref[:, w:2 * w] = (dyv * hv * _gelu_grad(ugv)).astype(dz_ref.dtype)

        dbuf[t:t + V7X_SUBLANES, :] = dbuf[0:V7X_SUBLANES, :]
        tmp[...] = a
        acar[0:1, :] = tmp[0:1, :]
        tmp[...] = dh
        dhcar[0:1, :] = tmp[0:1, :]

    vec = _full((1, w))
    rev = lambda cbk: pl.BlockSpec((t, w), lambda i: (nt - 1 - i, cbk))
    prev8 = lambda cbk: pl.BlockSpec((V7X_SUBLANES, w),
                                     lambda i: (jnp.maximum((nt - 1 - i) * per8 - 1, 0), cbk))
    return pl.pallas_call(
        body, name=name, grid=(nt,),
        in_specs=[rev(0), rev(1), rev(0), rev(0), prev8(0), prev8(0), _full((CONV_WIDTH, w)), vec, _full((w, w)),
                  vec, _full((w, w)), vec, vec],
        out_specs=[pl.BlockSpec((t, 2 * w), lambda i: (nt - 1 - i, 0)), _full((CONV_WIDTH, w)), vec,
                   _full((w, w)), vec, _full((w, w)), vec, vec],
        out_shape=[jax.ShapeDtypeStruct((s, 2 * w), MXU_DTYPE), jax.ShapeDtypeStruct((CONV_WIDTH, w), F32),
                   jax.ShapeDtypeStruct((1, w), F32), jax.ShapeDtypeStruct((w, w), F32),
                   jax.ShapeDtypeStruct((1, w), F32), jax.ShapeDtypeStruct((w, w), F32),
                   jax.ShapeDtypeStruct((1, w), F32), jax.ShapeDtypeStruct((1, w), F32)],
        scratch_shapes=[pltpu.VMEM((t + V7X_SUBLANES, w), F32), pltpu.VMEM((t + V7X_SUBLANES, w), F32),
                        pltpu.VMEM((V7X_SUBLANES, w), F32), pltpu.VMEM((V7X_SUBLANES, w), F32),
                        pltpu.VMEM((t, w), F32)],
        compiler_params=_params(("arbitrary",), 80 * _nbytes((t, w), F32)),
    )(z, z, h, dy, z, h, cw, cb, wa, ba, wx, bx, lam)


def _rope_apply(v, cos, sin, width):
    half = MLA_ROPE // 2
    lanes = lax.broadcasted_iota(jnp.int32, v.shape, 1)
    first = (lanes % MLA_ROPE) < half
    partner = jnp.where(first, pltpu.roll(v, width - half, 1), pltpu.roll(v, half, 1))
    return v * cos + partner * sin


def _rope(x, cos, sin, *, width, cb, out_dtype, sum_heads=False, name):
    s = x.shape[0]
    t = _tile(s, 512)
    out_w = V7X_LANES if sum_heads else width

    def body(x_ref, c_ref, s_ref, o_ref):
        v = x_ref[...].astype(F32)
        if sum_heads:
            v = v[:, 0:V7X_LANES] + v[:, V7X_LANES:2 * V7X_LANES]
            v = v + pltpu.roll(v, 64, 1)
            v = v + pltpu.roll(v, 32, 1)
            out = _rope_apply(v, c_ref[...], s_ref[...], V7X_LANES)
            lanes = lax.broadcasted_iota(jnp.int32, out.shape, 1)
            out = jnp.where(lanes < MLA_ROPE, out, 0.0)
        else:
            out = _rope_apply(v, c_ref[...], s_ref[...], width)
        o_ref[...] = out.astype(o_ref.dtype)

    return pl.pallas_call(
        body, name=name, grid=(s // t,),
        in_specs=[_rows(t, width, cb), _rows(t, out_w), _rows(t, out_w)], out_specs=_rows(t, out_w),
        out_shape=jax.ShapeDtypeStruct((s, out_w), out_dtype),
        compiler_params=_params(("parallel",), 12 * _nbytes((t, width), F32)),
    )(x, cos, sin)


def _visible(t, unit, transposed):
    q_idx = lax.broadcasted_iota(jnp.int32, (t, t), 1 if transposed else 0)
    k_idx = lax.broadcasted_iota(jnp.int32, (t, t), 0 if transposed else 1)
    shift = int(math.log2(unit))
    return (q_idx >> shift) >= (k_idx >> shift)


def _attn_tile(s):
    return min(512, s // 4)


def _attn_fwd(q, k, v, cq, ck, *, scale, unit, name):
    hn, s, dk = q.shape
    dv = v.shape[-1]
    t = _attn_tile(s)
    decay = cq is not None

    def body(*refs):
        q_ref, k_ref, v_ref = refs[:3]
        cq_ref, ck_ref = (refs[3], refs[4]) if decay else (None, None)
        o_ref, lse_ref = refs[-2], refs[-1]
        i = pl.program_id(1)
        qt = q_ref[0]

        def tile(j, carry, masked):
            m, l, acc = carry
            off = pl.multiple_of(j * t, t)
            kt = k_ref[0, pl.ds(off, t), :]
            vt = v_ref[0, pl.ds(off, t), :]
            sc = _dot_nt(qt, kt) * scale
            if decay:
                sc = sc + cq_ref[0] - ck_ref[0, :, pl.ds(off, t)]
            if masked:
                sc = jnp.where(_visible(t, unit, False), sc, NEG_INF)
            m_new = jnp.maximum(m, jnp.max(sc, axis=-1, keepdims=True))
            alpha = jnp.exp(m - m_new)
            pr = jnp.exp(sc - m_new)
            l = alpha * l + jnp.sum(pr, axis=-1, keepdims=True)
            acc = alpha * acc + _dot(pr, vt)
            return m_new, l, acc

        init = (jnp.full((t, 1), NEG_INF, F32), jnp.zeros((t, 1), F32), jnp.zeros((t, dv), F32))
        carry = lax.fori_loop(0, i, lambda j, c: tile(j, c, False), init)
        m, l, acc = tile(i, carry, True)
        o_ref[0] = (acc / l).astype(o_ref.dtype)
        lse_ref[0] = m + jnp.log(l)

    qs = lambda d: pl.BlockSpec((1, t, d), lambda h, i: (h, i, 0))
    whole = lambda d: pl.BlockSpec((1, s, d), lambda h, i: (h, 0, 0))
    in_specs = [qs(dk), whole(dk), whole(dv)]
    ins = [q, k, v]
    if decay:
        in_specs += [qs(1), pl.BlockSpec((1, 1, s), lambda h, i: (h, 0, 0))]
        ins += [cq, ck]
    vmem = 4 * _nbytes((s, dk + dv), q.dtype) + 10 * _nbytes((t, t), F32) + 8 * _nbytes((t, V7X_LANES), F32)
    return pl.pallas_call(
        body, name=name, grid=(hn, s // t),
        in_specs=in_specs, out_specs=[qs(dv), qs(1)],
        out_shape=[jax.ShapeDtypeStruct((hn, s, dv), MXU_DTYPE), jax.ShapeDtypeStruct((hn, s, 1), F32)],
        compiler_params=_params(("parallel", "arbitrary"), vmem),
    )(*ins)


def _attn_bwd_q(q, k, v, do, lse, cq, ck, *, scale, unit, name):
    hn, s, dk = q.shape
    dv = v.shape[-1]
    t = _attn_tile(s)
    nt = s // t
    decay = cq is not None

    def body(*refs):
        q_ref, k_ref, v_ref, do_ref, lse_ref = refs[:5]
        cq_ref, ck_ref = (refs[5], refs[6]) if decay else (None, None)
        dq_ref, dl_ref, p_sc, dp_sc = refs[-4:]
        i = pl.program_id(1)
        qt = q_ref[0]
        dot = do_ref[0]
        lse_t = lse_ref[0]

        def sweep1(j, delta, masked):
            off = pl.multiple_of(j * t, t)
            kt = k_ref[0, pl.ds(off, t), :]
            vt = v_ref[0, pl.ds(off, t), :]
            sc = _dot_nt(qt, kt) * scale
            if decay:
                sc = sc + cq_ref[0] - ck_ref[0, :, pl.ds(off, t)]
            if masked:
                sc = jnp.where(_visible(t, unit, False), sc, NEG_INF)
            pr = jnp.exp(sc - lse_t)
            dp = _dot_nt(dot, vt)
            p_sc[j] = pr
            dp_sc[j] = dp
            return delta + jnp.sum(pr * dp, axis=-1, keepdims=True)

        delta = lax.fori_loop(0, i, lambda j, c: sweep1(j, c, False), jnp.zeros((t, 1), F32))
        delta = sweep1(i, delta, True)

        def sweep2(j, dq):
            off = pl.multiple_of(j * t, t)
            ds = p_sc[j] * (dp_sc[j] - delta)
            return dq + _dot(ds, k_ref[0, pl.ds(off, t), :])

        dq = lax.fori_loop(0, i + 1, sweep2, jnp.zeros((t, dk), F32))
        dq_ref[0] = dq * scale
        dl_ref[0] = delta

    qs = lambda d: pl.BlockSpec((1, t, d), lambda h, i: (h, i, 0))
    whole = lambda d: pl.BlockSpec((1, s, d), lambda h, i: (h, 0, 0))
    in_specs = [qs(dk), whole(dk), whole(dv), qs(dv), qs(1)]
    ins = [q, k, v, do, lse]
    if decay:
        in_specs += [qs(1), pl.BlockSpec((1, 1, s), lambda h, i: (h, 0, 0))]
        ins += [cq, ck]
    vmem = (4 * _nbytes((s, dk + dv), q.dtype) + 2 * _nbytes((nt, t, t), F32) + 8 * _nbytes((t, t), F32)
            + 12 * _nbytes((t, V7X_LANES), F32))
    return pl.pallas_call(
        body, name=name, grid=(hn, nt),
        in_specs=in_specs, out_specs=[qs(dk), qs(1)],
        out_shape=[jax.ShapeDtypeStruct((hn, s, dk), F32), jax.ShapeDtypeStruct((hn, s, 1), F32)],
        scratch_shapes=[pltpu.VMEM((nt, t, t), F32), pltpu.VMEM((nt, t, t), F32)],
        compiler_params=_params(("parallel", "arbitrary"), vmem),
    )(*ins)


def _attn_bwd_kv(q, k, v, do, lse_row, delta_row, cq_row, ck, *, scale, unit, name):
    hn, s, dk = q.shape
    dv = v.shape[-1]
    t = _attn_tile(s)
    nt = s // t
    decay = ck is not None

    def body(*refs):
        k_ref, v_ref, q_ref, do_ref, lse_ref, dl_ref = refs[:6]
        ck_ref, cq_ref = (refs[6], refs[7]) if decay else (None, None)
        outs = refs[8:] if decay else refs[6:]
        dk_ref, dv_ref = outs[0], outs[1]
        j = pl.program_id(1)
        kt = k_ref[0]
        vt = v_ref[0]

        def tile(i, carry, masked):
            dk_acc, dv_acc, dc_acc = carry
            off = pl.multiple_of(i * t, t)
            qt = q_ref[0, pl.ds(off, t), :]
            dot = do_ref[0, pl.ds(off, t), :]
            sc = _dot_nt(kt, qt) * scale
            if decay:
                sc = sc + cq_ref[0, :, pl.ds(off, t)] - ck_ref[0]
            if masked:
                sc = jnp.where(_visible(t, unit, True), sc, NEG_INF)
            pr = jnp.exp(sc - lse_ref[0, :, pl.ds(off, t)])
            dv_acc = dv_acc + _dot(pr, dot)
            ds = pr * (_dot_nt(vt, dot) - dl_ref[0, :, pl.ds(off, t)])
            dk_acc = dk_acc + _dot(ds, qt)
            if decay:
                dc_acc = dc_acc + jnp.sum(ds, axis=-1, keepdims=True)
            return dk_acc, dv_acc, dc_acc

        init = (jnp.zeros((t, dk), F32), jnp.zeros((t, dv), F32), jnp.zeros((t, 1), F32))
        carry = tile(j, init, True)
        dk_acc, dv_acc, dc_acc = lax.fori_loop(j + 1, nt, lambda i, c: tile(i, c, False), carry)
        dk_ref[0] = dk_acc * scale
        dv_ref[0] = dv_acc
        if decay:
            outs[2][0] = -dc_acc

    ks = lambda d: pl.BlockSpec((1, t, d), lambda h, j: (h, j, 0))
    whole = lambda d: pl.BlockSpec((1, s, d), lambda h, j: (h, 0, 0))
    row = pl.BlockSpec((1, 1, s), lambda h, j: (h, 0, 0))
    in_specs = [ks(dk), ks(dv), whole(dk), whole(dv), row, row]
    ins = [k, v, q, do, lse_row, delta_row]
    out_specs = [ks(dk), ks(dv)]
    out_shape = [jax.ShapeDtypeStruct((hn, s, dk), F32), jax.ShapeDtypeStruct((hn, s, dv), F32)]
    if decay:
        in_specs += [ks(1), row]
        ins += [ck, cq_row]
        out_specs.append(ks(1))
        out_shape.append(jax.ShapeDtypeStruct((hn, s, 1), F32))
    vmem = 4 * _nbytes((s, dk + dv), q.dtype) + 10 * _nbytes((t, t), F32) + 12 * _nbytes((t, V7X_LANES), F32)
    return pl.pallas_call(
        body, name=name, grid=(hn, nt),
        in_specs=in_specs, out_specs=out_specs, out_shape=out_shape,
        compiler_params=_params(("parallel", "arbitrary"), vmem),
    )(*ins)


STRIP = 32
HEAD_PAIRS = HEADS // 2


def _strip_rows(t):
    return min(STRIP, t)


def _pair_mask(t):
    lane = lax.broadcasted_iota(jnp.int32, (t, V7X_LANES), 1)
    return lane < (V7X_LANES // 2)


def _strip_visible(r, t, row0, unit, transposed):
    rows = lax.broadcasted_iota(jnp.int32, (r, t), 0) + row0
    cols = lax.broadcasted_iota(jnp.int32, (r, t), 1)
    shift = int(math.log2(unit))
    if transposed:
        return (cols >> shift) >= (rows >> shift)
    return (rows >> shift) >= (cols >> shift)


def _rope_lanes(x, cos, sin, *, cb, groups, out_dtype, sum_parts=0, name):
    s = cos.shape[0]
    t = _tile(s, 512)
    w = groups * V7X_LANES

    def body(x_ref, c_ref, s_ref, o_ref):
        if sum_parts:
            v = x_ref[0].astype(F32)
            for part in range(1, sum_parts):
                v = v + x_ref[part].astype(F32)
            o_ref[...] = _rope_apply(v, c_ref[...], s_ref[...], V7X_LANES).astype(o_ref.dtype)
        else:
            for g in range(groups):
                sl = slice(g * V7X_LANES, (g + 1) * V7X_LANES)
                o_ref[:, sl] = _rope_apply(x_ref[:, sl].astype(F32), c_ref[...], s_ref[...],
                                           V7X_LANES).astype(o_ref.dtype)

    x_spec = (pl.BlockSpec((sum_parts, t, V7X_LANES), lambda i: (0, i, 0)) if sum_parts else _rows(t, w, cb))
    return pl.pallas_call(
        body, name=name, grid=(s // t,),
        in_specs=[x_spec, _rows(t, V7X_LANES), _rows(t, V7X_LANES)], out_specs=_rows(t, w),
        out_shape=jax.ShapeDtypeStruct((s, w), out_dtype),
        compiler_params=_params(("parallel",), 12 * _nbytes((t, max(w, 4 * V7X_LANES)), F32)),
    )(x, cos, sin)


def _pair_fwd(q_arr, q_cb, k_arr, k_cb, v_arr, v_cb, rope, decay, *, scale, unit, name):
    s = q_arr.shape[0]
    t = _attn_tile(s)
    r = _strip_rows(t)
    has_rope, has_decay = rope is not None, decay is not None
    kw = 2 * V7X_LANES if has_rope else V7X_LANES

    def body(*refs):
        it = iter(refs)
        q_ref, k_ref, v_ref = next(it), next(it), next(it)
        qr_ref, kr_ref = (next(it), next(it)) if has_rope else (None, None)
        cq_ref, ck_ref = (next(it), next(it)) if has_decay else (None, None)
        o_ref, lse_ref = next(it), next(it)
        q_sc, s_sc, p_sc, acc_sc, m_sc, l_sc, al_sc = (next(it) for _ in range(7))
        i = pl.program_id(1)
        in_a = _pair_mask(t)
        qv = q_ref[...]
        for hd in range(2):
            q_sc[hd, :, 0:V7X_LANES] = jnp.where(in_a if hd == 0 else jnp.logical_not(in_a), qv, 0).astype(MXU_DTYPE)
            if has_rope:
                q_sc[hd, :, V7X_LANES:kw] = qr_ref[:, hd * V7X_LANES:(hd + 1) * V7X_LANES].astype(MXU_DTYPE)
        m_sc[...] = jnp.full(m_sc.shape, NEG_INF, F32)
        l_sc[...] = jnp.zeros(l_sc.shape, F32)
        acc_sc[...] = jnp.zeros(acc_sc.shape, F32)

        def tile(j, masked):
            off = pl.multiple_of(j * t, t)
            kt = k_ref[pl.ds(off, t), :]
            if has_rope:
                kt = jnp.concatenate([kt, kr_ref[pl.ds(off, t), :]], axis=-1)
            vt = v_ref[pl.ds(off, t), :]
            for hd in range(2):
                s_sc[...] = _dot_nt(q_sc[hd], kt)
                ck_row = ck_ref[hd, :, pl.ds(off, t)] if has_decay else None

                def strip(b, carry, hd=hd, ck_row=ck_row):
                    row0 = pl.multiple_of(b * r, r)
                    rows = pl.ds(row0, r)
                    sc = s_sc[rows, :] * scale
                    if has_decay:
                        sc = sc + (cq_ref[hd, rows, :] - ck_row)
                    if masked:
                        sc = jnp.where(_strip_visible(r, t, row0, unit, False), sc, NEG_INF)
                    m_old = m_sc[hd, rows, :]
                    m_new = jnp.maximum(m_old, jnp.max(sc, axis=-1, keepdims=True))
                    alpha = jnp.exp(m_old - m_new)
                    pr = jnp.exp(sc - m_new)
                    l_sc[hd, rows, :] = alpha * l_sc[hd, rows, :] + jnp.sum(pr, axis=-1, keepdims=True)
                    m_sc[hd, rows, :] = m_new
                    al_sc[rows, :] = alpha
                    p_sc[rows, :] = pr.astype(MXU_DTYPE)
                    return carry

                lax.fori_loop(0, t // r, strip, 0)
                acc_sc[hd] = al_sc[...] * acc_sc[hd] + _dot(p_sc[...], vt)

        def unmasked(j, carry):
            tile(j, False)
            return carry

        lax.fori_loop(0, i, unmasked, 0)
        tile(i, True)
        o_ref[...] = jnp.where(in_a, acc_sc[0] / l_sc[0], acc_sc[1] / l_sc[1]).astype(o_ref.dtype)
        lse_ref[...] = m_sc[...] + jnp.log(l_sc[...])

    blk = lambda cb: pl.BlockSpec((t, V7X_LANES), lambda p, i: (i, cb + p))
    whole = lambda cb: pl.BlockSpec((s, V7X_LANES), lambda p, i: (0, cb + p))
    stat = pl.BlockSpec((2, t, 1), lambda p, i: (p, i, 0))
    in_specs = [blk(q_cb), whole(k_cb), whole(v_cb)]
    ins = [q_arr, k_arr, v_arr]
    if has_rope:
        in_specs += [pl.BlockSpec((t, 2 * V7X_LANES), lambda p, i: (i, p)),
                     pl.BlockSpec((s, V7X_LANES), lambda p, i: (0, 0))]
        ins += list(rope)
    if has_decay:
        in_specs += [stat, pl.BlockSpec((2, 1, s), lambda p, i: (p, 0, 0))]
        ins += list(decay)
    col = (2, t, 1)
    vmem = (6 * _nbytes((s, V7X_LANES), MXU_DTYPE) + 6 * _nbytes((t, t), F32) + 10 * _nbytes((t, V7X_LANES), F32)
            + 8 * _nbytes((2, t, V7X_LANES), F32))
    return pl.pallas_call(
        body, name=name, grid=(HEAD_PAIRS, s // t),
        in_specs=in_specs, out_specs=[pl.BlockSpec((t, V7X_LANES), lambda p, i: (i, p)), stat],
        out_shape=[jax.ShapeDtypeStruct((s, HEADS * 64), MXU_DTYPE), jax.ShapeDtypeStruct((HEADS, s, 1), F32)],
        scratch_shapes=[pltpu.VMEM((2, t, kw), MXU_DTYPE), pltpu.VMEM((t, t), F32), pltpu.VMEM((t, t), MXU_DTYPE),
                        pltpu.VMEM((2, t, V7X_LANES), F32), pltpu.VMEM(col, F32), pltpu.VMEM(col, F32),
                        pltpu.VMEM((t, 1), F32)],
        compiler_params=_params(("parallel", "arbitrary"), vmem),
    )(*ins)


def _pair_bwd_q(q_arr, q_cb, k_arr, k_cb, v_arr, v_cb, do, lse, rope, decay, *, scale, unit, name):
    s = q_arr.shape[0]
    t = min(256, _attn_tile(s))
    nt = s // t
    r = _strip_rows(t)
    has_rope, has_decay = rope is not None, decay is not None
    kw = 2 * V7X_LANES if has_rope else V7X_LANES

    def body(*refs):
        it = iter(refs)
        q_ref, k_ref, v_ref, do_ref, lse_ref = (next(it) for _ in range(5))
        qr_ref, kr_ref = (next(it), next(it)) if has_rope else (None, None)
        cq_ref, ck_ref = (next(it), next(it)) if has_decay else (None, None)
        dq_ref, dl_ref = next(it), next(it)
        dqr_ref = next(it) if has_rope else None
        q_sc, do_sc, p_sc, dp_sc, ds_sc, dq_sc, dl_sc = (next(it) for _ in range(7))
        i = pl.program_id(1)
        in_a = _pair_mask(t)
        qv = q_ref[...]
        dov = do_ref[...]
        for hd in range(2):
            sel = in_a if hd == 0 else jnp.logical_not(in_a)
            q_sc[hd, :, 0:V7X_LANES] = jnp.where(sel, qv, 0).astype(MXU_DTYPE)
            if has_rope:
                q_sc[hd, :, V7X_LANES:kw] = qr_ref[:, hd * V7X_LANES:(hd + 1) * V7X_LANES].astype(MXU_DTYPE)
            do_sc[hd] = jnp.where(sel, dov, 0).astype(MXU_DTYPE)
        dl_sc[...] = jnp.zeros(dl_sc.shape, F32)
        dq_sc[...] = jnp.zeros(dq_sc.shape, F32)

        def keys(j):
            off = pl.multiple_of(j * t, t)
            kt = k_ref[pl.ds(off, t), :]
            if has_rope:
                kt = jnp.concatenate([kt, kr_ref[pl.ds(off, t), :]], axis=-1)
            return off, kt

        for hd in range(2):
            def sweep1(j, masked, hd=hd):
                off, kt = keys(j)
                p_sc[j] = _dot_nt(q_sc[hd], kt)
                dp_sc[j] = _dot_nt(do_sc[hd], v_ref[pl.ds(off, t), :])
                ck_row = ck_ref[hd, :, pl.ds(off, t)] if has_decay else None

                def strip(b, carry):
                    row0 = pl.multiple_of(b * r, r)
                    rows = pl.ds(row0, r)
                    sc = p_sc[j, rows, :] * scale
                    if has_decay:
                        sc = sc + (cq_ref[hd, rows, :] - ck_row)
                    if masked:
                        sc = jnp.where(_strip_visible(r, t, row0, unit, False), sc, NEG_INF)
                    pr = jnp.exp(sc - lse_ref[hd, rows, :])
                    p_sc[j, rows, :] = pr
                    dl_sc[hd, rows, :] += jnp.sum(pr * dp_sc[j, rows, :], axis=-1, keepdims=True)
                    return carry

                lax.fori_loop(0, t // r, strip, 0)

            def sweep1_unmasked(j, carry, sweep1=sweep1):
                sweep1(j, False)
                return carry

            lax.fori_loop(0, i, sweep1_unmasked, 0)
            sweep1(i, True)

            def sweep2(j, carry, hd=hd):
                _, kt = keys(j)

                def strip(b, c):
                    rows = pl.ds(pl.multiple_of(b * r, r), r)
                    ds = p_sc[j, rows, :] * (dp_sc[j, rows, :] - dl_sc[hd, rows, :])
                    ds_sc[rows, :] = ds.astype(MXU_DTYPE)
                    return c

                lax.fori_loop(0, t // r, strip, 0)
                dq_sc[hd] += _dot(ds_sc[...], kt)
                return carry

            lax.fori_loop(0, i + 1, sweep2, 0)

        dq_ref[...] = (jnp.where(in_a, dq_sc[0, :, 0:V7X_LANES], dq_sc[1, :, 0:V7X_LANES]) * scale).astype(dq_ref.dtype)
        dl_ref[...] = dl_sc[...]
        if has_rope:
            dqr_ref[:, 0:V7X_LANES] = dq_sc[0, :, V7X_LANES:kw] * scale
            dqr_ref[:, V7X_LANES:kw] = dq_sc[1, :, V7X_LANES:kw] * scale

    blk = lambda cb: pl.BlockSpec((t, V7X_LANES), lambda p, i: (i, cb + p))
    whole = lambda cb: pl.BlockSpec((s, V7X_LANES), lambda p, i: (0, cb + p))
    stat = pl.BlockSpec((2, t, 1), lambda p, i: (p, i, 0))
    in_specs = [blk(q_cb), whole(k_cb), whole(v_cb), blk(0), stat]
    ins = [q_arr, k_arr, v_arr, do, lse]
    out_specs = [blk(0), stat]
    out_shape = [jax.ShapeDtypeStruct((s, HEADS * 64), MXU_DTYPE), jax.ShapeDtypeStruct((HEADS, s, 1), F32)]
    if has_rope:
        pair_rot = pl.BlockSpec((t, 2 * V7X_LANES), lambda p, i: (i, p))
        in_specs += [pair_rot, pl.BlockSpec((s, V7X_LANES), lambda p, i: (0, 0))]
        ins += list(rope)
        out_specs.append(pair_rot)
        out_shape.append(jax.ShapeDtypeStruct((s, HEADS * V7X_LANES), F32))
    if has_decay:
        in_specs += [stat, pl.BlockSpec((2, 1, s), lambda p, i: (p, 0, 0))]
        ins += list(decay)
    vmem = (6 * _nbytes((s, V7X_LANES), MXU_DTYPE) + 2 * _nbytes((nt, t, t), F32) + 6 * _nbytes((t, t), F32)
            + 16 * _nbytes((t, kw), F32))
    return pl.pallas_call(
        body, name=name, grid=(HEAD_PAIRS, nt),
        in_specs=in_specs, out_specs=out_specs, out_shape=out_shape,
        scratch_shapes=[pltpu.VMEM((2, t, kw), MXU_DTYPE), pltpu.VMEM((2, t, V7X_LANES), MXU_DTYPE),
                        pltpu.VMEM((nt, t, t), F32), pltpu.VMEM((nt, t, t), F32), pltpu.VMEM((t, t), MXU_DTYPE),
                        pltpu.VMEM((2, t, kw), F32), pltpu.VMEM((2, t, 1), F32)],
        compiler_params=_params(("parallel", "arbitrary"), vmem),
    )(*ins)


def _pair_bwd_kv(q_arr, q_cb, k_arr, k_cb, v_arr, v_cb, do, lse_row, delta_row, rope, decay, *, scale, unit, name):
    s = q_arr.shape[0]
    t = _attn_tile(s)
    nt = s // t
    r = _strip_rows(t)
    has_rope, has_decay = rope is not None, decay is not None
    kw = 2 * V7X_LANES if has_rope else V7X_LANES

    def body(*refs):
        it = iter(refs)
        k_ref, v_ref, q_ref, do_ref, lse_ref, dl_ref = (next(it) for _ in range(6))
        qr_ref, kr_ref = (next(it), next(it)) if has_rope else (None, None)
        ck_ref, cq_ref = (next(it), next(it)) if has_decay else (None, None)
        dk_ref, dv_ref = next(it), next(it)
        dkr_ref = next(it) if has_rope else None
        dc_ref = next(it) if has_decay else None
        k_sc, v_sc, st_sc, dpt_sc, pt_sc, dst_sc, dk_sc, dv_sc, dc_sc = (next(it) for _ in range(9))
        j = pl.program_id(1)
        in_a = _pair_mask(t)
        kv_, vv_ = k_ref[...], v_ref[...]
        for hd in range(2):
            sel = in_a if hd == 0 else jnp.logical_not(in_a)
            k_sc[hd, :, 0:V7X_LANES] = jnp.where(sel, kv_, 0).astype(MXU_DTYPE)
            if has_rope:
                k_sc[hd, :, V7X_LANES:kw] = kr_ref[...].astype(MXU_DTYPE)
            v_sc[hd] = jnp.where(sel, vv_, 0).astype(MXU_DTYPE)
        dk_sc[...] = jnp.zeros(dk_sc.shape, F32)
        dv_sc[...] = jnp.zeros(dv_sc.shape, F32)
        dc_sc[...] = jnp.zeros(dc_sc.shape, F32)

        def tile(i, masked):
            off = pl.multiple_of(i * t, t)
            qt = q_ref[pl.ds(off, t), :]
            dot = do_ref[pl.ds(off, t), :]
            for hd in range(2):
                qcat = qt
                if has_rope:
                    qcat = jnp.concatenate([qt, qr_ref[pl.ds(off, t), hd * V7X_LANES:(hd + 1) * V7X_LANES]], axis=-1)
                st_sc[...] = _dot_nt(k_sc[hd], qcat)
                dpt_sc[...] = _dot_nt(v_sc[hd], dot)
                lse_r = lse_ref[hd, :, pl.ds(off, t)]
                dl_r = dl_ref[hd, :, pl.ds(off, t)]
                cq_r = cq_ref[hd, :, pl.ds(off, t)] if has_decay else None

                def strip(b, carry, hd=hd, lse_r=lse_r, dl_r=dl_r, cq_r=cq_r):
                    row0 = pl.multiple_of(b * r, r)
                    rows = pl.ds(row0, r)
                    sc = st_sc[rows, :] * scale
                    if has_decay:
                        sc = sc + (cq_r - ck_ref[hd, rows, :])
                    if masked:
                        sc = jnp.where(_strip_visible(r, t, row0, unit, True), sc, NEG_INF)
                    pr = jnp.exp(sc - lse_r)
                    ds = pr * (dpt_sc[rows, :] - dl_r)
                    pt_sc[rows, :] = pr.astype(MXU_DTYPE)
                    dst_sc[rows, :] = ds.astype(MXU_DTYPE)
                    if has_decay:
                        dc_sc[hd, rows, :] += jnp.sum(ds, axis=-1, keepdims=True)
                    return carry

                lax.fori_loop(0, t // r, strip, 0)
                dv_sc[hd] += _dot(pt_sc[...], dot)
                dk_sc[hd] += _dot(dst_sc[...], qcat)

        tile(j, True)

        def unmasked(i, carry):
            tile(i, False)
            return carry

        lax.fori_loop(j + 1, nt, unmasked, 0)
        dk_ref[...] = (jnp.where(in_a, dk_sc[0, :, 0:V7X_LANES], dk_sc[1, :, 0:V7X_LANES]) * scale).astype(dk_ref.dtype)
        dv_ref[...] = jnp.where(in_a, dv_sc[0], dv_sc[1]).astype(dv_ref.dtype)
        if has_rope:
            dkr_ref[0] = (dk_sc[0, :, V7X_LANES:kw] + dk_sc[1, :, V7X_LANES:kw]) * scale
        if has_decay:
            dc_ref[...] = -dc_sc[...]

    blk = lambda cb: pl.BlockSpec((t, V7X_LANES), lambda p, j: (j, cb + p))
    whole = lambda cb: pl.BlockSpec((s, V7X_LANES), lambda p, j: (0, cb + p))
    stat = pl.BlockSpec((2, t, 1), lambda p, j: (p, j, 0))
    row = pl.BlockSpec((2, 1, s), lambda p, j: (p, 0, 0))
    in_specs = [blk(k_cb), blk(v_cb), whole(q_cb), whole(0), row, row]
    ins = [k_arr, v_arr, q_arr, do, lse_row, delta_row]
    out_specs = [blk(0), blk(0)]
    out_shape = [jax.ShapeDtypeStruct((s, HEADS * 64), MXU_DTYPE)] * 2
    if has_rope:
        in_specs += [pl.BlockSpec((s, 2 * V7X_LANES), lambda p, j: (0, p)),
                     pl.BlockSpec((t, V7X_LANES), lambda p, j: (j, 0))]
        ins += list(rope)
        out_specs.append(pl.BlockSpec((1, t, V7X_LANES), lambda p, j: (p, j, 0)))
        out_shape.append(jax.ShapeDtypeStruct((HEAD_PAIRS, s, V7X_LANES), F32))
    if has_decay:
        in_specs += [stat, row]
        ins += list(decay)
        out_specs.append(stat)
        out_shape.append(jax.ShapeDtypeStruct((HEADS, s, 1), F32))
    vmem = (12 * _nbytes((s, V7X_LANES), MXU_DTYPE) + 8 * _nbytes((t, t), F32) + 16 * _nbytes((t, kw), F32))
    return pl.pallas_call(
        body, name=name, grid=(HEAD_PAIRS, nt),
        in_specs=in_specs, out_specs=out_specs, out_shape=out_shape,
        scratch_shapes=[pltpu.VMEM((2, t, kw), MXU_DTYPE), pltpu.VMEM((2, t, V7X_LANES), MXU_DTYPE),
                        pltpu.VMEM((t, t), F32), pltpu.VMEM((t, t), F32), pltpu.VMEM((t, t), MXU_DTYPE),
                        pltpu.VMEM((t, t), MXU_DTYPE), pltpu.VMEM((2, t, kw), F32),
                        pltpu.VMEM((2, t, V7X_LANES), F32), pltpu.VMEM((2, t, 1), F32)],
        compiler_params=_params(("parallel", "arbitrary"), vmem),
    )(*ins)


def _fox_cum(z, bf, *, name):
    s = z.shape[0]
    w = V7X_LANES
    t = _row_tile(s, 512)
    steps = [1 << k for k in range(int(math.log2(t)))]
    cb = SEG["fl"][3] // w

    def body(f_ref, bf_ref, c_ref, car):
        @pl.when(pl.program_id(0) == 0)
        def _():
            car[...] = jnp.zeros_like(car)

        acc = -_softplus(-(f_ref[...] + bf_ref[...]))
        rows = lax.broadcasted_iota(jnp.int32, (t, w), 0)
        for d in steps:
            acc = acc + _shift_down(acc, d, 0.0, rows)
        c_ref[...] = acc + car[0:1, :]
        car[0:1, :] = c_ref[t - 1:t, :]

    return pl.pallas_call(
        body, name=name, grid=(s // t,),
        in_specs=[_rows(t, w, cb), _full((1, w))], out_specs=_rows(t, w),
        out_shape=jax.ShapeDtypeStruct((s, w), F32),
        scratch_shapes=[pltpu.VMEM((V7X_SUBLANES, w), F32)],
        compiler_params=_params(("arbitrary",), 16 * _nbytes((t, w), F32)),
    )(z, bf)


def _fox_cum_bwd(z, bf, dcum, *, name):
    s = z.shape[0]
    w = V7X_LANES
    t = _row_tile(s, 512)
    nt = s // t
    steps = [1 << k for k in range(int(math.log2(t)))]
    cb = SEG["fl"][3] // w

    def body(f_ref, bf_ref, dc_ref, df_ref, dbf_ref, car, tmp):
        @pl.when(pl.program_id(0) == 0)
        def _():
            car[...] = jnp.zeros_like(car)
            dbf_ref[...] = jnp.zeros_like(dbf_ref)

        acc = dc_ref[...]
        rows = lax.broadcasted_iota(jnp.int32, (t, w), 0)
        for d in steps:
            acc = acc + _shift_up(acc, d, 0.0, rows, t)
        dlf = acc + car[0:1, :]
        tmp[...] = dlf
        car[0:1, :] = tmp[0:1, :]
        df = dlf * _sigmoid(-(f_ref[...] + bf_ref[...]))
        df_ref[...] = df.astype(df_ref.dtype)
        dbf_ref[...] += jnp.sum(df, axis=0, keepdims=True)

    rev = lambda cbk: pl.BlockSpec((t, w), lambda i: (nt - 1 - i, cbk))
    return pl.pallas_call(
        body, name=name, grid=(nt,),
        in_specs=[rev(cb), _full((1, w)), rev(0)], out_specs=[rev(0), _full((1, w))],
        out_shape=[jax.ShapeDtypeStruct((s, w), MXU_DTYPE), jax.ShapeDtypeStruct((1, w), F32)],
        scratch_shapes=[pltpu.VMEM((V7X_SUBLANES, w), F32), pltpu.VMEM((t, w), F32)],
        compiler_params=_params(("arbitrary",), 16 * _nbytes((t, w), F32)),
    )(z, bf, dcum)


_GATE_CB = SEG["gate"][3] // D_MODEL


def _merge_fwd(ya, yb, yc, z, gate_b, *, name):
    s = ya.shape[0]
    d = D_MODEL
    t = _tile(s, 256)

    def body(ya_ref, yb_ref, yc_ref, g0_ref, g1_ref, g2_ref, gb_ref, o_ref):
        out = _sigmoid(g0_ref[...] + gb_ref[:, 0:d]) * ya_ref[...]
        out = out + _sigmoid(g1_ref[...] + gb_ref[:, d:2 * d]) * yb_ref[...]
        out = out + _sigmoid(g2_ref[...] + gb_ref[:, 2 * d:3 * d]) * yc_ref[...]
        o_ref[...] = out.astype(o_ref.dtype)

    return pl.pallas_call(
        body, name=name, grid=(s // t,),
        in_specs=[_rows(t, d)] * 3 + [_rows(t, d, _GATE_CB + b) for b in range(3)] + [_full((1, 3 * d))],
        out_specs=_rows(t, d), out_shape=jax.ShapeDtypeStruct((s, d), MXU_DTYPE),
        compiler_params=_params(("parallel",), 20 * _nbytes((t, d), F32)),
    )(ya, yb, yc, z, z, z, gate_b)


def _merge_bwd(dm, ya, yb, yc, z, gate_b, *, name):
    s = ya.shape[0]
    d = D_MODEL
    t = _tile(s, 256)

    def body(dm_ref, ya_ref, yb_ref, yc_ref, g0_ref, g1_ref, g2_ref, gb_ref, da_ref, db_ref, dc_ref, dgl_ref,
             dgb_ref):
        dmv = dm_ref[...]
        parts = []
        for b, (y_ref, g_ref, dy_ref) in enumerate(((ya_ref, g0_ref, da_ref), (yb_ref, g1_ref, db_ref),
                                                    (yc_ref, g2_ref, dc_ref))):
            gate = _sigmoid(g_ref[...] + gb_ref[:, b * d:(b + 1) * d])
            dy_ref[...] = (dmv * gate).astype(dy_ref.dtype)
            dgl = dmv * y_ref[...] * gate * (1.0 - gate)
            dgl_ref[:, b * d:(b + 1) * d] = dgl.astype(dgl_ref.dtype)
            parts.append(jnp.sum(dgl, axis=0, keepdims=True))

        @pl.when(pl.program_id(0) == 0)
        def _():
            for b, part in enumerate(parts):
                dgb_ref[:, b * d:(b + 1) * d] = part

        @pl.when(pl.program_id(0) > 0)
        def _():
            for b, part in enumerate(parts):
                dgb_ref[:, b * d:(b + 1) * d] += part

    return pl.pallas_call(
        body, name=name, grid=(s // t,),
        in_specs=[_rows(t, d)] * 4 + [_rows(t, d, _GATE_CB + b) for b in range(3)] + [_full((1, 3 * d))],
        out_specs=[_rows(t, d)] * 3 + [_rows(t, 3 * d), _full((1, 3 * d))],
        out_shape=[jax.ShapeDtypeStruct((s, d), MXU_DTYPE)] * 3
        + [jax.ShapeDtypeStruct((s, 3 * d), MXU_DTYPE), jax.ShapeDtypeStruct((1, 3 * d), F32)],
        compiler_params=_params(("arbitrary",), 36 * _nbytes((t, d), F32)),
    )(dm, ya, yb, yc, z, z, z, gate_b)


def _swiglu_fwd(hf, *, name):
    s = hf.shape[0]
    t = _tile(s, 256)

    def body(g_ref, u_ref, o_ref):
        gv = g_ref[...]
        o_ref[...] = (gv * _sigmoid(gv) * u_ref[...]).astype(o_ref.dtype)

    return pl.pallas_call(
        body, name=name, grid=(s // t,),
        in_specs=[_rows(t, D_FF, 0), _rows(t, D_FF, 1)], out_specs=_rows(t, D_FF),
        out_shape=jax.ShapeDtypeStruct((s, D_FF), MXU_DTYPE),
        compiler_params=_params(("parallel",), 10 * _nbytes((t, D_FF), F32)),
    )(hf, hf)


def _swiglu_bwd(hf, dact, *, name):
    s = hf.shape[0]
    t = _tile(s, 256)

    def body(g_ref, u_ref, da_ref, o_ref):
        gv = g_ref[...]
        dav = da_ref[...]
        sg = _sigmoid(gv)
        o_ref[:, 0:D_FF] = (dav * u_ref[...] * sg * (1.0 + gv * (1.0 - sg))).astype(o_ref.dtype)
        o_ref[:, D_FF:2 * D_FF] = (dav * gv * sg).astype(o_ref.dtype)

    return pl.pallas_call(
        body, name=name, grid=(s // t,),
        in_specs=[_rows(t, D_FF, 0), _rows(t, D_FF, 1), _rows(t, D_FF)], out_specs=_rows(t, 2 * D_FF),
        out_shape=jax.ShapeDtypeStruct((s, 2 * D_FF), MXU_DTYPE),
        compiler_params=_params(("parallel",), 14 * _nbytes((t, D_FF), F32)),
    )(hf, hf, dact)


def _ple_fwd(x, lg, pe, *, name):
    s, d = x.shape
    t = _tile(s, 512)

    def body(x_ref, lg_ref, pe_ref, o_ref):
        o_ref[...] = x_ref[...] + _sigmoid(lg_ref[...]) * pe_ref[...]

    return pl.pallas_call(
        body, name=name, grid=(s // t,),
        in_specs=[_rows(t, d)] * 3, out_specs=_rows(t, d), out_shape=jax.ShapeDtypeStruct((s, d), F32),
        compiler_params=_params(("parallel",), 12 * _nbytes((t, d), F32)),
    )(x, lg, pe)


def _ple_bwd(dx, lg, pe, *, name):
    s, d = dx.shape
    t = _tile(s, 512)

    def body(dx_ref, lg_ref, pe_ref, dpe_ref, dlg_ref):
        dxv = dx_ref[...]
        sg = _sigmoid(lg_ref[...])
        dpe_ref[...] = (dxv * sg).astype(dpe_ref.dtype)
        dlg_ref[...] = (dxv * pe_ref[...] * sg * (1.0 - sg)).astype(dlg_ref.dtype)

    return pl.pallas_call(
        body, name=name, grid=(s // t,),
        in_specs=[_rows(t, d)] * 3, out_specs=[_rows(t, d)] * 2,
        out_shape=[jax.ShapeDtypeStruct((s, d), MXU_DTYPE)] * 2,
        compiler_params=_params(("parallel",), 14 * _nbytes((t, d), F32)),
    )(dx, lg, pe)


def _adamw(parts, w, m, v, *, name):
    rows, lanes = w.shape
    t = math.gcd(rows, 160)
    assert rows % t == 0 and t % V7X_SUBLANES == 0
    c1 = 1.0 / (1.0 - ADAM_B1 ** ADAM_STEP)
    c2 = 1.0 / (1.0 - ADAM_B2 ** ADAM_STEP)

    def body(p_ref, w_ref, m_ref, v_ref, g_ref, d_ref, nm_ref, nv_ref):
        g = p_ref[0].astype(F32)
        for j in range(1, N_DEV):
            g = g + p_ref[j].astype(F32)
        m2 = ADAM_B1 * m_ref[...] + (1.0 - ADAM_B1) * g
        v2 = ADAM_B2 * v_ref[...] + (1.0 - ADAM_B2) * (g * g)
        g_ref[...] = g
        nm_ref[...] = m2
        nv_ref[...] = v2
        d_ref[...] = -ADAM_LR * ((m2 * c1) / (jnp.sqrt(v2 * c2) + ADAM_EPS) + ADAM_WD * w_ref[...])

    blk = _rows(t, lanes)
    return pl.pallas_call(
        body, name=name, grid=(rows // t,),
        in_specs=[pl.BlockSpec((N_DEV, t, lanes), lambda i: (0, i, 0)), blk, blk, blk], out_specs=[blk] * 4,
        out_shape=[jax.ShapeDtypeStruct((rows, lanes), F32)] * 4,
        compiler_params=_params(("parallel",), 40 * _nbytes((t, lanes), F32)),
    )(parts, w, m, v)


def _mesh_pos():
    return lax.axis_index("x"), lax.axis_index("y"), lax.axis_index("c")


def _all_gather(blk, *, name):
    r, c_dim = blk.shape

    def body(x_ref, out_ref, send_sems, recv_sems, local_sem):
        x, y, c = _mesh_pos()
        me, sibling = (x, y, c), (x, y, 1 - c)
        chips = [(1 - x, y), (x, 1 - y), (1 - x, 1 - y)]

        def slot(px, py, pc):
            return out_ref.at[4 * px + 2 * py + pc]

        def copy(k, block, to, src=None):
            return pltpu.make_async_remote_copy(
                src_ref=slot(*block) if src is None else src, dst_ref=slot(*block),
                send_sem=send_sems.at[k], recv_sem=recv_sems.at[k],
                device_id=to, device_id_type=pl.DeviceIdType.MESH)

        mine = pltpu.make_async_copy(x_ref, slot(*me), local_sem)
        mine.start()
        first = [copy(0, me, sibling, src=x_ref)]
        first += [copy(1 + j, me, (*chip, c), src=x_ref) for j, chip in enumerate(chips)]
        for cp in first:
            cp.start()
        passed = [copy(4 + j, (*chip, c), sibling) for j, chip in enumerate(chips)]
        for j, chip in enumerate(chips):
            copy(1 + j, (*chip, c), me).wait_recv()
            passed[j].start()
        copy(0, sibling, me).wait_recv()
        for j, chip in enumerate(chips):
            copy(4 + j, (*chip, 1 - c), me).wait_recv()
        for cp in first + passed:
            cp.wait_send()
        mine.wait()

    return pl.pallas_call(
        body, name=name,
        out_shape=jax.ShapeDtypeStruct((N_DEV, r, c_dim), blk.dtype),
        in_specs=[pl.BlockSpec(memory_space=pl.ANY)], out_specs=pl.BlockSpec(memory_space=pl.ANY),
        scratch_shapes=[pltpu.SemaphoreType.DMA((7,)), pltpu.SemaphoreType.DMA((7,)), pltpu.SemaphoreType.DMA],
    )(blk)


def _all_to_all(pay, *, name):
    _, r, c_dim = pay.shape

    def body(in_ref, out_ref, send_sems, recv_sems, local_sem):
        x, y, c = _mesh_pos()
        me = 4 * x + 2 * y + c
        local = pltpu.make_async_copy(in_ref.at[me], out_ref.at[me], local_sem)
        local.start()
        copies = []
        for k in range(1, N_DEV):
            px = 1 - x if k & 4 else x
            py = 1 - y if k & 2 else y
            pc = 1 - c if k & 1 else c
            copies.append(pltpu.make_async_remote_copy(
                src_ref=in_ref.at[4 * px + 2 * py + pc], dst_ref=out_ref.at[me],
                send_sem=send_sems.at[k - 1], recv_sem=recv_sems.at[k - 1],
                device_id=(px, py, pc), device_id_type=pl.DeviceIdType.MESH))
        for cp in copies:
            cp.start()
        for cp in copies:
            cp.wait()
        local.wait()

    return pl.pallas_call(
        body, name=name,
        out_shape=jax.ShapeDtypeStruct((N_DEV, r, c_dim), pay.dtype),
        in_specs=[pl.BlockSpec(memory_space=pl.ANY)], out_specs=pl.BlockSpec(memory_space=pl.ANY),
        scratch_shapes=[pltpu.SemaphoreType.DMA((7,)), pltpu.SemaphoreType.DMA((7,)), pltpu.SemaphoreType.DMA],
    )(pay)


def _flat_rows(parts, row_multiple):
    flat = jnp.concatenate([p.reshape(-1) for p in parts])
    chunk = PAYLOAD_LANES * row_multiple
    total = -(-flat.shape[0] // chunk) * chunk
    return jnp.pad(flat, (0, total - flat.shape[0])).reshape(total // PAYLOAD_LANES, PAYLOAD_LANES)


def _split_flat(flat, shapes):
    out, off = [], 0
    flat = flat.reshape(-1)
    for shp in shapes:
        n = math.prod(shp)
        out.append(flat[off:off + n].reshape(shp))
        off += n
    return out


def _pad_w_in(w):
    pieces, cursor = [], 0
    for _, off, width, pad_off, _ in SEGS:
        if pad_off > cursor:
            pieces.append(jnp.zeros(w.shape[:-1] + (pad_off - cursor,), w.dtype))
        pieces.append(w[..., off:off + width])
        cursor = pad_off + width
    pieces.append(jnp.zeros(w.shape[:-1] + (D_IN_PAD - cursor,), w.dtype))
    return jnp.concatenate(pieces, axis=-1)


def _unpad_w_in(w):
    return jnp.concatenate([w[..., pad_off:pad_off + width] for _, _, width, pad_off, _ in SEGS], axis=-1)


def _heads(a, hd):
    return a.reshape(a.shape[0], HEADS, hd).transpose(1, 0, 2)


def _unheads(a):
    return a.transpose(1, 0, 2).reshape(a.shape[1], -1)


def _block_diag(w):
    eye = jnp.eye(LRU_HEADS, dtype=w.dtype)
    return (eye[:, None, :, None] * w[:, :, None, :]).reshape(LRU_WIDTH, LRU_WIDTH)


def _diag_blocks(w):
    w4 = w.reshape(LRU_HEADS, LRU_HEAD_DIM, LRU_HEADS, LRU_HEAD_DIM)
    return jnp.stack([w4[h, :, h, :] for h in range(LRU_HEADS)])


def _lane_pad(a, width):
    return jnp.pad(a, ((0, 0), (0, width - a.shape[-1])))


def _layer_fwd(x, p_i, wts, tabs, tag):
    n = functools.partial(lambda base, t=tag: f"{base}_{t}")
    sv = {"x": x}
    n1 = _rms_fwd(x, wts["mix_norm"], width=D_MODEL, name=n("mix_norm_fwd"))
    z, z16 = _mm(n1, wts["w_in"], also_mxu=True, name=n("w_in_fwd"))
    sv.update(n1=n1, z=z, z16=z16)
    lanes = V7X_LANES

    ya_pre, hseq = _lru_fwd(z, wts["conv_w"], wts["conv_b"], wts["lru_wa"], wts["lru_ba"], wts["lru_wx"],
                            wts["lru_bx"], wts["lru_lambda"], name=n("lru_fwd"))
    ya = _mm(ya_pre, wts["w_br_a"], name=n("br_a_fwd"))
    sv.update(ya_pre=ya_pre, hseq=hseq, ya=ya)

    cqn = _rms_fwd(z, wts["mla_q_norm"], width=MLA_Q_LORA, cb=SEG["cq"][3] // MLA_Q_LORA, name=n("q_norm_fwd"))
    ckvn = _rms_fwd(z, wts["mla_kv_norm"], width=MLA_KV_LORA, cb=SEG["ckv"][3] // MLA_KV_LORA,
                    name=n("kv_norm_fwd"))
    qp, qp16 = _mm(cqn, wts["mla_wuq"], also_mxu=True, name=n("wuq_fwd"))
    kv = _mm(ckvn, wts["mla_wukv"], out_dtype=MXU_DTYPE, name=n("wukv_fwd"))
    q_rot = _rope_lanes(qp, tabs["cos128"], tabs["sin128"], cb=0, groups=HEADS, out_dtype=MXU_DTYPE,
                        name=n("q_rope_fwd"))
    k_rot = _rope_lanes(z, tabs["cos128"], tabs["sin128"], cb=SEG["kr"][3] // lanes, groups=1, out_dtype=MXU_DTYPE,
                        name=n("k_rope_fwd"))
    mla_ops = (qp16, HEADS, kv, 0, kv, HEAD_PAIRS)
    ob_flat, lse_b = _pair_fwd(*mla_ops, (q_rot, k_rot), None, scale=(MLA_NOPE + MLA_ROPE) ** -0.5, unit=CHUNK,
                               name=n("mla_attn_fwd"))
    yb = _mm(ob_flat, wts["w_br_b"], name=n("br_b_fwd"))
    sv.update(cqn=cqn, ckvn=ckvn, mla_ops=mla_ops, mla_rot=(q_rot, k_rot), lse_b=lse_b, ob_flat=ob_flat, yb=yb)

    cum = _fox_cum(z, wts["fox_bf"], name=n("fox_cum_fwd"))
    cum_h = cum[:, :HEADS].T
    fox_decay = (cum_h[:, :, None], cum_h[:, None, :])
    fox_ops = (z16, SEG["fq"][3] // lanes, z16, SEG["fk"][3] // lanes, z16, SEG["fv"][3] // lanes)
    oc_flat, lse_c = _pair_fwd(*fox_ops, None, fox_decay, scale=FOX_HEAD_DIM ** -0.5, unit=1, name=n("fox_attn_fwd"))
    yc = _mm(oc_flat, wts["w_br_c"], name=n("br_c_fwd"))
    sv.update(fox_ops=fox_ops, fox_decay=fox_decay, lse_c=lse_c, oc_flat=oc_flat, yc=yc)

    merged = _merge_fwd(ya, yb, yc, z, wts["gate_b"], name=n("merge_fwd"))
    x1 = _mm(merged, wts["w_o"], res=x, name=n("w_o_fwd"))
    n2 = _rms_fwd(x1, wts["ffn_norm"], width=D_MODEL, name=n("ffn_norm_fwd"))
    hf = _mm(n2, wts["w_gate_up"], name=n("gate_up_fwd"))
    act = _swiglu_fwd(hf, name=n("swiglu_fwd"))
    x2 = _mm(act, wts["w_down"], res=x1, name=n("down_fwd"))
    n3 = _rms_fwd(x2, wts["ple_norm"], width=D_MODEL, name=n("ple_norm_fwd"))
    lg = _mm(n3, wts["w_ple_gate"], name=n("ple_gate_fwd"))
    pe = _mm(p_i, wts["w_ple"], name=n("ple_fwd_mm"))
    x3 = _ple_fwd(x2, lg, pe, name=n("ple_fwd"))
    sv.update(merged=merged, x1=x1, n2=n2, hf=hf, act=act, x2=x2, n3=n3, lg=lg, pe=pe, p_i=p_i)
    return x3, sv


def _layer_bwd(dx3, sv, wts, tabs, tag):
    n = functools.partial(lambda base, t=tag: f"{base}_{t}")
    gr = {}
    z = sv["z"]
    s = z.shape[0]

    dpe, dlg = _ple_bwd(dx3, sv["lg"], sv["pe"], name=n("ple_bwd"))
    gr["w_ple"] = _mm(sv["p_i"], dpe, ta=True, name=n("ple_dw"))
    gr["w_ple_gate"] = _mm(sv["n3"], dlg, ta=True, name=n("ple_gate_dw"))
    dn3 = _mm(dlg, wts["w_ple_gate"], tb=True, name=n("ple_gate_dx"))
    dx2, gr["ple_norm"] = _rms_bwd(sv["x2"], wts["ple_norm"], dn3, width=D_MODEL, res=dx3, name=n("ple_norm_bwd"))

    dact = _mm(dx2, wts["w_down"], tb=True, name=n("down_dx"))
    gr["w_down"] = _mm(sv["act"], dx2, ta=True, name=n("down_dw"))
    dhf = _swiglu_bwd(sv["hf"], dact, name=n("swiglu_bwd"))
    gr["w_gate_up"] = _mm(sv["n2"], dhf, ta=True, name=n("gate_up_dw"))
    dn2 = _mm(dhf, wts["w_gate_up"], tb=True, name=n("gate_up_dx"))
    dx1, gr["ffn_norm"] = _rms_bwd(sv["x1"], wts["ffn_norm"], dn2, width=D_MODEL, res=dx2, name=n("ffn_norm_bwd"))

    dmerged = _mm(dx1, wts["w_o"], tb=True, name=n("w_o_dx"))
    gr["w_o"] = _mm(sv["merged"], dx1, ta=True, name=n("w_o_dw"))
    dya, dyb, dyc, dgl, gr["gate_b"] = _merge_bwd(dmerged, sv["ya"], sv["yb"], sv["yc"], z, wts["gate_b"],
                                                  name=n("merge_bwd"))
    gr["w_br_a"] = _mm(sv["ya_pre"], dya, ta=True, name=n("br_a_dw"))
    gr["w_br_b"] = _mm(sv["ob_flat"], dyb, ta=True, name=n("br_b_dw"))
    gr["w_br_c"] = _mm(sv["oc_flat"], dyc, ta=True, name=n("br_c_dw"))
    dya_pre = _mm(dya, wts["w_br_a"], tb=True, name=n("br_a_dx"))
    dob = _mm(dyb, wts["w_br_b"], tb=True, out_dtype=MXU_DTYPE, name=n("br_b_dx"))
    doc = _mm(dyc, wts["w_br_c"], tb=True, out_dtype=MXU_DTYPE, name=n("br_c_dx"))

    (dz_a, gr["conv_w"], gr["conv_b"], dwa, gr["lru_ba"], dwx, gr["lru_bx"], gr["lru_lambda"]) = _lru_bwd(
        z, sv["hseq"], dya_pre, wts["conv_w"], wts["conv_b"], wts["lru_wa"], wts["lru_ba"], wts["lru_wx"],
        wts["lru_bx"], wts["lru_lambda"], name=n("lru_bwd"))
    gr["lru_wa"], gr["lru_wx"] = _diag_blocks(dwa), _diag_blocks(dwx)

    scale_b = (MLA_NOPE + MLA_ROPE) ** -0.5
    dq_nope, delta_b, dq_rot = _pair_bwd_q(*sv["mla_ops"], dob, sv["lse_b"], sv["mla_rot"], None,
                                           scale=scale_b, unit=CHUNK, name=n("mla_attn_dq"))
    dk_nope, dv_mla, dk_rot = _pair_bwd_kv(*sv["mla_ops"], dob, sv["lse_b"].reshape(HEADS, 1, s),
                                           delta_b.reshape(HEADS, 1, s), sv["mla_rot"], None,
                                           scale=scale_b, unit=CHUNK, name=n("mla_attn_dkv"))
    dq_rope = _rope_lanes(dq_rot, tabs["cos128"], -tabs["sin128"], cb=0, groups=HEADS, out_dtype=MXU_DTYPE,
                          name=n("q_rope_bwd"))
    dk_rope = _rope_lanes(dk_rot, tabs["cos128"], -tabs["sin128"], cb=0, groups=1, out_dtype=MXU_DTYPE,
                          sum_parts=HEAD_PAIRS, name=n("k_rope_bwd"))
    dqp = jnp.concatenate([dq_rope, dq_nope], axis=-1)
    dkv = jnp.concatenate([dk_nope, dv_mla], axis=-1)
    gr["mla_wuq"] = _mm(sv["cqn"], dqp, ta=True, name=n("wuq_dw"))
    gr["mla_wukv"] = _mm(sv["ckvn"], dkv, ta=True, name=n("wukv_dw"))
    dcqn = _mm(dqp, wts["mla_wuq"], tb=True, name=n("wuq_dx"))
    dckvn = _mm(dkv, wts["mla_wukv"], tb=True, name=n("wukv_dx"))
    dcq, gr["mla_q_norm"] = _rms_bwd(z, wts["mla_q_norm"], dcqn, width=MLA_Q_LORA, cb=SEG["cq"][3] // MLA_Q_LORA,
                                     out_dtype=MXU_DTYPE, name=n("q_norm_bwd"))
    dckv, gr["mla_kv_norm"] = _rms_bwd(z, wts["mla_kv_norm"], dckvn, width=MLA_KV_LORA,
                                       cb=SEG["ckv"][3] // MLA_KV_LORA, out_dtype=MXU_DTYPE, name=n("kv_norm_bwd"))

    scale_c = FOX_HEAD_DIM ** -0.5
    dfq, delta_c = _pair_bwd_q(*sv["fox_ops"], doc, sv["lse_c"], None, sv["fox_decay"],
                               scale=scale_c, unit=1, name=n("fox_attn_dq"))
    dfk, dfv, dcum = _pair_bwd_kv(*sv["fox_ops"], doc, sv["lse_c"].reshape(HEADS, 1, s),
                                  delta_c.reshape(HEADS, 1, s), None, sv["fox_decay"],
                                  scale=scale_c, unit=1, name=n("fox_attn_dkv"))
    dcum_rows = _lane_pad(dcum.reshape(HEADS, s).T, V7X_LANES)
    dfl, dbf = _fox_cum_bwd(z, wts["fox_bf"], dcum_rows, name=n("fox_cum_bwd"))
    gr["fox_bf"] = dbf[:, :HEADS]

    zero = lambda width: jnp.zeros((s, width), MXU_DTYPE)
    dz = jnp.concatenate([dz_a, zero(128), dcq, dckv, dk_rope, zero(128), dfq, dfk, dfv, dfl, zero(384), dgl],
                         axis=-1)
    gr["w_in"] = _mm(sv["n1"], dz, ta=True, name=n("w_in_dw"))
    dn1 = _mm(dz, wts["w_in"], tb=True, name=n("w_in_dx"))
    dx, gr["mix_norm"] = _rms_bwd(sv["x"], wts["mix_norm"], dn1, width=D_MODEL, res=dx1, name=n("mix_norm_bwd"))
    return dx, gr


def _rope_tables(s):
    pos = jnp.arange(s, dtype=F32)
    inv_freq = ROPE_BASE ** (-jnp.arange(0, MLA_ROPE, 2, dtype=F32) / MLA_ROPE)
    ang = pos[:, None] * inv_freq[None, :]
    cos, sin = jnp.cos(ang), jnp.sin(ang)
    cos32 = jnp.concatenate([cos, cos], axis=-1)
    sin32 = jnp.concatenate([-sin, sin], axis=-1)
    return {"cos256": jnp.tile(cos32, (1, 8)), "sin256": jnp.tile(sin32, (1, 8)),
            "cos128": jnp.tile(cos32, (1, 4)), "sin128": jnp.tile(sin32, (1, 4))}


def _gather_weights(shards):
    names = [nm for nm, _ in SHARDED if nm != "conv_w"]
    pay = _flat_rows([shards[nm].astype(MXU_DTYPE) for nm in names], 16)
    got = _all_gather(pay, name="weights_all_gather").reshape(N_DEV, -1)
    full, off = {}, 0
    axes = dict(SHARDED)
    for nm in names:
        shp = shards[nm].shape
        cnt = math.prod(shp)
        blk = got[:, off:off + cnt].reshape((N_DEV,) + shp)
        off += cnt
        if axes[nm] == 2:
            full[nm] = blk.transpose(1, 2, 0, 3).reshape(shp[0], shp[1], N_DEV * shp[2])
        else:
            full[nm] = blk.transpose(1, 0, 2, 3).reshape(shp[0], N_DEV * shp[1], shp[2])
    cw = _all_gather(_flat_rows([shards["conv_w"]], 8), name="conv_w_all_gather").reshape(N_DEV, -1)
    shp = shards["conv_w"].shape
    full["conv_w"] = cw[:, :math.prod(shp)].reshape((N_DEV,) + shp).transpose(1, 2, 0, 3).reshape(
        shp[0], shp[1], N_DEV * shp[2])
    return full


def _to_dest_major(g, axis):
    d0, r, c = g.shape
    if axis == 2:
        return g.reshape(d0, r, N_DEV, c // N_DEV).transpose(2, 0, 1, 3).reshape(N_DEV, -1)
    return g.reshape(d0, N_DEV, r // N_DEV, c).transpose(1, 0, 2, 3).reshape(N_DEV, -1)


def kernel(x, p, mix_norm, w_in, gate_b, conv_w, conv_b, lru_wa, lru_ba, lru_wx, lru_bx, lru_lambda, mla_q_norm, mla_wuq, mla_kv_norm, mla_wukv, fox_bf, w_br_a, w_br_b, w_br_c, w_o, ffn_norm, w_gate_up, w_down, ple_norm, w_ple_gate, w_ple, final_norm, loss_target, m_mix_norm, m_w_in, m_gate_b, m_conv_w, m_conv_b, m_lru_wa, m_lru_ba, m_lru_wx, m_lru_bx, m_lru_lambda, m_mla_q_norm, m_mla_wuq, m_mla_kv_norm, m_mla_wukv, m_fox_bf, m_w_br_a, m_w_br_b, m_w_br_c, m_w_o, m_ffn_norm, m_w_gate_up, m_w_down, m_ple_norm, m_w_ple_gate, m_w_ple, m_final_norm, v_mix_norm, v_w_in, v_gate_b, v_conv_w, v_conv_b, v_lru_wa, v_lru_ba, v_lru_wx, v_lru_bx, v_lru_lambda, v_mla_q_norm, v_mla_wuq, v_mla_kv_norm, v_mla_wukv, v_fox_bf, v_w_br_a, v_w_br_b, v_w_br_c, v_w_o, v_ffn_norm, v_w_gate_up, v_w_down, v_ple_norm, v_w_ple_gate, v_w_ple, v_final_norm):
    given = dict(locals())
    w_loc = {nm: given[nm] for nm in WEIGHTS}
    m_loc = {nm: given["m_" + nm] for nm in WEIGHTS}
    v_loc = {nm: given["v_" + nm] for nm in WEIGHTS}
    xs = x[0]
    s = xs.shape[0]
    tabs = _rope_tables(s)

    full = _gather_weights({nm: w_loc[nm] for nm, _ in SHARDED})
    full["w_in"] = _pad_w_in(full["w_in"])
    wq = full["mla_wuq"].reshape(DEPTH, MLA_Q_LORA, HEADS, MLA_NOPE + MLA_ROPE)
    wq_rot = jnp.pad(wq[..., MLA_NOPE:], ((0, 0), (0, 0), (0, 0), (0, V7X_LANES - MLA_ROPE)))
    full["mla_wuq"] = jnp.concatenate([wq_rot.reshape(DEPTH, MLA_Q_LORA, -1),
                                       wq[..., :MLA_NOPE].reshape(DEPTH, MLA_Q_LORA, -1)], axis=-1)
    wkv = full["mla_wukv"].reshape(DEPTH, MLA_KV_LORA, HEADS, MLA_NOPE + MLA_V)
    full["mla_wukv"] = jnp.concatenate([wkv[..., :MLA_NOPE].reshape(DEPTH, MLA_KV_LORA, -1),
                                        wkv[..., MLA_NOPE:].reshape(DEPTH, MLA_KV_LORA, -1)], axis=-1)

    def layer_weights(i):
        wts = {nm: full[nm][i] for nm, _ in SHARDED}
        for nm in ("mix_norm", "gate_b", "conv_b", "lru_ba", "lru_bx", "lru_lambda", "mla_q_norm", "mla_kv_norm",
                   "ffn_norm", "ple_norm"):
            wts[nm] = w_loc[nm][i][None, :]
        wts["fox_bf"] = _lane_pad(w_loc["fox_bf"][i][None, :], V7X_LANES)
        wts["lru_wa"] = _block_diag(w_loc["lru_wa"][i]).astype(MXU_DTYPE)
        wts["lru_wx"] = _block_diag(w_loc["lru_wx"][i]).astype(MXU_DTYPE)
        return wts

    layers = [layer_weights(i) for i in range(DEPTH)]

    h = xs
    saved = []
    for i in range(DEPTH):
        h, sv = _layer_fwd(h, p[i, 0].astype(MXU_DTYPE), layers[i], tabs, f"l{i}")
        saved.append(sv)
    loss_blk, dh, dg_final = _final_loss(h, w_loc["final_norm"][None, :], loss_target[0], name="final_loss")
    loss = lax.psum(loss_blk[0, 0], ("x", "y", "c"))

    grads = [None] * DEPTH
    for i in reversed(range(DEPTH)):
        dh, grads[i] = _layer_bwd(dh, saved[i], layers[i], tabs, f"l{i}")
    grad_x = dh[None]

    def stacked(nm):
        return jnp.stack([grads[i][nm] for i in range(DEPTH)])

    gfull = {}
    for nm, _ in SHARDED:
        gfull[nm] = stacked(nm)
    gfull["w_in"] = _unpad_w_in(gfull["w_in"])
    gq = gfull["mla_wuq"]
    rot_w = HEADS * V7X_LANES
    gfull["mla_wuq"] = jnp.concatenate(
        [gq[..., rot_w:].reshape(DEPTH, MLA_Q_LORA, HEADS, MLA_NOPE),
         gq[..., :rot_w].reshape(DEPTH, MLA_Q_LORA, HEADS, V7X_LANES)[..., :MLA_ROPE]],
        axis=-1).reshape(DEPTH, MLA_Q_LORA, -1)
    gkv = gfull["mla_wukv"]
    gfull["mla_wukv"] = jnp.concatenate(
        [gkv[..., :512].reshape(DEPTH, MLA_KV_LORA, HEADS, MLA_NOPE),
         gkv[..., 512:].reshape(DEPTH, MLA_KV_LORA, HEADS, MLA_V)], axis=-1).reshape(DEPTH, MLA_KV_LORA, -1)

    pay = jnp.concatenate([_to_dest_major(gfull[nm], ax) for nm, ax in SHARDED], axis=1)
    chunk = PAYLOAD_LANES * 16
    total = -(-pay.shape[1] // chunk) * chunk
    pay = jnp.pad(pay, ((0, 0), (0, total - pay.shape[1]))).astype(MXU_DTYPE).reshape(N_DEV, -1, PAYLOAD_LANES)
    parts = _all_to_all(pay, name="grads_all_to_all")
    names_s = [nm for nm, _ in SHARDED]
    shapes_s = [w_loc[nm].shape for nm in names_s]
    outs_s = _adamw(parts, _flat_rows([w_loc[nm] for nm in names_s], 16),
                    _flat_rows([m_loc[nm] for nm in names_s], 16),
                    _flat_rows([v_loc[nm] for nm in names_s], 16), name="adamw_sharded")
    res_s = [dict(zip(names_s, _split_flat(o, shapes_s))) for o in outs_s]

    small = {nm: stacked(nm) for nm in REPLICATED if nm != "final_norm"}
    small["final_norm"] = dg_final
    names_r = list(REPLICATED)
    shapes_r = [w_loc[nm].shape for nm in names_r]
    parts_r = _all_gather(_flat_rows([small[nm] for nm in names_r], 8), name="small_grads_all_gather")
    outs_r = _adamw(parts_r, _flat_rows([w_loc[nm] for nm in names_r], 8),
                    _flat_rows([m_loc[nm] for nm in names_r], 8),
                    _flat_rows([v_loc[nm] for nm in names_r], 8), name="adamw_replicated")
    res_r = [dict(zip(names_r, _split_flat(o, shapes_r))) for o in outs_r]

    out = [loss, grad_x]
    for kind in range(4):
        for nm in WEIGHTS:
            out.append(res_s[kind][nm] if nm in res_s[kind] else res_r[kind][nm])
    return tuple(out)
```

```python
import functools
import math

import jax
import jax.numpy as jnp
from jax import lax
from jax.experimental import pallas as pl
from jax.experimental.pallas import tpu as pltpu

F32 = jnp.float32
BF16 = jnp.bfloat16
MXU_DTYPE = jnp.bfloat16

D_MODEL = 1024
DEPTH = 2
CHUNK = 64
EPS = 1e-6
NEG_INF = -1e30
LRU_WIDTH = 512
LRU_HEADS = 8
LRU_HEAD_DIM = 64
CONV_WIDTH = 4
LRU_C = 8.0
HEADS = 8
MLA_Q_LORA = 384
MLA_KV_LORA = 256
MLA_NOPE = 64
MLA_ROPE = 32
MLA_V = 64
ROPE_BASE = 10000.0
FOX_HEAD_DIM = 64
FOX_WIDTH = 512
D_FF = 2816
PLE_DIM = 256
D_IN = 6312
ADAM_LR = 0.001
ADAM_B1 = 0.9
ADAM_B2 = 0.999
ADAM_EPS = 1e-08
ADAM_WD = 0.01
ADAM_STEP = 10

V7X_VMEM_BYTES = 64 * 1024 * 1024
V7X_LANES = 128
V7X_SUBLANES = 8
VMEM_LIMIT_CAP = 56 * 1024 * 1024
N_DEV = 8

SEGS = (
    ("u", 0, 512, 0, 512),
    ("ug", 512, 512, 512, 512),
    ("cq", 1024, 384, 1152, 384),
    ("ckv", 1408, 256, 1536, 256),
    ("kr", 1664, 32, 1792, 128),
    ("fq", 1696, 512, 2048, 512),
    ("fk", 2208, 512, 2560, 512),
    ("fv", 2720, 512, 3072, 512),
    ("fl", 3232, 8, 3584, 128),
    ("gate", 3240, 3072, 4096, 1024),
)
D_IN_PAD = 7168
SEG = {s[0]: s for s in SEGS}

SHARDED = (("w_in", 2), ("mla_wuq", 2), ("mla_wukv", 2), ("w_br_a", 2), ("w_br_b", 2), ("w_br_c", 2),
           ("w_o", 1), ("w_gate_up", 2), ("w_down", 1), ("w_ple_gate", 1), ("w_ple", 2), ("conv_w", 2))
REPLICATED = ("mix_norm", "gate_b", "conv_b", "lru_wa", "lru_ba", "lru_wx", "lru_bx", "lru_lambda",
              "mla_q_norm", "mla_kv_norm", "fox_bf", "ffn_norm", "ple_norm", "final_norm")
WEIGHTS = ("mix_norm", "w_in", "gate_b", "conv_w", "conv_b", "lru_wa", "lru_ba", "lru_wx", "lru_bx",
           "lru_lambda", "mla_q_norm", "mla_wuq", "mla_kv_norm", "mla_wukv", "fox_bf", "w_br_a", "w_br_b",
           "w_br_c", "w_o", "ffn_norm", "w_gate_up", "w_down", "ple_norm", "w_ple_gate", "w_ple", "final_norm")
PAYLOAD_LANES = 1024


def _tile(n, cap=1024):
    best = None
    for t in range(V7X_LANES, min(n, cap) + 1, V7X_LANES):
        if n % t == 0:
            best = t
    return best if best is not None else n


def _row_tile(s, pref):
    t = min(pref, s // 2)
    assert s % t == 0 and t % V7X_SUBLANES == 0
    return t


def _nbytes(shape, dtype):
    return math.prod(shape) * jnp.dtype(dtype).itemsize


def _params(sem, vmem_bytes):
    limit = int(min(VMEM_LIMIT_CAP, max(16 * 1024 * 1024, vmem_bytes)))
    return pltpu.CompilerParams(dimension_semantics=sem, vmem_limit_bytes=limit)


def _full(shape):
    return pl.BlockSpec(shape, lambda *_: (0,) * len(shape))


def _rows(t, w, cb=0):
    return pl.BlockSpec((t, w), lambda i: (i, cb))


def _mxu(v):
    return v.astype(MXU_DTYPE)


def _dot(a, b):
    return lax.dot_general(_mxu(a), _mxu(b), (((1,), (0,)), ((), ())), preferred_element_type=F32)


def _dot_nt(a, b):
    return lax.dot_general(_mxu(a), _mxu(b), (((1,), (1,)), ((), ())), preferred_element_type=F32)


def _dot_tn(a, b):
    return lax.dot_general(_mxu(a), _mxu(b), (((0,), (0,)), ((), ())), preferred_element_type=F32)


def _sigmoid(v):
    return 1.0 / (1.0 + jnp.exp(-v))


def _softplus(v):
    return jnp.maximum(v, 0.0) + jnp.log(1.0 + jnp.exp(-jnp.abs(v)))


def _neg_expm1(v):
    series = -v * (1.0 + v * (0.5 + v * (1.0 / 6.0 + v * (1.0 / 24.0))))
    return jnp.where(v > -0.03, series, 1.0 - jnp.exp(v))


_GELU_C = math.sqrt(2.0 / math.pi)
_GELU_A = 0.044715


def _gelu(v):
    t = jnp.tanh(_GELU_C * (v + _GELU_A * v * v * v))
    return 0.5 * v * (1.0 + t)


def _gelu_grad(v):
    t = jnp.tanh(_GELU_C * (v + _GELU_A * v * v * v))
    return 0.5 * (1.0 + t) + 0.5 * v * (1.0 - t * t) * _GELU_C * (1.0 + 3.0 * _GELU_A * v * v)


def _mm(a, b, *, ta=False, tb=False, out_dtype=F32, res=None, also_mxu=False, name):
    k_dim, m_dim = (a.shape[0], a.shape[1]) if ta else (a.shape[1], a.shape[0])
    n_dim = b.shape[0] if tb else b.shape[1]
    assert (b.shape[1] if tb else b.shape[0]) == k_dim
    tm, tn, tk = _tile(m_dim), _tile(n_dim, 1408), _tile(k_dim, 1408)
    nk = k_dim // tk
    a_spec = pl.BlockSpec((tk, tm), lambda i, j, k: (k, i)) if ta else pl.BlockSpec((tm, tk), lambda i, j, k: (i, k))
    b_spec = pl.BlockSpec((tn, tk), lambda i, j, k: (j, k)) if tb else pl.BlockSpec((tk, tn), lambda i, j, k: (k, j))
    o_spec = pl.BlockSpec((tm, tn), lambda i, j, k: (i, j))
    has_res = res is not None

    def body(*refs):
        a_ref, b_ref = refs[0], refs[1]
        res_ref = refs[2] if has_res else None
        o_ref = refs[3] if has_res else refs[2]
        o2_ref = refs[-2] if also_mxu else None
        acc_ref = refs[-1]
        k = pl.program_id(2)
        if ta:
            part = _dot_tn(a_ref[...], b_ref[...])
        elif tb:
            part = _dot_nt(a_ref[...], b_ref[...])
        else:
            part = _dot(a_ref[...], b_ref[...])

        def finish(total):
            if has_res:
                total = total + res_ref[...].astype(F32)
            o_ref[...] = total.astype(o_ref.dtype)
            if also_mxu:
                o2_ref[...] = total.astype(o2_ref.dtype)

        if nk == 1:
            finish(part)
        else:
            @pl.when(k == 0)
            def _():
                acc_ref[...] = part

            @pl.when(jnp.logical_and(k > 0, k < nk - 1))
            def _():
                acc_ref[...] += part

            @pl.when(k == nk - 1)
            def _():
                finish(acc_ref[...] + part)

    ins = [a, b] + ([res] if has_res else [])
    in_specs = [a_spec, b_spec] + ([o_spec] if has_res else [])
    acc_shape = (tm, tn) if nk > 1 else (V7X_SUBLANES, V7X_LANES)
    vmem = (2 * (_nbytes((tm, tk), a.dtype) + _nbytes((tk, tn), b.dtype) + _nbytes((tm, tn), out_dtype)
                 + (_nbytes((tm, tn), res.dtype) if has_res else 0))
            + _nbytes((tm, tk), MXU_DTYPE) + _nbytes((tk, tn), MXU_DTYPE) + 3 * _nbytes((tm, tn), F32))
    return pl.pallas_call(
        body, name=name, grid=(m_dim // tm, n_dim // tn, nk),
        in_specs=in_specs, out_specs=[o_spec, o_spec] if also_mxu else o_spec,
        out_shape=([jax.ShapeDtypeStruct((m_dim, n_dim), out_dtype), jax.ShapeDtypeStruct((m_dim, n_dim), MXU_DTYPE)]
                   if also_mxu else jax.ShapeDtypeStruct((m_dim, n_dim), out_dtype)),
        scratch_shapes=[pltpu.VMEM(acc_shape, F32)],
        compiler_params=_params(("parallel", "parallel", "arbitrary"), vmem),
    )(*ins)


def _rms_fwd(x, g, *, width, cb=0, name):
    s = x.shape[0]
    t = _tile(s, 512)

    def body(x_ref, g_ref, o_ref):
        xv = x_ref[...].astype(F32)
        r = lax.rsqrt(jnp.mean(xv * xv, axis=-1, keepdims=True) + EPS)
        o_ref[...] = (xv * r * g_ref[...]).astype(o_ref.dtype)

    return pl.pallas_call(
        body, name=name, grid=(s // t,),
        in_specs=[_rows(t, width, cb), _full((1, width))], out_specs=_rows(t, width),
        out_shape=jax.ShapeDtypeStruct((s, width), MXU_DTYPE),
        compiler_params=_params(("parallel",), 8 * _nbytes((t, width), F32)),
    )(x, g)


def _rms_bwd(x, g, dn, *, width, cb=0, res=None, out_dtype=F32, name):
    s = x.shape[0]
    t = _tile(s, 256)
    has_res = res is not None

    def body(*refs):
        x_ref, g_ref, dn_ref = refs[:3]
        res_ref = refs[3] if has_res else None
        dx_ref, dg_ref = refs[-2], refs[-1]
        xv = x_ref[...].astype(F32)
        dnv = dn_ref[...].astype(F32)
        r = lax.rsqrt(jnp.mean(xv * xv, axis=-1, keepdims=True) + EPS)
        xr = xv * r
        dng = dnv * g_ref[...]
        dx = r * dng - xr * (r * r) * jnp.mean(dng * xv, axis=-1, keepdims=True)
        if has_res:
            dx = dx + res_ref[...].astype(F32)
        dx_ref[...] = dx.astype(dx_ref.dtype)
        part = jnp.sum(dnv * xr, axis=0, keepdims=True)

        @pl.when(pl.program_id(0) == 0)
        def _():
            dg_ref[...] = part

        @pl.when(pl.program_id(0) > 0)
        def _():
            dg_ref[...] += part

    ins = [x, g, dn] + ([res] if has_res else [])
    in_specs = [_rows(t, width, cb), _full((1, width)), _rows(t, width)] + ([_rows(t, width)] if has_res else [])
    return pl.pallas_call(
        body, name=name, grid=(s // t,),
        in_specs=in_specs, out_specs=[_rows(t, width), _full((1, width))],
        out_shape=[jax.ShapeDtypeStruct((s, width), out_dtype), jax.ShapeDtypeStruct((1, width), F32)],
        compiler_params=_params(("arbitrary",), 16 * _nbytes((t, width), F32)),
    )(*ins)


def _final_loss(x, g, target, *, name):
    s, d = x.shape
    t = _tile(s, 256)

    def body(x_ref, g_ref, t_ref, loss_ref, dx_ref, dg_ref):
        xv = x_ref[...]
        r = lax.rsqrt(jnp.mean(xv * xv, axis=-1, keepdims=True) + EPS)
        xr = xv * r
        err = xr * g_ref[...] - t_ref[...]
        part_loss = 0.5 * jnp.sum(jnp.mean(err * err, axis=-1, keepdims=True), axis=0, keepdims=True)
        dnv = err * (1.0 / d)
        dng = dnv * g_ref[...]
        dx_ref[...] = r * dng - xr * (r * r) * jnp.mean(dng * xv, axis=-1, keepdims=True)
        part_dg = jnp.sum(dnv * xr, axis=0, keepdims=True)

        @pl.when(pl.program_id(0) == 0)
        def _():
            dg_ref[...] = part_dg
            loss_ref[...] = jnp.zeros(loss_ref.shape, F32) + part_loss

        @pl.when(pl.program_id(0) > 0)
        def _():
            dg_ref[...] += part_dg
            loss_ref[...] += part_loss

    return pl.pallas_call(
        body, name=name, grid=(s // t,),
        in_specs=[_rows(t, d), _full((1, d)), _rows(t, d)],
        out_specs=[_full((V7X_SUBLANES, V7X_LANES)), _rows(t, d), _full((1, d))],
        out_shape=[jax.ShapeDtypeStruct((V7X_SUBLANES, V7X_LANES), F32), jax.ShapeDtypeStruct((s, d), F32),
                   jax.ShapeDtypeStruct((1, d), F32)],
        compiler_params=_params(("arbitrary",), 16 * _nbytes((t, d), F32)),
    )(x, g, target)


def _shift_down(v, d, fill, rows):
    return jnp.where(rows >= d, pltpu.roll(v, d, 0), fill)


def _shift_up(v, d, fill, rows, t):
    return jnp.where(rows < t - d, pltpu.roll(v, t - d, 0), fill)


def _lru_gates(xc, wa_ref, ba_ref, wx_ref, bx_ref, lam_ref):
    ra = _sigmoid(_dot(xc, wa_ref[...]) + ba_ref[...])
    ig = _sigmoid(_dot(xc, wx_ref[...]) + bx_ref[...])
    sp = _softplus(-lam_ref[...])
    log_a = -LRU_C * ra * sp
    a = jnp.exp(log_a)
    s2 = _neg_expm1(2.0 * log_a)
    return ra, ig, sp, a, s2


def _conv(ubuf, cw_ref, cb_ref, t):
    big = ubuf[...]
    shifted = [pltpu.roll(big, CONV_WIDTH - 1 - k, 0)[V7X_SUBLANES:t + V7X_SUBLANES] if k < CONV_WIDTH - 1
               else big[V7X_SUBLANES:t + V7X_SUBLANES] for k in range(CONV_WIDTH)]
    xc = cb_ref[...] + shifted[0] * cw_ref[0:1, :]
    for k in range(1, CONV_WIDTH):
        xc = xc + shifted[k] * cw_ref[k:k + 1, :]
    return xc, shifted


def _lru_fwd(z, cw, cb, wa, ba, wx, bx, lam, *, name):
    s = z.shape[0]
    w = LRU_WIDTH
    t = _row_tile(s, 256)
    steps = [1 << k for k in range(int(math.log2(t)))]

    def body(u_ref, ug_ref, cw_ref, cb_ref, wa_ref, ba_ref, wx_ref, bx_ref, lam_ref, y_ref, h_ref, ubuf, hc):
        @pl.when(pl.program_id(0) == 0)
        def _():
            ubuf[0:V7X_SUBLANES, :] = jnp.zeros((V7X_SUBLANES, w), F32)
            hc[...] = jnp.zeros_like(hc)

        ubuf[V7X_SUBLANES:t + V7X_SUBLANES, :] = u_ref[...]
        xc, _ = _conv(ubuf, cw_ref, cb_ref, t)
        _, ig, _, a, s2 = _lru_gates(xc, wa_ref, ba_ref, wx_ref, bx_ref, lam_ref)
        b = jnp.sqrt(s2) * (ig * xc)
        rows = lax.broadcasted_iota(jnp.int32, (t, w), 0)
        for d in steps:
            b = a * _shift_down(b, d, 0.0, rows) + b
            a = a * _shift_down(a, d, 1.0, rows)
        h = a * hc[0:1, :] + b
        h_ref[...] = h
        y_ref[...] = (h * _gelu(ug_ref[...])).astype(y_ref.dtype)
        hc[0:1, :] = h_ref[t - 1:t, :]
        ubuf[0:V7X_SUBLANES, :] = ubuf[t:t + V7X_SUBLANES, :]

    vec = _full((1, w))
    return pl.pallas_call(
        body, name=name, grid=(s // t,),
        in_specs=[_rows(t, w, 0), _rows(t, w, 1), _full((CONV_WIDTH, w)), vec, _full((w, w)), vec, _full((w, w)),
                  vec, vec],
        out_specs=[_rows(t, w), _rows(t, w)],
        out_shape=[jax.ShapeDtypeStruct((s, w), MXU_DTYPE), jax.ShapeDtypeStruct((s, w), F32)],
        scratch_shapes=[pltpu.VMEM((t + V7X_SUBLANES, w), F32), pltpu.VMEM((V7X_SUBLANES, w), F32)],
        compiler_params=_params(("arbitrary",), 40 * _nbytes((t, w), F32)),
    )(z, z, cw, cb, wa, ba, wx, bx, lam)


def _lru_bwd(z, h, dy, cw, cb, wa, ba, wx, bx, lam, *, name):
    s = z.shape[0]
    w = LRU_WIDTH
    t = _row_tile(s, 256)
    nt = s // t
    per8 = t // V7X_SUBLANES
    steps = [1 << k for k in range(int(math.log2(t)))]

    def body(u_ref, ug_ref, h_ref, dy_ref, uprev_ref, hprev_ref, cw_ref, cb_ref, wa_ref, ba_ref, wx_ref, bx_ref,
             lam_ref, dz_ref, dcw_ref, dcb_ref, dwa_ref, dba_ref, dwx_ref, dbx_ref, dlam_ref,
             ubuf, dbuf, acar, dhcar, tmp):
        i = pl.program_id(0)
        first_tile = i == nt - 1

        @pl.when(i == 0)
        def _():
            for r in (dcw_ref, dcb_ref, dwa_ref, dba_ref, dwx_ref, dbx_ref, dlam_ref, acar, dhcar):
                r[...] = jnp.zeros_like(r)
            dbuf[t:t + V7X_SUBLANES, :] = jnp.zeros((V7X_SUBLANES, w), F32)

        keep = jnp.where(first_tile, 0.0, 1.0)
        ubuf[0:V7X_SUBLANES, :] = uprev_ref[...] * keep
        ubuf[V7X_SUBLANES:t + V7X_SUBLANES, :] = u_ref[...]
        xc, shifted = _conv(ubuf, cw_ref, cb_ref, t)
        ra, ig, sp, a, s2 = _lru_gates(xc, wa_ref, ba_ref, wx_ref, bx_ref, lam_ref)
        sq = jnp.sqrt(s2)
        gx = ig * xc
        rows = lax.broadcasted_iota(jnp.int32, (t, w), 0)
        ugv = ug_ref[...]
        dyv = dy_ref[...].astype(F32)
        hv = h_ref[...]

        acc_g = dyv * _gelu(ugv)
        acc_a = _shift_up(a, 1, acar[0:1, :], rows, t)
        for d in steps:
            acc_g = acc_a * _shift_up(acc_g, d, 0.0, rows, t) + acc_g
            acc_a = acc_a * _shift_up(acc_a, d, 1.0, rows, t)
        dh = acc_a * dhcar[0:1, :] + acc_g

        hprev = _shift_down(hv, 1, hprev_ref[V7X_SUBLANES - 1:V7X_SUBLANES, :] * keep, rows)
        d_a = dh * hprev
        d_sq = dh * gx
        d_gx = dh * sq
        d_ig = d_gx * xc
        dxc = d_gx * ig
        d_log_a = d_a * a - d_sq * (1.0 - s2) / sq
        d_ra = d_log_a * (-LRU_C * sp)
        lamv = lam_ref[...]
        dlam_ref[...] += jnp.sum(d_log_a * (-LRU_C * ra), axis=0, keepdims=True) * (-_sigmoid(-lamv))
        dpa = d_ra * ra * (1.0 - ra)
        dpx = d_ig * ig * (1.0 - ig)
        dba_ref[...] += jnp.sum(dpa, axis=0, keepdims=True)
        dbx_ref[...] += jnp.sum(dpx, axis=0, keepdims=True)
        dwa_ref[...] += _dot_tn(xc, dpa)
        dwx_ref[...] += _dot_tn(xc, dpx)
        dxc = dxc + _dot_nt(dpa, wa_ref[...]) + _dot_nt(dpx, wx_ref[...])
        dcb_ref[...] += jnp.sum(dxc, axis=0, keepdims=True)
        for k in range(CONV_WIDTH):
            dcw_ref[k:k + 1, :] += jnp.sum(dxc * shifted[k], axis=0, keepdims=True)

        dbuf[0:t, :] = dxc
        bigd = dbuf[...]
        du = dxc * cw_ref[CONV_WIDTH - 1:CONV_WIDTH, :]
        for k in range(CONV_WIDTH - 1):
            e = CONV_WIDTH - 1 - k
            du = du + pltpu.roll(bigd, t + V7X_SUBLANES - e, 0)[0:t] * cw_ref[k:k + 1, :]
        dz_ref[:, 0:w] = du.astype(dz_ref.dtype)
        dz_ref[:, w:2 * w] = (dyv * hv * _gelu_grad(ugv)).astype(dz_ref.dtype)

        dbuf[t:t + V7X_SUBLANES, :] = dbuf[0:V7X_SUBLANES, :]
        tmp[...] = a
        acar[0:1, :] = tmp[0:1, :]
        tmp[...] = dh
        dhcar[0:1, :] = tmp[0:1, :]

    vec = _full((1, w))
    rev = lambda cbk: pl.BlockSpec((t, w), lambda i: (nt - 1 - i, cbk))
    prev8 = lambda cbk: pl.BlockSpec((V7X_SUBLANES, w),
                                     lambda i: (jnp.maximum((nt - 1 - i) * per8 - 1, 0), cbk))
    return pl.pallas_call(
        body, name=name, grid=(nt,),
        in_specs=[rev(0), rev(1), rev(0), rev(0), prev8(0), prev8(0), _full((CONV_WIDTH, w)), vec, _full((w, w)),
                  vec, _full((w, w)), vec, vec],
        out_specs=[pl.BlockSpec((t, 2 * w), lambda i: (nt - 1 - i, 0)), _full((CONV_WIDTH, w)), vec,
                   _full((w, w)), vec, _full((w, w)), vec, vec],
        out_shape=[jax.ShapeDtypeStruct((s, 2 * w), MXU_DTYPE), jax.ShapeDtypeStruct((CONV_WIDTH, w), F32),
                   jax.ShapeDtypeStruct((1, w), F32), jax.ShapeDtypeStruct((w, w), F32),
                   jax.ShapeDtypeStruct((1, w), F32), jax.ShapeDtypeStruct((w, w), F32),
                   jax.ShapeDtypeStruct((1, w), F32), jax.ShapeDtypeStruct((1, w), F32)],
        scratch_shapes=[pltpu.VMEM((t + V7X_SUBLANES, w), F32), pltpu.VMEM((t + V7X_SUBLANES, w), F32),
                        pltpu.VMEM((V7X_SUBLANES, w), F32), pltpu.VMEM((V7X_SUBLANES, w), F32),
                        pltpu.VMEM((t, w), F32)],
        compiler_params=_params(("arbitrary",), 80 * _nbytes((t, w), F32)),
    )(z, z, h, dy, z, h, cw, cb, wa, ba, wx, bx, lam)


def _rope_apply(v, cos, sin, width):
    half = MLA_ROPE // 2
    lanes = lax.broadcasted_iota(jnp.int32, v.shape, 1)
    first = (lanes % MLA_ROPE) < half
    partner = jnp.where(first, pltpu.roll(v, width - half, 1), pltpu.roll(v, half, 1))
    return v * cos + partner * sin


def _rope(x, cos, sin, *, width, cb, out_dtype, sum_heads=False, name):
    s = x.shape[0]
    t = _tile(s, 512)
    out_w = V7X_LANES if sum_heads else width

    def body(x_ref, c_ref, s_ref, o_ref):
        v = x_ref[...].astype(F32)
        if sum_heads:
            v = v[:, 0:V7X_LANES] + v[:, V7X_LANES:2 * V7X_LANES]
            v = v + pltpu.roll(v, 64, 1)
            v = v + pltpu.roll(v, 32, 1)
            out = _rope_apply(v, c_ref[...], s_ref[...], V7X_LANES)
            lanes = lax.broadcasted_iota(jnp.int32, out.shape, 1)
            out = jnp.where(lanes < MLA_ROPE, out, 0.0)
        else:
            out = _rope_apply(v, c_ref[...], s_ref[...], width)
        o_ref[...] = out.astype(o_ref.dtype)

    return pl.pallas_call(
        body, name=name, grid=(s // t,),
        in_specs=[_rows(t, width, cb), _rows(t, out_w), _rows(t, out_w)], out_specs=_rows(t, out_w),
        out_shape=jax.ShapeDtypeStruct((s, out_w), out_dtype),
        compiler_params=_params(("parallel",), 12 * _nbytes((t, width), F32)),
    )(x, cos, sin)


def _visible(t, unit, transposed):
    q_idx = lax.broadcasted_iota(jnp.int32, (t, t), 1 if transposed else 0)
    k_idx = lax.broadcasted_iota(jnp.int32, (t, t), 0 if transposed else 1)
    shift = int(math.log2(unit))
    return (q_idx >> shift) >= (k_idx >> shift)


def _attn_tile(s):
    return min(512, s // 4)


def _attn_fwd(q, k, v, cq, ck, *, scale, unit, name):
    hn, s, dk = q.shape
    dv = v.shape[-1]
    t = _attn_tile(s)
    decay = cq is not None

    def body(*refs):
        q_ref, k_ref, v_ref = refs[:3]
        cq_ref, ck_ref = (refs[3], refs[4]) if decay else (None, None)
        o_ref, lse_ref = refs[-2], refs[-1]
        i = pl.program_id(1)
        qt = q_ref[0]

        def tile(j, carry, masked):
            m, l, acc = carry
            off = pl.multiple_of(j * t, t)
            kt = k_ref[0, pl.ds(off, t), :]
            vt = v_ref[0, pl.ds(off, t), :]
            sc = _dot_nt(qt, kt) * scale
            if decay:
                sc = sc + cq_ref[0] - ck_ref[0, :, pl.ds(off, t)]
            if masked:
                sc = jnp.where(_visible(t, unit, False), sc, NEG_INF)
            m_new = jnp.maximum(m, jnp.max(sc, axis=-1, keepdims=True))
            alpha = jnp.exp(m - m_new)
            pr = jnp.exp(sc - m_new)
            l = alpha * l + jnp.sum(pr, axis=-1, keepdims=True)
            acc = alpha * acc + _dot(pr, vt)
            return m_new, l, acc

        init = (jnp.full((t, 1), NEG_INF, F32), jnp.zeros((t, 1), F32), jnp.zeros((t, dv), F32))
        carry = lax.fori_loop(0, i, lambda j, c: tile(j, c, False), init)
        m, l, acc = tile(i, carry, True)
        o_ref[0] = (acc / l).astype(o_ref.dtype)
        lse_ref[0] = m + jnp.log(l)

    qs = lambda d: pl.BlockSpec((1, t, d), lambda h, i: (h, i, 0))
    whole = lambda d: pl.BlockSpec((1, s, d), lambda h, i: (h, 0, 0))
    in_specs = [qs(dk), whole(dk), whole(dv)]
    ins = [q, k, v]
    if decay:
        in_specs += [qs(1), pl.BlockSpec((1, 1, s), lambda h, i: (h, 0, 0))]
        ins += [cq, ck]
    vmem = 4 * _nbytes((s, dk + dv), q.dtype) + 10 * _nbytes((t, t), F32) + 8 * _nbytes((t, V7X_LANES), F32)
    return pl.pallas_call(
        body, name=name, grid=(hn, s // t),
        in_specs=in_specs, out_specs=[qs(dv), qs(1)],
        out_shape=[jax.ShapeDtypeStruct((hn, s, dv), MXU_DTYPE), jax.ShapeDtypeStruct((hn, s, 1), F32)],
        compiler_params=_params(("parallel", "arbitrary"), vmem),
    )(*ins)


def _attn_bwd_q(q, k, v, do, lse, cq, ck, *, scale, unit, name):
    hn, s, dk = q.shape
    dv = v.shape[-1]
    t = _attn_tile(s)
    nt = s // t
    decay = cq is not None

    def body(*refs):
        q_ref, k_ref, v_ref, do_ref, lse_ref = refs[:5]
        cq_ref, ck_ref = (refs[5], refs[6]) if decay else (None, None)
        dq_ref, dl_ref, p_sc, dp_sc = refs[-4:]
        i = pl.program_id(1)
        qt = q_ref[0]
        dot = do_ref[0]
        lse_t = lse_ref[0]

        def sweep1(j, delta, masked):
            off = pl.multiple_of(j * t, t)
            kt = k_ref[0, pl.ds(off, t), :]
            vt = v_ref[0, pl.ds(off, t), :]
            sc = _dot_nt(qt, kt) * scale
            if decay:
                sc = sc + cq_ref[0] - ck_ref[0, :, pl.ds(off, t)]
            if masked:
                sc = jnp.where(_visible(t, unit, False), sc, NEG_INF)
            pr = jnp.exp(sc - lse_t)
            dp = _dot_nt(dot, vt)
            p_sc[j] = pr
            dp_sc[j] = dp
            return delta + jnp.sum(pr * dp, axis=-1, keepdims=True)

        delta = lax.fori_loop(0, i, lambda j, c: sweep1(j, c, False), jnp.zeros((t, 1), F32))
        delta = sweep1(i, delta, True)

        def sweep2(j, dq):
            off = pl.multiple_of(j * t, t)
            ds = p_sc[j] * (dp_sc[j] - delta)
            return dq + _dot(ds, k_ref[0, pl.ds(off, t), :])

        dq = lax.fori_loop(0, i + 1, sweep2, jnp.zeros((t, dk), F32))
        dq_ref[0] = dq * scale
        dl_ref[0] = delta

    qs = lambda d: pl.BlockSpec((1, t, d), lambda h, i: (h, i, 0))
    whole = lambda d: pl.BlockSpec((1, s, d), lambda h, i: (h, 0, 0))
    in_specs = [qs(dk), whole(dk), whole(dv), qs(dv), qs(1)]
    ins = [q, k, v, do, lse]
    if decay:
        in_specs += [qs(1), pl.BlockSpec((1, 1, s), lambda h, i: (h, 0, 0))]
        ins += [cq, ck]
    vmem = (4 * _nbytes((s, dk + dv), q.dtype) + 2 * _nbytes((nt, t, t), F32) + 8 * _nbytes((t, t), F32)
            + 12 * _nbytes((t, V7X_LANES), F32))
    return pl.pallas_call(
        body, name=name, grid=(hn, nt),
        in_specs=in_specs, out_specs=[qs(dk), qs(1)],
        out_shape=[jax.ShapeDtypeStruct((hn, s, dk), F32), jax.ShapeDtypeStruct((hn, s, 1), F32)],
        scratch_shapes=[pltpu.VMEM((nt, t, t), F32), pltpu.VMEM((nt, t, t), F32)],
        compiler_params=_params(("parallel", "arbitrary"), vmem),
    )(*ins)


def _attn_bwd_kv(q, k, v, do, lse_row, delta_row, cq_row, ck, *, scale, unit, name):
    hn, s, dk = q.shape
    dv = v.shape[-1]
    t = _attn_tile(s)
    nt = s // t
    decay = ck is not None

    def body(*refs):
        k_ref, v_ref, q_ref, do_ref, lse_ref, dl_ref = refs[:6]
        ck_ref, cq_ref = (refs[6], refs[7]) if decay else (None, None)
        outs = refs[8:] if decay else refs[6:]
        dk_ref, dv_ref = outs[0], outs[1]
        j = pl.program_id(1)
        kt = k_ref[0]
        vt = v_ref[0]

        def tile(i, carry, masked):
            dk_acc, dv_acc, dc_acc = carry
            off = pl.multiple_of(i * t, t)
            qt = q_ref[0, pl.ds(off, t), :]
            dot = do_ref[0, pl.ds(off, t), :]
            sc = _dot_nt(kt, qt) * scale
            if decay:
                sc = sc + cq_ref[0, :, pl.ds(off, t)] - ck_ref[0]
            if masked:
                sc = jnp.where(_visible(t, unit, True), sc, NEG_INF)
            pr = jnp.exp(sc - lse_ref[0, :, pl.ds(off, t)])
            dv_acc = dv_acc + _dot(pr, dot)
            ds = pr * (_dot_nt(vt, dot) - dl_ref[0, :, pl.ds(off, t)])
            dk_acc = dk_acc + _dot(ds, qt)
            if decay:
                dc_acc = dc_acc + jnp.sum(ds, axis=-1, keepdims=True)
            return dk_acc, dv_acc, dc_acc

        init = (jnp.zeros((t, dk), F32), jnp.zeros((t, dv), F32), jnp.zeros((t, 1), F32))
        carry = tile(j, init, True)
        dk_acc, dv_acc, dc_acc = lax.fori_loop(j + 1, nt, lambda i, c: tile(i, c, False), carry)
        dk_ref[0] = dk_acc * scale
        dv_ref[0] = dv_acc
        if decay:
            outs[2][0] = -dc_acc

    ks = lambda d: pl.BlockSpec((1, t, d), lambda h, j: (h, j, 0))
    whole = lambda d: pl.BlockSpec((1, s, d), lambda h, j: (h, 0, 0))
    row = pl.BlockSpec((1, 1, s), lambda h, j: (h, 0, 0))
    in_specs = [ks(dk), ks(dv), whole(dk), whole(dv), row, row]
    ins = [k, v, q, do, lse_row, delta_row]
    out_specs = [ks(dk), ks(dv)]
    out_shape = [jax.ShapeDtypeStruct((hn, s, dk), F32), jax.ShapeDtypeStruct((hn, s, dv), F32)]
    if decay:
        in_specs += [ks(1), row]
        ins += [ck, cq_row]
        out_specs.append(ks(1))
        out_shape.append(jax.ShapeDtypeStruct((hn, s, 1), F32))
    vmem = 4 * _nbytes((s, dk + dv), q.dtype) + 10 * _nbytes((t, t), F32) + 12 * _nbytes((t, V7X_LANES), F32)
    return pl.pallas_call(
        body, name=name, grid=(hn, nt),
        in_specs=in_specs, out_specs=out_specs, out_shape=out_shape,
        compiler_params=_params(("parallel", "arbitrary"), vmem),
    )(*ins)


STRIP = 32
HEAD_PAIRS = HEADS // 2


def _strip_rows(t):
    return min(STRIP, t)


def _pair_mask(t):
    lane = lax.broadcasted_iota(jnp.int32, (t, V7X_LANES), 1)
    return lane < (V7X_LANES // 2)


def _strip_visible(r, t, row0, unit, transposed):
    rows = lax.broadcasted_iota(jnp.int32, (r, t), 0) + row0
    cols = lax.broadcasted_iota(jnp.int32, (r, t), 1)
    shift = int(math.log2(unit))
    if transposed:
        return (cols >> shift) >= (rows >> shift)
    return (rows >> shift) >= (cols >> shift)


def _rope_lanes(x, cos, sin, *, cb, groups, out_dtype, sum_parts=0, name):
    s = cos.shape[0]
    t = _tile(s, 512)
    w = groups * V7X_LANES

    def body(x_ref, c_ref, s_ref, o_ref):
        if sum_parts:
            v = x_ref[0].astype(F32)
            for part in range(1, sum_parts):
                v = v + x_ref[part].astype(F32)
            o_ref[...] = _rope_apply(v, c_ref[...], s_ref[...], V7X_LANES).astype(o_ref.dtype)
        else:
            for g in range(groups):
                sl = slice(g * V7X_LANES, (g + 1) * V7X_LANES)
                o_ref[:, sl] = _rope_apply(x_ref[:, sl].astype(F32), c_ref[...], s_ref[...],
                                           V7X_LANES).astype(o_ref.dtype)

    x_spec = (pl.BlockSpec((sum_parts, t, V7X_LANES), lambda i: (0, i, 0)) if sum_parts else _rows(t, w, cb))
    return pl.pallas_call(
        body, name=name, grid=(s // t,),
        in_specs=[x_spec, _rows(t, V7X_LANES), _rows(t, V7X_LANES)], out_specs=_rows(t, w),
        out_shape=jax.ShapeDtypeStruct((s, w), out_dtype),
        compiler_params=_params(("parallel",), 12 * _nbytes((t, max(w, 4 * V7X_LANES)), F32)),
    )(x, cos, sin)


def _pair_fwd(q_arr, q_cb, k_arr, k_cb, v_arr, v_cb, rope, decay, *, scale, unit, name):
    s = q_arr.shape[0]
    t = _attn_tile(s)
    r = t
    has_rope, has_decay = rope is not None, decay is not None
    kw = 2 * V7X_LANES if has_rope else V7X_LANES

    def body(*refs):
        it = iter(refs)
        q_ref, k_ref, v_ref = next(it), next(it), next(it)
        qr_ref, kr_ref = (next(it), next(it)) if has_rope else (None, None)
        cq_ref, ck_ref = (next(it), next(it)) if has_decay else (None, None)
        o_ref, lse_ref = next(it), next(it)
        q_sc, s_sc, p_sc, acc_sc, m_sc, l_sc, al_sc = (next(it) for _ in range(7))
        i = pl.program_id(1)
        in_a = _pair_mask(t)
        qv = q_ref[...]
        for hd in range(2):
            q_sc[hd, :, 0:V7X_LANES] = jnp.where(in_a if hd == 0 else jnp.logical_not(in_a), qv, 0).astype(MXU_DTYPE)
            if has_rope:
                q_sc[hd, :, V7X_LANES:kw] = qr_ref[:, hd * V7X_LANES:(hd + 1) * V7X_LANES].astype(MXU_DTYPE)
        m_sc[...] = jnp.full(m_sc.shape, NEG_INF, F32)
        l_sc[...] = jnp.zeros(l_sc.shape, F32)
        acc_sc[...] = jnp.zeros(acc_sc.shape, F32)

        def tile(j, masked):
            off = pl.multiple_of(j * t, t)
            kt = k_ref[pl.ds(off, t), :]
            if has_rope:
                kt = jnp.concatenate([kt, kr_ref[pl.ds(off, t), :]], axis=-1)
            vt = v_ref[pl.ds(off, t), :]
            for hd in range(2):
                s_sc[hd] = _dot_nt(q_sc[hd], kt)
                ck_row = ck_ref[hd, :, pl.ds(off, t)] if has_decay else None
                m_all, l_all = m_sc[hd], l_sc[hd]
                cq_all = cq_ref[hd] if has_decay else None
                m_parts, l_parts, a_parts = [], [], []
                for b in range(t // r):
                    row0 = b * r
                    sc = s_sc[hd, pl.ds(row0, r), :] * scale
                    if has_decay:
                        sc = sc + (cq_all[row0:row0 + r] - ck_row)
                    if masked:
                        sc = jnp.where(_strip_visible(r, t, row0, unit, False), sc, NEG_INF)
                    m_old = m_all[row0:row0 + r]
                    m_new = jnp.maximum(m_old, jnp.max(sc, axis=-1, keepdims=True))
                    alpha = jnp.exp(m_old - m_new)
                    pr = jnp.exp(sc - m_new)
                    l_parts.append(alpha * l_all[row0:row0 + r] + jnp.sum(pr, axis=-1, keepdims=True))
                    m_parts.append(m_new)
                    a_parts.append(alpha)
                    p_sc[hd, pl.ds(row0, r), :] = pr.astype(MXU_DTYPE)
                m_sc[hd] = jnp.concatenate(m_parts, axis=0)
                l_sc[hd] = jnp.concatenate(l_parts, axis=0)
                acc_sc[hd] = jnp.concatenate(a_parts, axis=0) * acc_sc[hd] + _dot(p_sc[hd], vt)

        def unmasked(j, carry):
            tile(j, False)
            return carry

        lax.fori_loop(0, i, unmasked, 0)
        tile(i, True)
        o_ref[...] = jnp.where(in_a, acc_sc[0] / l_sc[0], acc_sc[1] / l_sc[1]).astype(o_ref.dtype)
        lse_ref[...] = m_sc[...] + jnp.log(l_sc[...])

    blk = lambda cb: pl.BlockSpec((t, V7X_LANES), lambda p, i: (i, cb + p))
    whole = lambda cb: pl.BlockSpec((s, V7X_LANES), lambda p, i: (0, cb + p))
    stat = pl.BlockSpec((2, t, 1), lambda p, i: (p, i, 0))
    in_specs = [blk(q_cb), whole(k_cb), whole(v_cb)]
    ins = [q_arr, k_arr, v_arr]
    if has_rope:
        in_specs += [pl.BlockSpec((t, 2 * V7X_LANES), lambda p, i: (i, p)),
                     pl.BlockSpec((s, V7X_LANES), lambda p, i: (0, 0))]
        ins += list(rope)
    if has_decay:
        in_specs += [stat, pl.BlockSpec((2, 1, s), lambda p, i: (p, 0, 0))]
        ins += list(decay)
    col = (2, t, 1)
    vmem = (6 * _nbytes((s, V7X_LANES), MXU_DTYPE) + 6 * _nbytes((t, t), F32) + 10 * _nbytes((t, V7X_LANES), F32)
            + 8 * _nbytes((2, t, V7X_LANES), F32))
    return pl.pallas_call(
        body, name=name, grid=(HEAD_PAIRS, s // t),
        in_specs=in_specs, out_specs=[pl.BlockSpec((t, V7X_LANES), lambda p, i: (i, p)), stat],
        out_shape=[jax.ShapeDtypeStruct((s, HEADS * 64), MXU_DTYPE), jax.ShapeDtypeStruct((HEADS, s, 1), F32)],
        scratch_shapes=[pltpu.VMEM((2, t, kw), MXU_DTYPE), pltpu.VMEM((2, t, t), F32), pltpu.VMEM((2, t, t), MXU_DTYPE),
                        pltpu.VMEM((2, t, V7X_LANES), F32), pltpu.VMEM(col, F32), pltpu.VMEM(col, F32),
                        pltpu.VMEM((t, 1), F32)],
        compiler_params=_params(("parallel", "arbitrary"), vmem),
    )(*ins)


def _pair_bwd_q(q_arr, q_cb, k_arr, k_cb, v_arr, v_cb, do, lse, rope, decay, *, scale, unit, name):
    s = q_arr.shape[0]
    t = _attn_tile(s)
    nt = s // t
    r = t
    has_rope, has_decay = rope is not None, decay is not None
    kw = 2 * V7X_LANES if has_rope else V7X_LANES

    def body(*refs):
        it = iter(refs)
        q_ref, k_ref, v_ref, do_ref, lse_ref = (next(it) for _ in range(5))
        qr_ref, kr_ref = (next(it), next(it)) if has_rope else (None, None)
        cq_ref, ck_ref = (next(it), next(it)) if has_decay else (None, None)
        dq_ref, dl_ref = next(it), next(it)
        dqr_ref = next(it) if has_rope else None
        q_sc, do_sc, p_sc, dp_sc, ds_sc, dq_sc, dl_sc, s_sc = (next(it) for _ in range(8))
        i = pl.program_id(1)
        in_a = _pair_mask(t)
        qv = q_ref[...]
        dov = do_ref[...]
        for hd in range(2):
            sel = in_a if hd == 0 else jnp.logical_not(in_a)
            q_sc[hd, :, 0:V7X_LANES] = jnp.where(sel, qv, 0).astype(MXU_DTYPE)
            if has_rope:
                q_sc[hd, :, V7X_LANES:kw] = qr_ref[:, hd * V7X_LANES:(hd + 1) * V7X_LANES].astype(MXU_DTYPE)
            do_sc[hd] = jnp.where(sel, dov, 0).astype(MXU_DTYPE)
        dl_sc[...] = jnp.zeros(dl_sc.shape, F32)
        dq_sc[...] = jnp.zeros(dq_sc.shape, F32)

        def keys(j):
            off = pl.multiple_of(j * t, t)
            kt = k_ref[pl.ds(off, t), :]
            if has_rope:
                kt = jnp.concatenate([kt, kr_ref[pl.ds(off, t), :]], axis=-1)
            return off, kt

        for hd in range(2):
            lse_all = lse_ref[hd]
            cq_all = cq_ref[hd] if has_decay else None

            def sweep1(j, masked, hd=hd, lse_all=lse_all, cq_all=cq_all):
                off, kt = keys(j)
                s_sc[...] = _dot_nt(q_sc[hd], kt)
                dp_sc[j] = _dot_nt(do_sc[hd], v_ref[pl.ds(off, t), :])
                ck_row = ck_ref[hd, :, pl.ds(off, t)] if has_decay else None
                parts = []
                for b in range(t // r):
                    row0 = b * r
                    rows = pl.ds(row0, r)
                    sc = s_sc[rows, :] * scale
                    if has_decay:
                        sc = sc + (cq_all[row0:row0 + r] - ck_row)
                    if masked:
                        sc = jnp.where(_strip_visible(r, t, row0, unit, False), sc, NEG_INF)
                    pr = jnp.exp(sc - lse_all[row0:row0 + r])
                    p_sc[j, rows, :] = pr
                    parts.append(jnp.sum(pr * dp_sc[j, rows, :], axis=-1, keepdims=True))
                dl_sc[hd] += jnp.concatenate(parts, axis=0)

            def sweep1_unmasked(j, carry, sweep1=sweep1):
                sweep1(j, False)
                return carry

            lax.fori_loop(0, i, sweep1_unmasked, 0)
            sweep1(i, True)
            dl_all = dl_sc[hd]

            def sweep2(j, carry, hd=hd, dl_all=dl_all):
                _, kt = keys(j)
                for b in range(t // r):
                    row0 = b * r
                    rows = pl.ds(row0, r)
                    ds = p_sc[j, rows, :] * (dp_sc[j, rows, :] - dl_all[row0:row0 + r])
                    ds_sc[rows, :] = ds.astype(MXU_DTYPE)
                dq_sc[hd] += _dot(ds_sc[...], kt)
                return carry

            lax.fori_loop(0, i + 1, sweep2, 0)

        dq_ref[...] = (jnp.where(in_a, dq_sc[0, :, 0:V7X_LANES], dq_sc[1, :, 0:V7X_LANES]) * scale).astype(dq_ref.dtype)
        dl_ref[...] = dl_sc[...]
        if has_rope:
            dqr_ref[:, 0:V7X_LANES] = dq_sc[0, :, V7X_LANES:kw] * scale
            dqr_ref[:, V7X_LANES:kw] = dq_sc[1, :, V7X_LANES:kw] * scale

    blk = lambda cb: pl.BlockSpec((t, V7X_LANES), lambda p, i: (i, cb + p))
    whole = lambda cb: pl.BlockSpec((s, V7X_LANES), lambda p, i: (0, cb + p))
    stat = pl.BlockSpec((2, t, 1), lambda p, i: (p, i, 0))
    in_specs = [blk(q_cb), whole(k_cb), whole(v_cb), blk(0), stat]
    ins = [q_arr, k_arr, v_arr, do, lse]
    out_specs = [blk(0), stat]
    out_shape = [jax.ShapeDtypeStruct((s, HEADS * 64), MXU_DTYPE), jax.ShapeDtypeStruct((HEADS, s, 1), F32)]
    if has_rope:
        pair_rot = pl.BlockSpec((t, 2 * V7X_LANES), lambda p, i: (i, p))
        in_specs += [pair_rot, pl.BlockSpec((s, V7X_LANES), lambda p, i: (0, 0))]
        ins += list(rope)
        out_specs.append(pair_rot)
        out_shape.append(jax.ShapeDtypeStruct((s, HEADS * V7X_LANES), F32))
    if has_decay:
        in_specs += [stat, pl.BlockSpec((2, 1, s), lambda p, i: (p, 0, 0))]
        ins += list(decay)
    vmem = (6 * _nbytes((s, V7X_LANES), MXU_DTYPE) + 2 * _nbytes((nt, t, t), F32) + 6 * _nbytes((t, t), F32)
            + 16 * _nbytes((t, kw), F32))
    return pl.pallas_call(
        body, name=name, grid=(HEAD_PAIRS, nt),
        in_specs=in_specs, out_specs=out_specs, out_shape=out_shape,
        scratch_shapes=[pltpu.VMEM((2, t, kw), MXU_DTYPE), pltpu.VMEM((2, t, V7X_LANES), MXU_DTYPE),
                        pltpu.VMEM((nt, t, t), F32), pltpu.VMEM((nt, t, t), F32), pltpu.VMEM((t, t), MXU_DTYPE),
                        pltpu.VMEM((2, t, kw), F32), pltpu.VMEM((2, t, 1), F32), pltpu.VMEM((t, t), F32)],
        compiler_params=_params(("parallel", "arbitrary"), vmem),
    )(*ins)


def _pair_bwd_kv(q_arr, q_cb, k_arr, k_cb, v_arr, v_cb, do, lse_row, delta_row, rope, decay, *, scale, unit, name):
    s = q_arr.shape[0]
    t = _attn_tile(s)
    nt = s // t
    r = _strip_rows(t)
    has_rope, has_decay = rope is not None, decay is not None
    kw = 2 * V7X_LANES if has_rope else V7X_LANES

    def body(*refs):
        it = iter(refs)
        k_ref, v_ref, q_ref, do_ref, lse_ref, dl_ref = (next(it) for _ in range(6))
        qr_ref, kr_ref = (next(it), next(it)) if has_rope else (None, None)
        ck_ref, cq_ref = (next(it), next(it)) if has_decay else (None, None)
        dk_ref, dv_ref = next(it), next(it)
        dkr_ref = next(it) if has_rope else None
        dc_ref = next(it) if has_decay else None
        k_sc, v_sc, st_sc, dpt_sc, pt_sc, dst_sc, dk_sc, dv_sc, dc_sc = (next(it) for _ in range(9))
        j = pl.program_id(1)
        in_a = _pair_mask(t)
        kv_, vv_ = k_ref[...], v_ref[...]
        for hd in range(2):
            sel = in_a if hd == 0 else jnp.logical_not(in_a)
            k_sc[hd, :, 0:V7X_LANES] = jnp.where(sel, kv_, 0).astype(MXU_DTYPE)
            if has_rope:
                k_sc[hd, :, V7X_LANES:kw] = kr_ref[...].astype(MXU_DTYPE)
            v_sc[hd] = jnp.where(sel, vv_, 0).astype(MXU_DTYPE)
        dk_sc[...] = jnp.zeros(dk_sc.shape, F32)
        dv_sc[...] = jnp.zeros(dv_sc.shape, F32)
        dc_sc[...] = jnp.zeros(dc_sc.shape, F32)

        def tile(i, masked):
            off = pl.multiple_of(i * t, t)
            qt = q_ref[pl.ds(off, t), :]
            dot = do_ref[pl.ds(off, t), :]
            for hd in range(2):
                qcat = qt
                if has_rope:
                    qcat = jnp.concatenate([qt, qr_ref[pl.ds(off, t), hd * V7X_LANES:(hd + 1) * V7X_LANES]], axis=-1)
                st_sc[hd] = _dot_nt(k_sc[hd], qcat)
                dpt_sc[hd] = _dot_nt(v_sc[hd], dot)
                lse_r = lse_ref[hd, :, pl.ds(off, t)]
                dl_r = dl_ref[hd, :, pl.ds(off, t)]
                cq_r = cq_ref[hd, :, pl.ds(off, t)] if has_decay else None
                ck_all = ck_ref[hd] if has_decay else None
                parts = []
                for b in range(t // r):
                    row0 = b * r
                    rows = pl.ds(row0, r)
                    sc = st_sc[hd, rows, :] * scale
                    if has_decay:
                        sc = sc + (cq_r - ck_all[row0:row0 + r])
                    if masked:
                        sc = jnp.where(_strip_visible(r, t, row0, unit, True), sc, NEG_INF)
                    pr = jnp.exp(sc - lse_r)
                    ds = pr * (dpt_sc[hd, rows, :] - dl_r)
                    pt_sc[hd, rows, :] = pr.astype(MXU_DTYPE)
                    dst_sc[hd, rows, :] = ds.astype(MXU_DTYPE)
                    if has_decay:
                        parts.append(jnp.sum(ds, axis=-1, keepdims=True))
                if has_decay:
                    dc_sc[hd] += jnp.concatenate(parts, axis=0)
                dv_sc[hd] += _dot(pt_sc[hd], dot)
                dk_sc[hd] += _dot(dst_sc[hd], qcat)

        tile(j, True)

        def unmasked(i, carry):
            tile(i, False)
            return carry

        lax.fori_loop(j + 1, nt, unmasked, 0)
        dk_ref[...] = (jnp.where(in_a, dk_sc[0, :, 0:V7X_LANES], dk_sc[1, :, 0:V7X_LANES]) * scale).astype(dk_ref.dtype)
        dv_ref[...] = jnp.where(in_a, dv_sc[0], dv_sc[1]).astype(dv_ref.dtype)
        if has_rope:
            dkr_ref[0] = (dk_sc[0, :, V7X_LANES:kw] + dk_sc[1, :, V7X_LANES:kw]) * scale
        if has_decay:
            dc_ref[...] = -dc_sc[...]

    blk = lambda cb: pl.BlockSpec((t, V7X_LANES), lambda p, j: (j, cb + p))
    whole = lambda cb: pl.BlockSpec((s, V7X_LANES), lambda p, j: (0, cb + p))
    stat = pl.BlockSpec((2, t, 1), lambda p, j: (p, j, 0))
    row = pl.BlockSpec((2, 1, s), lambda p, j: (p, 0, 0))
    in_specs = [blk(k_cb), blk(v_cb), whole(q_cb), whole(0), row, row]
    ins = [k_arr, v_arr, q_arr, do, lse_row, delta_row]
    out_specs = [blk(0), blk(0)]
    out_shape = [jax.ShapeDtypeStruct((s, HEADS * 64), MXU_DTYPE)] * 2
    if has_rope:
        in_specs += [pl.BlockSpec((s, 2 * V7X_LANES), lambda p, j: (0, p)),
                     pl.BlockSpec((t, V7X_LANES), lambda p, j: (j, 0))]
        ins += list(rope)
        out_specs.append(pl.BlockSpec((1, t, V7X_LANES), lambda p, j: (p, j, 0)))
        out_shape.append(jax.ShapeDtypeStruct((HEAD_PAIRS, s, V7X_LANES), F32))
    if has_decay:
        in_specs += [stat, row]
        ins += list(decay)
        out_specs.append(stat)
        out_shape.append(jax.ShapeDtypeStruct((HEADS, s, 1), F32))
    vmem = (12 * _nbytes((s, V7X_LANES), MXU_DTYPE) + 8 * _nbytes((t, t), F32) + 16 * _nbytes((t, kw), F32))
    return pl.pallas_call(
        body, name=name, grid=(HEAD_PAIRS, nt),
        in_specs=in_specs, out_specs=out_specs, out_shape=out_shape,
        scratch_shapes=[pltpu.VMEM((2, t, kw), MXU_DTYPE), pltpu.VMEM((2, t, V7X_LANES), MXU_DTYPE),
                        pltpu.VMEM((2, t, t), F32), pltpu.VMEM((2, t, t), F32), pltpu.VMEM((2, t, t), MXU_DTYPE),
                        pltpu.VMEM((2, t, t), MXU_DTYPE), pltpu.VMEM((2, t, kw), F32),
                        pltpu.VMEM((2, t, V7X_LANES), F32), pltpu.VMEM((2, t, 1), F32)],
        compiler_params=_params(("parallel", "arbitrary"), vmem),
    )(*ins)


def _fox_cum(z, bf, *, name):
    s = z.shape[0]
    w = V7X_LANES
    t = _row_tile(s, 512)
    steps = [1 << k for k in range(int(math.log2(t)))]
    cb = SEG["fl"][3] // w

    def body(f_ref, bf_ref, c_ref, car):
        @pl.when(pl.program_id(0) == 0)
        def _():
            car[...] = jnp.zeros_like(car)

        acc = -_softplus(-(f_ref[...] + bf_ref[...]))
        rows = lax.broadcasted_iota(jnp.int32, (t, w), 0)
        for d in steps:
            acc = acc + _shift_down(acc, d, 0.0, rows)
        c_ref[...] = acc + car[0:1, :]
        car[0:1, :] = c_ref[t - 1:t, :]

    return pl.pallas_call(
        body, name=name, grid=(s // t,),
        in_specs=[_rows(t, w, cb), _full((1, w))], out_specs=_rows(t, w),
        out_shape=jax.ShapeDtypeStruct((s, w), F32),
        scratch_shapes=[pltpu.VMEM((V7X_SUBLANES, w), F32)],
        compiler_params=_params(("arbitrary",), 16 * _nbytes((t, w), F32)),
    )(z, bf)


def _fox_cum_bwd(z, bf, dcum, *, name):
    s = z.shape[0]
    w = V7X_LANES
    t = _row_tile(s, 512)
    nt = s // t
    steps = [1 << k for k in range(int(math.log2(t)))]
    cb = SEG["fl"][3] // w

    def body(f_ref, bf_ref, dc_ref, df_ref, dbf_ref, car, tmp):
        @pl.when(pl.program_id(0) == 0)
        def _():
            car[...] = jnp.zeros_like(car)
            dbf_ref[...] = jnp.zeros_like(dbf_ref)

        acc = dc_ref[...]
        rows = lax.broadcasted_iota(jnp.int32, (t, w), 0)
        for d in steps:
            acc = acc + _shift_up(acc, d, 0.0, rows, t)
        dlf = acc + car[0:1, :]
        tmp[...] = dlf
        car[0:1, :] = tmp[0:1, :]
        df = dlf * _sigmoid(-(f_ref[...] + bf_ref[...]))
        df_ref[...] = df.astype(df_ref.dtype)
        dbf_ref[...] += jnp.sum(df, axis=0, keepdims=True)

    rev = lambda cbk: pl.BlockSpec((t, w), lambda i: (nt - 1 - i, cbk))
    return pl.pallas_call(
        body, name=name, grid=(nt,),
        in_specs=[rev(cb), _full((1, w)), rev(0)], out_specs=[rev(0), _full((1, w))],
        out_shape=[jax.ShapeDtypeStruct((s, w), MXU_DTYPE), jax.ShapeDtypeStruct((1, w), F32)],
        scratch_shapes=[pltpu.VMEM((V7X_SUBLANES, w), F32), pltpu.VMEM((t, w), F32)],
        compiler_params=_params(("arbitrary",), 16 * _nbytes((t, w), F32)),
    )(z, bf, dcum)


_GATE_CB = SEG["gate"][3] // D_MODEL


def _merge_fwd(ya, yb, yc, z, gate_b, *, name):
    s = ya.shape[0]
    d = D_MODEL
    t = _tile(s, 256)

    def body(ya_ref, yb_ref, yc_ref, g0_ref, g1_ref, g2_ref, gb_ref, o_ref):
        out = _sigmoid(g0_ref[...] + gb_ref[:, 0:d]) * ya_ref[...]
        out = out + _sigmoid(g1_ref[...] + gb_ref[:, d:2 * d]) * yb_ref[...]
        out = out + _sigmoid(g2_ref[...] + gb_ref[:, 2 * d:3 * d]) * yc_ref[...]
        o_ref[...] = out.astype(o_ref.dtype)

    return pl.pallas_call(
        body, name=name, grid=(s // t,),
        in_specs=[_rows(t, d)] * 3 + [_rows(t, d, _GATE_CB + b) for b in range(3)] + [_full((1, 3 * d))],
        out_specs=_rows(t, d), out_shape=jax.ShapeDtypeStruct((s, d), MXU_DTYPE),
        compiler_params=_params(("parallel",), 20 * _nbytes((t, d), F32)),
    )(ya, yb, yc, z, z, z, gate_b)


def _merge_bwd(dm, ya, yb, yc, z, gate_b, *, name):
    s = ya.shape[0]
    d = D_MODEL
    t = _tile(s, 256)

    def body(dm_ref, ya_ref, yb_ref, yc_ref, g0_ref, g1_ref, g2_ref, gb_ref, da_ref, db_ref, dc_ref, dgl_ref,
             dgb_ref):
        dmv = dm_ref[...]
        parts = []
        for b, (y_ref, g_ref, dy_ref) in enumerate(((ya_ref, g0_ref, da_ref), (yb_ref, g1_ref, db_ref),
                                                    (yc_ref, g2_ref, dc_ref))):
            gate = _sigmoid(g_ref[...] + gb_ref[:, b * d:(b + 1) * d])
            dy_ref[...] = (dmv * gate).astype(dy_ref.dtype)
            dgl = dmv * y_ref[...] * gate * (1.0 - gate)
            dgl_ref[:, b * d:(b + 1) * d] = dgl.astype(dgl_ref.dtype)
            parts.append(jnp.sum(dgl, axis=0, keepdims=True))

        @pl.when(pl.program_id(0) == 0)
        def _():
            for b, part in enumerate(parts):
                dgb_ref[:, b * d:(b + 1) * d] = part

        @pl.when(pl.program_id(0) > 0)
        def _():
            for b, part in enumerate(parts):
                dgb_ref[:, b * d:(b + 1) * d] += part

    return pl.pallas_call(
        body, name=name, grid=(s // t,),
        in_specs=[_rows(t, d)] * 4 + [_rows(t, d, _GATE_CB + b) for b in range(3)] + [_full((1, 3 * d))],
        out_specs=[_rows(t, d)] * 3 + [_rows(t, 3 * d), _full((1, 3 * d))],
        out_shape=[jax.ShapeDtypeStruct((s, d), MXU_DTYPE)] * 3
        + [jax.ShapeDtypeStruct((s, 3 * d), MXU_DTYPE), jax.ShapeDtypeStruct((1, 3 * d), F32)],
        compiler_params=_params(("arbitrary",), 36 * _nbytes((t, d), F32)),
    )(dm, ya, yb, yc, z, z, z, gate_b)


def _swiglu_fwd(hf, *, name):
    s = hf.shape[0]
    t = _tile(s, 256)

    def body(g_ref, u_ref, o_ref):
        gv = g_ref[...]
        o_ref[...] = (gv * _sigmoid(gv) * u_ref[...]).astype(o_ref.dtype)

    return pl.pallas_call(
        body, name=name, grid=(s // t,),
        in_specs=[_rows(t, D_FF, 0), _rows(t, D_FF, 1)], out_specs=_rows(t, D_FF),
        out_shape=jax.ShapeDtypeStruct((s, D_FF), MXU_DTYPE),
        compiler_params=_params(("parallel",), 10 * _nbytes((t, D_FF), F32)),
    )(hf, hf)


def _swiglu_bwd(hf, dact, *, name):
    s = hf.shape[0]
    t = _tile(s, 256)

    def body(g_ref, u_ref, da_ref, o_ref):
        gv = g_ref[...]
        dav = da_ref[...]
        sg = _sigmoid(gv)
        o_ref[:, 0:D_FF] = (dav * u_ref[...] * sg * (1.0 + gv * (1.0 - sg))).astype(o_ref.dtype)
        o_ref[:, D_FF:2 * D_FF] = (dav * gv * sg).astype(o_ref.dtype)

    return pl.pallas_call(
        body, name=name, grid=(s // t,),
        in_specs=[_rows(t, D_FF, 0), _rows(t, D_FF, 1), _rows(t, D_FF)], out_specs=_rows(t, 2 * D_FF),
        out_shape=jax.ShapeDtypeStruct((s, 2 * D_FF), MXU_DTYPE),
        compiler_params=_params(("parallel",), 14 * _nbytes((t, D_FF), F32)),
    )(hf, hf, dact)


def _ple_fwd(x, lg, pe, *, name):
    s, d = x.shape
    t = _tile(s, 512)

    def body(x_ref, lg_ref, pe_ref, o_ref):
        o_ref[...] = x_ref[...] + _sigmoid(lg_ref[...]) * pe_ref[...]

    return pl.pallas_call(
        body, name=name, grid=(s // t,),
        in_specs=[_rows(t, d)] * 3, out_specs=_rows(t, d), out_shape=jax.ShapeDtypeStruct((s, d), F32),
        compiler_params=_params(("parallel",), 12 * _nbytes((t, d), F32)),
    )(x, lg, pe)


def _ple_bwd(dx, lg, pe, *, name):
    s, d = dx.shape
    t = _tile(s, 512)

    def body(dx_ref, lg_ref, pe_ref, dpe_ref, dlg_ref):
        dxv = dx_ref[...]
        sg = _sigmoid(lg_ref[...])
        dpe_ref[...] = (dxv * sg).astype(dpe_ref.dtype)
        dlg_ref[...] = (dxv * pe_ref[...] * sg * (1.0 - sg)).astype(dlg_ref.dtype)

    return pl.pallas_call(
        body, name=name, grid=(s // t,),
        in_specs=[_rows(t, d)] * 3, out_specs=[_rows(t, d)] * 2,
        out_shape=[jax.ShapeDtypeStruct((s, d), MXU_DTYPE)] * 2,
        compiler_params=_params(("parallel",), 14 * _nbytes((t, d), F32)),
    )(dx, lg, pe)


def _adamw(parts, w, m, v, *, name):
    rows, lanes = w.shape
    t = math.gcd(rows, 160)
    assert rows % t == 0 and t % V7X_SUBLANES == 0
    c1 = 1.0 / (1.0 - ADAM_B1 ** ADAM_STEP)
    c2 = 1.0 / (1.0 - ADAM_B2 ** ADAM_STEP)

    def body(p_ref, w_ref, m_ref, v_ref, g_ref, d_ref, nm_ref, nv_ref):
        g = p_ref[0].astype(F32)
        for j in range(1, N_DEV):
            g = g + p_ref[j].astype(F32)
        m2 = ADAM_B1 * m_ref[...] + (1.0 - ADAM_B1) * g
        v2 = ADAM_B2 * v_ref[...] + (1.0 - ADAM_B2) * (g * g)
        g_ref[...] = g
        nm_ref[...] = m2
        nv_ref[...] = v2
        d_ref[...] = -ADAM_LR * ((m2 * c1) / (jnp.sqrt(v2 * c2) + ADAM_EPS) + ADAM_WD * w_ref[...])

    blk = _rows(t, lanes)
    return pl.pallas_call(
        body, name=name, grid=(rows // t,),
        in_specs=[pl.BlockSpec((N_DEV, t, lanes), lambda i: (0, i, 0)), blk, blk, blk], out_specs=[blk] * 4,
        out_shape=[jax.ShapeDtypeStruct((rows, lanes), F32)] * 4,
        compiler_params=_params(("parallel",), 40 * _nbytes((t, lanes), F32)),
    )(parts, w, m, v)


def _mesh_pos():
    return lax.axis_index("x"), lax.axis_index("y"), lax.axis_index("c")


def _all_gather(blk, *, name):
    r, c_dim = blk.shape

    def body(x_ref, out_ref, send_sems, recv_sems, local_sem):
        x, y, c = _mesh_pos()
        me, sibling = (x, y, c), (x, y, 1 - c)
        chips = [(1 - x, y), (x, 1 - y), (1 - x, 1 - y)]

        def slot(px, py, pc):
            return out_ref.at[4 * px + 2 * py + pc]

        def copy(k, block, to, src=None):
            return pltpu.make_async_remote_copy(
                src_ref=slot(*block) if src is None else src, dst_ref=slot(*block),
                send_sem=send_sems.at[k], recv_sem=recv_sems.at[k],
                device_id=to, device_id_type=pl.DeviceIdType.MESH)

        mine = pltpu.make_async_copy(x_ref, slot(*me), local_sem)
        mine.start()
        first = [copy(0, me, sibling, src=x_ref)]
        first += [copy(1 + j, me, (*chip, c), src=x_ref) for j, chip in enumerate(chips)]
        for cp in first:
            cp.start()
        passed = [copy(4 + j, (*chip, c), sibling) for j, chip in enumerate(chips)]
        for j, chip in enumerate(chips):
            copy(1 + j, (*chip, c), me).wait_recv()
            passed[j].start()
        copy(0, sibling, me).wait_recv()
        for j, chip in enumerate(chips):
            copy(4 + j, (*chip, 1 - c), me).wait_recv()
        for cp in first + passed:
            cp.wait_send()
        mine.wait()

    return pl.pallas_call(
        body, name=name,
        out_shape=jax.ShapeDtypeStruct((N_DEV, r, c_dim), blk.dtype),
        in_specs=[pl.BlockSpec(memory_space=pl.ANY)], out_specs=pl.BlockSpec(memory_space=pl.ANY),
        scratch_shapes=[pltpu.SemaphoreType.DMA((7,)), pltpu.SemaphoreType.DMA((7,)), pltpu.SemaphoreType.DMA],
    )(blk)


def _all_to_all(pay, *, name):
    _, r, c_dim = pay.shape

    def body(in_ref, out_ref, send_sems, recv_sems, local_sem):
        x, y, c = _mesh_pos()
        me = 4 * x + 2 * y + c
        local = pltpu.make_async_copy(in_ref.at[me], out_ref.at[me], local_sem)
        local.start()
        copies = []
        for k in range(1, N_DEV):
            px = 1 - x if k & 4 else x
            py = 1 - y if k & 2 else y
            pc = 1 - c if k & 1 else c
            copies.append(pltpu.make_async_remote_copy(
                src_ref=in_ref.at[4 * px + 2 * py + pc], dst_ref=out_ref.at[me],
                send_sem=send_sems.at[k - 1], recv_sem=recv_sems.at[k - 1],
                device_id=(px, py, pc), device_id_type=pl.DeviceIdType.MESH))
        for cp in copies:
            cp.start()
        for cp in copies:
            cp.wait()
        local.wait()

    return pl.pallas_call(
        body, name=name,
        out_shape=jax.ShapeDtypeStruct((N_DEV, r, c_dim), pay.dtype),
        in_specs=[pl.BlockSpec(memory_space=pl.ANY)], out_specs=pl.BlockSpec(memory_space=pl.ANY),
        scratch_shapes=[pltpu.SemaphoreType.DMA((7,)), pltpu.SemaphoreType.DMA((7,)), pltpu.SemaphoreType.DMA],
    )(pay)


def _all_gather_many(blocks, *, name):
    n = len(blocks)

    def body(*refs):
        x_refs, out_refs = refs[:n], refs[n:2 * n]
        send_sems, recv_sems, local_sems = refs[2 * n:]
        x, y, c = _mesh_pos()
        me, sibling = (x, y, c), (x, y, 1 - c)
        chips = [(1 - x, y), (x, 1 - y), (1 - x, 1 - y)]

        def slot(a, px, py, pc):
            return out_refs[a].at[4 * px + 2 * py + pc]

        def copy(k, a, block, to, src=None):
            return pltpu.make_async_remote_copy(
                src_ref=slot(a, *block) if src is None else src, dst_ref=slot(a, *block),
                send_sem=send_sems.at[k, a], recv_sem=recv_sems.at[k, a],
                device_id=to, device_id_type=pl.DeviceIdType.MESH)

        mine = [pltpu.make_async_copy(x_refs[a], slot(a, *me), local_sems.at[a]) for a in range(n)]
        for cp in mine:
            cp.start()
        first = [copy(0, a, me, sibling, src=x_refs[a]) for a in range(n)]
        first += [copy(1 + j, a, me, (*chip, c), src=x_refs[a]) for j, chip in enumerate(chips) for a in range(n)]
        for cp in first:
            cp.start()
        passed = []
        for j, chip in enumerate(chips):
            for a in range(n):
                copy(1 + j, a, (*chip, c), me).wait_recv()
                fwd = copy(4 + j, a, (*chip, c), sibling)
                fwd.start()
                passed.append(fwd)
        for a in range(n):
            copy(0, a, sibling, me).wait_recv()
        for j, chip in enumerate(chips):
            for a in range(n):
                copy(4 + j, a, (*chip, 1 - c), me).wait_recv()
        for cp in first + passed:
            cp.wait_send()
        for cp in mine:
            cp.wait()

    any_spec = pl.BlockSpec(memory_space=pl.ANY)
    return pl.pallas_call(
        body, name=name,
        out_shape=[jax.ShapeDtypeStruct((N_DEV,) + b.shape, b.dtype) for b in blocks],
        in_specs=[any_spec] * n, out_specs=[any_spec] * n,
        scratch_shapes=[pltpu.SemaphoreType.DMA((7, n)), pltpu.SemaphoreType.DMA((7, n)),
                        pltpu.SemaphoreType.DMA((n,))],
    )(*blocks)


def _all_to_all_many(pays, *, name):
    n = len(pays)

    def body(*refs):
        in_refs, out_refs = refs[:n], refs[n:2 * n]
        send_sems, recv_sems, local_sems = refs[2 * n:]
        x, y, c = _mesh_pos()
        me = 4 * x + 2 * y + c
        local = [pltpu.make_async_copy(in_refs[a].at[me], out_refs[a].at[me], local_sems.at[a]) for a in range(n)]
        for cp in local:
            cp.start()
        copies = []
        for k in range(1, N_DEV):
            px = 1 - x if k & 4 else x
            py = 1 - y if k & 2 else y
            pc = 1 - c if k & 1 else c
            for a in range(n):
                copies.append(pltpu.make_async_remote_copy(
                    src_ref=in_refs[a].at[4 * px + 2 * py + pc], dst_ref=out_refs[a].at[me],
                    send_sem=send_sems.at[k - 1, a], recv_sem=recv_sems.at[k - 1, a],
                    device_id=(px, py, pc), device_id_type=pl.DeviceIdType.MESH))
        for cp in copies:
            cp.start()
        for cp in copies:
            cp.wait()
        for cp in local:
            cp.wait()

    any_spec = pl.BlockSpec(memory_space=pl.ANY)
    return pl.pallas_call(
        body, name=name,
        out_shape=[jax.ShapeDtypeStruct(p.shape, p.dtype) for p in pays],
        in_specs=[any_spec] * n, out_specs=[any_spec] * n,
        scratch_shapes=[pltpu.SemaphoreType.DMA((7, n)), pltpu.SemaphoreType.DMA((7, n)),
                        pltpu.SemaphoreType.DMA((n,))],
    )(*pays)


def _adamw_nd(parts, w, m, v, *, name):
    d0, rows, cols = w.shape
    t = rows
    for cand in range(V7X_SUBLANES, min(rows, 256) + 1, V7X_SUBLANES):
        if rows % cand == 0:
            t = cand
    c1 = 1.0 / (1.0 - ADAM_B1 ** ADAM_STEP)
    c2 = 1.0 / (1.0 - ADAM_B2 ** ADAM_STEP)

    def body(p_ref, w_ref, m_ref, v_ref, g_ref, d_ref, nm_ref, nv_ref):
        g = p_ref[0, 0].astype(F32)
        for j in range(1, N_DEV):
            g = g + p_ref[j, 0].astype(F32)
        m2 = ADAM_B1 * m_ref[0] + (1.0 - ADAM_B1) * g
        v2 = ADAM_B2 * v_ref[0] + (1.0 - ADAM_B2) * (g * g)
        g_ref[0] = g
        nm_ref[0] = m2
        nv_ref[0] = v2
        d_ref[0] = -ADAM_LR * ((m2 * c1) / (jnp.sqrt(v2 * c2) + ADAM_EPS) + ADAM_WD * w_ref[0])

    blk = pl.BlockSpec((1, t, cols), lambda l, i: (l, i, 0))
    lanes = -(-cols // V7X_LANES) * V7X_LANES
    return pl.pallas_call(
        body, name=name, grid=(d0, rows // t),
        in_specs=[pl.BlockSpec((N_DEV, 1, t, cols), lambda l, i: (0, l, i, 0)), blk, blk, blk], out_specs=[blk] * 4,
        out_shape=[jax.ShapeDtypeStruct(w.shape, F32)] * 4,
        compiler_params=_params(("parallel", "parallel"), 40 * _nbytes((max(t, 16), lanes), F32)),
    )(parts, w, m, v)


def _flat_rows(parts, row_multiple):
    flat = jnp.concatenate([p.reshape(-1) for p in parts])
    chunk = PAYLOAD_LANES * row_multiple
    total = -(-flat.shape[0] // chunk) * chunk
    return jnp.pad(flat, (0, total - flat.shape[0])).reshape(total // PAYLOAD_LANES, PAYLOAD_LANES)


def _split_flat(flat, shapes):
    out, off = [], 0
    flat = flat.reshape(-1)
    for shp in shapes:
        n = math.prod(shp)
        out.append(flat[off:off + n].reshape(shp))
        off += n
    return out


def _pad_w_in(w):
    pieces, cursor = [], 0
    for _, off, width, pad_off, _ in SEGS:
        if pad_off > cursor:
            pieces.append(jnp.zeros(w.shape[:-1] + (pad_off - cursor,), w.dtype))
        pieces.append(w[..., off:off + width])
        cursor = pad_off + width
    pieces.append(jnp.zeros(w.shape[:-1] + (D_IN_PAD - cursor,), w.dtype))
    return jnp.concatenate(pieces, axis=-1)


def _unpad_w_in(w):
    return jnp.concatenate([w[..., pad_off:pad_off + width] for _, _, width, pad_off, _ in SEGS], axis=-1)


def _heads(a, hd):
    return a.reshape(a.shape[0], HEADS, hd).transpose(1, 0, 2)


def _unheads(a):
    return a.transpose(1, 0, 2).reshape(a.shape[1], -1)


def _block_diag(w):
    eye = jnp.eye(LRU_HEADS, dtype=w.dtype)
    return (eye[:, None, :, None] * w[:, :, None, :]).reshape(LRU_WIDTH, LRU_WIDTH)


def _diag_blocks(w):
    w4 = w.reshape(LRU_HEADS, LRU_HEAD_DIM, LRU_HEADS, LRU_HEAD_DIM)
    return jnp.stack([w4[h, :, h, :] for h in range(LRU_HEADS)])


def _lane_pad(a, width):
    return jnp.pad(a, ((0, 0), (0, width - a.shape[-1])))


def _layer_fwd(x, p_i, wts, tabs, tag):
    n = functools.partial(lambda base, t=tag: f"{base}_{t}")
    sv = {"x": x}
    n1 = _rms_fwd(x, wts["mix_norm"], width=D_MODEL, name=n("mix_norm_fwd"))
    z, z16 = _mm(n1, wts["w_in"], also_mxu=True, name=n("w_in_fwd"))
    sv.update(n1=n1, z=z, z16=z16)
    lanes = V7X_LANES

    ya_pre, hseq = _lru_fwd(z, wts["conv_w"], wts["conv_b"], wts["lru_wa"], wts["lru_ba"], wts["lru_wx"],
                            wts["lru_bx"], wts["lru_lambda"], name=n("lru_fwd"))
    ya = _mm(ya_pre, wts["w_br_a"], name=n("br_a_fwd"))
    sv.update(ya_pre=ya_pre, hseq=hseq, ya=ya)

    cqn = _rms_fwd(z, wts["mla_q_norm"], width=MLA_Q_LORA, cb=SEG["cq"][3] // MLA_Q_LORA, name=n("q_norm_fwd"))
    ckvn = _rms_fwd(z, wts["mla_kv_norm"], width=MLA_KV_LORA, cb=SEG["ckv"][3] // MLA_KV_LORA,
                    name=n("kv_norm_fwd"))
    qp, qp16 = _mm(cqn, wts["mla_wuq"], also_mxu=True, name=n("wuq_fwd"))
    kv = _mm(ckvn, wts["mla_wukv"], out_dtype=MXU_DTYPE, name=n("wukv_fwd"))
    q_rot = _rope_lanes(qp, tabs["cos128"], tabs["sin128"], cb=0, groups=HEADS, out_dtype=MXU_DTYPE,
                        name=n("q_rope_fwd"))
    k_rot = _rope_lanes(z, tabs["cos128"], tabs["sin128"], cb=SEG["kr"][3] // lanes, groups=1, out_dtype=MXU_DTYPE,
                        name=n("k_rope_fwd"))
    mla_ops = (qp16, HEADS, kv, 0, kv, HEAD_PAIRS)
    ob_flat, lse_b = _pair_fwd(*mla_ops, (q_rot, k_rot), None, scale=(MLA_NOPE + MLA_ROPE) ** -0.5, unit=CHUNK,
                               name=n("mla_attn_fwd"))
    yb = _mm(ob_flat, wts["w_br_b"], name=n("br_b_fwd"))
    sv.update(cqn=cqn, ckvn=ckvn, mla_ops=mla_ops, mla_rot=(q_rot, k_rot), lse_b=lse_b, ob_flat=ob_flat, yb=yb)

    cum = _fox_cum(z, wts["fox_bf"], name=n("fox_cum_fwd"))
    cum_h = cum[:, :HEADS].T
    fox_decay = (cum_h[:, :, None], cum_h[:, None, :])
    fox_ops = (z16, SEG["fq"][3] // lanes, z16, SEG["fk"][3] // lanes, z16, SEG["fv"][3] // lanes)
    oc_flat, lse_c = _pair_fwd(*fox_ops, None, fox_decay, scale=FOX_HEAD_DIM ** -0.5, unit=1, name=n("fox_attn_fwd"))
    yc = _mm(oc_flat, wts["w_br_c"], name=n("br_c_fwd"))
    sv.update(fox_ops=fox_ops, fox_decay=fox_decay, lse_c=lse_c, oc_flat=oc_flat, yc=yc)

    merged = _merge_fwd(ya, yb, yc, z, wts["gate_b"], name=n("merge_fwd"))
    x1 = _mm(merged, wts["w_o"], res=x, name=n("w_o_fwd"))
    n2 = _rms_fwd(x1, wts["ffn_norm"], width=D_MODEL, name=n("ffn_norm_fwd"))
    hf = _mm(n2, wts["w_gate_up"], name=n("gate_up_fwd"))
    act = _swiglu_fwd(hf, name=n("swiglu_fwd"))
    x2 = _mm(act, wts["w_down"], res=x1, name=n("down_fwd"))
    n3 = _rms_fwd(x2, wts["ple_norm"], width=D_MODEL, name=n("ple_norm_fwd"))
    lg = _mm(n3, wts["w_ple_gate"], name=n("ple_gate_fwd"))
    pe = _mm(p_i, wts["w_ple"], name=n("ple_fwd_mm"))
    x3 = _ple_fwd(x2, lg, pe, name=n("ple_fwd"))
    sv.update(merged=merged, x1=x1, n2=n2, hf=hf, act=act, x2=x2, n3=n3, lg=lg, pe=pe, p_i=p_i)
    return x3, sv


def _layer_bwd(dx3, sv, wts, tabs, tag):
    n = functools.partial(lambda base, t=tag: f"{base}_{t}")
    gr = {}
    z = sv["z"]
    s = z.shape[0]

    dpe, dlg = _ple_bwd(dx3, sv["lg"], sv["pe"], name=n("ple_bwd"))
    gr["w_ple"] = _mm(sv["p_i"], dpe, ta=True, name=n("ple_dw"))
    gr["w_ple_gate"] = _mm(sv["n3"], dlg, ta=True, name=n("ple_gate_dw"))
    dn3 = _mm(dlg, wts["w_ple_gate"], tb=True, name=n("ple_gate_dx"))
    dx2, gr["ple_norm"] = _rms_bwd(sv["x2"], wts["ple_norm"], dn3, width=D_MODEL, res=dx3, name=n("ple_norm_bwd"))

    dact = _mm(dx2, wts["w_down"], tb=True, name=n("down_dx"))
    gr["w_down"] = _mm(sv["act"], dx2, ta=True, name=n("down_dw"))
    dhf = _swiglu_bwd(sv["hf"], dact, name=n("swiglu_bwd"))
    gr["w_gate_up"] = _mm(sv["n2"], dhf, ta=True, name=n("gate_up_dw"))
    dn2 = _mm(dhf, wts["w_gate_up"], tb=True, name=n("gate_up_dx"))
    dx1, gr["ffn_norm"] = _rms_bwd(sv["x1"], wts["ffn_norm"], dn2, width=D_MODEL, res=dx2, name=n("ffn_norm_bwd"))

    dmerged = _mm(dx1, wts["w_o"], tb=True, name=n("w_o_dx"))
    gr["w_o"] = _mm(sv["merged"], dx1, ta=True, name=n("w_o_dw"))
    dya, dyb, dyc, dgl, gr["gate_b"] = _merge_bwd(dmerged, sv["ya"], sv["yb"], sv["yc"], z, wts["gate_b"],
                                                  name=n("merge_bwd"))
    gr["w_br_a"] = _mm(sv["ya_pre"], dya, ta=True, name=n("br_a_dw"))
    gr["w_br_b"] = _mm(sv["ob_flat"], dyb, ta=True, name=n("br_b_dw"))
    gr["w_br_c"] = _mm(sv["oc_flat"], dyc, ta=True, name=n("br_c_dw"))
    dya_pre = _mm(dya, wts["w_br_a"], tb=True, name=n("br_a_dx"))
    dob = _mm(dyb, wts["w_br_b"], tb=True, out_dtype=MXU_DTYPE, name=n("br_b_dx"))
    doc = _mm(dyc, wts["w_br_c"], tb=True, out_dtype=MXU_DTYPE, name=n("br_c_dx"))

    (dz_a, gr["conv_w"], gr["conv_b"], dwa, gr["lru_ba"], dwx, gr["lru_bx"], gr["lru_lambda"]) = _lru_bwd(
        z, sv["hseq"], dya_pre, wts["conv_w"], wts["conv_b"], wts["lru_wa"], wts["lru_ba"], wts["lru_wx"],
        wts["lru_bx"], wts["lru_lambda"], name=n("lru_bwd"))
    gr["lru_wa"], gr["lru_wx"] = _diag_blocks(dwa), _diag_blocks(dwx)

    scale_b = (MLA_NOPE + MLA_ROPE) ** -0.5
    dq_nope, delta_b, dq_rot = _pair_bwd_q(*sv["mla_ops"], dob, sv["lse_b"], sv["mla_rot"], None,
                                           scale=scale_b, unit=CHUNK, name=n("mla_attn_dq"))
    dk_nope, dv_mla, dk_rot = _pair_bwd_kv(*sv["mla_ops"], dob, sv["lse_b"].reshape(HEADS, 1, s),
                                           delta_b.reshape(HEADS, 1, s), sv["mla_rot"], None,
                                           scale=scale_b, unit=CHUNK, name=n("mla_attn_dkv"))
    dq_rope = _rope_lanes(dq_rot, tabs["cos128"], -tabs["sin128"], cb=0, groups=HEADS, out_dtype=MXU_DTYPE,
                          name=n("q_rope_bwd"))
    dk_rope = _rope_lanes(dk_rot, tabs["cos128"], -tabs["sin128"], cb=0, groups=1, out_dtype=MXU_DTYPE,
                          sum_parts=HEAD_PAIRS, name=n("k_rope_bwd"))
    dqp = jnp.concatenate([dq_rope, dq_nope], axis=-1)
    dkv = jnp.concatenate([dk_nope, dv_mla], axis=-1)
    gr["mla_wuq"] = _mm(sv["cqn"], dqp, ta=True, name=n("wuq_dw"))
    gr["mla_wukv"] = _mm(sv["ckvn"], dkv, ta=True, name=n("wukv_dw"))
    dcqn = _mm(dqp, wts["mla_wuq"], tb=True, name=n("wuq_dx"))
    dckvn = _mm(dkv, wts["mla_wukv"], tb=True, name=n("wukv_dx"))
    dcq, gr["mla_q_norm"] = _rms_bwd(z, wts["mla_q_norm"], dcqn, width=MLA_Q_LORA, cb=SEG["cq"][3] // MLA_Q_LORA,
                                     out_dtype=MXU_DTYPE, name=n("q_norm_bwd"))
    dckv, gr["mla_kv_norm"] = _rms_bwd(z, wts["mla_kv_norm"], dckvn, width=MLA_KV_LORA,
                                       cb=SEG["ckv"][3] // MLA_KV_LORA, out_dtype=MXU_DTYPE, name=n("kv_norm_bwd"))

    scale_c = FOX_HEAD_DIM ** -0.5
    dfq, delta_c = _pair_bwd_q(*sv["fox_ops"], doc, sv["lse_c"], None, sv["fox_decay"],
                               scale=scale_c, unit=1, name=n("fox_attn_dq"))
    dfk, dfv, dcum = _pair_bwd_kv(*sv["fox_ops"], doc, sv["lse_c"].reshape(HEADS, 1, s),
                                  delta_c.reshape(HEADS, 1, s), None, sv["fox_decay"],
                                  scale=scale_c, unit=1, name=n("fox_attn_dkv"))
    dcum_rows = _lane_pad(dcum.reshape(HEADS, s).T, V7X_LANES)
    dfl, dbf = _fox_cum_bwd(z, wts["fox_bf"], dcum_rows, name=n("fox_cum_bwd"))
    gr["fox_bf"] = dbf[:, :HEADS]

    zero = lambda width: jnp.zeros((s, width), MXU_DTYPE)
    dz = jnp.concatenate([dz_a, zero(128), dcq, dckv, dk_rope, zero(128), dfq, dfk, dfv, dfl, zero(384), dgl],
                         axis=-1)
    gr["w_in"] = _mm(sv["n1"], dz, ta=True, name=n("w_in_dw"))
    dn1 = _mm(dz, wts["w_in"], tb=True, name=n("w_in_dx"))
    dx, gr["mix_norm"] = _rms_bwd(sv["x"], wts["mix_norm"], dn1, width=D_MODEL, res=dx1, name=n("mix_norm_bwd"))
    return dx, gr


def _rope_tables(s):
    pos = jnp.arange(s, dtype=F32)
    inv_freq = ROPE_BASE ** (-jnp.arange(0, MLA_ROPE, 2, dtype=F32) / MLA_ROPE)
    ang = pos[:, None] * inv_freq[None, :]
    cos, sin = jnp.cos(ang), jnp.sin(ang)
    cos32 = jnp.concatenate([cos, cos], axis=-1)
    sin32 = jnp.concatenate([-sin, sin], axis=-1)
    return {"cos256": jnp.tile(cos32, (1, 8)), "sin256": jnp.tile(sin32, (1, 8)),
            "cos128": jnp.tile(cos32, (1, 4)), "sin128": jnp.tile(sin32, (1, 4))}


def _gather_weights(shards):
    names = [nm for nm, _ in SHARDED]
    got = _all_gather_many([shards[nm] if nm == "conv_w" else shards[nm].astype(MXU_DTYPE) for nm in names],
                           name="weights_all_gather")
    full = {}
    for (nm, axis), blk in zip(SHARDED, got):
        shp = shards[nm].shape
        if axis == 2:
            full[nm] = blk.transpose(1, 2, 0, 3).reshape(shp[0], shp[1], N_DEV * shp[2])
        else:
            full[nm] = blk.transpose(1, 0, 2, 3).reshape(shp[0], N_DEV * shp[1], shp[2])
    return full


def _to_dest_major(g, axis):
    d0, r, c = g.shape
    if axis == 2:
        return g.reshape(d0, r, N_DEV, c // N_DEV).transpose(2, 0, 1, 3)
    return g.reshape(d0, N_DEV, r // N_DEV, c).transpose(1, 0, 2, 3)


def kernel(x, p, mix_norm, w_in, gate_b, conv_w, conv_b, lru_wa, lru_ba, lru_wx, lru_bx, lru_lambda, mla_q_norm, mla_wuq, mla_kv_norm, mla_wukv, fox_bf, w_br_a, w_br_b, w_br_c, w_o, ffn_norm, w_gate_up, w_down, ple_norm, w_ple_gate, w_ple, final_norm, loss_target, m_mix_norm, m_w_in, m_gate_b, m_conv_w, m_conv_b, m_lru_wa, m_lru_ba, m_lru_wx, m_lru_bx, m_lru_lambda, m_mla_q_norm, m_mla_wuq, m_mla_kv_norm, m_mla_wukv, m_fox_bf, m_w_br_a, m_w_br_b, m_w_br_c, m_w_o, m_ffn_norm, m_w_gate_up, m_w_down, m_ple_norm, m_w_ple_gate, m_w_ple, m_final_norm, v_mix_norm, v_w_in, v_gate_b, v_conv_w, v_conv_b, v_lru_wa, v_lru_ba, v_lru_wx, v_lru_bx, v_lru_lambda, v_mla_q_norm, v_mla_wuq, v_mla_kv_norm, v_mla_wukv, v_fox_bf, v_w_br_a, v_w_br_b, v_w_br_c, v_w_o, v_ffn_norm, v_w_gate_up, v_w_down, v_ple_norm, v_w_ple_gate, v_w_ple, v_final_norm):
    given = dict(locals())
    w_loc = {nm: given[nm] for nm in WEIGHTS}
    m_loc = {nm: given["m_" + nm] for nm in WEIGHTS}
    v_loc = {nm: given["v_" + nm] for nm in WEIGHTS}
    xs = x[0]
    s = xs.shape[0]
    tabs = _rope_tables(s)

    full = _gather_weights({nm: w_loc[nm] for nm, _ in SHARDED})
    full["w_in"] = _pad_w_in(full["w_in"])
    wq = full["mla_wuq"].reshape(DEPTH, MLA_Q_LORA, HEADS, MLA_NOPE + MLA_ROPE)
    wq_rot = jnp.pad(wq[..., MLA_NOPE:], ((0, 0), (0, 0), (0, 0), (0, V7X_LANES - MLA_ROPE)))
    full["mla_wuq"] = jnp.concatenate([wq_rot.reshape(DEPTH, MLA_Q_LORA, -1),
                                       wq[..., :MLA_NOPE].reshape(DEPTH, MLA_Q_LORA, -1)], axis=-1)
    wkv = full["mla_wukv"].reshape(DEPTH, MLA_KV_LORA, HEADS, MLA_NOPE + MLA_V)
    full["mla_wukv"] = jnp.concatenate([wkv[..., :MLA_NOPE].reshape(DEPTH, MLA_KV_LORA, -1),
                                        wkv[..., MLA_NOPE:].reshape(DEPTH, MLA_KV_LORA, -1)], axis=-1)

    def layer_weights(i):
        wts = {nm: full[nm][i] for nm, _ in SHARDED}
        for nm in ("mix_norm", "gate_b", "conv_b", "lru_ba", "lru_bx", "lru_lambda", "mla_q_norm", "mla_kv_norm",
                   "ffn_norm", "ple_norm"):
            wts[nm] = w_loc[nm][i][None, :]
        wts["fox_bf"] = _lane_pad(w_loc["fox_bf"][i][None, :], V7X_LANES)
        wts["lru_wa"] = _block_diag(w_loc["lru_wa"][i]).astype(MXU_DTYPE)
        wts["lru_wx"] = _block_diag(w_loc["lru_wx"][i]).astype(MXU_DTYPE)
        return wts

    layers = [layer_weights(i) for i in range(DEPTH)]

    h = xs
    saved = []
    for i in range(DEPTH):
        h, sv = _layer_fwd(h, p[i, 0].astype(MXU_DTYPE), layers[i], tabs, f"l{i}")
        saved.append(sv)
    loss_blk, dh, dg_final = _final_loss(h, w_loc["final_norm"][None, :], loss_target[0], name="final_loss")
    loss = lax.psum(loss_blk[0, 0], ("x", "y", "c"))

    grads = [None] * DEPTH
    for i in reversed(range(DEPTH)):
        dh, grads[i] = _layer_bwd(dh, saved[i], layers[i], tabs, f"l{i}")
    grad_x = dh[None]

    def stacked(nm):
        return jnp.stack([grads[i][nm] for i in range(DEPTH)])

    gfull = {}
    for nm, _ in SHARDED:
        gfull[nm] = stacked(nm)
    gfull["w_in"] = _unpad_w_in(gfull["w_in"])
    gq = gfull["mla_wuq"]
    rot_w = HEADS * V7X_LANES
    gfull["mla_wuq"] = jnp.concatenate(
        [gq[..., rot_w:].reshape(DEPTH, MLA_Q_LORA, HEADS, MLA_NOPE),
         gq[..., :rot_w].reshape(DEPTH, MLA_Q_LORA, HEADS, V7X_LANES)[..., :MLA_ROPE]],
        axis=-1).reshape(DEPTH, MLA_Q_LORA, -1)
    gkv = gfull["mla_wukv"]
    gfull["mla_wukv"] = jnp.concatenate(
        [gkv[..., :512].reshape(DEPTH, MLA_KV_LORA, HEADS, MLA_NOPE),
         gkv[..., 512:].reshape(DEPTH, MLA_KV_LORA, HEADS, MLA_V)], axis=-1).reshape(DEPTH, MLA_KV_LORA, -1)

    parts = _all_to_all_many([_to_dest_major(gfull[nm], ax).astype(MXU_DTYPE) for nm, ax in SHARDED],
                             name="grads_all_to_all")
    res_s = [{}, {}, {}, {}]
    for (nm, _), part in zip(SHARDED, parts):
        outs = _adamw_nd(part, w_loc[nm], m_loc[nm], v_loc[nm], name=f"adamw_{nm}")
        for kind in range(4):
            res_s[kind][nm] = outs[kind]

    small = {nm: stacked(nm) for nm in REPLICATED if nm != "final_norm"}
    small["final_norm"] = dg_final
    names_r = list(REPLICATED)
    shapes_r = [w_loc[nm].shape for nm in names_r]
    parts_r = _all_gather(_flat_rows([small[nm] for nm in names_r], 8), name="small_grads_all_gather")
    outs_r = _adamw(parts_r, _flat_rows([w_loc[nm] for nm in names_r], 8),
                    _flat_rows([m_loc[nm] for nm in names_r], 8),
                    _flat_rows([v_loc[nm] for nm in names_r], 8), name="adamw_replicated")
    res_r = [dict(zip(names_r, _split_flat(o, shapes_r))) for o in outs_r]

    out = [loss, grad_x]
    for kind in range(4):
        for nm in WEIGHTS:
            out.append(res_s[kind][nm] if nm in res_s[kind] else res_r[kind][nm])
    return tuple(out)
```

```python
import functools
import math

import jax
import jax.numpy as jnp
from jax import lax
from jax.experimental import pallas as pl
from jax.experimental.pallas import tpu as pltpu

F32 = jnp.float32
BF16 = jnp.bfloat16
MXU_DTYPE = jnp.bfloat16

D_MODEL = 1024
DEPTH = 2
CHUNK = 64
EPS = 1e-6
NEG_INF = -1e30
LRU_WIDTH = 512
LRU_HEADS = 8
LRU_HEAD_DIM = 64
CONV_WIDTH = 4
LRU_C = 8.0
HEADS = 8
MLA_Q_LORA = 384
MLA_KV_LORA = 256
MLA_NOPE = 64
MLA_ROPE = 32
MLA_V = 64
ROPE_BASE = 10000.0
FOX_HEAD_DIM = 64
FOX_WIDTH = 512
D_FF = 2816
PLE_DIM = 256
D_IN = 6312
ADAM_LR = 0.001
ADAM_B1 = 0.9
ADAM_B2 = 0.999
ADAM_EPS = 1e-08
ADAM_WD = 0.01
ADAM_STEP = 10

V7X_VMEM_BYTES = 64 * 1024 * 1024
V7X_LANES = 128
V7X_SUBLANES = 8
VMEM_LIMIT_CAP = 56 * 1024 * 1024
N_DEV = 8

SEGS = (
    ("u", 0, 512, 0, 512),
    ("ug", 512, 512, 512, 512),
    ("cq", 1024, 384, 1152, 384),
    ("ckv", 1408, 256, 1536, 256),
    ("kr", 1664, 32, 1792, 128),
    ("fq", 1696, 512, 2048, 512),
    ("fk", 2208, 512, 2560, 512),
    ("fv", 2720, 512, 3072, 512),
    ("fl", 3232, 8, 3584, 128),
    ("gate", 3240, 3072, 4096, 1024),
)
D_IN_PAD = 7168
SEG = {s[0]: s for s in SEGS}

SHARDED = (("w_in", 2), ("mla_wuq", 2), ("mla_wukv", 2), ("w_br_a", 2), ("w_br_b", 2), ("w_br_c", 2),
           ("w_o", 1), ("w_gate_up", 2), ("w_down", 1), ("w_ple_gate", 1), ("w_ple", 2), ("conv_w", 2))
REPLICATED = ("mix_norm", "gate_b", "conv_b", "lru_wa", "lru_ba", "lru_wx", "lru_bx", "lru_lambda",
              "mla_q_norm", "mla_kv_norm", "fox_bf", "ffn_norm", "ple_norm", "final_norm")
WEIGHTS = ("mix_norm", "w_in", "gate_b", "conv_w", "conv_b", "lru_wa", "lru_ba", "lru_wx", "lru_bx",
           "lru_lambda", "mla_q_norm", "mla_wuq", "mla_kv_norm", "mla_wukv", "fox_bf", "w_br_a", "w_br_b",
           "w_br_c", "w_o", "ffn_norm", "w_gate_up", "w_down", "ple_norm", "w_ple_gate", "w_ple", "final_norm")
PAYLOAD_LANES = 1024


def _tile(n, cap=1024):
    best = None
    for t in range(V7X_LANES, min(n, cap) + 1, V7X_LANES):
        if n % t == 0:
            best = t
    return best if best is not None else n


def _row_tile(s, pref):
    t = min(pref, s // 2)
    assert s % t == 0 and t % V7X_SUBLANES == 0
    return t


def _nbytes(shape, dtype):
    return math.prod(shape) * jnp.dtype(dtype).itemsize


def _params(sem, vmem_bytes):
    limit = int(min(VMEM_LIMIT_CAP, max(16 * 1024 * 1024, vmem_bytes)))
    return pltpu.CompilerParams(dimension_semantics=sem, vmem_limit_bytes=limit)


def _full(shape):
    return pl.BlockSpec(shape, lambda *_: (0,) * len(shape))


def _rows(t, w, cb=0):
    return pl.BlockSpec((t, w), lambda i: (i, cb))


def _mxu(v):
    return v.astype(MXU_DTYPE)


def _dot(a, b):
    return lax.dot_general(_mxu(a), _mxu(b), (((1,), (0,)), ((), ())), preferred_element_type=F32)


def _dot_nt(a, b):
    return lax.dot_general(_mxu(a), _mxu(b), (((1,), (1,)), ((), ())), preferred_element_type=F32)


def _dot_tn(a, b):
    return lax.dot_general(_mxu(a), _mxu(b), (((0,), (0,)), ((), ())), preferred_element_type=F32)


def _sigmoid(v):
    return 1.0 / (1.0 + jnp.exp(-v))


def _softplus(v):
    return jnp.maximum(v, 0.0) + jnp.log(1.0 + jnp.exp(-jnp.abs(v)))


def _neg_expm1(v):
    series = -v * (1.0 + v * (0.5 + v * (1.0 / 6.0 + v * (1.0 / 24.0))))
    return jnp.where(v > -0.03, series, 1.0 - jnp.exp(v))


_GELU_C = math.sqrt(2.0 / math.pi)
_GELU_A = 0.044715


def _gelu(v):
    t = jnp.tanh(_GELU_C * (v + _GELU_A * v * v * v))
    return 0.5 * v * (1.0 + t)


def _gelu_grad(v):
    t = jnp.tanh(_GELU_C * (v + _GELU_A * v * v * v))
    return 0.5 * (1.0 + t) + 0.5 * v * (1.0 - t * t) * _GELU_C * (1.0 + 3.0 * _GELU_A * v * v)


def _mm(a, b, *, ta=False, tb=False, out_dtype=F32, res=None, also_mxu=False, name):
    k_dim, m_dim = (a.shape[0], a.shape[1]) if ta else (a.shape[1], a.shape[0])
    n_dim = b.shape[0] if tb else b.shape[1]
    assert (b.shape[1] if tb else b.shape[0]) == k_dim
    tm, tn, tk = _tile(m_dim), _tile(n_dim, 1408), _tile(k_dim, 1408)
    nk = k_dim // tk
    a_spec = pl.BlockSpec((tk, tm), lambda i, j, k: (k, i)) if ta else pl.BlockSpec((tm, tk), lambda i, j, k: (i, k))
    b_spec = pl.BlockSpec((tn, tk), lambda i, j, k: (j, k)) if tb else pl.BlockSpec((tk, tn), lambda i, j, k: (k, j))
    o_spec = pl.BlockSpec((tm, tn), lambda i, j, k: (i, j))
    has_res = res is not None

    def body(*refs):
        a_ref, b_ref = refs[0], refs[1]
        res_ref = refs[2] if has_res else None
        o_ref = refs[3] if has_res else refs[2]
        o2_ref = refs[-2] if also_mxu else None
        acc_ref = refs[-1]
        k = pl.program_id(2)
        if ta:
            part = _dot_tn(a_ref[...], b_ref[...])
        elif tb:
            part = _dot_nt(a_ref[...], b_ref[...])
        else:
            part = _dot(a_ref[...], b_ref[...])

        def finish(total):
            if has_res:
                total = total + res_ref[...].astype(F32)
            o_ref[...] = total.astype(o_ref.dtype)
            if also_mxu:
                o2_ref[...] = total.astype(o2_ref.dtype)

        if nk == 1:
            finish(part)
        else:
            @pl.when(k == 0)
            def _():
                acc_ref[...] = part

            @pl.when(jnp.logical_and(k > 0, k < nk - 1))
            def _():
                acc_ref[...] += part

            @pl.when(k == nk - 1)
            def _():
                finish(acc_ref[...] + part)

    ins = [a, b] + ([res] if has_res else [])
    in_specs = [a_spec, b_spec] + ([o_spec] if has_res else [])
    acc_shape = (tm, tn) if nk > 1 else (V7X_SUBLANES, V7X_LANES)
    vmem = (2 * (_nbytes((tm, tk), a.dtype) + _nbytes((tk, tn), b.dtype) + _nbytes((tm, tn), out_dtype)
                 + (_nbytes((tm, tn), res.dtype) if has_res else 0))
            + _nbytes((tm, tk), MXU_DTYPE) + _nbytes((tk, tn), MXU_DTYPE) + 3 * _nbytes((tm, tn), F32))
    return pl.pallas_call(
        body, name=name, grid=(m_dim // tm, n_dim // tn, nk),
        in_specs=in_specs, out_specs=[o_spec, o_spec] if also_mxu else o_spec,
        out_shape=([jax.ShapeDtypeStruct((m_dim, n_dim), out_dtype), jax.ShapeDtypeStruct((m_dim, n_dim), MXU_DTYPE)]
                   if also_mxu else jax.ShapeDtypeStruct((m_dim, n_dim), out_dtype)),
        scratch_shapes=[pltpu.VMEM(acc_shape, F32)],
        compiler_params=_params(("parallel", "parallel", "arbitrary"), vmem),
    )(*ins)


def _rms_fwd(x, g, *, width, cb=0, name):
    s = x.shape[0]
    t = _tile(s, 512)

    def body(x_ref, g_ref, o_ref):
        xv = x_ref[...].astype(F32)
        r = lax.rsqrt(jnp.mean(xv * xv, axis=-1, keepdims=True) + EPS)
        o_ref[...] = (xv * r * g_ref[...]).astype(o_ref.dtype)

    return pl.pallas_call(
        body, name=name, grid=(s // t,),
        in_specs=[_rows(t, width, cb), _full((1, width))], out_specs=_rows(t, width),
        out_shape=jax.ShapeDtypeStruct((s, width), MXU_DTYPE),
        compiler_params=_params(("parallel",), 8 * _nbytes((t, width), F32)),
    )(x, g)


def _rms_bwd(x, g, dn, *, width, cb=0, res=None, out_dtype=F32, name):
    s = x.shape[0]
    t = _tile(s, 256)
    has_res = res is not None

    def body(*refs):
        x_ref, g_ref, dn_ref = refs[:3]
        res_ref = refs[3] if has_res else None
        dx_ref, dg_ref = refs[-2], refs[-1]
        xv = x_ref[...].astype(F32)
        dnv = dn_ref[...].astype(F32)
        r = lax.rsqrt(jnp.mean(xv * xv, axis=-1, keepdims=True) + EPS)
        xr = xv * r
        dng = dnv * g_ref[...]
        dx = r * dng - xr * (r * r) * jnp.mean(dng * xv, axis=-1, keepdims=True)
        if has_res:
            dx = dx + res_ref[...].astype(F32)
        dx_ref[...] = dx.astype(dx_ref.dtype)
        part = jnp.sum(dnv * xr, axis=0, keepdims=True)

        @pl.when(pl.program_id(0) == 0)
        def _():
            dg_ref[...] = part

        @pl.when(pl.program_id(0) > 0)
        def _():
            dg_ref[...] += part

    ins = [x, g, dn] + ([res] if has_res else [])
    in_specs = [_rows(t, width, cb), _full((1, width)), _rows(t, width)] + ([_rows(t, width)] if has_res else [])
    return pl.pallas_call(
        body, name=name, grid=(s // t,),
        in_specs=in_specs, out_specs=[_rows(t, width), _full((1, width))],
        out_shape=[jax.ShapeDtypeStruct((s, width), out_dtype), jax.ShapeDtypeStruct((1, width), F32)],
        compiler_params=_params(("arbitrary",), 16 * _nbytes((t, width), F32)),
    )(*ins)


def _final_loss(x, g, target, *, name):
    s, d = x.shape
    t = _tile(s, 256)

    def body(x_ref, g_ref, t_ref, loss_ref, dx_ref, dg_ref):
        xv = x_ref[...]
        r = lax.rsqrt(jnp.mean(xv * xv, axis=-1, keepdims=True) + EPS)
        xr = xv * r
        err = xr * g_ref[...] - t_ref[...]
        part_loss = 0.5 * jnp.sum(jnp.mean(err * err, axis=-1, keepdims=True), axis=0, keepdims=True)
        dnv = err * (1.0 / d)
        dng = dnv * g_ref[...]
        dx_ref[...] = r * dng - xr * (r * r) * jnp.mean(dng * xv, axis=-1, keepdims=True)
        part_dg = jnp.sum(dnv * xr, axis=0, keepdims=True)

        @pl.when(pl.program_id(0) == 0)
        def _():
            dg_ref[...] = part_dg
            loss_ref[...] = jnp.zeros(loss_ref.shape, F32) + part_loss

        @pl.when(pl.program_id(0) > 0)
        def _():
            dg_ref[...] += part_dg
            loss_ref[...] += part_loss

    return pl.pallas_call(
        body, name=name, grid=(s // t,),
        in_specs=[_rows(t, d), _full((1, d)), _rows(t, d)],
        out_specs=[_full((V7X_SUBLANES, V7X_LANES)), _rows(t, d), _full((1, d))],
        out_shape=[jax.ShapeDtypeStruct((V7X_SUBLANES, V7X_LANES), F32), jax.ShapeDtypeStruct((s, d), F32),
                   jax.ShapeDtypeStruct((1, d), F32)],
        compiler_params=_params(("arbitrary",), 16 * _nbytes((t, d), F32)),
    )(x, g, target)


def _shift_down(v, d, fill, rows):
    return jnp.where(rows >= d, pltpu.roll(v, d, 0), fill)


def _shift_up(v, d, fill, rows, t):
    return jnp.where(rows < t - d, pltpu.roll(v, t - d, 0), fill)


def _lru_gates(xc, wa_ref, ba_ref, wx_ref, bx_ref, lam_ref):
    ra = _sigmoid(_dot(xc, wa_ref[...]) + ba_ref[...])
    ig = _sigmoid(_dot(xc, wx_ref[...]) + bx_ref[...])
    sp = _softplus(-lam_ref[...])
    log_a = -LRU_C * ra * sp
    a = jnp.exp(log_a)
    s2 = _neg_expm1(2.0 * log_a)
    return ra, ig, sp, a, s2


def _conv(ubuf, cw_ref, cb_ref, t):
    big = ubuf[...]
    shifted = [pltpu.roll(big, CONV_WIDTH - 1 - k, 0)[V7X_SUBLANES:t + V7X_SUBLANES] if k < CONV_WIDTH - 1
               else big[V7X_SUBLANES:t + V7X_SUBLANES] for k in range(CONV_WIDTH)]
    xc = cb_ref[...] + shifted[0] * cw_ref[0:1, :]
    for k in range(1, CONV_WIDTH):
        xc = xc + shifted[k] * cw_ref[k:k + 1, :]
    return xc, shifted


def _lru_fwd(z, cw, cb, wa, ba, wx, bx, lam, *, name):
    s = z.shape[0]
    w = LRU_WIDTH
    t = _row_tile(s, 256)
    steps = [1 << k for k in range(int(math.log2(t)))]

    def body(u_ref, ug_ref, cw_ref, cb_ref, wa_ref, ba_ref, wx_ref, bx_ref, lam_ref, y_ref, h_ref, ubuf, hc):
        @pl.when(pl.program_id(0) == 0)
        def _():
            ubuf[0:V7X_SUBLANES, :] = jnp.zeros((V7X_SUBLANES, w), F32)
            hc[...] = jnp.zeros_like(hc)

        ubuf[V7X_SUBLANES:t + V7X_SUBLANES, :] = u_ref[...]
        xc, _ = _conv(ubuf, cw_ref, cb_ref, t)
        _, ig, _, a, s2 = _lru_gates(xc, wa_ref, ba_ref, wx_ref, bx_ref, lam_ref)
        b = jnp.sqrt(s2) * (ig * xc)
        rows = lax.broadcasted_iota(jnp.int32, (t, w), 0)
        for d in steps:
            b = a * _shift_down(b, d, 0.0, rows) + b
            a = a * _shift_down(a, d, 1.0, rows)
        h = a * hc[0:1, :] + b
        h_ref[...] = h
        y_ref[...] = (h * _gelu(ug_ref[...])).astype(y_ref.dtype)
        hc[0:1, :] = h_ref[t - 1:t, :]
        ubuf[0:V7X_SUBLANES, :] = ubuf[t:t + V7X_SUBLANES, :]

    vec = _full((1, w))
    return pl.pallas_call(
        body, name=name, grid=(s // t,),
        in_specs=[_rows(t, w, 0), _rows(t, w, 1), _full((CONV_WIDTH, w)), vec, _full((w, w)), vec, _full((w, w)),
                  vec, vec],
        out_specs=[_rows(t, w), _rows(t, w)],
        out_shape=[jax.ShapeDtypeStruct((s, w), MXU_DTYPE), jax.ShapeDtypeStruct((s, w), F32)],
        scratch_shapes=[pltpu.VMEM((t + V7X_SUBLANES, w), F32), pltpu.VMEM((V7X_SUBLANES, w), F32)],
        compiler_params=_params(("arbitrary",), 40 * _nbytes((t, w), F32)),
    )(z, z, cw, cb, wa, ba, wx, bx, lam)


def _lru_bwd(z, h, dy, cw, cb, wa, ba, wx, bx, lam, *, name):
    s = z.shape[0]
    w = LRU_WIDTH
    t = _row_tile(s, 256)
    nt = s // t
    per8 = t // V7X_SUBLANES
    steps = [1 << k for k in range(int(math.log2(t)))]

    def body(u_ref, ug_ref, h_ref, dy_ref, uprev_ref, hprev_ref, cw_ref, cb_ref, wa_ref, ba_ref, wx_ref, bx_ref,
             lam_ref, dz_ref, dcw_ref, dcb_ref, dwa_ref, dba_ref, dwx_ref, dbx_ref, dlam_ref,
             ubuf, dbuf, acar, dhcar, tmp):
        i = pl.program_id(0)
        first_tile = i == nt - 1

        @pl.when(i == 0)
        def _():
            for r in (dcw_ref, dcb_ref, dwa_ref, dba_ref, dwx_ref, dbx_ref, dlam_ref, acar, dhcar):
                r[...] = jnp.zeros_like(r)
            dbuf[t:t + V7X_SUBLANES, :] = jnp.zeros((V7X_SUBLANES, w), F32)

        keep = jnp.where(first_tile, 0.0, 1.0)
        ubuf[0:V7X_SUBLANES, :] = uprev_ref[...] * keep
        ubuf[V7X_SUBLANES:t + V7X_SUBLANES, :] = u_ref[...]
        xc, shifted = _conv(ubuf, cw_ref, cb_ref, t)
        ra, ig, sp, a, s2 = _lru_gates(xc, wa_ref, ba_ref, wx_ref, bx_ref, lam_ref)
        sq = jnp.sqrt(s2)
        gx = ig * xc
        rows = lax.broadcasted_iota(jnp.int32, (t, w), 0)
        ugv = ug_ref[...]
        dyv = dy_ref[...].astype(F32)
        hv = h_ref[...]

        acc_g = dyv * _gelu(ugv)
        acc_a = _shift_up(a, 1, acar[0:1, :], rows, t)
        for d in steps:
            acc_g = acc_a * _shift_up(acc_g, d, 0.0, rows, t) + acc_g
            acc_a = acc_a * _shift_up(acc_a, d, 1.0, rows, t)
        dh = acc_a * dhcar[0:1, :] + acc_g

        hprev = _shift_down(hv, 1, hprev_ref[V7X_SUBLANES - 1:V7X_SUBLANES, :] * keep, rows)
        d_a = dh * hprev
        d_sq = dh * gx
        d_gx = dh * sq
        d_ig = d_gx * xc
        dxc = d_gx * ig
        d_log_a = d_a * a - d_sq * (1.0 - s2) / sq
        d_ra = d_log_a * (-LRU_C * sp)
        lamv = lam_ref[...]
        dlam_ref[...] += jnp.sum(d_log_a * (-LRU_C * ra), axis=0, keepdims=True) * (-_sigmoid(-lamv))
        dpa = d_ra * ra * (1.0 - ra)
        dpx = d_ig * ig * (1.0 - ig)
        dba_ref[...] += jnp.sum(dpa, axis=0, keepdims=True)
        dbx_ref[...] += jnp.sum(dpx, axis=0, keepdims=True)
        dwa_ref[...] += _dot_tn(xc, dpa)
        dwx_ref[...] += _dot_tn(xc, dpx)
        dxc = dxc + _dot_nt(dpa, wa_ref[...]) + _dot_nt(dpx, wx_ref[...])
        dcb_ref[...] += jnp.sum(dxc, axis=0, keepdims=True)
        for k in range(CONV_WIDTH):
            dcw_ref[k:k + 1, :] += jnp.sum(dxc * shifted[k], axis=0, keepdims=True)

        dbuf[0:t, :] = dxc
        bigd = dbuf[...]
        du = dxc * cw_ref[CONV_WIDTH - 1:CONV_WIDTH, :]
        for k in range(CONV_WIDTH - 1):
            e = CONV_WIDTH - 1 - k
            du = du + pltpu.roll(bigd, t + V7X_SUBLANES - e, 0)[0:t] * cw_ref[k:k + 1, :]
        dz_ref[:, 0:w] = du.astype(dz_ref.dtype)
        dz_ref[:, w:2 * w] = (dyv * hv * _gelu_grad(ugv)).astype(dz_ref.dtype)

        dbuf[t:t + V7X_SUBLANES, :] = dbuf[0:V7X_SUBLANES, :]
        tmp[...] = a
        acar[0:1, :] = tmp[0:1, :]
        tmp[...] = dh
        dhcar[0:1, :] = tmp[0:1, :]

    vec = _full((1, w))
    rev = lambda cbk: pl.BlockSpec((t, w), lambda i: (nt - 1 - i, cbk))
    prev8 = lambda cbk: pl.BlockSpec((V7X_SUBLANES, w),
                                     lambda i: (jnp.maximum((nt - 1 - i) * per8 - 1, 0), cbk))
    return pl.pallas_call(
        body, name=name, grid=(nt,),
        in_specs=[rev(0), rev(1), rev(0), rev(0), prev8(0), prev8(0), _full((CONV_WIDTH, w)), vec, _full((w, w)),
                  vec, _full((w, w)), vec, vec],
        out_specs=[pl.BlockSpec((t, 2 * w), lambda i: (nt - 1 - i, 0)), _full((CONV_WIDTH, w)), vec,
                   _full((w, w)), vec, _full((w, w)), vec, vec],
        out_shape=[jax.ShapeDtypeStruct((s, 2 * w), MXU_DTYPE), jax.ShapeDtypeStruct((CONV_WIDTH, w), F32),
                   jax.ShapeDtypeStruct((1, w), F32), jax.ShapeDtypeStruct((w, w), F32),
                   jax.ShapeDtypeStruct((1, w), F32), jax.ShapeDtypeStruct((w, w), F32),
                   jax.ShapeDtypeStruct((1, w), F32), jax.ShapeDtypeStruct((1, w), F32)],
        scratch_shapes=[pltpu.VMEM((t + V7X_SUBLANES, w), F32), pltpu.VMEM((t + V7X_SUBLANES, w), F32),
                        pltpu.VMEM((V7X_SUBLANES, w), F32), pltpu.VMEM((V7X_SUBLANES, w), F32),
                        pltpu.VMEM((t, w), F32)],
        compiler_params=_params(("arbitrary",), 80 * _nbytes((t, w), F32)),
    )(z, z, h, dy, z, h, cw, cb, wa, ba, wx, bx, lam)


def _rope_apply(v, cos, sin, width):
    half = MLA_ROPE // 2
    lanes = lax.broadcasted_iota(jnp.int32, v.shape, 1)
    first = (lanes % MLA_ROPE) < half
    partner = jnp.where(first, pltpu.roll(v, width - half, 1), pltpu.roll(v, half, 1))
    return v * cos + partner * sin


def _rope(x, cos, sin, *, width, cb, out_dtype, sum_heads=False, name):
    s = x.shape[0]
    t = _tile(s, 512)
    out_w = V7X_LANES if sum_heads else width

    def body(x_ref, c_ref, s_ref, o_ref):
        v = x_ref[...].astype(F32)
        if sum_heads:
            v = v[:, 0:V7X_LANES] + v[:, V7X_LANES:2 * V7X_LANES]
            v = v + pltpu.roll(v, 64, 1)
            v = v + pltpu.roll(v, 32, 1)
            out = _rope_apply(v, c_ref[...], s_ref[...], V7X_LANES)
            lanes = lax.broadcasted_iota(jnp.int32, out.shape, 1)
            out = jnp.where(lanes < MLA_ROPE, out, 0.0)
        else:
            out = _rope_apply(v, c_ref[...], s_ref[...], width)
        o_ref[...] = out.astype(o_ref.dtype)

    return pl.pallas_call(
        body, name=name, grid=(s // t,),
        in_specs=[_rows(t, width, cb), _rows(t, out_w), _rows(t, out_w)], out_specs=_rows(t, out_w),
        out_shape=jax.ShapeDtypeStruct((s, out_w), out_dtype),
        compiler_params=_params(("parallel",), 12 * _nbytes((t, width), F32)),
    )(x, cos, sin)


def _visible(t, unit, transposed):
    q_idx = lax.broadcasted_iota(jnp.int32, (t, t), 1 if transposed else 0)
    k_idx = lax.broadcasted_iota(jnp.int32, (t, t), 0 if transposed else 1)
    shift = int(math.log2(unit))
    return (q_idx >> shift) >= (k_idx >> shift)


def _attn_tile(s):
    return min(512, s // 4)


def _attn_fwd(q, k, v, cq, ck, *, scale, unit, name):
    hn, s, dk = q.shape
    dv = v.shape[-1]
    t = _attn_tile(s)
    decay = cq is not None

    def body(*refs):
        q_ref, k_ref, v_ref = refs[:3]
        cq_ref, ck_ref = (refs[3], refs[4]) if decay else (None, None)
        o_ref, lse_ref = refs[-2], refs[-1]
        i = pl.program_id(1)
        qt = q_ref[0]

        def tile(j, carry, masked):
            m, l, acc = carry
            off = pl.multiple_of(j * t, t)
            kt = k_ref[0, pl.ds(off, t), :]
            vt = v_ref[0, pl.ds(off, t), :]
            sc = _dot_nt(qt, kt) * scale
            if decay:
                sc = sc + cq_ref[0] - ck_ref[0, :, pl.ds(off, t)]
            if masked:
                sc = jnp.where(_visible(t, unit, False), sc, NEG_INF)
            m_new = jnp.maximum(m, jnp.max(sc, axis=-1, keepdims=True))
            alpha = jnp.exp(m - m_new)
            pr = jnp.exp(sc - m_new)
            l = alpha * l + jnp.sum(pr, axis=-1, keepdims=True)
            acc = alpha * acc + _dot(pr, vt)
            return m_new, l, acc

        init = (jnp.full((t, 1), NEG_INF, F32), jnp.zeros((t, 1), F32), jnp.zeros((t, dv), F32))
        carry = lax.fori_loop(0, i, lambda j, c: tile(j, c, False), init)
        m, l, acc = tile(i, carry, True)
        o_ref[0] = (acc / l).astype(o_ref.dtype)
        lse_ref[0] = m + jnp.log(l)

    qs = lambda d: pl.BlockSpec((1, t, d), lambda h, i: (h, i, 0))
    whole = lambda d: pl.BlockSpec((1, s, d), lambda h, i: (h, 0, 0))
    in_specs = [qs(dk), whole(dk), whole(dv)]
    ins = [q, k, v]
    if decay:
        in_specs += [qs(1), pl.BlockSpec((1, 1, s), lambda h, i: (h, 0, 0))]
        ins += [cq, ck]
    vmem = 4 * _nbytes((s, dk + dv), q.dtype) + 10 * _nbytes((t, t), F32) + 8 * _nbytes((t, V7X_LANES), F32)
    return pl.pallas_call(
        body, name=name, grid=(hn, s // t),
        in_specs=in_specs, out_specs=[qs(dv), qs(1)],
        out_shape=[jax.ShapeDtypeStruct((hn, s, dv), MXU_DTYPE), jax.ShapeDtypeStruct((hn, s, 1), F32)],
        compiler_params=_params(("parallel", "arbitrary"), vmem),
    )(*ins)


def _attn_bwd_q(q, k, v, do, lse, cq, ck, *, scale, unit, name):
    hn, s, dk = q.shape
    dv = v.shape[-1]
    t = _attn_tile(s)
    nt = s // t
    decay = cq is not None

    def body(*refs):
        q_ref, k_ref, v_ref, do_ref, lse_ref = refs[:5]
        cq_ref, ck_ref = (refs[5], refs[6]) if decay else (None, None)
        dq_ref, dl_ref, p_sc, dp_sc = refs[-4:]
        i = pl.program_id(1)
        qt = q_ref[0]
        dot = do_ref[0]
        lse_t = lse_ref[0]

        def sweep1(j, delta, masked):
            off = pl.multiple_of(j * t, t)
            kt = k_ref[0, pl.ds(off, t), :]
            vt = v_ref[0, pl.ds(off, t), :]
            sc = _dot_nt(qt, kt) * scale
            if decay:
                sc = sc + cq_ref[0] - ck_ref[0, :, pl.ds(off, t)]
            if masked:
                sc = jnp.where(_visible(t, unit, False), sc, NEG_INF)
            pr = jnp.exp(sc - lse_t)
            dp = _dot_nt(dot, vt)
            p_sc[j] = pr
            dp_sc[j] = dp
            return delta + jnp.sum(pr * dp, axis=-1, keepdims=True)

        delta = lax.fori_loop(0, i, lambda j, c: sweep1(j, c, False), jnp.zeros((t, 1), F32))
        delta = sweep1(i, delta, True)

        def sweep2(j, dq):
            off = pl.multiple_of(j * t, t)
            ds = p_sc[j] * (dp_sc[j] - delta)
            return dq + _dot(ds, k_ref[0, pl.ds(off, t), :])

        dq = lax.fori_loop(0, i + 1, sweep2, jnp.zeros((t, dk), F32))
        dq_ref[0] = dq * scale
        dl_ref[0] = delta

    qs = lambda d: pl.BlockSpec((1, t, d), lambda h, i: (h, i, 0))
    whole = lambda d: pl.BlockSpec((1, s, d), lambda h, i: (h, 0, 0))
    in_specs = [qs(dk), whole(dk), whole(dv), qs(dv), qs(1)]
    ins = [q, k, v, do, lse]
    if decay:
        in_specs += [qs(1), pl.BlockSpec((1, 1, s), lambda h, i: (h, 0, 0))]
        ins += [cq, ck]
    vmem = (4 * _nbytes((s, dk + dv), q.dtype) + 2 * _nbytes((nt, t, t), F32) + 8 * _nbytes((t, t), F32)
            + 12 * _nbytes((t, V7X_LANES), F32))
    return pl.pallas_call(
        body, name=name, grid=(hn, nt),
        in_specs=in_specs, out_specs=[qs(dk), qs(1)],
        out_shape=[jax.ShapeDtypeStruct((hn, s, dk), F32), jax.ShapeDtypeStruct((hn, s, 1), F32)],
        scratch_shapes=[pltpu.VMEM((nt, t, t), F32), pltpu.VMEM((nt, t, t), F32)],
        compiler_params=_params(("parallel", "arbitrary"), vmem),
    )(*ins)


def _attn_bwd_kv(q, k, v, do, lse_row, delta_row, cq_row, ck, *, scale, unit, name):
    hn, s, dk = q.shape
    dv = v.shape[-1]
    t = _attn_tile(s)
    nt = s // t
    decay = ck is not None

    def body(*refs):
        k_ref, v_ref, q_ref, do_ref, lse_ref, dl_ref = refs[:6]
        ck_ref, cq_ref = (refs[6], refs[7]) if decay else (None, None)
        outs = refs[8:] if decay else refs[6:]
        dk_ref, dv_ref = outs[0], outs[1]
        j = pl.program_id(1)
        kt = k_ref[0]
        vt = v_ref[0]

        def tile(i, carry, masked):
            dk_acc, dv_acc, dc_acc = carry
            off = pl.multiple_of(i * t, t)
            qt = q_ref[0, pl.ds(off, t), :]
            dot = do_ref[0, pl.ds(off, t), :]
            sc = _dot_nt(kt, qt) * scale
            if decay:
                sc = sc + cq_ref[0, :, pl.ds(off, t)] - ck_ref[0]
            if masked:
                sc = jnp.where(_visible(t, unit, True), sc, NEG_INF)
            pr = jnp.exp(sc - lse_ref[0, :, pl.ds(off, t)])
            dv_acc = dv_acc + _dot(pr, dot)
            ds = pr * (_dot_nt(vt, dot) - dl_ref[0, :, pl.ds(off, t)])
            dk_acc = dk_acc + _dot(ds, qt)
            if decay:
                dc_acc = dc_acc + jnp.sum(ds, axis=-1, keepdims=True)
            return dk_acc, dv_acc, dc_acc

        init = (jnp.zeros((t, dk), F32), jnp.zeros((t, dv), F32), jnp.zeros((t, 1), F32))
        carry = tile(j, init, True)
        dk_acc, dv_acc, dc_acc = lax.fori_loop(j + 1, nt, lambda i, c: tile(i, c, False), carry)
        dk_ref[0] = dk_acc * scale
        dv_ref[0] = dv_acc
        if decay:
            outs[2][0] = -dc_acc

    ks = lambda d: pl.BlockSpec((1, t, d), lambda h, j: (h, j, 0))
    whole = lambda d: pl.BlockSpec((1, s, d), lambda h, j: (h, 0, 0))
    row = pl.BlockSpec((1, 1, s), lambda h, j: (h, 0, 0))
    in_specs = [ks(dk), ks(dv), whole(dk), whole(dv), row, row]
    ins = [k, v, q, do, lse_row, delta_row]
    out_specs = [ks(dk), ks(dv)]
    out_shape = [jax.ShapeDtypeStruct((hn, s, dk), F32), jax.ShapeDtypeStruct((hn, s, dv), F32)]
    if decay:
        in_specs += [ks(1), row]
        ins += [ck, cq_row]
        out_specs.append(ks(1))
        out_shape.append(jax.ShapeDtypeStruct((hn, s, 1), F32))
    vmem = 4 * _nbytes((s, dk + dv), q.dtype) + 10 * _nbytes((t, t), F32) + 12 * _nbytes((t, V7X_LANES), F32)
    return pl.pallas_call(
        body, name=name, grid=(hn, nt),
        in_specs=in_specs, out_specs=out_specs, out_shape=out_shape,
        compiler_params=_params(("parallel", "arbitrary"), vmem),
    )(*ins)


STRIP = 32
HEAD_PAIRS = HEADS // 2


def _strip_rows(t):
    return min(STRIP, t)


def _pair_mask(t):
    lane = lax.broadcasted_iota(jnp.int32, (t, V7X_LANES), 1)
    return lane < (V7X_LANES // 2)


def _strip_visible(r, t, row0, unit, transposed):
    rows = lax.broadcasted_iota(jnp.int32, (r, t), 0) + row0
    cols = lax.broadcasted_iota(jnp.int32, (r, t), 1)
    shift = int(math.log2(unit))
    if transposed:
        return (cols >> shift) >= (rows >> shift)
    return (rows >> shift) >= (cols >> shift)


def _rope_lanes(x, cos, sin, *, cb, groups, out_dtype, sum_parts=0, name):
    s = cos.shape[0]
    t = _tile(s, 512)
    w = groups * V7X_LANES

    def body(x_ref, c_ref, s_ref, o_ref):
        if sum_parts:
            v = x_ref[0].astype(F32)
            for part in range(1, sum_parts):
                v = v + x_ref[part].astype(F32)
            o_ref[...] = _rope_apply(v, c_ref[...], s_ref[...], V7X_LANES).astype(o_ref.dtype)
        else:
            for g in range(groups):
                sl = slice(g * V7X_LANES, (g + 1) * V7X_LANES)
                o_ref[:, sl] = _rope_apply(x_ref[:, sl].astype(F32), c_ref[...], s_ref[...],
                                           V7X_LANES).astype(o_ref.dtype)

    x_spec = (pl.BlockSpec((sum_parts, t, V7X_LANES), lambda i: (0, i, 0)) if sum_parts else _rows(t, w, cb))
    return pl.pallas_call(
        body, name=name, grid=(s // t,),
        in_specs=[x_spec, _rows(t, V7X_LANES), _rows(t, V7X_LANES)], out_specs=_rows(t, w),
        out_shape=jax.ShapeDtypeStruct((s, w), out_dtype),
        compiler_params=_params(("parallel",), 12 * _nbytes((t, max(w, 4 * V7X_LANES)), F32)),
    )(x, cos, sin)


def _pair_fwd(q_arr, q_cb, k_arr, k_cb, v_arr, v_cb, rope, decay, *, scale, unit, name):
    s = q_arr.shape[0]
    t = _attn_tile(s)
    r = _strip_rows(t)
    has_rope, has_decay = rope is not None, decay is not None
    kw = 2 * V7X_LANES if has_rope else V7X_LANES

    def body(*refs):
        it = iter(refs)
        q_ref, k_ref, v_ref = next(it), next(it), next(it)
        qr_ref, kr_ref = (next(it), next(it)) if has_rope else (None, None)
        cq_ref, ck_ref = (next(it), next(it)) if has_decay else (None, None)
        o_ref, lse_ref = next(it), next(it)
        q_sc, s_sc, p_sc, acc_sc, mx_sc, ls_sc = (next(it) for _ in range(6))
        i = pl.program_id(1)
        in_a = _pair_mask(t)
        qv = q_ref[...]
        for hd in range(2):
            q_sc[hd, :, 0:V7X_LANES] = jnp.where(in_a if hd == 0 else jnp.logical_not(in_a), qv, 0).astype(MXU_DTYPE)
            if has_rope:
                q_sc[hd, :, V7X_LANES:kw] = qr_ref[:, hd * V7X_LANES:(hd + 1) * V7X_LANES].astype(MXU_DTYPE)
        mx_sc[...] = jnp.full(mx_sc.shape, NEG_INF, F32)
        ls_sc[...] = jnp.zeros(ls_sc.shape, F32)
        acc_sc[...] = jnp.zeros(acc_sc.shape, F32)
        cq_all = [cq_ref[hd] for hd in range(2)] if has_decay else None
        chunks = t // V7X_LANES

        def keys(j):
            off = pl.multiple_of(j * t, t)
            kt = k_ref[pl.ds(off, t), :]
            if has_rope:
                kt = jnp.concatenate([kt, kr_ref[pl.ds(off, t), :]], axis=-1)
            return off, kt

        def strip_scores(hd, row0, ck_row, masked):
            sc = s_sc[hd, pl.ds(row0, r), :] * scale
            if has_decay:
                sc = sc + (cq_all[hd][row0:row0 + r] - ck_row)
            if masked:
                sc = jnp.where(_strip_visible(r, t, row0, unit, False), sc, NEG_INF)
            return sc

        def fold(v, op):
            out = v[:, 0:V7X_LANES]
            for ch in range(1, chunks):
                out = op(out, v[:, ch * V7X_LANES:(ch + 1) * V7X_LANES])
            return out

        def tile_max(j, masked):
            off, kt = keys(j)
            for hd in range(2):
                s_sc[hd] = _dot_nt(q_sc[hd], kt)
                ck_row = ck_ref[hd, :, pl.ds(off, t)] if has_decay else None
                for b in range(t // r):
                    rows = pl.ds(b * r, r)
                    sc = strip_scores(hd, b * r, ck_row, masked)
                    mx_sc[hd, rows, :] = jnp.maximum(mx_sc[hd, rows, :], fold(sc, jnp.maximum))

        lax.fori_loop(0, i, lambda j, c: (tile_max(j, False), c)[1], 0)
        tile_max(i, True)
        m_all = [jnp.max(mx_sc[hd], axis=-1, keepdims=True) for hd in range(2)]

        def tile_sum(j, masked):
            off, kt = keys(j)
            vt = v_ref[pl.ds(off, t), :]
            for hd in range(2):
                s_sc[hd] = _dot_nt(q_sc[hd], kt)
                ck_row = ck_ref[hd, :, pl.ds(off, t)] if has_decay else None
                for b in range(t // r):
                    row0 = b * r
                    rows = pl.ds(row0, r)
                    pr = jnp.exp(strip_scores(hd, row0, ck_row, masked) - m_all[hd][row0:row0 + r])
                    ls_sc[hd, rows, :] += fold(pr, jnp.add)
                    p_sc[hd, rows, :] = pr.astype(MXU_DTYPE)
                acc_sc[hd] += _dot(p_sc[hd], vt)

        lax.fori_loop(0, i, lambda j, c: (tile_sum(j, False), c)[1], 0)
        tile_sum(i, True)
        l_all = [jnp.sum(ls_sc[hd], axis=-1, keepdims=True) for hd in range(2)]
        o_ref[...] = jnp.where(in_a, acc_sc[0] / l_all[0], acc_sc[1] / l_all[1]).astype(o_ref.dtype)
        for hd in range(2):
            lse_ref[hd] = m_all[hd] + jnp.log(l_all[hd])

    blk = lambda cb: pl.BlockSpec((t, V7X_LANES), lambda p, i: (i, cb + p))
    whole = lambda cb: pl.BlockSpec((s, V7X_LANES), lambda p, i: (0, cb + p))
    stat = pl.BlockSpec((2, t, 1), lambda p, i: (p, i, 0))
    in_specs = [blk(q_cb), whole(k_cb), whole(v_cb)]
    ins = [q_arr, k_arr, v_arr]
    if has_rope:
        in_specs += [pl.BlockSpec((t, 2 * V7X_LANES), lambda p, i: (i, p)),
                     pl.BlockSpec((s, V7X_LANES), lambda p, i: (0, 0))]
        ins += list(rope)
    if has_decay:
        in_specs += [stat, pl.BlockSpec((2, 1, s), lambda p, i: (p, 0, 0))]
        ins += list(decay)
    col = (2, t, 1)
    vmem = (6 * _nbytes((s, V7X_LANES), MXU_DTYPE) + 6 * _nbytes((t, t), F32) + 10 * _nbytes((t, V7X_LANES), F32)
            + 8 * _nbytes((2, t, V7X_LANES), F32))
    return pl.pallas_call(
        body, name=name, grid=(HEAD_PAIRS, s // t),
        in_specs=in_specs, out_specs=[pl.BlockSpec((t, V7X_LANES), lambda p, i: (i, p)), stat],
        out_shape=[jax.ShapeDtypeStruct((s, HEADS * 64), MXU_DTYPE), jax.ShapeDtypeStruct((HEADS, s, 1), F32)],
        scratch_shapes=[pltpu.VMEM((2, t, kw), MXU_DTYPE), pltpu.VMEM((2, t, t), F32), pltpu.VMEM((2, t, t), MXU_DTYPE),
                        pltpu.VMEM((2, t, V7X_LANES), F32), pltpu.VMEM((2, t, V7X_LANES), F32),
                        pltpu.VMEM((2, t, V7X_LANES), F32)],
        compiler_params=_params(("parallel", "arbitrary"), vmem),
    )(*ins)


def _pair_bwd_q(q_arr, q_cb, k_arr, k_cb, v_arr, v_cb, do, lse, rope, decay, *, scale, unit, name):
    s = q_arr.shape[0]
    t = _attn_tile(s)
    nt = s // t
    r = t
    has_rope, has_decay = rope is not None, decay is not None
    kw = 2 * V7X_LANES if has_rope else V7X_LANES

    def body(*refs):
        it = iter(refs)
        q_ref, k_ref, v_ref, do_ref, lse_ref = (next(it) for _ in range(5))
        qr_ref, kr_ref = (next(it), next(it)) if has_rope else (None, None)
        cq_ref, ck_ref = (next(it), next(it)) if has_decay else (None, None)
        dq_ref, dl_ref = next(it), next(it)
        dqr_ref = next(it) if has_rope else None
        q_sc, do_sc, p_sc, dp_sc, ds_sc, dq_sc, dl_sc, s_sc = (next(it) for _ in range(8))
        i = pl.program_id(1)
        in_a = _pair_mask(t)
        qv = q_ref[...]
        dov = do_ref[...]
        for hd in range(2):
            sel = in_a if hd == 0 else jnp.logical_not(in_a)
            q_sc[hd, :, 0:V7X_LANES] = jnp.where(sel, qv, 0).astype(MXU_DTYPE)
            if has_rope:
                q_sc[hd, :, V7X_LANES:kw] = qr_ref[:, hd * V7X_LANES:(hd + 1) * V7X_LANES].astype(MXU_DTYPE)
            do_sc[hd] = jnp.where(sel, dov, 0).astype(MXU_DTYPE)
        dl_sc[...] = jnp.zeros(dl_sc.shape, F32)
        dq_sc[...] = jnp.zeros(dq_sc.shape, F32)

        def keys(j):
            off = pl.multiple_of(j * t, t)
            kt = k_ref[pl.ds(off, t), :]
            if has_rope:
                kt = jnp.concatenate([kt, kr_ref[pl.ds(off, t), :]], axis=-1)
            return off, kt

        for hd in range(2):
            lse_all = lse_ref[hd]
            cq_all = cq_ref[hd] if has_decay else None

            def sweep1(j, masked, hd=hd, lse_all=lse_all, cq_all=cq_all):
                off, kt = keys(j)
                s_sc[...] = _dot_nt(q_sc[hd], kt)
                dp_sc[j] = _dot_nt(do_sc[hd], v_ref[pl.ds(off, t), :])
                ck_row = ck_ref[hd, :, pl.ds(off, t)] if has_decay else None
                parts = []
                for b in range(t // r):
                    row0 = b * r
                    rows = pl.ds(row0, r)
                    sc = s_sc[rows, :] * scale
                    if has_decay:
                        sc = sc + (cq_all[row0:row0 + r] - ck_row)
                    if masked:
                        sc = jnp.where(_strip_visible(r, t, row0, unit, False), sc, NEG_INF)
                    pr = jnp.exp(sc - lse_all[row0:row0 + r])
                    p_sc[j, rows, :] = pr
                    parts.append(jnp.sum(pr * dp_sc[j, rows, :], axis=-1, keepdims=True))
                dl_sc[hd] += jnp.concatenate(parts, axis=0)

            def sweep1_unmasked(j, carry, sweep1=sweep1):
                sweep1(j, False)
                return carry

            lax.fori_loop(0, i, sweep1_unmasked, 0)
            sweep1(i, True)
            dl_all = dl_sc[hd]

            def sweep2(j, carry, hd=hd, dl_all=dl_all):
                _, kt = keys(j)
                for b in range(t // r):
                    row0 = b * r
                    rows = pl.ds(row0, r)
                    ds = p_sc[j, rows, :] * (dp_sc[j, rows, :] - dl_all[row0:row0 + r])
                    ds_sc[rows, :] = ds.astype(MXU_DTYPE)
                dq_sc[hd] += _dot(ds_sc[...], kt)
                return carry

            lax.fori_loop(0, i + 1, sweep2, 0)

        dq_ref[...] = (jnp.where(in_a, dq_sc[0, :, 0:V7X_LANES], dq_sc[1, :, 0:V7X_LANES]) * scale).astype(dq_ref.dtype)
        dl_ref[...] = dl_sc[...]
        if has_rope:
            dqr_ref[:, 0:V7X_LANES] = dq_sc[0, :, V7X_LANES:kw] * scale
            dqr_ref[:, V7X_LANES:kw] = dq_sc[1, :, V7X_LANES:kw] * scale

    blk = lambda cb: pl.BlockSpec((t, V7X_LANES), lambda p, i: (i, cb + p))
    whole = lambda cb: pl.BlockSpec((s, V7X_LANES), lambda p, i: (0, cb + p))
    stat = pl.BlockSpec((2, t, 1), lambda p, i: (p, i, 0))
    in_specs = [blk(q_cb), whole(k_cb), whole(v_cb), blk(0), stat]
    ins = [q_arr, k_arr, v_arr, do, lse]
    out_specs = [blk(0), stat]
    out_shape = [jax.ShapeDtypeStruct((s, HEADS * 64), MXU_DTYPE), jax.ShapeDtypeStruct((HEADS, s, 1), F32)]
    if has_rope:
        pair_rot = pl.BlockSpec((t, 2 * V7X_LANES), lambda p, i: (i, p))
        in_specs += [pair_rot, pl.BlockSpec((s, V7X_LANES), lambda p, i: (0, 0))]
        ins += list(rope)
        out_specs.append(pair_rot)
        out_shape.append(jax.ShapeDtypeStruct((s, HEADS * V7X_LANES), F32))
    if has_decay:
        in_specs += [stat, pl.BlockSpec((2, 1, s), lambda p, i: (p, 0, 0))]
        ins += list(decay)
    vmem = (6 * _nbytes((s, V7X_LANES), MXU_DTYPE) + 2 * _nbytes((nt, t, t), F32) + 6 * _nbytes((t, t), F32)
            + 16 * _nbytes((t, kw), F32))
    return pl.pallas_call(
        body, name=name, grid=(HEAD_PAIRS, nt),
        in_specs=in_specs, out_specs=out_specs, out_shape=out_shape,
        scratch_shapes=[pltpu.VMEM((2, t, kw), MXU_DTYPE), pltpu.VMEM((2, t, V7X_LANES), MXU_DTYPE),
                        pltpu.VMEM((nt, t, t), F32), pltpu.VMEM((nt, t, t), F32), pltpu.VMEM((t, t), MXU_DTYPE),
                        pltpu.VMEM((2, t, kw), F32), pltpu.VMEM((2, t, 1), F32), pltpu.VMEM((t, t), F32)],
        compiler_params=_params(("parallel", "arbitrary"), vmem),
    )(*ins)


def _pair_bwd_kv(q_arr, q_cb, k_arr, k_cb, v_arr, v_cb, do, lse_row, delta_row, rope, decay, *, scale, unit, name):
    s = q_arr.shape[0]
    t = _attn_tile(s)
    nt = s // t
    r = _strip_rows(t)
    has_rope, has_decay = rope is not None, decay is not None
    kw = 2 * V7X_LANES if has_rope else V7X_LANES

    def body(*refs):
        it = iter(refs)
        k_ref, v_ref, q_ref, do_ref, lse_ref, dl_ref = (next(it) for _ in range(6))
        qr_ref, kr_ref = (next(it), next(it)) if has_rope else (None, None)
        ck_ref, cq_ref = (next(it), next(it)) if has_decay else (None, None)
        dk_ref, dv_ref = next(it), next(it)
        dkr_ref = next(it) if has_rope else None
        dc_ref = next(it) if has_decay else None
        k_sc, v_sc, st_sc, dpt_sc, pt_sc, dst_sc, dk_sc, dv_sc, dc_sc = (next(it) for _ in range(9))
        j = pl.program_id(1)
        in_a = _pair_mask(t)
        kv_, vv_ = k_ref[...], v_ref[...]
        for hd in range(2):
            sel = in_a if hd == 0 else jnp.logical_not(in_a)
            k_sc[hd, :, 0:V7X_LANES] = jnp.where(sel, kv_, 0).astype(MXU_DTYPE)
            if has_rope:
                k_sc[hd, :, V7X_LANES:kw] = kr_ref[...].astype(MXU_DTYPE)
            v_sc[hd] = jnp.where(sel, vv_, 0).astype(MXU_DTYPE)
        dk_sc[...] = jnp.zeros(dk_sc.shape, F32)
        dv_sc[...] = jnp.zeros(dv_sc.shape, F32)
        dc_sc[...] = jnp.zeros(dc_sc.shape, F32)

        def tile(i, masked):
            off = pl.multiple_of(i * t, t)
            qt = q_ref[pl.ds(off, t), :]
            dot = do_ref[pl.ds(off, t), :]
            for hd in range(2):
                qcat = qt
                if has_rope:
                    qcat = jnp.concatenate([qt, qr_ref[pl.ds(off, t), hd * V7X_LANES:(hd + 1) * V7X_LANES]], axis=-1)
                st_sc[hd] = _dot_nt(k_sc[hd], qcat)
                dpt_sc[hd] = _dot_nt(v_sc[hd], dot)
                lse_r = lse_ref[hd, :, pl.ds(off, t)]
                dl_r = dl_ref[hd, :, pl.ds(off, t)]
                cq_r = cq_ref[hd, :, pl.ds(off, t)] if has_decay else None
                ck_all = ck_ref[hd] if has_decay else None
                parts = []
                for b in range(t // r):
                    row0 = b * r
                    rows = pl.ds(row0, r)
                    sc = st_sc[hd, rows, :] * scale
                    if has_decay:
                        sc = sc + (cq_r - ck_all[row0:row0 + r])
                    if masked:
                        sc = jnp.where(_strip_visible(r, t, row0, unit, True), sc, NEG_INF)
                    pr = jnp.exp(sc - lse_r)
                    ds = pr * (dpt_sc[hd, rows, :] - dl_r)
                    pt_sc[hd, rows, :] = pr.astype(MXU_DTYPE)
                    dst_sc[hd, rows, :] = ds.astype(MXU_DTYPE)
                    if has_decay:
                        parts.append(jnp.sum(ds, axis=-1, keepdims=True))
                if has_decay:
                    dc_sc[hd] += jnp.concatenate(parts, axis=0)
                dv_sc[hd] += _dot(pt_sc[hd], dot)
                dk_sc[hd] += _dot(dst_sc[hd], qcat)

        tile(j, True)

        def unmasked(i, carry):
            tile(i, False)
            return carry

        lax.fori_loop(j + 1, nt, unmasked, 0)
        dk_ref[...] = (jnp.where(in_a, dk_sc[0, :, 0:V7X_LANES], dk_sc[1, :, 0:V7X_LANES]) * scale).astype(dk_ref.dtype)
        dv_ref[...] = jnp.where(in_a, dv_sc[0], dv_sc[1]).astype(dv_ref.dtype)
        if has_rope:
            dkr_ref[0] = (dk_sc[0, :, V7X_LANES:kw] + dk_sc[1, :, V7X_LANES:kw]) * scale
        if has_decay:
            dc_ref[...] = -dc_sc[...]

    blk = lambda cb: pl.BlockSpec((t, V7X_LANES), lambda p, j: (j, cb + p))
    whole = lambda cb: pl.BlockSpec((s, V7X_LANES), lambda p, j: (0, cb + p))
    stat = pl.BlockSpec((2, t, 1), lambda p, j: (p, j, 0))
    row = pl.BlockSpec((2, 1, s), lambda p, j: (p, 0, 0))
    in_specs = [blk(k_cb), blk(v_cb), whole(q_cb), whole(0), row, row]
    ins = [k_arr, v_arr, q_arr, do, lse_row, delta_row]
    out_specs = [blk(0), blk(0)]
    out_shape = [jax.ShapeDtypeStruct((s, HEADS * 64), MXU_DTYPE)] * 2
    if has_rope:
        in_specs += [pl.BlockSpec((s, 2 * V7X_LANES), lambda p, j: (0, p)),
                     pl.BlockSpec((t, V7X_LANES), lambda p, j: (j, 0))]
        ins += list(rope)
        out_specs.append(pl.BlockSpec((1, t, V7X_LANES), lambda p, j: (p, j, 0)))
        out_shape.append(jax.ShapeDtypeStruct((HEAD_PAIRS, s, V7X_LANES), F32))
    if has_decay:
        in_specs += [stat, row]
        ins += list(decay)
        out_specs.append(stat)
        out_shape.append(jax.ShapeDtypeStruct((HEADS, s, 1), F32))
    vmem = (12 * _nbytes((s, V7X_LANES), MXU_DTYPE) + 8 * _nbytes((t, t), F32) + 16 * _nbytes((t, kw), F32))
    return pl.pallas_call(
        body, name=name, grid=(HEAD_PAIRS, nt),
        in_specs=in_specs, out_specs=out_specs, out_shape=out_shape,
        scratch_shapes=[pltpu.VMEM((2, t, kw), MXU_DTYPE), pltpu.VMEM((2, t, V7X_LANES), MXU_DTYPE),
                        pltpu.VMEM((2, t, t), F32), pltpu.VMEM((2, t, t), F32), pltpu.VMEM((2, t, t), MXU_DTYPE),
                        pltpu.VMEM((2, t, t), MXU_DTYPE), pltpu.VMEM((2, t, kw), F32),
                        pltpu.VMEM((2, t, V7X_LANES), F32), pltpu.VMEM((2, t, 1), F32)],
        compiler_params=_params(("parallel", "arbitrary"), vmem),
    )(*ins)


def _fox_cum(z, bf, *, name):
    s = z.shape[0]
    w = V7X_LANES
    t = _row_tile(s, 512)
    steps = [1 << k for k in range(int(math.log2(t)))]
    cb = SEG["fl"][3] // w

    def body(f_ref, bf_ref, c_ref, car):
        @pl.when(pl.program_id(0) == 0)
        def _():
            car[...] = jnp.zeros_like(car)

        acc = -_softplus(-(f_ref[...] + bf_ref[...]))
        rows = lax.broadcasted_iota(jnp.int32, (t, w), 0)
        for d in steps:
            acc = acc + _shift_down(acc, d, 0.0, rows)
        c_ref[...] = acc + car[0:1, :]
        car[0:1, :] = c_ref[t - 1:t, :]

    return pl.pallas_call(
        body, name=name, grid=(s // t,),
        in_specs=[_rows(t, w, cb), _full((1, w))], out_specs=_rows(t, w),
        out_shape=jax.ShapeDtypeStruct((s, w), F32),
        scratch_shapes=[pltpu.VMEM((V7X_SUBLANES, w), F32)],
        compiler_params=_params(("arbitrary",), 16 * _nbytes((t, w), F32)),
    )(z, bf)


def _fox_cum_bwd(z, bf, dcum, *, name):
    s = z.shape[0]
    w = V7X_LANES
    t = _row_tile(s, 512)
    nt = s // t
    steps = [1 << k for k in range(int(math.log2(t)))]
    cb = SEG["fl"][3] // w

    def body(f_ref, bf_ref, dc_ref, df_ref, dbf_ref, car, tmp):
        @pl.when(pl.program_id(0) == 0)
        def _():
            car[...] = jnp.zeros_like(car)
            dbf_ref[...] = jnp.zeros_like(dbf_ref)

        acc = dc_ref[...]
        rows = lax.broadcasted_iota(jnp.int32, (t, w), 0)
        for d in steps:
            acc = acc + _shift_up(acc, d, 0.0, rows, t)
        dlf = acc + car[0:1, :]
        tmp[...] = dlf
        car[0:1, :] = tmp[0:1, :]
        df = dlf * _sigmoid(-(f_ref[...] + bf_ref[...]))
        df_ref[...] = df.astype(df_ref.dtype)
        dbf_ref[...] += jnp.sum(df, axis=0, keepdims=True)

    rev = lambda cbk: pl.BlockSpec((t, w), lambda i: (nt - 1 - i, cbk))
    return pl.pallas_call(
        body, name=name, grid=(nt,),
        in_specs=[rev(cb), _full((1, w)), rev(0)], out_specs=[rev(0), _full((1, w))],
        out_shape=[jax.ShapeDtypeStruct((s, w), MXU_DTYPE), jax.ShapeDtypeStruct((1, w), F32)],
        scratch_shapes=[pltpu.VMEM((V7X_SUBLANES, w), F32), pltpu.VMEM((t, w), F32)],
        compiler_params=_params(("arbitrary",), 16 * _nbytes((t, w), F32)),
    )(z, bf, dcum)


_GATE_CB = SEG["gate"][3] // D_MODEL


def _merge_fwd(ya, yb, yc, z, gate_b, *, name):
    s = ya.shape[0]
    d = D_MODEL
    t = _tile(s, 256)

    def body(ya_ref, yb_ref, yc_ref, g0_ref, g1_ref, g2_ref, gb_ref, o_ref):
        out = _sigmoid(g0_ref[...] + gb_ref[:, 0:d]) * ya_ref[...]
        out = out + _sigmoid(g1_ref[...] + gb_ref[:, d:2 * d]) * yb_ref[...]
        out = out + _sigmoid(g2_ref[...] + gb_ref[:, 2 * d:3 * d]) * yc_ref[...]
        o_ref[...] = out.astype(o_ref.dtype)

    return pl.pallas_call(
        body, name=name, grid=(s // t,),
        in_specs=[_rows(t, d)] * 3 + [_rows(t, d, _GATE_CB + b) for b in range(3)] + [_full((1, 3 * d))],
        out_specs=_rows(t, d), out_shape=jax.ShapeDtypeStruct((s, d), MXU_DTYPE),
        compiler_params=_params(("parallel",), 20 * _nbytes((t, d), F32)),
    )(ya, yb, yc, z, z, z, gate_b)


def _merge_bwd(dm, ya, yb, yc, z, gate_b, *, name):
    s = ya.shape[0]
    d = D_MODEL
    t = _tile(s, 256)

    def body(dm_ref, ya_ref, yb_ref, yc_ref, g0_ref, g1_ref, g2_ref, gb_ref, da_ref, db_ref, dc_ref, dgl_ref,
             dgb_ref):
        dmv = dm_ref[...]
        parts = []
        for b, (y_ref, g_ref, dy_ref) in enumerate(((ya_ref, g0_ref, da_ref), (yb_ref, g1_ref, db_ref),
                                                    (yc_ref, g2_ref, dc_ref))):
            gate = _sigmoid(g_ref[...] + gb_ref[:, b * d:(b + 1) * d])
            dy_ref[...] = (dmv * gate).astype(dy_ref.dtype)
            dgl = dmv * y_ref[...] * gate * (1.0 - gate)
            dgl_ref[:, b * d:(b + 1) * d] = dgl.astype(dgl_ref.dtype)
            parts.append(jnp.sum(dgl, axis=0, keepdims=True))

        @pl.when(pl.program_id(0) == 0)
        def _():
            for b, part in enumerate(parts):
                dgb_ref[:, b * d:(b + 1) * d] = part

        @pl.when(pl.program_id(0) > 0)
        def _():
            for b, part in enumerate(parts):
                dgb_ref[:, b * d:(b + 1) * d] += part

    return pl.pallas_call(
        body, name=name, grid=(s // t,),
        in_specs=[_rows(t, d)] * 4 + [_rows(t, d, _GATE_CB + b) for b in range(3)] + [_full((1, 3 * d))],
        out_specs=[_rows(t, d)] * 3 + [_rows(t, 3 * d), _full((1, 3 * d))],
        out_shape=[jax.ShapeDtypeStruct((s, d), MXU_DTYPE)] * 3
        + [jax.ShapeDtypeStruct((s, 3 * d), MXU_DTYPE), jax.ShapeDtypeStruct((1, 3 * d), F32)],
        compiler_params=_params(("arbitrary",), 36 * _nbytes((t, d), F32)),
    )(dm, ya, yb, yc, z, z, z, gate_b)


def _swiglu_fwd(hf, *, name):
    s = hf.shape[0]
    t = _tile(s, 256)

    def body(g_ref, u_ref, o_ref):
        gv = g_ref[...]
        o_ref[...] = (gv * _sigmoid(gv) * u_ref[...]).astype(o_ref.dtype)

    return pl.pallas_call(
        body, name=name, grid=(s // t,),
        in_specs=[_rows(t, D_FF, 0), _rows(t, D_FF, 1)], out_specs=_rows(t, D_FF),
        out_shape=jax.ShapeDtypeStruct((s, D_FF), MXU_DTYPE),
        compiler_params=_params(("parallel",), 10 * _nbytes((t, D_FF), F32)),
    )(hf, hf)


def _swiglu_bwd(hf, dact, *, name):
    s = hf.shape[0]
    t = _tile(s, 256)

    def body(g_ref, u_ref, da_ref, o_ref):
        gv = g_ref[...]
        dav = da_ref[...]
        sg = _sigmoid(gv)
        o_ref[:, 0:D_FF] = (dav * u_ref[...] * sg * (1.0 + gv * (1.0 - sg))).astype(o_ref.dtype)
        o_ref[:, D_FF:2 * D_FF] = (dav * gv * sg).astype(o_ref.dtype)

    return pl.pallas_call(
        body, name=name, grid=(s // t,),
        in_specs=[_rows(t, D_FF, 0), _rows(t, D_FF, 1), _rows(t, D_FF)], out_specs=_rows(t, 2 * D_FF),
        out_shape=jax.ShapeDtypeStruct((s, 2 * D_FF), MXU_DTYPE),
        compiler_params=_params(("parallel",), 14 * _nbytes((t, D_FF), F32)),
    )(hf, hf, dact)


def _ple_fwd(x, lg, pe, *, name):
    s, d = x.shape
    t = _tile(s, 512)

    def body(x_ref, lg_ref, pe_ref, o_ref):
        o_ref[...] = x_ref[...] + _sigmoid(lg_ref[...]) * pe_ref[...]

    return pl.pallas_call(
        body, name=name, grid=(s // t,),
        in_specs=[_rows(t, d)] * 3, out_specs=_rows(t, d), out_shape=jax.ShapeDtypeStruct((s, d), F32),
        compiler_params=_params(("parallel",), 12 * _nbytes((t, d), F32)),
    )(x, lg, pe)


def _ple_bwd(dx, lg, pe, *, name):
    s, d = dx.shape
    t = _tile(s, 512)

    def body(dx_ref, lg_ref, pe_ref, dpe_ref, dlg_ref):
        dxv = dx_ref[...]
        sg = _sigmoid(lg_ref[...])
        dpe_ref[...] = (dxv * sg).astype(dpe_ref.dtype)
        dlg_ref[...] = (dxv * pe_ref[...] * sg * (1.0 - sg)).astype(dlg_ref.dtype)

    return pl.pallas_call(
        body, name=name, grid=(s // t,),
        in_specs=[_rows(t, d)] * 3, out_specs=[_rows(t, d)] * 2,
        out_shape=[jax.ShapeDtypeStruct((s, d), MXU_DTYPE)] * 2,
        compiler_params=_params(("parallel",), 14 * _nbytes((t, d), F32)),
    )(dx, lg, pe)


def _adamw(parts, w, m, v, *, name):
    rows, lanes = w.shape
    t = math.gcd(rows, 160)
    assert rows % t == 0 and t % V7X_SUBLANES == 0
    c1 = 1.0 / (1.0 - ADAM_B1 ** ADAM_STEP)
    c2 = 1.0 / (1.0 - ADAM_B2 ** ADAM_STEP)

    def body(p_ref, w_ref, m_ref, v_ref, g_ref, d_ref, nm_ref, nv_ref):
        g = p_ref[0].astype(F32)
        for j in range(1, N_DEV):
            g = g + p_ref[j].astype(F32)
        m2 = ADAM_B1 * m_ref[...] + (1.0 - ADAM_B1) * g
        v2 = ADAM_B2 * v_ref[...] + (1.0 - ADAM_B2) * (g * g)
        g_ref[...] = g
        nm_ref[...] = m2
        nv_ref[...] = v2
        d_ref[...] = -ADAM_LR * ((m2 * c1) / (jnp.sqrt(v2 * c2) + ADAM_EPS) + ADAM_WD * w_ref[...])

    blk = _rows(t, lanes)
    return pl.pallas_call(
        body, name=name, grid=(rows // t,),
        in_specs=[pl.BlockSpec((N_DEV, t, lanes), lambda i: (0, i, 0)), blk, blk, blk], out_specs=[blk] * 4,
        out_shape=[jax.ShapeDtypeStruct((rows, lanes), F32)] * 4,
        compiler_params=_params(("parallel",), 40 * _nbytes((t, lanes), F32)),
    )(parts, w, m, v)


def _mesh_pos():
    return lax.axis_index("x"), lax.axis_index("y"), lax.axis_index("c")


def _all_gather(blk, *, name):
    r, c_dim = blk.shape

    def body(x_ref, out_ref, send_sems, recv_sems, local_sem):
        x, y, c = _mesh_pos()
        me, sibling = (x, y, c), (x, y, 1 - c)
        chips = [(1 - x, y), (x, 1 - y), (1 - x, 1 - y)]

        def slot(px, py, pc):
            return out_ref.at[4 * px + 2 * py + pc]

        def copy(k, block, to, src=None):
            return pltpu.make_async_remote_copy(
                src_ref=slot(*block) if src is None else src, dst_ref=slot(*block),
                send_sem=send_sems.at[k], recv_sem=recv_sems.at[k],
                device_id=to, device_id_type=pl.DeviceIdType.MESH)

        mine = pltpu.make_async_copy(x_ref, slot(*me), local_sem)
        mine.start()
        first = [copy(0, me, sibling, src=x_ref)]
        first += [copy(1 + j, me, (*chip, c), src=x_ref) for j, chip in enumerate(chips)]
        for cp in first:
            cp.start()
        passed = [copy(4 + j, (*chip, c), sibling) for j, chip in enumerate(chips)]
        for j, chip in enumerate(chips):
            copy(1 + j, (*chip, c), me).wait_recv()
            passed[j].start()
        copy(0, sibling, me).wait_recv()
        for j, chip in enumerate(chips):
            copy(4 + j, (*chip, 1 - c), me).wait_recv()
        for cp in first + passed:
            cp.wait_send()
        mine.wait()

    return pl.pallas_call(
        body, name=name,
        out_shape=jax.ShapeDtypeStruct((N_DEV, r, c_dim), blk.dtype),
        in_specs=[pl.BlockSpec(memory_space=pl.ANY)], out_specs=pl.BlockSpec(memory_space=pl.ANY),
        scratch_shapes=[pltpu.SemaphoreType.DMA((7,)), pltpu.SemaphoreType.DMA((7,)), pltpu.SemaphoreType.DMA],
    )(blk)


def _all_to_all(pay, *, name):
    _, r, c_dim = pay.shape

    def body(in_ref, out_ref, send_sems, recv_sems, local_sem):
        x, y, c = _mesh_pos()
        me = 4 * x + 2 * y + c
        local = pltpu.make_async_copy(in_ref.at[me], out_ref.at[me], local_sem)
        local.start()
        copies = []
        for k in range(1, N_DEV):
            px = 1 - x if k & 4 else x
            py = 1 - y if k & 2 else y
            pc = 1 - c if k & 1 else c
            copies.append(pltpu.make_async_remote_copy(
                src_ref=in_ref.at[4 * px + 2 * py + pc], dst_ref=out_ref.at[me],
                send_sem=send_sems.at[k - 1], recv_sem=recv_sems.at[k - 1],
                device_id=(px, py, pc), device_id_type=pl.DeviceIdType.MESH))
        for cp in copies:
            cp.start()
        for cp in copies:
            cp.wait()
        local.wait()

    return pl.pallas_call(
        body, name=name,
        out_shape=jax.ShapeDtypeStruct((N_DEV, r, c_dim), pay.dtype),
        in_specs=[pl.BlockSpec(memory_space=pl.ANY)], out_specs=pl.BlockSpec(memory_space=pl.ANY),
        scratch_shapes=[pltpu.SemaphoreType.DMA((7,)), pltpu.SemaphoreType.DMA((7,)), pltpu.SemaphoreType.DMA],
    )(pay)


def _all_gather_many(blocks, *, name):
    n = len(blocks)

    def body(*refs):
        x_refs, out_refs = refs[:n], refs[n:2 * n]
        send_sems, recv_sems, local_sems = refs[2 * n:]
        x, y, c = _mesh_pos()
        me, sibling = (x, y, c), (x, y, 1 - c)
        chips = [(1 - x, y), (x, 1 - y), (1 - x, 1 - y)]

        def slot(a, px, py, pc):
            return out_refs[a].at[4 * px + 2 * py + pc]

        def copy(k, a, block, to, src=None):
            return pltpu.make_async_remote_copy(
                src_ref=slot(a, *block) if src is None else src, dst_ref=slot(a, *block),
                send_sem=send_sems.at[k, a], recv_sem=recv_sems.at[k, a],
                device_id=to, device_id_type=pl.DeviceIdType.MESH)

        mine = [pltpu.make_async_copy(x_refs[a], slot(a, *me), local_sems.at[a]) for a in range(n)]
        for cp in mine:
            cp.start()
        first = [copy(0, a, me, sibling, src=x_refs[a]) for a in range(n)]
        first += [copy(1 + j, a, me, (*chip, c), src=x_refs[a]) for j, chip in enumerate(chips) for a in range(n)]
        for cp in first:
            cp.start()
        passed = []
        for j, chip in enumerate(chips):
            for a in range(n):
                copy(1 + j, a, (*chip, c), me).wait_recv()
                fwd = copy(4 + j, a, (*chip, c), sibling)
                fwd.start()
                passed.append(fwd)
        for a in range(n):
            copy(0, a, sibling, me).wait_recv()
        for j, chip in enumerate(chips):
            for a in range(n):
                copy(4 + j, a, (*chip, 1 - c), me).wait_recv()
        for cp in first + passed:
            cp.wait_send()
        for cp in mine:
            cp.wait()

    any_spec = pl.BlockSpec(memory_space=pl.ANY)
    return pl.pallas_call(
        body, name=name,
        out_shape=[jax.ShapeDtypeStruct((N_DEV,) + b.shape, b.dtype) for b in blocks],
        in_specs=[any_spec] * n, out_specs=[any_spec] * n,
        scratch_shapes=[pltpu.SemaphoreType.DMA((7, n)), pltpu.SemaphoreType.DMA((7, n)),
                        pltpu.SemaphoreType.DMA((n,))],
    )(*blocks)


def _all_to_all_many(pays, *, name):
    n = len(pays)

    def body(*refs):
        in_refs, out_refs = refs[:n], refs[n:2 * n]
        send_sems, recv_sems, local_sems = refs[2 * n:]
        x, y, c = _mesh_pos()
        me = 4 * x + 2 * y + c
        local = [pltpu.make_async_copy(in_refs[a].at[me], out_refs[a].at[me], local_sems.at[a]) for a in range(n)]
        for cp in local:
            cp.start()
        copies = []
        for k in range(1, N_DEV):
            px = 1 - x if k & 4 else x
            py = 1 - y if k & 2 else y
            pc = 1 - c if k & 1 else c
            for a in range(n):
                copies.append(pltpu.make_async_remote_copy(
                    src_ref=in_refs[a].at[4 * px + 2 * py + pc], dst_ref=out_refs[a].at[me],
                    send_sem=send_sems.at[k - 1, a], recv_sem=recv_sems.at[k - 1, a],
                    device_id=(px, py, pc), device_id_type=pl.DeviceIdType.MESH))
        for cp in copies:
            cp.start()
        for cp in copies:
            cp.wait()
        for cp in local:
            cp.wait()

    any_spec = pl.BlockSpec(memory_space=pl.ANY)
    return pl.pallas_call(
        body, name=name,
        out_shape=[jax.ShapeDtypeStruct(p.shape, p.dtype) for p in pays],
        in_specs=[any_spec] * n, out_specs=[any_spec] * n,
        scratch_shapes=[pltpu.SemaphoreType.DMA((7, n)), pltpu.SemaphoreType.DMA((7, n)),
                        pltpu.SemaphoreType.DMA((n,))],
    )(*pays)


def _adamw_nd(parts, w, m, v, *, name):
    d0, rows, cols = w.shape
    t = rows
    for cand in range(V7X_SUBLANES, min(rows, 256) + 1, V7X_SUBLANES):
        if rows % cand == 0:
            t = cand
    c1 = 1.0 / (1.0 - ADAM_B1 ** ADAM_STEP)
    c2 = 1.0 / (1.0 - ADAM_B2 ** ADAM_STEP)

    def body(p_ref, w_ref, m_ref, v_ref, g_ref, d_ref, nm_ref, nv_ref):
        g = p_ref[0, 0].astype(F32)
        for j in range(1, N_DEV):
            g = g + p_ref[j, 0].astype(F32)
        m2 = ADAM_B1 * m_ref[0] + (1.0 - ADAM_B1) * g
        v2 = ADAM_B2 * v_ref[0] + (1.0 - ADAM_B2) * (g * g)
        g_ref[0] = g
        nm_ref[0] = m2
        nv_ref[0] = v2
        d_ref[0] = -ADAM_LR * ((m2 * c1) / (jnp.sqrt(v2 * c2) + ADAM_EPS) + ADAM_WD * w_ref[0])

    blk = pl.BlockSpec((1, t, cols), lambda l, i: (l, i, 0))
    lanes = -(-cols // V7X_LANES) * V7X_LANES
    return pl.pallas_call(
        body, name=name, grid=(d0, rows // t),
        in_specs=[pl.BlockSpec((N_DEV, 1, t, cols), lambda l, i: (0, l, i, 0)), blk, blk, blk], out_specs=[blk] * 4,
        out_shape=[jax.ShapeDtypeStruct(w.shape, F32)] * 4,
        compiler_params=_params(("parallel", "parallel"), 40 * _nbytes((max(t, 16), lanes), F32)),
    )(parts, w, m, v)


def _flat_rows(parts, row_multiple):
    flat = jnp.concatenate([p.reshape(-1) for p in parts])
    chunk = PAYLOAD_LANES * row_multiple
    total = -(-flat.shape[0] // chunk) * chunk
    return jnp.pad(flat, (0, total - flat.shape[0])).reshape(total // PAYLOAD_LANES, PAYLOAD_LANES)


def _split_flat(flat, shapes):
    out, off = [], 0
    flat = flat.reshape(-1)
    for shp in shapes:
        n = math.prod(shp)
        out.append(flat[off:off + n].reshape(shp))
        off += n
    return out


def _pad_w_in(w):
    pieces, cursor = [], 0
    for _, off, width, pad_off, _ in SEGS:
        if pad_off > cursor:
            pieces.append(jnp.zeros(w.shape[:-1] + (pad_off - cursor,), w.dtype))
        pieces.append(w[..., off:off + width])
        cursor = pad_off + width
    pieces.append(jnp.zeros(w.shape[:-1] + (D_IN_PAD - cursor,), w.dtype))
    return jnp.concatenate(pieces, axis=-1)


def _unpad_w_in(w):
    return jnp.concatenate([w[..., pad_off:pad_off + width] for _, _, width, pad_off, _ in SEGS], axis=-1)


def _heads(a, hd):
    return a.reshape(a.shape[0], HEADS, hd).transpose(1, 0, 2)


def _unheads(a):
    return a.transpose(1, 0, 2).reshape(a.shape[1], -1)


def _block_diag(w):
    eye = jnp.eye(LRU_HEADS, dtype=w.dtype)
    return (eye[:, None, :, None] * w[:, :, None, :]).reshape(LRU_WIDTH, LRU_WIDTH)


def _diag_blocks(w):
    w4 = w.reshape(LRU_HEADS, LRU_HEAD_DIM, LRU_HEADS, LRU_HEAD_DIM)
    return jnp.stack([w4[h, :, h, :] for h in range(LRU_HEADS)])


def _lane_pad(a, width):
    return jnp.pad(a, ((0, 0), (0, width - a.shape[-1])))


def _layer_fwd(x, p_i, wts, tabs, tag):
    n = functools.partial(lambda base, t=tag: f"{base}_{t}")
    sv = {"x": x}
    n1 = _rms_fwd(x, wts["mix_norm"], width=D_MODEL, name=n("mix_norm_fwd"))
    z, z16 = _mm(n1, wts["w_in"], also_mxu=True, name=n("w_in_fwd"))
    sv.update(n1=n1, z=z, z16=z16)
    lanes = V7X_LANES

    ya_pre, hseq = _lru_fwd(z, wts["conv_w"], wts["conv_b"], wts["lru_wa"], wts["lru_ba"], wts["lru_wx"],
                            wts["lru_bx"], wts["lru_lambda"], name=n("lru_fwd"))
    ya = _mm(ya_pre, wts["w_br_a"], name=n("br_a_fwd"))
    sv.update(ya_pre=ya_pre, hseq=hseq, ya=ya)

    cqn = _rms_fwd(z, wts["mla_q_norm"], width=MLA_Q_LORA, cb=SEG["cq"][3] // MLA_Q_LORA, name=n("q_norm_fwd"))
    ckvn = _rms_fwd(z, wts["mla_kv_norm"], width=MLA_KV_LORA, cb=SEG["ckv"][3] // MLA_KV_LORA,
                    name=n("kv_norm_fwd"))
    qp, qp16 = _mm(cqn, wts["mla_wuq"], also_mxu=True, name=n("wuq_fwd"))
    kv = _mm(ckvn, wts["mla_wukv"], out_dtype=MXU_DTYPE, name=n("wukv_fwd"))
    q_rot = _rope_lanes(qp, tabs["cos128"], tabs["sin128"], cb=0, groups=HEADS, out_dtype=MXU_DTYPE,
                        name=n("q_rope_fwd"))
    k_rot = _rope_lanes(z, tabs["cos128"], tabs["sin128"], cb=SEG["kr"][3] // lanes, groups=1, out_dtype=MXU_DTYPE,
                        name=n("k_rope_fwd"))
    mla_ops = (qp16, HEADS, kv, 0, kv, HEAD_PAIRS)
    ob_flat, lse_b = _pair_fwd(*mla_ops, (q_rot, k_rot), None, scale=(MLA_NOPE + MLA_ROPE) ** -0.5, unit=CHUNK,
                               name=n("mla_attn_fwd"))
    yb = _mm(ob_flat, wts["w_br_b"], name=n("br_b_fwd"))
    sv.update(cqn=cqn, ckvn=ckvn, mla_ops=mla_ops, mla_rot=(q_rot, k_rot), lse_b=lse_b, ob_flat=ob_flat, yb=yb)

    cum = _fox_cum(z, wts["fox_bf"], name=n("fox_cum_fwd"))
    cum_h = cum[:, :HEADS].T
    fox_decay = (cum_h[:, :, None], cum_h[:, None, :])
    fox_ops = (z16, SEG["fq"][3] // lanes, z16, SEG["fk"][3] // lanes, z16, SEG["fv"][3] // lanes)
    oc_flat, lse_c = _pair_fwd(*fox_ops, None, fox_decay, scale=FOX_HEAD_DIM ** -0.5, unit=1, name=n("fox_attn_fwd"))
    yc = _mm(oc_flat, wts["w_br_c"], name=n("br_c_fwd"))
    sv.update(fox_ops=fox_ops, fox_decay=fox_decay, lse_c=lse_c, oc_flat=oc_flat, yc=yc)

    merged = _merge_fwd(ya, yb, yc, z, wts["gate_b"], name=n("merge_fwd"))
    x1 = _mm(merged, wts["w_o"], res=x, name=n("w_o_fwd"))
    n2 = _rms_fwd(x1, wts["ffn_norm"], width=D_MODEL, name=n("ffn_norm_fwd"))
    hf = _mm(n2, wts["w_gate_up"], name=n("gate_up_fwd"))
    act = _swiglu_fwd(hf, name=n("swiglu_fwd"))
    x2 = _mm(act, wts["w_down"], res=x1, name=n("down_fwd"))
    n3 = _rms_fwd(x2, wts["ple_norm"], width=D_MODEL, name=n("ple_norm_fwd"))
    lg = _mm(n3, wts["w_ple_gate"], name=n("ple_gate_fwd"))
    pe = _mm(p_i, wts["w_ple"], name=n("ple_fwd_mm"))
    x3 = _ple_fwd(x2, lg, pe, name=n("ple_fwd"))
    sv.update(merged=merged, x1=x1, n2=n2, hf=hf, act=act, x2=x2, n3=n3, lg=lg, pe=pe, p_i=p_i)
    return x3, sv


def _layer_bwd(dx3, sv, wts, tabs, tag):
    n = functools.partial(lambda base, t=tag: f"{base}_{t}")
    gr = {}
    z = sv["z"]
    s = z.shape[0]

    dpe, dlg = _ple_bwd(dx3, sv["lg"], sv["pe"], name=n("ple_bwd"))
    gr["w_ple"] = _mm(sv["p_i"], dpe, ta=True, name=n("ple_dw"))
    gr["w_ple_gate"] = _mm(sv["n3"], dlg, ta=True, name=n("ple_gate_dw"))
    dn3 = _mm(dlg, wts["w_ple_gate"], tb=True, name=n("ple_gate_dx"))
    dx2, gr["ple_norm"] = _rms_bwd(sv["x2"], wts["ple_norm"], dn3, width=D_MODEL, res=dx3, name=n("ple_norm_bwd"))

    dact = _mm(dx2, wts["w_down"], tb=True, name=n("down_dx"))
    gr["w_down"] = _mm(sv["act"], dx2, ta=True, name=n("down_dw"))
    dhf = _swiglu_bwd(sv["hf"], dact, name=n("swiglu_bwd"))
    gr["w_gate_up"] = _mm(sv["n2"], dhf, ta=True, name=n("gate_up_dw"))
    dn2 = _mm(dhf, wts["w_gate_up"], tb=True, name=n("gate_up_dx"))
    dx1, gr["ffn_norm"] = _rms_bwd(sv["x1"], wts["ffn_norm"], dn2, width=D_MODEL, res=dx2, name=n("ffn_norm_bwd"))

    dmerged = _mm(dx1, wts["w_o"], tb=True, name=n("w_o_dx"))
    gr["w_o"] = _mm(sv["merged"], dx1, ta=True, name=n("w_o_dw"))
    dya, dyb, dyc, dgl, gr["gate_b"] = _merge_bwd(dmerged, sv["ya"], sv["yb"], sv["yc"], z, wts["gate_b"],
                                                  name=n("merge_bwd"))
    gr["w_br_a"] = _mm(sv["ya_pre"], dya, ta=True, name=n("br_a_dw"))
    gr["w_br_b"] = _mm(sv["ob_flat"], dyb, ta=True, name=n("br_b_dw"))
    gr["w_br_c"] = _mm(sv["oc_flat"], dyc, ta=True, name=n("br_c_dw"))
    dya_pre = _mm(dya, wts["w_br_a"], tb=True, name=n("br_a_dx"))
    dob = _mm(dyb, wts["w_br_b"], tb=True, out_dtype=MXU_DTYPE, name=n("br_b_dx"))
    doc = _mm(dyc, wts["w_br_c"], tb=True, out_dtype=MXU_DTYPE, name=n("br_c_dx"))

    (dz_a, gr["conv_w"], gr["conv_b"], dwa, gr["lru_ba"], dwx, gr["lru_bx"], gr["lru_lambda"]) = _lru_bwd(
        z, sv["hseq"], dya_pre, wts["conv_w"], wts["conv_b"], wts["lru_wa"], wts["lru_ba"], wts["lru_wx"],
        wts["lru_bx"], wts["lru_lambda"], name=n("lru_bwd"))
    gr["lru_wa"], gr["lru_wx"] = _diag_blocks(dwa), _diag_blocks(dwx)

    scale_b = (MLA_NOPE + MLA_ROPE) ** -0.5
    dq_nope, delta_b, dq_rot = _pair_bwd_q(*sv["mla_ops"], dob, sv["lse_b"], sv["mla_rot"], None,
                                           scale=scale_b, unit=CHUNK, name=n("mla_attn_dq"))
    dk_nope, dv_mla, dk_rot = _pair_bwd_kv(*sv["mla_ops"], dob, sv["lse_b"].reshape(HEADS, 1, s),
                                           delta_b.reshape(HEADS, 1, s), sv["mla_rot"], None,
                                           scale=scale_b, unit=CHUNK, name=n("mla_attn_dkv"))
    dq_rope = _rope_lanes(dq_rot, tabs["cos128"], -tabs["sin128"], cb=0, groups=HEADS, out_dtype=MXU_DTYPE,
                          name=n("q_rope_bwd"))
    dk_rope = _rope_lanes(dk_rot, tabs["cos128"], -tabs["sin128"], cb=0, groups=1, out_dtype=MXU_DTYPE,
                          sum_parts=HEAD_PAIRS, name=n("k_rope_bwd"))
    dqp = jnp.concatenate([dq_rope, dq_nope], axis=-1)
    dkv = jnp.concatenate([dk_nope, dv_mla], axis=-1)
    gr["mla_wuq"] = _mm(sv["cqn"], dqp, ta=True, name=n("wuq_dw"))
    gr["mla_wukv"] = _mm(sv["ckvn"], dkv, ta=True, name=n("wukv_dw"))
    dcqn = _mm(dqp, wts["mla_wuq"], tb=True, name=n("wuq_dx"))
    dckvn = _mm(dkv, wts["mla_wukv"], tb=True, name=n("wukv_dx"))
    dcq, gr["mla_q_norm"] = _rms_bwd(z, wts["mla_q_norm"], dcqn, width=MLA_Q_LORA, cb=SEG["cq"][3] // MLA_Q_LORA,
                                     out_dtype=MXU_DTYPE, name=n("q_norm_bwd"))
    dckv, gr["mla_kv_norm"] = _rms_bwd(z, wts["mla_kv_norm"], dckvn, width=MLA_KV_LORA,
                                       cb=SEG["ckv"][3] // MLA_KV_LORA, out_dtype=MXU_DTYPE, name=n("kv_norm_bwd"))

    scale_c = FOX_HEAD_DIM ** -0.5
    dfq, delta_c = _pair_bwd_q(*sv["fox_ops"], doc, sv["lse_c"], None, sv["fox_decay"],
                               scale=scale_c, unit=1, name=n("fox_attn_dq"))
    dfk, dfv, dcum = _pair_bwd_kv(*sv["fox_ops"], doc, sv["lse_c"].reshape(HEADS, 1, s),
                                  delta_c.reshape(HEADS, 1, s), None, sv["fox_decay"],
                                  scale=scale_c, unit=1, name=n("fox_attn_dkv"))
    dcum_rows = _lane_pad(dcum.reshape(HEADS, s).T, V7X_LANES)
    dfl, dbf = _fox_cum_bwd(z, wts["fox_bf"], dcum_rows, name=n("fox_cum_bwd"))
    gr["fox_bf"] = dbf[:, :HEADS]

    zero = lambda width: jnp.zeros((s, width), MXU_DTYPE)
    dz = jnp.concatenate([dz_a, zero(128), dcq, dckv, dk_rope, zero(128), dfq, dfk, dfv, dfl, zero(384), dgl],
                         axis=-1)
    gr["w_in"] = _mm(sv["n1"], dz, ta=True, name=n("w_in_dw"))
    dn1 = _mm(dz, wts["w_in"], tb=True, name=n("w_in_dx"))
    dx, gr["mix_norm"] = _rms_bwd(sv["x"], wts["mix_norm"], dn1, width=D_MODEL, res=dx1, name=n("mix_norm_bwd"))
    return dx, gr


def _rope_tables(s):
    pos = jnp.arange(s, dtype=F32)
    inv_freq = ROPE_BASE ** (-jnp.arange(0, MLA_ROPE, 2, dtype=F32) / MLA_ROPE)
    ang = pos[:, None] * inv_freq[None, :]
    cos, sin = jnp.cos(ang), jnp.sin(ang)
    cos32 = jnp.concatenate([cos, cos], axis=-1)
    sin32 = jnp.concatenate([-sin, sin], axis=-1)
    return {"cos256": jnp.tile(cos32, (1, 8)), "sin256": jnp.tile(sin32, (1, 8)),
            "cos128": jnp.tile(cos32, (1, 4)), "sin128": jnp.tile(sin32, (1, 4))}


def _gather_weights(shards):
    names = [nm for nm, _ in SHARDED]
    got = _all_gather_many([shards[nm] if nm == "conv_w" else shards[nm].astype(MXU_DTYPE) for nm in names],
                           name="weights_all_gather")
    full = {}
    for (nm, axis), blk in zip(SHARDED, got):
        shp = shards[nm].shape
        if axis == 2:
            full[nm] = blk.transpose(1, 2, 0, 3).reshape(shp[0], shp[1], N_DEV * shp[2])
        else:
            full[nm] = blk.transpose(1, 0, 2, 3).reshape(shp[0], N_DEV * shp[1], shp[2])
    return full


def _to_dest_major(g, axis):
    d0, r, c = g.shape
    if axis == 2:
        return g.reshape(d0, r, N_DEV, c // N_DEV).transpose(2, 0, 1, 3)
    return g.reshape(d0, N_DEV, r // N_DEV, c).transpose(1, 0, 2, 3)


def kernel(x, p, mix_norm, w_in, gate_b, conv_w, conv_b, lru_wa, lru_ba, lru_wx, lru_bx, lru_lambda, mla_q_norm, mla_wuq, mla_kv_norm, mla_wukv, fox_bf, w_br_a, w_br_b, w_br_c, w_o, ffn_norm, w_gate_up, w_down, ple_norm, w_ple_gate, w_ple, final_norm, loss_target, m_mix_norm, m_w_in, m_gate_b, m_conv_w, m_conv_b, m_lru_wa, m_lru_ba, m_lru_wx, m_lru_bx, m_lru_lambda, m_mla_q_norm, m_mla_wuq, m_mla_kv_norm, m_mla_wukv, m_fox_bf, m_w_br_a, m_w_br_b, m_w_br_c, m_w_o, m_ffn_norm, m_w_gate_up, m_w_down, m_ple_norm, m_w_ple_gate, m_w_ple, m_final_norm, v_mix_norm, v_w_in, v_gate_b, v_conv_w, v_conv_b, v_lru_wa, v_lru_ba, v_lru_wx, v_lru_bx, v_lru_lambda, v_mla_q_norm, v_mla_wuq, v_mla_kv_norm, v_mla_wukv, v_fox_bf, v_w_br_a, v_w_br_b, v_w_br_c, v_w_o, v_ffn_norm, v_w_gate_up, v_w_down, v_ple_norm, v_w_ple_gate, v_w_ple, v_final_norm):
    given = dict(locals())
    w_loc = {nm: given[nm] for nm in WEIGHTS}
    m_loc = {nm: given["m_" + nm] for nm in WEIGHTS}
    v_loc = {nm: given["v_" + nm] for nm in WEIGHTS}
    xs = x[0]
    s = xs.shape[0]
    tabs = _rope_tables(s)

    full = _gather_weights({nm: w_loc[nm] for nm, _ in SHARDED})
    full["w_in"] = _pad_w_in(full["w_in"])
    wq = full["mla_wuq"].reshape(DEPTH, MLA_Q_LORA, HEADS, MLA_NOPE + MLA_ROPE)
    wq_rot = jnp.pad(wq[..., MLA_NOPE:], ((0, 0), (0, 0), (0, 0), (0, V7X_LANES - MLA_ROPE)))
    full["mla_wuq"] = jnp.concatenate([wq_rot.reshape(DEPTH, MLA_Q_LORA, -1),
                                       wq[..., :MLA_NOPE].reshape(DEPTH, MLA_Q_LORA, -1)], axis=-1)
    wkv = full["mla_wukv"].reshape(DEPTH, MLA_KV_LORA, HEADS, MLA_NOPE + MLA_V)
    full["mla_wukv"] = jnp.concatenate([wkv[..., :MLA_NOPE].reshape(DEPTH, MLA_KV_LORA, -1),
                                        wkv[..., MLA_NOPE:].reshape(DEPTH, MLA_KV_LORA, -1)], axis=-1)

    def layer_weights(i):
        wts = {nm: full[nm][i] for nm, _ in SHARDED}
        for nm in ("mix_norm", "gate_b", "conv_b", "lru_ba", "lru_bx", "lru_lambda", "mla_q_norm", "mla_kv_norm",
                   "ffn_norm", "ple_norm"):
            wts[nm] = w_loc[nm][i][None, :]
        wts["fox_bf"] = _lane_pad(w_loc["fox_bf"][i][None, :], V7X_LANES)
        wts["lru_wa"] = _block_diag(w_loc["lru_wa"][i]).astype(MXU_DTYPE)
        wts["lru_wx"] = _block_diag(w_loc["lru_wx"][i]).astype(MXU_DTYPE)
        return wts

    layers = [layer_weights(i) for i in range(DEPTH)]

    h = xs
    saved = []
    for i in range(DEPTH):
        h, sv = _layer_fwd(h, p[i, 0].astype(MXU_DTYPE), layers[i], tabs, f"l{i}")
        saved.append(sv)
    loss_blk, dh, dg_final = _final_loss(h, w_loc["final_norm"][None, :], loss_target[0], name="final_loss")
    loss = lax.psum(loss_blk[0, 0], ("x", "y", "c"))

    grads = [None] * DEPTH
    for i in reversed(range(DEPTH)):
        dh, grads[i] = _layer_bwd(dh, saved[i], layers[i], tabs, f"l{i}")
    grad_x = dh[None]

    def stacked(nm):
        return jnp.stack([grads[i][nm] for i in range(DEPTH)])

    gfull = {}
    for nm, _ in SHARDED:
        gfull[nm] = stacked(nm)
    gfull["w_in"] = _unpad_w_in(gfull["w_in"])
    gq = gfull["mla_wuq"]
    rot_w = HEADS * V7X_LANES
    gfull["mla_wuq"] = jnp.concatenate(
        [gq[..., rot_w:].reshape(DEPTH, MLA_Q_LORA, HEADS, MLA_NOPE),
         gq[..., :rot_w].reshape(DEPTH, MLA_Q_LORA, HEADS, V7X_LANES)[..., :MLA_ROPE]],
        axis=-1).reshape(DEPTH, MLA_Q_LORA, -1)
    gkv = gfull["mla_wukv"]
    gfull["mla_wukv"] = jnp.concatenate(
        [gkv[..., :512].reshape(DEPTH, MLA_KV_LORA, HEADS, MLA_NOPE),
         gkv[..., 512:].reshape(DEPTH, MLA_KV_LORA, HEADS, MLA_V)], axis=-1).reshape(DEPTH, MLA_KV_LORA, -1)

    parts = _all_to_all_many([_to_dest_major(gfull[nm], ax).astype(MXU_DTYPE) for nm, ax in SHARDED],
                             name="grads_all_to_all")
    res_s = [{}, {}, {}, {}]
    for (nm, _), part in zip(SHARDED, parts):
        outs = _adamw_nd(part, w_loc[nm], m_loc[nm], v_loc[nm], name=f"adamw_{nm}")
        for kind in range(4):
            res_s[kind][nm] = outs[kind]

    small = {nm: stacked(nm) for nm in REPLICATED if nm != "final_norm"}
    small["final_norm"] = dg_final
    names_r = list(REPLICATED)
    shapes_r = [w_loc[nm].shape for nm in names_r]
    parts_r = _all_gather(_flat_rows([small[nm] for nm in names_r], 8), name="small_grads_all_gather")
    outs_r = _adamw(parts_r, _flat_rows([w_loc[nm] for nm in names_r], 8),
                    _flat_rows([m_loc[nm] for nm in names_r], 8),
                    _flat_rows([v_loc[nm] for nm in names_r], 8), name="adamw_replicated")
    res_r = [dict(zip(names_r, _split_flat(o, shapes_r))) for o in outs_r]

    out = [loss, grad_x]
    for kind in range(4):
        for nm in WEIGHTS:
            out.append(res_s[kind][nm] if nm in res_s[kind] else res_r[kind][nm])
    return tuple(out)
```

```python
import functools
import math

import jax
import jax.numpy as jnp
from jax import lax
from jax.experimental import pallas as pl
from jax.experimental.pallas import tpu as pltpu

F32 = jnp.float32
BF16 = jnp.bfloat16
MXU_DTYPE = jnp.bfloat16

D_MODEL = 1024
DEPTH = 2
CHUNK = 64
EPS = 1e-6
NEG_INF = -1e30
LRU_WIDTH = 512
LRU_HEADS = 8
LRU_HEAD_DIM = 64
CONV_WIDTH = 4
LRU_C = 8.0
HEADS = 8
MLA_Q_LORA = 384
MLA_KV_LORA = 256
MLA_NOPE = 64
MLA_ROPE = 32
MLA_V = 64
ROPE_BASE = 10000.0
FOX_HEAD_DIM = 64
FOX_WIDTH = 512
D_FF = 2816
PLE_DIM = 256
D_IN = 6312
ADAM_LR = 0.001
ADAM_B1 = 0.9
ADAM_B2 = 0.999
ADAM_EPS = 1e-08
ADAM_WD = 0.01
ADAM_STEP = 10

V7X_VMEM_BYTES = 64 * 1024 * 1024
V7X_LANES = 128
V7X_SUBLANES = 8
VMEM_LIMIT_CAP = 56 * 1024 * 1024
N_DEV = 8

SEGS = (
    ("u", 0, 512, 0, 512),
    ("ug", 512, 512, 512, 512),
    ("cq", 1024, 384, 1152, 384),
    ("ckv", 1408, 256, 1536, 256),
    ("kr", 1664, 32, 1792, 128),
    ("fq", 1696, 512, 2048, 512),
    ("fk", 2208, 512, 2560, 512),
    ("fv", 2720, 512, 3072, 512),
    ("fl", 3232, 8, 3584, 128),
    ("gate", 3240, 3072, 4096, 1024),
)
D_IN_PAD = 7168
SEG = {s[0]: s for s in SEGS}

SHARDED = (("w_in", 2), ("mla_wuq", 2), ("mla_wukv", 2), ("w_br_a", 2), ("w_br_b", 2), ("w_br_c", 2),
           ("w_o", 1), ("w_gate_up", 2), ("w_down", 1), ("w_ple_gate", 1), ("w_ple", 2), ("conv_w", 2))
REPLICATED = ("mix_norm", "gate_b", "conv_b", "lru_wa", "lru_ba", "lru_wx", "lru_bx", "lru_lambda",
              "mla_q_norm", "mla_kv_norm", "fox_bf", "ffn_norm", "ple_norm", "final_norm")
WEIGHTS = ("mix_norm", "w_in", "gate_b", "conv_w", "conv_b", "lru_wa", "lru_ba", "lru_wx", "lru_bx",
           "lru_lambda", "mla_q_norm", "mla_wuq", "mla_kv_norm", "mla_wukv", "fox_bf", "w_br_a", "w_br_b",
           "w_br_c", "w_o", "ffn_norm", "w_gate_up", "w_down", "ple_norm", "w_ple_gate", "w_ple", "final_norm")
PAYLOAD_LANES = 1024


def _tile(n, cap=1024):
    best = None
    for t in range(V7X_LANES, min(n, cap) + 1, V7X_LANES):
        if n % t == 0:
            best = t
    return best if best is not None else n


def _row_tile(s, pref):
    t = min(pref, s // 2)
    assert s % t == 0 and t % V7X_SUBLANES == 0
    return t


def _nbytes(shape, dtype):
    return math.prod(shape) * jnp.dtype(dtype).itemsize


def _params(sem, vmem_bytes):
    limit = int(min(VMEM_LIMIT_CAP, max(16 * 1024 * 1024, vmem_bytes)))
    return pltpu.CompilerParams(dimension_semantics=sem, vmem_limit_bytes=limit)


def _full(shape):
    return pl.BlockSpec(shape, lambda *_: (0,) * len(shape))


def _rows(t, w, cb=0):
    return pl.BlockSpec((t, w), lambda i: (i, cb))


def _mxu(v):
    return v.astype(MXU_DTYPE)


def _dot(a, b):
    return lax.dot_general(_mxu(a), _mxu(b), (((1,), (0,)), ((), ())), preferred_element_type=F32)


def _dot_nt(a, b):
    return lax.dot_general(_mxu(a), _mxu(b), (((1,), (1,)), ((), ())), preferred_element_type=F32)


def _dot_tn(a, b):
    return lax.dot_general(_mxu(a), _mxu(b), (((0,), (0,)), ((), ())), preferred_element_type=F32)


def _sigmoid(v):
    return 1.0 / (1.0 + jnp.exp(-v))


def _softplus(v):
    return jnp.maximum(v, 0.0) + jnp.log(1.0 + jnp.exp(-jnp.abs(v)))


def _neg_expm1(v):
    series = -v * (1.0 + v * (0.5 + v * (1.0 / 6.0 + v * (1.0 / 24.0))))
    return jnp.where(v > -0.03, series, 1.0 - jnp.exp(v))


_GELU_C = math.sqrt(2.0 / math.pi)
_GELU_A = 0.044715


def _gelu(v):
    t = jnp.tanh(_GELU_C * (v + _GELU_A * v * v * v))
    return 0.5 * v * (1.0 + t)


def _gelu_grad(v):
    t = jnp.tanh(_GELU_C * (v + _GELU_A * v * v * v))
    return 0.5 * (1.0 + t) + 0.5 * v * (1.0 - t * t) * _GELU_C * (1.0 + 3.0 * _GELU_A * v * v)


def _mm(a, b, *, ta=False, tb=False, out_dtype=F32, res=None, also_mxu=False, name):
    k_dim, m_dim = (a.shape[0], a.shape[1]) if ta else (a.shape[1], a.shape[0])
    n_dim = b.shape[0] if tb else b.shape[1]
    assert (b.shape[1] if tb else b.shape[0]) == k_dim
    tm, tn, tk = _tile(m_dim), _tile(n_dim, 1408), _tile(k_dim, 1408)
    nk = k_dim // tk
    a_spec = pl.BlockSpec((tk, tm), lambda i, j, k: (k, i)) if ta else pl.BlockSpec((tm, tk), lambda i, j, k: (i, k))
    b_spec = pl.BlockSpec((tn, tk), lambda i, j, k: (j, k)) if tb else pl.BlockSpec((tk, tn), lambda i, j, k: (k, j))
    o_spec = pl.BlockSpec((tm, tn), lambda i, j, k: (i, j))
    has_res = res is not None

    def body(*refs):
        a_ref, b_ref = refs[0], refs[1]
        res_ref = refs[2] if has_res else None
        o_ref = refs[3] if has_res else refs[2]
        o2_ref = refs[-2] if also_mxu else None
        acc_ref = refs[-1]
        k = pl.program_id(2)
        if ta:
            part = _dot_tn(a_ref[...], b_ref[...])
        elif tb:
            part = _dot_nt(a_ref[...], b_ref[...])
        else:
            part = _dot(a_ref[...], b_ref[...])

        def finish(total):
            if has_res:
                total = total + res_ref[...].astype(F32)
            o_ref[...] = total.astype(o_ref.dtype)
            if also_mxu:
                o2_ref[...] = total.astype(o2_ref.dtype)

        if nk == 1:
            finish(part)
        else:
            @pl.when(k == 0)
            def _():
                acc_ref[...] = part

            @pl.when(jnp.logical_and(k > 0, k < nk - 1))
            def _():
                acc_ref[...] += part

            @pl.when(k == nk - 1)
            def _():
                finish(acc_ref[...] + part)

    ins = [a, b] + ([res] if has_res else [])
    in_specs = [a_spec, b_spec] + ([o_spec] if has_res else [])
    acc_shape = (tm, tn) if nk > 1 else (V7X_SUBLANES, V7X_LANES)
    vmem = (2 * (_nbytes((tm, tk), a.dtype) + _nbytes((tk, tn), b.dtype) + _nbytes((tm, tn), out_dtype)
                 + (_nbytes((tm, tn), res.dtype) if has_res else 0))
            + _nbytes((tm, tk), MXU_DTYPE) + _nbytes((tk, tn), MXU_DTYPE) + 3 * _nbytes((tm, tn), F32))
    return pl.pallas_call(
        body, name=name, grid=(m_dim // tm, n_dim // tn, nk),
        in_specs=in_specs, out_specs=[o_spec, o_spec] if also_mxu else o_spec,
        out_shape=([jax.ShapeDtypeStruct((m_dim, n_dim), out_dtype), jax.ShapeDtypeStruct((m_dim, n_dim), MXU_DTYPE)]
                   if also_mxu else jax.ShapeDtypeStruct((m_dim, n_dim), out_dtype)),
        scratch_shapes=[pltpu.VMEM(acc_shape, F32)],
        compiler_params=_params(("parallel", "parallel", "arbitrary"), vmem),
    )(*ins)


def _rms_fwd(x, g, *, width, cb=0, name):
    s = x.shape[0]
    t = _tile(s, 512)

    def body(x_ref, g_ref, o_ref):
        xv = x_ref[...].astype(F32)
        r = lax.rsqrt(jnp.mean(xv * xv, axis=-1, keepdims=True) + EPS)
        o_ref[...] = (xv * r * g_ref[...]).astype(o_ref.dtype)

    return pl.pallas_call(
        body, name=name, grid=(s // t,),
        in_specs=[_rows(t, width, cb), _full((1, width))], out_specs=_rows(t, width),
        out_shape=jax.ShapeDtypeStruct((s, width), MXU_DTYPE),
        compiler_params=_params(("parallel",), 8 * _nbytes((t, width), F32)),
    )(x, g)


def _rms_bwd(x, g, dn, *, width, cb=0, res=None, out_dtype=F32, name):
    s = x.shape[0]
    t = _tile(s, 256)
    has_res = res is not None

    def body(*refs):
        x_ref, g_ref, dn_ref = refs[:3]
        res_ref = refs[3] if has_res else None
        dx_ref, dg_ref = refs[-2], refs[-1]
        xv = x_ref[...].astype(F32)
        dnv = dn_ref[...].astype(F32)
        r = lax.rsqrt(jnp.mean(xv * xv, axis=-1, keepdims=True) + EPS)
        xr = xv * r
        dng = dnv * g_ref[...]
        dx = r * dng - xr * (r * r) * jnp.mean(dng * xv, axis=-1, keepdims=True)
        if has_res:
            dx = dx + res_ref[...].astype(F32)
        dx_ref[...] = dx.astype(dx_ref.dtype)
        part = jnp.sum(dnv * xr, axis=0, keepdims=True)

        @pl.when(pl.program_id(0) == 0)
        def _():
            dg_ref[...] = part

        @pl.when(pl.program_id(0) > 0)
        def _():
            dg_ref[...] += part

    ins = [x, g, dn] + ([res] if has_res else [])
    in_specs = [_rows(t, width, cb), _full((1, width)), _rows(t, width)] + ([_rows(t, width)] if has_res else [])
    return pl.pallas_call(
        body, name=name, grid=(s // t,),
        in_specs=in_specs, out_specs=[_rows(t, width), _full((1, width))],
        out_shape=[jax.ShapeDtypeStruct((s, width), out_dtype), jax.ShapeDtypeStruct((1, width), F32)],
        compiler_params=_params(("arbitrary",), 16 * _nbytes((t, width), F32)),
    )(*ins)


def _final_loss(x, g, target, *, name):
    s, d = x.shape
    t = _tile(s, 256)

    def body(x_ref, g_ref, t_ref, loss_ref, dx_ref, dg_ref):
        xv = x_ref[...]
        r = lax.rsqrt(jnp.mean(xv * xv, axis=-1, keepdims=True) + EPS)
        xr = xv * r
        err = xr * g_ref[...] - t_ref[...]
        part_loss = 0.5 * jnp.sum(jnp.mean(err * err, axis=-1, keepdims=True), axis=0, keepdims=True)
        dnv = err * (1.0 / d)
        dng = dnv * g_ref[...]
        dx_ref[...] = r * dng - xr * (r * r) * jnp.mean(dng * xv, axis=-1, keepdims=True)
        part_dg = jnp.sum(dnv * xr, axis=0, keepdims=True)

        @pl.when(pl.program_id(0) == 0)
        def _():
            dg_ref[...] = part_dg
            loss_ref[...] = jnp.zeros(loss_ref.shape, F32) + part_loss

        @pl.when(pl.program_id(0) > 0)
        def _():
            dg_ref[...] += part_dg
            loss_ref[...] += part_loss

    return pl.pallas_call(
        body, name=name, grid=(s // t,),
        in_specs=[_rows(t, d), _full((1, d)), _rows(t, d)],
        out_specs=[_full((V7X_SUBLANES, V7X_LANES)), _rows(t, d), _full((1, d))],
        out_shape=[jax.ShapeDtypeStruct((V7X_SUBLANES, V7X_LANES), F32), jax.ShapeDtypeStruct((s, d), F32),
                   jax.ShapeDtypeStruct((1, d), F32)],
        compiler_params=_params(("arbitrary",), 16 * _nbytes((t, d), F32)),
    )(x, g, target)


def _shift_down(v, d, fill, rows):
    return jnp.where(rows >= d, pltpu.roll(v, d, 0), fill)


def _shift_up(v, d, fill, rows, t):
    return jnp.where(rows < t - d, pltpu.roll(v, t - d, 0), fill)


def _lru_gates(xc, wa_ref, ba_ref, wx_ref, bx_ref, lam_ref):
    ra = _sigmoid(_dot(xc, wa_ref[...]) + ba_ref[...])
    ig = _sigmoid(_dot(xc, wx_ref[...]) + bx_ref[...])
    sp = _softplus(-lam_ref[...])
    log_a = -LRU_C * ra * sp
    a = jnp.exp(log_a)
    s2 = _neg_expm1(2.0 * log_a)
    return ra, ig, sp, a, s2


def _conv(ubuf, cw_ref, cb_ref, t):
    big = ubuf[...]
    shifted = [pltpu.roll(big, CONV_WIDTH - 1 - k, 0)[V7X_SUBLANES:t + V7X_SUBLANES] if k < CONV_WIDTH - 1
               else big[V7X_SUBLANES:t + V7X_SUBLANES] for k in range(CONV_WIDTH)]
    xc = cb_ref[...] + shifted[0] * cw_ref[0:1, :]
    for k in range(1, CONV_WIDTH):
        xc = xc + shifted[k] * cw_ref[k:k + 1, :]
    return xc, shifted


def _lru_fwd(z, cw, cb, wa, ba, wx, bx, lam, *, name):
    s = z.shape[0]
    w = LRU_WIDTH
    t = _row_tile(s, 256)
    steps = [1 << k for k in range(int(math.log2(t)))]

    def body(u_ref, ug_ref, cw_ref, cb_ref, wa_ref, ba_ref, wx_ref, bx_ref, lam_ref, y_ref, h_ref, ubuf, hc):
        @pl.when(pl.program_id(0) == 0)
        def _():
            ubuf[0:V7X_SUBLANES, :] = jnp.zeros((V7X_SUBLANES, w), F32)
            hc[...] = jnp.zeros_like(hc)

        ubuf[V7X_SUBLANES:t + V7X_SUBLANES, :] = u_ref[...]
        xc, _ = _conv(ubuf, cw_ref, cb_ref, t)
        _, ig, _, a, s2 = _lru_gates(xc, wa_ref, ba_ref, wx_ref, bx_ref, lam_ref)
        b = jnp.sqrt(s2) * (ig * xc)
        rows = lax.broadcasted_iota(jnp.int32, (t, w), 0)
        for d in steps:
            b = a * _shift_down(b, d, 0.0, rows) + b
            a = a * _shift_down(a, d, 1.0, rows)
        h = a * hc[0:1, :] + b
        h_ref[...] = h
        y_ref[...] = (h * _gelu(ug_ref[...])).astype(y_ref.dtype)
        hc[0:1, :] = h_ref[t - 1:t, :]
        ubuf[0:V7X_SUBLANES, :] = ubuf[t:t + V7X_SUBLANES, :]

    vec = _full((1, w))
    return pl.pallas_call(
        body, name=name, grid=(s // t,),
        in_specs=[_rows(t, w, 0), _rows(t, w, 1), _full((CONV_WIDTH, w)), vec, _full((w, w)), vec, _full((w, w)),
                  vec, vec],
        out_specs=[_rows(t, w), _rows(t, w)],
        out_shape=[jax.ShapeDtypeStruct((s, w), MXU_DTYPE), jax.ShapeDtypeStruct((s, w), F32)],
        scratch_shapes=[pltpu.VMEM((t + V7X_SUBLANES, w), F32), pltpu.VMEM((V7X_SUBLANES, w), F32)],
        compiler_params=_params(("arbitrary",), 40 * _nbytes((t, w), F32)),
    )(z, z, cw, cb, wa, ba, wx, bx, lam)


def _lru_bwd(z, h, dy, cw, cb, wa, ba, wx, bx, lam, *, name):
    s = z.shape[0]
    w = LRU_WIDTH
    t = _row_tile(s, 256)
    nt = s // t
    per8 = t // V7X_SUBLANES
    steps = [1 << k for k in range(int(math.log2(t)))]

    def body(u_ref, ug_ref, h_ref, dy_ref, uprev_ref, hprev_ref, cw_ref, cb_ref, wa_ref, ba_ref, wx_ref, bx_ref,
             lam_ref, dz_ref, dcw_ref, dcb_ref, dwa_ref, dba_ref, dwx_ref, dbx_ref, dlam_ref,
             ubuf, dbuf, acar, dhcar, tmp):
        i = pl.program_id(0)
        first_tile = i == nt - 1

        @pl.when(i == 0)
        def _():
            for r in (dcw_ref, dcb_ref, dwa_ref, dba_ref, dwx_ref, dbx_ref, dlam_ref, acar, dhcar):
                r[...] = jnp.zeros_like(r)
            dbuf[t:t + V7X_SUBLANES, :] = jnp.zeros((V7X_SUBLANES, w), F32)

        keep = jnp.where(first_tile, 0.0, 1.0)
        ubuf[0:V7X_SUBLANES, :] = uprev_ref[...] * keep
        ubuf[V7X_SUBLANES:t + V7X_SUBLANES, :] = u_ref[...]
        xc, shifted = _conv(ubuf, cw_ref, cb_ref, t)
        ra, ig, sp, a, s2 = _lru_gates(xc, wa_ref, ba_ref, wx_ref, bx_ref, lam_ref)
        sq = jnp.sqrt(s2)
        gx = ig * xc
        rows = lax.broadcasted_iota(jnp.int32, (t, w), 0)
        ugv = ug_ref[...]
        dyv = dy_ref[...].astype(F32)
        hv = h_ref[...]

        acc_g = dyv * _gelu(ugv)
        acc_a = _shift_up(a, 1, acar[0:1, :], rows, t)
        for d in steps:
            acc_g = acc_a * _shift_up(acc_g, d, 0.0, rows, t) + acc_g
            acc_a = acc_a * _shift_up(acc_a, d, 1.0, rows, t)
        dh = acc_a * dhcar[0:1, :] + acc_g

        hprev = _shift_down(hv, 1, hprev_ref[V7X_SUBLANES - 1:V7X_SUBLANES, :] * keep, rows)
        d_a = dh * hprev
        d_sq = dh * gx
        d_gx = dh * sq
        d_ig = d_gx * xc
        dxc = d_gx * ig
        d_log_a = d_a * a - d_sq * (1.0 - s2) / sq
        d_ra = d_log_a * (-LRU_C * sp)
        lamv = lam_ref[...]
        dlam_ref[...] += jnp.sum(d_log_a * (-LRU_C * ra), axis=0, keepdims=True) * (-_sigmoid(-lamv))
        dpa = d_ra * ra * (1.0 - ra)
        dpx = d_ig * ig * (1.0 - ig)
        dba_ref[...] += jnp.sum(dpa, axis=0, keepdims=True)
        dbx_ref[...] += jnp.sum(dpx, axis=0, keepdims=True)
        dwa_ref[...] += _dot_tn(xc, dpa)
        dwx_ref[...] += _dot_tn(xc, dpx)
        dxc = dxc + _dot_nt(dpa, wa_ref[...]) + _dot_nt(dpx, wx_ref[...])
        dcb_ref[...] += jnp.sum(dxc, axis=0, keepdims=True)
        for k in range(CONV_WIDTH):
            dcw_ref[k:k + 1, :] += jnp.sum(dxc * shifted[k], axis=0, keepdims=True)

        dbuf[0:t, :] = dxc
        bigd = dbuf[...]
        du = dxc * cw_ref[CONV_WIDTH - 1:CONV_WIDTH, :]
        for k in range(CONV_WIDTH - 1):
            e = CONV_WIDTH - 1 - k
            du = du + pltpu.roll(bigd, t + V7X_SUBLANES - e, 0)[0:t] * cw_ref[k:k + 1, :]
        dz_ref[:, 0:w] = du.astype(dz_ref.dtype)
        dz_ref[:, w:2 * w] = (dyv * hv * _gelu_grad(ugv)).astype(dz_ref.dtype)

        dbuf[t:t + V7X_SUBLANES, :] = dbuf[0:V7X_SUBLANES, :]
        tmp[...] = a
        acar[0:1, :] = tmp[0:1, :]
        tmp[...] = dh
        dhcar[0:1, :] = tmp[0:1, :]

    vec = _full((1, w))
    rev = lambda cbk: pl.BlockSpec((t, w), lambda i: (nt - 1 - i, cbk))
    prev8 = lambda cbk: pl.BlockSpec((V7X_SUBLANES, w),
                                     lambda i: (jnp.maximum((nt - 1 - i) * per8 - 1, 0), cbk))
    return pl.pallas_call(
        body, name=name, grid=(nt,),
        in_specs=[rev(0), rev(1), rev(0), rev(0), prev8(0), prev8(0), _full((CONV_WIDTH, w)), vec, _full((w, w)),
                  vec, _full((w, w)), vec, vec],
        out_specs=[pl.BlockSpec((t, 2 * w), lambda i: (nt - 1 - i, 0)), _full((CONV_WIDTH, w)), vec,
                   _full((w, w)), vec, _full((w, w)), vec, vec],
        out_shape=[jax.ShapeDtypeStruct((s, 2 * w), MXU_DTYPE), jax.ShapeDtypeStruct((CONV_WIDTH, w), F32),
                   jax.ShapeDtypeStruct((1, w), F32), jax.ShapeDtypeStruct((w, w), F32),
                   jax.ShapeDtypeStruct((1, w), F32), jax.ShapeDtypeStruct((w, w), F32),
                   jax.ShapeDtypeStruct((1, w), F32), jax.ShapeDtypeStruct((1, w), F32)],
        scratch_shapes=[pltpu.VMEM((t + V7X_SUBLANES, w), F32), pltpu.VMEM((t + V7X_SUBLANES, w), F32),
                        pltpu.VMEM((V7X_SUBLANES, w), F32), pltpu.VMEM((V7X_SUBLANES, w), F32),
                        pltpu.VMEM((t, w), F32)],
        compiler_params=_params(("arbitrary",), 80 * _nbytes((t, w), F32)),
    )(z, z, h, dy, z, h, cw, cb, wa, ba, wx, bx, lam)


def _rope_apply(v, cos, sin, width):
    half = MLA_ROPE // 2
    lanes = lax.broadcasted_iota(jnp.int32, v.shape, 1)
    first = (lanes % MLA_ROPE) < half
    partner = jnp.where(first, pltpu.roll(v, width - half, 1), pltpu.roll(v, half, 1))
    return v * cos + partner * sin


def _rope(x, cos, sin, *, width, cb, out_dtype, sum_heads=False, name):
    s = x.shape[0]
    t = _tile(s, 512)
    out_w = V7X_LANES if sum_heads else width

    def body(x_ref, c_ref, s_ref, o_ref):
        v = x_ref[...].astype(F32)
        if sum_heads:
            v = v[:, 0:V7X_LANES] + v[:, V7X_LANES:2 * V7X_LANES]
            v = v + pltpu.roll(v, 64, 1)
            v = v + pltpu.roll(v, 32, 1)
            out = _rope_apply(v, c_ref[...], s_ref[...], V7X_LANES)
            lanes = lax.broadcasted_iota(jnp.int32, out.shape, 1)
            out = jnp.where(lanes < MLA_ROPE, out, 0.0)
        else:
            out = _rope_apply(v, c_ref[...], s_ref[...], width)
        o_ref[...] = out.astype(o_ref.dtype)

    return pl.pallas_call(
        body, name=name, grid=(s // t,),
        in_specs=[_rows(t, width, cb), _rows(t, out_w), _rows(t, out_w)], out_specs=_rows(t, out_w),
        out_shape=jax.ShapeDtypeStruct((s, out_w), out_dtype),
        compiler_params=_params(("parallel",), 12 * _nbytes((t, width), F32)),
    )(x, cos, sin)


def _visible(t, unit, transposed):
    q_idx = lax.broadcasted_iota(jnp.int32, (t, t), 1 if transposed else 0)
    k_idx = lax.broadcasted_iota(jnp.int32, (t, t), 0 if transposed else 1)
    shift = int(math.log2(unit))
    return (q_idx >> shift) >= (k_idx >> shift)


def _attn_tile(s):
    return min(512, s // 4)


def _attn_fwd(q, k, v, cq, ck, *, scale, unit, name):
    hn, s, dk = q.shape
    dv = v.shape[-1]
    t = _attn_tile(s)
    decay = cq is not None

    def body(*refs):
        q_ref, k_ref, v_ref = refs[:3]
        cq_ref, ck_ref = (refs[3], refs[4]) if decay else (None, None)
        o_ref, lse_ref = refs[-2], refs[-1]
        i = pl.program_id(1)
        qt = q_ref[0]

        def tile(j, carry, masked):
            m, l, acc = carry
            off = pl.multiple_of(j * t, t)
            kt = k_ref[0, pl.ds(off, t), :]
            vt = v_ref[0, pl.ds(off, t), :]
            sc = _dot_nt(qt, kt) * scale
            if decay:
                sc = sc + cq_ref[0] - ck_ref[0, :, pl.ds(off, t)]
            if masked:
                sc = jnp.where(_visible(t, unit, False), sc, NEG_INF)
            m_new = jnp.maximum(m, jnp.max(sc, axis=-1, keepdims=True))
            alpha = jnp.exp(m - m_new)
            pr = jnp.exp(sc - m_new)
            l = alpha * l + jnp.sum(pr, axis=-1, keepdims=True)
            acc = alpha * acc + _dot(pr, vt)
            return m_new, l, acc

        init = (jnp.full((t, 1), NEG_INF, F32), jnp.zeros((t, 1), F32), jnp.zeros((t, dv), F32))
        carry = lax.fori_loop(0, i, lambda j, c: tile(j, c, False), init)
        m, l, acc = tile(i, carry, True)
        o_ref[0] = (acc / l).astype(o_ref.dtype)
        lse_ref[0] = m + jnp.log(l)

    qs = lambda d: pl.BlockSpec((1, t, d), lambda h, i: (h, i, 0))
    whole = lambda d: pl.BlockSpec((1, s, d), lambda h, i: (h, 0, 0))
    in_specs = [qs(dk), whole(dk), whole(dv)]
    ins = [q, k, v]
    if decay:
        in_specs += [qs(1), pl.BlockSpec((1, 1, s), lambda h, i: (h, 0, 0))]
        ins += [cq, ck]
    vmem = 4 * _nbytes((s, dk + dv), q.dtype) + 10 * _nbytes((t, t), F32) + 8 * _nbytes((t, V7X_LANES), F32)
    return pl.pallas_call(
        body, name=name, grid=(hn, s // t),
        in_specs=in_specs, out_specs=[qs(dv), qs(1)],
        out_shape=[jax.ShapeDtypeStruct((hn, s, dv), MXU_DTYPE), jax.ShapeDtypeStruct((hn, s, 1), F32)],
        compiler_params=_params(("parallel", "arbitrary"), vmem),
    )(*ins)


def _attn_bwd_q(q, k, v, do, lse, cq, ck, *, scale, unit, name):
    hn, s, dk = q.shape
    dv = v.shape[-1]
    t = _attn_tile(s)
    nt = s // t
    decay = cq is not None

    def body(*refs):
        q_ref, k_ref, v_ref, do_ref, lse_ref = refs[:5]
        cq_ref, ck_ref = (refs[5], refs[6]) if decay else (None, None)
        dq_ref, dl_ref, p_sc, dp_sc = refs[-4:]
        i = pl.program_id(1)
        qt = q_ref[0]
        dot = do_ref[0]
        lse_t = lse_ref[0]

        def sweep1(j, delta, masked):
            off = pl.multiple_of(j * t, t)
            kt = k_ref[0, pl.ds(off, t), :]
            vt = v_ref[0, pl.ds(off, t), :]
            sc = _dot_nt(qt, kt) * scale
            if decay:
                sc = sc + cq_ref[0] - ck_ref[0, :, pl.ds(off, t)]
            if masked:
                sc = jnp.where(_visible(t, unit, False), sc, NEG_INF)
            pr = jnp.exp(sc - lse_t)
            dp = _dot_nt(dot, vt)
            p_sc[j] = pr
            dp_sc[j] = dp
            return delta + jnp.sum(pr * dp, axis=-1, keepdims=True)

        delta = lax.fori_loop(0, i, lambda j, c: sweep1(j, c, False), jnp.zeros((t, 1), F32))
        delta = sweep1(i, delta, True)

        def sweep2(j, dq):
            off = pl.multiple_of(j * t, t)
            ds = p_sc[j] * (dp_sc[j] - delta)
            return dq + _dot(ds, k_ref[0, pl.ds(off, t), :])

        dq = lax.fori_loop(0, i + 1, sweep2, jnp.zeros((t, dk), F32))
        dq_ref[0] = dq * scale
        dl_ref[0] = delta

    qs = lambda d: pl.BlockSpec((1, t, d), lambda h, i: (h, i, 0))
    whole = lambda d: pl.BlockSpec((1, s, d), lambda h, i: (h, 0, 0))
    in_specs = [qs(dk), whole(dk), whole(dv), qs(dv), qs(1)]
    ins = [q, k, v, do, lse]
    if decay:
        in_specs += [qs(1), pl.BlockSpec((1, 1, s), lambda h, i: (h, 0, 0))]
        ins += [cq, ck]
    vmem = (4 * _nbytes((s, dk + dv), q.dtype) + 2 * _nbytes((nt, t, t), F32) + 8 * _nbytes((t, t), F32)
            + 12 * _nbytes((t, V7X_LANES), F32))
    return pl.pallas_call(
        body, name=name, grid=(hn, nt),
        in_specs=in_specs, out_specs=[qs(dk), qs(1)],
        out_shape=[jax.ShapeDtypeStruct((hn, s, dk), F32), jax.ShapeDtypeStruct((hn, s, 1), F32)],
        scratch_shapes=[pltpu.VMEM((nt, t, t), F32), pltpu.VMEM((nt, t, t), F32)],
        compiler_params=_params(("parallel", "arbitrary"), vmem),
    )(*ins)


def _attn_bwd_kv(q, k, v, do, lse_row, delta_row, cq_row, ck, *, scale, unit, name):
    hn, s, dk = q.shape
    dv = v.shape[-1]
    t = _attn_tile(s)
    nt = s // t
    decay = ck is not None

    def body(*refs):
        k_ref, v_ref, q_ref, do_ref, lse_ref, dl_ref = refs[:6]
        ck_ref, cq_ref = (refs[6], refs[7]) if decay else (None, None)
        outs = refs[8:] if decay else refs[6:]
        dk_ref, dv_ref = outs[0], outs[1]
        j = pl.program_id(1)
        kt = k_ref[0]
        vt = v_ref[0]

        def tile(i, carry, masked):
            dk_acc, dv_acc, dc_acc = carry
            off = pl.multiple_of(i * t, t)
            qt = q_ref[0, pl.ds(off, t), :]
            dot = do_ref[0, pl.ds(off, t), :]
            sc = _dot_nt(kt, qt) * scale
            if decay:
                sc = sc + cq_ref[0, :, pl.ds(off, t)] - ck_ref[0]
            if masked:
                sc = jnp.where(_visible(t, unit, True), sc, NEG_INF)
            pr = jnp.exp(sc - lse_ref[0, :, pl.ds(off, t)])
            dv_acc = dv_acc + _dot(pr, dot)
            ds = pr * (_dot_nt(vt, dot) - dl_ref[0, :, pl.ds(off, t)])
            dk_acc = dk_acc + _dot(ds, qt)
            if decay:
                dc_acc = dc_acc + jnp.sum(ds, axis=-1, keepdims=True)
            return dk_acc, dv_acc, dc_acc

        init = (jnp.zeros((t, dk), F32), jnp.zeros((t, dv), F32), jnp.zeros((t, 1), F32))
        carry = tile(j, init, True)
        dk_acc, dv_acc, dc_acc = lax.fori_loop(j + 1, nt, lambda i, c: tile(i, c, False), carry)
        dk_ref[0] = dk_acc * scale
        dv_ref[0] = dv_acc
        if decay:
            outs[2][0] = -dc_acc

    ks = lambda d: pl.BlockSpec((1, t, d), lambda h, j: (h, j, 0))
    whole = lambda d: pl.BlockSpec((1, s, d), lambda h, j: (h, 0, 0))
    row = pl.BlockSpec((1, 1, s), lambda h, j: (h, 0, 0))
    in_specs = [ks(dk), ks(dv), whole(dk), whole(dv), row, row]
    ins = [k, v, q, do, lse_row, delta_row]
    out_specs = [ks(dk), ks(dv)]
    out_shape = [jax.ShapeDtypeStruct((hn, s, dk), F32), jax.ShapeDtypeStruct((hn, s, dv), F32)]
    if decay:
        in_specs += [ks(1), row]
        ins += [ck, cq_row]
        out_specs.append(ks(1))
        out_shape.append(jax.ShapeDtypeStruct((hn, s, 1), F32))
    vmem = 4 * _nbytes((s, dk + dv), q.dtype) + 10 * _nbytes((t, t), F32) + 12 * _nbytes((t, V7X_LANES), F32)
    return pl.pallas_call(
        body, name=name, grid=(hn, nt),
        in_specs=in_specs, out_specs=out_specs, out_shape=out_shape,
        compiler_params=_params(("parallel", "arbitrary"), vmem),
    )(*ins)


STRIP = 32
HEAD_PAIRS = HEADS // 2


def _strip_rows(t):
    return min(STRIP, t)


def _pair_mask(t):
    lane = lax.broadcasted_iota(jnp.int32, (t, V7X_LANES), 1)
    return lane < (V7X_LANES // 2)


def _strip_visible(r, t, row0, unit, transposed):
    rows = lax.broadcasted_iota(jnp.int32, (r, t), 0) + row0
    cols = lax.broadcasted_iota(jnp.int32, (r, t), 1)
    shift = int(math.log2(unit))
    if transposed:
        return (cols >> shift) >= (rows >> shift)
    return (rows >> shift) >= (cols >> shift)


def _rope_lanes(x, cos, sin, *, cb, groups, out_dtype, sum_parts=0, name):
    s = cos.shape[0]
    t = _tile(s, 512)
    w = groups * V7X_LANES

    def body(x_ref, c_ref, s_ref, o_ref):
        if sum_parts:
            v = x_ref[0].astype(F32)
            for part in range(1, sum_parts):
                v = v + x_ref[part].astype(F32)
            o_ref[...] = _rope_apply(v, c_ref[...], s_ref[...], V7X_LANES).astype(o_ref.dtype)
        else:
            for g in range(groups):
                sl = slice(g * V7X_LANES, (g + 1) * V7X_LANES)
                o_ref[:, sl] = _rope_apply(x_ref[:, sl].astype(F32), c_ref[...], s_ref[...],
                                           V7X_LANES).astype(o_ref.dtype)

    x_spec = (pl.BlockSpec((sum_parts, t, V7X_LANES), lambda i: (0, i, 0)) if sum_parts else _rows(t, w, cb))
    return pl.pallas_call(
        body, name=name, grid=(s // t,),
        in_specs=[x_spec, _rows(t, V7X_LANES), _rows(t, V7X_LANES)], out_specs=_rows(t, w),
        out_shape=jax.ShapeDtypeStruct((s, w), out_dtype),
        compiler_params=_params(("parallel",), 12 * _nbytes((t, max(w, 4 * V7X_LANES)), F32)),
    )(x, cos, sin)


def _pair_fwd(q_arr, q_cb, k_arr, k_cb, v_arr, v_cb, rope, decay, *, scale, unit, name):
    s = q_arr.shape[0]
    t = _attn_tile(s)
    r = _strip_rows(t)
    has_rope, has_decay = rope is not None, decay is not None
    kw = 2 * V7X_LANES if has_rope else V7X_LANES

    def body(*refs):
        it = iter(refs)
        q_ref, k_ref, v_ref = next(it), next(it), next(it)
        qr_ref, kr_ref = (next(it), next(it)) if has_rope else (None, None)
        cq_ref, ck_ref = (next(it), next(it)) if has_decay else (None, None)
        o_ref, lse_ref = next(it), next(it)
        q_sc, s_sc, p_sc, acc_sc, mx_sc, ls_sc = (next(it) for _ in range(6))
        i = pl.program_id(1)
        in_a = _pair_mask(t)
        qv = q_ref[...]
        for hd in range(2):
            q_sc[hd, :, 0:V7X_LANES] = jnp.where(in_a if hd == 0 else jnp.logical_not(in_a), qv, 0).astype(MXU_DTYPE)
            if has_rope:
                q_sc[hd, :, V7X_LANES:kw] = qr_ref[:, hd * V7X_LANES:(hd + 1) * V7X_LANES].astype(MXU_DTYPE)
        mx_sc[...] = jnp.full(mx_sc.shape, NEG_INF, F32)
        ls_sc[...] = jnp.zeros(ls_sc.shape, F32)
        acc_sc[...] = jnp.zeros(acc_sc.shape, F32)
        cq_all = [cq_ref[hd] for hd in range(2)] if has_decay else None
        chunks = t // V7X_LANES

        def keys(j):
            off = pl.multiple_of(j * t, t)
            kt = k_ref[pl.ds(off, t), :]
            if has_rope:
                kt = jnp.concatenate([kt, kr_ref[pl.ds(off, t), :]], axis=-1)
            return off, kt

        def strip_scores(hd, row0, ck_row, masked):
            sc = s_sc[hd, pl.ds(row0, r), :] * scale
            if has_decay:
                sc = sc + (cq_all[hd][row0:row0 + r] - ck_row)
            if masked:
                sc = jnp.where(_strip_visible(r, t, row0, unit, False), sc, NEG_INF)
            return sc

        def fold(v, op):
            out = v[:, 0:V7X_LANES]
            for ch in range(1, chunks):
                out = op(out, v[:, ch * V7X_LANES:(ch + 1) * V7X_LANES])
            return out

        def tile_max(j, masked):
            off, kt = keys(j)
            for hd in range(2):
                s_sc[hd] = _dot_nt(q_sc[hd], kt)
                ck_row = ck_ref[hd, :, pl.ds(off, t)] if has_decay else None
                for b in range(t // r):
                    rows = pl.ds(b * r, r)
                    sc = strip_scores(hd, b * r, ck_row, masked)
                    mx_sc[hd, rows, :] = jnp.maximum(mx_sc[hd, rows, :], fold(sc, jnp.maximum))

        lax.fori_loop(0, i, lambda j, c: (tile_max(j, False), c)[1], 0)
        tile_max(i, True)
        m_all = [jnp.max(mx_sc[hd], axis=-1, keepdims=True) for hd in range(2)]

        def tile_sum(j, masked):
            off, kt = keys(j)
            vt = v_ref[pl.ds(off, t), :]
            for hd in range(2):
                s_sc[hd] = _dot_nt(q_sc[hd], kt)
                ck_row = ck_ref[hd, :, pl.ds(off, t)] if has_decay else None
                for b in range(t // r):
                    row0 = b * r
                    rows = pl.ds(row0, r)
                    pr = jnp.exp(strip_scores(hd, row0, ck_row, masked) - m_all[hd][row0:row0 + r])
                    ls_sc[hd, rows, :] += fold(pr, jnp.add)
                    p_sc[hd, rows, :] = pr.astype(MXU_DTYPE)
                acc_sc[hd] += _dot(p_sc[hd], vt)

        lax.fori_loop(0, i, lambda j, c: (tile_sum(j, False), c)[1], 0)
        tile_sum(i, True)
        l_all = [jnp.sum(ls_sc[hd], axis=-1, keepdims=True) for hd in range(2)]
        o_ref[...] = jnp.where(in_a, acc_sc[0] / l_all[0], acc_sc[1] / l_all[1]).astype(o_ref.dtype)
        for hd in range(2):
            lse_ref[hd] = m_all[hd] + jnp.log(l_all[hd])

    blk = lambda cb: pl.BlockSpec((t, V7X_LANES), lambda p, i: (i, cb + p))
    whole = lambda cb: pl.BlockSpec((s, V7X_LANES), lambda p, i: (0, cb + p))
    stat = pl.BlockSpec((2, t, 1), lambda p, i: (p, i, 0))
    in_specs = [blk(q_cb), whole(k_cb), whole(v_cb)]
    ins = [q_arr, k_arr, v_arr]
    if has_rope:
        in_specs += [pl.BlockSpec((t, 2 * V7X_LANES), lambda p, i: (i, p)),
                     pl.BlockSpec((s, V7X_LANES), lambda p, i: (0, 0))]
        ins += list(rope)
    if has_decay:
        in_specs += [stat, pl.BlockSpec((2, 1, s), lambda p, i: (p, 0, 0))]
        ins += list(decay)
    col = (2, t, 1)
    vmem = (6 * _nbytes((s, V7X_LANES), MXU_DTYPE) + 6 * _nbytes((t, t), F32) + 10 * _nbytes((t, V7X_LANES), F32)
            + 8 * _nbytes((2, t, V7X_LANES), F32))
    return pl.pallas_call(
        body, name=name, grid=(HEAD_PAIRS, s // t),
        in_specs=in_specs, out_specs=[pl.BlockSpec((t, V7X_LANES), lambda p, i: (i, p)), stat],
        out_shape=[jax.ShapeDtypeStruct((s, HEADS * 64), MXU_DTYPE), jax.ShapeDtypeStruct((HEADS, s, 1), F32)],
        scratch_shapes=[pltpu.VMEM((2, t, kw), MXU_DTYPE), pltpu.VMEM((2, t, t), F32), pltpu.VMEM((2, t, t), MXU_DTYPE),
                        pltpu.VMEM((2, t, V7X_LANES), F32), pltpu.VMEM((2, t, V7X_LANES), F32),
                        pltpu.VMEM((2, t, V7X_LANES), F32)],
        compiler_params=_params(("parallel", "arbitrary"), vmem),
    )(*ins)


def _pair_bwd_q(q_arr, q_cb, k_arr, k_cb, v_arr, v_cb, do, lse, rope, decay, *, scale, unit, name):
    s = q_arr.shape[0]
    t = _attn_tile(s)
    nt = s // t
    r = t
    has_rope, has_decay = rope is not None, decay is not None
    kw = 2 * V7X_LANES if has_rope else V7X_LANES

    def body(*refs):
        it = iter(refs)
        q_ref, k_ref, v_ref, do_ref, lse_ref = (next(it) for _ in range(5))
        qr_ref, kr_ref = (next(it), next(it)) if has_rope else (None, None)
        cq_ref, ck_ref = (next(it), next(it)) if has_decay else (None, None)
        dq_ref, dl_ref = next(it), next(it)
        dqr_ref = next(it) if has_rope else None
        q_sc, do_sc, p_sc, dp_sc, ds_sc, dq_sc, dl_sc, s_sc = (next(it) for _ in range(8))
        i = pl.program_id(1)
        in_a = _pair_mask(t)
        qv = q_ref[...]
        dov = do_ref[...]
        for hd in range(2):
            sel = in_a if hd == 0 else jnp.logical_not(in_a)
            q_sc[hd, :, 0:V7X_LANES] = jnp.where(sel, qv, 0).astype(MXU_DTYPE)
            if has_rope:
                q_sc[hd, :, V7X_LANES:kw] = qr_ref[:, hd * V7X_LANES:(hd + 1) * V7X_LANES].astype(MXU_DTYPE)
            do_sc[hd] = jnp.where(sel, dov, 0).astype(MXU_DTYPE)
        dl_sc[...] = jnp.zeros(dl_sc.shape, F32)
        dq_sc[...] = jnp.zeros(dq_sc.shape, F32)

        def keys(j):
            off = pl.multiple_of(j * t, t)
            kt = k_ref[pl.ds(off, t), :]
            if has_rope:
                kt = jnp.concatenate([kt, kr_ref[pl.ds(off, t), :]], axis=-1)
            return off, kt

        for hd in range(2):
            lse_all = lse_ref[hd]
            cq_all = cq_ref[hd] if has_decay else None

            def sweep1(j, masked, hd=hd, lse_all=lse_all, cq_all=cq_all):
                off, kt = keys(j)
                s_sc[...] = _dot_nt(q_sc[hd], kt)
                dp_sc[j] = _dot_nt(do_sc[hd], v_ref[pl.ds(off, t), :])
                ck_row = ck_ref[hd, :, pl.ds(off, t)] if has_decay else None
                parts = []
                for b in range(t // r):
                    row0 = b * r
                    rows = pl.ds(row0, r)
                    sc = s_sc[rows, :] * scale
                    if has_decay:
                        sc = sc + (cq_all[row0:row0 + r] - ck_row)
                    if masked:
                        sc = jnp.where(_strip_visible(r, t, row0, unit, False), sc, NEG_INF)
                    pr = jnp.exp(sc - lse_all[row0:row0 + r])
                    p_sc[j, rows, :] = pr
                    parts.append(jnp.sum(pr * dp_sc[j, rows, :], axis=-1, keepdims=True))
                dl_sc[hd] += jnp.concatenate(parts, axis=0)

            def sweep1_unmasked(j, carry, sweep1=sweep1):
                sweep1(j, False)
                return carry

            lax.fori_loop(0, i, sweep1_unmasked, 0)
            sweep1(i, True)
            dl_all = dl_sc[hd]

            def sweep2(j, carry, hd=hd, dl_all=dl_all):
                _, kt = keys(j)
                for b in range(t // r):
                    row0 = b * r
                    rows = pl.ds(row0, r)
                    ds = p_sc[j, rows, :] * (dp_sc[j, rows, :] - dl_all[row0:row0 + r])
                    ds_sc[rows, :] = ds.astype(MXU_DTYPE)
                dq_sc[hd] += _dot(ds_sc[...], kt)
                return carry

            lax.fori_loop(0, i + 1, sweep2, 0)

        dq_ref[...] = (jnp.where(in_a, dq_sc[0, :, 0:V7X_LANES], dq_sc[1, :, 0:V7X_LANES]) * scale).astype(dq_ref.dtype)
        dl_ref[...] = dl_sc[...]
        if has_rope:
            dqr_ref[:, 0:V7X_LANES] = dq_sc[0, :, V7X_LANES:kw] * scale
            dqr_ref[:, V7X_LANES:kw] = dq_sc[1, :, V7X_LANES:kw] * scale

    blk = lambda cb: pl.BlockSpec((t, V7X_LANES), lambda p, i: (i, cb + p))
    whole = lambda cb: pl.BlockSpec((s, V7X_LANES), lambda p, i: (0, cb + p))
    stat = pl.BlockSpec((2, t, 1), lambda p, i: (p, i, 0))
    in_specs = [blk(q_cb), whole(k_cb), whole(v_cb), blk(0), stat]
    ins = [q_arr, k_arr, v_arr, do, lse]
    out_specs = [blk(0), stat]
    out_shape = [jax.ShapeDtypeStruct((s, HEADS * 64), MXU_DTYPE), jax.ShapeDtypeStruct((HEADS, s, 1), F32)]
    if has_rope:
        pair_rot = pl.BlockSpec((t, 2 * V7X_LANES), lambda p, i: (i, p))
        in_specs += [pair_rot, pl.BlockSpec((s, V7X_LANES), lambda p, i: (0, 0))]
        ins += list(rope)
        out_specs.append(pair_rot)
        out_shape.append(jax.ShapeDtypeStruct((s, HEADS * V7X_LANES), F32))
    if has_decay:
        in_specs += [stat, pl.BlockSpec((2, 1, s), lambda p, i: (p, 0, 0))]
        ins += list(decay)
    vmem = (6 * _nbytes((s, V7X_LANES), MXU_DTYPE) + 2 * _nbytes((nt, t, t), F32) + 6 * _nbytes((t, t), F32)
            + 16 * _nbytes((t, kw), F32))
    return pl.pallas_call(
        body, name=name, grid=(HEAD_PAIRS, nt),
        in_specs=in_specs, out_specs=out_specs, out_shape=out_shape,
        scratch_shapes=[pltpu.VMEM((2, t, kw), MXU_DTYPE), pltpu.VMEM((2, t, V7X_LANES), MXU_DTYPE),
                        pltpu.VMEM((nt, t, t), F32), pltpu.VMEM((nt, t, t), F32), pltpu.VMEM((t, t), MXU_DTYPE),
                        pltpu.VMEM((2, t, kw), F32), pltpu.VMEM((2, t, 1), F32), pltpu.VMEM((t, t), F32)],
        compiler_params=_params(("parallel", "arbitrary"), vmem),
    )(*ins)


def _pair_delta(q_arr, q_cb, k_arr, k_cb, v_arr, v_cb, do, lse, rope, decay, *, scale, unit, name):
    s = q_arr.shape[0]
    t = _attn_tile(s)
    r = _strip_rows(t)
    has_rope, has_decay = rope is not None, decay is not None
    kw = 2 * V7X_LANES if has_rope else V7X_LANES
    chunks = t // V7X_LANES

    def body(*refs):
        it = iter(refs)
        q_ref, k_ref, v_ref, do_ref, lse_ref = (next(it) for _ in range(5))
        qr_ref, kr_ref = (next(it), next(it)) if has_rope else (None, None)
        cq_ref, ck_ref = (next(it), next(it)) if has_decay else (None, None)
        dl_ref = next(it)
        q_sc, do_sc, s_sc, dp_sc, acc_sc = (next(it) for _ in range(5))
        i = pl.program_id(1)
        in_a = _pair_mask(t)
        qv, dov = q_ref[...], do_ref[...]
        for hd in range(2):
            sel = in_a if hd == 0 else jnp.logical_not(in_a)
            q_sc[hd, :, 0:V7X_LANES] = jnp.where(sel, qv, 0).astype(MXU_DTYPE)
            if has_rope:
                q_sc[hd, :, V7X_LANES:kw] = qr_ref[:, hd * V7X_LANES:(hd + 1) * V7X_LANES].astype(MXU_DTYPE)
            do_sc[hd] = jnp.where(sel, dov, 0).astype(MXU_DTYPE)
        acc_sc[...] = jnp.zeros(acc_sc.shape, F32)
        lse_all = [lse_ref[hd] for hd in range(2)]
        cq_all = [cq_ref[hd] for hd in range(2)] if has_decay else None

        def fold_add(v):
            out = v[:, 0:V7X_LANES]
            for ch in range(1, chunks):
                out = out + v[:, ch * V7X_LANES:(ch + 1) * V7X_LANES]
            return out

        def tile(j, masked):
            off = pl.multiple_of(j * t, t)
            kt = k_ref[pl.ds(off, t), :]
            if has_rope:
                kt = jnp.concatenate([kt, kr_ref[pl.ds(off, t), :]], axis=-1)
            vt = v_ref[pl.ds(off, t), :]
            for hd in range(2):
                s_sc[hd] = _dot_nt(q_sc[hd], kt)
                dp_sc[hd] = _dot_nt(do_sc[hd], vt)
                ck_row = ck_ref[hd, :, pl.ds(off, t)] if has_decay else None
                for b in range(t // r):
                    row0 = b * r
                    rows = pl.ds(row0, r)
                    sc = s_sc[hd, rows, :] * scale
                    if has_decay:
                        sc = sc + (cq_all[hd][row0:row0 + r] - ck_row)
                    if masked:
                        sc = jnp.where(_strip_visible(r, t, row0, unit, False), sc, NEG_INF)
                    pr = jnp.exp(sc - lse_all[hd][row0:row0 + r])
                    acc_sc[hd, rows, :] += fold_add(pr * dp_sc[hd, rows, :])

        lax.fori_loop(0, i, lambda j, c: (tile(j, False), c)[1], 0)
        tile(i, True)
        for hd in range(2):
            dl_ref[hd] = jnp.sum(acc_sc[hd], axis=-1, keepdims=True)

    blk = lambda cb: pl.BlockSpec((t, V7X_LANES), lambda p, i: (i, cb + p))
    whole = lambda cb: pl.BlockSpec((s, V7X_LANES), lambda p, i: (0, cb + p))
    stat = pl.BlockSpec((2, t, 1), lambda p, i: (p, i, 0))
    in_specs = [blk(q_cb), whole(k_cb), whole(v_cb), blk(0), stat]
    ins = [q_arr, k_arr, v_arr, do, lse]
    if has_rope:
        in_specs += [pl.BlockSpec((t, 2 * V7X_LANES), lambda p, i: (i, p)),
                     pl.BlockSpec((s, V7X_LANES), lambda p, i: (0, 0))]
        ins += list(rope)
    if has_decay:
        in_specs += [stat, pl.BlockSpec((2, 1, s), lambda p, i: (p, 0, 0))]
        ins += list(decay)
    vmem = (6 * _nbytes((s, V7X_LANES), MXU_DTYPE) + 8 * _nbytes((t, t), F32) + 12 * _nbytes((t, kw), F32))
    return pl.pallas_call(
        body, name=name, grid=(HEAD_PAIRS, s // t),
        in_specs=in_specs, out_specs=stat, out_shape=jax.ShapeDtypeStruct((HEADS, s, 1), F32),
        scratch_shapes=[pltpu.VMEM((2, t, kw), MXU_DTYPE), pltpu.VMEM((2, t, V7X_LANES), MXU_DTYPE),
                        pltpu.VMEM((2, t, t), F32), pltpu.VMEM((2, t, t), F32),
                        pltpu.VMEM((2, t, V7X_LANES), F32)],
        compiler_params=_params(("parallel", "arbitrary"), vmem),
    )(*ins)


def _pair_bwd_kv(q_arr, q_cb, k_arr, k_cb, v_arr, v_cb, do, lse_row, delta_row, rope, decay, *, scale, unit, name):
    s = q_arr.shape[0]
    t = _attn_tile(s)
    nt = s // t
    r = _strip_rows(t)
    has_rope, has_decay = rope is not None, decay is not None
    kw = 2 * V7X_LANES if has_rope else V7X_LANES

    def body(*refs):
        it = iter(refs)
        k_ref, v_ref, q_ref, do_ref, lse_ref, dl_ref = (next(it) for _ in range(6))
        qr_ref, kr_ref = (next(it), next(it)) if has_rope else (None, None)
        ck_ref, cq_ref = (next(it), next(it)) if has_decay else (None, None)
        dk_ref, dv_ref, dq_ref = next(it), next(it), next(it)
        dkr_ref, dqr_ref = (next(it), next(it)) if has_rope else (None, None)
        dc_ref = next(it) if has_decay else None
        k_sc, v_sc, st_sc, dpt_sc, pt_sc, dst_sc, dk_sc, dv_sc, dc_sc, dqt_sc, kt_sc = (next(it) for _ in range(11))
        j = pl.program_id(1)
        in_a = _pair_mask(t)
        kv_, vv_ = k_ref[...], v_ref[...]
        for hd in range(2):
            sel = in_a if hd == 0 else jnp.logical_not(in_a)
            k_sc[hd, :, 0:V7X_LANES] = jnp.where(sel, kv_, 0).astype(MXU_DTYPE)
            if has_rope:
                k_sc[hd, :, V7X_LANES:kw] = kr_ref[...].astype(MXU_DTYPE)
            v_sc[hd] = jnp.where(sel, vv_, 0).astype(MXU_DTYPE)
        dk_sc[...] = jnp.zeros(dk_sc.shape, F32)
        dv_sc[...] = jnp.zeros(dv_sc.shape, F32)
        dc_sc[...] = jnp.zeros(dc_sc.shape, F32)
        k_all = kv_.astype(F32)
        if has_rope:
            k_all = jnp.concatenate([k_all, kr_ref[...].astype(F32)], axis=-1)
        kt_sc[...] = k_all.T.astype(MXU_DTYPE)

        @pl.when(j == 0)
        def _():
            dqt_sc[...] = jnp.zeros(dqt_sc.shape, F32)

        def tile(i, masked):
            off = pl.multiple_of(i * t, t)
            qt = q_ref[pl.ds(off, t), :]
            dot = do_ref[pl.ds(off, t), :]
            for hd in range(2):
                qcat = qt
                if has_rope:
                    qcat = jnp.concatenate([qt, qr_ref[pl.ds(off, t), hd * V7X_LANES:(hd + 1) * V7X_LANES]], axis=-1)
                st_sc[hd] = _dot_nt(k_sc[hd], qcat)
                dpt_sc[hd] = _dot_nt(v_sc[hd], dot)
                lse_r = lse_ref[hd, :, pl.ds(off, t)]
                dl_r = dl_ref[hd, :, pl.ds(off, t)]
                cq_r = cq_ref[hd, :, pl.ds(off, t)] if has_decay else None
                ck_all = ck_ref[hd] if has_decay else None
                parts = []
                for b in range(t // r):
                    row0 = b * r
                    rows = pl.ds(row0, r)
                    sc = st_sc[hd, rows, :] * scale
                    if has_decay:
                        sc = sc + (cq_r - ck_all[row0:row0 + r])
                    if masked:
                        sc = jnp.where(_strip_visible(r, t, row0, unit, True), sc, NEG_INF)
                    pr = jnp.exp(sc - lse_r)
                    ds = pr * (dpt_sc[hd, rows, :] - dl_r)
                    pt_sc[hd, rows, :] = pr.astype(MXU_DTYPE)
                    dst_sc[hd, rows, :] = ds.astype(MXU_DTYPE)
                    if has_decay:
                        parts.append(jnp.sum(ds, axis=-1, keepdims=True))
                if has_decay:
                    dc_sc[hd] += jnp.concatenate(parts, axis=0)
                dv_sc[hd] += _dot(pt_sc[hd], dot)
                dk_sc[hd] += _dot(dst_sc[hd], qcat)
                dqt_sc[hd, :, pl.ds(off, t)] += _dot(kt_sc[...], dst_sc[hd])

        tile(j, True)

        def unmasked(i, carry):
            tile(i, False)
            return carry

        lax.fori_loop(j + 1, nt, unmasked, 0)
        dk_ref[...] = (jnp.where(in_a, dk_sc[0, :, 0:V7X_LANES], dk_sc[1, :, 0:V7X_LANES]) * scale).astype(dk_ref.dtype)
        dv_ref[...] = jnp.where(in_a, dv_sc[0], dv_sc[1]).astype(dv_ref.dtype)
        if has_rope:
            dkr_ref[0] = (dk_sc[0, :, V7X_LANES:kw] + dk_sc[1, :, V7X_LANES:kw]) * scale
        if has_decay:
            dc_ref[...] = -dc_sc[...]
        own = pl.ds(pl.multiple_of(j * t, t), t)
        dq_a = dqt_sc[0, :, own].T * scale
        dq_b = dqt_sc[1, :, own].T * scale
        dq_ref[...] = jnp.where(in_a, dq_a[:, 0:V7X_LANES], dq_b[:, 0:V7X_LANES]).astype(dq_ref.dtype)
        if has_rope:
            dqr_ref[:, 0:V7X_LANES] = dq_a[:, V7X_LANES:kw]
            dqr_ref[:, V7X_LANES:kw] = dq_b[:, V7X_LANES:kw]

    blk = lambda cb: pl.BlockSpec((t, V7X_LANES), lambda p, j: (j, cb + p))
    whole = lambda cb: pl.BlockSpec((s, V7X_LANES), lambda p, j: (0, cb + p))
    stat = pl.BlockSpec((2, t, 1), lambda p, j: (p, j, 0))
    row = pl.BlockSpec((2, 1, s), lambda p, j: (p, 0, 0))
    in_specs = [blk(k_cb), blk(v_cb), whole(q_cb), whole(0), row, row]
    ins = [k_arr, v_arr, q_arr, do, lse_row, delta_row]
    out_specs = [blk(0), blk(0), blk(0)]
    out_shape = [jax.ShapeDtypeStruct((s, HEADS * 64), MXU_DTYPE)] * 3
    if has_rope:
        in_specs += [pl.BlockSpec((s, 2 * V7X_LANES), lambda p, j: (0, p)),
                     pl.BlockSpec((t, V7X_LANES), lambda p, j: (j, 0))]
        ins += list(rope)
        out_specs += [pl.BlockSpec((1, t, V7X_LANES), lambda p, j: (p, j, 0)),
                      pl.BlockSpec((t, 2 * V7X_LANES), lambda p, j: (j, p))]
        out_shape += [jax.ShapeDtypeStruct((HEAD_PAIRS, s, V7X_LANES), F32),
                      jax.ShapeDtypeStruct((s, HEADS * V7X_LANES), F32)]
    if has_decay:
        in_specs += [stat, row]
        ins += list(decay)
        out_specs.append(stat)
        out_shape.append(jax.ShapeDtypeStruct((HEADS, s, 1), F32))
    vmem = (12 * _nbytes((s, V7X_LANES), MXU_DTYPE) + 8 * _nbytes((t, t), F32) + 16 * _nbytes((t, kw), F32)
            + _nbytes((2, kw, s), F32))
    return pl.pallas_call(
        body, name=name, grid=(HEAD_PAIRS, nt),
        in_specs=in_specs, out_specs=out_specs, out_shape=out_shape,
        scratch_shapes=[pltpu.VMEM((2, t, kw), MXU_DTYPE), pltpu.VMEM((2, t, V7X_LANES), MXU_DTYPE),
                        pltpu.VMEM((2, t, t), F32), pltpu.VMEM((2, t, t), F32), pltpu.VMEM((2, t, t), MXU_DTYPE),
                        pltpu.VMEM((2, t, t), MXU_DTYPE), pltpu.VMEM((2, t, kw), F32),
                        pltpu.VMEM((2, t, V7X_LANES), F32), pltpu.VMEM((2, t, 1), F32),
                        pltpu.VMEM((2, kw, s), F32), pltpu.VMEM((kw, t), MXU_DTYPE)],
        compiler_params=_params(("arbitrary", "arbitrary"), vmem),
    )(*ins)


def _fox_cum(z, bf, *, name):
    s = z.shape[0]
    w = V7X_LANES
    t = _row_tile(s, 512)
    steps = [1 << k for k in range(int(math.log2(t)))]
    cb = SEG["fl"][3] // w

    def body(f_ref, bf_ref, c_ref, car):
        @pl.when(pl.program_id(0) == 0)
        def _():
            car[...] = jnp.zeros_like(car)

        acc = -_softplus(-(f_ref[...] + bf_ref[...]))
        rows = lax.broadcasted_iota(jnp.int32, (t, w), 0)
        for d in steps:
            acc = acc + _shift_down(acc, d, 0.0, rows)
        c_ref[...] = acc + car[0:1, :]
        car[0:1, :] = c_ref[t - 1:t, :]

    return pl.pallas_call(
        body, name=name, grid=(s // t,),
        in_specs=[_rows(t, w, cb), _full((1, w))], out_specs=_rows(t, w),
        out_shape=jax.ShapeDtypeStruct((s, w), F32),
        scratch_shapes=[pltpu.VMEM((V7X_SUBLANES, w), F32)],
        compiler_params=_params(("arbitrary",), 16 * _nbytes((t, w), F32)),
    )(z, bf)


def _fox_cum_bwd(z, bf, dcum, *, name):
    s = z.shape[0]
    w = V7X_LANES
    t = _row_tile(s, 512)
    nt = s // t
    steps = [1 << k for k in range(int(math.log2(t)))]
    cb = SEG["fl"][3] // w

    def body(f_ref, bf_ref, dc_ref, df_ref, dbf_ref, car, tmp):
        @pl.when(pl.program_id(0) == 0)
        def _():
            car[...] = jnp.zeros_like(car)
            dbf_ref[...] = jnp.zeros_like(dbf_ref)

        acc = dc_ref[...]
        rows = lax.broadcasted_iota(jnp.int32, (t, w), 0)
        for d in steps:
            acc = acc + _shift_up(acc, d, 0.0, rows, t)
        dlf = acc + car[0:1, :]
        tmp[...] = dlf
        car[0:1, :] = tmp[0:1, :]
        df = dlf * _sigmoid(-(f_ref[...] + bf_ref[...]))
        df_ref[...] = df.astype(df_ref.dtype)
        dbf_ref[...] += jnp.sum(df, axis=0, keepdims=True)

    rev = lambda cbk: pl.BlockSpec((t, w), lambda i: (nt - 1 - i, cbk))
    return pl.pallas_call(
        body, name=name, grid=(nt,),
        in_specs=[rev(cb), _full((1, w)), rev(0)], out_specs=[rev(0), _full((1, w))],
        out_shape=[jax.ShapeDtypeStruct((s, w), MXU_DTYPE), jax.ShapeDtypeStruct((1, w), F32)],
        scratch_shapes=[pltpu.VMEM((V7X_SUBLANES, w), F32), pltpu.VMEM((t, w), F32)],
        compiler_params=_params(("arbitrary",), 16 * _nbytes((t, w), F32)),
    )(z, bf, dcum)


_GATE_CB = SEG["gate"][3] // D_MODEL


def _merge_fwd(ya, yb, yc, z, gate_b, *, name):
    s = ya.shape[0]
    d = D_MODEL
    t = _tile(s, 256)

    def body(ya_ref, yb_ref, yc_ref, g0_ref, g1_ref, g2_ref, gb_ref, o_ref):
        out = _sigmoid(g0_ref[...] + gb_ref[:, 0:d]) * ya_ref[...]
        out = out + _sigmoid(g1_ref[...] + gb_ref[:, d:2 * d]) * yb_ref[...]
        out = out + _sigmoid(g2_ref[...] + gb_ref[:, 2 * d:3 * d]) * yc_ref[...]
        o_ref[...] = out.astype(o_ref.dtype)

    return pl.pallas_call(
        body, name=name, grid=(s // t,),
        in_specs=[_rows(t, d)] * 3 + [_rows(t, d, _GATE_CB + b) for b in range(3)] + [_full((1, 3 * d))],
        out_specs=_rows(t, d), out_shape=jax.ShapeDtypeStruct((s, d), MXU_DTYPE),
        compiler_params=_params(("parallel",), 20 * _nbytes((t, d), F32)),
    )(ya, yb, yc, z, z, z, gate_b)


def _merge_bwd(dm, ya, yb, yc, z, gate_b, *, name):
    s = ya.shape[0]
    d = D_MODEL
    t = _tile(s, 256)

    def body(dm_ref, ya_ref, yb_ref, yc_ref, g0_ref, g1_ref, g2_ref, gb_ref, da_ref, db_ref, dc_ref, dgl_ref,
             dgb_ref):
        dmv = dm_ref[...]
        parts = []
        for b, (y_ref, g_ref, dy_ref) in enumerate(((ya_ref, g0_ref, da_ref), (yb_ref, g1_ref, db_ref),
                                                    (yc_ref, g2_ref, dc_ref))):
            gate = _sigmoid(g_ref[...] + gb_ref[:, b * d:(b + 1) * d])
            dy_ref[...] = (dmv * gate).astype(dy_ref.dtype)
            dgl = dmv * y_ref[...] * gate * (1.0 - gate)
            dgl_ref[:, b * d:(b + 1) * d] = dgl.astype(dgl_ref.dtype)
            parts.append(jnp.sum(dgl, axis=0, keepdims=True))

        @pl.when(pl.program_id(0) == 0)
        def _():
            for b, part in enumerate(parts):
                dgb_ref[:, b * d:(b + 1) * d] = part

        @pl.when(pl.program_id(0) > 0)
        def _():
            for b, part in enumerate(parts):
                dgb_ref[:, b * d:(b + 1) * d] += part

    return pl.pallas_call(
        body, name=name, grid=(s // t,),
        in_specs=[_rows(t, d)] * 4 + [_rows(t, d, _GATE_CB + b) for b in range(3)] + [_full((1, 3 * d))],
        out_specs=[_rows(t, d)] * 3 + [_rows(t, 3 * d), _full((1, 3 * d))],
        out_shape=[jax.ShapeDtypeStruct((s, d), MXU_DTYPE)] * 3
        + [jax.ShapeDtypeStruct((s, 3 * d), MXU_DTYPE), jax.ShapeDtypeStruct((1, 3 * d), F32)],
        compiler_params=_params(("arbitrary",), 36 * _nbytes((t, d), F32)),
    )(dm, ya, yb, yc, z, z, z, gate_b)


def _swiglu_fwd(hf, *, name):
    s = hf.shape[0]
    t = _tile(s, 256)

    def body(g_ref, u_ref, o_ref):
        gv = g_ref[...]
        o_ref[...] = (gv * _sigmoid(gv) * u_ref[...]).astype(o_ref.dtype)

    return pl.pallas_call(
        body, name=name, grid=(s // t,),
        in_specs=[_rows(t, D_FF, 0), _rows(t, D_FF, 1)], out_specs=_rows(t, D_FF),
        out_shape=jax.ShapeDtypeStruct((s, D_FF), MXU_DTYPE),
        compiler_params=_params(("parallel",), 10 * _nbytes((t, D_FF), F32)),
    )(hf, hf)


def _swiglu_bwd(hf, dact, *, name):
    s = hf.shape[0]
    t = _tile(s, 256)

    def body(g_ref, u_ref, da_ref, o_ref):
        gv = g_ref[...]
        dav = da_ref[...]
        sg = _sigmoid(gv)
        o_ref[:, 0:D_FF] = (dav * u_ref[...] * sg * (1.0 + gv * (1.0 - sg))).astype(o_ref.dtype)
        o_ref[:, D_FF:2 * D_FF] = (dav * gv * sg).astype(o_ref.dtype)

    return pl.pallas_call(
        body, name=name, grid=(s // t,),
        in_specs=[_rows(t, D_FF, 0), _rows(t, D_FF, 1), _rows(t, D_FF)], out_specs=_rows(t, 2 * D_FF),
        out_shape=jax.ShapeDtypeStruct((s, 2 * D_FF), MXU_DTYPE),
        compiler_params=_params(("parallel",), 14 * _nbytes((t, D_FF), F32)),
    )(hf, hf, dact)


def _ple_fwd(x, lg, pe, *, name):
    s, d = x.shape
    t = _tile(s, 512)

    def body(x_ref, lg_ref, pe_ref, o_ref):
        o_ref[...] = x_ref[...] + _sigmoid(lg_ref[...]) * pe_ref[...]

    return pl.pallas_call(
        body, name=name, grid=(s // t,),
        in_specs=[_rows(t, d)] * 3, out_specs=_rows(t, d), out_shape=jax.ShapeDtypeStruct((s, d), F32),
        compiler_params=_params(("parallel",), 12 * _nbytes((t, d), F32)),
    )(x, lg, pe)


def _ple_bwd(dx, lg, pe, *, name):
    s, d = dx.shape
    t = _tile(s, 512)

    def body(dx_ref, lg_ref, pe_ref, dpe_ref, dlg_ref):
        dxv = dx_ref[...]
        sg = _sigmoid(lg_ref[...])
        dpe_ref[...] = (dxv * sg).astype(dpe_ref.dtype)
        dlg_ref[...] = (dxv * pe_ref[...] * sg * (1.0 - sg)).astype(dlg_ref.dtype)

    return pl.pallas_call(
        body, name=name, grid=(s // t,),
        in_specs=[_rows(t, d)] * 3, out_specs=[_rows(t, d)] * 2,
        out_shape=[jax.ShapeDtypeStruct((s, d), MXU_DTYPE)] * 2,
        compiler_params=_params(("parallel",), 14 * _nbytes((t, d), F32)),
    )(dx, lg, pe)


def _adamw(parts, w, m, v, *, name):
    rows, lanes = w.shape
    t = math.gcd(rows, 160)
    assert rows % t == 0 and t % V7X_SUBLANES == 0
    c1 = 1.0 / (1.0 - ADAM_B1 ** ADAM_STEP)
    c2 = 1.0 / (1.0 - ADAM_B2 ** ADAM_STEP)

    def body(p_ref, w_ref, m_ref, v_ref, g_ref, d_ref, nm_ref, nv_ref):
        g = p_ref[0].astype(F32)
        for j in range(1, N_DEV):
            g = g + p_ref[j].astype(F32)
        m2 = ADAM_B1 * m_ref[...] + (1.0 - ADAM_B1) * g
        v2 = ADAM_B2 * v_ref[...] + (1.0 - ADAM_B2) * (g * g)
        g_ref[...] = g
        nm_ref[...] = m2
        nv_ref[...] = v2
        d_ref[...] = -ADAM_LR * ((m2 * c1) / (jnp.sqrt(v2 * c2) + ADAM_EPS) + ADAM_WD * w_ref[...])

    blk = _rows(t, lanes)
    return pl.pallas_call(
        body, name=name, grid=(rows // t,),
        in_specs=[pl.BlockSpec((N_DEV, t, lanes), lambda i: (0, i, 0)), blk, blk, blk], out_specs=[blk] * 4,
        out_shape=[jax.ShapeDtypeStruct((rows, lanes), F32)] * 4,
        compiler_params=_params(("parallel",), 40 * _nbytes((t, lanes), F32)),
    )(parts, w, m, v)


def _mesh_pos():
    return lax.axis_index("x"), lax.axis_index("y"), lax.axis_index("c")


def _all_gather(blk, *, name):
    r, c_dim = blk.shape

    def body(x_ref, out_ref, send_sems, recv_sems, local_sem):
        x, y, c = _mesh_pos()
        me, sibling = (x, y, c), (x, y, 1 - c)
        chips = [(1 - x, y), (x, 1 - y), (1 - x, 1 - y)]

        def slot(px, py, pc):
            return out_ref.at[4 * px + 2 * py + pc]

        def copy(k, block, to, src=None):
            return pltpu.make_async_remote_copy(
                src_ref=slot(*block) if src is None else src, dst_ref=slot(*block),
                send_sem=send_sems.at[k], recv_sem=recv_sems.at[k],
                device_id=to, device_id_type=pl.DeviceIdType.MESH)

        mine = pltpu.make_async_copy(x_ref, slot(*me), local_sem)
        mine.start()
        first = [copy(0, me, sibling, src=x_ref)]
        first += [copy(1 + j, me, (*chip, c), src=x_ref) for j, chip in enumerate(chips)]
        for cp in first:
            cp.start()
        passed = [copy(4 + j, (*chip, c), sibling) for j, chip in enumerate(chips)]
        for j, chip in enumerate(chips):
            copy(1 + j, (*chip, c), me).wait_recv()
            passed[j].start()
        copy(0, sibling, me).wait_recv()
        for j, chip in enumerate(chips):
            copy(4 + j, (*chip, 1 - c), me).wait_recv()
        for cp in first + passed:
            cp.wait_send()
        mine.wait()

    return pl.pallas_call(
        body, name=name,
        out_shape=jax.ShapeDtypeStruct((N_DEV, r, c_dim), blk.dtype),
        in_specs=[pl.BlockSpec(memory_space=pl.ANY)], out_specs=pl.BlockSpec(memory_space=pl.ANY),
        scratch_shapes=[pltpu.SemaphoreType.DMA((7,)), pltpu.SemaphoreType.DMA((7,)), pltpu.SemaphoreType.DMA],
    )(blk)


def _all_to_all(pay, *, name):
    _, r, c_dim = pay.shape

    def body(in_ref, out_ref, send_sems, recv_sems, local_sem):
        x, y, c = _mesh_pos()
        me = 4 * x + 2 * y + c
        local = pltpu.make_async_copy(in_ref.at[me], out_ref.at[me], local_sem)
        local.start()
        copies = []
        for k in range(1, N_DEV):
            px = 1 - x if k & 4 else x
            py = 1 - y if k & 2 else y
            pc = 1 - c if k & 1 else c
            copies.append(pltpu.make_async_remote_copy(
                src_ref=in_ref.at[4 * px + 2 * py + pc], dst_ref=out_ref.at[me],
                send_sem=send_sems.at[k - 1], recv_sem=recv_sems.at[k - 1],
                device_id=(px, py, pc), device_id_type=pl.DeviceIdType.MESH))
        for cp in copies:
            cp.start()
        for cp in copies:
            cp.wait()
        local.wait()

    return pl.pallas_call(
        body, name=name,
        out_shape=jax.ShapeDtypeStruct((N_DEV, r, c_dim), pay.dtype),
        in_specs=[pl.BlockSpec(memory_space=pl.ANY)], out_specs=pl.BlockSpec(memory_space=pl.ANY),
        scratch_shapes=[pltpu.SemaphoreType.DMA((7,)), pltpu.SemaphoreType.DMA((7,)), pltpu.SemaphoreType.DMA],
    )(pay)


def _all_gather_many(blocks, *, name):
    n = len(blocks)

    def body(*refs):
        x_refs, out_refs = refs[:n], refs[n:2 * n]
        send_sems, recv_sems, local_sems = refs[2 * n:]
        x, y, c = _mesh_pos()
        me, sibling = (x, y, c), (x, y, 1 - c)
        chips = [(1 - x, y), (x, 1 - y), (1 - x, 1 - y)]

        def slot(a, px, py, pc):
            return out_refs[a].at[4 * px + 2 * py + pc]

        def copy(k, a, block, to, src=None):
            return pltpu.make_async_remote_copy(
                src_ref=slot(a, *block) if src is None else src, dst_ref=slot(a, *block),
                send_sem=send_sems.at[k, a], recv_sem=recv_sems.at[k, a],
                device_id=to, device_id_type=pl.DeviceIdType.MESH)

        mine = [pltpu.make_async_copy(x_refs[a], slot(a, *me), local_sems.at[a]) for a in range(n)]
        for cp in mine:
            cp.start()
        first = [copy(0, a, me, sibling, src=x_refs[a]) for a in range(n)]
        first += [copy(1 + j, a, me, (*chip, c), src=x_refs[a]) for j, chip in enumerate(chips) for a in range(n)]
        for cp in first:
            cp.start()
        passed = []
        for j, chip in enumerate(chips):
            for a in range(n):
                copy(1 + j, a, (*chip, c), me).wait_recv()
                fwd = copy(4 + j, a, (*chip, c), sibling)
                fwd.start()
                passed.append(fwd)
        for a in range(n):
            copy(0, a, sibling, me).wait_recv()
        for j, chip in enumerate(chips):
            for a in range(n):
                copy(4 + j, a, (*chip, 1 - c), me).wait_recv()
        for cp in first + passed:
            cp.wait_send()
        for cp in mine:
            cp.wait()

    any_spec = pl.BlockSpec(memory_space=pl.ANY)
    return pl.pallas_call(
        body, name=name,
        out_shape=[jax.ShapeDtypeStruct((N_DEV,) + b.shape, b.dtype) for b in blocks],
        in_specs=[any_spec] * n, out_specs=[any_spec] * n,
        scratch_shapes=[pltpu.SemaphoreType.DMA((7, n)), pltpu.SemaphoreType.DMA((7, n)),
                        pltpu.SemaphoreType.DMA((n,))],
    )(*blocks)


def _all_to_all_many(pays, *, name):
    n = len(pays)

    def body(*refs):
        in_refs, out_refs = refs[:n], refs[n:2 * n]
        send_sems, recv_sems, local_sems = refs[2 * n:]
        x, y, c = _mesh_pos()
        me = 4 * x + 2 * y + c
        local = [pltpu.make_async_copy(in_refs[a].at[me], out_refs[a].at[me], local_sems.at[a]) for a in range(n)]
        for cp in local:
            cp.start()
        copies = []
        for k in range(1, N_DEV):
            px = 1 - x if k & 4 else x
            py = 1 - y if k & 2 else y
            pc = 1 - c if k & 1 else c
            for a in range(n):
                copies.append(pltpu.make_async_remote_copy(
                    src_ref=in_refs[a].at[4 * px + 2 * py + pc], dst_ref=out_refs[a].at[me],
                    send_sem=send_sems.at[k - 1, a], recv_sem=recv_sems.at[k - 1, a],
                    device_id=(px, py, pc), device_id_type=pl.DeviceIdType.MESH))
        for cp in copies:
            cp.start()
        for cp in copies:
            cp.wait()
        for cp in local:
            cp.wait()

    any_spec = pl.BlockSpec(memory_space=pl.ANY)
    return pl.pallas_call(
        body, name=name,
        out_shape=[jax.ShapeDtypeStruct(p.shape, p.dtype) for p in pays],
        in_specs=[any_spec] * n, out_specs=[any_spec] * n,
        scratch_shapes=[pltpu.SemaphoreType.DMA((7, n)), pltpu.SemaphoreType.DMA((7, n)),
                        pltpu.SemaphoreType.DMA((n,))],
    )(*pays)


def _adamw_nd(parts, w, m, v, *, name):
    d0, rows, cols = w.shape
    t = rows
    for cand in range(V7X_SUBLANES, min(rows, 256) + 1, V7X_SUBLANES):
        if rows % cand == 0:
            t = cand
    c1 = 1.0 / (1.0 - ADAM_B1 ** ADAM_STEP)
    c2 = 1.0 / (1.0 - ADAM_B2 ** ADAM_STEP)

    def body(p_ref, w_ref, m_ref, v_ref, g_ref, d_ref, nm_ref, nv_ref):
        g = p_ref[0, 0].astype(F32)
        for j in range(1, N_DEV):
            g = g + p_ref[j, 0].astype(F32)
        m2 = ADAM_B1 * m_ref[0] + (1.0 - ADAM_B1) * g
        v2 = ADAM_B2 * v_ref[0] + (1.0 - ADAM_B2) * (g * g)
        g_ref[0] = g
        nm_ref[0] = m2
        nv_ref[0] = v2
        d_ref[0] = -ADAM_LR * ((m2 * c1) / (jnp.sqrt(v2 * c2) + ADAM_EPS) + ADAM_WD * w_ref[0])

    blk = pl.BlockSpec((1, t, cols), lambda l, i: (l, i, 0))
    lanes = -(-cols // V7X_LANES) * V7X_LANES
    return pl.pallas_call(
        body, name=name, grid=(d0, rows // t),
        in_specs=[pl.BlockSpec((N_DEV, 1, t, cols), lambda l, i: (0, l, i, 0)), blk, blk, blk], out_specs=[blk] * 4,
        out_shape=[jax.ShapeDtypeStruct(w.shape, F32)] * 4,
        compiler_params=_params(("parallel", "parallel"), 40 * _nbytes((max(t, 16), lanes), F32)),
    )(parts, w, m, v)


def _flat_rows(parts, row_multiple):
    flat = jnp.concatenate([p.reshape(-1) for p in parts])
    chunk = PAYLOAD_LANES * row_multiple
    total = -(-flat.shape[0] // chunk) * chunk
    return jnp.pad(flat, (0, total - flat.shape[0])).reshape(total // PAYLOAD_LANES, PAYLOAD_LANES)


def _split_flat(flat, shapes):
    out, off = [], 0
    flat = flat.reshape(-1)
    for shp in shapes:
        n = math.prod(shp)
        out.append(flat[off:off + n].reshape(shp))
        off += n
    return out


def _pad_w_in(w):
    pieces, cursor = [], 0
    for _, off, width, pad_off, _ in SEGS:
        if pad_off > cursor:
            pieces.append(jnp.zeros(w.shape[:-1] + (pad_off - cursor,), w.dtype))
        pieces.append(w[..., off:off + width])
        cursor = pad_off + width
    pieces.append(jnp.zeros(w.shape[:-1] + (D_IN_PAD - cursor,), w.dtype))
    return jnp.concatenate(pieces, axis=-1)


def _unpad_w_in(w):
    return jnp.concatenate([w[..., pad_off:pad_off + width] for _, _, width, pad_off, _ in SEGS], axis=-1)


def _heads(a, hd):
    return a.reshape(a.shape[0], HEADS, hd).transpose(1, 0, 2)


def _unheads(a):
    return a.transpose(1, 0, 2).reshape(a.shape[1], -1)


def _block_diag(w):
    eye = jnp.eye(LRU_HEADS, dtype=w.dtype)
    return (eye[:, None, :, None] * w[:, :, None, :]).reshape(LRU_WIDTH, LRU_WIDTH)


def _diag_blocks(w):
    w4 = w.reshape(LRU_HEADS, LRU_HEAD_DIM, LRU_HEADS, LRU_HEAD_DIM)
    return jnp.stack([w4[h, :, h, :] for h in range(LRU_HEADS)])


def _lane_pad(a, width):
    return jnp.pad(a, ((0, 0), (0, width - a.shape[-1])))


def _layer_fwd(x, p_i, wts, tabs, tag):
    n = functools.partial(lambda base, t=tag: f"{base}_{t}")
    sv = {"x": x}
    n1 = _rms_fwd(x, wts["mix_norm"], width=D_MODEL, name=n("mix_norm_fwd"))
    z, z16 = _mm(n1, wts["w_in"], also_mxu=True, name=n("w_in_fwd"))
    sv.update(n1=n1, z=z, z16=z16)
    lanes = V7X_LANES

    ya_pre, hseq = _lru_fwd(z, wts["conv_w"], wts["conv_b"], wts["lru_wa"], wts["lru_ba"], wts["lru_wx"],
                            wts["lru_bx"], wts["lru_lambda"], name=n("lru_fwd"))
    ya = _mm(ya_pre, wts["w_br_a"], name=n("br_a_fwd"))
    sv.update(ya_pre=ya_pre, hseq=hseq, ya=ya)

    cqn = _rms_fwd(z, wts["mla_q_norm"], width=MLA_Q_LORA, cb=SEG["cq"][3] // MLA_Q_LORA, name=n("q_norm_fwd"))
    ckvn = _rms_fwd(z, wts["mla_kv_norm"], width=MLA_KV_LORA, cb=SEG["ckv"][3] // MLA_KV_LORA,
                    name=n("kv_norm_fwd"))
    qp, qp16 = _mm(cqn, wts["mla_wuq"], also_mxu=True, name=n("wuq_fwd"))
    kv = _mm(ckvn, wts["mla_wukv"], out_dtype=MXU_DTYPE, name=n("wukv_fwd"))
    q_rot = _rope_lanes(qp, tabs["cos128"], tabs["sin128"], cb=0, groups=HEADS, out_dtype=MXU_DTYPE,
                        name=n("q_rope_fwd"))
    k_rot = _rope_lanes(z, tabs["cos128"], tabs["sin128"], cb=SEG["kr"][3] // lanes, groups=1, out_dtype=MXU_DTYPE,
                        name=n("k_rope_fwd"))
    mla_ops = (qp16, HEADS, kv, 0, kv, HEAD_PAIRS)
    ob_flat, lse_b = _pair_fwd(*mla_ops, (q_rot, k_rot), None, scale=(MLA_NOPE + MLA_ROPE) ** -0.5, unit=CHUNK,
                               name=n("mla_attn_fwd"))
    yb = _mm(ob_flat, wts["w_br_b"], name=n("br_b_fwd"))
    sv.update(cqn=cqn, ckvn=ckvn, mla_ops=mla_ops, mla_rot=(q_rot, k_rot), lse_b=lse_b, ob_flat=ob_flat, yb=yb)

    cum = _fox_cum(z, wts["fox_bf"], name=n("fox_cum_fwd"))
    cum_h = cum[:, :HEADS].T
    fox_decay = (cum_h[:, :, None], cum_h[:, None, :])
    fox_ops = (z16, SEG["fq"][3] // lanes, z16, SEG["fk"][3] // lanes, z16, SEG["fv"][3] // lanes)
    oc_flat, lse_c = _pair_fwd(*fox_ops, None, fox_decay, scale=FOX_HEAD_DIM ** -0.5, unit=1, name=n("fox_attn_fwd"))
    yc = _mm(oc_flat, wts["w_br_c"], name=n("br_c_fwd"))
    sv.update(fox_ops=fox_ops, fox_decay=fox_decay, lse_c=lse_c, oc_flat=oc_flat, yc=yc)

    merged = _merge_fwd(ya, yb, yc, z, wts["gate_b"], name=n("merge_fwd"))
    x1 = _mm(merged, wts["w_o"], res=x, name=n("w_o_fwd"))
    n2 = _rms_fwd(x1, wts["ffn_norm"], width=D_MODEL, name=n("ffn_norm_fwd"))
    hf = _mm(n2, wts["w_gate_up"], name=n("gate_up_fwd"))
    act = _swiglu_fwd(hf, name=n("swiglu_fwd"))
    x2 = _mm(act, wts["w_down"], res=x1, name=n("down_fwd"))
    n3 = _rms_fwd(x2, wts["ple_norm"], width=D_MODEL, name=n("ple_norm_fwd"))
    lg = _mm(n3, wts["w_ple_gate"], name=n("ple_gate_fwd"))
    pe = _mm(p_i, wts["w_ple"], name=n("ple_fwd_mm"))
    x3 = _ple_fwd(x2, lg, pe, name=n("ple_fwd"))
    sv.update(merged=merged, x1=x1, n2=n2, hf=hf, act=act, x2=x2, n3=n3, lg=lg, pe=pe, p_i=p_i)
    return x3, sv


def _layer_bwd(dx3, sv, wts, tabs, tag):
    n = functools.partial(lambda base, t=tag: f"{base}_{t}")
    gr = {}
    z = sv["z"]
    s = z.shape[0]

    dpe, dlg = _ple_bwd(dx3, sv["lg"], sv["pe"], name=n("ple_bwd"))
    gr["w_ple"] = _mm(sv["p_i"], dpe, ta=True, name=n("ple_dw"))
    gr["w_ple_gate"] = _mm(sv["n3"], dlg, ta=True, name=n("ple_gate_dw"))
    dn3 = _mm(dlg, wts["w_ple_gate"], tb=True, name=n("ple_gate_dx"))
    dx2, gr["ple_norm"] = _rms_bwd(sv["x2"], wts["ple_norm"], dn3, width=D_MODEL, res=dx3, name=n("ple_norm_bwd"))

    dact = _mm(dx2, wts["w_down"], tb=True, name=n("down_dx"))
    gr["w_down"] = _mm(sv["act"], dx2, ta=True, name=n("down_dw"))
    dhf = _swiglu_bwd(sv["hf"], dact, name=n("swiglu_bwd"))
    gr["w_gate_up"] = _mm(sv["n2"], dhf, ta=True, name=n("gate_up_dw"))
    dn2 = _mm(dhf, wts["w_gate_up"], tb=True, name=n("gate_up_dx"))
    dx1, gr["ffn_norm"] = _rms_bwd(sv["x1"], wts["ffn_norm"], dn2, width=D_MODEL, res=dx2, name=n("ffn_norm_bwd"))

    dmerged = _mm(dx1, wts["w_o"], tb=True, name=n("w_o_dx"))
    gr["w_o"] = _mm(sv["merged"], dx1, ta=True, name=n("w_o_dw"))
    dya, dyb, dyc, dgl, gr["gate_b"] = _merge_bwd(dmerged, sv["ya"], sv["yb"], sv["yc"], z, wts["gate_b"],
                                                  name=n("merge_bwd"))
    gr["w_br_a"] = _mm(sv["ya_pre"], dya, ta=True, name=n("br_a_dw"))
    gr["w_br_b"] = _mm(sv["ob_flat"], dyb, ta=True, name=n("br_b_dw"))
    gr["w_br_c"] = _mm(sv["oc_flat"], dyc, ta=True, name=n("br_c_dw"))
    dya_pre = _mm(dya, wts["w_br_a"], tb=True, name=n("br_a_dx"))
    dob = _mm(dyb, wts["w_br_b"], tb=True, out_dtype=MXU_DTYPE, name=n("br_b_dx"))
    doc = _mm(dyc, wts["w_br_c"], tb=True, out_dtype=MXU_DTYPE, name=n("br_c_dx"))

    (dz_a, gr["conv_w"], gr["conv_b"], dwa, gr["lru_ba"], dwx, gr["lru_bx"], gr["lru_lambda"]) = _lru_bwd(
        z, sv["hseq"], dya_pre, wts["conv_w"], wts["conv_b"], wts["lru_wa"], wts["lru_ba"], wts["lru_wx"],
        wts["lru_bx"], wts["lru_lambda"], name=n("lru_bwd"))
    gr["lru_wa"], gr["lru_wx"] = _diag_blocks(dwa), _diag_blocks(dwx)

    scale_b = (MLA_NOPE + MLA_ROPE) ** -0.5
    delta_b = _pair_delta(*sv["mla_ops"], dob, sv["lse_b"], sv["mla_rot"], None,
                          scale=scale_b, unit=CHUNK, name=n("mla_attn_delta"))
    dk_nope, dv_mla, dq_nope, dk_rot, dq_rot = _pair_bwd_kv(
        *sv["mla_ops"], dob, sv["lse_b"].reshape(HEADS, 1, s), delta_b.reshape(HEADS, 1, s), sv["mla_rot"], None,
        scale=scale_b, unit=CHUNK, name=n("mla_attn_bwd"))
    dq_rope = _rope_lanes(dq_rot, tabs["cos128"], -tabs["sin128"], cb=0, groups=HEADS, out_dtype=MXU_DTYPE,
                          name=n("q_rope_bwd"))
    dk_rope = _rope_lanes(dk_rot, tabs["cos128"], -tabs["sin128"], cb=0, groups=1, out_dtype=MXU_DTYPE,
                          sum_parts=HEAD_PAIRS, name=n("k_rope_bwd"))
    dqp = jnp.concatenate([dq_rope, dq_nope], axis=-1)
    dkv = jnp.concatenate([dk_nope, dv_mla], axis=-1)
    gr["mla_wuq"] = _mm(sv["cqn"], dqp, ta=True, name=n("wuq_dw"))
    gr["mla_wukv"] = _mm(sv["ckvn"], dkv, ta=True, name=n("wukv_dw"))
    dcqn = _mm(dqp, wts["mla_wuq"], tb=True, name=n("wuq_dx"))
    dckvn = _mm(dkv, wts["mla_wukv"], tb=True, name=n("wukv_dx"))
    dcq, gr["mla_q_norm"] = _rms_bwd(z, wts["mla_q_norm"], dcqn, width=MLA_Q_LORA, cb=SEG["cq"][3] // MLA_Q_LORA,
                                     out_dtype=MXU_DTYPE, name=n("q_norm_bwd"))
    dckv, gr["mla_kv_norm"] = _rms_bwd(z, wts["mla_kv_norm"], dckvn, width=MLA_KV_LORA,
                                       cb=SEG["ckv"][3] // MLA_KV_LORA, out_dtype=MXU_DTYPE, name=n("kv_norm_bwd"))

    scale_c = FOX_HEAD_DIM ** -0.5
    delta_c = _pair_delta(*sv["fox_ops"], doc, sv["lse_c"], None, sv["fox_decay"],
                          scale=scale_c, unit=1, name=n("fox_attn_delta"))
    dfk, dfv, dfq, dcum = _pair_bwd_kv(*sv["fox_ops"], doc, sv["lse_c"].reshape(HEADS, 1, s),
                                       delta_c.reshape(HEADS, 1, s), None, sv["fox_decay"],
                                       scale=scale_c, unit=1, name=n("fox_attn_bwd"))
    dcum_rows = _lane_pad(dcum.reshape(HEADS, s).T, V7X_LANES)
    dfl, dbf = _fox_cum_bwd(z, wts["fox_bf"], dcum_rows, name=n("fox_cum_bwd"))
    gr["fox_bf"] = dbf[:, :HEADS]

    zero = lambda width: jnp.zeros((s, width), MXU_DTYPE)
    dz = jnp.concatenate([dz_a, zero(128), dcq, dckv, dk_rope, zero(128), dfq, dfk, dfv, dfl, zero(384), dgl],
                         axis=-1)
    gr["w_in"] = _mm(sv["n1"], dz, ta=True, name=n("w_in_dw"))
    dn1 = _mm(dz, wts["w_in"], tb=True, name=n("w_in_dx"))
    dx, gr["mix_norm"] = _rms_bwd(sv["x"], wts["mix_norm"], dn1, width=D_MODEL, res=dx1, name=n("mix_norm_bwd"))
    return dx, gr


def _rope_tables(s):
    pos = jnp.arange(s, dtype=F32)
    inv_freq = ROPE_BASE ** (-jnp.arange(0, MLA_ROPE, 2, dtype=F32) / MLA_ROPE)
    ang = pos[:, None] * inv_freq[None, :]
    cos, sin = jnp.cos(ang), jnp.sin(ang)
    cos32 = jnp.concatenate([cos, cos], axis=-1)
    sin32 = jnp.concatenate([-sin, sin], axis=-1)
    return {"cos256": jnp.tile(cos32, (1, 8)), "sin256": jnp.tile(sin32, (1, 8)),
            "cos128": jnp.tile(cos32, (1, 4)), "sin128": jnp.tile(sin32, (1, 4))}


def _gather_weights(shards):
    names = [nm for nm, _ in SHARDED]
    got = _all_gather_many([shards[nm] if nm == "conv_w" else shards[nm].astype(MXU_DTYPE) for nm in names],
                           name="weights_all_gather")
    full = {}
    for (nm, axis), blk in zip(SHARDED, got):
        shp = shards[nm].shape
        if axis == 2:
            full[nm] = blk.transpose(1, 2, 0, 3).reshape(shp[0], shp[1], N_DEV * shp[2])
        else:
            full[nm] = blk.transpose(1, 0, 2, 3).reshape(shp[0], N_DEV * shp[1], shp[2])
    return full


def _to_dest_major(g, axis):
    d0, r, c = g.shape
    if axis == 2:
        return g.reshape(d0, r, N_DEV, c // N_DEV).transpose(2, 0, 1, 3)
    return g.reshape(d0, N_DEV, r // N_DEV, c).transpose(1, 0, 2, 3)


def kernel(x, p, mix_norm, w_in, gate_b, conv_w, conv_b, lru_wa, lru_ba, lru_wx, lru_bx, lru_lambda, mla_q_norm, mla_wuq, mla_kv_norm, mla_wukv, fox_bf, w_br_a, w_br_b, w_br_c, w_o, ffn_norm, w_gate_up, w_down, ple_norm, w_ple_gate, w_ple, final_norm, loss_target, m_mix_norm, m_w_in, m_gate_b, m_conv_w, m_conv_b, m_lru_wa, m_lru_ba, m_lru_wx, m_lru_bx, m_lru_lambda, m_mla_q_norm, m_mla_wuq, m_mla_kv_norm, m_mla_wukv, m_fox_bf, m_w_br_a, m_w_br_b, m_w_br_c, m_w_o, m_ffn_norm, m_w_gate_up, m_w_down, m_ple_norm, m_w_ple_gate, m_w_ple, m_final_norm, v_mix_norm, v_w_in, v_gate_b, v_conv_w, v_conv_b, v_lru_wa, v_lru_ba, v_lru_wx, v_lru_bx, v_lru_lambda, v_mla_q_norm, v_mla_wuq, v_mla_kv_norm, v_mla_wukv, v_fox_bf, v_w_br_a, v_w_br_b, v_w_br_c, v_w_o, v_ffn_norm, v_w_gate_up, v_w_down, v_ple_norm, v_w_ple_gate, v_w_ple, v_final_norm):
    given = dict(locals())
    w_loc = {nm: given[nm] for nm in WEIGHTS}
    m_loc = {nm: given["m_" + nm] for nm in WEIGHTS}
    v_loc = {nm: given["v_" + nm] for nm in WEIGHTS}
    xs = x[0]
    s = xs.shape[0]
    tabs = _rope_tables(s)

    full = _gather_weights({nm: w_loc[nm] for nm, _ in SHARDED})
    full["w_in"] = _pad_w_in(full["w_in"])
    wq = full["mla_wuq"].reshape(DEPTH, MLA_Q_LORA, HEADS, MLA_NOPE + MLA_ROPE)
    wq_rot = jnp.pad(wq[..., MLA_NOPE:], ((0, 0), (0, 0), (0, 0), (0, V7X_LANES - MLA_ROPE)))
    full["mla_wuq"] = jnp.concatenate([wq_rot.reshape(DEPTH, MLA_Q_LORA, -1),
                                       wq[..., :MLA_NOPE].reshape(DEPTH, MLA_Q_LORA, -1)], axis=-1)
    wkv = full["mla_wukv"].reshape(DEPTH, MLA_KV_LORA, HEADS, MLA_NOPE + MLA_V)
    full["mla_wukv"] = jnp.concatenate([wkv[..., :MLA_NOPE].reshape(DEPTH, MLA_KV_LORA, -1),
                                        wkv[..., MLA_NOPE:].reshape(DEPTH, MLA_KV_LORA, -1)], axis=-1)

    def layer_weights(i):
        wts = {nm: full[nm][i] for nm, _ in SHARDED}
        for nm in ("mix_norm", "gate_b", "conv_b", "lru_ba", "lru_bx", "lru_lambda", "mla_q_norm", "mla_kv_norm",
                   "ffn_norm", "ple_norm"):
            wts[nm] = w_loc[nm][i][None, :]
        wts["fox_bf"] = _lane_pad(w_loc["fox_bf"][i][None, :], V7X_LANES)
        wts["lru_wa"] = _block_diag(w_loc["lru_wa"][i]).astype(MXU_DTYPE)
        wts["lru_wx"] = _block_diag(w_loc["lru_wx"][i]).astype(MXU_DTYPE)
        return wts

    layers = [layer_weights(i) for i in range(DEPTH)]

    h = xs
    saved = []
    for i in range(DEPTH):
        h, sv = _layer_fwd(h, p[i, 0].astype(MXU_DTYPE), layers[i], tabs, f"l{i}")
        saved.append(sv)
    loss_blk, dh, dg_final = _final_loss(h, w_loc["final_norm"][None, :], loss_target[0], name="final_loss")
    loss = lax.psum(loss_blk[0, 0], ("x", "y", "c"))

    grads = [None] * DEPTH
    for i in reversed(range(DEPTH)):
        dh, grads[i] = _layer_bwd(dh, saved[i], layers[i], tabs, f"l{i}")
    grad_x = dh[None]

    def stacked(nm):
        return jnp.stack([grads[i][nm] for i in range(DEPTH)])

    gfull = {}
    for nm, _ in SHARDED:
        gfull[nm] = stacked(nm)
    gfull["w_in"] = _unpad_w_in(gfull["w_in"])
    gq = gfull["mla_wuq"]
    rot_w = HEADS * V7X_LANES
    gfull["mla_wuq"] = jnp.concatenate(
        [gq[..., rot_w:].reshape(DEPTH, MLA_Q_LORA, HEADS, MLA_NOPE),
         gq[..., :rot_w].reshape(DEPTH, MLA_Q_LORA, HEADS, V7X_LANES)[..., :MLA_ROPE]],
        axis=-1).reshape(DEPTH, MLA_Q_LORA, -1)
    gkv = gfull["mla_wukv"]
    gfull["mla_wukv"] = jnp.concatenate(
        [gkv[..., :512].reshape(DEPTH, MLA_KV_LORA, HEADS, MLA_NOPE),
         gkv[..., 512:].reshape(DEPTH, MLA_KV_LORA, HEADS, MLA_V)], axis=-1).reshape(DEPTH, MLA_KV_LORA, -1)

    parts = _all_to_all_many([_to_dest_major(gfull[nm], ax).astype(MXU_DTYPE) for nm, ax in SHARDED],
                             name="grads_all_to_all")
    res_s = [{}, {}, {}, {}]
    for (nm, _), part in zip(SHARDED, parts):
        outs = _adamw_nd(part, w_loc[nm], m_loc[nm], v_loc[nm], name=f"adamw_{nm}")
        for kind in range(4):
            res_s[kind][nm] = outs[kind]

    small = {nm: stacked(nm) for nm in REPLICATED if nm != "final_norm"}
    small["final_norm"] = dg_final
    names_r = list(REPLICATED)
    shapes_r = [w_loc[nm].shape for nm in names_r]
    parts_r = _all_gather(_flat_rows([small[nm] for nm in names_r], 8), name="small_grads_all_gather")
    outs_r = _adamw(parts_r, _flat_rows([w_loc[nm] for nm in names_r], 8),
                    _flat_rows([m_loc[nm] for nm in names_r], 8),
                    _flat_rows([v_loc[nm] for nm in names_r], 8), name="adamw_replicated")
    res_r = [dict(zip(names_r, _split_flat(o, shapes_r))) for o in outs_r]

    out = [loss, grad_x]
    for kind in range(4):
        for nm in WEIGHTS:
            out.append(res_s[kind][nm] if nm in res_s[kind] else res_r[kind][nm])
    return tuple(out)
```

```python
import functools
import math

import jax
import jax.numpy as jnp
from jax import lax
from jax.experimental import pallas as pl
from jax.experimental.pallas import tpu as pltpu

F32 = jnp.float32
BF16 = jnp.bfloat16
MXU_DTYPE = jnp.bfloat16

D_MODEL = 1024
DEPTH = 2
CHUNK = 64
EPS = 1e-6
NEG_INF = -1e30
LRU_WIDTH = 512
LRU_HEADS = 8
LRU_HEAD_DIM = 64
CONV_WIDTH = 4
LRU_C = 8.0
HEADS = 8
MLA_Q_LORA = 384
MLA_KV_LORA = 256
MLA_NOPE = 64
MLA_ROPE = 32
MLA_V = 64
ROPE_BASE = 10000.0
FOX_HEAD_DIM = 64
FOX_WIDTH = 512
D_FF = 2816
PLE_DIM = 256
D_IN = 6312
ADAM_LR = 0.001
ADAM_B1 = 0.9
ADAM_B2 = 0.999
ADAM_EPS = 1e-08
ADAM_WD = 0.01
ADAM_STEP = 10

V7X_VMEM_BYTES = 64 * 1024 * 1024
V7X_LANES = 128
V7X_SUBLANES = 8
VMEM_LIMIT_CAP = 56 * 1024 * 1024
N_DEV = 8

SEGS = (
    ("u", 0, 512, 0, 512),
    ("ug", 512, 512, 512, 512),
    ("cq", 1024, 384, 1152, 384),
    ("ckv", 1408, 256, 1536, 256),
    ("kr", 1664, 32, 1792, 128),
    ("fq", 1696, 512, 2048, 512),
    ("fk", 2208, 512, 2560, 512),
    ("fv", 2720, 512, 3072, 512),
    ("fl", 3232, 8, 3584, 128),
    ("gate", 3240, 3072, 4096, 1024),
)
D_IN_PAD = 7168
SEG = {s[0]: s for s in SEGS}

SHARDED = (("w_in", 2), ("mla_wuq", 2), ("mla_wukv", 2), ("w_br_a", 2), ("w_br_b", 2), ("w_br_c", 2),
           ("w_o", 1), ("w_gate_up", 2), ("w_down", 1), ("w_ple_gate", 1), ("w_ple", 2), ("conv_w", 2))
REPLICATED = ("mix_norm", "gate_b", "conv_b", "lru_wa", "lru_ba", "lru_wx", "lru_bx", "lru_lambda",
              "mla_q_norm", "mla_kv_norm", "fox_bf", "ffn_norm", "ple_norm", "final_norm")
WEIGHTS = ("mix_norm", "w_in", "gate_b", "conv_w", "conv_b", "lru_wa", "lru_ba", "lru_wx", "lru_bx",
           "lru_lambda", "mla_q_norm", "mla_wuq", "mla_kv_norm", "mla_wukv", "fox_bf", "w_br_a", "w_br_b",
           "w_br_c", "w_o", "ffn_norm", "w_gate_up", "w_down", "ple_norm", "w_ple_gate", "w_ple", "final_norm")
PAYLOAD_LANES = 1024


def _tile(n, cap=1024):
    best = None
    for t in range(V7X_LANES, min(n, cap) + 1, V7X_LANES):
        if n % t == 0:
            best = t
    return best if best is not None else n


def _row_tile(s, pref):
    t = min(pref, s // 2)
    assert s % t == 0 and t % V7X_SUBLANES == 0
    return t


def _nbytes(shape, dtype):
    return math.prod(shape) * jnp.dtype(dtype).itemsize


def _params(sem, vmem_bytes):
    limit = int(min(VMEM_LIMIT_CAP, max(16 * 1024 * 1024, vmem_bytes)))
    return pltpu.CompilerParams(dimension_semantics=sem, vmem_limit_bytes=limit)


def _full(shape):
    return pl.BlockSpec(shape, lambda *_: (0,) * len(shape))


def _rows(t, w, cb=0):
    return pl.BlockSpec((t, w), lambda i: (i, cb))


def _mxu(v):
    return v.astype(MXU_DTYPE)


def _dot(a, b):
    return lax.dot_general(_mxu(a), _mxu(b), (((1,), (0,)), ((), ())), preferred_element_type=F32)


def _dot_nt(a, b):
    return lax.dot_general(_mxu(a), _mxu(b), (((1,), (1,)), ((), ())), preferred_element_type=F32)


def _dot_tn(a, b):
    return lax.dot_general(_mxu(a), _mxu(b), (((0,), (0,)), ((), ())), preferred_element_type=F32)


def _sigmoid(v):
    return 1.0 / (1.0 + jnp.exp(-v))


def _softplus(v):
    return jnp.maximum(v, 0.0) + jnp.log(1.0 + jnp.exp(-jnp.abs(v)))


def _neg_expm1(v):
    series = -v * (1.0 + v * (0.5 + v * (1.0 / 6.0 + v * (1.0 / 24.0))))
    return jnp.where(v > -0.03, series, 1.0 - jnp.exp(v))


_GELU_C = math.sqrt(2.0 / math.pi)
_GELU_A = 0.044715


def _gelu(v):
    t = jnp.tanh(_GELU_C * (v + _GELU_A * v * v * v))
    return 0.5 * v * (1.0 + t)


def _gelu_grad(v):
    t = jnp.tanh(_GELU_C * (v + _GELU_A * v * v * v))
    return 0.5 * (1.0 + t) + 0.5 * v * (1.0 - t * t) * _GELU_C * (1.0 + 3.0 * _GELU_A * v * v)


def _mm(a, b, *, ta=False, tb=False, out_dtype=F32, res=None, also_mxu=False, name):
    k_dim, m_dim = (a.shape[0], a.shape[1]) if ta else (a.shape[1], a.shape[0])
    n_dim = b.shape[0] if tb else b.shape[1]
    assert (b.shape[1] if tb else b.shape[0]) == k_dim
    tm, tn, tk = _tile(m_dim), _tile(n_dim, 1408), _tile(k_dim, 1408)
    nk = k_dim // tk
    a_spec = pl.BlockSpec((tk, tm), lambda i, j, k: (k, i)) if ta else pl.BlockSpec((tm, tk), lambda i, j, k: (i, k))
    b_spec = pl.BlockSpec((tn, tk), lambda i, j, k: (j, k)) if tb else pl.BlockSpec((tk, tn), lambda i, j, k: (k, j))
    o_spec = pl.BlockSpec((tm, tn), lambda i, j, k: (i, j))
    has_res = res is not None

    def body(*refs):
        a_ref, b_ref = refs[0], refs[1]
        res_ref = refs[2] if has_res else None
        o_ref = refs[3] if has_res else refs[2]
        o2_ref = refs[-2] if also_mxu else None
        acc_ref = refs[-1]
        k = pl.program_id(2)
        if ta:
            part = _dot_tn(a_ref[...], b_ref[...])
        elif tb:
            part = _dot_nt(a_ref[...], b_ref[...])
        else:
            part = _dot(a_ref[...], b_ref[...])

        def finish(total):
            if has_res:
                total = total + res_ref[...].astype(F32)
            o_ref[...] = total.astype(o_ref.dtype)
            if also_mxu:
                o2_ref[...] = total.astype(o2_ref.dtype)

        if nk == 1:
            finish(part)
        else:
            @pl.when(k == 0)
            def _():
                acc_ref[...] = part

            @pl.when(jnp.logical_and(k > 0, k < nk - 1))
            def _():
                acc_ref[...] += part

            @pl.when(k == nk - 1)
            def _():
                finish(acc_ref[...] + part)

    ins = [a, b] + ([res] if has_res else [])
    in_specs = [a_spec, b_spec] + ([o_spec] if has_res else [])
    acc_shape = (tm, tn) if nk > 1 else (V7X_SUBLANES, V7X_LANES)
    vmem = (2 * (_nbytes((tm, tk), a.dtype) + _nbytes((tk, tn), b.dtype) + _nbytes((tm, tn), out_dtype)
                 + (_nbytes((tm, tn), res.dtype) if has_res else 0))
            + _nbytes((tm, tk), MXU_DTYPE) + _nbytes((tk, tn), MXU_DTYPE) + 3 * _nbytes((tm, tn), F32))
    return pl.pallas_call(
        body, name=name, grid=(m_dim // tm, n_dim // tn, nk),
        in_specs=in_specs, out_specs=[o_spec, o_spec] if also_mxu else o_spec,
        out_shape=([jax.ShapeDtypeStruct((m_dim, n_dim), out_dtype), jax.ShapeDtypeStruct((m_dim, n_dim), MXU_DTYPE)]
                   if also_mxu else jax.ShapeDtypeStruct((m_dim, n_dim), out_dtype)),
        scratch_shapes=[pltpu.VMEM(acc_shape, F32)],
        compiler_params=_params(("parallel", "parallel", "arbitrary"), vmem),
    )(*ins)


def _rms_fwd(x, g, *, width, cb=0, name):
    s = x.shape[0]
    t = _tile(s, 512)

    def body(x_ref, g_ref, o_ref):
        xv = x_ref[...].astype(F32)
        r = lax.rsqrt(jnp.mean(xv * xv, axis=-1, keepdims=True) + EPS)
        o_ref[...] = (xv * r * g_ref[...]).astype(o_ref.dtype)

    return pl.pallas_call(
        body, name=name, grid=(s // t,),
        in_specs=[_rows(t, width, cb), _full((1, width))], out_specs=_rows(t, width),
        out_shape=jax.ShapeDtypeStruct((s, width), MXU_DTYPE),
        compiler_params=_params(("parallel",), 8 * _nbytes((t, width), F32)),
    )(x, g)


def _rms_bwd(x, g, dn, *, width, cb=0, res=None, out_dtype=F32, name):
    s = x.shape[0]
    t = _tile(s, 256)
    has_res = res is not None

    def body(*refs):
        x_ref, g_ref, dn_ref = refs[:3]
        res_ref = refs[3] if has_res else None
        dx_ref, dg_ref = refs[-2], refs[-1]
        xv = x_ref[...].astype(F32)
        dnv = dn_ref[...].astype(F32)
        r = lax.rsqrt(jnp.mean(xv * xv, axis=-1, keepdims=True) + EPS)
        xr = xv * r
        dng = dnv * g_ref[...]
        dx = r * dng - xr * (r * r) * jnp.mean(dng * xv, axis=-1, keepdims=True)
        if has_res:
            dx = dx + res_ref[...].astype(F32)
        dx_ref[...] = dx.astype(dx_ref.dtype)
        part = jnp.sum(dnv * xr, axis=0, keepdims=True)

        @pl.when(pl.program_id(0) == 0)
        def _():
            dg_ref[...] = part

        @pl.when(pl.program_id(0) > 0)
        def _():
            dg_ref[...] += part

    ins = [x, g, dn] + ([res] if has_res else [])
    in_specs = [_rows(t, width, cb), _full((1, width)), _rows(t, width)] + ([_rows(t, width)] if has_res else [])
    return pl.pallas_call(
        body, name=name, grid=(s // t,),
        in_specs=in_specs, out_specs=[_rows(t, width), _full((1, width))],
        out_shape=[jax.ShapeDtypeStruct((s, width), out_dtype), jax.ShapeDtypeStruct((1, width), F32)],
        compiler_params=_params(("arbitrary",), 16 * _nbytes((t, width), F32)),
    )(*ins)


def _final_loss(x, g, target, *, name):
    s, d = x.shape
    t = _tile(s, 256)

    def body(x_ref, g_ref, t_ref, loss_ref, dx_ref, dg_ref):
        xv = x_ref[...]
        r = lax.rsqrt(jnp.mean(xv * xv, axis=-1, keepdims=True) + EPS)
        xr = xv * r
        err = xr * g_ref[...] - t_ref[...]
        part_loss = 0.5 * jnp.sum(jnp.mean(err * err, axis=-1, keepdims=True), axis=0, keepdims=True)
        dnv = err * (1.0 / d)
        dng = dnv * g_ref[...]
        dx_ref[...] = r * dng - xr * (r * r) * jnp.mean(dng * xv, axis=-1, keepdims=True)
        part_dg = jnp.sum(dnv * xr, axis=0, keepdims=True)

        @pl.when(pl.program_id(0) == 0)
        def _():
            dg_ref[...] = part_dg
            loss_ref[...] = jnp.zeros(loss_ref.shape, F32) + part_loss

        @pl.when(pl.program_id(0) > 0)
        def _():
            dg_ref[...] += part_dg
            loss_ref[...] += part_loss

    return pl.pallas_call(
        body, name=name, grid=(s // t,),
        in_specs=[_rows(t, d), _full((1, d)), _rows(t, d)],
        out_specs=[_full((V7X_SUBLANES, V7X_LANES)), _rows(t, d), _full((1, d))],
        out_shape=[jax.ShapeDtypeStruct((V7X_SUBLANES, V7X_LANES), F32), jax.ShapeDtypeStruct((s, d), F32),
                   jax.ShapeDtypeStruct((1, d), F32)],
        compiler_params=_params(("arbitrary",), 16 * _nbytes((t, d), F32)),
    )(x, g, target)


def _shift_down(v, d, fill, rows):
    return jnp.where(rows >= d, pltpu.roll(v, d, 0), fill)


def _shift_up(v, d, fill, rows, t):
    return jnp.where(rows < t - d, pltpu.roll(v, t - d, 0), fill)


def _lru_gates(xc, wa_ref, ba_ref, wx_ref, bx_ref, lam_ref):
    ra = _sigmoid(_dot(xc, wa_ref[...]) + ba_ref[...])
    ig = _sigmoid(_dot(xc, wx_ref[...]) + bx_ref[...])
    sp = _softplus(-lam_ref[...])
    log_a = -LRU_C * ra * sp
    a = jnp.exp(log_a)
    s2 = _neg_expm1(2.0 * log_a)
    return ra, ig, sp, a, s2


def _conv(ubuf, cw_ref, cb_ref, t):
    big = ubuf[...]
    shifted = [pltpu.roll(big, CONV_WIDTH - 1 - k, 0)[V7X_SUBLANES:t + V7X_SUBLANES] if k < CONV_WIDTH - 1
               else big[V7X_SUBLANES:t + V7X_SUBLANES] for k in range(CONV_WIDTH)]
    xc = cb_ref[...] + shifted[0] * cw_ref[0:1, :]
    for k in range(1, CONV_WIDTH):
        xc = xc + shifted[k] * cw_ref[k:k + 1, :]
    return xc, shifted


def _lru_fwd(z, cw, cb, wa, ba, wx, bx, lam, *, name):
    s = z.shape[0]
    w = LRU_WIDTH
    t = _row_tile(s, 256)
    steps = [1 << k for k in range(int(math.log2(t)))]

    def body(u_ref, ug_ref, cw_ref, cb_ref, wa_ref, ba_ref, wx_ref, bx_ref, lam_ref, y_ref, h_ref, ubuf, hc):
        @pl.when(pl.program_id(0) == 0)
        def _():
            ubuf[0:V7X_SUBLANES, :] = jnp.zeros((V7X_SUBLANES, w), F32)
            hc[...] = jnp.zeros_like(hc)

        ubuf[V7X_SUBLANES:t + V7X_SUBLANES, :] = u_ref[...]
        xc, _ = _conv(ubuf, cw_ref, cb_ref, t)
        _, ig, _, a, s2 = _lru_gates(xc, wa_ref, ba_ref, wx_ref, bx_ref, lam_ref)
        b = jnp.sqrt(s2) * (ig * xc)
        rows = lax.broadcasted_iota(jnp.int32, (t, w), 0)
        for d in steps:
            b = a * _shift_down(b, d, 0.0, rows) + b
            a = a * _shift_down(a, d, 1.0, rows)
        h = a * hc[0:1, :] + b
        h_ref[...] = h
        y_ref[...] = (h * _gelu(ug_ref[...])).astype(y_ref.dtype)
        hc[0:1, :] = h_ref[t - 1:t, :]
        ubuf[0:V7X_SUBLANES, :] = ubuf[t:t + V7X_SUBLANES, :]

    vec = _full((1, w))
    return pl.pallas_call(
        body, name=name, grid=(s // t,),
        in_specs=[_rows(t, w, 0), _rows(t, w, 1), _full((CONV_WIDTH, w)), vec, _full((w, w)), vec, _full((w, w)),
                  vec, vec],
        out_specs=[_rows(t, w), _rows(t, w)],
        out_shape=[jax.ShapeDtypeStruct((s, w), MXU_DTYPE), jax.ShapeDtypeStruct((s, w), F32)],
        scratch_shapes=[pltpu.VMEM((t + V7X_SUBLANES, w), F32), pltpu.VMEM((V7X_SUBLANES, w), F32)],
        compiler_params=_params(("arbitrary",), 40 * _nbytes((t, w), F32)),
    )(z, z, cw, cb, wa, ba, wx, bx, lam)


def _lru_bwd(z, h, dy, cw, cb, wa, ba, wx, bx, lam, *, name):
    s = z.shape[0]
    w = LRU_WIDTH
    t = _row_tile(s, 256)
    nt = s // t
    per8 = t // V7X_SUBLANES
    steps = [1 << k for k in range(int(math.log2(t)))]

    def body(u_ref, ug_ref, h_ref, dy_ref, uprev_ref, hprev_ref, cw_ref, cb_ref, wa_ref, ba_ref, wx_ref, bx_ref,
             lam_ref, dz_ref, dcw_ref, dcb_ref, dwa_ref, dba_ref, dwx_ref, dbx_ref, dlam_ref,
             ubuf, dbuf, acar, dhcar, tmp):
        i = pl.program_id(0)
        first_tile = i == nt - 1

        @pl.when(i == 0)
        def _():
            for r in (dcw_ref, dcb_ref, dwa_ref, dba_ref, dwx_ref, dbx_ref, dlam_ref, acar, dhcar):
                r[...] = jnp.zeros_like(r)
            dbuf[t:t + V7X_SUBLANES, :] = jnp.zeros((V7X_SUBLANES, w), F32)

        keep = jnp.where(first_tile, 0.0, 1.0)
        ubuf[0:V7X_SUBLANES, :] = uprev_ref[...] * keep
        ubuf[V7X_SUBLANES:t + V7X_SUBLANES, :] = u_ref[...]
        xc, shifted = _conv(ubuf, cw_ref, cb_ref, t)
        ra, ig, sp, a, s2 = _lru_gates(xc, wa_ref, ba_ref, wx_ref, bx_ref, lam_ref)
        sq = jnp.sqrt(s2)
        gx = ig * xc
        rows = lax.broadcasted_iota(jnp.int32, (t, w), 0)
        ugv = ug_ref[...]
        dyv = dy_ref[...].astype(F32)
        hv = h_ref[...]

        acc_g = dyv * _gelu(ugv)
        acc_a = _shift_up(a, 1, acar[0:1, :], rows, t)
        for d in steps:
            acc_g = acc_a * _shift_up(acc_g, d, 0.0, rows, t) + acc_g
            acc_a = acc_a * _shift_up(acc_a, d, 1.0, rows, t)
        dh = acc_a * dhcar[0:1, :] + acc_g

        hprev = _shift_down(hv, 1, hprev_ref[V7X_SUBLANES - 1:V7X_SUBLANES, :] * keep, rows)
        d_a = dh * hprev
        d_sq = dh * gx
        d_gx = dh * sq
        d_ig = d_gx * xc
        dxc = d_gx * ig
        d_log_a = d_a * a - d_sq * (1.0 - s2) / sq
        d_ra = d_log_a * (-LRU_C * sp)
        lamv = lam_ref[...]
        dlam_ref[...] += jnp.sum(d_log_a * (-LRU_C * ra), axis=0, keepdims=True) * (-_sigmoid(-lamv))
        dpa = d_ra * ra * (1.0 - ra)
        dpx = d_ig * ig * (1.0 - ig)
        dba_ref[...] += jnp.sum(dpa, axis=0, keepdims=True)
        dbx_ref[...] += jnp.sum(dpx, axis=0, keepdims=True)
        dwa_ref[...] += _dot_tn(xc, dpa)
        dwx_ref[...] += _dot_tn(xc, dpx)
        dxc = dxc + _dot_nt(dpa, wa_ref[...]) + _dot_nt(dpx, wx_ref[...])
        dcb_ref[...] += jnp.sum(dxc, axis=0, keepdims=True)
        for k in range(CONV_WIDTH):
            dcw_ref[k:k + 1, :] += jnp.sum(dxc * shifted[k], axis=0, keepdims=True)

        dbuf[0:t, :] = dxc
        bigd = dbuf[...]
        du = dxc * cw_ref[CONV_WIDTH - 1:CONV_WIDTH, :]
        for k in range(CONV_WIDTH - 1):
            e = CONV_WIDTH - 1 - k
            du = du + pltpu.roll(bigd, t + V7X_SUBLANES - e, 0)[0:t] * cw_ref[k:k + 1, :]
        dz_ref[:, 0:w] = du.astype(dz_ref.dtype)
        dz_ref[:, w:2 * w] = (dyv * hv * _gelu_grad(ugv)).astype(dz_ref.dtype)

        dbuf[t:t + V7X_SUBLANES, :] = dbuf[0:V7X_SUBLANES, :]
        tmp[...] = a
        acar[0:1, :] = tmp[0:1, :]
        tmp[...] = dh
        dhcar[0:1, :] = tmp[0:1, :]

    vec = _full((1, w))
    rev = lambda cbk: pl.BlockSpec((t, w), lambda i: (nt - 1 - i, cbk))
    prev8 = lambda cbk: pl.BlockSpec((V7X_SUBLANES, w),
                                     lambda i: (jnp.maximum((nt - 1 - i) * per8 - 1, 0), cbk))
    return pl.pallas_call(
        body, name=name, grid=(nt,),
        in_specs=[rev(0), rev(1), rev(0), rev(0), prev8(0), prev8(0), _full((CONV_WIDTH, w)), vec, _full((w, w)),
                  vec, _full((w, w)), vec, vec],
        out_specs=[pl.BlockSpec((t, 2 * w), lambda i: (nt - 1 - i, 0)), _full((CONV_WIDTH, w)), vec,
                   _full((w, w)), vec, _full((w, w)), vec, vec],
        out_shape=[jax.ShapeDtypeStruct((s, 2 * w), MXU_DTYPE), jax.ShapeDtypeStruct((CONV_WIDTH, w), F32),
                   jax.ShapeDtypeStruct((1, w), F32), jax.ShapeDtypeStruct((w, w), F32),
                   jax.ShapeDtypeStruct((1, w), F32), jax.ShapeDtypeStruct((w, w), F32),
                   jax.ShapeDtypeStruct((1, w), F32), jax.ShapeDtypeStruct((1, w), F32)],
        scratch_shapes=[pltpu.VMEM((t + V7X_SUBLANES, w), F32), pltpu.VMEM((t + V7X_SUBLANES, w), F32),
                        pltpu.VMEM((V7X_SUBLANES, w), F32), pltpu.VMEM((V7X_SUBLANES, w), F32),
                        pltpu.VMEM((t, w), F32)],
        compiler_params=_params(("arbitrary",), 80 * _nbytes((t, w), F32)),
    )(z, z, h, dy, z, h, cw, cb, wa, ba, wx, bx, lam)


def _rope_apply(v, cos, sin, width):
    half = MLA_ROPE // 2
    lanes = lax.broadcasted_iota(jnp.int32, v.shape, 1)
    first = (lanes % MLA_ROPE) < half
    partner = jnp.where(first, pltpu.roll(v, width - half, 1), pltpu.roll(v, half, 1))
    return v * cos + partner * sin


def _rope(x, cos, sin, *, width, cb, out_dtype, sum_heads=False, name):
    s = x.shape[0]
    t = _tile(s, 512)
    out_w = V7X_LANES if sum_heads else width

    def body(x_ref, c_ref, s_ref, o_ref):
        v = x_ref[...].astype(F32)
        if sum_heads:
            v = v[:, 0:V7X_LANES] + v[:, V7X_LANES:2 * V7X_LANES]
            v = v + pltpu.roll(v, 64, 1)
            v = v + pltpu.roll(v, 32, 1)
            out = _rope_apply(v, c_ref[...], s_ref[...], V7X_LANES)
            lanes = lax.broadcasted_iota(jnp.int32, out.shape, 1)
            out = jnp.where(lanes < MLA_ROPE, out, 0.0)
        else:
            out = _rope_apply(v, c_ref[...], s_ref[...], width)
        o_ref[...] = out.astype(o_ref.dtype)

    return pl.pallas_call(
        body, name=name, grid=(s // t,),
        in_specs=[_rows(t, width, cb), _rows(t, out_w), _rows(t, out_w)], out_specs=_rows(t, out_w),
        out_shape=jax.ShapeDtypeStruct((s, out_w), out_dtype),
        compiler_params=_params(("parallel",), 12 * _nbytes((t, width), F32)),
    )(x, cos, sin)


def _visible(t, unit, transposed):
    q_idx = lax.broadcasted_iota(jnp.int32, (t, t), 1 if transposed else 0)
    k_idx = lax.broadcasted_iota(jnp.int32, (t, t), 0 if transposed else 1)
    shift = int(math.log2(unit))
    return (q_idx >> shift) >= (k_idx >> shift)


def _attn_tile(s):
    return min(512, s // 4)


def _attn_fwd(q, k, v, cq, ck, *, scale, unit, name):
    hn, s, dk = q.shape
    dv = v.shape[-1]
    t = _attn_tile(s)
    decay = cq is not None

    def body(*refs):
        q_ref, k_ref, v_ref = refs[:3]
        cq_ref, ck_ref = (refs[3], refs[4]) if decay else (None, None)
        o_ref, lse_ref = refs[-2], refs[-1]
        i = pl.program_id(1)
        qt = q_ref[0]

        def tile(j, carry, masked):
            m, l, acc = carry
            off = pl.multiple_of(j * t, t)
            kt = k_ref[0, pl.ds(off, t), :]
            vt = v_ref[0, pl.ds(off, t), :]
            sc = _dot_nt(qt, kt) * scale
            if decay:
                sc = sc + cq_ref[0] - ck_ref[0, :, pl.ds(off, t)]
            if masked:
                sc = jnp.where(_visible(t, unit, False), sc, NEG_INF)
            m_new = jnp.maximum(m, jnp.max(sc, axis=-1, keepdims=True))
            alpha = jnp.exp(m - m_new)
            pr = jnp.exp(sc - m_new)
            l = alpha * l + jnp.sum(pr, axis=-1, keepdims=True)
            acc = alpha * acc + _dot(pr, vt)
            return m_new, l, acc

        init = (jnp.full((t, 1), NEG_INF, F32), jnp.zeros((t, 1), F32), jnp.zeros((t, dv), F32))
        carry = lax.fori_loop(0, i, lambda j, c: tile(j, c, False), init)
        m, l, acc = tile(i, carry, True)
        o_ref[0] = (acc / l).astype(o_ref.dtype)
        lse_ref[0] = m + jnp.log(l)

    qs = lambda d: pl.BlockSpec((1, t, d), lambda h, i: (h, i, 0))
    whole = lambda d: pl.BlockSpec((1, s, d), lambda h, i: (h, 0, 0))
    in_specs = [qs(dk), whole(dk), whole(dv)]
    ins = [q, k, v]
    if decay:
        in_specs += [qs(1), pl.BlockSpec((1, 1, s), lambda h, i: (h, 0, 0))]
        ins += [cq, ck]
    vmem = 4 * _nbytes((s, dk + dv), q.dtype) + 10 * _nbytes((t, t), F32) + 8 * _nbytes((t, V7X_LANES), F32)
    return pl.pallas_call(
        body, name=name, grid=(hn, s // t),
        in_specs=in_specs, out_specs=[qs(dv), qs(1)],
        out_shape=[jax.ShapeDtypeStruct((hn, s, dv), MXU_DTYPE), jax.ShapeDtypeStruct((hn, s, 1), F32)],
        compiler_params=_params(("parallel", "arbitrary"), vmem),
    )(*ins)


def _attn_bwd_q(q, k, v, do, lse, cq, ck, *, scale, unit, name):
    hn, s, dk = q.shape
    dv = v.shape[-1]
    t = _attn_tile(s)
    nt = s // t
    decay = cq is not None

    def body(*refs):
        q_ref, k_ref, v_ref, do_ref, lse_ref = refs[:5]
        cq_ref, ck_ref = (refs[5], refs[6]) if decay else (None, None)
        dq_ref, dl_ref, p_sc, dp_sc = refs[-4:]
        i = pl.program_id(1)
        qt = q_ref[0]
        dot = do_ref[0]
        lse_t = lse_ref[0]

        def sweep1(j, delta, masked):
            off = pl.multiple_of(j * t, t)
            kt = k_ref[0, pl.ds(off, t), :]
            vt = v_ref[0, pl.ds(off, t), :]
            sc = _dot_nt(qt, kt) * scale
            if decay:
                sc = sc + cq_ref[0] - ck_ref[0, :, pl.ds(off, t)]
            if masked:
                sc = jnp.where(_visible(t, unit, False), sc, NEG_INF)
            pr = jnp.exp(sc - lse_t)
            dp = _dot_nt(dot, vt)
            p_sc[j] = pr
            dp_sc[j] = dp
            return delta + jnp.sum(pr * dp, axis=-1, keepdims=True)

        delta = lax.fori_loop(0, i, lambda j, c: sweep1(j, c, False), jnp.zeros((t, 1), F32))
        delta = sweep1(i, delta, True)

        def sweep2(j, dq):
            off = pl.multiple_of(j * t, t)
            ds = p_sc[j] * (dp_sc[j] - delta)
            return dq + _dot(ds, k_ref[0, pl.ds(off, t), :])

        dq = lax.fori_loop(0, i + 1, sweep2, jnp.zeros((t, dk), F32))
        dq_ref[0] = dq * scale
        dl_ref[0] = delta

    qs = lambda d: pl.BlockSpec((1, t, d), lambda h, i: (h, i, 0))
    whole = lambda d: pl.BlockSpec((1, s, d), lambda h, i: (h, 0, 0))
    in_specs = [qs(dk), whole(dk), whole(dv), qs(dv), qs(1)]
    ins = [q, k, v, do, lse]
    if decay:
        in_specs += [qs(1), pl.BlockSpec((1, 1, s), lambda h, i: (h, 0, 0))]
        ins += [cq, ck]
    vmem = (4 * _nbytes((s, dk + dv), q.dtype) + 2 * _nbytes((nt, t, t), F32) + 8 * _nbytes((t, t), F32)
            + 12 * _nbytes((t, V7X_LANES), F32))
    return pl.pallas_call(
        body, name=name, grid=(hn, nt),
        in_specs=in_specs, out_specs=[qs(dk), qs(1)],
        out_shape=[jax.ShapeDtypeStruct((hn, s, dk), F32), jax.ShapeDtypeStruct((hn, s, 1), F32)],
        scratch_shapes=[pltpu.VMEM((nt, t, t), F32), pltpu.VMEM((nt, t, t), F32)],
        compiler_params=_params(("parallel", "arbitrary"), vmem),
    )(*ins)


def _attn_bwd_kv(q, k, v, do, lse_row, delta_row, cq_row, ck, *, scale, unit, name):
    hn, s, dk = q.shape
    dv = v.shape[-1]
    t = _attn_tile(s)
    nt = s // t
    decay = ck is not None

    def body(*refs):
        k_ref, v_ref, q_ref, do_ref, lse_ref, dl_ref = refs[:6]
        ck_ref, cq_ref = (refs[6], refs[7]) if decay else (None, None)
        outs = refs[8:] if decay else refs[6:]
        dk_ref, dv_ref = outs[0], outs[1]
        j = pl.program_id(1)
        kt = k_ref[0]
        vt = v_ref[0]

        def tile(i, carry, masked):
            dk_acc, dv_acc, dc_acc = carry
            off = pl.multiple_of(i * t, t)
            qt = q_ref[0, pl.ds(off, t), :]
            dot = do_ref[0, pl.ds(off, t), :]
            sc = _dot_nt(kt, qt) * scale
            if decay:
                sc = sc + cq_ref[0, :, pl.ds(off, t)] - ck_ref[0]
            if masked:
                sc = jnp.where(_visible(t, unit, True), sc, NEG_INF)
            pr = jnp.exp(sc - lse_ref[0, :, pl.ds(off, t)])
            dv_acc = dv_acc + _dot(pr, dot)
            ds = pr * (_dot_nt(vt, dot) - dl_ref[0, :, pl.ds(off, t)])
            dk_acc = dk_acc + _dot(ds, qt)
            if decay:
                dc_acc = dc_acc + jnp.sum(ds, axis=-1, keepdims=True)
            return dk_acc, dv_acc, dc_acc

        init = (jnp.zeros((t, dk), F32), jnp.zeros((t, dv), F32), jnp.zeros((t, 1), F32))
        carry = tile(j, init, True)
        dk_acc, dv_acc, dc_acc = lax.fori_loop(j + 1, nt, lambda i, c: tile(i, c, False), carry)
        dk_ref[0] = dk_acc * scale
        dv_ref[0] = dv_acc
        if decay:
            outs[2][0] = -dc_acc

    ks = lambda d: pl.BlockSpec((1, t, d), lambda h, j: (h, j, 0))
    whole = lambda d: pl.BlockSpec((1, s, d), lambda h, j: (h, 0, 0))
    row = pl.BlockSpec((1, 1, s), lambda h, j: (h, 0, 0))
    in_specs = [ks(dk), ks(dv), whole(dk), whole(dv), row, row]
    ins = [k, v, q, do, lse_row, delta_row]
    out_specs = [ks(dk), ks(dv)]
    out_shape = [jax.ShapeDtypeStruct((hn, s, dk), F32), jax.ShapeDtypeStruct((hn, s, dv), F32)]
    if decay:
        in_specs += [ks(1), row]
        ins += [ck, cq_row]
        out_specs.append(ks(1))
        out_shape.append(jax.ShapeDtypeStruct((hn, s, 1), F32))
    vmem = 4 * _nbytes((s, dk + dv), q.dtype) + 10 * _nbytes((t, t), F32) + 12 * _nbytes((t, V7X_LANES), F32)
    return pl.pallas_call(
        body, name=name, grid=(hn, nt),
        in_specs=in_specs, out_specs=out_specs, out_shape=out_shape,
        compiler_params=_params(("parallel", "arbitrary"), vmem),
    )(*ins)


STRIP = 32
HEAD_PAIRS = HEADS // 2


def _strip_rows(t):
    return min(STRIP, t)


def _split_scale(scale, has_rope):
    if not has_rope and math.frexp(scale)[0] == 0.5:
        return scale, 1.0
    return 1.0, scale


def _pair_mask(t):
    lane = lax.broadcasted_iota(jnp.int32, (t, V7X_LANES), 1)
    return lane < (V7X_LANES // 2)


def _strip_visible(r, t, row0, unit, transposed):
    rows = lax.broadcasted_iota(jnp.int32, (r, t), 0) + row0
    cols = lax.broadcasted_iota(jnp.int32, (r, t), 1)
    shift = int(math.log2(unit))
    if transposed:
        return (cols >> shift) >= (rows >> shift)
    return (rows >> shift) >= (cols >> shift)


def _rope_lanes(x, cos, sin, *, cb, groups, out_dtype, sum_parts=0, name):
    s = cos.shape[0]
    t = _tile(s, 512)
    w = groups * V7X_LANES

    def body(x_ref, c_ref, s_ref, o_ref):
        if sum_parts:
            v = x_ref[0].astype(F32)
            for part in range(1, sum_parts):
                v = v + x_ref[part].astype(F32)
            o_ref[...] = _rope_apply(v, c_ref[...], s_ref[...], V7X_LANES).astype(o_ref.dtype)
        else:
            for g in range(groups):
                sl = slice(g * V7X_LANES, (g + 1) * V7X_LANES)
                o_ref[:, sl] = _rope_apply(x_ref[:, sl].astype(F32), c_ref[...], s_ref[...],
                                           V7X_LANES).astype(o_ref.dtype)

    x_spec = (pl.BlockSpec((sum_parts, t, V7X_LANES), lambda i: (0, i, 0)) if sum_parts else _rows(t, w, cb))
    return pl.pallas_call(
        body, name=name, grid=(s // t,),
        in_specs=[x_spec, _rows(t, V7X_LANES), _rows(t, V7X_LANES)], out_specs=_rows(t, w),
        out_shape=jax.ShapeDtypeStruct((s, w), out_dtype),
        compiler_params=_params(("parallel",), 12 * _nbytes((t, max(w, 4 * V7X_LANES)), F32)),
    )(x, cos, sin)


def _pair_fwd(q_arr, q_cb, k_arr, k_cb, v_arr, v_cb, rope, decay, *, scale, unit, name):
    s = q_arr.shape[0]
    t = _attn_tile(s)
    r = _strip_rows(t)
    has_rope, has_decay = rope is not None, decay is not None
    kw = 2 * V7X_LANES if has_rope else V7X_LANES
    q_mul, s_mul = _split_scale(scale, has_rope)

    def body(*refs):
        it = iter(refs)
        q_ref, k_ref, v_ref = next(it), next(it), next(it)
        qr_ref, kr_ref = (next(it), next(it)) if has_rope else (None, None)
        cq_ref, ck_ref = (next(it), next(it)) if has_decay else (None, None)
        o_ref, lse_ref, lser_ref = next(it), next(it), next(it)
        q_sc, s_sc, p_sc, acc_sc, mx_sc, ls_sc, tr_sc = (next(it) for _ in range(7))
        i = pl.program_id(1)
        in_a = _pair_mask(t)
        qv = q_ref[...] * q_mul
        for hd in range(2):
            q_sc[hd, :, 0:V7X_LANES] = jnp.where(in_a if hd == 0 else jnp.logical_not(in_a), qv, 0).astype(MXU_DTYPE)
            if has_rope:
                q_sc[hd, :, V7X_LANES:kw] = qr_ref[:, hd * V7X_LANES:(hd + 1) * V7X_LANES].astype(MXU_DTYPE)
        mx_sc[...] = jnp.full(mx_sc.shape, NEG_INF, F32)
        ls_sc[...] = jnp.zeros(ls_sc.shape, F32)
        acc_sc[...] = jnp.zeros(acc_sc.shape, F32)
        cq_all = [cq_ref[hd] for hd in range(2)] if has_decay else None
        chunks = t // V7X_LANES

        def keys(j):
            off = pl.multiple_of(j * t, t)
            kt = k_ref[pl.ds(off, t), :]
            if has_rope:
                kt = jnp.concatenate([kt, kr_ref[pl.ds(off, t), :]], axis=-1)
            return off, kt

        def strip_scores(hd, row0, ck_row, masked):
            sc = s_sc[hd, pl.ds(row0, r), :]
            if s_mul != 1.0:
                sc = sc * s_mul
            if has_decay:
                sc = sc + (cq_all[hd][row0:row0 + r] - ck_row)
            if masked:
                sc = jnp.where(_strip_visible(r, t, row0, unit, False), sc, NEG_INF)
            return sc

        def fold(v, op):
            out = v[:, 0:V7X_LANES]
            for ch in range(1, chunks):
                out = op(out, v[:, ch * V7X_LANES:(ch + 1) * V7X_LANES])
            return out

        def tile_max(j, masked):
            off, kt = keys(j)
            for hd in range(2):
                s_sc[hd] = _dot_nt(q_sc[hd], kt)
                ck_row = ck_ref[hd, :, pl.ds(off, t)] if has_decay else None
                for b in range(t // r):
                    rows = pl.ds(b * r, r)
                    sc = strip_scores(hd, b * r, ck_row, masked)
                    mx_sc[hd, rows, :] = jnp.maximum(mx_sc[hd, rows, :], fold(sc, jnp.maximum))

        lax.fori_loop(0, i, lambda j, c: (tile_max(j, False), c)[1], 0)
        tile_max(i, True)
        m_all = [jnp.max(mx_sc[hd], axis=-1, keepdims=True) for hd in range(2)]

        def tile_sum(j, masked):
            off, kt = keys(j)
            vt = v_ref[pl.ds(off, t), :]
            for hd in range(2):
                s_sc[hd] = _dot_nt(q_sc[hd], kt)
                ck_row = ck_ref[hd, :, pl.ds(off, t)] if has_decay else None
                for b in range(t // r):
                    row0 = b * r
                    rows = pl.ds(row0, r)
                    pr = jnp.exp(strip_scores(hd, row0, ck_row, masked) - m_all[hd][row0:row0 + r])
                    ls_sc[hd, rows, :] += fold(pr, jnp.add)
                    p_sc[hd, rows, :] = pr.astype(MXU_DTYPE)
                acc_sc[hd] += _dot(p_sc[hd], vt)

        lax.fori_loop(0, i, lambda j, c: (tile_sum(j, False), c)[1], 0)
        tile_sum(i, True)
        l_all = [jnp.sum(ls_sc[hd], axis=-1, keepdims=True) for hd in range(2)]
        o_ref[...] = jnp.where(in_a, acc_sc[0] / l_all[0], acc_sc[1] / l_all[1]).astype(o_ref.dtype)
        for hd in range(2):
            lse_col = m_all[hd] + jnp.log(l_all[hd])
            lse_ref[hd] = lse_col
            tr_sc[...] = jnp.broadcast_to(lse_col, (t, V7X_LANES)).T
            lser_ref[hd] = tr_sc[0:1, :]

    blk = lambda cb: pl.BlockSpec((t, V7X_LANES), lambda p, i: (i, cb + p))
    whole = lambda cb: pl.BlockSpec((s, V7X_LANES), lambda p, i: (0, cb + p))
    stat = pl.BlockSpec((2, t, 1), lambda p, i: (p, i, 0))
    in_specs = [blk(q_cb), whole(k_cb), whole(v_cb)]
    ins = [q_arr, k_arr, v_arr]
    if has_rope:
        in_specs += [pl.BlockSpec((t, 2 * V7X_LANES), lambda p, i: (i, p)),
                     pl.BlockSpec((s, V7X_LANES), lambda p, i: (0, 0))]
        ins += list(rope)
    if has_decay:
        in_specs += [stat, pl.BlockSpec((2, 1, s), lambda p, i: (p, 0, 0))]
        ins += list(decay)
    col = (2, t, 1)
    vmem = (6 * _nbytes((s, V7X_LANES), MXU_DTYPE) + 6 * _nbytes((t, t), F32) + 10 * _nbytes((t, V7X_LANES), F32)
            + 8 * _nbytes((2, t, V7X_LANES), F32))
    return pl.pallas_call(
        body, name=name, grid=(HEAD_PAIRS, s // t),
        in_specs=in_specs,
        out_specs=[pl.BlockSpec((t, V7X_LANES), lambda p, i: (i, p)), stat,
                   pl.BlockSpec((2, 1, t), lambda p, i: (p, 0, i))],
        out_shape=[jax.ShapeDtypeStruct((s, HEADS * 64), MXU_DTYPE), jax.ShapeDtypeStruct((HEADS, s, 1), F32),
                   jax.ShapeDtypeStruct((HEADS, 1, s), F32)],
        scratch_shapes=[pltpu.VMEM((2, t, kw), MXU_DTYPE), pltpu.VMEM((2, t, t), F32), pltpu.VMEM((2, t, t), MXU_DTYPE),
                        pltpu.VMEM((2, t, V7X_LANES), F32), pltpu.VMEM((2, t, V7X_LANES), F32),
                        pltpu.VMEM((2, t, V7X_LANES), F32), pltpu.VMEM((V7X_LANES, t), F32)],
        compiler_params=_params(("parallel", "arbitrary"), vmem),
    )(*ins)


def _pair_bwd_q(q_arr, q_cb, k_arr, k_cb, v_arr, v_cb, do, lse, rope, decay, *, scale, unit, name):
    s = q_arr.shape[0]
    t = _attn_tile(s)
    nt = s // t
    r = t
    has_rope, has_decay = rope is not None, decay is not None
    kw = 2 * V7X_LANES if has_rope else V7X_LANES

    def body(*refs):
        it = iter(refs)
        q_ref, k_ref, v_ref, do_ref, lse_ref = (next(it) for _ in range(5))
        qr_ref, kr_ref = (next(it), next(it)) if has_rope else (None, None)
        cq_ref, ck_ref = (next(it), next(it)) if has_decay else (None, None)
        dq_ref, dl_ref = next(it), next(it)
        dqr_ref = next(it) if has_rope else None
        q_sc, do_sc, p_sc, dp_sc, ds_sc, dq_sc, dl_sc, s_sc = (next(it) for _ in range(8))
        i = pl.program_id(1)
        in_a = _pair_mask(t)
        qv = q_ref[...]
        dov = do_ref[...]
        for hd in range(2):
            sel = in_a if hd == 0 else jnp.logical_not(in_a)
            q_sc[hd, :, 0:V7X_LANES] = jnp.where(sel, qv, 0).astype(MXU_DTYPE)
            if has_rope:
                q_sc[hd, :, V7X_LANES:kw] = qr_ref[:, hd * V7X_LANES:(hd + 1) * V7X_LANES].astype(MXU_DTYPE)
            do_sc[hd] = jnp.where(sel, dov, 0).astype(MXU_DTYPE)
        dl_sc[...] = jnp.zeros(dl_sc.shape, F32)
        dq_sc[...] = jnp.zeros(dq_sc.shape, F32)

        def keys(j):
            off = pl.multiple_of(j * t, t)
            kt = k_ref[pl.ds(off, t), :]
            if has_rope:
                kt = jnp.concatenate([kt, kr_ref[pl.ds(off, t), :]], axis=-1)
            return off, kt

        for hd in range(2):
            lse_all = lse_ref[hd]
            cq_all = cq_ref[hd] if has_decay else None

            def sweep1(j, masked, hd=hd, lse_all=lse_all, cq_all=cq_all):
                off, kt = keys(j)
                s_sc[...] = _dot_nt(q_sc[hd], kt)
                dp_sc[j] = _dot_nt(do_sc[hd], v_ref[pl.ds(off, t), :])
                ck_row = ck_ref[hd, :, pl.ds(off, t)] if has_decay else None
                parts = []
                for b in range(t // r):
                    row0 = b * r
                    rows = pl.ds(row0, r)
                    sc = s_sc[rows, :] * scale
                    if has_decay:
                        sc = sc + (cq_all[row0:row0 + r] - ck_row)
                    if masked:
                        sc = jnp.where(_strip_visible(r, t, row0, unit, False), sc, NEG_INF)
                    pr = jnp.exp(sc - lse_all[row0:row0 + r])
                    p_sc[j, rows, :] = pr
                    parts.append(jnp.sum(pr * dp_sc[j, rows, :], axis=-1, keepdims=True))
                dl_sc[hd] += jnp.concatenate(parts, axis=0)

            def sweep1_unmasked(j, carry, sweep1=sweep1):
                sweep1(j, False)
                return carry

            lax.fori_loop(0, i, sweep1_unmasked, 0)
            sweep1(i, True)
            dl_all = dl_sc[hd]

            def sweep2(j, carry, hd=hd, dl_all=dl_all):
                _, kt = keys(j)
                for b in range(t // r):
                    row0 = b * r
                    rows = pl.ds(row0, r)
                    ds = p_sc[j, rows, :] * (dp_sc[j, rows, :] - dl_all[row0:row0 + r])
                    ds_sc[rows, :] = ds.astype(MXU_DTYPE)
                dq_sc[hd] += _dot(ds_sc[...], kt)
                return carry

            lax.fori_loop(0, i + 1, sweep2, 0)

        dq_ref[...] = (jnp.where(in_a, dq_sc[0, :, 0:V7X_LANES], dq_sc[1, :, 0:V7X_LANES]) * scale).astype(dq_ref.dtype)
        dl_ref[...] = dl_sc[...]
        if has_rope:
            dqr_ref[:, 0:V7X_LANES] = dq_sc[0, :, V7X_LANES:kw] * scale
            dqr_ref[:, V7X_LANES:kw] = dq_sc[1, :, V7X_LANES:kw] * scale

    blk = lambda cb: pl.BlockSpec((t, V7X_LANES), lambda p, i: (i, cb + p))
    whole = lambda cb: pl.BlockSpec((s, V7X_LANES), lambda p, i: (0, cb + p))
    stat = pl.BlockSpec((2, t, 1), lambda p, i: (p, i, 0))
    in_specs = [blk(q_cb), whole(k_cb), whole(v_cb), blk(0), stat]
    ins = [q_arr, k_arr, v_arr, do, lse]
    out_specs = [blk(0), stat]
    out_shape = [jax.ShapeDtypeStruct((s, HEADS * 64), MXU_DTYPE), jax.ShapeDtypeStruct((HEADS, s, 1), F32)]
    if has_rope:
        pair_rot = pl.BlockSpec((t, 2 * V7X_LANES), lambda p, i: (i, p))
        in_specs += [pair_rot, pl.BlockSpec((s, V7X_LANES), lambda p, i: (0, 0))]
        ins += list(rope)
        out_specs.append(pair_rot)
        out_shape.append(jax.ShapeDtypeStruct((s, HEADS * V7X_LANES), F32))
    if has_decay:
        in_specs += [stat, pl.BlockSpec((2, 1, s), lambda p, i: (p, 0, 0))]
        ins += list(decay)
    vmem = (6 * _nbytes((s, V7X_LANES), MXU_DTYPE) + 2 * _nbytes((nt, t, t), F32) + 6 * _nbytes((t, t), F32)
            + 16 * _nbytes((t, kw), F32))
    return pl.pallas_call(
        body, name=name, grid=(HEAD_PAIRS, nt),
        in_specs=in_specs, out_specs=out_specs, out_shape=out_shape,
        scratch_shapes=[pltpu.VMEM((2, t, kw), MXU_DTYPE), pltpu.VMEM((2, t, V7X_LANES), MXU_DTYPE),
                        pltpu.VMEM((nt, t, t), F32), pltpu.VMEM((nt, t, t), F32), pltpu.VMEM((t, t), MXU_DTYPE),
                        pltpu.VMEM((2, t, kw), F32), pltpu.VMEM((2, t, 1), F32), pltpu.VMEM((t, t), F32)],
        compiler_params=_params(("parallel", "arbitrary"), vmem),
    )(*ins)


def _pair_delta(q_arr, q_cb, k_arr, k_cb, v_arr, v_cb, do, lse, rope, decay, *, scale, unit, name):
    s = q_arr.shape[0]
    t = _attn_tile(s)
    r = _strip_rows(t)
    has_rope, has_decay = rope is not None, decay is not None
    kw = 2 * V7X_LANES if has_rope else V7X_LANES
    chunks = t // V7X_LANES
    q_mul, s_mul = _split_scale(scale, has_rope)

    def body(*refs):
        it = iter(refs)
        q_ref, k_ref, v_ref, do_ref, lse_ref = (next(it) for _ in range(5))
        qr_ref, kr_ref = (next(it), next(it)) if has_rope else (None, None)
        cq_ref, ck_ref = (next(it), next(it)) if has_decay else (None, None)
        dl_ref = next(it)
        q_sc, do_sc, s_sc, dp_sc, acc_sc = (next(it) for _ in range(5))
        i = pl.program_id(1)
        in_a = _pair_mask(t)
        qv, dov = q_ref[...] * q_mul, do_ref[...]
        for hd in range(2):
            sel = in_a if hd == 0 else jnp.logical_not(in_a)
            q_sc[hd, :, 0:V7X_LANES] = jnp.where(sel, qv, 0).astype(MXU_DTYPE)
            if has_rope:
                q_sc[hd, :, V7X_LANES:kw] = qr_ref[:, hd * V7X_LANES:(hd + 1) * V7X_LANES].astype(MXU_DTYPE)
            do_sc[hd] = jnp.where(sel, dov, 0).astype(MXU_DTYPE)
        acc_sc[...] = jnp.zeros(acc_sc.shape, F32)
        lse_all = [lse_ref[hd] for hd in range(2)]
        cq_all = [cq_ref[hd] for hd in range(2)] if has_decay else None

        def fold_add(v):
            out = v[:, 0:V7X_LANES]
            for ch in range(1, chunks):
                out = out + v[:, ch * V7X_LANES:(ch + 1) * V7X_LANES]
            return out

        def tile(j, masked):
            off = pl.multiple_of(j * t, t)
            kt = k_ref[pl.ds(off, t), :]
            if has_rope:
                kt = jnp.concatenate([kt, kr_ref[pl.ds(off, t), :]], axis=-1)
            vt = v_ref[pl.ds(off, t), :]
            for hd in range(2):
                s_sc[hd] = _dot_nt(q_sc[hd], kt)
                dp_sc[hd] = _dot_nt(do_sc[hd], vt)
                ck_row = ck_ref[hd, :, pl.ds(off, t)] if has_decay else None
                for b in range(t // r):
                    row0 = b * r
                    rows = pl.ds(row0, r)
                    sc = s_sc[hd, rows, :]
                    if s_mul != 1.0:
                        sc = sc * s_mul
                    if has_decay:
                        sc = sc + (cq_all[hd][row0:row0 + r] - ck_row)
                    if masked:
                        sc = jnp.where(_strip_visible(r, t, row0, unit, False), sc, NEG_INF)
                    pr = jnp.exp(sc - lse_all[hd][row0:row0 + r])
                    acc_sc[hd, rows, :] += fold_add(pr * dp_sc[hd, rows, :])

        lax.fori_loop(0, i, lambda j, c: (tile(j, False), c)[1], 0)
        tile(i, True)
        for hd in range(2):
            dl_ref[hd] = jnp.sum(acc_sc[hd].T, axis=0, keepdims=True)

    blk = lambda cb: pl.BlockSpec((t, V7X_LANES), lambda p, i: (i, cb + p))
    whole = lambda cb: pl.BlockSpec((s, V7X_LANES), lambda p, i: (0, cb + p))
    stat = pl.BlockSpec((2, t, 1), lambda p, i: (p, i, 0))
    in_specs = [blk(q_cb), whole(k_cb), whole(v_cb), blk(0), stat]
    ins = [q_arr, k_arr, v_arr, do, lse]
    if has_rope:
        in_specs += [pl.BlockSpec((t, 2 * V7X_LANES), lambda p, i: (i, p)),
                     pl.BlockSpec((s, V7X_LANES), lambda p, i: (0, 0))]
        ins += list(rope)
    if has_decay:
        in_specs += [stat, pl.BlockSpec((2, 1, s), lambda p, i: (p, 0, 0))]
        ins += list(decay)
    vmem = (6 * _nbytes((s, V7X_LANES), MXU_DTYPE) + 8 * _nbytes((t, t), F32) + 12 * _nbytes((t, kw), F32))
    return pl.pallas_call(
        body, name=name, grid=(HEAD_PAIRS, s // t),
        in_specs=in_specs, out_specs=pl.BlockSpec((2, 1, t), lambda p, i: (p, 0, i)),
        out_shape=jax.ShapeDtypeStruct((HEADS, 1, s), F32),
        scratch_shapes=[pltpu.VMEM((2, t, kw), MXU_DTYPE), pltpu.VMEM((2, t, V7X_LANES), MXU_DTYPE),
                        pltpu.VMEM((2, t, t), F32), pltpu.VMEM((2, t, t), F32),
                        pltpu.VMEM((2, t, V7X_LANES), F32)],
        compiler_params=_params(("parallel", "arbitrary"), vmem),
    )(*ins)


def _pair_bwd_kv(q_arr, q_cb, k_arr, k_cb, v_arr, v_cb, do, lse_row, delta_row, rope, decay, *, scale, unit, name):
    s = q_arr.shape[0]
    t = _attn_tile(s)
    nt = s // t
    r = _strip_rows(t)
    has_rope, has_decay = rope is not None, decay is not None
    kw = 2 * V7X_LANES if has_rope else V7X_LANES
    q_mul, s_mul = _split_scale(scale, has_rope)

    def body(*refs):
        it = iter(refs)
        k_ref, v_ref, q_ref, do_ref, lse_ref, dl_ref = (next(it) for _ in range(6))
        qr_ref, kr_ref = (next(it), next(it)) if has_rope else (None, None)
        ck_ref, cq_ref = (next(it), next(it)) if has_decay else (None, None)
        dk_ref, dv_ref, dq_ref = next(it), next(it), next(it)
        dkr_ref, dqr_ref = (next(it), next(it)) if has_rope else (None, None)
        dc_ref = next(it) if has_decay else None
        k_sc, v_sc, st_sc, dpt_sc, pt_sc, dst_sc, dk_sc, dv_sc, dc_sc, dqt_sc, kt_sc = (next(it) for _ in range(11))
        j = pl.program_id(1)
        in_a = _pair_mask(t)
        kv_, vv_ = k_ref[...], v_ref[...]
        for hd in range(2):
            sel = in_a if hd == 0 else jnp.logical_not(in_a)
            k_sc[hd, :, 0:V7X_LANES] = jnp.where(sel, kv_, 0).astype(MXU_DTYPE)
            if has_rope:
                k_sc[hd, :, V7X_LANES:kw] = kr_ref[...].astype(MXU_DTYPE)
            v_sc[hd] = jnp.where(sel, vv_, 0).astype(MXU_DTYPE)
        dk_sc[...] = jnp.zeros(dk_sc.shape, F32)
        dv_sc[...] = jnp.zeros(dv_sc.shape, F32)
        dc_sc[...] = jnp.zeros(dc_sc.shape, F32)
        k_all = kv_.astype(F32)
        if has_rope:
            k_all = jnp.concatenate([k_all, kr_ref[...].astype(F32)], axis=-1)
        kt_sc[...] = k_all.T.astype(MXU_DTYPE)

        @pl.when(j == 0)
        def _():
            dqt_sc[...] = jnp.zeros(dqt_sc.shape, F32)

        def tile(i, masked):
            off = pl.multiple_of(i * t, t)
            qt = q_ref[pl.ds(off, t), :] * q_mul
            dot = do_ref[pl.ds(off, t), :]
            for hd in range(2):
                qcat = qt
                if has_rope:
                    qcat = jnp.concatenate([qt, qr_ref[pl.ds(off, t), hd * V7X_LANES:(hd + 1) * V7X_LANES]], axis=-1)
                st_sc[hd] = _dot_nt(k_sc[hd], qcat)
                dpt_sc[hd] = _dot_nt(v_sc[hd], dot)
                lse_r = lse_ref[hd, :, pl.ds(off, t)]
                dl_r = dl_ref[hd, :, pl.ds(off, t)]
                cq_r = cq_ref[hd, :, pl.ds(off, t)] if has_decay else None
                ck_all = ck_ref[hd] if has_decay else None
                parts = []
                for b in range(t // r):
                    row0 = b * r
                    rows = pl.ds(row0, r)
                    sc = st_sc[hd, rows, :]
                    if s_mul != 1.0:
                        sc = sc * s_mul
                    if has_decay:
                        sc = sc + (cq_r - ck_all[row0:row0 + r])
                    if masked:
                        sc = jnp.where(_strip_visible(r, t, row0, unit, True), sc, NEG_INF)
                    pr = jnp.exp(sc - lse_r)
                    ds = pr * (dpt_sc[hd, rows, :] - dl_r)
                    pt_sc[hd, rows, :] = pr.astype(MXU_DTYPE)
                    dst_sc[hd, rows, :] = ds.astype(MXU_DTYPE)
                    if has_decay:
                        parts.append(jnp.sum(ds, axis=-1, keepdims=True))
                if has_decay:
                    dc_sc[hd] += jnp.concatenate(parts, axis=0)
                dv_sc[hd] += _dot(pt_sc[hd], dot)
                dk_sc[hd] += _dot(dst_sc[hd], qcat)
                dqt_sc[hd, :, pl.ds(off, t)] += _dot(kt_sc[...], dst_sc[hd])

        tile(j, True)

        def unmasked(i, carry):
            tile(i, False)
            return carry

        lax.fori_loop(j + 1, nt, unmasked, 0)
        dk_ref[...] = (jnp.where(in_a, dk_sc[0, :, 0:V7X_LANES], dk_sc[1, :, 0:V7X_LANES]) * s_mul).astype(dk_ref.dtype)
        dv_ref[...] = jnp.where(in_a, dv_sc[0], dv_sc[1]).astype(dv_ref.dtype)
        if has_rope:
            dkr_ref[0] = (dk_sc[0, :, V7X_LANES:kw] + dk_sc[1, :, V7X_LANES:kw]) * s_mul
        if has_decay:
            dc_ref[...] = -dc_sc[...]
        own = pl.ds(pl.multiple_of(j * t, t), t)
        dq_a = dqt_sc[0, :, own].T * scale
        dq_b = dqt_sc[1, :, own].T * scale
        dq_ref[...] = jnp.where(in_a, dq_a[:, 0:V7X_LANES], dq_b[:, 0:V7X_LANES]).astype(dq_ref.dtype)
        if has_rope:
            dqr_ref[:, 0:V7X_LANES] = dq_a[:, V7X_LANES:kw]
            dqr_ref[:, V7X_LANES:kw] = dq_b[:, V7X_LANES:kw]

    blk = lambda cb: pl.BlockSpec((t, V7X_LANES), lambda p, j: (j, cb + p))
    whole = lambda cb: pl.BlockSpec((s, V7X_LANES), lambda p, j: (0, cb + p))
    stat = pl.BlockSpec((2, t, 1), lambda p, j: (p, j, 0))
    row = pl.BlockSpec((2, 1, s), lambda p, j: (p, 0, 0))
    in_specs = [blk(k_cb), blk(v_cb), whole(q_cb), whole(0), row, row]
    ins = [k_arr, v_arr, q_arr, do, lse_row, delta_row]
    out_specs = [blk(0), blk(0), blk(0)]
    out_shape = [jax.ShapeDtypeStruct((s, HEADS * 64), MXU_DTYPE)] * 3
    if has_rope:
        in_specs += [pl.BlockSpec((s, 2 * V7X_LANES), lambda p, j: (0, p)),
                     pl.BlockSpec((t, V7X_LANES), lambda p, j: (j, 0))]
        ins += list(rope)
        out_specs += [pl.BlockSpec((1, t, V7X_LANES), lambda p, j: (p, j, 0)),
                      pl.BlockSpec((t, 2 * V7X_LANES), lambda p, j: (j, p))]
        out_shape += [jax.ShapeDtypeStruct((HEAD_PAIRS, s, V7X_LANES), F32),
                      jax.ShapeDtypeStruct((s, HEADS * V7X_LANES), F32)]
    if has_decay:
        in_specs += [stat, row]
        ins += list(decay)
        out_specs.append(stat)
        out_shape.append(jax.ShapeDtypeStruct((HEADS, s, 1), F32))
    vmem = (12 * _nbytes((s, V7X_LANES), MXU_DTYPE) + 8 * _nbytes((t, t), F32) + 16 * _nbytes((t, kw), F32)
            + _nbytes((2, kw, s), F32))
    return pl.pallas_call(
        body, name=name, grid=(HEAD_PAIRS, nt),
        in_specs=in_specs, out_specs=out_specs, out_shape=out_shape,
        scratch_shapes=[pltpu.VMEM((2, t, kw), MXU_DTYPE), pltpu.VMEM((2, t, V7X_LANES), MXU_DTYPE),
                        pltpu.VMEM((2, t, t), F32), pltpu.VMEM((2, t, t), F32), pltpu.VMEM((2, t, t), MXU_DTYPE),
                        pltpu.VMEM((2, t, t), MXU_DTYPE), pltpu.VMEM((2, t, kw), F32),
                        pltpu.VMEM((2, t, V7X_LANES), F32), pltpu.VMEM((2, t, 1), F32),
                        pltpu.VMEM((2, kw, s), F32), pltpu.VMEM((kw, t), MXU_DTYPE)],
        compiler_params=_params(("arbitrary", "arbitrary"), vmem),
    )(*ins)


def _fox_cum(z, bf, *, name):
    s = z.shape[0]
    w = V7X_LANES
    t = _row_tile(s, 512)
    steps = [1 << k for k in range(int(math.log2(t)))]
    cb = SEG["fl"][3] // w

    def body(f_ref, bf_ref, c_ref, car):
        @pl.when(pl.program_id(0) == 0)
        def _():
            car[...] = jnp.zeros_like(car)

        acc = -_softplus(-(f_ref[...] + bf_ref[...]))
        rows = lax.broadcasted_iota(jnp.int32, (t, w), 0)
        for d in steps:
            acc = acc + _shift_down(acc, d, 0.0, rows)
        c_ref[...] = acc + car[0:1, :]
        car[0:1, :] = c_ref[t - 1:t, :]

    return pl.pallas_call(
        body, name=name, grid=(s // t,),
        in_specs=[_rows(t, w, cb), _full((1, w))], out_specs=_rows(t, w),
        out_shape=jax.ShapeDtypeStruct((s, w), F32),
        scratch_shapes=[pltpu.VMEM((V7X_SUBLANES, w), F32)],
        compiler_params=_params(("arbitrary",), 16 * _nbytes((t, w), F32)),
    )(z, bf)


def _fox_cum_bwd(z, bf, dcum, *, name):
    s = z.shape[0]
    w = V7X_LANES
    t = _row_tile(s, 512)
    nt = s // t
    steps = [1 << k for k in range(int(math.log2(t)))]
    cb = SEG["fl"][3] // w

    def body(f_ref, bf_ref, dc_ref, df_ref, dbf_ref, car, tmp):
        @pl.when(pl.program_id(0) == 0)
        def _():
            car[...] = jnp.zeros_like(car)
            dbf_ref[...] = jnp.zeros_like(dbf_ref)

        acc = dc_ref[...]
        rows = lax.broadcasted_iota(jnp.int32, (t, w), 0)
        for d in steps:
            acc = acc + _shift_up(acc, d, 0.0, rows, t)
        dlf = acc + car[0:1, :]
        tmp[...] = dlf
        car[0:1, :] = tmp[0:1, :]
        df = dlf * _sigmoid(-(f_ref[...] + bf_ref[...]))
        df_ref[...] = df.astype(df_ref.dtype)
        dbf_ref[...] += jnp.sum(df, axis=0, keepdims=True)

    rev = lambda cbk: pl.BlockSpec((t, w), lambda i: (nt - 1 - i, cbk))
    return pl.pallas_call(
        body, name=name, grid=(nt,),
        in_specs=[rev(cb), _full((1, w)), rev(0)], out_specs=[rev(0), _full((1, w))],
        out_shape=[jax.ShapeDtypeStruct((s, w), MXU_DTYPE), jax.ShapeDtypeStruct((1, w), F32)],
        scratch_shapes=[pltpu.VMEM((V7X_SUBLANES, w), F32), pltpu.VMEM((t, w), F32)],
        compiler_params=_params(("arbitrary",), 16 * _nbytes((t, w), F32)),
    )(z, bf, dcum)


_GATE_CB = SEG["gate"][3] // D_MODEL


def _merge_fwd(ya, yb, yc, z, gate_b, *, name):
    s = ya.shape[0]
    d = D_MODEL
    t = _tile(s, 256)

    def body(ya_ref, yb_ref, yc_ref, g0_ref, g1_ref, g2_ref, gb_ref, o_ref):
        out = _sigmoid(g0_ref[...] + gb_ref[:, 0:d]) * ya_ref[...]
        out = out + _sigmoid(g1_ref[...] + gb_ref[:, d:2 * d]) * yb_ref[...]
        out = out + _sigmoid(g2_ref[...] + gb_ref[:, 2 * d:3 * d]) * yc_ref[...]
        o_ref[...] = out.astype(o_ref.dtype)

    return pl.pallas_call(
        body, name=name, grid=(s // t,),
        in_specs=[_rows(t, d)] * 3 + [_rows(t, d, _GATE_CB + b) for b in range(3)] + [_full((1, 3 * d))],
        out_specs=_rows(t, d), out_shape=jax.ShapeDtypeStruct((s, d), MXU_DTYPE),
        compiler_params=_params(("parallel",), 20 * _nbytes((t, d), F32)),
    )(ya, yb, yc, z, z, z, gate_b)


def _merge_bwd(dm, ya, yb, yc, z, gate_b, *, name):
    s = ya.shape[0]
    d = D_MODEL
    t = _tile(s, 256)

    def body(dm_ref, ya_ref, yb_ref, yc_ref, g0_ref, g1_ref, g2_ref, gb_ref, da_ref, db_ref, dc_ref, dgl_ref,
             dgb_ref):
        dmv = dm_ref[...]
        parts = []
        for b, (y_ref, g_ref, dy_ref) in enumerate(((ya_ref, g0_ref, da_ref), (yb_ref, g1_ref, db_ref),
                                                    (yc_ref, g2_ref, dc_ref))):
            gate = _sigmoid(g_ref[...] + gb_ref[:, b * d:(b + 1) * d])
            dy_ref[...] = (dmv * gate).astype(dy_ref.dtype)
            dgl = dmv * y_ref[...] * gate * (1.0 - gate)
            dgl_ref[:, b * d:(b + 1) * d] = dgl.astype(dgl_ref.dtype)
            parts.append(jnp.sum(dgl, axis=0, keepdims=True))

        @pl.when(pl.program_id(0) == 0)
        def _():
            for b, part in enumerate(parts):
                dgb_ref[:, b * d:(b + 1) * d] = part

        @pl.when(pl.program_id(0) > 0)
        def _():
            for b, part in enumerate(parts):
                dgb_ref[:, b * d:(b + 1) * d] += part

    return pl.pallas_call(
        body, name=name, grid=(s // t,),
        in_specs=[_rows(t, d)] * 4 + [_rows(t, d, _GATE_CB + b) for b in range(3)] + [_full((1, 3 * d))],
        out_specs=[_rows(t, d)] * 3 + [_rows(t, 3 * d), _full((1, 3 * d))],
        out_shape=[jax.ShapeDtypeStruct((s, d), MXU_DTYPE)] * 3
        + [jax.ShapeDtypeStruct((s, 3 * d), MXU_DTYPE), jax.ShapeDtypeStruct((1, 3 * d), F32)],
        compiler_params=_params(("arbitrary",), 36 * _nbytes((t, d), F32)),
    )(dm, ya, yb, yc, z, z, z, gate_b)


def _swiglu_fwd(hf, *, name):
    s = hf.shape[0]
    t = _tile(s, 256)

    def body(g_ref, u_ref, o_ref):
        gv = g_ref[...]
        o_ref[...] = (gv * _sigmoid(gv) * u_ref[...]).astype(o_ref.dtype)

    return pl.pallas_call(
        body, name=name, grid=(s // t,),
        in_specs=[_rows(t, D_FF, 0), _rows(t, D_FF, 1)], out_specs=_rows(t, D_FF),
        out_shape=jax.ShapeDtypeStruct((s, D_FF), MXU_DTYPE),
        compiler_params=_params(("parallel",), 10 * _nbytes((t, D_FF), F32)),
    )(hf, hf)


def _swiglu_bwd(hf, dact, *, name):
    s = hf.shape[0]
    t = _tile(s, 256)

    def body(g_ref, u_ref, da_ref, o_ref):
        gv = g_ref[...]
        dav = da_ref[...]
        sg = _sigmoid(gv)
        o_ref[:, 0:D_FF] = (dav * u_ref[...] * sg * (1.0 + gv * (1.0 - sg))).astype(o_ref.dtype)
        o_ref[:, D_FF:2 * D_FF] = (dav * gv * sg).astype(o_ref.dtype)

    return pl.pallas_call(
        body, name=name, grid=(s // t,),
        in_specs=[_rows(t, D_FF, 0), _rows(t, D_FF, 1), _rows(t, D_FF)], out_specs=_rows(t, 2 * D_FF),
        out_shape=jax.ShapeDtypeStruct((s, 2 * D_FF), MXU_DTYPE),
        compiler_params=_params(("parallel",), 14 * _nbytes((t, D_FF), F32)),
    )(hf, hf, dact)


def _ple_fwd(x, lg, pe, *, name):
    s, d = x.shape
    t = _tile(s, 512)

    def body(x_ref, lg_ref, pe_ref, o_ref):
        o_ref[...] = x_ref[...] + _sigmoid(lg_ref[...]) * pe_ref[...]

    return pl.pallas_call(
        body, name=name, grid=(s // t,),
        in_specs=[_rows(t, d)] * 3, out_specs=_rows(t, d), out_shape=jax.ShapeDtypeStruct((s, d), F32),
        compiler_params=_params(("parallel",), 12 * _nbytes((t, d), F32)),
    )(x, lg, pe)


def _ple_bwd(dx, lg, pe, *, name):
    s, d = dx.shape
    t = _tile(s, 512)

    def body(dx_ref, lg_ref, pe_ref, dpe_ref, dlg_ref):
        dxv = dx_ref[...]
        sg = _sigmoid(lg_ref[...])
        dpe_ref[...] = (dxv * sg).astype(dpe_ref.dtype)
        dlg_ref[...] = (dxv * pe_ref[...] * sg * (1.0 - sg)).astype(dlg_ref.dtype)

    return pl.pallas_call(
        body, name=name, grid=(s // t,),
        in_specs=[_rows(t, d)] * 3, out_specs=[_rows(t, d)] * 2,
        out_shape=[jax.ShapeDtypeStruct((s, d), MXU_DTYPE)] * 2,
        compiler_params=_params(("parallel",), 14 * _nbytes((t, d), F32)),
    )(dx, lg, pe)


def _adamw(parts, w, m, v, *, name):
    rows, lanes = w.shape
    t = math.gcd(rows, 160)
    assert rows % t == 0 and t % V7X_SUBLANES == 0
    c1 = 1.0 / (1.0 - ADAM_B1 ** ADAM_STEP)
    c2 = 1.0 / (1.0 - ADAM_B2 ** ADAM_STEP)

    def body(p_ref, w_ref, m_ref, v_ref, g_ref, d_ref, nm_ref, nv_ref):
        g = p_ref[0].astype(F32)
        for j in range(1, N_DEV):
            g = g + p_ref[j].astype(F32)
        m2 = ADAM_B1 * m_ref[...] + (1.0 - ADAM_B1) * g
        v2 = ADAM_B2 * v_ref[...] + (1.0 - ADAM_B2) * (g * g)
        g_ref[...] = g
        nm_ref[...] = m2
        nv_ref[...] = v2
        d_ref[...] = -ADAM_LR * ((m2 * c1) / (jnp.sqrt(v2 * c2) + ADAM_EPS) + ADAM_WD * w_ref[...])

    blk = _rows(t, lanes)
    return pl.pallas_call(
        body, name=name, grid=(rows // t,),
        in_specs=[pl.BlockSpec((N_DEV, t, lanes), lambda i: (0, i, 0)), blk, blk, blk], out_specs=[blk] * 4,
        out_shape=[jax.ShapeDtypeStruct((rows, lanes), F32)] * 4,
        compiler_params=_params(("parallel",), 40 * _nbytes((t, lanes), F32)),
    )(parts, w, m, v)


def _mesh_pos():
    return lax.axis_index("x"), lax.axis_index("y"), lax.axis_index("c")


def _all_gather(blk, *, name):
    r, c_dim = blk.shape

    def body(x_ref, out_ref, send_sems, recv_sems, local_sem):
        x, y, c = _mesh_pos()
        me, sibling = (x, y, c), (x, y, 1 - c)
        chips = [(1 - x, y), (x, 1 - y), (1 - x, 1 - y)]

        def slot(px, py, pc):
            return out_ref.at[4 * px + 2 * py + pc]

        def copy(k, block, to, src=None):
            return pltpu.make_async_remote_copy(
                src_ref=slot(*block) if src is None else src, dst_ref=slot(*block),
                send_sem=send_sems.at[k], recv_sem=recv_sems.at[k],
                device_id=to, device_id_type=pl.DeviceIdType.MESH)

        mine = pltpu.make_async_copy(x_ref, slot(*me), local_sem)
        mine.start()
        first = [copy(0, me, sibling, src=x_ref)]
        first += [copy(1 + j, me, (*chip, c), src=x_ref) for j, chip in enumerate(chips)]
        for cp in first:
            cp.start()
        passed = [copy(4 + j, (*chip, c), sibling) for j, chip in enumerate(chips)]
        for j, chip in enumerate(chips):
            copy(1 + j, (*chip, c), me).wait_recv()
            passed[j].start()
        copy(0, sibling, me).wait_recv()
        for j, chip in enumerate(chips):
            copy(4 + j, (*chip, 1 - c), me).wait_recv()
        for cp in first + passed:
            cp.wait_send()
        mine.wait()

    return pl.pallas_call(
        body, name=name,
        out_shape=jax.ShapeDtypeStruct((N_DEV, r, c_dim), blk.dtype),
        in_specs=[pl.BlockSpec(memory_space=pl.ANY)], out_specs=pl.BlockSpec(memory_space=pl.ANY),
        scratch_shapes=[pltpu.SemaphoreType.DMA((7,)), pltpu.SemaphoreType.DMA((7,)), pltpu.SemaphoreType.DMA],
    )(blk)


def _all_to_all(pay, *, name):
    _, r, c_dim = pay.shape

    def body(in_ref, out_ref, send_sems, recv_sems, local_sem):
        x, y, c = _mesh_pos()
        me = 4 * x + 2 * y + c
        local = pltpu.make_async_copy(in_ref.at[me], out_ref.at[me], local_sem)
        local.start()
        copies = []
        for k in range(1, N_DEV):
            px = 1 - x if k & 4 else x
            py = 1 - y if k & 2 else y
            pc = 1 - c if k & 1 else c
            copies.append(pltpu.make_async_remote_copy(
                src_ref=in_ref.at[4 * px + 2 * py + pc], dst_ref=out_ref.at[me],
                send_sem=send_sems.at[k - 1], recv_sem=recv_sems.at[k - 1],
                device_id=(px, py, pc), device_id_type=pl.DeviceIdType.MESH))
        for cp in copies:
            cp.start()
        for cp in copies:
            cp.wait()
        local.wait()

    return pl.pallas_call(
        body, name=name,
        out_shape=jax.ShapeDtypeStruct((N_DEV, r, c_dim), pay.dtype),
        in_specs=[pl.BlockSpec(memory_space=pl.ANY)], out_specs=pl.BlockSpec(memory_space=pl.ANY),
        scratch_shapes=[pltpu.SemaphoreType.DMA((7,)), pltpu.SemaphoreType.DMA((7,)), pltpu.SemaphoreType.DMA],
    )(pay)


def _all_gather_many(blocks, *, name):
    n = len(blocks)

    def body(*refs):
        x_refs, out_refs = refs[:n], refs[n:2 * n]
        send_sems, recv_sems, local_sems = refs[2 * n:]
        x, y, c = _mesh_pos()
        me, sibling = (x, y, c), (x, y, 1 - c)
        chips = [(1 - x, y), (x, 1 - y), (1 - x, 1 - y)]

        def slot(a, px, py, pc):
            return out_refs[a].at[4 * px + 2 * py + pc]

        def copy(k, a, block, to, src=None):
            return pltpu.make_async_remote_copy(
                src_ref=slot(a, *block) if src is None else src, dst_ref=slot(a, *block),
                send_sem=send_sems.at[k, a], recv_sem=recv_sems.at[k, a],
                device_id=to, device_id_type=pl.DeviceIdType.MESH)

        mine = [pltpu.make_async_copy(x_refs[a], slot(a, *me), local_sems.at[a]) for a in range(n)]
        for cp in mine:
            cp.start()
        first = [copy(0, a, me, sibling, src=x_refs[a]) for a in range(n)]
        first += [copy(1 + j, a, me, (*chip, c), src=x_refs[a]) for j, chip in enumerate(chips) for a in range(n)]
        for cp in first:
            cp.start()
        passed = []
        for j, chip in enumerate(chips):
            for a in range(n):
                copy(1 + j, a, (*chip, c), me).wait_recv()
                fwd = copy(4 + j, a, (*chip, c), sibling)
                fwd.start()
                passed.append(fwd)
        for a in range(n):
            copy(0, a, sibling, me).wait_recv()
        for j, chip in enumerate(chips):
            for a in range(n):
                copy(4 + j, a, (*chip, 1 - c), me).wait_recv()
        for cp in first + passed:
            cp.wait_send()
        for cp in mine:
            cp.wait()

    any_spec = pl.BlockSpec(memory_space=pl.ANY)
    return pl.pallas_call(
        body, name=name,
        out_shape=[jax.ShapeDtypeStruct((N_DEV,) + b.shape, b.dtype) for b in blocks],
        in_specs=[any_spec] * n, out_specs=[any_spec] * n,
        scratch_shapes=[pltpu.SemaphoreType.DMA((7, n)), pltpu.SemaphoreType.DMA((7, n)),
                        pltpu.SemaphoreType.DMA((n,))],
    )(*blocks)


def _all_to_all_many(pays, *, name):
    n = len(pays)

    def body(*refs):
        in_refs, out_refs = refs[:n], refs[n:2 * n]
        send_sems, recv_sems, local_sems = refs[2 * n:]
        x, y, c = _mesh_pos()
        me = 4 * x + 2 * y + c
        local = [pltpu.make_async_copy(in_refs[a].at[me], out_refs[a].at[me], local_sems.at[a]) for a in range(n)]
        for cp in local:
            cp.start()
        copies = []
        for k in range(1, N_DEV):
            px = 1 - x if k & 4 else x
            py = 1 - y if k & 2 else y
            pc = 1 - c if k & 1 else c
            for a in range(n):
                copies.append(pltpu.make_async_remote_copy(
                    src_ref=in_refs[a].at[4 * px + 2 * py + pc], dst_ref=out_refs[a].at[me],
                    send_sem=send_sems.at[k - 1, a], recv_sem=recv_sems.at[k - 1, a],
                    device_id=(px, py, pc), device_id_type=pl.DeviceIdType.MESH))
        for cp in copies:
            cp.start()
        for cp in copies:
            cp.wait()
        for cp in local:
            cp.wait()

    any_spec = pl.BlockSpec(memory_space=pl.ANY)
    return pl.pallas_call(
        body, name=name,
        out_shape=[jax.ShapeDtypeStruct(p.shape, p.dtype) for p in pays],
        in_specs=[any_spec] * n, out_specs=[any_spec] * n,
        scratch_shapes=[pltpu.SemaphoreType.DMA((7, n)), pltpu.SemaphoreType.DMA((7, n)),
                        pltpu.SemaphoreType.DMA((n,))],
    )(*pays)


def _adamw_nd(parts, w, m, v, *, name):
    d0, rows, cols = w.shape
    t = rows
    for cand in range(V7X_SUBLANES, min(rows, 256) + 1, V7X_SUBLANES):
        if rows % cand == 0:
            t = cand
    c1 = 1.0 / (1.0 - ADAM_B1 ** ADAM_STEP)
    c2 = 1.0 / (1.0 - ADAM_B2 ** ADAM_STEP)

    def body(p_ref, w_ref, m_ref, v_ref, g_ref, d_ref, nm_ref, nv_ref):
        g = p_ref[0, 0].astype(F32)
        for j in range(1, N_DEV):
            g = g + p_ref[j, 0].astype(F32)
        m2 = ADAM_B1 * m_ref[0] + (1.0 - ADAM_B1) * g
        v2 = ADAM_B2 * v_ref[0] + (1.0 - ADAM_B2) * (g * g)
        g_ref[0] = g
        nm_ref[0] = m2
        nv_ref[0] = v2
        d_ref[0] = -ADAM_LR * ((m2 * c1) / (jnp.sqrt(v2 * c2) + ADAM_EPS) + ADAM_WD * w_ref[0])

    blk = pl.BlockSpec((1, t, cols), lambda l, i: (l, i, 0))
    lanes = -(-cols // V7X_LANES) * V7X_LANES
    return pl.pallas_call(
        body, name=name, grid=(d0, rows // t),
        in_specs=[pl.BlockSpec((N_DEV, 1, t, cols), lambda l, i: (0, l, i, 0)), blk, blk, blk], out_specs=[blk] * 4,
        out_shape=[jax.ShapeDtypeStruct(w.shape, F32)] * 4,
        compiler_params=_params(("parallel", "parallel"), 40 * _nbytes((max(t, 16), lanes), F32)),
    )(parts, w, m, v)


def _flat_rows(parts, row_multiple):
    flat = jnp.concatenate([p.reshape(-1) for p in parts])
    chunk = PAYLOAD_LANES * row_multiple
    total = -(-flat.shape[0] // chunk) * chunk
    return jnp.pad(flat, (0, total - flat.shape[0])).reshape(total // PAYLOAD_LANES, PAYLOAD_LANES)


def _split_flat(flat, shapes):
    out, off = [], 0
    flat = flat.reshape(-1)
    for shp in shapes:
        n = math.prod(shp)
        out.append(flat[off:off + n].reshape(shp))
        off += n
    return out


def _pad_w_in(w):
    pieces, cursor = [], 0
    for _, off, width, pad_off, _ in SEGS:
        if pad_off > cursor:
            pieces.append(jnp.zeros(w.shape[:-1] + (pad_off - cursor,), w.dtype))
        pieces.append(w[..., off:off + width])
        cursor = pad_off + width
    pieces.append(jnp.zeros(w.shape[:-1] + (D_IN_PAD - cursor,), w.dtype))
    return jnp.concatenate(pieces, axis=-1)


def _unpad_w_in(w):
    return jnp.concatenate([w[..., pad_off:pad_off + width] for _, _, width, pad_off, _ in SEGS], axis=-1)


def _heads(a, hd):
    return a.reshape(a.shape[0], HEADS, hd).transpose(1, 0, 2)


def _unheads(a):
    return a.transpose(1, 0, 2).reshape(a.shape[1], -1)


def _block_diag(w):
    eye = jnp.eye(LRU_HEADS, dtype=w.dtype)
    return (eye[:, None, :, None] * w[:, :, None, :]).reshape(LRU_WIDTH, LRU_WIDTH)


def _diag_blocks(w):
    w4 = w.reshape(LRU_HEADS, LRU_HEAD_DIM, LRU_HEADS, LRU_HEAD_DIM)
    return jnp.stack([w4[h, :, h, :] for h in range(LRU_HEADS)])


def _lane_pad(a, width):
    return jnp.pad(a, ((0, 0), (0, width - a.shape[-1])))


def _layer_fwd(x, p_i, wts, tabs, tag):
    n = functools.partial(lambda base, t=tag: f"{base}_{t}")
    sv = {"x": x}
    n1 = _rms_fwd(x, wts["mix_norm"], width=D_MODEL, name=n("mix_norm_fwd"))
    z, z16 = _mm(n1, wts["w_in"], also_mxu=True, name=n("w_in_fwd"))
    sv.update(n1=n1, z=z, z16=z16)
    lanes = V7X_LANES

    ya_pre, hseq = _lru_fwd(z, wts["conv_w"], wts["conv_b"], wts["lru_wa"], wts["lru_ba"], wts["lru_wx"],
                            wts["lru_bx"], wts["lru_lambda"], name=n("lru_fwd"))
    ya = _mm(ya_pre, wts["w_br_a"], name=n("br_a_fwd"))
    sv.update(ya_pre=ya_pre, hseq=hseq, ya=ya)

    cqn = _rms_fwd(z, wts["mla_q_norm"], width=MLA_Q_LORA, cb=SEG["cq"][3] // MLA_Q_LORA, name=n("q_norm_fwd"))
    ckvn = _rms_fwd(z, wts["mla_kv_norm"], width=MLA_KV_LORA, cb=SEG["ckv"][3] // MLA_KV_LORA,
                    name=n("kv_norm_fwd"))
    qp, qp16 = _mm(cqn, wts["mla_wuq"], also_mxu=True, name=n("wuq_fwd"))
    kv = _mm(ckvn, wts["mla_wukv"], out_dtype=MXU_DTYPE, name=n("wukv_fwd"))
    q_rot = _rope_lanes(qp, tabs["cos128"], tabs["sin128"], cb=0, groups=HEADS, out_dtype=MXU_DTYPE,
                        name=n("q_rope_fwd"))
    k_rot = _rope_lanes(z, tabs["cos128"], tabs["sin128"], cb=SEG["kr"][3] // lanes, groups=1, out_dtype=MXU_DTYPE,
                        name=n("k_rope_fwd"))
    mla_ops = (qp16, HEADS, kv, 0, kv, HEAD_PAIRS)
    ob_flat, lse_b, lse_b_row = _pair_fwd(*mla_ops, (q_rot, k_rot), None, scale=(MLA_NOPE + MLA_ROPE) ** -0.5,
                                          unit=CHUNK, name=n("mla_attn_fwd"))
    yb = _mm(ob_flat, wts["w_br_b"], name=n("br_b_fwd"))
    sv.update(cqn=cqn, ckvn=ckvn, mla_ops=mla_ops, mla_rot=(q_rot, k_rot), lse_b=lse_b, lse_b_row=lse_b_row,
              ob_flat=ob_flat, yb=yb)

    cum = _fox_cum(z, wts["fox_bf"], name=n("fox_cum_fwd"))
    cum_h = cum[:, :HEADS].T
    fox_decay = (cum_h[:, :, None], cum_h[:, None, :])
    fox_ops = (z16, SEG["fq"][3] // lanes, z16, SEG["fk"][3] // lanes, z16, SEG["fv"][3] // lanes)
    oc_flat, lse_c, lse_c_row = _pair_fwd(*fox_ops, None, fox_decay, scale=FOX_HEAD_DIM ** -0.5, unit=1,
                                          name=n("fox_attn_fwd"))
    yc = _mm(oc_flat, wts["w_br_c"], name=n("br_c_fwd"))
    sv.update(fox_ops=fox_ops, fox_decay=fox_decay, lse_c=lse_c, lse_c_row=lse_c_row, oc_flat=oc_flat, yc=yc)

    merged = _merge_fwd(ya, yb, yc, z, wts["gate_b"], name=n("merge_fwd"))
    x1 = _mm(merged, wts["w_o"], res=x, name=n("w_o_fwd"))
    n2 = _rms_fwd(x1, wts["ffn_norm"], width=D_MODEL, name=n("ffn_norm_fwd"))
    hf = _mm(n2, wts["w_gate_up"], name=n("gate_up_fwd"))
    act = _swiglu_fwd(hf, name=n("swiglu_fwd"))
    x2 = _mm(act, wts["w_down"], res=x1, name=n("down_fwd"))
    n3 = _rms_fwd(x2, wts["ple_norm"], width=D_MODEL, name=n("ple_norm_fwd"))
    lg = _mm(n3, wts["w_ple_gate"], name=n("ple_gate_fwd"))
    pe = _mm(p_i, wts["w_ple"], name=n("ple_fwd_mm"))
    x3 = _ple_fwd(x2, lg, pe, name=n("ple_fwd"))
    sv.update(merged=merged, x1=x1, n2=n2, hf=hf, act=act, x2=x2, n3=n3, lg=lg, pe=pe, p_i=p_i)
    return x3, sv


def _layer_bwd(dx3, sv, wts, tabs, tag):
    n = functools.partial(lambda base, t=tag: f"{base}_{t}")
    gr = {}
    z = sv["z"]
    s = z.shape[0]

    dpe, dlg = _ple_bwd(dx3, sv["lg"], sv["pe"], name=n("ple_bwd"))
    gr["w_ple"] = _mm(sv["p_i"], dpe, ta=True, name=n("ple_dw"))
    gr["w_ple_gate"] = _mm(sv["n3"], dlg, ta=True, name=n("ple_gate_dw"))
    dn3 = _mm(dlg, wts["w_ple_gate"], tb=True, name=n("ple_gate_dx"))
    dx2, gr["ple_norm"] = _rms_bwd(sv["x2"], wts["ple_norm"], dn3, width=D_MODEL, res=dx3, name=n("ple_norm_bwd"))

    dact = _mm(dx2, wts["w_down"], tb=True, name=n("down_dx"))
    gr["w_down"] = _mm(sv["act"], dx2, ta=True, name=n("down_dw"))
    dhf = _swiglu_bwd(sv["hf"], dact, name=n("swiglu_bwd"))
    gr["w_gate_up"] = _mm(sv["n2"], dhf, ta=True, name=n("gate_up_dw"))
    dn2 = _mm(dhf, wts["w_gate_up"], tb=True, name=n("gate_up_dx"))
    dx1, gr["ffn_norm"] = _rms_bwd(sv["x1"], wts["ffn_norm"], dn2, width=D_MODEL, res=dx2, name=n("ffn_norm_bwd"))

    dmerged = _mm(dx1, wts["w_o"], tb=True, name=n("w_o_dx"))
    gr["w_o"] = _mm(sv["merged"], dx1, ta=True, name=n("w_o_dw"))
    dya, dyb, dyc, dgl, gr["gate_b"] = _merge_bwd(dmerged, sv["ya"], sv["yb"], sv["yc"], z, wts["gate_b"],
                                                  name=n("merge_bwd"))
    gr["w_br_a"] = _mm(sv["ya_pre"], dya, ta=True, name=n("br_a_dw"))
    gr["w_br_b"] = _mm(sv["ob_flat"], dyb, ta=True, name=n("br_b_dw"))
    gr["w_br_c"] = _mm(sv["oc_flat"], dyc, ta=True, name=n("br_c_dw"))
    dya_pre = _mm(dya, wts["w_br_a"], tb=True, name=n("br_a_dx"))
    dob = _mm(dyb, wts["w_br_b"], tb=True, out_dtype=MXU_DTYPE, name=n("br_b_dx"))
    doc = _mm(dyc, wts["w_br_c"], tb=True, out_dtype=MXU_DTYPE, name=n("br_c_dx"))

    (dz_a, gr["conv_w"], gr["conv_b"], dwa, gr["lru_ba"], dwx, gr["lru_bx"], gr["lru_lambda"]) = _lru_bwd(
        z, sv["hseq"], dya_pre, wts["conv_w"], wts["conv_b"], wts["lru_wa"], wts["lru_ba"], wts["lru_wx"],
        wts["lru_bx"], wts["lru_lambda"], name=n("lru_bwd"))
    gr["lru_wa"], gr["lru_wx"] = _diag_blocks(dwa), _diag_blocks(dwx)

    scale_b = (MLA_NOPE + MLA_ROPE) ** -0.5
    delta_b = _pair_delta(*sv["mla_ops"], dob, sv["lse_b"], sv["mla_rot"], None,
                          scale=scale_b, unit=CHUNK, name=n("mla_attn_delta"))
    dk_nope, dv_mla, dq_nope, dk_rot, dq_rot = _pair_bwd_kv(
        *sv["mla_ops"], dob, sv["lse_b_row"], delta_b, sv["mla_rot"], None,
        scale=scale_b, unit=CHUNK, name=n("mla_attn_bwd"))
    dq_rope = _rope_lanes(dq_rot, tabs["cos128"], -tabs["sin128"], cb=0, groups=HEADS, out_dtype=MXU_DTYPE,
                          name=n("q_rope_bwd"))
    dk_rope = _rope_lanes(dk_rot, tabs["cos128"], -tabs["sin128"], cb=0, groups=1, out_dtype=MXU_DTYPE,
                          sum_parts=HEAD_PAIRS, name=n("k_rope_bwd"))
    dqp = jnp.concatenate([dq_rope, dq_nope], axis=-1)
    dkv = jnp.concatenate([dk_nope, dv_mla], axis=-1)
    gr["mla_wuq"] = _mm(sv["cqn"], dqp, ta=True, name=n("wuq_dw"))
    gr["mla_wukv"] = _mm(sv["ckvn"], dkv, ta=True, name=n("wukv_dw"))
    dcqn = _mm(dqp, wts["mla_wuq"], tb=True, name=n("wuq_dx"))
    dckvn = _mm(dkv, wts["mla_wukv"], tb=True, name=n("wukv_dx"))
    dcq, gr["mla_q_norm"] = _rms_bwd(z, wts["mla_q_norm"], dcqn, width=MLA_Q_LORA, cb=SEG["cq"][3] // MLA_Q_LORA,
                                     out_dtype=MXU_DTYPE, name=n("q_norm_bwd"))
    dckv, gr["mla_kv_norm"] = _rms_bwd(z, wts["mla_kv_norm"], dckvn, width=MLA_KV_LORA,
                                       cb=SEG["ckv"][3] // MLA_KV_LORA, out_dtype=MXU_DTYPE, name=n("kv_norm_bwd"))

    scale_c = FOX_HEAD_DIM ** -0.5
    delta_c = _pair_delta(*sv["fox_ops"], doc, sv["lse_c"], None, sv["fox_decay"],
                          scale=scale_c, unit=1, name=n("fox_attn_delta"))
    dfk, dfv, dfq, dcum = _pair_bwd_kv(*sv["fox_ops"], doc, sv["lse_c_row"], delta_c, None, sv["fox_decay"],
                                       scale=scale_c, unit=1, name=n("fox_attn_bwd"))
    dcum_rows = _lane_pad(dcum.reshape(HEADS, s).T, V7X_LANES)
    dfl, dbf = _fox_cum_bwd(z, wts["fox_bf"], dcum_rows, name=n("fox_cum_bwd"))
    gr["fox_bf"] = dbf[:, :HEADS]

    zero = lambda width: jnp.zeros((s, width), MXU_DTYPE)
    dz = jnp.concatenate([dz_a, zero(128), dcq, dckv, dk_rope, zero(128), dfq, dfk, dfv, dfl, zero(384), dgl],
                         axis=-1)
    gr["w_in"] = _mm(sv["n1"], dz, ta=True, name=n("w_in_dw"))
    dn1 = _mm(dz, wts["w_in"], tb=True, name=n("w_in_dx"))
    dx, gr["mix_norm"] = _rms_bwd(sv["x"], wts["mix_norm"], dn1, width=D_MODEL, res=dx1, name=n("mix_norm_bwd"))
    return dx, gr


def _rope_tables(s):
    pos = jnp.arange(s, dtype=F32)
    inv_freq = ROPE_BASE ** (-jnp.arange(0, MLA_ROPE, 2, dtype=F32) / MLA_ROPE)
    ang = pos[:, None] * inv_freq[None, :]
    cos, sin = jnp.cos(ang), jnp.sin(ang)
    cos32 = jnp.concatenate([cos, cos], axis=-1)
    sin32 = jnp.concatenate([-sin, sin], axis=-1)
    return {"cos256": jnp.tile(cos32, (1, 8)), "sin256": jnp.tile(sin32, (1, 8)),
            "cos128": jnp.tile(cos32, (1, 4)), "sin128": jnp.tile(sin32, (1, 4))}


def _gather_weights(shards):
    names = [nm for nm, _ in SHARDED]
    got = _all_gather_many([shards[nm] if nm == "conv_w" else shards[nm].astype(MXU_DTYPE) for nm in names],
                           name="weights_all_gather")
    full = {}
    for (nm, axis), blk in zip(SHARDED, got):
        shp = shards[nm].shape
        if axis == 2:
            full[nm] = blk.transpose(1, 2, 0, 3).reshape(shp[0], shp[1], N_DEV * shp[2])
        else:
            full[nm] = blk.transpose(1, 0, 2, 3).reshape(shp[0], N_DEV * shp[1], shp[2])
    return full


def _to_dest_major(g, axis):
    d0, r, c = g.shape
    if axis == 2:
        return g.reshape(d0, r, N_DEV, c // N_DEV).transpose(2, 0, 1, 3)
    return g.reshape(d0, N_DEV, r // N_DEV, c).transpose(1, 0, 2, 3)


def kernel(x, p, mix_norm, w_in, gate_b, conv_w, conv_b, lru_wa, lru_ba, lru_wx, lru_bx, lru_lambda, mla_q_norm, mla_wuq, mla_kv_norm, mla_wukv, fox_bf, w_br_a, w_br_b, w_br_c, w_o, ffn_norm, w_gate_up, w_down, ple_norm, w_ple_gate, w_ple, final_norm, loss_target, m_mix_norm, m_w_in, m_gate_b, m_conv_w, m_conv_b, m_lru_wa, m_lru_ba, m_lru_wx, m_lru_bx, m_lru_lambda, m_mla_q_norm, m_mla_wuq, m_mla_kv_norm, m_mla_wukv, m_fox_bf, m_w_br_a, m_w_br_b, m_w_br_c, m_w_o, m_ffn_norm, m_w_gate_up, m_w_down, m_ple_norm, m_w_ple_gate, m_w_ple, m_final_norm, v_mix_norm, v_w_in, v_gate_b, v_conv_w, v_conv_b, v_lru_wa, v_lru_ba, v_lru_wx, v_lru_bx, v_lru_lambda, v_mla_q_norm, v_mla_wuq, v_mla_kv_norm, v_mla_wukv, v_fox_bf, v_w_br_a, v_w_br_b, v_w_br_c, v_w_o, v_ffn_norm, v_w_gate_up, v_w_down, v_ple_norm, v_w_ple_gate, v_w_ple, v_final_norm):
    given = dict(locals())
    w_loc = {nm: given[nm] for nm in WEIGHTS}
    m_loc = {nm: given["m_" + nm] for nm in WEIGHTS}
    v_loc = {nm: given["v_" + nm] for nm in WEIGHTS}
    xs = x[0]
    s = xs.shape[0]
    tabs = _rope_tables(s)

    full = _gather_weights({nm: w_loc[nm] for nm, _ in SHARDED})
    full["w_in"] = _pad_w_in(full["w_in"])
    wq = full["mla_wuq"].reshape(DEPTH, MLA_Q_LORA, HEADS, MLA_NOPE + MLA_ROPE)
    wq_rot = jnp.pad(wq[..., MLA_NOPE:], ((0, 0), (0, 0), (0, 0), (0, V7X_LANES - MLA_ROPE)))
    full["mla_wuq"] = jnp.concatenate([wq_rot.reshape(DEPTH, MLA_Q_LORA, -1),
                                       wq[..., :MLA_NOPE].reshape(DEPTH, MLA_Q_LORA, -1)], axis=-1)
    wkv = full["mla_wukv"].reshape(DEPTH, MLA_KV_LORA, HEADS, MLA_NOPE + MLA_V)
    full["mla_wukv"] = jnp.concatenate([wkv[..., :MLA_NOPE].reshape(DEPTH, MLA_KV_LORA, -1),
                                        wkv[..., MLA_NOPE:].reshape(DEPTH, MLA_KV_LORA, -1)], axis=-1)

    def layer_weights(i):
        wts = {nm: full[nm][i] for nm, _ in SHARDED}
        for nm in ("mix_norm", "gate_b", "conv_b", "lru_ba", "lru_bx", "lru_lambda", "mla_q_norm", "mla_kv_norm",
                   "ffn_norm", "ple_norm"):
            wts[nm] = w_loc[nm][i][None, :]
        wts["fox_bf"] = _lane_pad(w_loc["fox_bf"][i][None, :], V7X_LANES)
        wts["lru_wa"] = _block_diag(w_loc["lru_wa"][i]).astype(MXU_DTYPE)
        wts["lru_wx"] = _block_diag(w_loc["lru_wx"][i]).astype(MXU_DTYPE)
        return wts

    layers = [layer_weights(i) for i in range(DEPTH)]

    h = xs
    saved = []
    for i in range(DEPTH):
        h, sv = _layer_fwd(h, p[i, 0].astype(MXU_DTYPE), layers[i], tabs, f"l{i}")
        saved.append(sv)
    loss_blk, dh, dg_final = _final_loss(h, w_loc["final_norm"][None, :], loss_target[0], name="final_loss")
    loss = lax.psum(loss_blk[0, 0], ("x", "y", "c"))

    grads = [None] * DEPTH
    for i in reversed(range(DEPTH)):
        dh, grads[i] = _layer_bwd(dh, saved[i], layers[i], tabs, f"l{i}")
    grad_x = dh[None]

    def stacked(nm):
        return jnp.stack([grads[i][nm] for i in range(DEPTH)])

    gfull = {}
    for nm, _ in SHARDED:
        gfull[nm] = stacked(nm)
    gfull["w_in"] = _unpad_w_in(gfull["w_in"])
    gq = gfull["mla_wuq"]
    rot_w = HEADS * V7X_LANES
    gfull["mla_wuq"] = jnp.concatenate(
        [gq[..., rot_w:].reshape(DEPTH, MLA_Q_LORA, HEADS, MLA_NOPE),
         gq[..., :rot_w].reshape(DEPTH, MLA_Q_LORA, HEADS, V7X_LANES)[..., :MLA_ROPE]],
        axis=-1).reshape(DEPTH, MLA_Q_LORA, -1)
    gkv = gfull["mla_wukv"]
    gfull["mla_wukv"] = jnp.concatenate(
        [gkv[..., :512].reshape(DEPTH, MLA_KV_LORA, HEADS, MLA_NOPE),
         gkv[..., 512:].reshape(DEPTH, MLA_KV_LORA, HEADS, MLA_V)], axis=-1).reshape(DEPTH, MLA_KV_LORA, -1)

    parts = _all_to_all_many([_to_dest_major(gfull[nm], ax).astype(MXU_DTYPE) for nm, ax in SHARDED],
                             name="grads_all_to_all")
    res_s = [{}, {}, {}, {}]
    for (nm, _), part in zip(SHARDED, parts):
        outs = _adamw_nd(part, w_loc[nm], m_loc[nm], v_loc[nm], name=f"adamw_{nm}")
        for kind in range(4):
            res_s[kind][nm] = outs[kind]

    small = {nm: stacked(nm) for nm in REPLICATED if nm != "final_norm"}
    small["final_norm"] = dg_final
    names_r = list(REPLICATED)
    shapes_r = [w_loc[nm].shape for nm in names_r]
    parts_r = _all_gather(_flat_rows([small[nm] for nm in names_r], 8), name="small_grads_all_gather")
    outs_r = _adamw(parts_r, _flat_rows([w_loc[nm] for nm in names_r], 8),
                    _flat_rows([m_loc[nm] for nm in names_r], 8),
                    _flat_rows([v_loc[nm] for nm in names_r], 8), name="adamw_replicated")
    res_r = [dict(zip(names_r, _split_flat(o, shapes_r))) for o in outs_r]

    out = [loss, grad_x]
    for kind in range(4):
        for nm in WEIGHTS:
            out.append(res_s[kind][nm] if nm in res_s[kind] else res_r[kind][nm])
    return tuple(out)
```

```python
import functools
import math

import jax
import jax.numpy as jnp
from jax import lax
from jax.experimental import pallas as pl
from jax.experimental.pallas import tpu as pltpu

F32 = jnp.float32
BF16 = jnp.bfloat16
MXU_DTYPE = jnp.bfloat16

D_MODEL = 1024
DEPTH = 2
CHUNK = 64
EPS = 1e-6
NEG_INF = -1e30
LRU_WIDTH = 512
LRU_HEADS = 8
LRU_HEAD_DIM = 64
CONV_WIDTH = 4
LRU_C = 8.0
HEADS = 8
MLA_Q_LORA = 384
MLA_KV_LORA = 256
MLA_NOPE = 64
MLA_ROPE = 32
MLA_V = 64
ROPE_BASE = 10000.0
FOX_HEAD_DIM = 64
FOX_WIDTH = 512
D_FF = 2816
PLE_DIM = 256
D_IN = 6312
ADAM_LR = 0.001
ADAM_B1 = 0.9
ADAM_B2 = 0.999
ADAM_EPS = 1e-08
ADAM_WD = 0.01
ADAM_STEP = 10

V7X_VMEM_BYTES = 64 * 1024 * 1024
V7X_LANES = 128
V7X_SUBLANES = 8
VMEM_LIMIT_CAP = 56 * 1024 * 1024
N_DEV = 8

SEGS = (
    ("u", 0, 512, 0, 512),
    ("ug", 512, 512, 512, 512),
    ("cq", 1024, 384, 1152, 384),
    ("ckv", 1408, 256, 1536, 256),
    ("kr", 1664, 32, 1792, 128),
    ("fq", 1696, 512, 2048, 512),
    ("fk", 2208, 512, 2560, 512),
    ("fv", 2720, 512, 3072, 512),
    ("fl", 3232, 8, 3584, 128),
    ("gate", 3240, 3072, 4096, 1024),
)
D_IN_PAD = 7168
SEG = {s[0]: s for s in SEGS}

SHARDED = (("w_in", 2), ("mla_wuq", 2), ("mla_wukv", 2), ("w_br_a", 2), ("w_br_b", 2), ("w_br_c", 2),
           ("w_o", 1), ("w_gate_up", 2), ("w_down", 1), ("w_ple_gate", 1), ("w_ple", 2), ("conv_w", 2))
REPLICATED = ("mix_norm", "gate_b", "conv_b", "lru_wa", "lru_ba", "lru_wx", "lru_bx", "lru_lambda",
              "mla_q_norm", "mla_kv_norm", "fox_bf", "ffn_norm", "ple_norm", "final_norm")
WEIGHTS = ("mix_norm", "w_in", "gate_b", "conv_w", "conv_b", "lru_wa", "lru_ba", "lru_wx", "lru_bx",
           "lru_lambda", "mla_q_norm", "mla_wuq", "mla_kv_norm", "mla_wukv", "fox_bf", "w_br_a", "w_br_b",
           "w_br_c", "w_o", "ffn_norm", "w_gate_up", "w_down", "ple_norm", "w_ple_gate", "w_ple", "final_norm")
PAYLOAD_LANES = 1024


def _tile(n, cap=1024):
    best = None
    for t in range(V7X_LANES, min(n, cap) + 1, V7X_LANES):
        if n % t == 0:
            best = t
    return best if best is not None else n


def _row_tile(s, pref):
    t = min(pref, s // 2)
    assert s % t == 0 and t % V7X_SUBLANES == 0
    return t


def _nbytes(shape, dtype):
    return math.prod(shape) * jnp.dtype(dtype).itemsize


def _params(sem, vmem_bytes):
    limit = int(min(VMEM_LIMIT_CAP, max(16 * 1024 * 1024, vmem_bytes)))
    return pltpu.CompilerParams(dimension_semantics=sem, vmem_limit_bytes=limit)


def _full(shape):
    return pl.BlockSpec(shape, lambda *_: (0,) * len(shape))


def _rows(t, w, cb=0):
    return pl.BlockSpec((t, w), lambda i: (i, cb))


def _mxu(v):
    return v.astype(MXU_DTYPE)


def _dot(a, b):
    return lax.dot_general(_mxu(a), _mxu(b), (((1,), (0,)), ((), ())), preferred_element_type=F32)


def _dot_nt(a, b):
    return lax.dot_general(_mxu(a), _mxu(b), (((1,), (1,)), ((), ())), preferred_element_type=F32)


def _dot_tn(a, b):
    return lax.dot_general(_mxu(a), _mxu(b), (((0,), (0,)), ((), ())), preferred_element_type=F32)


def _sigmoid(v):
    return 1.0 / (1.0 + jnp.exp(-v))


def _softplus(v):
    return jnp.maximum(v, 0.0) + jnp.log(1.0 + jnp.exp(-jnp.abs(v)))


def _neg_expm1(v):
    series = -v * (1.0 + v * (0.5 + v * (1.0 / 6.0 + v * (1.0 / 24.0))))
    return jnp.where(v > -0.03, series, 1.0 - jnp.exp(v))


_GELU_C = math.sqrt(2.0 / math.pi)
_GELU_A = 0.044715


def _gelu(v):
    t = jnp.tanh(_GELU_C * (v + _GELU_A * v * v * v))
    return 0.5 * v * (1.0 + t)


def _gelu_grad(v):
    t = jnp.tanh(_GELU_C * (v + _GELU_A * v * v * v))
    return 0.5 * (1.0 + t) + 0.5 * v * (1.0 - t * t) * _GELU_C * (1.0 + 3.0 * _GELU_A * v * v)


def _mm(a, b, *, ta=False, tb=False, out_dtype=F32, res=None, also_mxu=False, name):
    k_dim, m_dim = (a.shape[0], a.shape[1]) if ta else (a.shape[1], a.shape[0])
    n_dim = b.shape[0] if tb else b.shape[1]
    assert (b.shape[1] if tb else b.shape[0]) == k_dim
    tm, tn, tk = _tile(m_dim), _tile(n_dim, 1408), _tile(k_dim, 1408)
    nk = k_dim // tk
    a_spec = pl.BlockSpec((tk, tm), lambda i, j, k: (k, i)) if ta else pl.BlockSpec((tm, tk), lambda i, j, k: (i, k))
    b_spec = pl.BlockSpec((tn, tk), lambda i, j, k: (j, k)) if tb else pl.BlockSpec((tk, tn), lambda i, j, k: (k, j))
    o_spec = pl.BlockSpec((tm, tn), lambda i, j, k: (i, j))
    has_res = res is not None

    def body(*refs):
        a_ref, b_ref = refs[0], refs[1]
        res_ref = refs[2] if has_res else None
        o_ref = refs[3] if has_res else refs[2]
        o2_ref = refs[-2] if also_mxu else None
        acc_ref = refs[-1]
        k = pl.program_id(2)
        if ta:
            part = _dot_tn(a_ref[...], b_ref[...])
        elif tb:
            part = _dot_nt(a_ref[...], b_ref[...])
        else:
            part = _dot(a_ref[...], b_ref[...])

        def finish(total):
            if has_res:
                total = total + res_ref[...].astype(F32)
            o_ref[...] = total.astype(o_ref.dtype)
            if also_mxu:
                o2_ref[...] = total.astype(o2_ref.dtype)

        if nk == 1:
            finish(part)
        else:
            @pl.when(k == 0)
            def _():
                acc_ref[...] = part

            @pl.when(jnp.logical_and(k > 0, k < nk - 1))
            def _():
                acc_ref[...] += part

            @pl.when(k == nk - 1)
            def _():
                finish(acc_ref[...] + part)

    ins = [a, b] + ([res] if has_res else [])
    in_specs = [a_spec, b_spec] + ([o_spec] if has_res else [])
    acc_shape = (tm, tn) if nk > 1 else (V7X_SUBLANES, V7X_LANES)
    vmem = (2 * (_nbytes((tm, tk), a.dtype) + _nbytes((tk, tn), b.dtype) + _nbytes((tm, tn), out_dtype)
                 + (_nbytes((tm, tn), res.dtype) if has_res else 0))
            + _nbytes((tm, tk), MXU_DTYPE) + _nbytes((tk, tn), MXU_DTYPE) + 3 * _nbytes((tm, tn), F32))
    return pl.pallas_call(
        body, name=name, grid=(m_dim // tm, n_dim // tn, nk),
        in_specs=in_specs, out_specs=[o_spec, o_spec] if also_mxu else o_spec,
        out_shape=([jax.ShapeDtypeStruct((m_dim, n_dim), out_dtype), jax.ShapeDtypeStruct((m_dim, n_dim), MXU_DTYPE)]
                   if also_mxu else jax.ShapeDtypeStruct((m_dim, n_dim), out_dtype)),
        scratch_shapes=[pltpu.VMEM(acc_shape, F32)],
        compiler_params=_params(("parallel", "parallel", "arbitrary"), vmem),
    )(*ins)


def _rms_fwd(x, g, *, width, cb=0, name):
    s = x.shape[0]
    t = _tile(s, 512)

    def body(x_ref, g_ref, o_ref):
        xv = x_ref[...].astype(F32)
        r = lax.rsqrt(jnp.mean(xv * xv, axis=-1, keepdims=True) + EPS)
        o_ref[...] = (xv * r * g_ref[...]).astype(o_ref.dtype)

    return pl.pallas_call(
        body, name=name, grid=(s // t,),
        in_specs=[_rows(t, width, cb), _full((1, width))], out_specs=_rows(t, width),
        out_shape=jax.ShapeDtypeStruct((s, width), MXU_DTYPE),
        compiler_params=_params(("parallel",), 8 * _nbytes((t, width), F32)),
    )(x, g)


def _rms_bwd(x, g, dn, *, width, cb=0, res=None, out_dtype=F32, name):
    s = x.shape[0]
    t = _tile(s, 256)
    has_res = res is not None

    def body(*refs):
        x_ref, g_ref, dn_ref = refs[:3]
        res_ref = refs[3] if has_res else None
        dx_ref, dg_ref = refs[-2], refs[-1]
        xv = x_ref[...].astype(F32)
        dnv = dn_ref[...].astype(F32)
        r = lax.rsqrt(jnp.mean(xv * xv, axis=-1, keepdims=True) + EPS)
        xr = xv * r
        dng = dnv * g_ref[...]
        dx = r * dng - xr * (r * r) * jnp.mean(dng * xv, axis=-1, keepdims=True)
        if has_res:
            dx = dx + res_ref[...].astype(F32)
        dx_ref[...] = dx.astype(dx_ref.dtype)
        part = jnp.sum(dnv * xr, axis=0, keepdims=True)

        @pl.when(pl.program_id(0) == 0)
        def _():
            dg_ref[...] = part

        @pl.when(pl.program_id(0) > 0)
        def _():
            dg_ref[...] += part

    ins = [x, g, dn] + ([res] if has_res else [])
    in_specs = [_rows(t, width, cb), _full((1, width)), _rows(t, width)] + ([_rows(t, width)] if has_res else [])
    return pl.pallas_call(
        body, name=name, grid=(s // t,),
        in_specs=in_specs, out_specs=[_rows(t, width), _full((1, width))],
        out_shape=[jax.ShapeDtypeStruct((s, width), out_dtype), jax.ShapeDtypeStruct((1, width), F32)],
        compiler_params=_params(("arbitrary",), 16 * _nbytes((t, width), F32)),
    )(*ins)


def _final_loss(x, g, target, *, name):
    s, d = x.shape
    t = _tile(s, 256)

    def body(x_ref, g_ref, t_ref, loss_ref, dx_ref, dg_ref):
        xv = x_ref[...]
        r = lax.rsqrt(jnp.mean(xv * xv, axis=-1, keepdims=True) + EPS)
        xr = xv * r
        err = xr * g_ref[...] - t_ref[...]
        part_loss = 0.5 * jnp.sum(jnp.mean(err * err, axis=-1, keepdims=True), axis=0, keepdims=True)
        dnv = err * (1.0 / d)
        dng = dnv * g_ref[...]
        dx_ref[...] = r * dng - xr * (r * r) * jnp.mean(dng * xv, axis=-1, keepdims=True)
        part_dg = jnp.sum(dnv * xr, axis=0, keepdims=True)

        @pl.when(pl.program_id(0) == 0)
        def _():
            dg_ref[...] = part_dg
            loss_ref[...] = jnp.zeros(loss_ref.shape, F32) + part_loss

        @pl.when(pl.program_id(0) > 0)
        def _():
            dg_ref[...] += part_dg
            loss_ref[...] += part_loss

    return pl.pallas_call(
        body, name=name, grid=(s // t,),
        in_specs=[_rows(t, d), _full((1, d)), _rows(t, d)],
        out_specs=[_full((V7X_SUBLANES, V7X_LANES)), _rows(t, d), _full((1, d))],
        out_shape=[jax.ShapeDtypeStruct((V7X_SUBLANES, V7X_LANES), F32), jax.ShapeDtypeStruct((s, d), F32),
                   jax.ShapeDtypeStruct((1, d), F32)],
        compiler_params=_params(("arbitrary",), 16 * _nbytes((t, d), F32)),
    )(x, g, target)


def _shift_down(v, d, fill, rows):
    return jnp.where(rows >= d, pltpu.roll(v, d, 0), fill)


def _shift_up(v, d, fill, rows, t):
    return jnp.where(rows < t - d, pltpu.roll(v, t - d, 0), fill)


def _lru_gates(xc, wa_ref, ba_ref, wx_ref, bx_ref, lam_ref):
    ra = _sigmoid(_dot(xc, wa_ref[...]) + ba_ref[...])
    ig = _sigmoid(_dot(xc, wx_ref[...]) + bx_ref[...])
    sp = _softplus(-lam_ref[...])
    log_a = -LRU_C * ra * sp
    a = jnp.exp(log_a)
    s2 = _neg_expm1(2.0 * log_a)
    return ra, ig, sp, a, s2


def _conv(ubuf, cw_ref, cb_ref, t):
    big = ubuf[...]
    shifted = [pltpu.roll(big, CONV_WIDTH - 1 - k, 0)[V7X_SUBLANES:t + V7X_SUBLANES] if k < CONV_WIDTH - 1
               else big[V7X_SUBLANES:t + V7X_SUBLANES] for k in range(CONV_WIDTH)]
    xc = cb_ref[...] + shifted[0] * cw_ref[0:1, :]
    for k in range(1, CONV_WIDTH):
        xc = xc + shifted[k] * cw_ref[k:k + 1, :]
    return xc, shifted


def _lru_fwd(z, cw, cb, wa, ba, wx, bx, lam, *, name):
    s = z.shape[0]
    w = LRU_WIDTH
    t = _row_tile(s, 256)
    steps = [1 << k for k in range(int(math.log2(t)))]

    def body(u_ref, ug_ref, cw_ref, cb_ref, wa_ref, ba_ref, wx_ref, bx_ref, lam_ref, y_ref, h_ref, ubuf, hc):
        @pl.when(pl.program_id(0) == 0)
        def _():
            ubuf[0:V7X_SUBLANES, :] = jnp.zeros((V7X_SUBLANES, w), F32)
            hc[...] = jnp.zeros_like(hc)

        ubuf[V7X_SUBLANES:t + V7X_SUBLANES, :] = u_ref[...]
        xc, _ = _conv(ubuf, cw_ref, cb_ref, t)
        _, ig, _, a, s2 = _lru_gates(xc, wa_ref, ba_ref, wx_ref, bx_ref, lam_ref)
        b = jnp.sqrt(s2) * (ig * xc)
        rows = lax.broadcasted_iota(jnp.int32, (t, w), 0)
        for d in steps:
            b = a * _shift_down(b, d, 0.0, rows) + b
            a = a * _shift_down(a, d, 1.0, rows)
        h = a * hc[0:1, :] + b
        h_ref[...] = h
        y_ref[...] = (h * _gelu(ug_ref[...])).astype(y_ref.dtype)
        hc[0:1, :] = h_ref[t - 1:t, :]
        ubuf[0:V7X_SUBLANES, :] = ubuf[t:t + V7X_SUBLANES, :]

    vec = _full((1, w))
    return pl.pallas_call(
        body, name=name, grid=(s // t,),
        in_specs=[_rows(t, w, 0), _rows(t, w, 1), _full((CONV_WIDTH, w)), vec, _full((w, w)), vec, _full((w, w)),
                  vec, vec],
        out_specs=[_rows(t, w), _rows(t, w)],
        out_shape=[jax.ShapeDtypeStruct((s, w), MXU_DTYPE), jax.ShapeDtypeStruct((s, w), F32)],
        scratch_shapes=[pltpu.VMEM((t + V7X_SUBLANES, w), F32), pltpu.VMEM((V7X_SUBLANES, w), F32)],
        compiler_params=_params(("arbitrary",), 40 * _nbytes((t, w), F32)),
    )(z, z, cw, cb, wa, ba, wx, bx, lam)


def _lru_bwd(z, h, dy, cw, cb, wa, ba, wx, bx, lam, *, name):
    s = z.shape[0]
    w = LRU_WIDTH
    t = _row_tile(s, 256)
    nt = s // t
    per8 = t // V7X_SUBLANES
    steps = [1 << k for k in range(int(math.log2(t)))]

    def body(u_ref, ug_ref, h_ref, dy_ref, uprev_ref, hprev_ref, cw_ref, cb_ref, wa_ref, ba_ref, wx_ref, bx_ref,
             lam_ref, dz_ref, dcw_ref, dcb_ref, dwa_ref, dba_ref, dwx_ref, dbx_ref, dlam_ref,
             ubuf, dbuf, acar, dhcar, tmp):
        i = pl.program_id(0)
        first_tile = i == nt - 1

        @pl.when(i == 0)
        def _():
            for r in (dcw_ref, dcb_ref, dwa_ref, dba_ref, dwx_ref, dbx_ref, dlam_ref, acar, dhcar):
                r[...] = jnp.zeros_like(r)
            dbuf[t:t + V7X_SUBLANES, :] = jnp.zeros((V7X_SUBLANES, w), F32)

        keep = jnp.where(first_tile, 0.0, 1.0)
        ubuf[0:V7X_SUBLANES, :] = uprev_ref[...] * keep
        ubuf[V7X_SUBLANES:t + V7X_SUBLANES, :] = u_ref[...]
        xc, shifted = _conv(ubuf, cw_ref, cb_ref, t)
        ra, ig, sp, a, s2 = _lru_gates(xc, wa_ref, ba_ref, wx_ref, bx_ref, lam_ref)
        sq = jnp.sqrt(s2)
        gx = ig * xc
        rows = lax.broadcasted_iota(jnp.int32, (t, w), 0)
        ugv = ug_ref[...]
        dyv = dy_ref[...].astype(F32)
        hv = h_ref[...]

        acc_g = dyv * _gelu(ugv)
        acc_a = _shift_up(a, 1, acar[0:1, :], rows, t)
        for d in steps:
            acc_g = acc_a * _shift_up(acc_g, d, 0.0, rows, t) + acc_g
            acc_a = acc_a * _shift_up(acc_a, d, 1.0, rows, t)
        dh = acc_a * dhcar[0:1, :] + acc_g

        hprev = _shift_down(hv, 1, hprev_ref[V7X_SUBLANES - 1:V7X_SUBLANES, :] * keep, rows)
        d_a = dh * hprev
        d_sq = dh * gx
        d_gx = dh * sq
        d_ig = d_gx * xc
        dxc = d_gx * ig
        d_log_a = d_a * a - d_sq * (1.0 - s2) / sq
        d_ra = d_log_a * (-LRU_C * sp)
        lamv = lam_ref[...]
        dlam_ref[...] += jnp.sum(d_log_a * (-LRU_C * ra), axis=0, keepdims=True) * (-_sigmoid(-lamv))
        dpa = d_ra * ra * (1.0 - ra)
        dpx = d_ig * ig * (1.0 - ig)
        dba_ref[...] += jnp.sum(dpa, axis=0, keepdims=True)
        dbx_ref[...] += jnp.sum(dpx, axis=0, keepdims=True)
        dwa_ref[...] += _dot_tn(xc, dpa)
        dwx_ref[...] += _dot_tn(xc, dpx)
        dxc = dxc + _dot_nt(dpa, wa_ref[...]) + _dot_nt(dpx, wx_ref[...])
        dcb_ref[...] += jnp.sum(dxc, axis=0, keepdims=True)
        for k in range(CONV_WIDTH):
            dcw_ref[k:k + 1, :] += jnp.sum(dxc * shifted[k], axis=0, keepdims=True)

        dbuf[0:t, :] = dxc
        bigd = dbuf[...]
        du = dxc * cw_ref[CONV_WIDTH - 1:CONV_WIDTH, :]
        for k in range(CONV_WIDTH - 1):
            e = CONV_WIDTH - 1 - k
            du = du + pltpu.roll(bigd, t + V7X_SUBLANES - e, 0)[0:t] * cw_ref[k:k + 1, :]
        dz_ref[:, 0:w] = du.astype(dz_ref.dtype)
        dz_ref[:, w:2 * w] = (dyv * hv * _gelu_grad(ugv)).astype(dz_ref.dtype)

        dbuf[t:t + V7X_SUBLANES, :] = dbuf[0:V7X_SUBLANES, :]
        tmp[...] = a
        acar[0:1, :] = tmp[0:1, :]
        tmp[...] = dh
        dhcar[0:1, :] = tmp[0:1, :]

    vec = _full((1, w))
    rev = lambda cbk: pl.BlockSpec((t, w), lambda i: (nt - 1 - i, cbk))
    prev8 = lambda cbk: pl.BlockSpec((V7X_SUBLANES, w),
                                     lambda i: (jnp.maximum((nt - 1 - i) * per8 - 1, 0), cbk))
    return pl.pallas_call(
        body, name=name, grid=(nt,),
        in_specs=[rev(0), rev(1), rev(0), rev(0), prev8(0), prev8(0), _full((CONV_WIDTH, w)), vec, _full((w, w)),
                  vec, _full((w, w)), vec, vec],
        out_specs=[pl.BlockSpec((t, 2 * w), lambda i: (nt - 1 - i, 0)), _full((CONV_WIDTH, w)), vec,
                   _full((w, w)), vec, _full((w, w)), vec, vec],
        out_shape=[jax.ShapeDtypeStruct((s, 2 * w), MXU_DTYPE), jax.ShapeDtypeStruct((CONV_WIDTH, w), F32),
                   jax.ShapeDtypeStruct((1, w), F32), jax.ShapeDtypeStruct((w, w), F32),
                   jax.ShapeDtypeStruct((1, w), F32), jax.ShapeDtypeStruct((w, w), F32),
                   jax.ShapeDtypeStruct((1, w), F32), jax.ShapeDtypeStruct((1, w), F32)],
        scratch_shapes=[pltpu.VMEM((t + V7X_SUBLANES, w), F32), pltpu.VMEM((t + V7X_SUBLANES, w), F32),
                        pltpu.VMEM((V7X_SUBLANES, w), F32), pltpu.VMEM((V7X_SUBLANES, w), F32),
                        pltpu.VMEM((t, w), F32)],
        compiler_params=_params(("arbitrary",), 80 * _nbytes((t, w), F32)),
    )(z, z, h, dy, z, h, cw, cb, wa, ba, wx, bx, lam)


def _rope_apply(v, cos, sin, width):
    half = MLA_ROPE // 2
    lanes = lax.broadcasted_iota(jnp.int32, v.shape, 1)
    first = (lanes % MLA_ROPE) < half
    partner = jnp.where(first, pltpu.roll(v, width - half, 1), pltpu.roll(v, half, 1))
    return v * cos + partner * sin


def _rope(x, cos, sin, *, width, cb, out_dtype, sum_heads=False, name):
    s = x.shape[0]
    t = _tile(s, 512)
    out_w = V7X_LANES if sum_heads else width

    def body(x_ref, c_ref, s_ref, o_ref):
        v = x_ref[...].astype(F32)
        if sum_heads:
            v = v[:, 0:V7X_LANES] + v[:, V7X_LANES:2 * V7X_LANES]
            v = v + pltpu.roll(v, 64, 1)
            v = v + pltpu.roll(v, 32, 1)
            out = _rope_apply(v, c_ref[...], s_ref[...], V7X_LANES)
            lanes = lax.broadcasted_iota(jnp.int32, out.shape, 1)
            out = jnp.where(lanes < MLA_ROPE, out, 0.0)
        else:
            out = _rope_apply(v, c_ref[...], s_ref[...], width)
        o_ref[...] = out.astype(o_ref.dtype)

    return pl.pallas_call(
        body, name=name, grid=(s // t,),
        in_specs=[_rows(t, width, cb), _rows(t, out_w), _rows(t, out_w)], out_specs=_rows(t, out_w),
        out_shape=jax.ShapeDtypeStruct((s, out_w), out_dtype),
        compiler_params=_params(("parallel",), 12 * _nbytes((t, width), F32)),
    )(x, cos, sin)


def _visible(t, unit, transposed):
    q_idx = lax.broadcasted_iota(jnp.int32, (t, t), 1 if transposed else 0)
    k_idx = lax.broadcasted_iota(jnp.int32, (t, t), 0 if transposed else 1)
    shift = int(math.log2(unit))
    return (q_idx >> shift) >= (k_idx >> shift)


def _attn_tile(s):
    return min(512, s // 4)


def _attn_fwd(q, k, v, cq, ck, *, scale, unit, name):
    hn, s, dk = q.shape
    dv = v.shape[-1]
    t = _attn_tile(s)
    decay = cq is not None

    def body(*refs):
        q_ref, k_ref, v_ref = refs[:3]
        cq_ref, ck_ref = (refs[3], refs[4]) if decay else (None, None)
        o_ref, lse_ref = refs[-2], refs[-1]
        i = pl.program_id(1)
        qt = q_ref[0]

        def tile(j, carry, masked):
            m, l, acc = carry
            off = pl.multiple_of(j * t, t)
            kt = k_ref[0, pl.ds(off, t), :]
            vt = v_ref[0, pl.ds(off, t), :]
            sc = _dot_nt(qt, kt) * scale
            if decay:
                sc = sc + cq_ref[0] - ck_ref[0, :, pl.ds(off, t)]
            if masked:
                sc = jnp.where(_visible(t, unit, False), sc, NEG_INF)
            m_new = jnp.maximum(m, jnp.max(sc, axis=-1, keepdims=True))
            alpha = jnp.exp(m - m_new)
            pr = jnp.exp(sc - m_new)
            l = alpha * l + jnp.sum(pr, axis=-1, keepdims=True)
            acc = alpha * acc + _dot(pr, vt)
            return m_new, l, acc

        init = (jnp.full((t, 1), NEG_INF, F32), jnp.zeros((t, 1), F32), jnp.zeros((t, dv), F32))
        carry = lax.fori_loop(0, i, lambda j, c: tile(j, c, False), init)
        m, l, acc = tile(i, carry, True)
        o_ref[0] = (acc / l).astype(o_ref.dtype)
        lse_ref[0] = m + jnp.log(l)

    qs = lambda d: pl.BlockSpec((1, t, d), lambda h, i: (h, i, 0))
    whole = lambda d: pl.BlockSpec((1, s, d), lambda h, i: (h, 0, 0))
    in_specs = [qs(dk), whole(dk), whole(dv)]
    ins = [q, k, v]
    if decay:
        in_specs += [qs(1), pl.BlockSpec((1, 1, s), lambda h, i: (h, 0, 0))]
        ins += [cq, ck]
    vmem = 4 * _nbytes((s, dk + dv), q.dtype) + 10 * _nbytes((t, t), F32) + 8 * _nbytes((t, V7X_LANES), F32)
    return pl.pallas_call(
        body, name=name, grid=(hn, s // t),
        in_specs=in_specs, out_specs=[qs(dv), qs(1)],
        out_shape=[jax.ShapeDtypeStruct((hn, s, dv), MXU_DTYPE), jax.ShapeDtypeStruct((hn, s, 1), F32)],
        compiler_params=_params(("parallel", "arbitrary"), vmem),
    )(*ins)


def _attn_bwd_q(q, k, v, do, lse, cq, ck, *, scale, unit, name):
    hn, s, dk = q.shape
    dv = v.shape[-1]
    t = _attn_tile(s)
    nt = s // t
    decay = cq is not None

    def body(*refs):
        q_ref, k_ref, v_ref, do_ref, lse_ref = refs[:5]
        cq_ref, ck_ref = (refs[5], refs[6]) if decay else (None, None)
        dq_ref, dl_ref, p_sc, dp_sc = refs[-4:]
        i = pl.program_id(1)
        qt = q_ref[0]
        dot = do_ref[0]
        lse_t = lse_ref[0]

        def sweep1(j, delta, masked):
            off = pl.multiple_of(j * t, t)
            kt = k_ref[0, pl.ds(off, t), :]
            vt = v_ref[0, pl.ds(off, t), :]
            sc = _dot_nt(qt, kt) * scale
            if decay:
                sc = sc + cq_ref[0] - ck_ref[0, :, pl.ds(off, t)]
            if masked:
                sc = jnp.where(_visible(t, unit, False), sc, NEG_INF)
            pr = jnp.exp(sc - lse_t)
            dp = _dot_nt(dot, vt)
            p_sc[j] = pr
            dp_sc[j] = dp
            return delta + jnp.sum(pr * dp, axis=-1, keepdims=True)

        delta = lax.fori_loop(0, i, lambda j, c: sweep1(j, c, False), jnp.zeros((t, 1), F32))
        delta = sweep1(i, delta, True)

        def sweep2(j, dq):
            off = pl.multiple_of(j * t, t)
            ds = p_sc[j] * (dp_sc[j] - delta)
            return dq + _dot(ds, k_ref[0, pl.ds(off, t), :])

        dq = lax.fori_loop(0, i + 1, sweep2, jnp.zeros((t, dk), F32))
        dq_ref[0] = dq * scale
        dl_ref[0] = delta

    qs = lambda d: pl.BlockSpec((1, t, d), lambda h, i: (h, i, 0))
    whole = lambda d: pl.BlockSpec((1, s, d), lambda h, i: (h, 0, 0))
    in_specs = [qs(dk), whole(dk), whole(dv), qs(dv), qs(1)]
    ins = [q, k, v, do, lse]
    if decay:
        in_specs += [qs(1), pl.BlockSpec((1, 1, s), lambda h, i: (h, 0, 0))]
        ins += [cq, ck]
    vmem = (4 * _nbytes((s, dk + dv), q.dtype) + 2 * _nbytes((nt, t, t), F32) + 8 * _nbytes((t, t), F32)
            + 12 * _nbytes((t, V7X_LANES), F32))
    return pl.pallas_call(
        body, name=name, grid=(hn, nt),
        in_specs=in_specs, out_specs=[qs(dk), qs(1)],
        out_shape=[jax.ShapeDtypeStruct((hn, s, dk), F32), jax.ShapeDtypeStruct((hn, s, 1), F32)],
        scratch_shapes=[pltpu.VMEM((nt, t, t), F32), pltpu.VMEM((nt, t, t), F32)],
        compiler_params=_params(("parallel", "arbitrary"), vmem),
    )(*ins)


def _attn_bwd_kv(q, k, v, do, lse_row, delta_row, cq_row, ck, *, scale, unit, name):
    hn, s, dk = q.shape
    dv = v.shape[-1]
    t = _attn_tile(s)
    nt = s // t
    decay = ck is not None

    def body(*refs):
        k_ref, v_ref, q_ref, do_ref, lse_ref, dl_ref = refs[:6]
        ck_ref, cq_ref = (refs[6], refs[7]) if decay else (None, None)
        outs = refs[8:] if decay else refs[6:]
        dk_ref, dv_ref = outs[0], outs[1]
        j = pl.program_id(1)
        kt = k_ref[0]
        vt = v_ref[0]

        def tile(i, carry, masked):
            dk_acc, dv_acc, dc_acc = carry
            off = pl.multiple_of(i * t, t)
            qt = q_ref[0, pl.ds(off, t), :]
            dot = do_ref[0, pl.ds(off, t), :]
            sc = _dot_nt(kt, qt) * scale
            if decay:
                sc = sc + cq_ref[0, :, pl.ds(off, t)] - ck_ref[0]
            if masked:
                sc = jnp.where(_visible(t, unit, True), sc, NEG_INF)
            pr = jnp.exp(sc - lse_ref[0, :, pl.ds(off, t)])
            dv_acc = dv_acc + _dot(pr, dot)
            ds = pr * (_dot_nt(vt, dot) - dl_ref[0, :, pl.ds(off, t)])
            dk_acc = dk_acc + _dot(ds, qt)
            if decay:
                dc_acc = dc_acc + jnp.sum(ds, axis=-1, keepdims=True)
            return dk_acc, dv_acc, dc_acc

        init = (jnp.zeros((t, dk), F32), jnp.zeros((t, dv), F32), jnp.zeros((t, 1), F32))
        carry = tile(j, init, True)
        dk_acc, dv_acc, dc_acc = lax.fori_loop(j + 1, nt, lambda i, c: tile(i, c, False), carry)
        dk_ref[0] = dk_acc * scale
        dv_ref[0] = dv_acc
        if decay:
            outs[2][0] = -dc_acc

    ks = lambda d: pl.BlockSpec((1, t, d), lambda h, j: (h, j, 0))
    whole = lambda d: pl.BlockSpec((1, s, d), lambda h, j: (h, 0, 0))
    row = pl.BlockSpec((1, 1, s), lambda h, j: (h, 0, 0))
    in_specs = [ks(dk), ks(dv), whole(dk), whole(dv), row, row]
    ins = [k, v, q, do, lse_row, delta_row]
    out_specs = [ks(dk), ks(dv)]
    out_shape = [jax.ShapeDtypeStruct((hn, s, dk), F32), jax.ShapeDtypeStruct((hn, s, dv), F32)]
    if decay:
        in_specs += [ks(1), row]
        ins += [ck, cq_row]
        out_specs.append(ks(1))
        out_shape.append(jax.ShapeDtypeStruct((hn, s, 1), F32))
    vmem = 4 * _nbytes((s, dk + dv), q.dtype) + 10 * _nbytes((t, t), F32) + 12 * _nbytes((t, V7X_LANES), F32)
    return pl.pallas_call(
        body, name=name, grid=(hn, nt),
        in_specs=in_specs, out_specs=out_specs, out_shape=out_shape,
        compiler_params=_params(("parallel", "arbitrary"), vmem),
    )(*ins)


STRIP = 32
HEAD_PAIRS = HEADS // 2


def _strip_rows(t):
    return min(STRIP, t)


def _split_scale(scale, has_rope):
    if not has_rope and math.frexp(scale)[0] == 0.5:
        return scale, 1.0
    return 1.0, scale


def _pair_mask(t):
    lane = lax.broadcasted_iota(jnp.int32, (t, V7X_LANES), 1)
    return lane < (V7X_LANES // 2)


def _strip_visible(r, t, row0, unit, transposed):
    rows = lax.broadcasted_iota(jnp.int32, (r, t), 0) + row0
    cols = lax.broadcasted_iota(jnp.int32, (r, t), 1)
    shift = int(math.log2(unit))
    if transposed:
        return (cols >> shift) >= (rows >> shift)
    return (rows >> shift) >= (cols >> shift)


def _rope_lanes(x, cos, sin, *, cb, groups, out_dtype, sum_parts=0, name):
    s = cos.shape[0]
    t = _tile(s, 512)
    w = groups * V7X_LANES

    def body(x_ref, c_ref, s_ref, o_ref):
        if sum_parts:
            v = x_ref[0].astype(F32)
            for part in range(1, sum_parts):
                v = v + x_ref[part].astype(F32)
            o_ref[...] = _rope_apply(v, c_ref[...], s_ref[...], V7X_LANES).astype(o_ref.dtype)
        else:
            for g in range(groups):
                sl = slice(g * V7X_LANES, (g + 1) * V7X_LANES)
                o_ref[:, sl] = _rope_apply(x_ref[:, sl].astype(F32), c_ref[...], s_ref[...],
                                           V7X_LANES).astype(o_ref.dtype)

    x_spec = (pl.BlockSpec((sum_parts, t, V7X_LANES), lambda i: (0, i, 0)) if sum_parts else _rows(t, w, cb))
    return pl.pallas_call(
        body, name=name, grid=(s // t,),
        in_specs=[x_spec, _rows(t, V7X_LANES), _rows(t, V7X_LANES)], out_specs=_rows(t, w),
        out_shape=jax.ShapeDtypeStruct((s, w), out_dtype),
        compiler_params=_params(("parallel",), 12 * _nbytes((t, max(w, 4 * V7X_LANES)), F32)),
    )(x, cos, sin)


def _pair_fwd(q_arr, q_cb, k_arr, k_cb, v_arr, v_cb, rope, decay, *, scale, unit, name):
    s = q_arr.shape[0]
    t = _attn_tile(s)
    r = _strip_rows(t)
    has_rope, has_decay = rope is not None, decay is not None
    kw = 2 * V7X_LANES if has_rope else V7X_LANES
    q_mul, s_mul = _split_scale(scale, has_rope)

    def body(*refs):
        it = iter(refs)
        q_ref, k_ref, v_ref = next(it), next(it), next(it)
        qr_ref, kr_ref = (next(it), next(it)) if has_rope else (None, None)
        cq_ref, ck_ref = (next(it), next(it)) if has_decay else (None, None)
        o_ref, lse_ref, lser_ref = next(it), next(it), next(it)
        q_sc, s_sc, p_sc, acc_sc, mx_sc, ls_sc, tr_sc = (next(it) for _ in range(7))
        i = pl.program_id(1)
        in_a = _pair_mask(t)
        qv = q_ref[...] * q_mul
        for hd in range(2):
            q_sc[hd, :, 0:V7X_LANES] = jnp.where(in_a if hd == 0 else jnp.logical_not(in_a), qv, 0).astype(MXU_DTYPE)
            if has_rope:
                q_sc[hd, :, V7X_LANES:kw] = qr_ref[:, hd * V7X_LANES:(hd + 1) * V7X_LANES].astype(MXU_DTYPE)
        mx_sc[...] = jnp.full(mx_sc.shape, NEG_INF, F32)
        ls_sc[...] = jnp.zeros(ls_sc.shape, F32)
        acc_sc[...] = jnp.zeros(acc_sc.shape, F32)
        cq_all = [cq_ref[hd] for hd in range(2)] if has_decay else None
        chunks = t // V7X_LANES

        def keys(j):
            off = pl.multiple_of(j * t, t)
            kt = k_ref[pl.ds(off, t), :]
            if has_rope:
                kt = jnp.concatenate([kt, kr_ref[pl.ds(off, t), :]], axis=-1)
            return off, kt

        def strip_scores(hd, row0, ck_row, masked):
            sc = s_sc[hd, pl.ds(row0, r), :]
            if s_mul != 1.0:
                sc = sc * s_mul
            if has_decay:
                sc = sc + (cq_all[hd][row0:row0 + r] - ck_row)
            if masked:
                sc = jnp.where(_strip_visible(r, t, row0, unit, False), sc, NEG_INF)
            return sc

        def fold(v, op):
            out = v[:, 0:V7X_LANES]
            for ch in range(1, chunks):
                out = op(out, v[:, ch * V7X_LANES:(ch + 1) * V7X_LANES])
            return out

        def tile_max(j, masked):
            off, kt = keys(j)
            for hd in range(2):
                s_sc[hd] = _dot_nt(q_sc[hd], kt)
                ck_row = ck_ref[hd, :, pl.ds(off, t)] if has_decay else None
                for b in range(t // r):
                    rows = pl.ds(b * r, r)
                    sc = strip_scores(hd, b * r, ck_row, masked)
                    mx_sc[hd, rows, :] = jnp.maximum(mx_sc[hd, rows, :], fold(sc, jnp.maximum))

        lax.fori_loop(0, i, lambda j, c: (tile_max(j, False), c)[1], 0)
        tile_max(i, True)
        m_all = [jnp.max(mx_sc[hd], axis=-1, keepdims=True) for hd in range(2)]

        def tile_sum(j, masked):
            off, kt = keys(j)
            vt = v_ref[pl.ds(off, t), :]
            for hd in range(2):
                s_sc[hd] = _dot_nt(q_sc[hd], kt)
                ck_row = ck_ref[hd, :, pl.ds(off, t)] if has_decay else None
                for b in range(t // r):
                    row0 = b * r
                    rows = pl.ds(row0, r)
                    pr = jnp.exp(strip_scores(hd, row0, ck_row, masked) - m_all[hd][row0:row0 + r])
                    ls_sc[hd, rows, :] += fold(pr, jnp.add)
                    p_sc[hd, rows, :] = pr.astype(MXU_DTYPE)
                acc_sc[hd] += _dot(p_sc[hd], vt)

        lax.fori_loop(0, i, lambda j, c: (tile_sum(j, False), c)[1], 0)
        tile_sum(i, True)
        l_all = [jnp.sum(ls_sc[hd], axis=-1, keepdims=True) for hd in range(2)]
        o_ref[...] = jnp.where(in_a, acc_sc[0] / l_all[0], acc_sc[1] / l_all[1]).astype(o_ref.dtype)
        for hd in range(2):
            lse_col = m_all[hd] + jnp.log(l_all[hd])
            lse_ref[hd] = lse_col
            tr_sc[...] = jnp.broadcast_to(lse_col, (t, V7X_LANES)).T
            lser_ref[hd] = tr_sc[0:1, :]

    blk = lambda cb: pl.BlockSpec((t, V7X_LANES), lambda p, i: (i, cb + p))
    whole = lambda cb: pl.BlockSpec((s, V7X_LANES), lambda p, i: (0, cb + p))
    stat = pl.BlockSpec((2, t, 1), lambda p, i: (p, i, 0))
    in_specs = [blk(q_cb), whole(k_cb), whole(v_cb)]
    ins = [q_arr, k_arr, v_arr]
    if has_rope:
        in_specs += [pl.BlockSpec((t, 2 * V7X_LANES), lambda p, i: (i, p)),
                     pl.BlockSpec((s, V7X_LANES), lambda p, i: (0, 0))]
        ins += list(rope)
    if has_decay:
        in_specs += [stat, pl.BlockSpec((2, 1, s), lambda p, i: (p, 0, 0))]
        ins += list(decay)
    col = (2, t, 1)
    vmem = (6 * _nbytes((s, V7X_LANES), MXU_DTYPE) + 6 * _nbytes((t, t), F32) + 10 * _nbytes((t, V7X_LANES), F32)
            + 8 * _nbytes((2, t, V7X_LANES), F32))
    return pl.pallas_call(
        body, name=name, grid=(HEAD_PAIRS, s // t),
        in_specs=in_specs,
        out_specs=[pl.BlockSpec((t, V7X_LANES), lambda p, i: (i, p)), stat,
                   pl.BlockSpec((2, 1, t), lambda p, i: (p, 0, i))],
        out_shape=[jax.ShapeDtypeStruct((s, HEADS * 64), MXU_DTYPE), jax.ShapeDtypeStruct((HEADS, s, 1), F32),
                   jax.ShapeDtypeStruct((HEADS, 1, s), F32)],
        scratch_shapes=[pltpu.VMEM((2, t, kw), MXU_DTYPE), pltpu.VMEM((2, t, t), F32), pltpu.VMEM((2, t, t), MXU_DTYPE),
                        pltpu.VMEM((2, t, V7X_LANES), F32), pltpu.VMEM((2, t, V7X_LANES), F32),
                        pltpu.VMEM((2, t, V7X_LANES), F32), pltpu.VMEM((V7X_LANES, t), F32)],
        compiler_params=_params(("parallel", "arbitrary"), vmem),
    )(*ins)


def _pair_bwd_q(q_arr, q_cb, k_arr, k_cb, v_arr, v_cb, do, lse, rope, decay, *, scale, unit, name):
    s = q_arr.shape[0]
    t = _attn_tile(s)
    nt = s // t
    r = t
    has_rope, has_decay = rope is not None, decay is not None
    kw = 2 * V7X_LANES if has_rope else V7X_LANES

    def body(*refs):
        it = iter(refs)
        q_ref, k_ref, v_ref, do_ref, lse_ref = (next(it) for _ in range(5))
        qr_ref, kr_ref = (next(it), next(it)) if has_rope else (None, None)
        cq_ref, ck_ref = (next(it), next(it)) if has_decay else (None, None)
        dq_ref, dl_ref = next(it), next(it)
        dqr_ref = next(it) if has_rope else None
        q_sc, do_sc, p_sc, dp_sc, ds_sc, dq_sc, dl_sc, s_sc = (next(it) for _ in range(8))
        i = pl.program_id(1)
        in_a = _pair_mask(t)
        qv = q_ref[...]
        dov = do_ref[...]
        for hd in range(2):
            sel = in_a if hd == 0 else jnp.logical_not(in_a)
            q_sc[hd, :, 0:V7X_LANES] = jnp.where(sel, qv, 0).astype(MXU_DTYPE)
            if has_rope:
                q_sc[hd, :, V7X_LANES:kw] = qr_ref[:, hd * V7X_LANES:(hd + 1) * V7X_LANES].astype(MXU_DTYPE)
            do_sc[hd] = jnp.where(sel, dov, 0).astype(MXU_DTYPE)
        dl_sc[...] = jnp.zeros(dl_sc.shape, F32)
        dq_sc[...] = jnp.zeros(dq_sc.shape, F32)

        def keys(j):
            off = pl.multiple_of(j * t, t)
            kt = k_ref[pl.ds(off, t), :]
            if has_rope:
                kt = jnp.concatenate([kt, kr_ref[pl.ds(off, t), :]], axis=-1)
            return off, kt

        for hd in range(2):
            lse_all = lse_ref[hd]
            cq_all = cq_ref[hd] if has_decay else None

            def sweep1(j, masked, hd=hd, lse_all=lse_all, cq_all=cq_all):
                off, kt = keys(j)
                s_sc[...] = _dot_nt(q_sc[hd], kt)
                dp_sc[j] = _dot_nt(do_sc[hd], v_ref[pl.ds(off, t), :])
                ck_row = ck_ref[hd, :, pl.ds(off, t)] if has_decay else None
                parts = []
                for b in range(t // r):
                    row0 = b * r
                    rows = pl.ds(row0, r)
                    sc = s_sc[rows, :] * scale
                    if has_decay:
                        sc = sc + (cq_all[row0:row0 + r] - ck_row)
                    if masked:
                        sc = jnp.where(_strip_visible(r, t, row0, unit, False), sc, NEG_INF)
                    pr = jnp.exp(sc - lse_all[row0:row0 + r])
                    p_sc[j, rows, :] = pr
                    parts.append(jnp.sum(pr * dp_sc[j, rows, :], axis=-1, keepdims=True))
                dl_sc[hd] += jnp.concatenate(parts, axis=0)

            def sweep1_unmasked(j, carry, sweep1=sweep1):
                sweep1(j, False)
                return carry

            lax.fori_loop(0, i, sweep1_unmasked, 0)
            sweep1(i, True)
            dl_all = dl_sc[hd]

            def sweep2(j, carry, hd=hd, dl_all=dl_all):
                _, kt = keys(j)
                for b in range(t // r):
                    row0 = b * r
                    rows = pl.ds(row0, r)
                    ds = p_sc[j, rows, :] * (dp_sc[j, rows, :] - dl_all[row0:row0 + r])
                    ds_sc[rows, :] = ds.astype(MXU_DTYPE)
                dq_sc[hd] += _dot(ds_sc[...], kt)
                return carry

            lax.fori_loop(0, i + 1, sweep2, 0)

        dq_ref[...] = (jnp.where(in_a, dq_sc[0, :, 0:V7X_LANES], dq_sc[1, :, 0:V7X_LANES]) * scale).astype(dq_ref.dtype)
        dl_ref[...] = dl_sc[...]
        if has_rope:
            dqr_ref[:, 0:V7X_LANES] = dq_sc[0, :, V7X_LANES:kw] * scale
            dqr_ref[:, V7X_LANES:kw] = dq_sc[1, :, V7X_LANES:kw] * scale

    blk = lambda cb: pl.BlockSpec((t, V7X_LANES), lambda p, i: (i, cb + p))
    whole = lambda cb: pl.BlockSpec((s, V7X_LANES), lambda p, i: (0, cb + p))
    stat = pl.BlockSpec((2, t, 1), lambda p, i: (p, i, 0))
    in_specs = [blk(q_cb), whole(k_cb), whole(v_cb), blk(0), stat]
    ins = [q_arr, k_arr, v_arr, do, lse]
    out_specs = [blk(0), stat]
    out_shape = [jax.ShapeDtypeStruct((s, HEADS * 64), MXU_DTYPE), jax.ShapeDtypeStruct((HEADS, s, 1), F32)]
    if has_rope:
        pair_rot = pl.BlockSpec((t, 2 * V7X_LANES), lambda p, i: (i, p))
        in_specs += [pair_rot, pl.BlockSpec((s, V7X_LANES), lambda p, i: (0, 0))]
        ins += list(rope)
        out_specs.append(pair_rot)
        out_shape.append(jax.ShapeDtypeStruct((s, HEADS * V7X_LANES), F32))
    if has_decay:
        in_specs += [stat, pl.BlockSpec((2, 1, s), lambda p, i: (p, 0, 0))]
        ins += list(decay)
    vmem = (6 * _nbytes((s, V7X_LANES), MXU_DTYPE) + 2 * _nbytes((nt, t, t), F32) + 6 * _nbytes((t, t), F32)
            + 16 * _nbytes((t, kw), F32))
    return pl.pallas_call(
        body, name=name, grid=(HEAD_PAIRS, nt),
        in_specs=in_specs, out_specs=out_specs, out_shape=out_shape,
        scratch_shapes=[pltpu.VMEM((2, t, kw), MXU_DTYPE), pltpu.VMEM((2, t, V7X_LANES), MXU_DTYPE),
                        pltpu.VMEM((nt, t, t), F32), pltpu.VMEM((nt, t, t), F32), pltpu.VMEM((t, t), MXU_DTYPE),
                        pltpu.VMEM((2, t, kw), F32), pltpu.VMEM((2, t, 1), F32), pltpu.VMEM((t, t), F32)],
        compiler_params=_params(("parallel", "arbitrary"), vmem),
    )(*ins)


def _pair_delta(q_arr, q_cb, k_arr, k_cb, v_arr, v_cb, do, lse, rope, decay, *, scale, unit, name):
    s = q_arr.shape[0]
    t = _attn_tile(s)
    r = _strip_rows(t)
    has_rope, has_decay = rope is not None, decay is not None
    kw = 2 * V7X_LANES if has_rope else V7X_LANES
    chunks = t // V7X_LANES
    q_mul, s_mul = _split_scale(scale, has_rope)

    def body(*refs):
        it = iter(refs)
        q_ref, k_ref, v_ref, do_ref, lse_ref = (next(it) for _ in range(5))
        qr_ref, kr_ref = (next(it), next(it)) if has_rope else (None, None)
        cq_ref, ck_ref = (next(it), next(it)) if has_decay else (None, None)
        dl_ref = next(it)
        q_sc, do_sc, s_sc, dp_sc, acc_sc = (next(it) for _ in range(5))
        i = pl.program_id(1)
        in_a = _pair_mask(t)
        qv, dov = q_ref[...] * q_mul, do_ref[...]
        for hd in range(2):
            sel = in_a if hd == 0 else jnp.logical_not(in_a)
            q_sc[hd, :, 0:V7X_LANES] = jnp.where(sel, qv, 0).astype(MXU_DTYPE)
            if has_rope:
                q_sc[hd, :, V7X_LANES:kw] = qr_ref[:, hd * V7X_LANES:(hd + 1) * V7X_LANES].astype(MXU_DTYPE)
            do_sc[hd] = jnp.where(sel, dov, 0).astype(MXU_DTYPE)
        acc_sc[...] = jnp.zeros(acc_sc.shape, F32)
        lse_all = [lse_ref[hd] for hd in range(2)]
        cq_all = [cq_ref[hd] for hd in range(2)] if has_decay else None

        def fold_add(v):
            out = v[:, 0:V7X_LANES]
            for ch in range(1, chunks):
                out = out + v[:, ch * V7X_LANES:(ch + 1) * V7X_LANES]
            return out

        def tile(j, masked):
            off = pl.multiple_of(j * t, t)
            kt = k_ref[pl.ds(off, t), :]
            if has_rope:
                kt = jnp.concatenate([kt, kr_ref[pl.ds(off, t), :]], axis=-1)
            vt = v_ref[pl.ds(off, t), :]
            for hd in range(2):
                s_sc[hd] = _dot_nt(q_sc[hd], kt)
                dp_sc[hd] = _dot_nt(do_sc[hd], vt)
                ck_row = ck_ref[hd, :, pl.ds(off, t)] if has_decay else None
                for b in range(t // r):
                    row0 = b * r
                    rows = pl.ds(row0, r)
                    sc = s_sc[hd, rows, :]
                    if s_mul != 1.0:
                        sc = sc * s_mul
                    if has_decay:
                        sc = sc + (cq_all[hd][row0:row0 + r] - ck_row)
                    if masked:
                        sc = jnp.where(_strip_visible(r, t, row0, unit, False), sc, NEG_INF)
                    pr = jnp.exp(sc - lse_all[hd][row0:row0 + r])
                    acc_sc[hd, rows, :] += fold_add(pr * dp_sc[hd, rows, :])

        lax.fori_loop(0, i, lambda j, c: (tile(j, False), c)[1], 0)
        tile(i, True)
        for hd in range(2):
            dl_ref[hd] = jnp.sum(acc_sc[hd].T, axis=0, keepdims=True)

    blk = lambda cb: pl.BlockSpec((t, V7X_LANES), lambda p, i: (i, cb + p))
    whole = lambda cb: pl.BlockSpec((s, V7X_LANES), lambda p, i: (0, cb + p))
    stat = pl.BlockSpec((2, t, 1), lambda p, i: (p, i, 0))
    in_specs = [blk(q_cb), whole(k_cb), whole(v_cb), blk(0), stat]
    ins = [q_arr, k_arr, v_arr, do, lse]
    if has_rope:
        in_specs += [pl.BlockSpec((t, 2 * V7X_LANES), lambda p, i: (i, p)),
                     pl.BlockSpec((s, V7X_LANES), lambda p, i: (0, 0))]
        ins += list(rope)
    if has_decay:
        in_specs += [stat, pl.BlockSpec((2, 1, s), lambda p, i: (p, 0, 0))]
        ins += list(decay)
    vmem = (6 * _nbytes((s, V7X_LANES), MXU_DTYPE) + 8 * _nbytes((t, t), F32) + 12 * _nbytes((t, kw), F32))
    return pl.pallas_call(
        body, name=name, grid=(HEAD_PAIRS, s // t),
        in_specs=in_specs, out_specs=pl.BlockSpec((2, 1, t), lambda p, i: (p, 0, i)),
        out_shape=jax.ShapeDtypeStruct((HEADS, 1, s), F32),
        scratch_shapes=[pltpu.VMEM((2, t, kw), MXU_DTYPE), pltpu.VMEM((2, t, V7X_LANES), MXU_DTYPE),
                        pltpu.VMEM((2, t, t), F32), pltpu.VMEM((2, t, t), F32),
                        pltpu.VMEM((2, t, V7X_LANES), F32)],
        compiler_params=_params(("parallel", "arbitrary"), vmem),
    )(*ins)


def _pair_bwd_kv(q_arr, q_cb, k_arr, k_cb, v_arr, v_cb, do, lse_row, delta_row, rope, decay, *, scale, unit, name):
    s = q_arr.shape[0]
    t = _attn_tile(s)
    nt = s // t
    r = _strip_rows(t)
    has_rope, has_decay = rope is not None, decay is not None
    kw = 2 * V7X_LANES if has_rope else V7X_LANES
    q_mul, s_mul = _split_scale(scale, has_rope)

    def body(*refs):
        it = iter(refs)
        k_ref, v_ref, q_ref, do_ref, lse_ref, dl_ref = (next(it) for _ in range(6))
        qr_ref, kr_ref = (next(it), next(it)) if has_rope else (None, None)
        ck_ref, cq_ref = (next(it), next(it)) if has_decay else (None, None)
        dk_ref, dv_ref, dq_ref = next(it), next(it), next(it)
        dkr_ref, dqr_ref = (next(it), next(it)) if has_rope else (None, None)
        dc_ref = next(it) if has_decay else None
        k_sc, v_sc, st_sc, dpt_sc, pt_sc, dst_sc, dk_sc, dv_sc, dc_sc, dqt_sc, kt_sc = (next(it) for _ in range(11))
        j = pl.program_id(1)
        in_a = _pair_mask(t)
        kv_, vv_ = k_ref[...], v_ref[...]
        for hd in range(2):
            sel = in_a if hd == 0 else jnp.logical_not(in_a)
            k_sc[hd, :, 0:V7X_LANES] = jnp.where(sel, kv_, 0).astype(MXU_DTYPE)
            if has_rope:
                k_sc[hd, :, V7X_LANES:kw] = kr_ref[...].astype(MXU_DTYPE)
            v_sc[hd] = jnp.where(sel, vv_, 0).astype(MXU_DTYPE)
        dk_sc[...] = jnp.zeros(dk_sc.shape, F32)
        dv_sc[...] = jnp.zeros(dv_sc.shape, F32)
        dc_sc[...] = jnp.zeros(dc_sc.shape, F32)
        k_all = kv_.astype(F32)
        if has_rope:
            k_all = jnp.concatenate([k_all, kr_ref[...].astype(F32)], axis=-1)
        kt_sc[...] = k_all.T.astype(MXU_DTYPE)

        @pl.when(j == 0)
        def _():
            dqt_sc[...] = jnp.zeros(dqt_sc.shape, F32)

        def tile(i, masked):
            off = pl.multiple_of(i * t, t)
            qt = q_ref[pl.ds(off, t), :] * q_mul
            dot = do_ref[pl.ds(off, t), :]
            for hd in range(2):
                qcat = qt
                if has_rope:
                    qcat = jnp.concatenate([qt, qr_ref[pl.ds(off, t), hd * V7X_LANES:(hd + 1) * V7X_LANES]], axis=-1)
                st_sc[hd] = _dot_nt(k_sc[hd], qcat)
                dpt_sc[hd] = _dot_nt(v_sc[hd], dot)
                lse_r = lse_ref[hd, :, pl.ds(off, t)]
                dl_r = dl_ref[hd, :, pl.ds(off, t)]
                cq_r = cq_ref[hd, :, pl.ds(off, t)] if has_decay else None
                ck_all = ck_ref[hd] if has_decay else None
                parts = []
                for b in range(t // r):
                    row0 = b * r
                    rows = pl.ds(row0, r)
                    sc = st_sc[hd, rows, :]
                    if s_mul != 1.0:
                        sc = sc * s_mul
                    if has_decay:
                        sc = sc + (cq_r - ck_all[row0:row0 + r])
                    if masked:
                        sc = jnp.where(_strip_visible(r, t, row0, unit, True), sc, NEG_INF)
                    pr = jnp.exp(sc - lse_r)
                    ds = pr * (dpt_sc[hd, rows, :] - dl_r)
                    pt_sc[hd, rows, :] = pr.astype(MXU_DTYPE)
                    dst_sc[hd, rows, :] = ds.astype(MXU_DTYPE)
                    if has_decay:
                        parts.append(jnp.sum(ds, axis=-1, keepdims=True))
                if has_decay:
                    dc_sc[hd] += jnp.concatenate(parts, axis=0)
                dv_sc[hd] += _dot(pt_sc[hd], dot)
                dk_sc[hd] += _dot(dst_sc[hd], qcat)
                dqt_sc[hd, :, pl.ds(off, t)] += _dot(kt_sc[...], dst_sc[hd])

        tile(j, True)

        def unmasked(i, carry):
            tile(i, False)
            return carry

        lax.fori_loop(j + 1, nt, unmasked, 0)
        dk_ref[...] = (jnp.where(in_a, dk_sc[0, :, 0:V7X_LANES], dk_sc[1, :, 0:V7X_LANES]) * s_mul).astype(dk_ref.dtype)
        dv_ref[...] = jnp.where(in_a, dv_sc[0], dv_sc[1]).astype(dv_ref.dtype)
        if has_rope:
            dkr_ref[0] = (dk_sc[0, :, V7X_LANES:kw] + dk_sc[1, :, V7X_LANES:kw]) * s_mul
        if has_decay:
            dc_ref[...] = -dc_sc[...]
        own = pl.ds(pl.multiple_of(j * t, t), t)
        dq_a = dqt_sc[0, :, own].T * scale
        dq_b = dqt_sc[1, :, own].T * scale
        dq_ref[...] = jnp.where(in_a, dq_a[:, 0:V7X_LANES], dq_b[:, 0:V7X_LANES]).astype(dq_ref.dtype)
        if has_rope:
            dqr_ref[:, 0:V7X_LANES] = dq_a[:, V7X_LANES:kw]
            dqr_ref[:, V7X_LANES:kw] = dq_b[:, V7X_LANES:kw]

    blk = lambda cb: pl.BlockSpec((t, V7X_LANES), lambda p, j: (j, cb + p))
    whole = lambda cb: pl.BlockSpec((s, V7X_LANES), lambda p, j: (0, cb + p))
    stat = pl.BlockSpec((2, t, 1), lambda p, j: (p, j, 0))
    row = pl.BlockSpec((2, 1, s), lambda p, j: (p, 0, 0))
    in_specs = [blk(k_cb), blk(v_cb), whole(q_cb), whole(0), row, row]
    ins = [k_arr, v_arr, q_arr, do, lse_row, delta_row]
    out_specs = [blk(0), blk(0), blk(0)]
    out_shape = [jax.ShapeDtypeStruct((s, HEADS * 64), MXU_DTYPE)] * 3
    if has_rope:
        in_specs += [pl.BlockSpec((s, 2 * V7X_LANES), lambda p, j: (0, p)),
                     pl.BlockSpec((t, V7X_LANES), lambda p, j: (j, 0))]
        ins += list(rope)
        out_specs += [pl.BlockSpec((1, t, V7X_LANES), lambda p, j: (p, j, 0)),
                      pl.BlockSpec((t, 2 * V7X_LANES), lambda p, j: (j, p))]
        out_shape += [jax.ShapeDtypeStruct((HEAD_PAIRS, s, V7X_LANES), F32),
                      jax.ShapeDtypeStruct((s, HEADS * V7X_LANES), F32)]
    if has_decay:
        in_specs += [stat, row]
        ins += list(decay)
        out_specs.append(stat)
        out_shape.append(jax.ShapeDtypeStruct((HEADS, s, 1), F32))
    vmem = (12 * _nbytes((s, V7X_LANES), MXU_DTYPE) + 8 * _nbytes((t, t), F32) + 16 * _nbytes((t, kw), F32)
            + _nbytes((2, kw, s), F32))
    return pl.pallas_call(
        body, name=name, grid=(HEAD_PAIRS, nt),
        in_specs=in_specs, out_specs=out_specs, out_shape=out_shape,
        scratch_shapes=[pltpu.VMEM((2, t, kw), MXU_DTYPE), pltpu.VMEM((2, t, V7X_LANES), MXU_DTYPE),
                        pltpu.VMEM((2, t, t), F32), pltpu.VMEM((2, t, t), F32), pltpu.VMEM((2, t, t), MXU_DTYPE),
                        pltpu.VMEM((2, t, t), MXU_DTYPE), pltpu.VMEM((2, t, kw), F32),
                        pltpu.VMEM((2, t, V7X_LANES), F32), pltpu.VMEM((2, t, 1), F32),
                        pltpu.VMEM((2, kw, s), F32), pltpu.VMEM((kw, t), MXU_DTYPE)],
        compiler_params=_params(("arbitrary", "arbitrary"), vmem),
    )(*ins)


def _fox_cum(z, bf, *, name):
    s = z.shape[0]
    w = V7X_LANES
    t = _row_tile(s, 512)
    steps = [1 << k for k in range(int(math.log2(t)))]
    cb = SEG["fl"][3] // w

    def body(f_ref, bf_ref, c_ref, car):
        @pl.when(pl.program_id(0) == 0)
        def _():
            car[...] = jnp.zeros_like(car)

        acc = -_softplus(-(f_ref[...] + bf_ref[...]))
        rows = lax.broadcasted_iota(jnp.int32, (t, w), 0)
        for d in steps:
            acc = acc + _shift_down(acc, d, 0.0, rows)
        c_ref[...] = acc + car[0:1, :]
        car[0:1, :] = c_ref[t - 1:t, :]

    return pl.pallas_call(
        body, name=name, grid=(s // t,),
        in_specs=[_rows(t, w, cb), _full((1, w))], out_specs=_rows(t, w),
        out_shape=jax.ShapeDtypeStruct((s, w), F32),
        scratch_shapes=[pltpu.VMEM((V7X_SUBLANES, w), F32)],
        compiler_params=_params(("arbitrary",), 16 * _nbytes((t, w), F32)),
    )(z, bf)


def _fox_cum_bwd(z, bf, dcum, *, name):
    s = z.shape[0]
    w = V7X_LANES
    t = _row_tile(s, 512)
    nt = s // t
    steps = [1 << k for k in range(int(math.log2(t)))]
    cb = SEG["fl"][3] // w

    def body(f_ref, bf_ref, dc_ref, df_ref, dbf_ref, car, tmp):
        @pl.when(pl.program_id(0) == 0)
        def _():
            car[...] = jnp.zeros_like(car)
            dbf_ref[...] = jnp.zeros_like(dbf_ref)

        acc = dc_ref[...]
        rows = lax.broadcasted_iota(jnp.int32, (t, w), 0)
        for d in steps:
            acc = acc + _shift_up(acc, d, 0.0, rows, t)
        dlf = acc + car[0:1, :]
        tmp[...] = dlf
        car[0:1, :] = tmp[0:1, :]
        df = dlf * _sigmoid(-(f_ref[...] + bf_ref[...]))
        df_ref[...] = df.astype(df_ref.dtype)
        dbf_ref[...] += jnp.sum(df, axis=0, keepdims=True)

    rev = lambda cbk: pl.BlockSpec((t, w), lambda i: (nt - 1 - i, cbk))
    return pl.pallas_call(
        body, name=name, grid=(nt,),
        in_specs=[rev(cb), _full((1, w)), rev(0)], out_specs=[rev(0), _full((1, w))],
        out_shape=[jax.ShapeDtypeStruct((s, w), MXU_DTYPE), jax.ShapeDtypeStruct((1, w), F32)],
        scratch_shapes=[pltpu.VMEM((V7X_SUBLANES, w), F32), pltpu.VMEM((t, w), F32)],
        compiler_params=_params(("arbitrary",), 16 * _nbytes((t, w), F32)),
    )(z, bf, dcum)


_GATE_CB = SEG["gate"][3] // D_MODEL


def _merge_fwd(ya, yb, yc, z, gate_b, *, name):
    s = ya.shape[0]
    d = D_MODEL
    t = _tile(s, 256)

    def body(ya_ref, yb_ref, yc_ref, g0_ref, g1_ref, g2_ref, gb_ref, o_ref):
        out = _sigmoid(g0_ref[...] + gb_ref[:, 0:d]) * ya_ref[...]
        out = out + _sigmoid(g1_ref[...] + gb_ref[:, d:2 * d]) * yb_ref[...]
        out = out + _sigmoid(g2_ref[...] + gb_ref[:, 2 * d:3 * d]) * yc_ref[...]
        o_ref[...] = out.astype(o_ref.dtype)

    return pl.pallas_call(
        body, name=name, grid=(s // t,),
        in_specs=[_rows(t, d)] * 3 + [_rows(t, d, _GATE_CB + b) for b in range(3)] + [_full((1, 3 * d))],
        out_specs=_rows(t, d), out_shape=jax.ShapeDtypeStruct((s, d), MXU_DTYPE),
        compiler_params=_params(("parallel",), 20 * _nbytes((t, d), F32)),
    )(ya, yb, yc, z, z, z, gate_b)


def _merge_bwd(dm, ya, yb, yc, z, gate_b, *, name):
    s = ya.shape[0]
    d = D_MODEL
    t = _tile(s, 256)

    def body(dm_ref, ya_ref, yb_ref, yc_ref, g0_ref, g1_ref, g2_ref, gb_ref, da_ref, db_ref, dc_ref, dgl_ref,
             dgb_ref):
        dmv = dm_ref[...]
        parts = []
        for b, (y_ref, g_ref, dy_ref) in enumerate(((ya_ref, g0_ref, da_ref), (yb_ref, g1_ref, db_ref),
                                                    (yc_ref, g2_ref, dc_ref))):
            gate = _sigmoid(g_ref[...] + gb_ref[:, b * d:(b + 1) * d])
            dy_ref[...] = (dmv * gate).astype(dy_ref.dtype)
            dgl = dmv * y_ref[...] * gate * (1.0 - gate)
            dgl_ref[:, b * d:(b + 1) * d] = dgl.astype(dgl_ref.dtype)
            parts.append(jnp.sum(dgl, axis=0, keepdims=True))

        @pl.when(pl.program_id(0) == 0)
        def _():
            for b, part in enumerate(parts):
                dgb_ref[:, b * d:(b + 1) * d] = part

        @pl.when(pl.program_id(0) > 0)
        def _():
            for b, part in enumerate(parts):
                dgb_ref[:, b * d:(b + 1) * d] += part

    return pl.pallas_call(
        body, name=name, grid=(s // t,),
        in_specs=[_rows(t, d)] * 4 + [_rows(t, d, _GATE_CB + b) for b in range(3)] + [_full((1, 3 * d))],
        out_specs=[_rows(t, d)] * 3 + [_rows(t, 3 * d), _full((1, 3 * d))],
        out_shape=[jax.ShapeDtypeStruct((s, d), MXU_DTYPE)] * 3
        + [jax.ShapeDtypeStruct((s, 3 * d), MXU_DTYPE), jax.ShapeDtypeStruct((1, 3 * d), F32)],
        compiler_params=_params(("arbitrary",), 36 * _nbytes((t, d), F32)),
    )(dm, ya, yb, yc, z, z, z, gate_b)


def _swiglu_fwd(hf, *, name):
    s = hf.shape[0]
    t = _tile(s, 256)

    def body(g_ref, u_ref, o_ref):
        gv = g_ref[...]
        o_ref[...] = (gv * _sigmoid(gv) * u_ref[...]).astype(o_ref.dtype)

    return pl.pallas_call(
        body, name=name, grid=(s // t,),
        in_specs=[_rows(t, D_FF, 0), _rows(t, D_FF, 1)], out_specs=_rows(t, D_FF),
        out_shape=jax.ShapeDtypeStruct((s, D_FF), MXU_DTYPE),
        compiler_params=_params(("parallel",), 10 * _nbytes((t, D_FF), F32)),
    )(hf, hf)


def _swiglu_bwd(hf, dact, *, name):
    s = hf.shape[0]
    t = _tile(s, 256)

    def body(g_ref, u_ref, da_ref, o_ref):
        gv = g_ref[...]
        dav = da_ref[...]
        sg = _sigmoid(gv)
        o_ref[:, 0:D_FF] = (dav * u_ref[...] * sg * (1.0 + gv * (1.0 - sg))).astype(o_ref.dtype)
        o_ref[:, D_FF:2 * D_FF] = (dav * gv * sg).astype(o_ref.dtype)

    return pl.pallas_call(
        body, name=name, grid=(s // t,),
        in_specs=[_rows(t, D_FF, 0), _rows(t, D_FF, 1), _rows(t, D_FF)], out_specs=_rows(t, 2 * D_FF),
        out_shape=jax.ShapeDtypeStruct((s, 2 * D_FF), MXU_DTYPE),
        compiler_params=_params(("parallel",), 14 * _nbytes((t, D_FF), F32)),
    )(hf, hf, dact)


def _ple_fwd(x, lg, pe, *, name):
    s, d = x.shape
    t = _tile(s, 512)

    def body(x_ref, lg_ref, pe_ref, o_ref):
        o_ref[...] = x_ref[...] + _sigmoid(lg_ref[...]) * pe_ref[...]

    return pl.pallas_call(
        body, name=name, grid=(s // t,),
        in_specs=[_rows(t, d)] * 3, out_specs=_rows(t, d), out_shape=jax.ShapeDtypeStruct((s, d), F32),
        compiler_params=_params(("parallel",), 12 * _nbytes((t, d), F32)),
    )(x, lg, pe)


def _ple_bwd(dx, lg, pe, *, name):
    s, d = dx.shape
    t = _tile(s, 512)

    def body(dx_ref, lg_ref, pe_ref, dpe_ref, dlg_ref):
        dxv = dx_ref[...]
        sg = _sigmoid(lg_ref[...])
        dpe_ref[...] = (dxv * sg).astype(dpe_ref.dtype)
        dlg_ref[...] = (dxv * pe_ref[...] * sg * (1.0 - sg)).astype(dlg_ref.dtype)

    return pl.pallas_call(
        body, name=name, grid=(s // t,),
        in_specs=[_rows(t, d)] * 3, out_specs=[_rows(t, d)] * 2,
        out_shape=[jax.ShapeDtypeStruct((s, d), MXU_DTYPE)] * 2,
        compiler_params=_params(("parallel",), 14 * _nbytes((t, d), F32)),
    )(dx, lg, pe)


def _adamw(parts, w, m, v, *, name):
    rows, lanes = w.shape
    t = math.gcd(rows, 160)
    assert rows % t == 0 and t % V7X_SUBLANES == 0
    c1 = 1.0 / (1.0 - ADAM_B1 ** ADAM_STEP)
    c2 = 1.0 / (1.0 - ADAM_B2 ** ADAM_STEP)

    def body(p_ref, w_ref, m_ref, v_ref, g_ref, d_ref, nm_ref, nv_ref):
        g = p_ref[0].astype(F32)
        for j in range(1, N_DEV):
            g = g + p_ref[j].astype(F32)
        m2 = ADAM_B1 * m_ref[...] + (1.0 - ADAM_B1) * g
        v2 = ADAM_B2 * v_ref[...] + (1.0 - ADAM_B2) * (g * g)
        g_ref[...] = g
        nm_ref[...] = m2
        nv_ref[...] = v2
        d_ref[...] = -ADAM_LR * ((m2 * c1) / (jnp.sqrt(v2 * c2) + ADAM_EPS) + ADAM_WD * w_ref[...])

    blk = _rows(t, lanes)
    return pl.pallas_call(
        body, name=name, grid=(rows // t,),
        in_specs=[pl.BlockSpec((N_DEV, t, lanes), lambda i: (0, i, 0)), blk, blk, blk], out_specs=[blk] * 4,
        out_shape=[jax.ShapeDtypeStruct((rows, lanes), F32)] * 4,
        compiler_params=_params(("parallel",), 40 * _nbytes((t, lanes), F32)),
    )(parts, w, m, v)


def _mesh_pos():
    return lax.axis_index("x"), lax.axis_index("y"), lax.axis_index("c")


def _all_gather(blk, *, name):
    r, c_dim = blk.shape

    def body(x_ref, out_ref, send_sems, recv_sems, local_sem):
        x, y, c = _mesh_pos()
        me, sibling = (x, y, c), (x, y, 1 - c)
        chips = [(1 - x, y), (x, 1 - y), (1 - x, 1 - y)]

        def slot(px, py, pc):
            return out_ref.at[4 * px + 2 * py + pc]

        def copy(k, block, to, src=None):
            return pltpu.make_async_remote_copy(
                src_ref=slot(*block) if src is None else src, dst_ref=slot(*block),
                send_sem=send_sems.at[k], recv_sem=recv_sems.at[k],
                device_id=to, device_id_type=pl.DeviceIdType.MESH)

        mine = pltpu.make_async_copy(x_ref, slot(*me), local_sem)
        mine.start()
        first = [copy(0, me, sibling, src=x_ref)]
        first += [copy(1 + j, me, (*chip, c), src=x_ref) for j, chip in enumerate(chips)]
        for cp in first:
            cp.start()
        passed = [copy(4 + j, (*chip, c), sibling) for j, chip in enumerate(chips)]
        for j, chip in enumerate(chips):
            copy(1 + j, (*chip, c), me).wait_recv()
            passed[j].start()
        copy(0, sibling, me).wait_recv()
        for j, chip in enumerate(chips):
            copy(4 + j, (*chip, 1 - c), me).wait_recv()
        for cp in first + passed:
            cp.wait_send()
        mine.wait()

    return pl.pallas_call(
        body, name=name,
        out_shape=jax.ShapeDtypeStruct((N_DEV, r, c_dim), blk.dtype),
        in_specs=[pl.BlockSpec(memory_space=pl.ANY)], out_specs=pl.BlockSpec(memory_space=pl.ANY),
        scratch_shapes=[pltpu.SemaphoreType.DMA((7,)), pltpu.SemaphoreType.DMA((7,)), pltpu.SemaphoreType.DMA],
    )(blk)


def _all_to_all(pay, *, name):
    _, r, c_dim = pay.shape

    def body(in_ref, out_ref, send_sems, recv_sems, local_sem):
        x, y, c = _mesh_pos()
        me = 4 * x + 2 * y + c
        local = pltpu.make_async_copy(in_ref.at[me], out_ref.at[me], local_sem)
        local.start()
        copies = []
        for k in range(1, N_DEV):
            px = 1 - x if k & 4 else x
            py = 1 - y if k & 2 else y
            pc = 1 - c if k & 1 else c
            copies.append(pltpu.make_async_remote_copy(
                src_ref=in_ref.at[4 * px + 2 * py + pc], dst_ref=out_ref.at[me],
                send_sem=send_sems.at[k - 1], recv_sem=recv_sems.at[k - 1],
                device_id=(px, py, pc), device_id_type=pl.DeviceIdType.MESH))
        for cp in copies:
            cp.start()
        for cp in copies:
            cp.wait()
        local.wait()

    return pl.pallas_call(
        body, name=name,
        out_shape=jax.ShapeDtypeStruct((N_DEV, r, c_dim), pay.dtype),
        in_specs=[pl.BlockSpec(memory_space=pl.ANY)], out_specs=pl.BlockSpec(memory_space=pl.ANY),
        scratch_shapes=[pltpu.SemaphoreType.DMA((7,)), pltpu.SemaphoreType.DMA((7,)), pltpu.SemaphoreType.DMA],
    )(pay)


def _all_gather_many(blocks, *, name):
    n = len(blocks)

    def body(*refs):
        x_refs, out_refs = refs[:n], refs[n:2 * n]
        send_sems, recv_sems, local_sems = refs[2 * n:]
        x, y, c = _mesh_pos()
        me, sibling = (x, y, c), (x, y, 1 - c)
        chips = [(1 - x, y), (x, 1 - y), (1 - x, 1 - y)]

        def slot(a, px, py, pc):
            return out_refs[a].at[4 * px + 2 * py + pc]

        def copy(k, a, block, to, src=None):
            return pltpu.make_async_remote_copy(
                src_ref=slot(a, *block) if src is None else src, dst_ref=slot(a, *block),
                send_sem=send_sems.at[k, a], recv_sem=recv_sems.at[k, a],
                device_id=to, device_id_type=pl.DeviceIdType.MESH)

        mine = [pltpu.make_async_copy(x_refs[a], slot(a, *me), local_sems.at[a]) for a in range(n)]
        for cp in mine:
            cp.start()
        first = [copy(0, a, me, sibling, src=x_refs[a]) for a in range(n)]
        first += [copy(1 + j, a, me, (*chip, c), src=x_refs[a]) for j, chip in enumerate(chips) for a in range(n)]
        for cp in first:
            cp.start()
        passed = []
        for j, chip in enumerate(chips):
            for a in range(n):
                copy(1 + j, a, (*chip, c), me).wait_recv()
                fwd = copy(4 + j, a, (*chip, c), sibling)
                fwd.start()
                passed.append(fwd)
        for a in range(n):
            copy(0, a, sibling, me).wait_recv()
        for j, chip in enumerate(chips):
            for a in range(n):
                copy(4 + j, a, (*chip, 1 - c), me).wait_recv()
        for cp in first + passed:
            cp.wait_send()
        for cp in mine:
            cp.wait()

    any_spec = pl.BlockSpec(memory_space=pl.ANY)
    return pl.pallas_call(
        body, name=name,
        out_shape=[jax.ShapeDtypeStruct((N_DEV,) + b.shape, b.dtype) for b in blocks],
        in_specs=[any_spec] * n, out_specs=[any_spec] * n,
        scratch_shapes=[pltpu.SemaphoreType.DMA((7, n)), pltpu.SemaphoreType.DMA((7, n)),
                        pltpu.SemaphoreType.DMA((n,))],
    )(*blocks)


def _all_to_all_many(pays, *, name):
    n = len(pays)

    def body(*refs):
        in_refs, out_refs = refs[:n], refs[n:2 * n]
        send_sems, recv_sems, local_sems = refs[2 * n:]
        x, y, c = _mesh_pos()
        me = 4 * x + 2 * y + c
        local = [pltpu.make_async_copy(in_refs[a].at[me], out_refs[a].at[me], local_sems.at[a]) for a in range(n)]
        for cp in local:
            cp.start()
        copies = []
        for k in range(1, N_DEV):
            px = 1 - x if k & 4 else x
            py = 1 - y if k & 2 else y
            pc = 1 - c if k & 1 else c
            for a in range(n):
                copies.append(pltpu.make_async_remote_copy(
                    src_ref=in_refs[a].at[4 * px + 2 * py + pc], dst_ref=out_refs[a].at[me],
                    send_sem=send_sems.at[k - 1, a], recv_sem=recv_sems.at[k - 1, a],
                    device_id=(px, py, pc), device_id_type=pl.DeviceIdType.MESH))
        for cp in copies:
            cp.start()
        for cp in copies:
            cp.wait()
        for cp in local:
            cp.wait()

    any_spec = pl.BlockSpec(memory_space=pl.ANY)
    return pl.pallas_call(
        body, name=name,
        out_shape=[jax.ShapeDtypeStruct(p.shape, p.dtype) for p in pays],
        in_specs=[any_spec] * n, out_specs=[any_spec] * n,
        scratch_shapes=[pltpu.SemaphoreType.DMA((7, n)), pltpu.SemaphoreType.DMA((7, n)),
                        pltpu.SemaphoreType.DMA((n,))],
    )(*pays)


CHIPS = 4


def _sibling_exchange_many(pays, *, name):
    n = len(pays)

    def body(*refs):
        in_refs, mine_refs, theirs_refs = refs[:n], refs[n:2 * n], refs[2 * n:3 * n]
        send_sems, recv_sems, local_sems = refs[3 * n:]
        x, y, c = _mesh_pos()
        copies = []
        for a in range(n):
            copies.append(pltpu.make_async_copy(in_refs[a].at[pl.ds(CHIPS * c, CHIPS)], mine_refs[a],
                                                local_sems.at[a]))
            copies.append(pltpu.make_async_remote_copy(
                src_ref=in_refs[a].at[pl.ds(CHIPS * (1 - c), CHIPS)], dst_ref=theirs_refs[a],
                send_sem=send_sems.at[a], recv_sem=recv_sems.at[a],
                device_id=(x, y, 1 - c), device_id_type=pl.DeviceIdType.MESH))
        for cp in copies:
            cp.start()
        for cp in copies:
            cp.wait()

    any_spec = pl.BlockSpec(memory_space=pl.ANY)
    half = [jax.ShapeDtypeStruct((CHIPS,) + p.shape[1:], p.dtype) for p in pays]
    outs = pl.pallas_call(
        body, name=name, out_shape=half + half,
        in_specs=[any_spec] * n, out_specs=[any_spec] * (2 * n),
        scratch_shapes=[pltpu.SemaphoreType.DMA((n,)), pltpu.SemaphoreType.DMA((n,)), pltpu.SemaphoreType.DMA((n,))],
    )(*pays)
    return outs[:n], outs[n:]


def _chip_exchange_many(sums, *, name):
    n = len(sums)

    def body(*refs):
        in_refs, out_refs = refs[:n], refs[n:2 * n]
        send_sems, recv_sems, local_sems = refs[2 * n:]
        x, y, c = _mesh_pos()
        me = 2 * x + y
        copies = [pltpu.make_async_copy(in_refs[a].at[me], out_refs[a].at[me], local_sems.at[a]) for a in range(n)]
        for k in range(1, CHIPS):
            px = 1 - x if k & 2 else x
            py = 1 - y if k & 1 else y
            for a in range(n):
                copies.append(pltpu.make_async_remote_copy(
                    src_ref=in_refs[a].at[2 * px + py], dst_ref=out_refs[a].at[me],
                    send_sem=send_sems.at[k - 1, a], recv_sem=recv_sems.at[k - 1, a],
                    device_id=(px, py, c), device_id_type=pl.DeviceIdType.MESH))
        for cp in copies:
            cp.start()
        for cp in copies:
            cp.wait()

    any_spec = pl.BlockSpec(memory_space=pl.ANY)
    return pl.pallas_call(
        body, name=name, out_shape=[jax.ShapeDtypeStruct(p.shape, p.dtype) for p in sums],
        in_specs=[any_spec] * n, out_specs=[any_spec] * n,
        scratch_shapes=[pltpu.SemaphoreType.DMA((CHIPS - 1, n)), pltpu.SemaphoreType.DMA((CHIPS - 1, n)),
                        pltpu.SemaphoreType.DMA((n,))],
    )(*sums)


def _shard_row_tile(rows):
    t = rows
    for cand in range(V7X_SUBLANES, min(rows, 256) + 1, V7X_SUBLANES):
        if rows % cand == 0:
            t = cand
    return t


def _pair_sum(a, b, *, name):
    n4, d0, rows, cols = a.shape
    t = _shard_row_tile(rows)

    def body(a_ref, b_ref, o_ref):
        o_ref[...] = (a_ref[...].astype(F32) + b_ref[...].astype(F32)).astype(o_ref.dtype)

    blk = pl.BlockSpec((1, 1, t, cols), lambda q, l, i: (q, l, i, 0))
    lanes = -(-cols // V7X_LANES) * V7X_LANES
    return pl.pallas_call(
        body, name=name, grid=(n4, d0, rows // t), in_specs=[blk, blk], out_specs=blk,
        out_shape=jax.ShapeDtypeStruct(a.shape, MXU_DTYPE),
        compiler_params=_params(("parallel", "parallel", "parallel"), 16 * _nbytes((max(t, 16), lanes), F32)),
    )(a, b)


def _adamw_nd(parts, w, m, v, *, name):
    n_parts = parts.shape[0]
    d0, rows, cols = w.shape
    t = _shard_row_tile(rows)
    c1 = 1.0 / (1.0 - ADAM_B1 ** ADAM_STEP)
    c2 = 1.0 / (1.0 - ADAM_B2 ** ADAM_STEP)

    def body(p_ref, w_ref, m_ref, v_ref, g_ref, d_ref, nm_ref, nv_ref):
        g = p_ref[0, 0].astype(F32)
        for j in range(1, n_parts):
            g = g + p_ref[j, 0].astype(F32)
        m2 = ADAM_B1 * m_ref[0] + (1.0 - ADAM_B1) * g
        v2 = ADAM_B2 * v_ref[0] + (1.0 - ADAM_B2) * (g * g)
        g_ref[0] = g
        nm_ref[0] = m2
        nv_ref[0] = v2
        d_ref[0] = -ADAM_LR * ((m2 * c1) / (jnp.sqrt(v2 * c2) + ADAM_EPS) + ADAM_WD * w_ref[0])

    blk = pl.BlockSpec((1, t, cols), lambda l, i: (l, i, 0))
    lanes = -(-cols // V7X_LANES) * V7X_LANES
    return pl.pallas_call(
        body, name=name, grid=(d0, rows // t),
        in_specs=[pl.BlockSpec((n_parts, 1, t, cols), lambda l, i: (0, l, i, 0)), blk, blk, blk], out_specs=[blk] * 4,
        out_shape=[jax.ShapeDtypeStruct(w.shape, F32)] * 4,
        compiler_params=_params(("parallel", "parallel"), 40 * _nbytes((max(t, 16), lanes), F32)),
    )(parts, w, m, v)


def _flat_rows(parts, row_multiple):
    flat = jnp.concatenate([p.reshape(-1) for p in parts])
    chunk = PAYLOAD_LANES * row_multiple
    total = -(-flat.shape[0] // chunk) * chunk
    return jnp.pad(flat, (0, total - flat.shape[0])).reshape(total // PAYLOAD_LANES, PAYLOAD_LANES)


def _split_flat(flat, shapes):
    out, off = [], 0
    flat = flat.reshape(-1)
    for shp in shapes:
        n = math.prod(shp)
        out.append(flat[off:off + n].reshape(shp))
        off += n
    return out


def _pad_w_in(w):
    pieces, cursor = [], 0
    for _, off, width, pad_off, _ in SEGS:
        if pad_off > cursor:
            pieces.append(jnp.zeros(w.shape[:-1] + (pad_off - cursor,), w.dtype))
        pieces.append(w[..., off:off + width])
        cursor = pad_off + width
    pieces.append(jnp.zeros(w.shape[:-1] + (D_IN_PAD - cursor,), w.dtype))
    return jnp.concatenate(pieces, axis=-1)


def _unpad_w_in(w):
    return jnp.concatenate([w[..., pad_off:pad_off + width] for _, _, width, pad_off, _ in SEGS], axis=-1)


def _heads(a, hd):
    return a.reshape(a.shape[0], HEADS, hd).transpose(1, 0, 2)


def _unheads(a):
    return a.transpose(1, 0, 2).reshape(a.shape[1], -1)


def _block_diag(w):
    eye = jnp.eye(LRU_HEADS, dtype=w.dtype)
    return (eye[:, None, :, None] * w[:, :, None, :]).reshape(LRU_WIDTH, LRU_WIDTH)


def _diag_blocks(w):
    w4 = w.reshape(LRU_HEADS, LRU_HEAD_DIM, LRU_HEADS, LRU_HEAD_DIM)
    return jnp.stack([w4[h, :, h, :] for h in range(LRU_HEADS)])


def _lane_pad(a, width):
    return jnp.pad(a, ((0, 0), (0, width - a.shape[-1])))


def _layer_fwd(x, p_i, wts, tabs, tag):
    n = functools.partial(lambda base, t=tag: f"{base}_{t}")
    sv = {"x": x}
    n1 = _rms_fwd(x, wts["mix_norm"], width=D_MODEL, name=n("mix_norm_fwd"))
    z, z16 = _mm(n1, wts["w_in"], also_mxu=True, name=n("w_in_fwd"))
    sv.update(n1=n1, z=z, z16=z16)
    lanes = V7X_LANES

    ya_pre, hseq = _lru_fwd(z, wts["conv_w"], wts["conv_b"], wts["lru_wa"], wts["lru_ba"], wts["lru_wx"],
                            wts["lru_bx"], wts["lru_lambda"], name=n("lru_fwd"))
    ya = _mm(ya_pre, wts["w_br_a"], name=n("br_a_fwd"))
    sv.update(ya_pre=ya_pre, hseq=hseq, ya=ya)

    cqn = _rms_fwd(z, wts["mla_q_norm"], width=MLA_Q_LORA, cb=SEG["cq"][3] // MLA_Q_LORA, name=n("q_norm_fwd"))
    ckvn = _rms_fwd(z, wts["mla_kv_norm"], width=MLA_KV_LORA, cb=SEG["ckv"][3] // MLA_KV_LORA,
                    name=n("kv_norm_fwd"))
    qp, qp16 = _mm(cqn, wts["mla_wuq"], also_mxu=True, name=n("wuq_fwd"))
    kv = _mm(ckvn, wts["mla_wukv"], out_dtype=MXU_DTYPE, name=n("wukv_fwd"))
    q_rot = _rope_lanes(qp, tabs["cos128"], tabs["sin128"], cb=0, groups=HEADS, out_dtype=MXU_DTYPE,
                        name=n("q_rope_fwd"))
    k_rot = _rope_lanes(z, tabs["cos128"], tabs["sin128"], cb=SEG["kr"][3] // lanes, groups=1, out_dtype=MXU_DTYPE,
                        name=n("k_rope_fwd"))
    mla_ops = (qp16, HEADS, kv, 0, kv, HEAD_PAIRS)
    ob_flat, lse_b, lse_b_row = _pair_fwd(*mla_ops, (q_rot, k_rot), None, scale=(MLA_NOPE + MLA_ROPE) ** -0.5,
                                          unit=CHUNK, name=n("mla_attn_fwd"))
    yb = _mm(ob_flat, wts["w_br_b"], name=n("br_b_fwd"))
    sv.update(cqn=cqn, ckvn=ckvn, mla_ops=mla_ops, mla_rot=(q_rot, k_rot), lse_b=lse_b, lse_b_row=lse_b_row,
              ob_flat=ob_flat, yb=yb)

    cum = _fox_cum(z, wts["fox_bf"], name=n("fox_cum_fwd"))
    cum_h = cum[:, :HEADS].T
    fox_decay = (cum_h[:, :, None], cum_h[:, None, :])
    fox_ops = (z16, SEG["fq"][3] // lanes, z16, SEG["fk"][3] // lanes, z16, SEG["fv"][3] // lanes)
    oc_flat, lse_c, lse_c_row = _pair_fwd(*fox_ops, None, fox_decay, scale=FOX_HEAD_DIM ** -0.5, unit=1,
                                          name=n("fox_attn_fwd"))
    yc = _mm(oc_flat, wts["w_br_c"], name=n("br_c_fwd"))
    sv.update(fox_ops=fox_ops, fox_decay=fox_decay, lse_c=lse_c, lse_c_row=lse_c_row, oc_flat=oc_flat, yc=yc)

    merged = _merge_fwd(ya, yb, yc, z, wts["gate_b"], name=n("merge_fwd"))
    x1 = _mm(merged, wts["w_o"], res=x, name=n("w_o_fwd"))
    n2 = _rms_fwd(x1, wts["ffn_norm"], width=D_MODEL, name=n("ffn_norm_fwd"))
    hf = _mm(n2, wts["w_gate_up"], name=n("gate_up_fwd"))
    act = _swiglu_fwd(hf, name=n("swiglu_fwd"))
    x2 = _mm(act, wts["w_down"], res=x1, name=n("down_fwd"))
    n3 = _rms_fwd(x2, wts["ple_norm"], width=D_MODEL, name=n("ple_norm_fwd"))
    lg = _mm(n3, wts["w_ple_gate"], name=n("ple_gate_fwd"))
    pe = _mm(p_i, wts["w_ple"], name=n("ple_fwd_mm"))
    x3 = _ple_fwd(x2, lg, pe, name=n("ple_fwd"))
    sv.update(merged=merged, x1=x1, n2=n2, hf=hf, act=act, x2=x2, n3=n3, lg=lg, pe=pe, p_i=p_i)
    return x3, sv


def _layer_bwd(dx3, sv, wts, tabs, tag):
    n = functools.partial(lambda base, t=tag: f"{base}_{t}")
    gr = {}
    z = sv["z"]
    s = z.shape[0]

    dpe, dlg = _ple_bwd(dx3, sv["lg"], sv["pe"], name=n("ple_bwd"))
    gr["w_ple"] = _mm(sv["p_i"], dpe, ta=True, name=n("ple_dw"))
    gr["w_ple_gate"] = _mm(sv["n3"], dlg, ta=True, name=n("ple_gate_dw"))
    dn3 = _mm(dlg, wts["w_ple_gate"], tb=True, name=n("ple_gate_dx"))
    dx2, gr["ple_norm"] = _rms_bwd(sv["x2"], wts["ple_norm"], dn3, width=D_MODEL, res=dx3, name=n("ple_norm_bwd"))

    dact = _mm(dx2, wts["w_down"], tb=True, name=n("down_dx"))
    gr["w_down"] = _mm(sv["act"], dx2, ta=True, name=n("down_dw"))
    dhf = _swiglu_bwd(sv["hf"], dact, name=n("swiglu_bwd"))
    gr["w_gate_up"] = _mm(sv["n2"], dhf, ta=True, name=n("gate_up_dw"))
    dn2 = _mm(dhf, wts["w_gate_up"], tb=True, name=n("gate_up_dx"))
    dx1, gr["ffn_norm"] = _rms_bwd(sv["x1"], wts["ffn_norm"], dn2, width=D_MODEL, res=dx2, name=n("ffn_norm_bwd"))

    dmerged = _mm(dx1, wts["w_o"], tb=True, name=n("w_o_dx"))
    gr["w_o"] = _mm(sv["merged"], dx1, ta=True, name=n("w_o_dw"))
    dya, dyb, dyc, dgl, gr["gate_b"] = _merge_bwd(dmerged, sv["ya"], sv["yb"], sv["yc"], z, wts["gate_b"],
                                                  name=n("merge_bwd"))
    gr["w_br_a"] = _mm(sv["ya_pre"], dya, ta=True, name=n("br_a_dw"))
    gr["w_br_b"] = _mm(sv["ob_flat"], dyb, ta=True, name=n("br_b_dw"))
    gr["w_br_c"] = _mm(sv["oc_flat"], dyc, ta=True, name=n("br_c_dw"))
    dya_pre = _mm(dya, wts["w_br_a"], tb=True, name=n("br_a_dx"))
    dob = _mm(dyb, wts["w_br_b"], tb=True, out_dtype=MXU_DTYPE, name=n("br_b_dx"))
    doc = _mm(dyc, wts["w_br_c"], tb=True, out_dtype=MXU_DTYPE, name=n("br_c_dx"))

    (dz_a, gr["conv_w"], gr["conv_b"], dwa, gr["lru_ba"], dwx, gr["lru_bx"], gr["lru_lambda"]) = _lru_bwd(
        z, sv["hseq"], dya_pre, wts["conv_w"], wts["conv_b"], wts["lru_wa"], wts["lru_ba"], wts["lru_wx"],
        wts["lru_bx"], wts["lru_lambda"], name=n("lru_bwd"))
    gr["lru_wa"], gr["lru_wx"] = _diag_blocks(dwa), _diag_blocks(dwx)

    scale_b = (MLA_NOPE + MLA_ROPE) ** -0.5
    delta_b = _pair_delta(*sv["mla_ops"], dob, sv["lse_b"], sv["mla_rot"], None,
                          scale=scale_b, unit=CHUNK, name=n("mla_attn_delta"))
    dk_nope, dv_mla, dq_nope, dk_rot, dq_rot = _pair_bwd_kv(
        *sv["mla_ops"], dob, sv["lse_b_row"], delta_b, sv["mla_rot"], None,
        scale=scale_b, unit=CHUNK, name=n("mla_attn_bwd"))
    dq_rope = _rope_lanes(dq_rot, tabs["cos128"], -tabs["sin128"], cb=0, groups=HEADS, out_dtype=MXU_DTYPE,
                          name=n("q_rope_bwd"))
    dk_rope = _rope_lanes(dk_rot, tabs["cos128"], -tabs["sin128"], cb=0, groups=1, out_dtype=MXU_DTYPE,
                          sum_parts=HEAD_PAIRS, name=n("k_rope_bwd"))
    dqp = jnp.concatenate([dq_rope, dq_nope], axis=-1)
    dkv = jnp.concatenate([dk_nope, dv_mla], axis=-1)
    gr["mla_wuq"] = _mm(sv["cqn"], dqp, ta=True, name=n("wuq_dw"))
    gr["mla_wukv"] = _mm(sv["ckvn"], dkv, ta=True, name=n("wukv_dw"))
    dcqn = _mm(dqp, wts["mla_wuq"], tb=True, name=n("wuq_dx"))
    dckvn = _mm(dkv, wts["mla_wukv"], tb=True, name=n("wukv_dx"))
    dcq, gr["mla_q_norm"] = _rms_bwd(z, wts["mla_q_norm"], dcqn, width=MLA_Q_LORA, cb=SEG["cq"][3] // MLA_Q_LORA,
                                     out_dtype=MXU_DTYPE, name=n("q_norm_bwd"))
    dckv, gr["mla_kv_norm"] = _rms_bwd(z, wts["mla_kv_norm"], dckvn, width=MLA_KV_LORA,
                                       cb=SEG["ckv"][3] // MLA_KV_LORA, out_dtype=MXU_DTYPE, name=n("kv_norm_bwd"))

    scale_c = FOX_HEAD_DIM ** -0.5
    delta_c = _pair_delta(*sv["fox_ops"], doc, sv["lse_c"], None, sv["fox_decay"],
                          scale=scale_c, unit=1, name=n("fox_attn_delta"))
    dfk, dfv, dfq, dcum = _pair_bwd_kv(*sv["fox_ops"], doc, sv["lse_c_row"], delta_c, None, sv["fox_decay"],
                                       scale=scale_c, unit=1, name=n("fox_attn_bwd"))
    dcum_rows = _lane_pad(dcum.reshape(HEADS, s).T, V7X_LANES)
    dfl, dbf = _fox_cum_bwd(z, wts["fox_bf"], dcum_rows, name=n("fox_cum_bwd"))
    gr["fox_bf"] = dbf[:, :HEADS]

    zero = lambda width: jnp.zeros((s, width), MXU_DTYPE)
    dz = jnp.concatenate([dz_a, zero(128), dcq, dckv, dk_rope, zero(128), dfq, dfk, dfv, dfl, zero(384), dgl],
                         axis=-1)
    gr["w_in"] = _mm(sv["n1"], dz, ta=True, name=n("w_in_dw"))
    dn1 = _mm(dz, wts["w_in"], tb=True, name=n("w_in_dx"))
    dx, gr["mix_norm"] = _rms_bwd(sv["x"], wts["mix_norm"], dn1, width=D_MODEL, res=dx1, name=n("mix_norm_bwd"))
    return dx, gr


def _rope_tables(s):
    pos = jnp.arange(s, dtype=F32)
    inv_freq = ROPE_BASE ** (-jnp.arange(0, MLA_ROPE, 2, dtype=F32) / MLA_ROPE)
    ang = pos[:, None] * inv_freq[None, :]
    cos, sin = jnp.cos(ang), jnp.sin(ang)
    cos32 = jnp.concatenate([cos, cos], axis=-1)
    sin32 = jnp.concatenate([-sin, sin], axis=-1)
    return {"cos256": jnp.tile(cos32, (1, 8)), "sin256": jnp.tile(sin32, (1, 8)),
            "cos128": jnp.tile(cos32, (1, 4)), "sin128": jnp.tile(sin32, (1, 4))}


def _gather_weights(shards):
    names = [nm for nm, _ in SHARDED]
    got = _all_gather_many([shards[nm] if nm == "conv_w" else shards[nm].astype(MXU_DTYPE) for nm in names],
                           name="weights_all_gather")
    full = {}
    for (nm, axis), blk in zip(SHARDED, got):
        shp = shards[nm].shape
        if axis == 2:
            full[nm] = blk.transpose(1, 2, 0, 3).reshape(shp[0], shp[1], N_DEV * shp[2])
        else:
            full[nm] = blk.transpose(1, 0, 2, 3).reshape(shp[0], N_DEV * shp[1], shp[2])
    return full


def _to_dest_major(g, axis):
    d0, r, c = g.shape
    if axis == 2:
        by_dev = g.reshape(d0, r, N_DEV, c // N_DEV).transpose(2, 0, 1, 3)
    else:
        by_dev = g.reshape(d0, N_DEV, r // N_DEV, c).transpose(1, 0, 2, 3)
    shp = by_dev.shape[1:]
    return by_dev.reshape((CHIPS, 2) + shp).transpose(1, 0, 2, 3, 4).reshape((N_DEV,) + shp)


def kernel(x, p, mix_norm, w_in, gate_b, conv_w, conv_b, lru_wa, lru_ba, lru_wx, lru_bx, lru_lambda, mla_q_norm, mla_wuq, mla_kv_norm, mla_wukv, fox_bf, w_br_a, w_br_b, w_br_c, w_o, ffn_norm, w_gate_up, w_down, ple_norm, w_ple_gate, w_ple, final_norm, loss_target, m_mix_norm, m_w_in, m_gate_b, m_conv_w, m_conv_b, m_lru_wa, m_lru_ba, m_lru_wx, m_lru_bx, m_lru_lambda, m_mla_q_norm, m_mla_wuq, m_mla_kv_norm, m_mla_wukv, m_fox_bf, m_w_br_a, m_w_br_b, m_w_br_c, m_w_o, m_ffn_norm, m_w_gate_up, m_w_down, m_ple_norm, m_w_ple_gate, m_w_ple, m_final_norm, v_mix_norm, v_w_in, v_gate_b, v_conv_w, v_conv_b, v_lru_wa, v_lru_ba, v_lru_wx, v_lru_bx, v_lru_lambda, v_mla_q_norm, v_mla_wuq, v_mla_kv_norm, v_mla_wukv, v_fox_bf, v_w_br_a, v_w_br_b, v_w_br_c, v_w_o, v_ffn_norm, v_w_gate_up, v_w_down, v_ple_norm, v_w_ple_gate, v_w_ple, v_final_norm):
    given = dict(locals())
    w_loc = {nm: given[nm] for nm in WEIGHTS}
    m_loc = {nm: given["m_" + nm] for nm in WEIGHTS}
    v_loc = {nm: given["v_" + nm] for nm in WEIGHTS}
    xs = x[0]
    s = xs.shape[0]
    tabs = _rope_tables(s)

    full = _gather_weights({nm: w_loc[nm] for nm, _ in SHARDED})
    full["w_in"] = _pad_w_in(full["w_in"])
    wq = full["mla_wuq"].reshape(DEPTH, MLA_Q_LORA, HEADS, MLA_NOPE + MLA_ROPE)
    wq_rot = jnp.pad(wq[..., MLA_NOPE:], ((0, 0), (0, 0), (0, 0), (0, V7X_LANES - MLA_ROPE)))
    full["mla_wuq"] = jnp.concatenate([wq_rot.reshape(DEPTH, MLA_Q_LORA, -1),
                                       wq[..., :MLA_NOPE].reshape(DEPTH, MLA_Q_LORA, -1)], axis=-1)
    wkv = full["mla_wukv"].reshape(DEPTH, MLA_KV_LORA, HEADS, MLA_NOPE + MLA_V)
    full["mla_wukv"] = jnp.concatenate([wkv[..., :MLA_NOPE].reshape(DEPTH, MLA_KV_LORA, -1),
                                        wkv[..., MLA_NOPE:].reshape(DEPTH, MLA_KV_LORA, -1)], axis=-1)

    def layer_weights(i):
        wts = {nm: full[nm][i] for nm, _ in SHARDED}
        for nm in ("mix_norm", "gate_b", "conv_b", "lru_ba", "lru_bx", "lru_lambda", "mla_q_norm", "mla_kv_norm",
                   "ffn_norm", "ple_norm"):
            wts[nm] = w_loc[nm][i][None, :]
        wts["fox_bf"] = _lane_pad(w_loc["fox_bf"][i][None, :], V7X_LANES)
        wts["lru_wa"] = _block_diag(w_loc["lru_wa"][i]).astype(MXU_DTYPE)
        wts["lru_wx"] = _block_diag(w_loc["lru_wx"][i]).astype(MXU_DTYPE)
        return wts

    layers = [layer_weights(i) for i in range(DEPTH)]

    h = xs
    saved = []
    for i in range(DEPTH):
        h, sv = _layer_fwd(h, p[i, 0].astype(MXU_DTYPE), layers[i], tabs, f"l{i}")
        saved.append(sv)
    loss_blk, dh, dg_final = _final_loss(h, w_loc["final_norm"][None, :], loss_target[0], name="final_loss")
    loss = lax.psum(loss_blk[0, 0], ("x", "y", "c"))

    grads = [None] * DEPTH
    for i in reversed(range(DEPTH)):
        dh, grads[i] = _layer_bwd(dh, saved[i], layers[i], tabs, f"l{i}")
    grad_x = dh[None]

    def stacked(nm):
        return jnp.stack([grads[i][nm] for i in range(DEPTH)])

    gfull = {}
    for nm, _ in SHARDED:
        gfull[nm] = stacked(nm)
    gfull["w_in"] = _unpad_w_in(gfull["w_in"])
    gq = gfull["mla_wuq"]
    rot_w = HEADS * V7X_LANES
    gfull["mla_wuq"] = jnp.concatenate(
        [gq[..., rot_w:].reshape(DEPTH, MLA_Q_LORA, HEADS, MLA_NOPE),
         gq[..., :rot_w].reshape(DEPTH, MLA_Q_LORA, HEADS, V7X_LANES)[..., :MLA_ROPE]],
        axis=-1).reshape(DEPTH, MLA_Q_LORA, -1)
    gkv = gfull["mla_wukv"]
    gfull["mla_wukv"] = jnp.concatenate(
        [gkv[..., :512].reshape(DEPTH, MLA_KV_LORA, HEADS, MLA_NOPE),
         gkv[..., 512:].reshape(DEPTH, MLA_KV_LORA, HEADS, MLA_V)], axis=-1).reshape(DEPTH, MLA_KV_LORA, -1)

    mine, theirs = _sibling_exchange_many([_to_dest_major(gfull[nm], ax).astype(MXU_DTYPE) for nm, ax in SHARDED],
                                          name="grads_sibling_exchange")
    pair_sums = [_pair_sum(a, b, name=f"grads_pair_sum_{nm}") for (nm, _), a, b in zip(SHARDED, mine, theirs)]
    parts = _chip_exchange_many(pair_sums, name="grads_chip_exchange")
    res_s = [{}, {}, {}, {}]
    for (nm, _), part in zip(SHARDED, parts):
        outs = _adamw_nd(part, w_loc[nm], m_loc[nm], v_loc[nm], name=f"adamw_{nm}")
        for kind in range(4):
            res_s[kind][nm] = outs[kind]

    small = {nm: stacked(nm) for nm in REPLICATED if nm != "final_norm"}
    small["final_norm"] = dg_final
    names_r = list(REPLICATED)
    shapes_r = [w_loc[nm].shape for nm in names_r]
    parts_r = _all_gather(_flat_rows([small[nm] for nm in names_r], 8), name="small_grads_all_gather")
    outs_r = _adamw(parts_r, _flat_rows([w_loc[nm] for nm in names_r], 8),
                    _flat_rows([m_loc[nm] for nm in names_r], 8),
                    _flat_rows([v_loc[nm] for nm in names_r], 8), name="adamw_replicated")
    res_r = [dict(zip(names_r, _split_flat(o, shapes_r))) for o in outs_r]

    out = [loss, grad_x]
    for kind in range(4):
        for nm in WEIGHTS:
            out.append(res_s[kind][nm] if nm in res_s[kind] else res_r[kind][nm])
    return tuple(out)
```

```python
import functools
import math

import jax
import jax.numpy as jnp
from jax import lax
from jax.experimental import pallas as pl
from jax.experimental.pallas import tpu as pltpu

F32 = jnp.float32
BF16 = jnp.bfloat16
MXU_DTYPE = jnp.bfloat16

D_MODEL = 1024
DEPTH = 2
CHUNK = 64
EPS = 1e-6
NEG_INF = -1e30
LRU_WIDTH = 512
LRU_HEADS = 8
LRU_HEAD_DIM = 64
CONV_WIDTH = 4
LRU_C = 8.0
HEADS = 8
MLA_Q_LORA = 384
MLA_KV_LORA = 256
MLA_NOPE = 64
MLA_ROPE = 32
MLA_V = 64
ROPE_BASE = 10000.0
FOX_HEAD_DIM = 64
FOX_WIDTH = 512
D_FF = 2816
PLE_DIM = 256
D_IN = 6312
ADAM_LR = 0.001
ADAM_B1 = 0.9
ADAM_B2 = 0.999
ADAM_EPS = 1e-08
ADAM_WD = 0.01
ADAM_STEP = 10

V7X_VMEM_BYTES = 64 * 1024 * 1024
V7X_LANES = 128
V7X_SUBLANES = 8
VMEM_LIMIT_CAP = 56 * 1024 * 1024
N_DEV = 8

SEGS = (
    ("u", 0, 512, 0, 512),
    ("ug", 512, 512, 512, 512),
    ("cq", 1024, 384, 1152, 384),
    ("ckv", 1408, 256, 1536, 256),
    ("kr", 1664, 32, 1792, 128),
    ("fq", 1696, 512, 2048, 512),
    ("fk", 2208, 512, 2560, 512),
    ("fv", 2720, 512, 3072, 512),
    ("fl", 3232, 8, 3584, 128),
    ("gate", 3240, 3072, 4096, 1024),
)
D_IN_PAD = 7168
SEG = {s[0]: s for s in SEGS}

SHARDED = (("w_in", 2), ("mla_wuq", 2), ("mla_wukv", 2), ("w_br_a", 2), ("w_br_b", 2), ("w_br_c", 2),
           ("w_o", 1), ("w_gate_up", 2), ("w_down", 1), ("w_ple_gate", 1), ("w_ple", 2), ("conv_w", 2))
REPLICATED = ("mix_norm", "gate_b", "conv_b", "lru_wa", "lru_ba", "lru_wx", "lru_bx", "lru_lambda",
              "mla_q_norm", "mla_kv_norm", "fox_bf", "ffn_norm", "ple_norm", "final_norm")
WEIGHTS = ("mix_norm", "w_in", "gate_b", "conv_w", "conv_b", "lru_wa", "lru_ba", "lru_wx", "lru_bx",
           "lru_lambda", "mla_q_norm", "mla_wuq", "mla_kv_norm", "mla_wukv", "fox_bf", "w_br_a", "w_br_b",
           "w_br_c", "w_o", "ffn_norm", "w_gate_up", "w_down", "ple_norm", "w_ple_gate", "w_ple", "final_norm")
PAYLOAD_LANES = 1024


def _tile(n, cap=1024):
    best = None
    for t in range(V7X_LANES, min(n, cap) + 1, V7X_LANES):
        if n % t == 0:
            best = t
    return best if best is not None else n


def _row_tile(s, pref):
    t = min(pref, s // 2)
    assert s % t == 0 and t % V7X_SUBLANES == 0
    return t


def _nbytes(shape, dtype):
    return math.prod(shape) * jnp.dtype(dtype).itemsize


def _params(sem, vmem_bytes):
    limit = int(min(VMEM_LIMIT_CAP, max(16 * 1024 * 1024, vmem_bytes)))
    return pltpu.CompilerParams(dimension_semantics=sem, vmem_limit_bytes=limit)


def _full(shape):
    return pl.BlockSpec(shape, lambda *_: (0,) * len(shape))


def _rows(t, w, cb=0):
    return pl.BlockSpec((t, w), lambda i: (i, cb))


def _mxu(v):
    return v.astype(MXU_DTYPE)


def _dot(a, b):
    return lax.dot_general(_mxu(a), _mxu(b), (((1,), (0,)), ((), ())), preferred_element_type=F32)


def _dot_nt(a, b):
    return lax.dot_general(_mxu(a), _mxu(b), (((1,), (1,)), ((), ())), preferred_element_type=F32)


def _dot_tn(a, b):
    return lax.dot_general(_mxu(a), _mxu(b), (((0,), (0,)), ((), ())), preferred_element_type=F32)


def _sigmoid(v):
    return 1.0 / (1.0 + jnp.exp(-v))


def _softplus(v):
    return jnp.maximum(v, 0.0) + jnp.log(1.0 + jnp.exp(-jnp.abs(v)))


def _neg_expm1(v):
    series = -v * (1.0 + v * (0.5 + v * (1.0 / 6.0 + v * (1.0 / 24.0))))
    return jnp.where(v > -0.03, series, 1.0 - jnp.exp(v))


_GELU_C = math.sqrt(2.0 / math.pi)
_GELU_A = 0.044715


def _gelu(v):
    t = jnp.tanh(_GELU_C * (v + _GELU_A * v * v * v))
    return 0.5 * v * (1.0 + t)


def _gelu_grad(v):
    t = jnp.tanh(_GELU_C * (v + _GELU_A * v * v * v))
    return 0.5 * (1.0 + t) + 0.5 * v * (1.0 - t * t) * _GELU_C * (1.0 + 3.0 * _GELU_A * v * v)


def _mm(a, b, *, ta=False, tb=False, out_dtype=F32, res=None, also_mxu=False, name):
    k_dim, m_dim = (a.shape[0], a.shape[1]) if ta else (a.shape[1], a.shape[0])
    n_dim = b.shape[0] if tb else b.shape[1]
    assert (b.shape[1] if tb else b.shape[0]) == k_dim
    tm, tn, tk = _tile(m_dim), _tile(n_dim, 1408), _tile(k_dim, 1408)
    nk = k_dim // tk
    a_spec = pl.BlockSpec((tk, tm), lambda i, j, k: (k, i)) if ta else pl.BlockSpec((tm, tk), lambda i, j, k: (i, k))
    b_spec = pl.BlockSpec((tn, tk), lambda i, j, k: (j, k)) if tb else pl.BlockSpec((tk, tn), lambda i, j, k: (k, j))
    o_spec = pl.BlockSpec((tm, tn), lambda i, j, k: (i, j))
    has_res = res is not None

    def body(*refs):
        a_ref, b_ref = refs[0], refs[1]
        res_ref = refs[2] if has_res else None
        o_ref = refs[3] if has_res else refs[2]
        o2_ref = refs[-2] if also_mxu else None
        acc_ref = refs[-1]
        k = pl.program_id(2)
        if ta:
            part = _dot_tn(a_ref[...], b_ref[...])
        elif tb:
            part = _dot_nt(a_ref[...], b_ref[...])
        else:
            part = _dot(a_ref[...], b_ref[...])

        def finish(total):
            if has_res:
                total = total + res_ref[...].astype(F32)
            o_ref[...] = total.astype(o_ref.dtype)
            if also_mxu:
                o2_ref[...] = total.astype(o2_ref.dtype)

        if nk == 1:
            finish(part)
        else:
            @pl.when(k == 0)
            def _():
                acc_ref[...] = part

            @pl.when(jnp.logical_and(k > 0, k < nk - 1))
            def _():
                acc_ref[...] += part

            @pl.when(k == nk - 1)
            def _():
                finish(acc_ref[...] + part)

    ins = [a, b] + ([res] if has_res else [])
    in_specs = [a_spec, b_spec] + ([o_spec] if has_res else [])
    acc_shape = (tm, tn) if nk > 1 else (V7X_SUBLANES, V7X_LANES)
    vmem = (2 * (_nbytes((tm, tk), a.dtype) + _nbytes((tk, tn), b.dtype) + _nbytes((tm, tn), out_dtype)
                 + (_nbytes((tm, tn), res.dtype) if has_res else 0))
            + _nbytes((tm, tk), MXU_DTYPE) + _nbytes((tk, tn), MXU_DTYPE) + 3 * _nbytes((tm, tn), F32))
    return pl.pallas_call(
        body, name=name, grid=(m_dim // tm, n_dim // tn, nk),
        in_specs=in_specs, out_specs=[o_spec, o_spec] if also_mxu else o_spec,
        out_shape=([jax.ShapeDtypeStruct((m_dim, n_dim), out_dtype), jax.ShapeDtypeStruct((m_dim, n_dim), MXU_DTYPE)]
                   if also_mxu else jax.ShapeDtypeStruct((m_dim, n_dim), out_dtype)),
        scratch_shapes=[pltpu.VMEM(acc_shape, F32)],
        compiler_params=_params(("parallel", "parallel", "arbitrary"), vmem),
    )(*ins)


def _rms_fwd(x, g, *, width, cb=0, name):
    s = x.shape[0]
    t = _tile(s, 512)

    def body(x_ref, g_ref, o_ref):
        xv = x_ref[...].astype(F32)
        r = lax.rsqrt(jnp.mean(xv * xv, axis=-1, keepdims=True) + EPS)
        o_ref[...] = (xv * r * g_ref[...]).astype(o_ref.dtype)

    return pl.pallas_call(
        body, name=name, grid=(s // t,),
        in_specs=[_rows(t, width, cb), _full((1, width))], out_specs=_rows(t, width),
        out_shape=jax.ShapeDtypeStruct((s, width), MXU_DTYPE),
        compiler_params=_params(("parallel",), 8 * _nbytes((t, width), F32)),
    )(x, g)


def _rms_bwd(x, g, dn, *, width, cb=0, res=None, out_dtype=F32, name):
    s = x.shape[0]
    t = _tile(s, 256)
    has_res = res is not None

    def body(*refs):
        x_ref, g_ref, dn_ref = refs[:3]
        res_ref = refs[3] if has_res else None
        dx_ref, dg_ref = refs[-2], refs[-1]
        xv = x_ref[...].astype(F32)
        dnv = dn_ref[...].astype(F32)
        r = lax.rsqrt(jnp.mean(xv * xv, axis=-1, keepdims=True) + EPS)
        xr = xv * r
        dng = dnv * g_ref[...]
        dx = r * dng - xr * (r * r) * jnp.mean(dng * xv, axis=-1, keepdims=True)
        if has_res:
            dx = dx + res_ref[...].astype(F32)
        dx_ref[...] = dx.astype(dx_ref.dtype)
        part = jnp.sum(dnv * xr, axis=0, keepdims=True)

        @pl.when(pl.program_id(0) == 0)
        def _():
            dg_ref[...] = part

        @pl.when(pl.program_id(0) > 0)
        def _():
            dg_ref[...] += part

    ins = [x, g, dn] + ([res] if has_res else [])
    in_specs = [_rows(t, width, cb), _full((1, width)), _rows(t, width)] + ([_rows(t, width)] if has_res else [])
    return pl.pallas_call(
        body, name=name, grid=(s // t,),
        in_specs=in_specs, out_specs=[_rows(t, width), _full((1, width))],
        out_shape=[jax.ShapeDtypeStruct((s, width), out_dtype), jax.ShapeDtypeStruct((1, width), F32)],
        compiler_params=_params(("arbitrary",), 16 * _nbytes((t, width), F32)),
    )(*ins)


def _final_loss(x, g, target, *, name):
    s, d = x.shape
    t = _tile(s, 256)

    def body(x_ref, g_ref, t_ref, loss_ref, dx_ref, dg_ref):
        xv = x_ref[...]
        r = lax.rsqrt(jnp.mean(xv * xv, axis=-1, keepdims=True) + EPS)
        xr = xv * r
        err = xr * g_ref[...] - t_ref[...]
        part_loss = 0.5 * jnp.sum(jnp.mean(err * err, axis=-1, keepdims=True), axis=0, keepdims=True)
        dnv = err * (1.0 / d)
        dng = dnv * g_ref[...]
        dx_ref[...] = r * dng - xr * (r * r) * jnp.mean(dng * xv, axis=-1, keepdims=True)
        part_dg = jnp.sum(dnv * xr, axis=0, keepdims=True)

        @pl.when(pl.program_id(0) == 0)
        def _():
            dg_ref[...] = part_dg
            loss_ref[...] = jnp.zeros(loss_ref.shape, F32) + part_loss

        @pl.when(pl.program_id(0) > 0)
        def _():
            dg_ref[...] += part_dg
            loss_ref[...] += part_loss

    return pl.pallas_call(
        body, name=name, grid=(s // t,),
        in_specs=[_rows(t, d), _full((1, d)), _rows(t, d)],
        out_specs=[_full((V7X_SUBLANES, V7X_LANES)), _rows(t, d), _full((1, d))],
        out_shape=[jax.ShapeDtypeStruct((V7X_SUBLANES, V7X_LANES), F32), jax.ShapeDtypeStruct((s, d), F32),
                   jax.ShapeDtypeStruct((1, d), F32)],
        compiler_params=_params(("arbitrary",), 16 * _nbytes((t, d), F32)),
    )(x, g, target)


def _shift_down(v, d, fill, rows):
    return jnp.where(rows >= d, pltpu.roll(v, d, 0), fill)


def _shift_up(v, d, fill, rows, t):
    return jnp.where(rows < t - d, pltpu.roll(v, t - d, 0), fill)


def _lru_gates(xc, wa_ref, ba_ref, wx_ref, bx_ref, lam_ref):
    ra = _sigmoid(_dot(xc, wa_ref[...]) + ba_ref[...])
    ig = _sigmoid(_dot(xc, wx_ref[...]) + bx_ref[...])
    sp = _softplus(-lam_ref[...])
    log_a = -LRU_C * ra * sp
    a = jnp.exp(log_a)
    s2 = _neg_expm1(2.0 * log_a)
    return ra, ig, sp, a, s2


def _conv(ubuf, cw_ref, cb_ref, t):
    big = ubuf[...]
    shifted = [pltpu.roll(big, CONV_WIDTH - 1 - k, 0)[V7X_SUBLANES:t + V7X_SUBLANES] if k < CONV_WIDTH - 1
               else big[V7X_SUBLANES:t + V7X_SUBLANES] for k in range(CONV_WIDTH)]
    xc = cb_ref[...] + shifted[0] * cw_ref[0:1, :]
    for k in range(1, CONV_WIDTH):
        xc = xc + shifted[k] * cw_ref[k:k + 1, :]
    return xc, shifted


def _lru_fwd(z, cw, cb, wa, ba, wx, bx, lam, *, name):
    s = z.shape[0]
    w = LRU_WIDTH
    t = _row_tile(s, 256)
    steps = [1 << k for k in range(int(math.log2(t)))]

    def body(u_ref, ug_ref, cw_ref, cb_ref, wa_ref, ba_ref, wx_ref, bx_ref, lam_ref, y_ref, h_ref, ubuf, hc):
        @pl.when(pl.program_id(0) == 0)
        def _():
            ubuf[0:V7X_SUBLANES, :] = jnp.zeros((V7X_SUBLANES, w), F32)
            hc[...] = jnp.zeros_like(hc)

        ubuf[V7X_SUBLANES:t + V7X_SUBLANES, :] = u_ref[...]
        xc, _ = _conv(ubuf, cw_ref, cb_ref, t)
        _, ig, _, a, s2 = _lru_gates(xc, wa_ref, ba_ref, wx_ref, bx_ref, lam_ref)
        b = jnp.sqrt(s2) * (ig * xc)
        rows = lax.broadcasted_iota(jnp.int32, (t, w), 0)
        for d in steps:
            b = a * _shift_down(b, d, 0.0, rows) + b
            a = a * _shift_down(a, d, 1.0, rows)
        h = a * hc[0:1, :] + b
        h_ref[...] = h
        y_ref[...] = (h * _gelu(ug_ref[...])).astype(y_ref.dtype)
        hc[0:1, :] = h_ref[t - 1:t, :]
        ubuf[0:V7X_SUBLANES, :] = ubuf[t:t + V7X_SUBLANES, :]

    vec = _full((1, w))
    return pl.pallas_call(
        body, name=name, grid=(s // t,),
        in_specs=[_rows(t, w, 0), _rows(t, w, 1), _full((CONV_WIDTH, w)), vec, _full((w, w)), vec, _full((w, w)),
                  vec, vec],
        out_specs=[_rows(t, w), _rows(t, w)],
        out_shape=[jax.ShapeDtypeStruct((s, w), MXU_DTYPE), jax.ShapeDtypeStruct((s, w), F32)],
        scratch_shapes=[pltpu.VMEM((t + V7X_SUBLANES, w), F32), pltpu.VMEM((V7X_SUBLANES, w), F32)],
        compiler_params=_params(("arbitrary",), 40 * _nbytes((t, w), F32)),
    )(z, z, cw, cb, wa, ba, wx, bx, lam)


def _lru_bwd(z, h, dy, cw, cb, wa, ba, wx, bx, lam, *, name):
    s = z.shape[0]
    w = LRU_WIDTH
    t = _row_tile(s, 256)
    nt = s // t
    per8 = t // V7X_SUBLANES
    steps = [1 << k for k in range(int(math.log2(t)))]

    def body(u_ref, ug_ref, h_ref, dy_ref, uprev_ref, hprev_ref, cw_ref, cb_ref, wa_ref, ba_ref, wx_ref, bx_ref,
             lam_ref, dz_ref, dcw_ref, dcb_ref, dwa_ref, dba_ref, dwx_ref, dbx_ref, dlam_ref,
             ubuf, dbuf, acar, dhcar, tmp):
        i = pl.program_id(0)
        first_tile = i == nt - 1

        @pl.when(i == 0)
        def _():
            for r in (dcw_ref, dcb_ref, dwa_ref, dba_ref, dwx_ref, dbx_ref, dlam_ref, acar, dhcar):
                r[...] = jnp.zeros_like(r)
            dbuf[t:t + V7X_SUBLANES, :] = jnp.zeros((V7X_SUBLANES, w), F32)

        keep = jnp.where(first_tile, 0.0, 1.0)
        ubuf[0:V7X_SUBLANES, :] = uprev_ref[...] * keep
        ubuf[V7X_SUBLANES:t + V7X_SUBLANES, :] = u_ref[...]
        xc, shifted = _conv(ubuf, cw_ref, cb_ref, t)
        ra, ig, sp, a, s2 = _lru_gates(xc, wa_ref, ba_ref, wx_ref, bx_ref, lam_ref)
        sq = jnp.sqrt(s2)
        gx = ig * xc
        rows = lax.broadcasted_iota(jnp.int32, (t, w), 0)
        ugv = ug_ref[...]
        dyv = dy_ref[...].astype(F32)
        hv = h_ref[...]

        acc_g = dyv * _gelu(ugv)
        acc_a = _shift_up(a, 1, acar[0:1, :], rows, t)
        for d in steps:
            acc_g = acc_a * _shift_up(acc_g, d, 0.0, rows, t) + acc_g
            acc_a = acc_a * _shift_up(acc_a, d, 1.0, rows, t)
        dh = acc_a * dhcar[0:1, :] + acc_g

        hprev = _shift_down(hv, 1, hprev_ref[V7X_SUBLANES - 1:V7X_SUBLANES, :] * keep, rows)
        d_a = dh * hprev
        d_sq = dh * gx
        d_gx = dh * sq
        d_ig = d_gx * xc
        dxc = d_gx * ig
        d_log_a = d_a * a - d_sq * (1.0 - s2) / sq
        d_ra = d_log_a * (-LRU_C * sp)
        lamv = lam_ref[...]
        dlam_ref[...] += jnp.sum(d_log_a * (-LRU_C * ra), axis=0, keepdims=True) * (-_sigmoid(-lamv))
        dpa = d_ra * ra * (1.0 - ra)
        dpx = d_ig * ig * (1.0 - ig)
        dba_ref[...] += jnp.sum(dpa, axis=0, keepdims=True)
        dbx_ref[...] += jnp.sum(dpx, axis=0, keepdims=True)
        dwa_ref[...] += _dot_tn(xc, dpa)
        dwx_ref[...] += _dot_tn(xc, dpx)
        dxc = dxc + _dot_nt(dpa, wa_ref[...]) + _dot_nt(dpx, wx_ref[...])
        dcb_ref[...] += jnp.sum(dxc, axis=0, keepdims=True)
        for k in range(CONV_WIDTH):
            dcw_ref[k:k + 1, :] += jnp.sum(dxc * shifted[k], axis=0, keepdims=True)

        dbuf[0:t, :] = dxc
        bigd = dbuf[...]
        du = dxc * cw_ref[CONV_WIDTH - 1:CONV_WIDTH, :]
        for k in range(CONV_WIDTH - 1):
            e = CONV_WIDTH - 1 - k
            du = du + pltpu.roll(bigd, t + V7X_SUBLANES - e, 0)[0:t] * cw_ref[k:k + 1, :]
        dz_ref[:, 0:w] = du.astype(dz_ref.dtype)
        dz_ref[:, w:2 * w] = (dyv * hv * _gelu_grad(ugv)).astype(dz_ref.dtype)

        dbuf[t:t + V7X_SUBLANES, :] = dbuf[0:V7X_SUBLANES, :]
        tmp[...] = a
        acar[0:1, :] = tmp[0:1, :]
        tmp[...] = dh
        dhcar[0:1, :] = tmp[0:1, :]

    vec = _full((1, w))
    rev = lambda cbk: pl.BlockSpec((t, w), lambda i: (nt - 1 - i, cbk))
    prev8 = lambda cbk: pl.BlockSpec((V7X_SUBLANES, w),
                                     lambda i: (jnp.maximum((nt - 1 - i) * per8 - 1, 0), cbk))
    return pl.pallas_call(
        body, name=name, grid=(nt,),
        in_specs=[rev(0), rev(1), rev(0), rev(0), prev8(0), prev8(0), _full((CONV_WIDTH, w)), vec, _full((w, w)),
                  vec, _full((w, w)), vec, vec],
        out_specs=[pl.BlockSpec((t, 2 * w), lambda i: (nt - 1 - i, 0)), _full((CONV_WIDTH, w)), vec,
                   _full((w, w)), vec, _full((w, w)), vec, vec],
        out_shape=[jax.ShapeDtypeStruct((s, 2 * w), MXU_DTYPE), jax.ShapeDtypeStruct((CONV_WIDTH, w), F32),
                   jax.ShapeDtypeStruct((1, w), F32), jax.ShapeDtypeStruct((w, w), F32),
                   jax.ShapeDtypeStruct((1, w), F32), jax.ShapeDtypeStruct((w, w), F32),
                   jax.ShapeDtypeStruct((1, w), F32), jax.ShapeDtypeStruct((1, w), F32)],
        scratch_shapes=[pltpu.VMEM((t + V7X_SUBLANES, w), F32), pltpu.VMEM((t + V7X_SUBLANES, w), F32),
                        pltpu.VMEM((V7X_SUBLANES, w), F32), pltpu.VMEM((V7X_SUBLANES, w), F32),
                        pltpu.VMEM((t, w), F32)],
        compiler_params=_params(("arbitrary",), 80 * _nbytes((t, w), F32)),
    )(z, z, h, dy, z, h, cw, cb, wa, ba, wx, bx, lam)


def _rope_apply(v, cos, sin, width):
    half = MLA_ROPE // 2
    lanes = lax.broadcasted_iota(jnp.int32, v.shape, 1)
    first = (lanes % MLA_ROPE) < half
    partner = jnp.where(first, pltpu.roll(v, width - half, 1), pltpu.roll(v, half, 1))
    return v * cos + partner * sin


def _rope(x, cos, sin, *, width, cb, out_dtype, sum_heads=False, name):
    s = x.shape[0]
    t = _tile(s, 512)
    out_w = V7X_LANES if sum_heads else width

    def body(x_ref, c_ref, s_ref, o_ref):
        v = x_ref[...].astype(F32)
        if sum_heads:
            v = v[:, 0:V7X_LANES] + v[:, V7X_LANES:2 * V7X_LANES]
            v = v + pltpu.roll(v, 64, 1)
            v = v + pltpu.roll(v, 32, 1)
            out = _rope_apply(v, c_ref[...], s_ref[...], V7X_LANES)
            lanes = lax.broadcasted_iota(jnp.int32, out.shape, 1)
            out = jnp.where(lanes < MLA_ROPE, out, 0.0)
        else:
            out = _rope_apply(v, c_ref[...], s_ref[...], width)
        o_ref[...] = out.astype(o_ref.dtype)

    return pl.pallas_call(
        body, name=name, grid=(s // t,),
        in_specs=[_rows(t, width, cb), _rows(t, out_w), _rows(t, out_w)], out_specs=_rows(t, out_w),
        out_shape=jax.ShapeDtypeStruct((s, out_w), out_dtype),
        compiler_params=_params(("parallel",), 12 * _nbytes((t, width), F32)),
    )(x, cos, sin)


def _visible(t, unit, transposed):
    q_idx = lax.broadcasted_iota(jnp.int32, (t, t), 1 if transposed else 0)
    k_idx = lax.broadcasted_iota(jnp.int32, (t, t), 0 if transposed else 1)
    shift = int(math.log2(unit))
    return (q_idx >> shift) >= (k_idx >> shift)


def _attn_tile(s):
    return min(512, s // 4)


def _attn_fwd(q, k, v, cq, ck, *, scale, unit, name):
    hn, s, dk = q.shape
    dv = v.shape[-1]
    t = _attn_tile(s)
    decay = cq is not None

    def body(*refs):
        q_ref, k_ref, v_ref = refs[:3]
        cq_ref, ck_ref = (refs[3], refs[4]) if decay else (None, None)
        o_ref, lse_ref = refs[-2], refs[-1]
        i = pl.program_id(1)
        qt = q_ref[0]

        def tile(j, carry, masked):
            m, l, acc = carry
            off = pl.multiple_of(j * t, t)
            kt = k_ref[0, pl.ds(off, t), :]
            vt = v_ref[0, pl.ds(off, t), :]
            sc = _dot_nt(qt, kt) * scale
            if decay:
                sc = sc + cq_ref[0] - ck_ref[0, :, pl.ds(off, t)]
            if masked:
                sc = jnp.where(_visible(t, unit, False), sc, NEG_INF)
            m_new = jnp.maximum(m, jnp.max(sc, axis=-1, keepdims=True))
            alpha = jnp.exp(m - m_new)
            pr = jnp.exp(sc - m_new)
            l = alpha * l + jnp.sum(pr, axis=-1, keepdims=True)
            acc = alpha * acc + _dot(pr, vt)
            return m_new, l, acc

        init = (jnp.full((t, 1), NEG_INF, F32), jnp.zeros((t, 1), F32), jnp.zeros((t, dv), F32))
        carry = lax.fori_loop(0, i, lambda j, c: tile(j, c, False), init)
        m, l, acc = tile(i, carry, True)
        o_ref[0] = (acc / l).astype(o_ref.dtype)
        lse_ref[0] = m + jnp.log(l)

    qs = lambda d: pl.BlockSpec((1, t, d), lambda h, i: (h, i, 0))
    whole = lambda d: pl.BlockSpec((1, s, d), lambda h, i: (h, 0, 0))
    in_specs = [qs(dk), whole(dk), whole(dv)]
    ins = [q, k, v]
    if decay:
        in_specs += [qs(1), pl.BlockSpec((1, 1, s), lambda h, i: (h, 0, 0))]
        ins += [cq, ck]
    vmem = 4 * _nbytes((s, dk + dv), q.dtype) + 10 * _nbytes((t, t), F32) + 8 * _nbytes((t, V7X_LANES), F32)
    return pl.pallas_call(
        body, name=name, grid=(hn, s // t),
        in_specs=in_specs, out_specs=[qs(dv), qs(1)],
        out_shape=[jax.ShapeDtypeStruct((hn, s, dv), MXU_DTYPE), jax.ShapeDtypeStruct((hn, s, 1), F32)],
        compiler_params=_params(("parallel", "arbitrary"), vmem),
    )(*ins)


def _attn_bwd_q(q, k, v, do, lse, cq, ck, *, scale, unit, name):
    hn, s, dk = q.shape
    dv = v.shape[-1]
    t = _attn_tile(s)
    nt = s // t
    decay = cq is not None

    def body(*refs):
        q_ref, k_ref, v_ref, do_ref, lse_ref = refs[:5]
        cq_ref, ck_ref = (refs[5], refs[6]) if decay else (None, None)
        dq_ref, dl_ref, p_sc, dp_sc = refs[-4:]
        i = pl.program_id(1)
        qt = q_ref[0]
        dot = do_ref[0]
        lse_t = lse_ref[0]

        def sweep1(j, delta, masked):
            off = pl.multiple_of(j * t, t)
            kt = k_ref[0, pl.ds(off, t), :]
            vt = v_ref[0, pl.ds(off, t), :]
            sc = _dot_nt(qt, kt) * scale
            if decay:
                sc = sc + cq_ref[0] - ck_ref[0, :, pl.ds(off, t)]
            if masked:
                sc = jnp.where(_visible(t, unit, False), sc, NEG_INF)
            pr = jnp.exp(sc - lse_t)
            dp = _dot_nt(dot, vt)
            p_sc[j] = pr
            dp_sc[j] = dp
            return delta + jnp.sum(pr * dp, axis=-1, keepdims=True)

        delta = lax.fori_loop(0, i, lambda j, c: sweep1(j, c, False), jnp.zeros((t, 1), F32))
        delta = sweep1(i, delta, True)

        def sweep2(j, dq):
            off = pl.multiple_of(j * t, t)
            ds = p_sc[j] * (dp_sc[j] - delta)
            return dq + _dot(ds, k_ref[0, pl.ds(off, t), :])

        dq = lax.fori_loop(0, i + 1, sweep2, jnp.zeros((t, dk), F32))
        dq_ref[0] = dq * scale
        dl_ref[0] = delta

    qs = lambda d: pl.BlockSpec((1, t, d), lambda h, i: (h, i, 0))
    whole = lambda d: pl.BlockSpec((1, s, d), lambda h, i: (h, 0, 0))
    in_specs = [qs(dk), whole(dk), whole(dv), qs(dv), qs(1)]
    ins = [q, k, v, do, lse]
    if decay:
        in_specs += [qs(1), pl.BlockSpec((1, 1, s), lambda h, i: (h, 0, 0))]
        ins += [cq, ck]
    vmem = (4 * _nbytes((s, dk + dv), q.dtype) + 2 * _nbytes((nt, t, t), F32) + 8 * _nbytes((t, t), F32)
            + 12 * _nbytes((t, V7X_LANES), F32))
    return pl.pallas_call(
        body, name=name, grid=(hn, nt),
        in_specs=in_specs, out_specs=[qs(dk), qs(1)],
        out_shape=[jax.ShapeDtypeStruct((hn, s, dk), F32), jax.ShapeDtypeStruct((hn, s, 1), F32)],
        scratch_shapes=[pltpu.VMEM((nt, t, t), F32), pltpu.VMEM((nt, t, t), F32)],
        compiler_params=_params(("parallel", "arbitrary"), vmem),
    )(*ins)


def _attn_bwd_kv(q, k, v, do, lse_row, delta_row, cq_row, ck, *, scale, unit, name):
    hn, s, dk = q.shape
    dv = v.shape[-1]
    t = _attn_tile(s)
    nt = s // t
    decay = ck is not None

    def body(*refs):
        k_ref, v_ref, q_ref, do_ref, lse_ref, dl_ref = refs[:6]
        ck_ref, cq_ref = (refs[6], refs[7]) if decay else (None, None)
        outs = refs[8:] if decay else refs[6:]
        dk_ref, dv_ref = outs[0], outs[1]
        j = pl.program_id(1)
        kt = k_ref[0]
        vt = v_ref[0]

        def tile(i, carry, masked):
            dk_acc, dv_acc, dc_acc = carry
            off = pl.multiple_of(i * t, t)
            qt = q_ref[0, pl.ds(off, t), :]
            dot = do_ref[0, pl.ds(off, t), :]
            sc = _dot_nt(kt, qt) * scale
            if decay:
                sc = sc + cq_ref[0, :, pl.ds(off, t)] - ck_ref[0]
            if masked:
                sc = jnp.where(_visible(t, unit, True), sc, NEG_INF)
            pr = jnp.exp(sc - lse_ref[0, :, pl.ds(off, t)])
            dv_acc = dv_acc + _dot(pr, dot)
            ds = pr * (_dot_nt(vt, dot) - dl_ref[0, :, pl.ds(off, t)])
            dk_acc = dk_acc + _dot(ds, qt)
            if decay:
                dc_acc = dc_acc + jnp.sum(ds, axis=-1, keepdims=True)
            return dk_acc, dv_acc, dc_acc

        init = (jnp.zeros((t, dk), F32), jnp.zeros((t, dv), F32), jnp.zeros((t, 1), F32))
        carry = tile(j, init, True)
        dk_acc, dv_acc, dc_acc = lax.fori_loop(j + 1, nt, lambda i, c: tile(i, c, False), carry)
        dk_ref[0] = dk_acc * scale
        dv_ref[0] = dv_acc
        if decay:
            outs[2][0] = -dc_acc

    ks = lambda d: pl.BlockSpec((1, t, d), lambda h, j: (h, j, 0))
    whole = lambda d: pl.BlockSpec((1, s, d), lambda h, j: (h, 0, 0))
    row = pl.BlockSpec((1, 1, s), lambda h, j: (h, 0, 0))
    in_specs = [ks(dk), ks(dv), whole(dk), whole(dv), row, row]
    ins = [k, v, q, do, lse_row, delta_row]
    out_specs = [ks(dk), ks(dv)]
    out_shape = [jax.ShapeDtypeStruct((hn, s, dk), F32), jax.ShapeDtypeStruct((hn, s, dv), F32)]
    if decay:
        in_specs += [ks(1), row]
        ins += [ck, cq_row]
        out_specs.append(ks(1))
        out_shape.append(jax.ShapeDtypeStruct((hn, s, 1), F32))
    vmem = 4 * _nbytes((s, dk + dv), q.dtype) + 10 * _nbytes((t, t), F32) + 12 * _nbytes((t, V7X_LANES), F32)
    return pl.pallas_call(
        body, name=name, grid=(hn, nt),
        in_specs=in_specs, out_specs=out_specs, out_shape=out_shape,
        compiler_params=_params(("parallel", "arbitrary"), vmem),
    )(*ins)


STRIP = 32
HEAD_PAIRS = HEADS // 2


def _strip_rows(t):
    return min(STRIP, t)


def _split_scale(scale, has_rope):
    if not has_rope and math.frexp(scale)[0] == 0.5:
        return scale, 1.0
    return 1.0, scale


def _pair_mask(t):
    lane = lax.broadcasted_iota(jnp.int32, (t, V7X_LANES), 1)
    return lane < (V7X_LANES // 2)


def _strip_visible(r, t, row0, unit, transposed):
    rows = lax.broadcasted_iota(jnp.int32, (r, t), 0) + row0
    cols = lax.broadcasted_iota(jnp.int32, (r, t), 1)
    shift = int(math.log2(unit))
    if transposed:
        return (cols >> shift) >= (rows >> shift)
    return (rows >> shift) >= (cols >> shift)


def _rope_lanes(x, cos, sin, *, cb, groups, out_dtype, sum_parts=0, name):
    s = cos.shape[0]
    t = _tile(s, 512)
    w = groups * V7X_LANES

    def body(x_ref, c_ref, s_ref, o_ref):
        if sum_parts:
            v = x_ref[0].astype(F32)
            for part in range(1, sum_parts):
                v = v + x_ref[part].astype(F32)
            o_ref[...] = _rope_apply(v, c_ref[...], s_ref[...], V7X_LANES).astype(o_ref.dtype)
        else:
            for g in range(groups):
                sl = slice(g * V7X_LANES, (g + 1) * V7X_LANES)
                o_ref[:, sl] = _rope_apply(x_ref[:, sl].astype(F32), c_ref[...], s_ref[...],
                                           V7X_LANES).astype(o_ref.dtype)

    x_spec = (pl.BlockSpec((sum_parts, t, V7X_LANES), lambda i: (0, i, 0)) if sum_parts else _rows(t, w, cb))
    return pl.pallas_call(
        body, name=name, grid=(s // t,),
        in_specs=[x_spec, _rows(t, V7X_LANES), _rows(t, V7X_LANES)], out_specs=_rows(t, w),
        out_shape=jax.ShapeDtypeStruct((s, w), out_dtype),
        compiler_params=_params(("parallel",), 12 * _nbytes((t, max(w, 4 * V7X_LANES)), F32)),
    )(x, cos, sin)


def _pair_fwd(q_arr, q_cb, k_arr, k_cb, v_arr, v_cb, rope, decay, *, scale, unit, name):
    s = q_arr.shape[0]
    t = _attn_tile(s)
    r = _strip_rows(t)
    has_rope, has_decay = rope is not None, decay is not None
    kw = 2 * V7X_LANES if has_rope else V7X_LANES
    q_mul, s_mul = _split_scale(scale, has_rope)

    def body(*refs):
        it = iter(refs)
        q_ref, k_ref, v_ref = next(it), next(it), next(it)
        qr_ref, kr_ref = (next(it), next(it)) if has_rope else (None, None)
        cq_ref, ck_ref = (next(it), next(it)) if has_decay else (None, None)
        o_ref, lse_ref, lser_ref = next(it), next(it), next(it)
        q_sc, s_sc, p_sc, acc_sc, mx_sc, ls_sc, tr_sc = (next(it) for _ in range(7))
        i = pl.program_id(1)
        in_a = _pair_mask(t)
        qv = q_ref[...] * q_mul
        for hd in range(2):
            q_sc[hd, :, 0:V7X_LANES] = jnp.where(in_a if hd == 0 else jnp.logical_not(in_a), qv, 0).astype(MXU_DTYPE)
            if has_rope:
                q_sc[hd, :, V7X_LANES:kw] = qr_ref[:, hd * V7X_LANES:(hd + 1) * V7X_LANES].astype(MXU_DTYPE)
        mx_sc[...] = jnp.full(mx_sc.shape, NEG_INF, F32)
        ls_sc[...] = jnp.zeros(ls_sc.shape, F32)
        acc_sc[...] = jnp.zeros(acc_sc.shape, F32)
        cq_all = [cq_ref[hd] for hd in range(2)] if has_decay else None
        chunks = t // V7X_LANES

        def keys(j):
            off = pl.multiple_of(j * t, t)
            kt = k_ref[pl.ds(off, t), :]
            if has_rope:
                kt = jnp.concatenate([kt, kr_ref[pl.ds(off, t), :]], axis=-1)
            return off, kt

        def strip_scores(hd, row0, ck_row, masked):
            sc = s_sc[hd, pl.ds(row0, r), :]
            if s_mul != 1.0:
                sc = sc * s_mul
            if has_decay:
                sc = sc + (cq_all[hd][row0:row0 + r] - ck_row)
            if masked:
                sc = jnp.where(_strip_visible(r, t, row0, unit, False), sc, NEG_INF)
            return sc

        def fold(v, op):
            out = v[:, 0:V7X_LANES]
            for ch in range(1, chunks):
                out = op(out, v[:, ch * V7X_LANES:(ch + 1) * V7X_LANES])
            return out

        def tile_max(j, masked):
            off, kt = keys(j)
            for hd in range(2):
                s_sc[hd] = _dot_nt(q_sc[hd], kt)
                ck_row = ck_ref[hd, :, pl.ds(off, t)] if has_decay else None
                for b in range(t // r):
                    rows = pl.ds(b * r, r)
                    sc = strip_scores(hd, b * r, ck_row, masked)
                    mx_sc[hd, rows, :] = jnp.maximum(mx_sc[hd, rows, :], fold(sc, jnp.maximum))

        lax.fori_loop(0, i, lambda j, c: (tile_max(j, False), c)[1], 0)
        tile_max(i, True)
        m_all = [jnp.max(mx_sc[hd], axis=-1, keepdims=True) for hd in range(2)]

        def tile_sum(j, masked):
            off, kt = keys(j)
            vt = v_ref[pl.ds(off, t), :]
            for hd in range(2):
                s_sc[hd] = _dot_nt(q_sc[hd], kt)
                ck_row = ck_ref[hd, :, pl.ds(off, t)] if has_decay else None
                for b in range(t // r):
                    row0 = b * r
                    rows = pl.ds(row0, r)
                    pr = jnp.exp(strip_scores(hd, row0, ck_row, masked) - m_all[hd][row0:row0 + r])
                    ls_sc[hd, rows, :] += fold(pr, jnp.add)
                    p_sc[hd, rows, :] = pr.astype(MXU_DTYPE)
                acc_sc[hd] += _dot(p_sc[hd], vt)

        lax.fori_loop(0, i, lambda j, c: (tile_sum(j, False), c)[1], 0)
        tile_sum(i, True)
        l_all = [jnp.sum(ls_sc[hd], axis=-1, keepdims=True) for hd in range(2)]
        o_ref[...] = jnp.where(in_a, acc_sc[0] / l_all[0], acc_sc[1] / l_all[1]).astype(o_ref.dtype)
        for hd in range(2):
            lse_col = m_all[hd] + jnp.log(l_all[hd])
            lse_ref[hd] = lse_col
            tr_sc[...] = jnp.broadcast_to(lse_col, (t, V7X_LANES)).T
            lser_ref[hd] = tr_sc[0:1, :]

    blk = lambda cb: pl.BlockSpec((t, V7X_LANES), lambda p, i: (i, cb + p))
    whole = lambda cb: pl.BlockSpec((s, V7X_LANES), lambda p, i: (0, cb + p))
    stat = pl.BlockSpec((2, t, 1), lambda p, i: (p, i, 0))
    in_specs = [blk(q_cb), whole(k_cb), whole(v_cb)]
    ins = [q_arr, k_arr, v_arr]
    if has_rope:
        in_specs += [pl.BlockSpec((t, 2 * V7X_LANES), lambda p, i: (i, p)),
                     pl.BlockSpec((s, V7X_LANES), lambda p, i: (0, 0))]
        ins += list(rope)
    if has_decay:
        in_specs += [stat, pl.BlockSpec((2, 1, s), lambda p, i: (p, 0, 0))]
        ins += list(decay)
    col = (2, t, 1)
    vmem = (6 * _nbytes((s, V7X_LANES), MXU_DTYPE) + 6 * _nbytes((t, t), F32) + 10 * _nbytes((t, V7X_LANES), F32)
            + 8 * _nbytes((2, t, V7X_LANES), F32))
    return pl.pallas_call(
        body, name=name, grid=(HEAD_PAIRS, s // t),
        in_specs=in_specs,
        out_specs=[pl.BlockSpec((t, V7X_LANES), lambda p, i: (i, p)), stat,
                   pl.BlockSpec((2, 1, t), lambda p, i: (p, 0, i))],
        out_shape=[jax.ShapeDtypeStruct((s, HEADS * 64), MXU_DTYPE), jax.ShapeDtypeStruct((HEADS, s, 1), F32),
                   jax.ShapeDtypeStruct((HEADS, 1, s), F32)],
        scratch_shapes=[pltpu.VMEM((2, t, kw), MXU_DTYPE), pltpu.VMEM((2, t, t), F32), pltpu.VMEM((2, t, t), MXU_DTYPE),
                        pltpu.VMEM((2, t, V7X_LANES), F32), pltpu.VMEM((2, t, V7X_LANES), F32),
                        pltpu.VMEM((2, t, V7X_LANES), F32), pltpu.VMEM((V7X_LANES, t), F32)],
        compiler_params=_params(("parallel", "arbitrary"), vmem),
    )(*ins)


def _pair_bwd_q(q_arr, q_cb, k_arr, k_cb, v_arr, v_cb, do, lse, rope, decay, *, scale, unit, name):
    s = q_arr.shape[0]
    t = _attn_tile(s)
    nt = s // t
    r = t
    has_rope, has_decay = rope is not None, decay is not None
    kw = 2 * V7X_LANES if has_rope else V7X_LANES

    def body(*refs):
        it = iter(refs)
        q_ref, k_ref, v_ref, do_ref, lse_ref = (next(it) for _ in range(5))
        qr_ref, kr_ref = (next(it), next(it)) if has_rope else (None, None)
        cq_ref, ck_ref = (next(it), next(it)) if has_decay else (None, None)
        dq_ref, dl_ref = next(it), next(it)
        dqr_ref = next(it) if has_rope else None
        q_sc, do_sc, p_sc, dp_sc, ds_sc, dq_sc, dl_sc, s_sc = (next(it) for _ in range(8))
        i = pl.program_id(1)
        in_a = _pair_mask(t)
        qv = q_ref[...]
        dov = do_ref[...]
        for hd in range(2):
            sel = in_a if hd == 0 else jnp.logical_not(in_a)
            q_sc[hd, :, 0:V7X_LANES] = jnp.where(sel, qv, 0).astype(MXU_DTYPE)
            if has_rope:
                q_sc[hd, :, V7X_LANES:kw] = qr_ref[:, hd * V7X_LANES:(hd + 1) * V7X_LANES].astype(MXU_DTYPE)
            do_sc[hd] = jnp.where(sel, dov, 0).astype(MXU_DTYPE)
        dl_sc[...] = jnp.zeros(dl_sc.shape, F32)
        dq_sc[...] = jnp.zeros(dq_sc.shape, F32)

        def keys(j):
            off = pl.multiple_of(j * t, t)
            kt = k_ref[pl.ds(off, t), :]
            if has_rope:
                kt = jnp.concatenate([kt, kr_ref[pl.ds(off, t), :]], axis=-1)
            return off, kt

        for hd in range(2):
            lse_all = lse_ref[hd]
            cq_all = cq_ref[hd] if has_decay else None

            def sweep1(j, masked, hd=hd, lse_all=lse_all, cq_all=cq_all):
                off, kt = keys(j)
                s_sc[...] = _dot_nt(q_sc[hd], kt)
                dp_sc[j] = _dot_nt(do_sc[hd], v_ref[pl.ds(off, t), :])
                ck_row = ck_ref[hd, :, pl.ds(off, t)] if has_decay else None
                parts = []
                for b in range(t // r):
                    row0 = b * r
                    rows = pl.ds(row0, r)
                    sc = s_sc[rows, :] * scale
                    if has_decay:
                        sc = sc + (cq_all[row0:row0 + r] - ck_row)
                    if masked:
                        sc = jnp.where(_strip_visible(r, t, row0, unit, False), sc, NEG_INF)
                    pr = jnp.exp(sc - lse_all[row0:row0 + r])
                    p_sc[j, rows, :] = pr
                    parts.append(jnp.sum(pr * dp_sc[j, rows, :], axis=-1, keepdims=True))
                dl_sc[hd] += jnp.concatenate(parts, axis=0)

            def sweep1_unmasked(j, carry, sweep1=sweep1):
                sweep1(j, False)
                return carry

            lax.fori_loop(0, i, sweep1_unmasked, 0)
            sweep1(i, True)
            dl_all = dl_sc[hd]

            def sweep2(j, carry, hd=hd, dl_all=dl_all):
                _, kt = keys(j)
                for b in range(t // r):
                    row0 = b * r
                    rows = pl.ds(row0, r)
                    ds = p_sc[j, rows, :] * (dp_sc[j, rows, :] - dl_all[row0:row0 + r])
                    ds_sc[rows, :] = ds.astype(MXU_DTYPE)
                dq_sc[hd] += _dot(ds_sc[...], kt)
                return carry

            lax.fori_loop(0, i + 1, sweep2, 0)

        dq_ref[...] = (jnp.where(in_a, dq_sc[0, :, 0:V7X_LANES], dq_sc[1, :, 0:V7X_LANES]) * scale).astype(dq_ref.dtype)
        dl_ref[...] = dl_sc[...]
        if has_rope:
            dqr_ref[:, 0:V7X_LANES] = dq_sc[0, :, V7X_LANES:kw] * scale
            dqr_ref[:, V7X_LANES:kw] = dq_sc[1, :, V7X_LANES:kw] * scale

    blk = lambda cb: pl.BlockSpec((t, V7X_LANES), lambda p, i: (i, cb + p))
    whole = lambda cb: pl.BlockSpec((s, V7X_LANES), lambda p, i: (0, cb + p))
    stat = pl.BlockSpec((2, t, 1), lambda p, i: (p, i, 0))
    in_specs = [blk(q_cb), whole(k_cb), whole(v_cb), blk(0), stat]
    ins = [q_arr, k_arr, v_arr, do, lse]
    out_specs = [blk(0), stat]
    out_shape = [jax.ShapeDtypeStruct((s, HEADS * 64), MXU_DTYPE), jax.ShapeDtypeStruct((HEADS, s, 1), F32)]
    if has_rope:
        pair_rot = pl.BlockSpec((t, 2 * V7X_LANES), lambda p, i: (i, p))
        in_specs += [pair_rot, pl.BlockSpec((s, V7X_LANES), lambda p, i: (0, 0))]
        ins += list(rope)
        out_specs.append(pair_rot)
        out_shape.append(jax.ShapeDtypeStruct((s, HEADS * V7X_LANES), F32))
    if has_decay:
        in_specs += [stat, pl.BlockSpec((2, 1, s), lambda p, i: (p, 0, 0))]
        ins += list(decay)
    vmem = (6 * _nbytes((s, V7X_LANES), MXU_DTYPE) + 2 * _nbytes((nt, t, t), F32) + 6 * _nbytes((t, t), F32)
            + 16 * _nbytes((t, kw), F32))
    return pl.pallas_call(
        body, name=name, grid=(HEAD_PAIRS, nt),
        in_specs=in_specs, out_specs=out_specs, out_shape=out_shape,
        scratch_shapes=[pltpu.VMEM((2, t, kw), MXU_DTYPE), pltpu.VMEM((2, t, V7X_LANES), MXU_DTYPE),
                        pltpu.VMEM((nt, t, t), F32), pltpu.VMEM((nt, t, t), F32), pltpu.VMEM((t, t), MXU_DTYPE),
                        pltpu.VMEM((2, t, kw), F32), pltpu.VMEM((2, t, 1), F32), pltpu.VMEM((t, t), F32)],
        compiler_params=_params(("parallel", "arbitrary"), vmem),
    )(*ins)


def _pair_delta(q_arr, q_cb, k_arr, k_cb, v_arr, v_cb, do, lse, rope, decay, *, scale, unit, name):
    s = q_arr.shape[0]
    t = _attn_tile(s)
    r = _strip_rows(t)
    has_rope, has_decay = rope is not None, decay is not None
    kw = 2 * V7X_LANES if has_rope else V7X_LANES
    chunks = t // V7X_LANES
    q_mul, s_mul = _split_scale(scale, has_rope)

    def body(*refs):
        it = iter(refs)
        q_ref, k_ref, v_ref, do_ref, lse_ref = (next(it) for _ in range(5))
        qr_ref, kr_ref = (next(it), next(it)) if has_rope else (None, None)
        cq_ref, ck_ref = (next(it), next(it)) if has_decay else (None, None)
        dl_ref = next(it)
        q_sc, do_sc, s_sc, dp_sc, acc_sc = (next(it) for _ in range(5))
        i = pl.program_id(1)
        in_a = _pair_mask(t)
        qv, dov = q_ref[...] * q_mul, do_ref[...]
        for hd in range(2):
            sel = in_a if hd == 0 else jnp.logical_not(in_a)
            q_sc[hd, :, 0:V7X_LANES] = jnp.where(sel, qv, 0).astype(MXU_DTYPE)
            if has_rope:
                q_sc[hd, :, V7X_LANES:kw] = qr_ref[:, hd * V7X_LANES:(hd + 1) * V7X_LANES].astype(MXU_DTYPE)
            do_sc[hd] = jnp.where(sel, dov, 0).astype(MXU_DTYPE)
        acc_sc[...] = jnp.zeros(acc_sc.shape, F32)
        lse_all = [lse_ref[hd] for hd in range(2)]
        cq_all = [cq_ref[hd] for hd in range(2)] if has_decay else None

        def fold_add(v):
            out = v[:, 0:V7X_LANES]
            for ch in range(1, chunks):
                out = out + v[:, ch * V7X_LANES:(ch + 1) * V7X_LANES]
            return out

        def tile(j, masked):
            off = pl.multiple_of(j * t, t)
            kt = k_ref[pl.ds(off, t), :]
            if has_rope:
                kt = jnp.concatenate([kt, kr_ref[pl.ds(off, t), :]], axis=-1)
            vt = v_ref[pl.ds(off, t), :]
            for hd in range(2):
                s_sc[hd] = _dot_nt(q_sc[hd], kt)
                dp_sc[hd] = _dot_nt(do_sc[hd], vt)
                ck_row = ck_ref[hd, :, pl.ds(off, t)] if has_decay else None
                for b in range(t // r):
                    row0 = b * r
                    rows = pl.ds(row0, r)
                    sc = s_sc[hd, rows, :]
                    if s_mul != 1.0:
                        sc = sc * s_mul
                    if has_decay:
                        sc = sc + (cq_all[hd][row0:row0 + r] - ck_row)
                    if masked:
                        sc = jnp.where(_strip_visible(r, t, row0, unit, False), sc, NEG_INF)
                    pr = jnp.exp(sc - lse_all[hd][row0:row0 + r])
                    acc_sc[hd, rows, :] += fold_add(pr * dp_sc[hd, rows, :])

        lax.fori_loop(0, i, lambda j, c: (tile(j, False), c)[1], 0)
        tile(i, True)
        for hd in range(2):
            dl_ref[hd] = jnp.sum(acc_sc[hd].T, axis=0, keepdims=True)

    blk = lambda cb: pl.BlockSpec((t, V7X_LANES), lambda p, i: (i, cb + p))
    whole = lambda cb: pl.BlockSpec((s, V7X_LANES), lambda p, i: (0, cb + p))
    stat = pl.BlockSpec((2, t, 1), lambda p, i: (p, i, 0))
    in_specs = [blk(q_cb), whole(k_cb), whole(v_cb), blk(0), stat]
    ins = [q_arr, k_arr, v_arr, do, lse]
    if has_rope:
        in_specs += [pl.BlockSpec((t, 2 * V7X_LANES), lambda p, i: (i, p)),
                     pl.BlockSpec((s, V7X_LANES), lambda p, i: (0, 0))]
        ins += list(rope)
    if has_decay:
        in_specs += [stat, pl.BlockSpec((2, 1, s), lambda p, i: (p, 0, 0))]
        ins += list(decay)
    vmem = (6 * _nbytes((s, V7X_LANES), MXU_DTYPE) + 8 * _nbytes((t, t), F32) + 12 * _nbytes((t, kw), F32))
    return pl.pallas_call(
        body, name=name, grid=(HEAD_PAIRS, s // t),
        in_specs=in_specs, out_specs=pl.BlockSpec((2, 1, t), lambda p, i: (p, 0, i)),
        out_shape=jax.ShapeDtypeStruct((HEADS, 1, s), F32),
        scratch_shapes=[pltpu.VMEM((2, t, kw), MXU_DTYPE), pltpu.VMEM((2, t, V7X_LANES), MXU_DTYPE),
                        pltpu.VMEM((2, t, t), F32), pltpu.VMEM((2, t, t), F32),
                        pltpu.VMEM((2, t, V7X_LANES), F32)],
        compiler_params=_params(("parallel", "arbitrary"), vmem),
    )(*ins)


def _pair_bwd_kv(q_arr, q_cb, k_arr, k_cb, v_arr, v_cb, do, lse_row, delta_row, rope, decay, *, scale, unit, name):
    s = q_arr.shape[0]
    t = _attn_tile(s)
    nt = s // t
    r = _strip_rows(t)
    has_rope, has_decay = rope is not None, decay is not None
    kw = 2 * V7X_LANES if has_rope else V7X_LANES
    q_mul, s_mul = _split_scale(scale, has_rope)

    def body(*refs):
        it = iter(refs)
        k_ref, v_ref, q_ref, do_ref, lse_ref, dl_ref = (next(it) for _ in range(6))
        qr_ref, kr_ref = (next(it), next(it)) if has_rope else (None, None)
        ck_ref, cq_ref = (next(it), next(it)) if has_decay else (None, None)
        dk_ref, dv_ref, dq_ref = next(it), next(it), next(it)
        dkr_ref, dqr_ref = (next(it), next(it)) if has_rope else (None, None)
        dc_ref = next(it) if has_decay else None
        k_sc, v_sc, st_sc, dpt_sc, pt_sc, dst_sc, dk_sc, dv_sc, dc_sc, dqt_sc, kt_sc = (next(it) for _ in range(11))
        j = pl.program_id(1)
        in_a = _pair_mask(t)
        kv_, vv_ = k_ref[...], v_ref[...]
        for hd in range(2):
            sel = in_a if hd == 0 else jnp.logical_not(in_a)
            k_sc[hd, :, 0:V7X_LANES] = jnp.where(sel, kv_, 0).astype(MXU_DTYPE)
            if has_rope:
                k_sc[hd, :, V7X_LANES:kw] = kr_ref[...].astype(MXU_DTYPE)
            v_sc[hd] = jnp.where(sel, vv_, 0).astype(MXU_DTYPE)
        dk_sc[...] = jnp.zeros(dk_sc.shape, F32)
        dv_sc[...] = jnp.zeros(dv_sc.shape, F32)
        dc_sc[...] = jnp.zeros(dc_sc.shape, F32)
        k_all = kv_.astype(F32)
        if has_rope:
            k_all = jnp.concatenate([k_all, kr_ref[...].astype(F32)], axis=-1)
        kt_sc[...] = k_all.T.astype(MXU_DTYPE)

        @pl.when(j == 0)
        def _():
            dqt_sc[...] = jnp.zeros(dqt_sc.shape, F32)

        def tile(i, masked):
            off = pl.multiple_of(i * t, t)
            qt = q_ref[pl.ds(off, t), :] * q_mul
            dot = do_ref[pl.ds(off, t), :]
            for hd in range(2):
                qcat = qt
                if has_rope:
                    qcat = jnp.concatenate([qt, qr_ref[pl.ds(off, t), hd * V7X_LANES:(hd + 1) * V7X_LANES]], axis=-1)
                st_sc[hd] = _dot_nt(k_sc[hd], qcat)
                dpt_sc[hd] = _dot_nt(v_sc[hd], dot)
                lse_r = lse_ref[hd, :, pl.ds(off, t)]
                dl_r = dl_ref[hd, :, pl.ds(off, t)]
                cq_r = cq_ref[hd, :, pl.ds(off, t)] if has_decay else None
                ck_all = ck_ref[hd] if has_decay else None
                parts = []
                for b in range(t // r):
                    row0 = b * r
                    rows = pl.ds(row0, r)
                    sc = st_sc[hd, rows, :]
                    if s_mul != 1.0:
                        sc = sc * s_mul
                    if has_decay:
                        sc = sc + (cq_r - ck_all[row0:row0 + r])
                    if masked:
                        sc = jnp.where(_strip_visible(r, t, row0, unit, True), sc, NEG_INF)
                    pr = jnp.exp(sc - lse_r)
                    ds = pr * (dpt_sc[hd, rows, :] - dl_r)
                    pt_sc[hd, rows, :] = pr.astype(MXU_DTYPE)
                    dst_sc[hd, rows, :] = ds.astype(MXU_DTYPE)
                    if has_decay:
                        parts.append(jnp.sum(ds, axis=-1, keepdims=True))
                if has_decay:
                    dc_sc[hd] += jnp.concatenate(parts, axis=0)
                dv_sc[hd] += _dot(pt_sc[hd], dot)
                dk_sc[hd] += _dot(dst_sc[hd], qcat)
                dqt_sc[hd, :, pl.ds(off, t)] += _dot(kt_sc[...], dst_sc[hd])

        tile(j, True)

        def unmasked(i, carry):
            tile(i, False)
            return carry

        lax.fori_loop(j + 1, nt, unmasked, 0)
        dk_ref[...] = (jnp.where(in_a, dk_sc[0, :, 0:V7X_LANES], dk_sc[1, :, 0:V7X_LANES]) * s_mul).astype(dk_ref.dtype)
        dv_ref[...] = jnp.where(in_a, dv_sc[0], dv_sc[1]).astype(dv_ref.dtype)
        if has_rope:
            dkr_ref[0] = (dk_sc[0, :, V7X_LANES:kw] + dk_sc[1, :, V7X_LANES:kw]) * s_mul
        if has_decay:
            dc_ref[...] = -dc_sc[...]
        own = pl.ds(pl.multiple_of(j * t, t), t)
        dq_a = dqt_sc[0, :, own].T * scale
        dq_b = dqt_sc[1, :, own].T * scale
        dq_ref[...] = jnp.where(in_a, dq_a[:, 0:V7X_LANES], dq_b[:, 0:V7X_LANES]).astype(dq_ref.dtype)
        if has_rope:
            dqr_ref[:, 0:V7X_LANES] = dq_a[:, V7X_LANES:kw]
            dqr_ref[:, V7X_LANES:kw] = dq_b[:, V7X_LANES:kw]

    blk = lambda cb: pl.BlockSpec((t, V7X_LANES), lambda p, j: (j, cb + p))
    whole = lambda cb: pl.BlockSpec((s, V7X_LANES), lambda p, j: (0, cb + p))
    stat = pl.BlockSpec((2, t, 1), lambda p, j: (p, j, 0))
    row = pl.BlockSpec((2, 1, s), lambda p, j: (p, 0, 0))
    in_specs = [blk(k_cb), blk(v_cb), whole(q_cb), whole(0), row, row]
    ins = [k_arr, v_arr, q_arr, do, lse_row, delta_row]
    out_specs = [blk(0), blk(0), blk(0)]
    out_shape = [jax.ShapeDtypeStruct((s, HEADS * 64), MXU_DTYPE)] * 3
    if has_rope:
        in_specs += [pl.BlockSpec((s, 2 * V7X_LANES), lambda p, j: (0, p)),
                     pl.BlockSpec((t, V7X_LANES), lambda p, j: (j, 0))]
        ins += list(rope)
        out_specs += [pl.BlockSpec((1, t, V7X_LANES), lambda p, j: (p, j, 0)),
                      pl.BlockSpec((t, 2 * V7X_LANES), lambda p, j: (j, p))]
        out_shape += [jax.ShapeDtypeStruct((HEAD_PAIRS, s, V7X_LANES), F32),
                      jax.ShapeDtypeStruct((s, HEADS * V7X_LANES), F32)]
    if has_decay:
        in_specs += [stat, row]
        ins += list(decay)
        out_specs.append(stat)
        out_shape.append(jax.ShapeDtypeStruct((HEADS, s, 1), F32))
    vmem = (12 * _nbytes((s, V7X_LANES), MXU_DTYPE) + 8 * _nbytes((t, t), F32) + 16 * _nbytes((t, kw), F32)
            + _nbytes((2, kw, s), F32))
    return pl.pallas_call(
        body, name=name, grid=(HEAD_PAIRS, nt),
        in_specs=in_specs, out_specs=out_specs, out_shape=out_shape,
        scratch_shapes=[pltpu.VMEM((2, t, kw), MXU_DTYPE), pltpu.VMEM((2, t, V7X_LANES), MXU_DTYPE),
                        pltpu.VMEM((2, t, t), F32), pltpu.VMEM((2, t, t), F32), pltpu.VMEM((2, t, t), MXU_DTYPE),
                        pltpu.VMEM((2, t, t), MXU_DTYPE), pltpu.VMEM((2, t, kw), F32),
                        pltpu.VMEM((2, t, V7X_LANES), F32), pltpu.VMEM((2, t, 1), F32),
                        pltpu.VMEM((2, kw, s), F32), pltpu.VMEM((kw, t), MXU_DTYPE)],
        compiler_params=_params(("arbitrary", "arbitrary"), vmem),
    )(*ins)


def _fox_cum(z, bf, *, name):
    s = z.shape[0]
    w = V7X_LANES
    t = _row_tile(s, 512)
    steps = [1 << k for k in range(int(math.log2(t)))]
    cb = SEG["fl"][3] // w

    def body(f_ref, bf_ref, c_ref, car):
        @pl.when(pl.program_id(0) == 0)
        def _():
            car[...] = jnp.zeros_like(car)

        acc = -_softplus(-(f_ref[...] + bf_ref[...]))
        rows = lax.broadcasted_iota(jnp.int32, (t, w), 0)
        for d in steps:
            acc = acc + _shift_down(acc, d, 0.0, rows)
        c_ref[...] = acc + car[0:1, :]
        car[0:1, :] = c_ref[t - 1:t, :]

    return pl.pallas_call(
        body, name=name, grid=(s // t,),
        in_specs=[_rows(t, w, cb), _full((1, w))], out_specs=_rows(t, w),
        out_shape=jax.ShapeDtypeStruct((s, w), F32),
        scratch_shapes=[pltpu.VMEM((V7X_SUBLANES, w), F32)],
        compiler_params=_params(("arbitrary",), 16 * _nbytes((t, w), F32)),
    )(z, bf)


def _fox_cum_bwd(z, bf, dcum, *, name):
    s = z.shape[0]
    w = V7X_LANES
    t = _row_tile(s, 512)
    nt = s // t
    steps = [1 << k for k in range(int(math.log2(t)))]
    cb = SEG["fl"][3] // w

    def body(f_ref, bf_ref, dc_ref, df_ref, dbf_ref, car, tmp):
        @pl.when(pl.program_id(0) == 0)
        def _():
            car[...] = jnp.zeros_like(car)
            dbf_ref[...] = jnp.zeros_like(dbf_ref)

        acc = dc_ref[...]
        rows = lax.broadcasted_iota(jnp.int32, (t, w), 0)
        for d in steps:
            acc = acc + _shift_up(acc, d, 0.0, rows, t)
        dlf = acc + car[0:1, :]
        tmp[...] = dlf
        car[0:1, :] = tmp[0:1, :]
        df = dlf * _sigmoid(-(f_ref[...] + bf_ref[...]))
        df_ref[...] = df.astype(df_ref.dtype)
        dbf_ref[...] += jnp.sum(df, axis=0, keepdims=True)

    rev = lambda cbk: pl.BlockSpec((t, w), lambda i: (nt - 1 - i, cbk))
    return pl.pallas_call(
        body, name=name, grid=(nt,),
        in_specs=[rev(cb), _full((1, w)), rev(0)], out_specs=[rev(0), _full((1, w))],
        out_shape=[jax.ShapeDtypeStruct((s, w), MXU_DTYPE), jax.ShapeDtypeStruct((1, w), F32)],
        scratch_shapes=[pltpu.VMEM((V7X_SUBLANES, w), F32), pltpu.VMEM((t, w), F32)],
        compiler_params=_params(("arbitrary",), 16 * _nbytes((t, w), F32)),
    )(z, bf, dcum)


_GATE_CB = SEG["gate"][3] // D_MODEL


def _merge_fwd(ya, yb, yc, z, gate_b, *, name):
    s = ya.shape[0]
    d = D_MODEL
    t = _tile(s, 256)

    def body(ya_ref, yb_ref, yc_ref, g0_ref, g1_ref, g2_ref, gb_ref, o_ref):
        out = _sigmoid(g0_ref[...] + gb_ref[:, 0:d]) * ya_ref[...]
        out = out + _sigmoid(g1_ref[...] + gb_ref[:, d:2 * d]) * yb_ref[...]
        out = out + _sigmoid(g2_ref[...] + gb_ref[:, 2 * d:3 * d]) * yc_ref[...]
        o_ref[...] = out.astype(o_ref.dtype)

    return pl.pallas_call(
        body, name=name, grid=(s // t,),
        in_specs=[_rows(t, d)] * 3 + [_rows(t, d, _GATE_CB + b) for b in range(3)] + [_full((1, 3 * d))],
        out_specs=_rows(t, d), out_shape=jax.ShapeDtypeStruct((s, d), MXU_DTYPE),
        compiler_params=_params(("parallel",), 20 * _nbytes((t, d), F32)),
    )(ya, yb, yc, z, z, z, gate_b)


def _merge_bwd(dm, ya, yb, yc, z, gate_b, *, name):
    s = ya.shape[0]
    d = D_MODEL
    t = _tile(s, 256)

    def body(dm_ref, ya_ref, yb_ref, yc_ref, g0_ref, g1_ref, g2_ref, gb_ref, da_ref, db_ref, dc_ref, dgl_ref,
             dgb_ref):
        dmv = dm_ref[...]
        parts = []
        for b, (y_ref, g_ref, dy_ref) in enumerate(((ya_ref, g0_ref, da_ref), (yb_ref, g1_ref, db_ref),
                                                    (yc_ref, g2_ref, dc_ref))):
            gate = _sigmoid(g_ref[...] + gb_ref[:, b * d:(b + 1) * d])
            dy_ref[...] = (dmv * gate).astype(dy_ref.dtype)
            dgl = dmv * y_ref[...] * gate * (1.0 - gate)
            dgl_ref[:, b * d:(b + 1) * d] = dgl.astype(dgl_ref.dtype)
            parts.append(jnp.sum(dgl, axis=0, keepdims=True))

        @pl.when(pl.program_id(0) == 0)
        def _():
            for b, part in enumerate(parts):
                dgb_ref[:, b * d:(b + 1) * d] = part

        @pl.when(pl.program_id(0) > 0)
        def _():
            for b, part in enumerate(parts):
                dgb_ref[:, b * d:(b + 1) * d] += part

    return pl.pallas_call(
        body, name=name, grid=(s // t,),
        in_specs=[_rows(t, d)] * 4 + [_rows(t, d, _GATE_CB + b) for b in range(3)] + [_full((1, 3 * d))],
        out_specs=[_rows(t, d)] * 3 + [_rows(t, 3 * d), _full((1, 3 * d))],
        out_shape=[jax.ShapeDtypeStruct((s, d), MXU_DTYPE)] * 3
        + [jax.ShapeDtypeStruct((s, 3 * d), MXU_DTYPE), jax.ShapeDtypeStruct((1, 3 * d), F32)],
        compiler_params=_params(("arbitrary",), 36 * _nbytes((t, d), F32)),
    )(dm, ya, yb, yc, z, z, z, gate_b)


def _swiglu_fwd(hf, *, name):
    s = hf.shape[0]
    t = _tile(s, 256)

    def body(g_ref, u_ref, o_ref):
        gv = g_ref[...]
        o_ref[...] = (gv * _sigmoid(gv) * u_ref[...]).astype(o_ref.dtype)

    return pl.pallas_call(
        body, name=name, grid=(s // t,),
        in_specs=[_rows(t, D_FF, 0), _rows(t, D_FF, 1)], out_specs=_rows(t, D_FF),
        out_shape=jax.ShapeDtypeStruct((s, D_FF), MXU_DTYPE),
        compiler_params=_params(("parallel",), 10 * _nbytes((t, D_FF), F32)),
    )(hf, hf)


def _swiglu_bwd(hf, dact, *, name):
    s = hf.shape[0]
    t = _tile(s, 256)

    def body(g_ref, u_ref, da_ref, o_ref):
        gv = g_ref[...]
        dav = da_ref[...]
        sg = _sigmoid(gv)
        o_ref[:, 0:D_FF] = (dav * u_ref[...] * sg * (1.0 + gv * (1.0 - sg))).astype(o_ref.dtype)
        o_ref[:, D_FF:2 * D_FF] = (dav * gv * sg).astype(o_ref.dtype)

    return pl.pallas_call(
        body, name=name, grid=(s // t,),
        in_specs=[_rows(t, D_FF, 0), _rows(t, D_FF, 1), _rows(t, D_FF)], out_specs=_rows(t, 2 * D_FF),
        out_shape=jax.ShapeDtypeStruct((s, 2 * D_FF), MXU_DTYPE),
        compiler_params=_params(("parallel",), 14 * _nbytes((t, D_FF), F32)),
    )(hf, hf, dact)


def _ple_fwd(x, lg, pe, *, name):
    s, d = x.shape
    t = _tile(s, 512)

    def body(x_ref, lg_ref, pe_ref, o_ref):
        o_ref[...] = x_ref[...] + _sigmoid(lg_ref[...]) * pe_ref[...]

    return pl.pallas_call(
        body, name=name, grid=(s // t,),
        in_specs=[_rows(t, d)] * 3, out_specs=_rows(t, d), out_shape=jax.ShapeDtypeStruct((s, d), F32),
        compiler_params=_params(("parallel",), 12 * _nbytes((t, d), F32)),
    )(x, lg, pe)


def _ple_bwd(dx, lg, pe, *, name):
    s, d = dx.shape
    t = _tile(s, 512)

    def body(dx_ref, lg_ref, pe_ref, dpe_ref, dlg_ref):
        dxv = dx_ref[...]
        sg = _sigmoid(lg_ref[...])
        dpe_ref[...] = (dxv * sg).astype(dpe_ref.dtype)
        dlg_ref[...] = (dxv * pe_ref[...] * sg * (1.0 - sg)).astype(dlg_ref.dtype)

    return pl.pallas_call(
        body, name=name, grid=(s // t,),
        in_specs=[_rows(t, d)] * 3, out_specs=[_rows(t, d)] * 2,
        out_shape=[jax.ShapeDtypeStruct((s, d), MXU_DTYPE)] * 2,
        compiler_params=_params(("parallel",), 14 * _nbytes((t, d), F32)),
    )(dx, lg, pe)


def _adamw(parts, w, m, v, *, name):
    rows, lanes = w.shape
    t = math.gcd(rows, 160)
    assert rows % t == 0 and t % V7X_SUBLANES == 0
    c1 = 1.0 / (1.0 - ADAM_B1 ** ADAM_STEP)
    c2 = 1.0 / (1.0 - ADAM_B2 ** ADAM_STEP)

    def body(p_ref, w_ref, m_ref, v_ref, g_ref, d_ref, nm_ref, nv_ref):
        g = p_ref[0].astype(F32)
        for j in range(1, N_DEV):
            g = g + p_ref[j].astype(F32)
        m2 = ADAM_B1 * m_ref[...] + (1.0 - ADAM_B1) * g
        v2 = ADAM_B2 * v_ref[...] + (1.0 - ADAM_B2) * (g * g)
        g_ref[...] = g
        nm_ref[...] = m2
        nv_ref[...] = v2
        d_ref[...] = -ADAM_LR * ((m2 * c1) / (jnp.sqrt(v2 * c2) + ADAM_EPS) + ADAM_WD * w_ref[...])

    blk = _rows(t, lanes)
    return pl.pallas_call(
        body, name=name, grid=(rows // t,),
        in_specs=[pl.BlockSpec((N_DEV, t, lanes), lambda i: (0, i, 0)), blk, blk, blk], out_specs=[blk] * 4,
        out_shape=[jax.ShapeDtypeStruct((rows, lanes), F32)] * 4,
        compiler_params=_params(("parallel",), 40 * _nbytes((t, lanes), F32)),
    )(parts, w, m, v)


def _mesh_pos():
    return lax.axis_index("x"), lax.axis_index("y"), lax.axis_index("c")


def _all_gather(blk, *, name):
    r, c_dim = blk.shape

    def body(x_ref, out_ref, send_sems, recv_sems, local_sem):
        x, y, c = _mesh_pos()
        me, sibling = (x, y, c), (x, y, 1 - c)
        chips = [(1 - x, y), (x, 1 - y), (1 - x, 1 - y)]

        def slot(px, py, pc):
            return out_ref.at[4 * px + 2 * py + pc]

        def copy(k, block, to, src=None):
            return pltpu.make_async_remote_copy(
                src_ref=slot(*block) if src is None else src, dst_ref=slot(*block),
                send_sem=send_sems.at[k], recv_sem=recv_sems.at[k],
                device_id=to, device_id_type=pl.DeviceIdType.MESH)

        mine = pltpu.make_async_copy(x_ref, slot(*me), local_sem)
        mine.start()
        first = [copy(0, me, sibling, src=x_ref)]
        first += [copy(1 + j, me, (*chip, c), src=x_ref) for j, chip in enumerate(chips)]
        for cp in first:
            cp.start()
        passed = [copy(4 + j, (*chip, c), sibling) for j, chip in enumerate(chips)]
        for j, chip in enumerate(chips):
            copy(1 + j, (*chip, c), me).wait_recv()
            passed[j].start()
        copy(0, sibling, me).wait_recv()
        for j, chip in enumerate(chips):
            copy(4 + j, (*chip, 1 - c), me).wait_recv()
        for cp in first + passed:
            cp.wait_send()
        mine.wait()

    return pl.pallas_call(
        body, name=name,
        out_shape=jax.ShapeDtypeStruct((N_DEV, r, c_dim), blk.dtype),
        in_specs=[pl.BlockSpec(memory_space=pl.ANY)], out_specs=pl.BlockSpec(memory_space=pl.ANY),
        scratch_shapes=[pltpu.SemaphoreType.DMA((7,)), pltpu.SemaphoreType.DMA((7,)), pltpu.SemaphoreType.DMA],
    )(blk)


def _all_to_all(pay, *, name):
    _, r, c_dim = pay.shape

    def body(in_ref, out_ref, send_sems, recv_sems, local_sem):
        x, y, c = _mesh_pos()
        me = 4 * x + 2 * y + c
        local = pltpu.make_async_copy(in_ref.at[me], out_ref.at[me], local_sem)
        local.start()
        copies = []
        for k in range(1, N_DEV):
            px = 1 - x if k & 4 else x
            py = 1 - y if k & 2 else y
            pc = 1 - c if k & 1 else c
            copies.append(pltpu.make_async_remote_copy(
                src_ref=in_ref.at[4 * px + 2 * py + pc], dst_ref=out_ref.at[me],
                send_sem=send_sems.at[k - 1], recv_sem=recv_sems.at[k - 1],
                device_id=(px, py, pc), device_id_type=pl.DeviceIdType.MESH))
        for cp in copies:
            cp.start()
        for cp in copies:
            cp.wait()
        local.wait()

    return pl.pallas_call(
        body, name=name,
        out_shape=jax.ShapeDtypeStruct((N_DEV, r, c_dim), pay.dtype),
        in_specs=[pl.BlockSpec(memory_space=pl.ANY)], out_specs=pl.BlockSpec(memory_space=pl.ANY),
        scratch_shapes=[pltpu.SemaphoreType.DMA((7,)), pltpu.SemaphoreType.DMA((7,)), pltpu.SemaphoreType.DMA],
    )(pay)


def _all_gather_many(blocks, *, name):
    n = len(blocks)

    def body(*refs):
        x_refs, out_refs = refs[:n], refs[n:2 * n]
        send_sems, recv_sems, local_sems = refs[2 * n:]
        x, y, c = _mesh_pos()
        me, sibling = (x, y, c), (x, y, 1 - c)
        chips = [(1 - x, y), (x, 1 - y), (1 - x, 1 - y)]

        def slot(a, px, py, pc):
            return out_refs[a].at[4 * px + 2 * py + pc]

        def copy(k, a, block, to, src=None):
            return pltpu.make_async_remote_copy(
                src_ref=slot(a, *block) if src is None else src, dst_ref=slot(a, *block),
                send_sem=send_sems.at[k, a], recv_sem=recv_sems.at[k, a],
                device_id=to, device_id_type=pl.DeviceIdType.MESH)

        mine = [pltpu.make_async_copy(x_refs[a], slot(a, *me), local_sems.at[a]) for a in range(n)]
        for cp in mine:
            cp.start()
        first = [copy(0, a, me, sibling, src=x_refs[a]) for a in range(n)]
        first += [copy(1 + j, a, me, (*chip, c), src=x_refs[a]) for j, chip in enumerate(chips) for a in range(n)]
        for cp in first:
            cp.start()
        passed = []
        for j, chip in enumerate(chips):
            for a in range(n):
                copy(1 + j, a, (*chip, c), me).wait_recv()
                fwd = copy(4 + j, a, (*chip, c), sibling)
                fwd.start()
                passed.append(fwd)
        for a in range(n):
            copy(0, a, sibling, me).wait_recv()
        for j, chip in enumerate(chips):
            for a in range(n):
                copy(4 + j, a, (*chip, 1 - c), me).wait_recv()
        for cp in first + passed:
            cp.wait_send()
        for cp in mine:
            cp.wait()

    any_spec = pl.BlockSpec(memory_space=pl.ANY)
    return pl.pallas_call(
        body, name=name,
        out_shape=[jax.ShapeDtypeStruct((N_DEV,) + b.shape, b.dtype) for b in blocks],
        in_specs=[any_spec] * n, out_specs=[any_spec] * n,
        scratch_shapes=[pltpu.SemaphoreType.DMA((7, n)), pltpu.SemaphoreType.DMA((7, n)),
                        pltpu.SemaphoreType.DMA((n,))],
    )(*blocks)


def _all_to_all_many(pays, *, name):
    n = len(pays)

    def body(*refs):
        in_refs, out_refs = refs[:n], refs[n:2 * n]
        send_sems, recv_sems, local_sems = refs[2 * n:]
        x, y, c = _mesh_pos()
        me = 4 * x + 2 * y + c
        local = [pltpu.make_async_copy(in_refs[a].at[me], out_refs[a].at[me], local_sems.at[a]) for a in range(n)]
        for cp in local:
            cp.start()
        copies = []
        for k in range(1, N_DEV):
            px = 1 - x if k & 4 else x
            py = 1 - y if k & 2 else y
            pc = 1 - c if k & 1 else c
            for a in range(n):
                copies.append(pltpu.make_async_remote_copy(
                    src_ref=in_refs[a].at[4 * px + 2 * py + pc], dst_ref=out_refs[a].at[me],
                    send_sem=send_sems.at[k - 1, a], recv_sem=recv_sems.at[k - 1, a],
                    device_id=(px, py, pc), device_id_type=pl.DeviceIdType.MESH))
        for cp in copies:
            cp.start()
        for cp in copies:
            cp.wait()
        for cp in local:
            cp.wait()

    any_spec = pl.BlockSpec(memory_space=pl.ANY)
    return pl.pallas_call(
        body, name=name,
        out_shape=[jax.ShapeDtypeStruct(p.shape, p.dtype) for p in pays],
        in_specs=[any_spec] * n, out_specs=[any_spec] * n,
        scratch_shapes=[pltpu.SemaphoreType.DMA((7, n)), pltpu.SemaphoreType.DMA((7, n)),
                        pltpu.SemaphoreType.DMA((n,))],
    )(*pays)


CHIPS = 4


def _sibling_exchange_many(pays, *, name):
    n = len(pays)

    def body(*refs):
        in_refs, theirs_refs = refs[:n], refs[n:2 * n]
        send_sems, recv_sems = refs[2 * n:]
        x, y, c = _mesh_pos()
        copies = [pltpu.make_async_remote_copy(
            src_ref=in_refs[a].at[pl.ds(CHIPS * (1 - c), CHIPS)], dst_ref=theirs_refs[a],
            send_sem=send_sems.at[a], recv_sem=recv_sems.at[a],
            device_id=(x, y, 1 - c), device_id_type=pl.DeviceIdType.MESH) for a in range(n)]
        for cp in copies:
            cp.start()
        for cp in copies:
            cp.wait()

    any_spec = pl.BlockSpec(memory_space=pl.ANY)
    return pl.pallas_call(
        body, name=name, out_shape=[jax.ShapeDtypeStruct((CHIPS,) + p.shape[1:], p.dtype) for p in pays],
        in_specs=[any_spec] * n, out_specs=[any_spec] * n,
        scratch_shapes=[pltpu.SemaphoreType.DMA((n,)), pltpu.SemaphoreType.DMA((n,))],
    )(*pays)


def _chip_exchange_many(sums, *, name):
    n = len(sums)

    def body(*refs):
        in_refs, out_refs = refs[:n], refs[n:2 * n]
        send_sems, recv_sems, local_sems = refs[2 * n:]
        x, y, c = _mesh_pos()
        me = 2 * x + y
        copies = [pltpu.make_async_copy(in_refs[a].at[me], out_refs[a].at[me], local_sems.at[a]) for a in range(n)]
        for k in range(1, CHIPS):
            px = 1 - x if k & 2 else x
            py = 1 - y if k & 1 else y
            for a in range(n):
                copies.append(pltpu.make_async_remote_copy(
                    src_ref=in_refs[a].at[2 * px + py], dst_ref=out_refs[a].at[me],
                    send_sem=send_sems.at[k - 1, a], recv_sem=recv_sems.at[k - 1, a],
                    device_id=(px, py, c), device_id_type=pl.DeviceIdType.MESH))
        for cp in copies:
            cp.start()
        for cp in copies:
            cp.wait()

    any_spec = pl.BlockSpec(memory_space=pl.ANY)
    return pl.pallas_call(
        body, name=name, out_shape=[jax.ShapeDtypeStruct(p.shape, p.dtype) for p in sums],
        in_specs=[any_spec] * n, out_specs=[any_spec] * n,
        scratch_shapes=[pltpu.SemaphoreType.DMA((CHIPS - 1, n)), pltpu.SemaphoreType.DMA((CHIPS - 1, n)),
                        pltpu.SemaphoreType.DMA((n,))],
    )(*sums)


def _shard_row_tile(rows):
    t = rows
    for cand in range(V7X_SUBLANES, min(rows, 256) + 1, V7X_SUBLANES):
        if rows % cand == 0:
            t = cand
    return t


def _pair_sum(a, b, *, name):
    n4, d0, rows, cols = a.shape
    t = _shard_row_tile(rows)

    def body(a_ref, b_ref, o_ref):
        o_ref[...] = (a_ref[...].astype(F32) + b_ref[...].astype(F32)).astype(o_ref.dtype)

    blk = pl.BlockSpec((1, 1, t, cols), lambda q, l, i: (q, l, i, 0))
    lanes = -(-cols // V7X_LANES) * V7X_LANES
    return pl.pallas_call(
        body, name=name, grid=(n4, d0, rows // t), in_specs=[blk, blk], out_specs=blk,
        out_shape=jax.ShapeDtypeStruct(a.shape, MXU_DTYPE),
        compiler_params=_params(("parallel", "parallel", "parallel"), 16 * _nbytes((max(t, 16), lanes), F32)),
    )(a, b)


def _adamw_nd(parts, w, m, v, *, name):
    n_parts = parts.shape[0]
    d0, rows, cols = w.shape
    t = _shard_row_tile(rows)
    c1 = 1.0 / (1.0 - ADAM_B1 ** ADAM_STEP)
    c2 = 1.0 / (1.0 - ADAM_B2 ** ADAM_STEP)

    def body(p_ref, w_ref, m_ref, v_ref, g_ref, d_ref, nm_ref, nv_ref):
        g = p_ref[0, 0].astype(F32)
        for j in range(1, n_parts):
            g = g + p_ref[j, 0].astype(F32)
        m2 = ADAM_B1 * m_ref[0] + (1.0 - ADAM_B1) * g
        v2 = ADAM_B2 * v_ref[0] + (1.0 - ADAM_B2) * (g * g)
        g_ref[0] = g
        nm_ref[0] = m2
        nv_ref[0] = v2
        d_ref[0] = -ADAM_LR * ((m2 * c1) / (jnp.sqrt(v2 * c2) + ADAM_EPS) + ADAM_WD * w_ref[0])

    blk = pl.BlockSpec((1, t, cols), lambda l, i: (l, i, 0))
    lanes = -(-cols // V7X_LANES) * V7X_LANES
    return pl.pallas_call(
        body, name=name, grid=(d0, rows // t),
        in_specs=[pl.BlockSpec((n_parts, 1, t, cols), lambda l, i: (0, l, i, 0)), blk, blk, blk], out_specs=[blk] * 4,
        out_shape=[jax.ShapeDtypeStruct(w.shape, F32)] * 4,
        compiler_params=_params(("parallel", "parallel"), 40 * _nbytes((max(t, 16), lanes), F32)),
    )(parts, w, m, v)


def _flat_rows(parts, row_multiple):
    flat = jnp.concatenate([p.reshape(-1) for p in parts])
    chunk = PAYLOAD_LANES * row_multiple
    total = -(-flat.shape[0] // chunk) * chunk
    return jnp.pad(flat, (0, total - flat.shape[0])).reshape(total // PAYLOAD_LANES, PAYLOAD_LANES)


def _split_flat(flat, shapes):
    out, off = [], 0
    flat = flat.reshape(-1)
    for shp in shapes:
        n = math.prod(shp)
        out.append(flat[off:off + n].reshape(shp))
        off += n
    return out


def _pad_w_in(w):
    pieces, cursor = [], 0
    for _, off, width, pad_off, _ in SEGS:
        if pad_off > cursor:
            pieces.append(jnp.zeros(w.shape[:-1] + (pad_off - cursor,), w.dtype))
        pieces.append(w[..., off:off + width])
        cursor = pad_off + width
    pieces.append(jnp.zeros(w.shape[:-1] + (D_IN_PAD - cursor,), w.dtype))
    return jnp.concatenate(pieces, axis=-1)


def _unpad_w_in(w):
    return jnp.concatenate([w[..., pad_off:pad_off + width] for _, _, width, pad_off, _ in SEGS], axis=-1)


def _heads(a, hd):
    return a.reshape(a.shape[0], HEADS, hd).transpose(1, 0, 2)


def _unheads(a):
    return a.transpose(1, 0, 2).reshape(a.shape[1], -1)


def _block_diag(w):
    eye = jnp.eye(LRU_HEADS, dtype=w.dtype)
    return (eye[:, None, :, None] * w[:, :, None, :]).reshape(LRU_WIDTH, LRU_WIDTH)


def _diag_blocks(w):
    w4 = w.reshape(LRU_HEADS, LRU_HEAD_DIM, LRU_HEADS, LRU_HEAD_DIM)
    return jnp.stack([w4[h, :, h, :] for h in range(LRU_HEADS)])


def _lane_pad(a, width):
    return jnp.pad(a, ((0, 0), (0, width - a.shape[-1])))


def _layer_fwd(x, p_i, wts, tabs, tag):
    n = functools.partial(lambda base, t=tag: f"{base}_{t}")
    sv = {"x": x}
    n1 = _rms_fwd(x, wts["mix_norm"], width=D_MODEL, name=n("mix_norm_fwd"))
    z, z16 = _mm(n1, wts["w_in"], also_mxu=True, name=n("w_in_fwd"))
    sv.update(n1=n1, z=z, z16=z16)
    lanes = V7X_LANES

    ya_pre, hseq = _lru_fwd(z, wts["conv_w"], wts["conv_b"], wts["lru_wa"], wts["lru_ba"], wts["lru_wx"],
                            wts["lru_bx"], wts["lru_lambda"], name=n("lru_fwd"))
    ya = _mm(ya_pre, wts["w_br_a"], name=n("br_a_fwd"))
    sv.update(ya_pre=ya_pre, hseq=hseq, ya=ya)

    cqn = _rms_fwd(z, wts["mla_q_norm"], width=MLA_Q_LORA, cb=SEG["cq"][3] // MLA_Q_LORA, name=n("q_norm_fwd"))
    ckvn = _rms_fwd(z, wts["mla_kv_norm"], width=MLA_KV_LORA, cb=SEG["ckv"][3] // MLA_KV_LORA,
                    name=n("kv_norm_fwd"))
    qp, qp16 = _mm(cqn, wts["mla_wuq"], also_mxu=True, name=n("wuq_fwd"))
    kv = _mm(ckvn, wts["mla_wukv"], out_dtype=MXU_DTYPE, name=n("wukv_fwd"))
    q_rot = _rope_lanes(qp, tabs["cos128"], tabs["sin128"], cb=0, groups=HEADS, out_dtype=MXU_DTYPE,
                        name=n("q_rope_fwd"))
    k_rot = _rope_lanes(z, tabs["cos128"], tabs["sin128"], cb=SEG["kr"][3] // lanes, groups=1, out_dtype=MXU_DTYPE,
                        name=n("k_rope_fwd"))
    mla_ops = (qp16, HEADS, kv, 0, kv, HEAD_PAIRS)
    ob_flat, lse_b, lse_b_row = _pair_fwd(*mla_ops, (q_rot, k_rot), None, scale=(MLA_NOPE + MLA_ROPE) ** -0.5,
                                          unit=CHUNK, name=n("mla_attn_fwd"))
    yb = _mm(ob_flat, wts["w_br_b"], name=n("br_b_fwd"))
    sv.update(cqn=cqn, ckvn=ckvn, mla_ops=mla_ops, mla_rot=(q_rot, k_rot), lse_b=lse_b, lse_b_row=lse_b_row,
              ob_flat=ob_flat, yb=yb)

    cum = _fox_cum(z, wts["fox_bf"], name=n("fox_cum_fwd"))
    cum_h = cum[:, :HEADS].T
    fox_decay = (cum_h[:, :, None], cum_h[:, None, :])
    fox_ops = (z16, SEG["fq"][3] // lanes, z16, SEG["fk"][3] // lanes, z16, SEG["fv"][3] // lanes)
    oc_flat, lse_c, lse_c_row = _pair_fwd(*fox_ops, None, fox_decay, scale=FOX_HEAD_DIM ** -0.5, unit=1,
                                          name=n("fox_attn_fwd"))
    yc = _mm(oc_flat, wts["w_br_c"], name=n("br_c_fwd"))
    sv.update(fox_ops=fox_ops, fox_decay=fox_decay, lse_c=lse_c, lse_c_row=lse_c_row, oc_flat=oc_flat, yc=yc)

    merged = _merge_fwd(ya, yb, yc, z, wts["gate_b"], name=n("merge_fwd"))
    x1 = _mm(merged, wts["w_o"], res=x, name=n("w_o_fwd"))
    n2 = _rms_fwd(x1, wts["ffn_norm"], width=D_MODEL, name=n("ffn_norm_fwd"))
    hf = _mm(n2, wts["w_gate_up"], name=n("gate_up_fwd"))
    act = _swiglu_fwd(hf, name=n("swiglu_fwd"))
    x2 = _mm(act, wts["w_down"], res=x1, name=n("down_fwd"))
    n3 = _rms_fwd(x2, wts["ple_norm"], width=D_MODEL, name=n("ple_norm_fwd"))
    lg = _mm(n3, wts["w_ple_gate"], name=n("ple_gate_fwd"))
    pe = _mm(p_i, wts["w_ple"], name=n("ple_fwd_mm"))
    x3 = _ple_fwd(x2, lg, pe, name=n("ple_fwd"))
    sv.update(merged=merged, x1=x1, n2=n2, hf=hf, act=act, x2=x2, n3=n3, lg=lg, pe=pe, p_i=p_i)
    return x3, sv


def _layer_bwd(dx3, sv, wts, tabs, tag):
    n = functools.partial(lambda base, t=tag: f"{base}_{t}")
    gr = {}
    z = sv["z"]
    s = z.shape[0]

    dpe, dlg = _ple_bwd(dx3, sv["lg"], sv["pe"], name=n("ple_bwd"))
    gr["w_ple"] = _mm(sv["p_i"], dpe, ta=True, name=n("ple_dw"))
    gr["w_ple_gate"] = _mm(sv["n3"], dlg, ta=True, name=n("ple_gate_dw"))
    dn3 = _mm(dlg, wts["w_ple_gate"], tb=True, name=n("ple_gate_dx"))
    dx2, gr["ple_norm"] = _rms_bwd(sv["x2"], wts["ple_norm"], dn3, width=D_MODEL, res=dx3, name=n("ple_norm_bwd"))

    dact = _mm(dx2, wts["w_down"], tb=True, name=n("down_dx"))
    gr["w_down"] = _mm(sv["act"], dx2, ta=True, name=n("down_dw"))
    dhf = _swiglu_bwd(sv["hf"], dact, name=n("swiglu_bwd"))
    gr["w_gate_up"] = _mm(sv["n2"], dhf, ta=True, name=n("gate_up_dw"))
    dn2 = _mm(dhf, wts["w_gate_up"], tb=True, name=n("gate_up_dx"))
    dx1, gr["ffn_norm"] = _rms_bwd(sv["x1"], wts["ffn_norm"], dn2, width=D_MODEL, res=dx2, name=n("ffn_norm_bwd"))

    dmerged = _mm(dx1, wts["w_o"], tb=True, name=n("w_o_dx"))
    gr["w_o"] = _mm(sv["merged"], dx1, ta=True, name=n("w_o_dw"))
    dya, dyb, dyc, dgl, gr["gate_b"] = _merge_bwd(dmerged, sv["ya"], sv["yb"], sv["yc"], z, wts["gate_b"],
                                                  name=n("merge_bwd"))
    gr["w_br_a"] = _mm(sv["ya_pre"], dya, ta=True, name=n("br_a_dw"))
    gr["w_br_b"] = _mm(sv["ob_flat"], dyb, ta=True, name=n("br_b_dw"))
    gr["w_br_c"] = _mm(sv["oc_flat"], dyc, ta=True, name=n("br_c_dw"))
    dya_pre = _mm(dya, wts["w_br_a"], tb=True, name=n("br_a_dx"))
    dob = _mm(dyb, wts["w_br_b"], tb=True, out_dtype=MXU_DTYPE, name=n("br_b_dx"))
    doc = _mm(dyc, wts["w_br_c"], tb=True, out_dtype=MXU_DTYPE, name=n("br_c_dx"))

    (dz_a, gr["conv_w"], gr["conv_b"], dwa, gr["lru_ba"], dwx, gr["lru_bx"], gr["lru_lambda"]) = _lru_bwd(
        z, sv["hseq"], dya_pre, wts["conv_w"], wts["conv_b"], wts["lru_wa"], wts["lru_ba"], wts["lru_wx"],
        wts["lru_bx"], wts["lru_lambda"], name=n("lru_bwd"))
    gr["lru_wa"], gr["lru_wx"] = _diag_blocks(dwa), _diag_blocks(dwx)

    scale_b = (MLA_NOPE + MLA_ROPE) ** -0.5
    delta_b = _pair_delta(*sv["mla_ops"], dob, sv["lse_b"], sv["mla_rot"], None,
                          scale=scale_b, unit=CHUNK, name=n("mla_attn_delta"))
    dk_nope, dv_mla, dq_nope, dk_rot, dq_rot = _pair_bwd_kv(
        *sv["mla_ops"], dob, sv["lse_b_row"], delta_b, sv["mla_rot"], None,
        scale=scale_b, unit=CHUNK, name=n("mla_attn_bwd"))
    dq_rope = _rope_lanes(dq_rot, tabs["cos128"], -tabs["sin128"], cb=0, groups=HEADS, out_dtype=MXU_DTYPE,
                          name=n("q_rope_bwd"))
    dk_rope = _rope_lanes(dk_rot, tabs["cos128"], -tabs["sin128"], cb=0, groups=1, out_dtype=MXU_DTYPE,
                          sum_parts=HEAD_PAIRS, name=n("k_rope_bwd"))
    dqp = jnp.concatenate([dq_rope, dq_nope], axis=-1)
    dkv = jnp.concatenate([dk_nope, dv_mla], axis=-1)
    gr["mla_wuq"] = _mm(sv["cqn"], dqp, ta=True, name=n("wuq_dw"))
    gr["mla_wukv"] = _mm(sv["ckvn"], dkv, ta=True, name=n("wukv_dw"))
    dcqn = _mm(dqp, wts["mla_wuq"], tb=True, name=n("wuq_dx"))
    dckvn = _mm(dkv, wts["mla_wukv"], tb=True, name=n("wukv_dx"))
    dcq, gr["mla_q_norm"] = _rms_bwd(z, wts["mla_q_norm"], dcqn, width=MLA_Q_LORA, cb=SEG["cq"][3] // MLA_Q_LORA,
                                     out_dtype=MXU_DTYPE, name=n("q_norm_bwd"))
    dckv, gr["mla_kv_norm"] = _rms_bwd(z, wts["mla_kv_norm"], dckvn, width=MLA_KV_LORA,
                                       cb=SEG["ckv"][3] // MLA_KV_LORA, out_dtype=MXU_DTYPE, name=n("kv_norm_bwd"))

    scale_c = FOX_HEAD_DIM ** -0.5
    delta_c = _pair_delta(*sv["fox_ops"], doc, sv["lse_c"], None, sv["fox_decay"],
                          scale=scale_c, unit=1, name=n("fox_attn_delta"))
    dfk, dfv, dfq, dcum = _pair_bwd_kv(*sv["fox_ops"], doc, sv["lse_c_row"], delta_c, None, sv["fox_decay"],
                                       scale=scale_c, unit=1, name=n("fox_attn_bwd"))
    dcum_rows = _lane_pad(dcum.reshape(HEADS, s).T, V7X_LANES)
    dfl, dbf = _fox_cum_bwd(z, wts["fox_bf"], dcum_rows, name=n("fox_cum_bwd"))
    gr["fox_bf"] = dbf[:, :HEADS]

    zero = lambda width: jnp.zeros((s, width), MXU_DTYPE)
    dz = jnp.concatenate([dz_a, zero(128), dcq, dckv, dk_rope, zero(128), dfq, dfk, dfv, dfl, zero(384), dgl],
                         axis=-1)
    gr["w_in"] = _mm(sv["n1"], dz, ta=True, name=n("w_in_dw"))
    dn1 = _mm(dz, wts["w_in"], tb=True, name=n("w_in_dx"))
    dx, gr["mix_norm"] = _rms_bwd(sv["x"], wts["mix_norm"], dn1, width=D_MODEL, res=dx1, name=n("mix_norm_bwd"))
    return dx, gr


def _rope_tables(s):
    pos = jnp.arange(s, dtype=F32)
    inv_freq = ROPE_BASE ** (-jnp.arange(0, MLA_ROPE, 2, dtype=F32) / MLA_ROPE)
    ang = pos[:, None] * inv_freq[None, :]
    cos, sin = jnp.cos(ang), jnp.sin(ang)
    cos32 = jnp.concatenate([cos, cos], axis=-1)
    sin32 = jnp.concatenate([-sin, sin], axis=-1)
    return {"cos256": jnp.tile(cos32, (1, 8)), "sin256": jnp.tile(sin32, (1, 8)),
            "cos128": jnp.tile(cos32, (1, 4)), "sin128": jnp.tile(sin32, (1, 4))}


def _gather_weights(shards):
    names = [nm for nm, _ in SHARDED]
    got = _all_gather_many([shards[nm] if nm == "conv_w" else shards[nm].astype(MXU_DTYPE) for nm in names],
                           name="weights_all_gather")
    full = {}
    for (nm, axis), blk in zip(SHARDED, got):
        shp = shards[nm].shape
        if axis == 2:
            full[nm] = blk.transpose(1, 2, 0, 3).reshape(shp[0], shp[1], N_DEV * shp[2])
        else:
            full[nm] = blk.transpose(1, 0, 2, 3).reshape(shp[0], N_DEV * shp[1], shp[2])
    return full


def _to_dest_major(g, axis):
    d0, r, c = g.shape
    if axis == 2:
        by_dev = g.reshape(d0, r, N_DEV, c // N_DEV).transpose(2, 0, 1, 3)
    else:
        by_dev = g.reshape(d0, N_DEV, r // N_DEV, c).transpose(1, 0, 2, 3)
    shp = by_dev.shape[1:]
    return by_dev.reshape((CHIPS, 2) + shp).transpose(1, 0, 2, 3, 4).reshape((N_DEV,) + shp)


def kernel(x, p, mix_norm, w_in, gate_b, conv_w, conv_b, lru_wa, lru_ba, lru_wx, lru_bx, lru_lambda, mla_q_norm, mla_wuq, mla_kv_norm, mla_wukv, fox_bf, w_br_a, w_br_b, w_br_c, w_o, ffn_norm, w_gate_up, w_down, ple_norm, w_ple_gate, w_ple, final_norm, loss_target, m_mix_norm, m_w_in, m_gate_b, m_conv_w, m_conv_b, m_lru_wa, m_lru_ba, m_lru_wx, m_lru_bx, m_lru_lambda, m_mla_q_norm, m_mla_wuq, m_mla_kv_norm, m_mla_wukv, m_fox_bf, m_w_br_a, m_w_br_b, m_w_br_c, m_w_o, m_ffn_norm, m_w_gate_up, m_w_down, m_ple_norm, m_w_ple_gate, m_w_ple, m_final_norm, v_mix_norm, v_w_in, v_gate_b, v_conv_w, v_conv_b, v_lru_wa, v_lru_ba, v_lru_wx, v_lru_bx, v_lru_lambda, v_mla_q_norm, v_mla_wuq, v_mla_kv_norm, v_mla_wukv, v_fox_bf, v_w_br_a, v_w_br_b, v_w_br_c, v_w_o, v_ffn_norm, v_w_gate_up, v_w_down, v_ple_norm, v_w_ple_gate, v_w_ple, v_final_norm):
    given = dict(locals())
    w_loc = {nm: given[nm] for nm in WEIGHTS}
    m_loc = {nm: given["m_" + nm] for nm in WEIGHTS}
    v_loc = {nm: given["v_" + nm] for nm in WEIGHTS}
    xs = x[0]
    s = xs.shape[0]
    tabs = _rope_tables(s)

    full = _gather_weights({nm: w_loc[nm] for nm, _ in SHARDED})
    full["w_in"] = _pad_w_in(full["w_in"])
    wq = full["mla_wuq"].reshape(DEPTH, MLA_Q_LORA, HEADS, MLA_NOPE + MLA_ROPE)
    wq_rot = jnp.pad(wq[..., MLA_NOPE:], ((0, 0), (0, 0), (0, 0), (0, V7X_LANES - MLA_ROPE)))
    full["mla_wuq"] = jnp.concatenate([wq_rot.reshape(DEPTH, MLA_Q_LORA, -1),
                                       wq[..., :MLA_NOPE].reshape(DEPTH, MLA_Q_LORA, -1)], axis=-1)
    wkv = full["mla_wukv"].reshape(DEPTH, MLA_KV_LORA, HEADS, MLA_NOPE + MLA_V)
    full["mla_wukv"] = jnp.concatenate([wkv[..., :MLA_NOPE].reshape(DEPTH, MLA_KV_LORA, -1),
                                        wkv[..., MLA_NOPE:].reshape(DEPTH, MLA_KV_LORA, -1)], axis=-1)

    def layer_weights(i):
        wts = {nm: full[nm][i] for nm, _ in SHARDED}
        for nm in ("mix_norm", "gate_b", "conv_b", "lru_ba", "lru_bx", "lru_lambda", "mla_q_norm", "mla_kv_norm",
                   "ffn_norm", "ple_norm"):
            wts[nm] = w_loc[nm][i][None, :]
        wts["fox_bf"] = _lane_pad(w_loc["fox_bf"][i][None, :], V7X_LANES)
        wts["lru_wa"] = _block_diag(w_loc["lru_wa"][i]).astype(MXU_DTYPE)
        wts["lru_wx"] = _block_diag(w_loc["lru_wx"][i]).astype(MXU_DTYPE)
        return wts

    layers = [layer_weights(i) for i in range(DEPTH)]

    h = xs
    saved = []
    for i in range(DEPTH):
        h, sv = _layer_fwd(h, p[i, 0].astype(MXU_DTYPE), layers[i], tabs, f"l{i}")
        saved.append(sv)
    loss_blk, dh, dg_final = _final_loss(h, w_loc["final_norm"][None, :], loss_target[0], name="final_loss")
    loss = lax.psum(loss_blk[0, 0], ("x", "y", "c"))

    grads = [None] * DEPTH
    for i in reversed(range(DEPTH)):
        dh, grads[i] = _layer_bwd(dh, saved[i], layers[i], tabs, f"l{i}")
    grad_x = dh[None]

    def stacked(nm):
        return jnp.stack([grads[i][nm] for i in range(DEPTH)])

    gfull = {}
    for nm, _ in SHARDED:
        gfull[nm] = stacked(nm)
    gfull["w_in"] = _unpad_w_in(gfull["w_in"])
    gq = gfull["mla_wuq"]
    rot_w = HEADS * V7X_LANES
    gfull["mla_wuq"] = jnp.concatenate(
        [gq[..., rot_w:].reshape(DEPTH, MLA_Q_LORA, HEADS, MLA_NOPE),
         gq[..., :rot_w].reshape(DEPTH, MLA_Q_LORA, HEADS, V7X_LANES)[..., :MLA_ROPE]],
        axis=-1).reshape(DEPTH, MLA_Q_LORA, -1)
    gkv = gfull["mla_wukv"]
    gfull["mla_wukv"] = jnp.concatenate(
        [gkv[..., :512].reshape(DEPTH, MLA_KV_LORA, HEADS, MLA_NOPE),
         gkv[..., 512:].reshape(DEPTH, MLA_KV_LORA, HEADS, MLA_V)], axis=-1).reshape(DEPTH, MLA_KV_LORA, -1)

    pays = [_to_dest_major(gfull[nm], ax).astype(MXU_DTYPE) for nm, ax in SHARDED]
    theirs = _sibling_exchange_many(pays, name="grads_sibling_exchange")
    own_first = CHIPS * lax.axis_index("c")
    mine = [lax.dynamic_slice_in_dim(pay, own_first, CHIPS, axis=0) for pay in pays]
    pair_sums = [_pair_sum(a, b, name=f"grads_pair_sum_{nm}") for (nm, _), a, b in zip(SHARDED, mine, theirs)]
    parts = _chip_exchange_many(pair_sums, name="grads_chip_exchange")
    res_s = [{}, {}, {}, {}]
    for (nm, _), part in zip(SHARDED, parts):
        outs = _adamw_nd(part, w_loc[nm], m_loc[nm], v_loc[nm], name=f"adamw_{nm}")
        for kind in range(4):
            res_s[kind][nm] = outs[kind]

    small = {nm: stacked(nm) for nm in REPLICATED if nm != "final_norm"}
    small["final_norm"] = dg_final
    names_r = list(REPLICATED)
    shapes_r = [w_loc[nm].shape for nm in names_r]
    parts_r = _all_gather(_flat_rows([small[nm] for nm in names_r], 8), name="small_grads_all_gather")
    outs_r = _adamw(parts_r, _flat_rows([w_loc[nm] for nm in names_r], 8),
                    _flat_rows([m_loc[nm] for nm in names_r], 8),
                    _flat_rows([v_loc[nm] for nm in names_r], 8), name="adamw_replicated")
    res_r = [dict(zip(names_r, _split_flat(o, shapes_r))) for o in outs_r]

    out = [loss, grad_x]
    for kind in range(4):
        for nm in WEIGHTS:
            out.append(res_s[kind][nm] if nm in res_s[kind] else res_r[kind][nm])
    return tuple(out)
```

```python
import functools
import math

import jax
import jax.numpy as jnp
from jax import lax
from jax.experimental import pallas as pl
from jax.experimental.pallas import tpu as pltpu

F32 = jnp.float32
BF16 = jnp.bfloat16
MXU_DTYPE = jnp.bfloat16

D_MODEL = 1024
DEPTH = 2
CHUNK = 64
EPS = 1e-6
NEG_INF = -1e30
LRU_WIDTH = 512
LRU_HEADS = 8
LRU_HEAD_DIM = 64
CONV_WIDTH = 4
LRU_C = 8.0
HEADS = 8
MLA_Q_LORA = 384
MLA_KV_LORA = 256
MLA_NOPE = 64
MLA_ROPE = 32
MLA_V = 64
ROPE_BASE = 10000.0
FOX_HEAD_DIM = 64
FOX_WIDTH = 512
D_FF = 2816
PLE_DIM = 256
D_IN = 6312
ADAM_LR = 0.001
ADAM_B1 = 0.9
ADAM_B2 = 0.999
ADAM_EPS = 1e-08
ADAM_WD = 0.01
ADAM_STEP = 10

V7X_VMEM_BYTES = 64 * 1024 * 1024
V7X_LANES = 128
V7X_SUBLANES = 8
VMEM_LIMIT_CAP = 56 * 1024 * 1024
N_DEV = 8

SEGS = (
    ("u", 0, 512, 0, 512),
    ("ug", 512, 512, 512, 512),
    ("cq", 1024, 384, 1152, 384),
    ("ckv", 1408, 256, 1536, 256),
    ("kr", 1664, 32, 1792, 128),
    ("fq", 1696, 512, 2048, 512),
    ("fk", 2208, 512, 2560, 512),
    ("fv", 2720, 512, 3072, 512),
    ("fl", 3232, 8, 3584, 128),
    ("gate", 3240, 3072, 4096, 1024),
)
D_IN_PAD = 7168
SEG = {s[0]: s for s in SEGS}

SHARDED = (("w_in", 2), ("mla_wuq", 2), ("mla_wukv", 2), ("w_br_a", 2), ("w_br_b", 2), ("w_br_c", 2),
           ("w_o", 1), ("w_gate_up", 2), ("w_down", 1), ("w_ple_gate", 1), ("w_ple", 2), ("conv_w", 2))
REPLICATED = ("mix_norm", "gate_b", "conv_b", "lru_wa", "lru_ba", "lru_wx", "lru_bx", "lru_lambda",
              "mla_q_norm", "mla_kv_norm", "fox_bf", "ffn_norm", "ple_norm", "final_norm")
WEIGHTS = ("mix_norm", "w_in", "gate_b", "conv_w", "conv_b", "lru_wa", "lru_ba", "lru_wx", "lru_bx",
           "lru_lambda", "mla_q_norm", "mla_wuq", "mla_kv_norm", "mla_wukv", "fox_bf", "w_br_a", "w_br_b",
           "w_br_c", "w_o", "ffn_norm", "w_gate_up", "w_down", "ple_norm", "w_ple_gate", "w_ple", "final_norm")
PAYLOAD_LANES = 1024


def _tile(n, cap=1024):
    best = None
    for t in range(V7X_LANES, min(n, cap) + 1, V7X_LANES):
        if n % t == 0:
            best = t
    return best if best is not None else n


def _row_tile(s, pref):
    t = min(pref, s // 2)
    assert s % t == 0 and t % V7X_SUBLANES == 0
    return t


def _nbytes(shape, dtype):
    return math.prod(shape) * jnp.dtype(dtype).itemsize


def _params(sem, vmem_bytes):
    limit = int(min(VMEM_LIMIT_CAP, max(16 * 1024 * 1024, vmem_bytes)))
    return pltpu.CompilerParams(dimension_semantics=sem, vmem_limit_bytes=limit)


def _full(shape):
    return pl.BlockSpec(shape, lambda *_: (0,) * len(shape))


def _rows(t, w, cb=0):
    return pl.BlockSpec((t, w), lambda i: (i, cb))


def _mxu(v):
    return v.astype(MXU_DTYPE)


def _dot(a, b):
    return lax.dot_general(_mxu(a), _mxu(b), (((1,), (0,)), ((), ())), preferred_element_type=F32)


def _dot_nt(a, b):
    return lax.dot_general(_mxu(a), _mxu(b), (((1,), (1,)), ((), ())), preferred_element_type=F32)


def _dot_tn(a, b):
    return lax.dot_general(_mxu(a), _mxu(b), (((0,), (0,)), ((), ())), preferred_element_type=F32)


def _sigmoid(v):
    return 1.0 / (1.0 + jnp.exp(-v))


def _softplus(v):
    return jnp.maximum(v, 0.0) + jnp.log(1.0 + jnp.exp(-jnp.abs(v)))


def _neg_expm1(v):
    series = -v * (1.0 + v * (0.5 + v * (1.0 / 6.0 + v * (1.0 / 24.0))))
    return jnp.where(v > -0.03, series, 1.0 - jnp.exp(v))


_GELU_C = math.sqrt(2.0 / math.pi)
_GELU_A = 0.044715


def _gelu(v):
    t = jnp.tanh(_GELU_C * (v + _GELU_A * v * v * v))
    return 0.5 * v * (1.0 + t)


def _gelu_grad(v):
    t = jnp.tanh(_GELU_C * (v + _GELU_A * v * v * v))
    return 0.5 * (1.0 + t) + 0.5 * v * (1.0 - t * t) * _GELU_C * (1.0 + 3.0 * _GELU_A * v * v)


def _mm(a, b, *, ta=False, tb=False, out_dtype=F32, res=None, also_mxu=False, name):
    k_dim, m_dim = (a.shape[0], a.shape[1]) if ta else (a.shape[1], a.shape[0])
    n_dim = b.shape[0] if tb else b.shape[1]
    assert (b.shape[1] if tb else b.shape[0]) == k_dim
    tm, tn, tk = _tile(m_dim, 1408), _tile(n_dim, 1408), _tile(k_dim, 1408)
    nk = k_dim // tk
    a_spec = pl.BlockSpec((tk, tm), lambda i, j, k: (k, i)) if ta else pl.BlockSpec((tm, tk), lambda i, j, k: (i, k))
    b_spec = pl.BlockSpec((tn, tk), lambda i, j, k: (j, k)) if tb else pl.BlockSpec((tk, tn), lambda i, j, k: (k, j))
    o_spec = pl.BlockSpec((tm, tn), lambda i, j, k: (i, j))
    has_res = res is not None

    def body(*refs):
        a_ref, b_ref = refs[0], refs[1]
        res_ref = refs[2] if has_res else None
        o_ref = refs[3] if has_res else refs[2]
        o2_ref = refs[-2] if also_mxu else None
        acc_ref = refs[-1]
        k = pl.program_id(2)
        if ta:
            part = _dot_tn(a_ref[...], b_ref[...])
        elif tb:
            part = _dot_nt(a_ref[...], b_ref[...])
        else:
            part = _dot(a_ref[...], b_ref[...])

        def finish(total):
            if has_res:
                total = total + res_ref[...].astype(F32)
            o_ref[...] = total.astype(o_ref.dtype)
            if also_mxu:
                o2_ref[...] = total.astype(o2_ref.dtype)

        if nk == 1:
            finish(part)
        else:
            @pl.when(k == 0)
            def _():
                acc_ref[...] = part

            @pl.when(jnp.logical_and(k > 0, k < nk - 1))
            def _():
                acc_ref[...] += part

            @pl.when(k == nk - 1)
            def _():
                finish(acc_ref[...] + part)

    ins = [a, b] + ([res] if has_res else [])
    in_specs = [a_spec, b_spec] + ([o_spec] if has_res else [])
    acc_shape = (tm, tn) if nk > 1 else (V7X_SUBLANES, V7X_LANES)
    vmem = (2 * (_nbytes((tm, tk), a.dtype) + _nbytes((tk, tn), b.dtype) + _nbytes((tm, tn), out_dtype)
                 + (_nbytes((tm, tn), res.dtype) if has_res else 0))
            + _nbytes((tm, tk), MXU_DTYPE) + _nbytes((tk, tn), MXU_DTYPE) + 3 * _nbytes((tm, tn), F32))
    return pl.pallas_call(
        body, name=name, grid=(m_dim // tm, n_dim // tn, nk),
        in_specs=in_specs, out_specs=[o_spec, o_spec] if also_mxu else o_spec,
        out_shape=([jax.ShapeDtypeStruct((m_dim, n_dim), out_dtype), jax.ShapeDtypeStruct((m_dim, n_dim), MXU_DTYPE)]
                   if also_mxu else jax.ShapeDtypeStruct((m_dim, n_dim), out_dtype)),
        scratch_shapes=[pltpu.VMEM(acc_shape, F32)],
        compiler_params=_params(("parallel", "parallel", "arbitrary"), vmem),
    )(*ins)


def _rms_fwd(x, g, *, width, cb=0, name):
    s = x.shape[0]
    t = _tile(s, 512)

    def body(x_ref, g_ref, o_ref):
        xv = x_ref[...].astype(F32)
        r = lax.rsqrt(jnp.mean(xv * xv, axis=-1, keepdims=True) + EPS)
        o_ref[...] = (xv * r * g_ref[...]).astype(o_ref.dtype)

    return pl.pallas_call(
        body, name=name, grid=(s // t,),
        in_specs=[_rows(t, width, cb), _full((1, width))], out_specs=_rows(t, width),
        out_shape=jax.ShapeDtypeStruct((s, width), MXU_DTYPE),
        compiler_params=_params(("parallel",), 8 * _nbytes((t, width), F32)),
    )(x, g)


def _rms_bwd(x, g, dn, *, width, cb=0, res=None, out_dtype=F32, name):
    s = x.shape[0]
    t = _tile(s, 256)
    has_res = res is not None

    def body(*refs):
        x_ref, g_ref, dn_ref = refs[:3]
        res_ref = refs[3] if has_res else None
        dx_ref, dg_ref = refs[-2], refs[-1]
        xv = x_ref[...].astype(F32)
        dnv = dn_ref[...].astype(F32)
        r = lax.rsqrt(jnp.mean(xv * xv, axis=-1, keepdims=True) + EPS)
        xr = xv * r
        dng = dnv * g_ref[...]
        dx = r * dng - xr * (r * r) * jnp.mean(dng * xv, axis=-1, keepdims=True)
        if has_res:
            dx = dx + res_ref[...].astype(F32)
        dx_ref[...] = dx.astype(dx_ref.dtype)
        part = jnp.sum(dnv * xr, axis=0, keepdims=True)

        @pl.when(pl.program_id(0) == 0)
        def _():
            dg_ref[...] = part

        @pl.when(pl.program_id(0) > 0)
        def _():
            dg_ref[...] += part

    ins = [x, g, dn] + ([res] if has_res else [])
    in_specs = [_rows(t, width, cb), _full((1, width)), _rows(t, width)] + ([_rows(t, width)] if has_res else [])
    return pl.pallas_call(
        body, name=name, grid=(s // t,),
        in_specs=in_specs, out_specs=[_rows(t, width), _full((1, width))],
        out_shape=[jax.ShapeDtypeStruct((s, width), out_dtype), jax.ShapeDtypeStruct((1, width), F32)],
        compiler_params=_params(("arbitrary",), 16 * _nbytes((t, width), F32)),
    )(*ins)


def _final_loss(x, g, target, *, name):
    s, d = x.shape
    t = _tile(s, 256)

    def body(x_ref, g_ref, t_ref, loss_ref, dx_ref, dg_ref):
        xv = x_ref[...]
        r = lax.rsqrt(jnp.mean(xv * xv, axis=-1, keepdims=True) + EPS)
        xr = xv * r
        err = xr * g_ref[...] - t_ref[...]
        part_loss = 0.5 * jnp.sum(jnp.mean(err * err, axis=-1, keepdims=True), axis=0, keepdims=True)
        dnv = err * (1.0 / d)
        dng = dnv * g_ref[...]
        dx_ref[...] = r * dng - xr * (r * r) * jnp.mean(dng * xv, axis=-1, keepdims=True)
        part_dg = jnp.sum(dnv * xr, axis=0, keepdims=True)

        @pl.when(pl.program_id(0) == 0)
        def _():
            dg_ref[...] = part_dg
            loss_ref[...] = jnp.zeros(loss_ref.shape, F32) + part_loss

        @pl.when(pl.program_id(0) > 0)
        def _():
            dg_ref[...] += part_dg
            loss_ref[...] += part_loss

    return pl.pallas_call(
        body, name=name, grid=(s // t,),
        in_specs=[_rows(t, d), _full((1, d)), _rows(t, d)],
        out_specs=[_full((V7X_SUBLANES, V7X_LANES)), _rows(t, d), _full((1, d))],
        out_shape=[jax.ShapeDtypeStruct((V7X_SUBLANES, V7X_LANES), F32), jax.ShapeDtypeStruct((s, d), F32),
                   jax.ShapeDtypeStruct((1, d), F32)],
        compiler_params=_params(("arbitrary",), 16 * _nbytes((t, d), F32)),
    )(x, g, target)


def _shift_down(v, d, fill, rows):
    return jnp.where(rows >= d, pltpu.roll(v, d, 0), fill)


def _shift_up(v, d, fill, rows, t):
    return jnp.where(rows < t - d, pltpu.roll(v, t - d, 0), fill)


def _lru_gates(xc, wa_ref, ba_ref, wx_ref, bx_ref, lam_ref):
    ra = _sigmoid(_dot(xc, wa_ref[...]) + ba_ref[...])
    ig = _sigmoid(_dot(xc, wx_ref[...]) + bx_ref[...])
    sp = _softplus(-lam_ref[...])
    log_a = -LRU_C * ra * sp
    a = jnp.exp(log_a)
    s2 = _neg_expm1(2.0 * log_a)
    return ra, ig, sp, a, s2


def _conv(ubuf, cw_ref, cb_ref, t):
    big = ubuf[...]
    shifted = [pltpu.roll(big, CONV_WIDTH - 1 - k, 0)[V7X_SUBLANES:t + V7X_SUBLANES] if k < CONV_WIDTH - 1
               else big[V7X_SUBLANES:t + V7X_SUBLANES] for k in range(CONV_WIDTH)]
    xc = cb_ref[...] + shifted[0] * cw_ref[0:1, :]
    for k in range(1, CONV_WIDTH):
        xc = xc + shifted[k] * cw_ref[k:k + 1, :]
    return xc, shifted


def _lru_fwd(z, cw, cb, wa, ba, wx, bx, lam, *, name):
    s = z.shape[0]
    w = LRU_WIDTH
    t = _row_tile(s, 256)
    steps = [1 << k for k in range(int(math.log2(t)))]

    def body(u_ref, ug_ref, cw_ref, cb_ref, wa_ref, ba_ref, wx_ref, bx_ref, lam_ref, y_ref, h_ref, ubuf, hc):
        @pl.when(pl.program_id(0) == 0)
        def _():
            ubuf[0:V7X_SUBLANES, :] = jnp.zeros((V7X_SUBLANES, w), F32)
            hc[...] = jnp.zeros_like(hc)

        ubuf[V7X_SUBLANES:t + V7X_SUBLANES, :] = u_ref[...]
        xc, _ = _conv(ubuf, cw_ref, cb_ref, t)
        _, ig, _, a, s2 = _lru_gates(xc, wa_ref, ba_ref, wx_ref, bx_ref, lam_ref)
        b = jnp.sqrt(s2) * (ig * xc)
        rows = lax.broadcasted_iota(jnp.int32, (t, w), 0)
        for d in steps:
            b = a * _shift_down(b, d, 0.0, rows) + b
            a = a * _shift_down(a, d, 1.0, rows)
        h = a * hc[0:1, :] + b
        h_ref[...] = h
        y_ref[...] = (h * _gelu(ug_ref[...])).astype(y_ref.dtype)
        hc[0:1, :] = h_ref[t - 1:t, :]
        ubuf[0:V7X_SUBLANES, :] = ubuf[t:t + V7X_SUBLANES, :]

    vec = _full((1, w))
    return pl.pallas_call(
        body, name=name, grid=(s // t,),
        in_specs=[_rows(t, w, 0), _rows(t, w, 1), _full((CONV_WIDTH, w)), vec, _full((w, w)), vec, _full((w, w)),
                  vec, vec],
        out_specs=[_rows(t, w), _rows(t, w)],
        out_shape=[jax.ShapeDtypeStruct((s, w), MXU_DTYPE), jax.ShapeDtypeStruct((s, w), F32)],
        scratch_shapes=[pltpu.VMEM((t + V7X_SUBLANES, w), F32), pltpu.VMEM((V7X_SUBLANES, w), F32)],
        compiler_params=_params(("arbitrary",), 40 * _nbytes((t, w), F32)),
    )(z, z, cw, cb, wa, ba, wx, bx, lam)


def _lru_bwd(z, h, dy, cw, cb, wa, ba, wx, bx, lam, *, name):
    s = z.shape[0]
    w = LRU_WIDTH
    t = _row_tile(s, 256)
    nt = s // t
    per8 = t // V7X_SUBLANES
    steps = [1 << k for k in range(int(math.log2(t)))]

    def body(u_ref, ug_ref, h_ref, dy_ref, uprev_ref, hprev_ref, cw_ref, cb_ref, wa_ref, ba_ref, wx_ref, bx_ref,
             lam_ref, dz_ref, dcw_ref, dcb_ref, dwa_ref, dba_ref, dwx_ref, dbx_ref, dlam_ref,
             ubuf, dbuf, acar, dhcar, tmp):
        i = pl.program_id(0)
        first_tile = i == nt - 1

        @pl.when(i == 0)
        def _():
            for r in (dcw_ref, dcb_ref, dwa_ref, dba_ref, dwx_ref, dbx_ref, dlam_ref, acar, dhcar):
                r[...] = jnp.zeros_like(r)
            dbuf[t:t + V7X_SUBLANES, :] = jnp.zeros((V7X_SUBLANES, w), F32)

        keep = jnp.where(first_tile, 0.0, 1.0)
        ubuf[0:V7X_SUBLANES, :] = uprev_ref[...] * keep
        ubuf[V7X_SUBLANES:t + V7X_SUBLANES, :] = u_ref[...]
        xc, shifted = _conv(ubuf, cw_ref, cb_ref, t)
        ra, ig, sp, a, s2 = _lru_gates(xc, wa_ref, ba_ref, wx_ref, bx_ref, lam_ref)
        sq = jnp.sqrt(s2)
        gx = ig * xc
        rows = lax.broadcasted_iota(jnp.int32, (t, w), 0)
        ugv = ug_ref[...]
        dyv = dy_ref[...].astype(F32)
        hv = h_ref[...]

        acc_g = dyv * _gelu(ugv)
        acc_a = _shift_up(a, 1, acar[0:1, :], rows, t)
        for d in steps:
            acc_g = acc_a * _shift_up(acc_g, d, 0.0, rows, t) + acc_g
            acc_a = acc_a * _shift_up(acc_a, d, 1.0, rows, t)
        dh = acc_a * dhcar[0:1, :] + acc_g

        hprev = _shift_down(hv, 1, hprev_ref[V7X_SUBLANES - 1:V7X_SUBLANES, :] * keep, rows)
        d_a = dh * hprev
        d_sq = dh * gx
        d_gx = dh * sq
        d_ig = d_gx * xc
        dxc = d_gx * ig
        d_log_a = d_a * a - d_sq * (1.0 - s2) / sq
        d_ra = d_log_a * (-LRU_C * sp)
        lamv = lam_ref[...]
        dlam_ref[...] += jnp.sum(d_log_a * (-LRU_C * ra), axis=0, keepdims=True) * (-_sigmoid(-lamv))
        dpa = d_ra * ra * (1.0 - ra)
        dpx = d_ig * ig * (1.0 - ig)
        dba_ref[...] += jnp.sum(dpa, axis=0, keepdims=True)
        dbx_ref[...] += jnp.sum(dpx, axis=0, keepdims=True)
        dwa_ref[...] += _dot_tn(xc, dpa)
        dwx_ref[...] += _dot_tn(xc, dpx)
        dxc = dxc + _dot_nt(dpa, wa_ref[...]) + _dot_nt(dpx, wx_ref[...])
        dcb_ref[...] += jnp.sum(dxc, axis=0, keepdims=True)
        for k in range(CONV_WIDTH):
            dcw_ref[k:k + 1, :] += jnp.sum(dxc * shifted[k], axis=0, keepdims=True)

        dbuf[0:t, :] = dxc
        bigd = dbuf[...]
        du = dxc * cw_ref[CONV_WIDTH - 1:CONV_WIDTH, :]
        for k in range(CONV_WIDTH - 1):
            e = CONV_WIDTH - 1 - k
            du = du + pltpu.roll(bigd, t + V7X_SUBLANES - e, 0)[0:t] * cw_ref[k:k + 1, :]
        dz_ref[:, 0:w] = du.astype(dz_ref.dtype)
        dz_ref[:, w:2 * w] = (dyv * hv * _gelu_grad(ugv)).astype(dz_ref.dtype)

        dbuf[t:t + V7X_SUBLANES, :] = dbuf[0:V7X_SUBLANES, :]
        tmp[...] = a
        acar[0:1, :] = tmp[0:1, :]
        tmp[...] = dh
        dhcar[0:1, :] = tmp[0:1, :]

    vec = _full((1, w))
    rev = lambda cbk: pl.BlockSpec((t, w), lambda i: (nt - 1 - i, cbk))
    prev8 = lambda cbk: pl.BlockSpec((V7X_SUBLANES, w),
                                     lambda i: (jnp.maximum((nt - 1 - i) * per8 - 1, 0), cbk))
    return pl.pallas_call(
        body, name=name, grid=(nt,),
        in_specs=[rev(0), rev(1), rev(0), rev(0), prev8(0), prev8(0), _full((CONV_WIDTH, w)), vec, _full((w, w)),
                  vec, _full((w, w)), vec, vec],
        out_specs=[pl.BlockSpec((t, 2 * w), lambda i: (nt - 1 - i, 0)), _full((CONV_WIDTH, w)), vec,
                   _full((w, w)), vec, _full((w, w)), vec, vec],
        out_shape=[jax.ShapeDtypeStruct((s, 2 * w), MXU_DTYPE), jax.ShapeDtypeStruct((CONV_WIDTH, w), F32),
                   jax.ShapeDtypeStruct((1, w), F32), jax.ShapeDtypeStruct((w, w), F32),
                   jax.ShapeDtypeStruct((1, w), F32), jax.ShapeDtypeStruct((w, w), F32),
                   jax.ShapeDtypeStruct((1, w), F32), jax.ShapeDtypeStruct((1, w), F32)],
        scratch_shapes=[pltpu.VMEM((t + V7X_SUBLANES, w), F32), pltpu.VMEM((t + V7X_SUBLANES, w), F32),
                        pltpu.VMEM((V7X_SUBLANES, w), F32), pltpu.VMEM((V7X_SUBLANES, w), F32),
                        pltpu.VMEM((t, w), F32)],
        compiler_params=_params(("arbitrary",), 80 * _nbytes((t, w), F32)),
    )(z, z, h, dy, z, h, cw, cb, wa, ba, wx, bx, lam)


def _rope_apply(v, cos, sin, width):
    half = MLA_ROPE // 2
    lanes = lax.broadcasted_iota(jnp.int32, v.shape, 1)
    first = (lanes % MLA_ROPE) < half
    partner = jnp.where(first, pltpu.roll(v, width - half, 1), pltpu.roll(v, half, 1))
    return v * cos + partner * sin


def _attn_tile(s):
    return min(512, s // 4)


STRIP = 32
HEAD_PAIRS = HEADS // 2


def _strip_rows(t):
    return min(STRIP, t)


def _split_scale(scale, has_rope):
    if not has_rope and math.frexp(scale)[0] == 0.5:
        return scale, 1.0
    return 1.0, scale


def _pair_mask(t):
    lane = lax.broadcasted_iota(jnp.int32, (t, V7X_LANES), 1)
    return lane < (V7X_LANES // 2)


def _strip_visible(r, t, row0, unit, transposed):
    rows = lax.broadcasted_iota(jnp.int32, (r, t), 0) + row0
    cols = lax.broadcasted_iota(jnp.int32, (r, t), 1)
    shift = int(math.log2(unit))
    if transposed:
        return (cols >> shift) >= (rows >> shift)
    return (rows >> shift) >= (cols >> shift)


def _rope_lanes(x, cos, sin, *, cb, groups, out_dtype, sum_parts=0, name):
    s = cos.shape[0]
    t = _tile(s, 512)
    w = groups * V7X_LANES

    def body(x_ref, c_ref, s_ref, o_ref):
        if sum_parts:
            v = x_ref[0].astype(F32)
            for part in range(1, sum_parts):
                v = v + x_ref[part].astype(F32)
            o_ref[...] = _rope_apply(v, c_ref[...], s_ref[...], V7X_LANES).astype(o_ref.dtype)
        else:
            for g in range(groups):
                sl = slice(g * V7X_LANES, (g + 1) * V7X_LANES)
                o_ref[:, sl] = _rope_apply(x_ref[:, sl].astype(F32), c_ref[...], s_ref[...],
                                           V7X_LANES).astype(o_ref.dtype)

    x_spec = (pl.BlockSpec((sum_parts, t, V7X_LANES), lambda i: (0, i, 0)) if sum_parts else _rows(t, w, cb))
    return pl.pallas_call(
        body, name=name, grid=(s // t,),
        in_specs=[x_spec, _rows(t, V7X_LANES), _rows(t, V7X_LANES)], out_specs=_rows(t, w),
        out_shape=jax.ShapeDtypeStruct((s, w), out_dtype),
        compiler_params=_params(("parallel",), 12 * _nbytes((t, max(w, 4 * V7X_LANES)), F32)),
    )(x, cos, sin)


def _pair_fwd(q_arr, q_cb, k_arr, k_cb, v_arr, v_cb, rope, decay, *, scale, unit, name):
    s = q_arr.shape[0]
    t = _attn_tile(s)
    r = _strip_rows(t)
    has_rope, has_decay = rope is not None, decay is not None
    kw = 2 * V7X_LANES if has_rope else V7X_LANES
    q_mul, s_mul = _split_scale(scale, has_rope)

    def body(*refs):
        it = iter(refs)
        q_ref, k_ref, v_ref = next(it), next(it), next(it)
        qr_ref, kr_ref = (next(it), next(it)) if has_rope else (None, None)
        cq_ref, ck_ref = (next(it), next(it)) if has_decay else (None, None)
        o_ref, lse_ref, lser_ref = next(it), next(it), next(it)
        q_sc, s_sc, p_sc, acc_sc, mx_sc, ls_sc, tr_sc = (next(it) for _ in range(7))
        i = pl.program_id(1)
        in_a = _pair_mask(t)
        qv = q_ref[...] * q_mul
        for hd in range(2):
            q_sc[hd, :, 0:V7X_LANES] = jnp.where(in_a if hd == 0 else jnp.logical_not(in_a), qv, 0).astype(MXU_DTYPE)
            if has_rope:
                q_sc[hd, :, V7X_LANES:kw] = qr_ref[:, hd * V7X_LANES:(hd + 1) * V7X_LANES].astype(MXU_DTYPE)
        mx_sc[...] = jnp.full(mx_sc.shape, NEG_INF, F32)
        ls_sc[...] = jnp.zeros(ls_sc.shape, F32)
        acc_sc[...] = jnp.zeros(acc_sc.shape, F32)
        cq_all = [cq_ref[hd] for hd in range(2)] if has_decay else None
        chunks = t // V7X_LANES

        def keys(j):
            off = pl.multiple_of(j * t, t)
            kt = k_ref[pl.ds(off, t), :]
            if has_rope:
                kt = jnp.concatenate([kt, kr_ref[pl.ds(off, t), :]], axis=-1)
            return off, kt

        def strip_scores(hd, row0, ck_row, masked):
            sc = s_sc[hd, pl.ds(row0, r), :]
            if s_mul != 1.0:
                sc = sc * s_mul
            if has_decay:
                sc = sc + (cq_all[hd][row0:row0 + r] - ck_row)
            if masked:
                sc = jnp.where(_strip_visible(r, t, row0, unit, False), sc, NEG_INF)
            return sc

        def fold(v, op):
            out = v[:, 0:V7X_LANES]
            for ch in range(1, chunks):
                out = op(out, v[:, ch * V7X_LANES:(ch + 1) * V7X_LANES])
            return out

        def tile_max(j, masked):
            off, kt = keys(j)
            for hd in range(2):
                s_sc[hd] = _dot_nt(q_sc[hd], kt)
                ck_row = ck_ref[hd, :, pl.ds(off, t)] if has_decay else None
                for b in range(t // r):
                    rows = pl.ds(b * r, r)
                    sc = strip_scores(hd, b * r, ck_row, masked)
                    mx_sc[hd, rows, :] = jnp.maximum(mx_sc[hd, rows, :], fold(sc, jnp.maximum))

        lax.fori_loop(0, i, lambda j, c: (tile_max(j, False), c)[1], 0)
        tile_max(i, True)
        m_all = [jnp.max(mx_sc[hd], axis=-1, keepdims=True) for hd in range(2)]

        def tile_sum(j, masked):
            off, kt = keys(j)
            vt = v_ref[pl.ds(off, t), :]
            for hd in range(2):
                s_sc[hd] = _dot_nt(q_sc[hd], kt)
                ck_row = ck_ref[hd, :, pl.ds(off, t)] if has_decay else None
                for b in range(t // r):
                    row0 = b * r
                    rows = pl.ds(row0, r)
                    pr = jnp.exp(strip_scores(hd, row0, ck_row, masked) - m_all[hd][row0:row0 + r])
                    ls_sc[hd, rows, :] += fold(pr, jnp.add)
                    p_sc[hd, rows, :] = pr.astype(MXU_DTYPE)
                acc_sc[hd] += _dot(p_sc[hd], vt)

        lax.fori_loop(0, i, lambda j, c: (tile_sum(j, False), c)[1], 0)
        tile_sum(i, True)
        l_all = [jnp.sum(ls_sc[hd], axis=-1, keepdims=True) for hd in range(2)]
        o_ref[...] = jnp.where(in_a, acc_sc[0] / l_all[0], acc_sc[1] / l_all[1]).astype(o_ref.dtype)
        for hd in range(2):
            lse_col = m_all[hd] + jnp.log(l_all[hd])
            lse_ref[hd] = lse_col
            tr_sc[...] = jnp.broadcast_to(lse_col, (t, V7X_LANES)).T
            lser_ref[hd] = tr_sc[0:1, :]

    blk = lambda cb: pl.BlockSpec((t, V7X_LANES), lambda p, i: (i, cb + p))
    whole = lambda cb: pl.BlockSpec((s, V7X_LANES), lambda p, i: (0, cb + p))
    stat = pl.BlockSpec((2, t, 1), lambda p, i: (p, i, 0))
    in_specs = [blk(q_cb), whole(k_cb), whole(v_cb)]
    ins = [q_arr, k_arr, v_arr]
    if has_rope:
        in_specs += [pl.BlockSpec((t, 2 * V7X_LANES), lambda p, i: (i, p)),
                     pl.BlockSpec((s, V7X_LANES), lambda p, i: (0, 0))]
        ins += list(rope)
    if has_decay:
        in_specs += [stat, pl.BlockSpec((2, 1, s), lambda p, i: (p, 0, 0))]
        ins += list(decay)
    col = (2, t, 1)
    vmem = (6 * _nbytes((s, V7X_LANES), MXU_DTYPE) + 6 * _nbytes((t, t), F32) + 10 * _nbytes((t, V7X_LANES), F32)
            + 8 * _nbytes((2, t, V7X_LANES), F32))
    return pl.pallas_call(
        body, name=name, grid=(HEAD_PAIRS, s // t),
        in_specs=in_specs,
        out_specs=[pl.BlockSpec((t, V7X_LANES), lambda p, i: (i, p)), stat,
                   pl.BlockSpec((2, 1, t), lambda p, i: (p, 0, i))],
        out_shape=[jax.ShapeDtypeStruct((s, HEADS * 64), MXU_DTYPE), jax.ShapeDtypeStruct((HEADS, s, 1), F32),
                   jax.ShapeDtypeStruct((HEADS, 1, s), F32)],
        scratch_shapes=[pltpu.VMEM((2, t, kw), MXU_DTYPE), pltpu.VMEM((2, t, t), F32), pltpu.VMEM((2, t, t), MXU_DTYPE),
                        pltpu.VMEM((2, t, V7X_LANES), F32), pltpu.VMEM((2, t, V7X_LANES), F32),
                        pltpu.VMEM((2, t, V7X_LANES), F32), pltpu.VMEM((V7X_LANES, t), F32)],
        compiler_params=_params(("parallel", "arbitrary"), vmem),
    )(*ins)


def _pair_delta(q_arr, q_cb, k_arr, k_cb, v_arr, v_cb, do, lse, rope, decay, *, scale, unit, name):
    s = q_arr.shape[0]
    t = _attn_tile(s)
    r = _strip_rows(t)
    has_rope, has_decay = rope is not None, decay is not None
    kw = 2 * V7X_LANES if has_rope else V7X_LANES
    chunks = t // V7X_LANES
    q_mul, s_mul = _split_scale(scale, has_rope)

    def body(*refs):
        it = iter(refs)
        q_ref, k_ref, v_ref, do_ref, lse_ref = (next(it) for _ in range(5))
        qr_ref, kr_ref = (next(it), next(it)) if has_rope else (None, None)
        cq_ref, ck_ref = (next(it), next(it)) if has_decay else (None, None)
        dl_ref = next(it)
        q_sc, do_sc, s_sc, dp_sc, acc_sc = (next(it) for _ in range(5))
        i = pl.program_id(1)
        in_a = _pair_mask(t)
        qv, dov = q_ref[...] * q_mul, do_ref[...]
        for hd in range(2):
            sel = in_a if hd == 0 else jnp.logical_not(in_a)
            q_sc[hd, :, 0:V7X_LANES] = jnp.where(sel, qv, 0).astype(MXU_DTYPE)
            if has_rope:
                q_sc[hd, :, V7X_LANES:kw] = qr_ref[:, hd * V7X_LANES:(hd + 1) * V7X_LANES].astype(MXU_DTYPE)
            do_sc[hd] = jnp.where(sel, dov, 0).astype(MXU_DTYPE)
        acc_sc[...] = jnp.zeros(acc_sc.shape, F32)
        lse_all = [lse_ref[hd] for hd in range(2)]
        cq_all = [cq_ref[hd] for hd in range(2)] if has_decay else None

        def fold_add(v):
            out = v[:, 0:V7X_LANES]
            for ch in range(1, chunks):
                out = out + v[:, ch * V7X_LANES:(ch + 1) * V7X_LANES]
            return out

        def tile(j, masked):
            off = pl.multiple_of(j * t, t)
            kt = k_ref[pl.ds(off, t), :]
            if has_rope:
                kt = jnp.concatenate([kt, kr_ref[pl.ds(off, t), :]], axis=-1)
            vt = v_ref[pl.ds(off, t), :]
            for hd in range(2):
                s_sc[hd] = _dot_nt(q_sc[hd], kt)
                dp_sc[hd] = _dot_nt(do_sc[hd], vt)
                ck_row = ck_ref[hd, :, pl.ds(off, t)] if has_decay else None
                for b in range(t // r):
                    row0 = b * r
                    rows = pl.ds(row0, r)
                    sc = s_sc[hd, rows, :]
                    if s_mul != 1.0:
                        sc = sc * s_mul
                    if has_decay:
                        sc = sc + (cq_all[hd][row0:row0 + r] - ck_row)
                    if masked:
                        sc = jnp.where(_strip_visible(r, t, row0, unit, False), sc, NEG_INF)
                    pr = jnp.exp(sc - lse_all[hd][row0:row0 + r])
                    acc_sc[hd, rows, :] += fold_add(pr * dp_sc[hd, rows, :])

        lax.fori_loop(0, i, lambda j, c: (tile(j, False), c)[1], 0)
        tile(i, True)
        for hd in range(2):
            dl_ref[hd] = jnp.sum(acc_sc[hd].T, axis=0, keepdims=True)

    blk = lambda cb: pl.BlockSpec((t, V7X_LANES), lambda p, i: (i, cb + p))
    whole = lambda cb: pl.BlockSpec((s, V7X_LANES), lambda p, i: (0, cb + p))
    stat = pl.BlockSpec((2, t, 1), lambda p, i: (p, i, 0))
    in_specs = [blk(q_cb), whole(k_cb), whole(v_cb), blk(0), stat]
    ins = [q_arr, k_arr, v_arr, do, lse]
    if has_rope:
        in_specs += [pl.BlockSpec((t, 2 * V7X_LANES), lambda p, i: (i, p)),
                     pl.BlockSpec((s, V7X_LANES), lambda p, i: (0, 0))]
        ins += list(rope)
    if has_decay:
        in_specs += [stat, pl.BlockSpec((2, 1, s), lambda p, i: (p, 0, 0))]
        ins += list(decay)
    vmem = (6 * _nbytes((s, V7X_LANES), MXU_DTYPE) + 8 * _nbytes((t, t), F32) + 12 * _nbytes((t, kw), F32))
    return pl.pallas_call(
        body, name=name, grid=(HEAD_PAIRS, s // t),
        in_specs=in_specs, out_specs=pl.BlockSpec((2, 1, t), lambda p, i: (p, 0, i)),
        out_shape=jax.ShapeDtypeStruct((HEADS, 1, s), F32),
        scratch_shapes=[pltpu.VMEM((2, t, kw), MXU_DTYPE), pltpu.VMEM((2, t, V7X_LANES), MXU_DTYPE),
                        pltpu.VMEM((2, t, t), F32), pltpu.VMEM((2, t, t), F32),
                        pltpu.VMEM((2, t, V7X_LANES), F32)],
        compiler_params=_params(("parallel", "arbitrary"), vmem),
    )(*ins)


def _pair_bwd_kv(q_arr, q_cb, k_arr, k_cb, v_arr, v_cb, do, lse_row, delta_row, rope, decay, *, scale, unit, name):
    s = q_arr.shape[0]
    t = _attn_tile(s)
    nt = s // t
    r = _strip_rows(t)
    has_rope, has_decay = rope is not None, decay is not None
    kw = 2 * V7X_LANES if has_rope else V7X_LANES
    q_mul, s_mul = _split_scale(scale, has_rope)

    def body(*refs):
        it = iter(refs)
        k_ref, v_ref, q_ref, do_ref, lse_ref, dl_ref = (next(it) for _ in range(6))
        qr_ref, kr_ref = (next(it), next(it)) if has_rope else (None, None)
        ck_ref, cq_ref = (next(it), next(it)) if has_decay else (None, None)
        dk_ref, dv_ref, dq_ref = next(it), next(it), next(it)
        dkr_ref, dqr_ref = (next(it), next(it)) if has_rope else (None, None)
        dc_ref = next(it) if has_decay else None
        k_sc, v_sc, st_sc, dpt_sc, pt_sc, dst_sc, dk_sc, dv_sc, dc_sc, dqt_sc, kt_sc = (next(it) for _ in range(11))
        j = pl.program_id(1)
        in_a = _pair_mask(t)
        kv_, vv_ = k_ref[...], v_ref[...]
        for hd in range(2):
            sel = in_a if hd == 0 else jnp.logical_not(in_a)
            k_sc[hd, :, 0:V7X_LANES] = jnp.where(sel, kv_, 0).astype(MXU_DTYPE)
            if has_rope:
                k_sc[hd, :, V7X_LANES:kw] = kr_ref[...].astype(MXU_DTYPE)
            v_sc[hd] = jnp.where(sel, vv_, 0).astype(MXU_DTYPE)
        dk_sc[...] = jnp.zeros(dk_sc.shape, F32)
        dv_sc[...] = jnp.zeros(dv_sc.shape, F32)
        dc_sc[...] = jnp.zeros(dc_sc.shape, F32)
        k_all = kv_.astype(F32)
        if has_rope:
            k_all = jnp.concatenate([k_all, kr_ref[...].astype(F32)], axis=-1)
        kt_sc[...] = k_all.T.astype(MXU_DTYPE)

        @pl.when(j == 0)
        def _():
            dqt_sc[...] = jnp.zeros(dqt_sc.shape, F32)

        def tile(i, masked):
            off = pl.multiple_of(i * t, t)
            qt = q_ref[pl.ds(off, t), :] * q_mul
            dot = do_ref[pl.ds(off, t), :]
            for hd in range(2):
                qcat = qt
                if has_rope:
                    qcat = jnp.concatenate([qt, qr_ref[pl.ds(off, t), hd * V7X_LANES:(hd + 1) * V7X_LANES]], axis=-1)
                st_sc[hd] = _dot_nt(k_sc[hd], qcat)
                dpt_sc[hd] = _dot_nt(v_sc[hd], dot)
                lse_r = lse_ref[hd, :, pl.ds(off, t)]
                dl_r = dl_ref[hd, :, pl.ds(off, t)]
                cq_r = cq_ref[hd, :, pl.ds(off, t)] if has_decay else None
                ck_all = ck_ref[hd] if has_decay else None
                parts = []
                for b in range(t // r):
                    row0 = b * r
                    rows = pl.ds(row0, r)
                    sc = st_sc[hd, rows, :]
                    if s_mul != 1.0:
                        sc = sc * s_mul
                    if has_decay:
                        sc = sc + (cq_r - ck_all[row0:row0 + r])
                    if masked:
                        sc = jnp.where(_strip_visible(r, t, row0, unit, True), sc, NEG_INF)
                    pr = jnp.exp(sc - lse_r)
                    ds = pr * (dpt_sc[hd, rows, :] - dl_r)
                    pt_sc[hd, rows, :] = pr.astype(MXU_DTYPE)
                    dst_sc[hd, rows, :] = ds.astype(MXU_DTYPE)
                    if has_decay:
                        parts.append(jnp.sum(ds, axis=-1, keepdims=True))
                if has_decay:
                    dc_sc[hd] += jnp.concatenate(parts, axis=0)
                dv_sc[hd] += _dot(pt_sc[hd], dot)
                dk_sc[hd] += _dot(dst_sc[hd], qcat)
                dqt_sc[hd, :, pl.ds(off, t)] += _dot(kt_sc[...], dst_sc[hd])

        tile(j, True)

        def unmasked(i, carry):
            tile(i, False)
            return carry

        lax.fori_loop(j + 1, nt, unmasked, 0)
        dk_ref[...] = (jnp.where(in_a, dk_sc[0, :, 0:V7X_LANES], dk_sc[1, :, 0:V7X_LANES]) * s_mul).astype(dk_ref.dtype)
        dv_ref[...] = jnp.where(in_a, dv_sc[0], dv_sc[1]).astype(dv_ref.dtype)
        if has_rope:
            dkr_ref[0] = (dk_sc[0, :, V7X_LANES:kw] + dk_sc[1, :, V7X_LANES:kw]) * s_mul
        if has_decay:
            dc_ref[...] = -dc_sc[...]
        own = pl.ds(pl.multiple_of(j * t, t), t)
        dq_a = dqt_sc[0, :, own].T * scale
        dq_b = dqt_sc[1, :, own].T * scale
        dq_ref[...] = jnp.where(in_a, dq_a[:, 0:V7X_LANES], dq_b[:, 0:V7X_LANES]).astype(dq_ref.dtype)
        if has_rope:
            dqr_ref[:, 0:V7X_LANES] = dq_a[:, V7X_LANES:kw]
            dqr_ref[:, V7X_LANES:kw] = dq_b[:, V7X_LANES:kw]

    blk = lambda cb: pl.BlockSpec((t, V7X_LANES), lambda p, j: (j, cb + p))
    whole = lambda cb: pl.BlockSpec((s, V7X_LANES), lambda p, j: (0, cb + p))
    stat = pl.BlockSpec((2, t, 1), lambda p, j: (p, j, 0))
    row = pl.BlockSpec((2, 1, s), lambda p, j: (p, 0, 0))
    in_specs = [blk(k_cb), blk(v_cb), whole(q_cb), whole(0), row, row]
    ins = [k_arr, v_arr, q_arr, do, lse_row, delta_row]
    out_specs = [blk(0), blk(0), blk(0)]
    out_shape = [jax.ShapeDtypeStruct((s, HEADS * 64), MXU_DTYPE)] * 3
    if has_rope:
        in_specs += [pl.BlockSpec((s, 2 * V7X_LANES), lambda p, j: (0, p)),
                     pl.BlockSpec((t, V7X_LANES), lambda p, j: (j, 0))]
        ins += list(rope)
        out_specs += [pl.BlockSpec((1, t, V7X_LANES), lambda p, j: (p, j, 0)),
                      pl.BlockSpec((t, 2 * V7X_LANES), lambda p, j: (j, p))]
        out_shape += [jax.ShapeDtypeStruct((HEAD_PAIRS, s, V7X_LANES), F32),
                      jax.ShapeDtypeStruct((s, HEADS * V7X_LANES), F32)]
    if has_decay:
        in_specs += [stat, row]
        ins += list(decay)
        out_specs.append(stat)
        out_shape.append(jax.ShapeDtypeStruct((HEADS, s, 1), F32))
    vmem = (12 * _nbytes((s, V7X_LANES), MXU_DTYPE) + 8 * _nbytes((t, t), F32) + 16 * _nbytes((t, kw), F32)
            + _nbytes((2, kw, s), F32))
    return pl.pallas_call(
        body, name=name, grid=(HEAD_PAIRS, nt),
        in_specs=in_specs, out_specs=out_specs, out_shape=out_shape,
        scratch_shapes=[pltpu.VMEM((2, t, kw), MXU_DTYPE), pltpu.VMEM((2, t, V7X_LANES), MXU_DTYPE),
                        pltpu.VMEM((2, t, t), F32), pltpu.VMEM((2, t, t), F32), pltpu.VMEM((2, t, t), MXU_DTYPE),
                        pltpu.VMEM((2, t, t), MXU_DTYPE), pltpu.VMEM((2, t, kw), F32),
                        pltpu.VMEM((2, t, V7X_LANES), F32), pltpu.VMEM((2, t, 1), F32),
                        pltpu.VMEM((2, kw, s), F32), pltpu.VMEM((kw, t), MXU_DTYPE)],
        compiler_params=_params(("arbitrary", "arbitrary"), vmem),
    )(*ins)


def _fox_cum(z, bf, *, name):
    s = z.shape[0]
    w = V7X_LANES
    t = _row_tile(s, 512)
    steps = [1 << k for k in range(int(math.log2(t)))]
    cb = SEG["fl"][3] // w

    def body(f_ref, bf_ref, c_ref, car):
        @pl.when(pl.program_id(0) == 0)
        def _():
            car[...] = jnp.zeros_like(car)

        acc = -_softplus(-(f_ref[...] + bf_ref[...]))
        rows = lax.broadcasted_iota(jnp.int32, (t, w), 0)
        for d in steps:
            acc = acc + _shift_down(acc, d, 0.0, rows)
        c_ref[...] = acc + car[0:1, :]
        car[0:1, :] = c_ref[t - 1:t, :]

    return pl.pallas_call(
        body, name=name, grid=(s // t,),
        in_specs=[_rows(t, w, cb), _full((1, w))], out_specs=_rows(t, w),
        out_shape=jax.ShapeDtypeStruct((s, w), F32),
        scratch_shapes=[pltpu.VMEM((V7X_SUBLANES, w), F32)],
        compiler_params=_params(("arbitrary",), 16 * _nbytes((t, w), F32)),
    )(z, bf)


def _fox_cum_bwd(z, bf, dcum, *, name):
    s = z.shape[0]
    w = V7X_LANES
    t = _row_tile(s, 512)
    nt = s // t
    steps = [1 << k for k in range(int(math.log2(t)))]
    cb = SEG["fl"][3] // w

    def body(f_ref, bf_ref, dc_ref, df_ref, dbf_ref, car, tmp):
        @pl.when(pl.program_id(0) == 0)
        def _():
            car[...] = jnp.zeros_like(car)
            dbf_ref[...] = jnp.zeros_like(dbf_ref)

        acc = dc_ref[...]
        rows = lax.broadcasted_iota(jnp.int32, (t, w), 0)
        for d in steps:
            acc = acc + _shift_up(acc, d, 0.0, rows, t)
        dlf = acc + car[0:1, :]
        tmp[...] = dlf
        car[0:1, :] = tmp[0:1, :]
        df = dlf * _sigmoid(-(f_ref[...] + bf_ref[...]))
        df_ref[...] = df.astype(df_ref.dtype)
        dbf_ref[...] += jnp.sum(df, axis=0, keepdims=True)

    rev = lambda cbk: pl.BlockSpec((t, w), lambda i: (nt - 1 - i, cbk))
    return pl.pallas_call(
        body, name=name, grid=(nt,),
        in_specs=[rev(cb), _full((1, w)), rev(0)], out_specs=[rev(0), _full((1, w))],
        out_shape=[jax.ShapeDtypeStruct((s, w), MXU_DTYPE), jax.ShapeDtypeStruct((1, w), F32)],
        scratch_shapes=[pltpu.VMEM((V7X_SUBLANES, w), F32), pltpu.VMEM((t, w), F32)],
        compiler_params=_params(("arbitrary",), 16 * _nbytes((t, w), F32)),
    )(z, bf, dcum)


_GATE_CB = SEG["gate"][3] // D_MODEL


def _merge_fwd(ya, yb, yc, z, gate_b, *, name):
    s = ya.shape[0]
    d = D_MODEL
    t = _tile(s, 256)

    def body(ya_ref, yb_ref, yc_ref, g0_ref, g1_ref, g2_ref, gb_ref, o_ref):
        out = _sigmoid(g0_ref[...] + gb_ref[:, 0:d]) * ya_ref[...]
        out = out + _sigmoid(g1_ref[...] + gb_ref[:, d:2 * d]) * yb_ref[...]
        out = out + _sigmoid(g2_ref[...] + gb_ref[:, 2 * d:3 * d]) * yc_ref[...]
        o_ref[...] = out.astype(o_ref.dtype)

    return pl.pallas_call(
        body, name=name, grid=(s // t,),
        in_specs=[_rows(t, d)] * 3 + [_rows(t, d, _GATE_CB + b) for b in range(3)] + [_full((1, 3 * d))],
        out_specs=_rows(t, d), out_shape=jax.ShapeDtypeStruct((s, d), MXU_DTYPE),
        compiler_params=_params(("parallel",), 20 * _nbytes((t, d), F32)),
    )(ya, yb, yc, z, z, z, gate_b)


def _merge_bwd(dm, ya, yb, yc, z, gate_b, *, name):
    s = ya.shape[0]
    d = D_MODEL
    t = _tile(s, 256)

    def body(dm_ref, ya_ref, yb_ref, yc_ref, g0_ref, g1_ref, g2_ref, gb_ref, da_ref, db_ref, dc_ref, dgl_ref,
             dgb_ref):
        dmv = dm_ref[...]
        parts = []
        for b, (y_ref, g_ref, dy_ref) in enumerate(((ya_ref, g0_ref, da_ref), (yb_ref, g1_ref, db_ref),
                                                    (yc_ref, g2_ref, dc_ref))):
            gate = _sigmoid(g_ref[...] + gb_ref[:, b * d:(b + 1) * d])
            dy_ref[...] = (dmv * gate).astype(dy_ref.dtype)
            dgl = dmv * y_ref[...] * gate * (1.0 - gate)
            dgl_ref[:, b * d:(b + 1) * d] = dgl.astype(dgl_ref.dtype)
            parts.append(jnp.sum(dgl, axis=0, keepdims=True))

        @pl.when(pl.program_id(0) == 0)
        def _():
            for b, part in enumerate(parts):
                dgb_ref[:, b * d:(b + 1) * d] = part

        @pl.when(pl.program_id(0) > 0)
        def _():
            for b, part in enumerate(parts):
                dgb_ref[:, b * d:(b + 1) * d] += part

    return pl.pallas_call(
        body, name=name, grid=(s // t,),
        in_specs=[_rows(t, d)] * 4 + [_rows(t, d, _GATE_CB + b) for b in range(3)] + [_full((1, 3 * d))],
        out_specs=[_rows(t, d)] * 3 + [_rows(t, 3 * d), _full((1, 3 * d))],
        out_shape=[jax.ShapeDtypeStruct((s, d), MXU_DTYPE)] * 3
        + [jax.ShapeDtypeStruct((s, 3 * d), MXU_DTYPE), jax.ShapeDtypeStruct((1, 3 * d), F32)],
        compiler_params=_params(("arbitrary",), 36 * _nbytes((t, d), F32)),
    )(dm, ya, yb, yc, z, z, z, gate_b)


def _swiglu_fwd(hf, *, name):
    s = hf.shape[0]
    t = _tile(s, 256)

    def body(g_ref, u_ref, o_ref):
        gv = g_ref[...]
        o_ref[...] = (gv * _sigmoid(gv) * u_ref[...]).astype(o_ref.dtype)

    return pl.pallas_call(
        body, name=name, grid=(s // t,),
        in_specs=[_rows(t, D_FF, 0), _rows(t, D_FF, 1)], out_specs=_rows(t, D_FF),
        out_shape=jax.ShapeDtypeStruct((s, D_FF), MXU_DTYPE),
        compiler_params=_params(("parallel",), 10 * _nbytes((t, D_FF), F32)),
    )(hf, hf)


def _swiglu_bwd(hf, dact, *, name):
    s = hf.shape[0]
    t = _tile(s, 256)

    def body(g_ref, u_ref, da_ref, o_ref):
        gv = g_ref[...]
        dav = da_ref[...]
        sg = _sigmoid(gv)
        o_ref[:, 0:D_FF] = (dav * u_ref[...] * sg * (1.0 + gv * (1.0 - sg))).astype(o_ref.dtype)
        o_ref[:, D_FF:2 * D_FF] = (dav * gv * sg).astype(o_ref.dtype)

    return pl.pallas_call(
        body, name=name, grid=(s // t,),
        in_specs=[_rows(t, D_FF, 0), _rows(t, D_FF, 1), _rows(t, D_FF)], out_specs=_rows(t, 2 * D_FF),
        out_shape=jax.ShapeDtypeStruct((s, 2 * D_FF), MXU_DTYPE),
        compiler_params=_params(("parallel",), 14 * _nbytes((t, D_FF), F32)),
    )(hf, hf, dact)


def _ple_fwd(x, lg, pe, *, name):
    s, d = x.shape
    t = _tile(s, 512)

    def body(x_ref, lg_ref, pe_ref, o_ref):
        o_ref[...] = x_ref[...] + _sigmoid(lg_ref[...]) * pe_ref[...]

    return pl.pallas_call(
        body, name=name, grid=(s // t,),
        in_specs=[_rows(t, d)] * 3, out_specs=_rows(t, d), out_shape=jax.ShapeDtypeStruct((s, d), F32),
        compiler_params=_params(("parallel",), 12 * _nbytes((t, d), F32)),
    )(x, lg, pe)


def _ple_bwd(dx, lg, pe, *, name):
    s, d = dx.shape
    t = _tile(s, 512)

    def body(dx_ref, lg_ref, pe_ref, dpe_ref, dlg_ref):
        dxv = dx_ref[...]
        sg = _sigmoid(lg_ref[...])
        dpe_ref[...] = (dxv * sg).astype(dpe_ref.dtype)
        dlg_ref[...] = (dxv * pe_ref[...] * sg * (1.0 - sg)).astype(dlg_ref.dtype)

    return pl.pallas_call(
        body, name=name, grid=(s // t,),
        in_specs=[_rows(t, d)] * 3, out_specs=[_rows(t, d)] * 2,
        out_shape=[jax.ShapeDtypeStruct((s, d), MXU_DTYPE)] * 2,
        compiler_params=_params(("parallel",), 14 * _nbytes((t, d), F32)),
    )(dx, lg, pe)


def _adamw(parts, w, m, v, *, name):
    rows, lanes = w.shape
    t = math.gcd(rows, 160)
    assert rows % t == 0 and t % V7X_SUBLANES == 0
    c1 = 1.0 / (1.0 - ADAM_B1 ** ADAM_STEP)
    c2 = 1.0 / (1.0 - ADAM_B2 ** ADAM_STEP)

    def body(p_ref, w_ref, m_ref, v_ref, g_ref, d_ref, nm_ref, nv_ref):
        g = p_ref[0].astype(F32)
        for j in range(1, N_DEV):
            g = g + p_ref[j].astype(F32)
        m2 = ADAM_B1 * m_ref[...] + (1.0 - ADAM_B1) * g
        v2 = ADAM_B2 * v_ref[...] + (1.0 - ADAM_B2) * (g * g)
        g_ref[...] = g
        nm_ref[...] = m2
        nv_ref[...] = v2
        d_ref[...] = -ADAM_LR * ((m2 * c1) / (jnp.sqrt(v2 * c2) + ADAM_EPS) + ADAM_WD * w_ref[...])

    blk = _rows(t, lanes)
    return pl.pallas_call(
        body, name=name, grid=(rows // t,),
        in_specs=[pl.BlockSpec((N_DEV, t, lanes), lambda i: (0, i, 0)), blk, blk, blk], out_specs=[blk] * 4,
        out_shape=[jax.ShapeDtypeStruct((rows, lanes), F32)] * 4,
        compiler_params=_params(("parallel",), 40 * _nbytes((t, lanes), F32)),
    )(parts, w, m, v)


def _mesh_pos():
    return lax.axis_index("x"), lax.axis_index("y"), lax.axis_index("c")


def _all_gather(blk, *, name):
    r, c_dim = blk.shape

    def body(x_ref, out_ref, send_sems, recv_sems, local_sem):
        x, y, c = _mesh_pos()
        me, sibling = (x, y, c), (x, y, 1 - c)
        chips = [(1 - x, y), (x, 1 - y), (1 - x, 1 - y)]

        def slot(px, py, pc):
            return out_ref.at[4 * px + 2 * py + pc]

        def copy(k, block, to, src=None):
            return pltpu.make_async_remote_copy(
                src_ref=slot(*block) if src is None else src, dst_ref=slot(*block),
                send_sem=send_sems.at[k], recv_sem=recv_sems.at[k],
                device_id=to, device_id_type=pl.DeviceIdType.MESH)

        mine = pltpu.make_async_copy(x_ref, slot(*me), local_sem)
        mine.start()
        first = [copy(0, me, sibling, src=x_ref)]
        first += [copy(1 + j, me, (*chip, c), src=x_ref) for j, chip in enumerate(chips)]
        for cp in first:
            cp.start()
        passed = [copy(4 + j, (*chip, c), sibling) for j, chip in enumerate(chips)]
        for j, chip in enumerate(chips):
            copy(1 + j, (*chip, c), me).wait_recv()
            passed[j].start()
        copy(0, sibling, me).wait_recv()
        for j, chip in enumerate(chips):
            copy(4 + j, (*chip, 1 - c), me).wait_recv()
        for cp in first + passed:
            cp.wait_send()
        mine.wait()

    return pl.pallas_call(
        body, name=name,
        out_shape=jax.ShapeDtypeStruct((N_DEV, r, c_dim), blk.dtype),
        in_specs=[pl.BlockSpec(memory_space=pl.ANY)], out_specs=pl.BlockSpec(memory_space=pl.ANY),
        scratch_shapes=[pltpu.SemaphoreType.DMA((7,)), pltpu.SemaphoreType.DMA((7,)), pltpu.SemaphoreType.DMA],
    )(blk)


def _all_gather_many(blocks, *, name):
    n = len(blocks)

    def body(*refs):
        x_refs, out_refs = refs[:n], refs[n:2 * n]
        send_sems, recv_sems, local_sems = refs[2 * n:]
        x, y, c = _mesh_pos()
        me, sibling = (x, y, c), (x, y, 1 - c)
        chips = [(1 - x, y), (x, 1 - y), (1 - x, 1 - y)]

        def slot(a, px, py, pc):
            return out_refs[a].at[4 * px + 2 * py + pc]

        def copy(k, a, block, to, src=None):
            return pltpu.make_async_remote_copy(
                src_ref=slot(a, *block) if src is None else src, dst_ref=slot(a, *block),
                send_sem=send_sems.at[k, a], recv_sem=recv_sems.at[k, a],
                device_id=to, device_id_type=pl.DeviceIdType.MESH)

        mine = [pltpu.make_async_copy(x_refs[a], slot(a, *me), local_sems.at[a]) for a in range(n)]
        for cp in mine:
            cp.start()
        first = [copy(0, a, me, sibling, src=x_refs[a]) for a in range(n)]
        first += [copy(1 + j, a, me, (*chip, c), src=x_refs[a]) for j, chip in enumerate(chips) for a in range(n)]
        for cp in first:
            cp.start()
        passed = []
        for j, chip in enumerate(chips):
            for a in range(n):
                copy(1 + j, a, (*chip, c), me).wait_recv()
                fwd = copy(4 + j, a, (*chip, c), sibling)
                fwd.start()
                passed.append(fwd)
        for a in range(n):
            copy(0, a, sibling, me).wait_recv()
        for j, chip in enumerate(chips):
            for a in range(n):
                copy(4 + j, a, (*chip, 1 - c), me).wait_recv()
        for cp in first + passed:
            cp.wait_send()
        for cp in mine:
            cp.wait()

    any_spec = pl.BlockSpec(memory_space=pl.ANY)
    return pl.pallas_call(
        body, name=name,
        out_shape=[jax.ShapeDtypeStruct((N_DEV,) + b.shape, b.dtype) for b in blocks],
        in_specs=[any_spec] * n, out_specs=[any_spec] * n,
        scratch_shapes=[pltpu.SemaphoreType.DMA((7, n)), pltpu.SemaphoreType.DMA((7, n)),
                        pltpu.SemaphoreType.DMA((n,))],
    )(*blocks)


CHIPS = 4


def _sibling_exchange_many(pays, *, name):
    n = len(pays)

    def body(*refs):
        in_refs, theirs_refs = refs[:n], refs[n:2 * n]
        send_sems, recv_sems = refs[2 * n:]
        x, y, c = _mesh_pos()
        copies = [pltpu.make_async_remote_copy(
            src_ref=in_refs[a].at[pl.ds(CHIPS * (1 - c), CHIPS)], dst_ref=theirs_refs[a],
            send_sem=send_sems.at[a], recv_sem=recv_sems.at[a],
            device_id=(x, y, 1 - c), device_id_type=pl.DeviceIdType.MESH) for a in range(n)]
        for cp in copies:
            cp.start()
        for cp in copies:
            cp.wait()

    any_spec = pl.BlockSpec(memory_space=pl.ANY)
    return pl.pallas_call(
        body, name=name, out_shape=[jax.ShapeDtypeStruct((CHIPS,) + p.shape[1:], p.dtype) for p in pays],
        in_specs=[any_spec] * n, out_specs=[any_spec] * n,
        scratch_shapes=[pltpu.SemaphoreType.DMA((n,)), pltpu.SemaphoreType.DMA((n,))],
    )(*pays)


def _chip_exchange_many(sums, *, name):
    n = len(sums)

    def body(*refs):
        in_refs, out_refs = refs[:n], refs[n:2 * n]
        send_sems, recv_sems, local_sems = refs[2 * n:]
        x, y, c = _mesh_pos()
        me = 2 * x + y
        copies = [pltpu.make_async_copy(in_refs[a].at[me], out_refs[a].at[me], local_sems.at[a]) for a in range(n)]
        for k in range(1, CHIPS):
            px = 1 - x if k & 2 else x
            py = 1 - y if k & 1 else y
            for a in range(n):
                copies.append(pltpu.make_async_remote_copy(
                    src_ref=in_refs[a].at[2 * px + py], dst_ref=out_refs[a].at[me],
                    send_sem=send_sems.at[k - 1, a], recv_sem=recv_sems.at[k - 1, a],
                    device_id=(px, py, c), device_id_type=pl.DeviceIdType.MESH))
        for cp in copies:
            cp.start()
        for cp in copies:
            cp.wait()

    any_spec = pl.BlockSpec(memory_space=pl.ANY)
    return pl.pallas_call(
        body, name=name, out_shape=[jax.ShapeDtypeStruct(p.shape, p.dtype) for p in sums],
        in_specs=[any_spec] * n, out_specs=[any_spec] * n,
        scratch_shapes=[pltpu.SemaphoreType.DMA((CHIPS - 1, n)), pltpu.SemaphoreType.DMA((CHIPS - 1, n)),
                        pltpu.SemaphoreType.DMA((n,))],
    )(*sums)


def _shard_row_tile(rows):
    t = rows
    for cand in range(V7X_SUBLANES, min(rows, 256) + 1, V7X_SUBLANES):
        if rows % cand == 0:
            t = cand
    return t


def _pair_sum(a, b, *, name):
    n4, d0, rows, cols = a.shape
    t = _shard_row_tile(rows)

    def body(a_ref, b_ref, o_ref):
        o_ref[...] = (a_ref[...].astype(F32) + b_ref[...].astype(F32)).astype(o_ref.dtype)

    blk = pl.BlockSpec((1, 1, t, cols), lambda q, l, i: (q, l, i, 0))
    lanes = -(-cols // V7X_LANES) * V7X_LANES
    return pl.pallas_call(
        body, name=name, grid=(n4, d0, rows // t), in_specs=[blk, blk], out_specs=blk,
        out_shape=jax.ShapeDtypeStruct(a.shape, MXU_DTYPE),
        compiler_params=_params(("parallel", "parallel", "parallel"), 16 * _nbytes((max(t, 16), lanes), F32)),
    )(a, b)


def _adamw_nd(parts, w, m, v, *, name):
    n_parts = parts.shape[0]
    d0, rows, cols = w.shape
    t = _shard_row_tile(rows)
    c1 = 1.0 / (1.0 - ADAM_B1 ** ADAM_STEP)
    c2 = 1.0 / (1.0 - ADAM_B2 ** ADAM_STEP)

    def body(p_ref, w_ref, m_ref, v_ref, g_ref, d_ref, nm_ref, nv_ref):
        g = p_ref[0, 0].astype(F32)
        for j in range(1, n_parts):
            g = g + p_ref[j, 0].astype(F32)
        m2 = ADAM_B1 * m_ref[0] + (1.0 - ADAM_B1) * g
        v2 = ADAM_B2 * v_ref[0] + (1.0 - ADAM_B2) * (g * g)
        g_ref[0] = g
        nm_ref[0] = m2
        nv_ref[0] = v2
        d_ref[0] = -ADAM_LR * ((m2 * c1) / (jnp.sqrt(v2 * c2) + ADAM_EPS) + ADAM_WD * w_ref[0])

    blk = pl.BlockSpec((1, t, cols), lambda l, i: (l, i, 0))
    lanes = -(-cols // V7X_LANES) * V7X_LANES
    return pl.pallas_call(
        body, name=name, grid=(d0, rows // t),
        in_specs=[pl.BlockSpec((n_parts, 1, t, cols), lambda l, i: (0, l, i, 0)), blk, blk, blk], out_specs=[blk] * 4,
        out_shape=[jax.ShapeDtypeStruct(w.shape, F32)] * 4,
        compiler_params=_params(("parallel", "parallel"), 40 * _nbytes((max(t, 16), lanes), F32)),
    )(parts, w, m, v)


def _flat_rows(parts, row_multiple):
    flat = jnp.concatenate([p.reshape(-1) for p in parts])
    chunk = PAYLOAD_LANES * row_multiple
    total = -(-flat.shape[0] // chunk) * chunk
    return jnp.pad(flat, (0, total - flat.shape[0])).reshape(total // PAYLOAD_LANES, PAYLOAD_LANES)


def _split_flat(flat, shapes):
    out, off = [], 0
    flat = flat.reshape(-1)
    for shp in shapes:
        n = math.prod(shp)
        out.append(flat[off:off + n].reshape(shp))
        off += n
    return out


def _pad_w_in(w):
    pieces, cursor = [], 0
    for _, off, width, pad_off, _ in SEGS:
        if pad_off > cursor:
            pieces.append(jnp.zeros(w.shape[:-1] + (pad_off - cursor,), w.dtype))
        pieces.append(w[..., off:off + width])
        cursor = pad_off + width
    pieces.append(jnp.zeros(w.shape[:-1] + (D_IN_PAD - cursor,), w.dtype))
    return jnp.concatenate(pieces, axis=-1)


def _unpad_w_in(w):
    return jnp.concatenate([w[..., pad_off:pad_off + width] for _, _, width, pad_off, _ in SEGS], axis=-1)


def _block_diag(w):
    eye = jnp.eye(LRU_HEADS, dtype=w.dtype)
    return (eye[:, None, :, None] * w[:, :, None, :]).reshape(LRU_WIDTH, LRU_WIDTH)


def _diag_blocks(w):
    w4 = w.reshape(LRU_HEADS, LRU_HEAD_DIM, LRU_HEADS, LRU_HEAD_DIM)
    return jnp.stack([w4[h, :, h, :] for h in range(LRU_HEADS)])


def _lane_pad(a, width):
    return jnp.pad(a, ((0, 0), (0, width - a.shape[-1])))


def _layer_fwd(x, p_i, wts, tabs, tag):
    n = functools.partial(lambda base, t=tag: f"{base}_{t}")
    sv = {"x": x}
    n1 = _rms_fwd(x, wts["mix_norm"], width=D_MODEL, name=n("mix_norm_fwd"))
    z, z16 = _mm(n1, wts["w_in"], also_mxu=True, name=n("w_in_fwd"))
    sv.update(n1=n1, z=z, z16=z16)
    lanes = V7X_LANES

    ya_pre, hseq = _lru_fwd(z, wts["conv_w"], wts["conv_b"], wts["lru_wa"], wts["lru_ba"], wts["lru_wx"],
                            wts["lru_bx"], wts["lru_lambda"], name=n("lru_fwd"))
    ya = _mm(ya_pre, wts["w_br_a"], name=n("br_a_fwd"))
    sv.update(ya_pre=ya_pre, hseq=hseq, ya=ya)

    cqn = _rms_fwd(z, wts["mla_q_norm"], width=MLA_Q_LORA, cb=SEG["cq"][3] // MLA_Q_LORA, name=n("q_norm_fwd"))
    ckvn = _rms_fwd(z, wts["mla_kv_norm"], width=MLA_KV_LORA, cb=SEG["ckv"][3] // MLA_KV_LORA,
                    name=n("kv_norm_fwd"))
    qp, qp16 = _mm(cqn, wts["mla_wuq"], also_mxu=True, name=n("wuq_fwd"))
    kv = _mm(ckvn, wts["mla_wukv"], out_dtype=MXU_DTYPE, name=n("wukv_fwd"))
    q_rot = _rope_lanes(qp, tabs["cos128"], tabs["sin128"], cb=0, groups=HEADS, out_dtype=MXU_DTYPE,
                        name=n("q_rope_fwd"))
    k_rot = _rope_lanes(z, tabs["cos128"], tabs["sin128"], cb=SEG["kr"][3] // lanes, groups=1, out_dtype=MXU_DTYPE,
                        name=n("k_rope_fwd"))
    mla_ops = (qp16, HEADS, kv, 0, kv, HEAD_PAIRS)
    ob_flat, lse_b, lse_b_row = _pair_fwd(*mla_ops, (q_rot, k_rot), None, scale=(MLA_NOPE + MLA_ROPE) ** -0.5,
                                          unit=CHUNK, name=n("mla_attn_fwd"))
    yb = _mm(ob_flat, wts["w_br_b"], name=n("br_b_fwd"))
    sv.update(cqn=cqn, ckvn=ckvn, mla_ops=mla_ops, mla_rot=(q_rot, k_rot), lse_b=lse_b, lse_b_row=lse_b_row,
              ob_flat=ob_flat, yb=yb)

    cum = _fox_cum(z, wts["fox_bf"], name=n("fox_cum_fwd"))
    cum_h = cum[:, :HEADS].T
    fox_decay = (cum_h[:, :, None], cum_h[:, None, :])
    fox_ops = (z16, SEG["fq"][3] // lanes, z16, SEG["fk"][3] // lanes, z16, SEG["fv"][3] // lanes)
    oc_flat, lse_c, lse_c_row = _pair_fwd(*fox_ops, None, fox_decay, scale=FOX_HEAD_DIM ** -0.5, unit=1,
                                          name=n("fox_attn_fwd"))
    yc = _mm(oc_flat, wts["w_br_c"], name=n("br_c_fwd"))
    sv.update(fox_ops=fox_ops, fox_decay=fox_decay, lse_c=lse_c, lse_c_row=lse_c_row, oc_flat=oc_flat, yc=yc)

    merged = _merge_fwd(ya, yb, yc, z, wts["gate_b"], name=n("merge_fwd"))
    x1 = _mm(merged, wts["w_o"], res=x, name=n("w_o_fwd"))
    n2 = _rms_fwd(x1, wts["ffn_norm"], width=D_MODEL, name=n("ffn_norm_fwd"))
    hf = _mm(n2, wts["w_gate_up"], name=n("gate_up_fwd"))
    act = _swiglu_fwd(hf, name=n("swiglu_fwd"))
    x2 = _mm(act, wts["w_down"], res=x1, name=n("down_fwd"))
    n3 = _rms_fwd(x2, wts["ple_norm"], width=D_MODEL, name=n("ple_norm_fwd"))
    lg = _mm(n3, wts["w_ple_gate"], name=n("ple_gate_fwd"))
    pe = _mm(p_i, wts["w_ple"], name=n("ple_fwd_mm"))
    x3 = _ple_fwd(x2, lg, pe, name=n("ple_fwd"))
    sv.update(merged=merged, x1=x1, n2=n2, hf=hf, act=act, x2=x2, n3=n3, lg=lg, pe=pe, p_i=p_i)
    return x3, sv


def _layer_bwd(dx3, sv, wts, tabs, tag):
    n = functools.partial(lambda base, t=tag: f"{base}_{t}")
    gr = {}
    z = sv["z"]
    s = z.shape[0]

    dpe, dlg = _ple_bwd(dx3, sv["lg"], sv["pe"], name=n("ple_bwd"))
    gr["w_ple"] = _mm(sv["p_i"], dpe, ta=True, name=n("ple_dw"))
    gr["w_ple_gate"] = _mm(sv["n3"], dlg, ta=True, name=n("ple_gate_dw"))
    dn3 = _mm(dlg, wts["w_ple_gate"], tb=True, name=n("ple_gate_dx"))
    dx2, gr["ple_norm"] = _rms_bwd(sv["x2"], wts["ple_norm"], dn3, width=D_MODEL, res=dx3, name=n("ple_norm_bwd"))

    dact = _mm(dx2, wts["w_down"], tb=True, name=n("down_dx"))
    gr["w_down"] = _mm(sv["act"], dx2, ta=True, name=n("down_dw"))
    dhf = _swiglu_bwd(sv["hf"], dact, name=n("swiglu_bwd"))
    gr["w_gate_up"] = _mm(sv["n2"], dhf, ta=True, name=n("gate_up_dw"))
    dn2 = _mm(dhf, wts["w_gate_up"], tb=True, name=n("gate_up_dx"))
    dx1, gr["ffn_norm"] = _rms_bwd(sv["x1"], wts["ffn_norm"], dn2, width=D_MODEL, res=dx2, name=n("ffn_norm_bwd"))

    dmerged = _mm(dx1, wts["w_o"], tb=True, name=n("w_o_dx"))
    gr["w_o"] = _mm(sv["merged"], dx1, ta=True, name=n("w_o_dw"))
    dya, dyb, dyc, dgl, gr["gate_b"] = _merge_bwd(dmerged, sv["ya"], sv["yb"], sv["yc"], z, wts["gate_b"],
                                                  name=n("merge_bwd"))
    gr["w_br_a"] = _mm(sv["ya_pre"], dya, ta=True, name=n("br_a_dw"))
    gr["w_br_b"] = _mm(sv["ob_flat"], dyb, ta=True, name=n("br_b_dw"))
    gr["w_br_c"] = _mm(sv["oc_flat"], dyc, ta=True, name=n("br_c_dw"))
    dya_pre = _mm(dya, wts["w_br_a"], tb=True, name=n("br_a_dx"))
    dob = _mm(dyb, wts["w_br_b"], tb=True, out_dtype=MXU_DTYPE, name=n("br_b_dx"))
    doc = _mm(dyc, wts["w_br_c"], tb=True, out_dtype=MXU_DTYPE, name=n("br_c_dx"))

    (dz_a, gr["conv_w"], gr["conv_b"], dwa, gr["lru_ba"], dwx, gr["lru_bx"], gr["lru_lambda"]) = _lru_bwd(
        z, sv["hseq"], dya_pre, wts["conv_w"], wts["conv_b"], wts["lru_wa"], wts["lru_ba"], wts["lru_wx"],
        wts["lru_bx"], wts["lru_lambda"], name=n("lru_bwd"))
    gr["lru_wa"], gr["lru_wx"] = _diag_blocks(dwa), _diag_blocks(dwx)

    scale_b = (MLA_NOPE + MLA_ROPE) ** -0.5
    delta_b = _pair_delta(*sv["mla_ops"], dob, sv["lse_b"], sv["mla_rot"], None,
                          scale=scale_b, unit=CHUNK, name=n("mla_attn_delta"))
    dk_nope, dv_mla, dq_nope, dk_rot, dq_rot = _pair_bwd_kv(
        *sv["mla_ops"], dob, sv["lse_b_row"], delta_b, sv["mla_rot"], None,
        scale=scale_b, unit=CHUNK, name=n("mla_attn_bwd"))
    dq_rope = _rope_lanes(dq_rot, tabs["cos128"], -tabs["sin128"], cb=0, groups=HEADS, out_dtype=MXU_DTYPE,
                          name=n("q_rope_bwd"))
    dk_rope = _rope_lanes(dk_rot, tabs["cos128"], -tabs["sin128"], cb=0, groups=1, out_dtype=MXU_DTYPE,
                          sum_parts=HEAD_PAIRS, name=n("k_rope_bwd"))
    dqp = jnp.concatenate([dq_rope, dq_nope], axis=-1)
    dkv = jnp.concatenate([dk_nope, dv_mla], axis=-1)
    gr["mla_wuq"] = _mm(sv["cqn"], dqp, ta=True, name=n("wuq_dw"))
    gr["mla_wukv"] = _mm(sv["ckvn"], dkv, ta=True, name=n("wukv_dw"))
    dcqn = _mm(dqp, wts["mla_wuq"], tb=True, name=n("wuq_dx"))
    dckvn = _mm(dkv, wts["mla_wukv"], tb=True, name=n("wukv_dx"))
    dcq, gr["mla_q_norm"] = _rms_bwd(z, wts["mla_q_norm"], dcqn, width=MLA_Q_LORA, cb=SEG["cq"][3] // MLA_Q_LORA,
                                     out_dtype=MXU_DTYPE, name=n("q_norm_bwd"))
    dckv, gr["mla_kv_norm"] = _rms_bwd(z, wts["mla_kv_norm"], dckvn, width=MLA_KV_LORA,
                                       cb=SEG["ckv"][3] // MLA_KV_LORA, out_dtype=MXU_DTYPE, name=n("kv_norm_bwd"))

    scale_c = FOX_HEAD_DIM ** -0.5
    delta_c = _pair_delta(*sv["fox_ops"], doc, sv["lse_c"], None, sv["fox_decay"],
                          scale=scale_c, unit=1, name=n("fox_attn_delta"))
    dfk, dfv, dfq, dcum = _pair_bwd_kv(*sv["fox_ops"], doc, sv["lse_c_row"], delta_c, None, sv["fox_decay"],
                                       scale=scale_c, unit=1, name=n("fox_attn_bwd"))
    dcum_rows = _lane_pad(dcum.reshape(HEADS, s).T, V7X_LANES)
    dfl, dbf = _fox_cum_bwd(z, wts["fox_bf"], dcum_rows, name=n("fox_cum_bwd"))
    gr["fox_bf"] = dbf[:, :HEADS]

    zero = lambda width: jnp.zeros((s, width), MXU_DTYPE)
    dz = jnp.concatenate([dz_a, zero(128), dcq, dckv, dk_rope, zero(128), dfq, dfk, dfv, dfl, zero(384), dgl],
                         axis=-1)
    gr["w_in"] = _mm(sv["n1"], dz, ta=True, name=n("w_in_dw"))
    dn1 = _mm(dz, wts["w_in"], tb=True, name=n("w_in_dx"))
    dx, gr["mix_norm"] = _rms_bwd(sv["x"], wts["mix_norm"], dn1, width=D_MODEL, res=dx1, name=n("mix_norm_bwd"))
    return dx, gr


def _rope_tables(s):
    pos = jnp.arange(s, dtype=F32)
    inv_freq = ROPE_BASE ** (-jnp.arange(0, MLA_ROPE, 2, dtype=F32) / MLA_ROPE)
    ang = pos[:, None] * inv_freq[None, :]
    cos, sin = jnp.cos(ang), jnp.sin(ang)
    cos32 = jnp.concatenate([cos, cos], axis=-1)
    sin32 = jnp.concatenate([-sin, sin], axis=-1)
    return {"cos128": jnp.tile(cos32, (1, 4)), "sin128": jnp.tile(sin32, (1, 4))}


def _gather_weights(shards):
    names = [nm for nm, _ in SHARDED]
    got = _all_gather_many([shards[nm] if nm == "conv_w" else shards[nm].astype(MXU_DTYPE) for nm in names],
                           name="weights_all_gather")
    full = {}
    for (nm, axis), blk in zip(SHARDED, got):
        shp = shards[nm].shape
        if axis == 2:
            full[nm] = blk.transpose(1, 2, 0, 3).reshape(shp[0], shp[1], N_DEV * shp[2])
        else:
            full[nm] = blk.transpose(1, 0, 2, 3).reshape(shp[0], N_DEV * shp[1], shp[2])
    return full


def _to_dest_major(g, axis):
    d0, r, c = g.shape
    if axis == 2:
        by_dev = g.reshape(d0, r, N_DEV, c // N_DEV).transpose(2, 0, 1, 3)
    else:
        by_dev = g.reshape(d0, N_DEV, r // N_DEV, c).transpose(1, 0, 2, 3)
    shp = by_dev.shape[1:]
    return by_dev.reshape((CHIPS, 2) + shp).transpose(1, 0, 2, 3, 4).reshape((N_DEV,) + shp)


def kernel(x, p, mix_norm, w_in, gate_b, conv_w, conv_b, lru_wa, lru_ba, lru_wx, lru_bx, lru_lambda, mla_q_norm, mla_wuq, mla_kv_norm, mla_wukv, fox_bf, w_br_a, w_br_b, w_br_c, w_o, ffn_norm, w_gate_up, w_down, ple_norm, w_ple_gate, w_ple, final_norm, loss_target, m_mix_norm, m_w_in, m_gate_b, m_conv_w, m_conv_b, m_lru_wa, m_lru_ba, m_lru_wx, m_lru_bx, m_lru_lambda, m_mla_q_norm, m_mla_wuq, m_mla_kv_norm, m_mla_wukv, m_fox_bf, m_w_br_a, m_w_br_b, m_w_br_c, m_w_o, m_ffn_norm, m_w_gate_up, m_w_down, m_ple_norm, m_w_ple_gate, m_w_ple, m_final_norm, v_mix_norm, v_w_in, v_gate_b, v_conv_w, v_conv_b, v_lru_wa, v_lru_ba, v_lru_wx, v_lru_bx, v_lru_lambda, v_mla_q_norm, v_mla_wuq, v_mla_kv_norm, v_mla_wukv, v_fox_bf, v_w_br_a, v_w_br_b, v_w_br_c, v_w_o, v_ffn_norm, v_w_gate_up, v_w_down, v_ple_norm, v_w_ple_gate, v_w_ple, v_final_norm):
    given = dict(locals())
    w_loc = {nm: given[nm] for nm in WEIGHTS}
    m_loc = {nm: given["m_" + nm] for nm in WEIGHTS}
    v_loc = {nm: given["v_" + nm] for nm in WEIGHTS}
    xs = x[0]
    s = xs.shape[0]
    tabs = _rope_tables(s)

    full = _gather_weights({nm: w_loc[nm] for nm, _ in SHARDED})
    full["w_in"] = _pad_w_in(full["w_in"])
    wq = full["mla_wuq"].reshape(DEPTH, MLA_Q_LORA, HEADS, MLA_NOPE + MLA_ROPE)
    wq_rot = jnp.pad(wq[..., MLA_NOPE:], ((0, 0), (0, 0), (0, 0), (0, V7X_LANES - MLA_ROPE)))
    full["mla_wuq"] = jnp.concatenate([wq_rot.reshape(DEPTH, MLA_Q_LORA, -1),
                                       wq[..., :MLA_NOPE].reshape(DEPTH, MLA_Q_LORA, -1)], axis=-1)
    wkv = full["mla_wukv"].reshape(DEPTH, MLA_KV_LORA, HEADS, MLA_NOPE + MLA_V)
    full["mla_wukv"] = jnp.concatenate([wkv[..., :MLA_NOPE].reshape(DEPTH, MLA_KV_LORA, -1),
                                        wkv[..., MLA_NOPE:].reshape(DEPTH, MLA_KV_LORA, -1)], axis=-1)

    def layer_weights(i):
        wts = {nm: full[nm][i] for nm, _ in SHARDED}
        for nm in ("mix_norm", "gate_b", "conv_b", "lru_ba", "lru_bx", "lru_lambda", "mla_q_norm", "mla_kv_norm",
                   "ffn_norm", "ple_norm"):
            wts[nm] = w_loc[nm][i][None, :]
        wts["fox_bf"] = _lane_pad(w_loc["fox_bf"][i][None, :], V7X_LANES)
        wts["lru_wa"] = _block_diag(w_loc["lru_wa"][i]).astype(MXU_DTYPE)
        wts["lru_wx"] = _block_diag(w_loc["lru_wx"][i]).astype(MXU_DTYPE)
        return wts

    layers = [layer_weights(i) for i in range(DEPTH)]

    h = xs
    saved = []
    for i in range(DEPTH):
        h, sv = _layer_fwd(h, p[i, 0].astype(MXU_DTYPE), layers[i], tabs, f"l{i}")
        saved.append(sv)
    loss_blk, dh, dg_final = _final_loss(h, w_loc["final_norm"][None, :], loss_target[0], name="final_loss")
    loss = lax.psum(loss_blk[0, 0], ("x", "y", "c"))

    grads = [None] * DEPTH
    for i in reversed(range(DEPTH)):
        dh, grads[i] = _layer_bwd(dh, saved[i], layers[i], tabs, f"l{i}")
    grad_x = dh[None]

    def stacked(nm):
        return jnp.stack([grads[i][nm] for i in range(DEPTH)])

    gfull = {}
    for nm, _ in SHARDED:
        gfull[nm] = stacked(nm)
    gfull["w_in"] = _unpad_w_in(gfull["w_in"])
    gq = gfull["mla_wuq"]
    rot_w = HEADS * V7X_LANES
    gfull["mla_wuq"] = jnp.concatenate(
        [gq[..., rot_w:].reshape(DEPTH, MLA_Q_LORA, HEADS, MLA_NOPE),
         gq[..., :rot_w].reshape(DEPTH, MLA_Q_LORA, HEADS, V7X_LANES)[..., :MLA_ROPE]],
        axis=-1).reshape(DEPTH, MLA_Q_LORA, -1)
    gkv = gfull["mla_wukv"]
    gfull["mla_wukv"] = jnp.concatenate(
        [gkv[..., :512].reshape(DEPTH, MLA_KV_LORA, HEADS, MLA_NOPE),
         gkv[..., 512:].reshape(DEPTH, MLA_KV_LORA, HEADS, MLA_V)], axis=-1).reshape(DEPTH, MLA_KV_LORA, -1)

    pays = [_to_dest_major(gfull[nm], ax).astype(MXU_DTYPE) for nm, ax in SHARDED]
    theirs = _sibling_exchange_many(pays, name="grads_sibling_exchange")
    own_first = CHIPS * lax.axis_index("c")
    mine = [lax.dynamic_slice_in_dim(pay, own_first, CHIPS, axis=0) for pay in pays]
    pair_sums = [_pair_sum(a, b, name=f"grads_pair_sum_{nm}") for (nm, _), a, b in zip(SHARDED, mine, theirs)]
    parts = _chip_exchange_many(pair_sums, name="grads_chip_exchange")
    res_s = [{}, {}, {}, {}]
    for (nm, _), part in zip(SHARDED, parts):
        outs = _adamw_nd(part, w_loc[nm], m_loc[nm], v_loc[nm], name=f"adamw_{nm}")
        for kind in range(4):
            res_s[kind][nm] = outs[kind]

    small = {nm: stacked(nm) for nm in REPLICATED if nm != "final_norm"}
    small["final_norm"] = dg_final
    names_r = list(REPLICATED)
    shapes_r = [w_loc[nm].shape for nm in names_r]
    parts_r = _all_gather(_flat_rows([small[nm] for nm in names_r], 8), name="small_grads_all_gather")
    outs_r = _adamw(parts_r, _flat_rows([w_loc[nm] for nm in names_r], 8),
                    _flat_rows([m_loc[nm] for nm in names_r], 8),
                    _flat_rows([v_loc[nm] for nm in names_r], 8), name="adamw_replicated")
    res_r = [dict(zip(names_r, _split_flat(o, shapes_r))) for o in outs_r]

    out = [loss, grad_x]
    for kind in range(4):
        for nm in WEIGHTS:
            out.append(res_s[kind][nm] if nm in res_s[kind] else res_r[kind][nm])
    return tuple(out)
```
